```python
import jax, jax.numpy as jnp
from jax import lax
import numpy as np


D_MODEL = 1024
BATCH = 32
SEQ = 2048
DEPTH = 2

CHUNK = 64
N_META = 16
EPS = 1e-6
N_EVEN = (DEPTH + 1) // 2
N_ODD = DEPTH // 2
CONV_W = 4

LRU_WIDTH = D_MODEL
LRU_BLOCKS = 4
LRU_BLOCK = LRU_WIDTH // LRU_BLOCKS
RG_LRU_C = 8.0

SSD_WIDTH = D_MODEL
SSD_HEAD_DIM = 64
SSD_HEADS = SSD_WIDTH // SSD_HEAD_DIM
SSD_GROUPS = 2
SSD_HPG = SSD_HEADS // SSD_GROUPS
SSD_STATE = 128
SSD_CHUNK = CHUNK
SSD_CONV_DIM = SSD_WIDTH + 2 * SSD_GROUPS * SSD_STATE

EVEN_SPLITS = (LRU_WIDTH, 2 * LRU_WIDTH, 2 * LRU_WIDTH + SSD_WIDTH,
               2 * LRU_WIDTH + SSD_WIDTH + SSD_CONV_DIM)
EVEN_IN = 2 * LRU_WIDTH + SSD_WIDTH + SSD_CONV_DIM + SSD_HEADS
EVEN_MIX = LRU_WIDTH + SSD_WIDTH

SB_HEADS = 16
SB_HEAD_DIM = D_MODEL // SB_HEADS
SB_WIDTH = SB_HEADS * SB_HEAD_DIM
SB_BLOCK = 128
ODD_IN = 4 * SB_WIDTH

kernel_name = 'hybrid_rglru_ssd_stickbreaking_meta'


def rmsnorm(x, w):
    xf = x.astype(jnp.float32)
    y = xf * lax.rsqrt(jnp.mean(xf * xf, axis=-1, keepdims=True) + EPS)
    return (y * w.astype(jnp.float32)).astype(x.dtype)


def causal_dwconv(u, w, b):
    out = lax.conv_general_dilated(
        u, w[:, None, :].astype(u.dtype), window_strides=(1,),
        padding=[(CONV_W - 1, 0)], dimension_numbers=('NWC', 'WIO', 'NWC'),
        feature_group_count=u.shape[-1])
    return out + b


def linear_scan(a, b):
    def combine(left, right):
        al, bl = left
        ar, br = right
        return al * ar, ar * bl + br
    _, h = lax.associative_scan(combine, (a, b), axis=1)
    return h


def rg_lru(u, w_a, b_a, w_x, b_x, lam):
    bsz, L, _ = u.shape
    uf = u.astype(jnp.float32)
    ub = uf.reshape(bsz, L, LRU_BLOCKS, LRU_BLOCK)
    r = jax.nn.sigmoid(jnp.einsum('blgi,gij->blgj', ub, w_a).reshape(bsz, L, LRU_WIDTH) + b_a)
    i = jax.nn.sigmoid(jnp.einsum('blgi,gij->blgj', ub, w_x).reshape(bsz, L, LRU_WIDTH) + b_x)
    log_a = -RG_LRU_C * r * jax.nn.softplus(-lam)
    a = jnp.exp(log_a)
    mult = jnp.sqrt(-jnp.expm1(2.0 * log_a))
    return linear_scan(a, mult * i * uf)


def ssd_scan(xh, dt, a, bmat, cmat):
    bsz, L, _, _ = xh.shape
    pad = (-L) % SSD_CHUNK
    def padf(t):
        return jnp.pad(t, [(0, 0), (pad, 0)] + [(0, 0)] * (t.ndim - 2))
    f32 = jnp.float32
    xdt = padf((xh * dt[..., None]).astype(f32))
    adt = padf((dt * a).astype(f32))
    bm = padf(bmat.astype(f32))
    cm = padf(cmat.astype(f32))
    nc = (L + pad) // SSD_CHUNK
    X = xdt.reshape(bsz, nc, SSD_CHUNK, SSD_GROUPS, SSD_HPG, SSD_HEAD_DIM)
    A = adt.reshape(bsz, nc, SSD_CHUNK, SSD_GROUPS, SSD_HPG).transpose(0, 3, 4, 1, 2)
    Bc = bm.reshape(bsz, nc, SSD_CHUNK, SSD_GROUPS, SSD_STATE)
    Cc = cm.reshape(bsz, nc, SSD_CHUNK, SSD_GROUPS, SSD_STATE)
    a_cum = jnp.cumsum(A, axis=-1)
    tri = jnp.tril(jnp.ones((SSD_CHUNK, SSD_CHUNK), bool))
    seg = a_cum[..., :, None] - a_cum[..., None, :]
    decay = jnp.exp(jnp.where(tri, seg, -jnp.inf))
    cb = jnp.einsum('bclgn,bcsgn->bcgls', Cc, Bc)
    y_diag = jnp.einsum('bcgls,bgecls,bcsgep->bclgep', cb, decay, X)
    decay_states = jnp.exp(a_cum[..., -1:] - a_cum)
    states = jnp.einsum('bclgn,bgecl,bclgep->bcgepn', Bc, decay_states, X)
    chunk_tot = jnp.pad(a_cum[..., -1], [(0, 0), (0, 0), (0, 0), (1, 0)])
    cs = jnp.cumsum(chunk_tot, axis=-1)
    tri_c = jnp.tril(jnp.ones((nc + 1, nc + 1), bool))
    decay_chunk = jnp.exp(jnp.where(tri_c, cs[..., :, None] - cs[..., None, :], -jnp.inf))
    states = jnp.concatenate([jnp.zeros_like(states[:, :1]), states], axis=1)
    states = jnp.einsum('bgezc,bcgepn->bzgepn', decay_chunk, states)[:, :-1]
    y_off = jnp.einsum('bclgn,bcgepn,bgecl->bclgep', Cc, states, jnp.exp(a_cum))
    y = (y_diag + y_off).reshape(bsz, L + pad, SSD_HEADS, SSD_HEAD_DIM)
    return y[:, pad:]


def gated_group_rmsnorm(y, z, w):
    bsz, L, W = y.shape
    g = (y * jax.nn.silu(z.astype(jnp.float32))).reshape(bsz, L, SSD_GROUPS, W // SSD_GROUPS)
    g = g * lax.rsqrt(jnp.mean(g * g, axis=-1, keepdims=True) + EPS)
    return g.reshape(bsz, L, W) * w.astype(jnp.float32)


def rglru_ssd_layer(h, norm_w, w_in, lru_conv_w, lru_conv_b, lru_w_a, lru_b_a,
                    lru_w_x, lru_b_x, lru_lambda, ssd_conv_w, ssd_conv_b,
                    ssd_dt_bias, ssd_a_log, ssd_d, ssd_norm, w_out):
    bsz, L, _ = h.shape
    u = rmsnorm(h, norm_w)
    proj = u @ w_in
    lru_x, lru_g, ssd_z, ssd_xbc, ssd_dt = jnp.split(proj, EVEN_SPLITS, axis=-1)
    lx = causal_dwconv(lru_x, lru_conv_w, lru_conv_b)
    y_a = rg_lru(lx, lru_w_a, lru_b_a, lru_w_x, lru_b_x, lru_lambda) * jax.nn.silu(lru_g.astype(jnp.float32))
    xbc = jax.nn.silu(causal_dwconv(ssd_xbc, ssd_conv_w, ssd_conv_b))
    xs, bm, cm = jnp.split(xbc, (SSD_WIDTH, SSD_WIDTH + SSD_GROUPS * SSD_STATE), axis=-1)
    dt = jax.nn.softplus(ssd_dt.astype(jnp.float32) + ssd_dt_bias)
    a = -jnp.exp(ssd_a_log.astype(jnp.float32))
    xh = xs.reshape(bsz, L, SSD_HEADS, SSD_HEAD_DIM)
    y = ssd_scan(xh, dt, a,
                 bm.reshape(bsz, L, SSD_GROUPS, SSD_STATE),
                 cm.reshape(bsz, L, SSD_GROUPS, SSD_STATE))
    y = y + xh.astype(jnp.float32) * ssd_d[:, None]
    y_b = gated_group_rmsnorm(y.reshape(bsz, L, SSD_WIDTH), ssd_z, ssd_norm)
    mixed = jnp.concatenate([y_a, y_b], axis=-1).astype(h.dtype)
    return h + mixed @ w_out


def stick_breaking_block(q_blk, k_ctx, v_ctx, q0):
    tq = q_blk.shape[1]
    s_len = k_ctx.shape[1]
    z = jnp.einsum('bthd,bshd->bhts', q_blk.astype(jnp.float32),
                   k_ctx.astype(jnp.float32)) * (SB_HEAD_DIM ** -0.5)
    before = jnp.arange(s_len)[None, :] < (q0 + jnp.arange(tq))[:, None]
    log_keep = jnp.where(before, jax.nn.log_sigmoid(-z), 0.0)
    csum = jnp.cumsum(log_keep, axis=-1)
    weights = jnp.where(before, jnp.exp(jax.nn.log_sigmoid(z) + csum[..., -1:] - csum), 0.0)
    return jnp.einsum('bhts,bshd->bthd', weights, v_ctx.astype(jnp.float32))


def stick_breaking_layer(h, norm_w, w_in, w_out):
    bsz, L, _ = h.shape
    u = rmsnorm(h, norm_w)
    q, k, v, g = jnp.split(u @ w_in, 4, axis=-1)
    q = q.reshape(bsz, L, SB_HEADS, SB_HEAD_DIM)
    k = k.reshape(bsz, L, SB_HEADS, SB_HEAD_DIM)
    v = v.reshape(bsz, L, SB_HEADS, SB_HEAD_DIM)
    bounds = [0] + list(range(N_META, L, SB_BLOCK)) + [L]
    outs = [stick_breaking_block(q[:, s:e], k[:, :e], v[:, :e], s)
            for s, e in zip(bounds[:-1], bounds[1:])]
    o = jnp.concatenate(outs, axis=1).reshape(bsz, L, SB_WIDTH)
    o = (o * jax.nn.silu(g.astype(jnp.float32))).astype(h.dtype)
    return h + o @ w_out


def _fwd_setup_inputs(seed: int = 0) -> dict:
    key = jax.random.key(seed)
    ks = jax.random.split(key, 24)
    f32 = jnp.float32

    def nrm(k, shape, fan_in):
        return jax.random.normal(k, shape, f32) * (fan_in ** -0.5)

    def gain(k, shape):
        return 1.0 + 0.05 * jax.random.normal(k, shape, f32)

    def bias(k, shape, s=0.05):
        return s * jax.random.normal(k, shape, f32)

    x = jax.random.normal(ks[0], (BATCH, SEQ, D_MODEL), f32)
    meta = jax.random.normal(ks[1], (N_META, D_MODEL), f32)
    even_norm = gain(ks[2], (N_EVEN, D_MODEL))
    even_w_in = nrm(ks[3], (N_EVEN, D_MODEL, EVEN_IN), D_MODEL)
    lru_conv_w = nrm(ks[4], (N_EVEN, CONV_W, LRU_WIDTH), CONV_W)
    lru_conv_b = bias(ks[5], (N_EVEN, LRU_WIDTH))
    lru_w_a = nrm(ks[6], (N_EVEN, LRU_BLOCKS, LRU_BLOCK, LRU_BLOCK), LRU_BLOCK)
    lru_b_a = bias(ks[7], (N_EVEN, LRU_WIDTH), 0.1)
    lru_w_x = nrm(ks[8], (N_EVEN, LRU_BLOCKS, LRU_BLOCK, LRU_BLOCK), LRU_BLOCK)
    lru_b_x = bias(ks[9], (N_EVEN, LRU_WIDTH), 0.1)
    a_c = jax.random.uniform(ks[10], (N_EVEN, LRU_WIDTH), f32, minval=0.9, maxval=0.999)
    a0 = a_c ** (1.0 / RG_LRU_C)
    lru_lambda = jnp.log(a0) - jnp.log1p(-a0)
    ssd_conv_w = nrm(ks[11], (N_EVEN, CONV_W, SSD_CONV_DIM), CONV_W)
    ssd_conv_b = bias(ks[12], (N_EVEN, SSD_CONV_DIM))
    dt0 = jnp.exp(jax.random.uniform(ks[13], (N_EVEN, SSD_HEADS), f32,
                                     minval=float(np.log(1e-3)), maxval=float(np.log(1e-1))))
    ssd_dt_bias = dt0 + jnp.log(-jnp.expm1(-dt0))
    ssd_a_log = jnp.log(jax.random.uniform(ks[14], (N_EVEN, SSD_HEADS), f32, minval=1.0, maxval=16.0))
    ssd_d = gain(ks[15], (N_EVEN, SSD_HEADS))
    ssd_norm = gain(ks[16], (N_EVEN, SSD_WIDTH))
    even_w_out = nrm(ks[17], (N_EVEN, EVEN_MIX, D_MODEL), EVEN_MIX)
    odd_norm = gain(ks[18], (N_ODD, D_MODEL))
    odd_w_in = nrm(ks[19], (N_ODD, D_MODEL, ODD_IN), D_MODEL)
    odd_w_out = nrm(ks[20], (N_ODD, SB_WIDTH, D_MODEL), SB_WIDTH)
    final_norm = gain(ks[21], (D_MODEL,))
    return {'x': x, 'meta': meta, 'even_norm': even_norm, 'even_w_in': even_w_in,
            'lru_conv_w': lru_conv_w, 'lru_conv_b': lru_conv_b,
            'lru_w_a': lru_w_a, 'lru_b_a': lru_b_a, 'lru_w_x': lru_w_x, 'lru_b_x': lru_b_x,
            'lru_lambda': lru_lambda, 'ssd_conv_w': ssd_conv_w, 'ssd_conv_b': ssd_conv_b,
            'ssd_dt_bias': ssd_dt_bias, 'ssd_a_log': ssd_a_log, 'ssd_d': ssd_d,
            'ssd_norm': ssd_norm, 'even_w_out': even_w_out, 'odd_norm': odd_norm,
            'odd_w_in': odd_w_in, 'odd_w_out': odd_w_out, 'final_norm': final_norm}


def _fwd_reference(x, meta, even_norm, even_w_in, lru_conv_w, lru_conv_b, lru_w_a, lru_b_a,
              lru_w_x, lru_b_x, lru_lambda, ssd_conv_w, ssd_conv_b, ssd_dt_bias,
              ssd_a_log, ssd_d, ssd_norm, even_w_out, odd_norm, odd_w_in, odd_w_out,
              final_norm):
    bsz = x.shape[0]
    meta_b = jnp.broadcast_to(meta[None].astype(x.dtype), (bsz, N_META, D_MODEL))
    h = jnp.concatenate([meta_b, x], axis=1)
    for layer in range(DEPTH):
        j = layer // 2
        if layer % 2 == 0:
            h = rglru_ssd_layer(h, even_norm[j], even_w_in[j], lru_conv_w[j], lru_conv_b[j],
                                lru_w_a[j], lru_b_a[j], lru_w_x[j], lru_b_x[j], lru_lambda[j],
                                ssd_conv_w[j], ssd_conv_b[j], ssd_dt_bias[j], ssd_a_log[j],
                                ssd_d[j], ssd_norm[j], even_w_out[j])
        else:
            h = stick_breaking_layer(h, odd_norm[j], odd_w_in[j], odd_w_out[j])
    return rmsnorm(h, final_norm)[:, N_META:].astype(x.dtype)


import jax as _jax
import jax.numpy as _jnp

TWIN_FORMAT = 'train_step'
FWD_PARAMS = ['x', 'meta', 'even_norm', 'even_w_in', 'lru_conv_w', 'lru_conv_b', 'lru_w_a', 'lru_b_a', 'lru_w_x', 'lru_b_x', 'lru_lambda', 'ssd_conv_w', 'ssd_conv_b', 'ssd_dt_bias', 'ssd_a_log', 'ssd_d', 'ssd_norm', 'even_w_out', 'odd_norm', 'odd_w_in', 'odd_w_out', 'final_norm']
TWIN_WEIGHTS = ['meta', 'even_norm', 'even_w_in', 'lru_conv_w', 'lru_conv_b', 'lru_w_a', 'lru_b_a', 'lru_w_x', 'lru_b_x', 'lru_lambda', 'ssd_conv_w', 'ssd_conv_b', 'ssd_dt_bias', 'ssd_a_log', 'ssd_d', 'ssd_norm', 'even_w_out', 'odd_norm', 'odd_w_in', 'odd_w_out', 'final_norm']
TWIN_DIFF_INPUT = 'x'
TWIN_INPUTS = ['x', 'meta', 'even_norm', 'even_w_in', 'lru_conv_w', 'lru_conv_b', 'lru_w_a', 'lru_b_a', 'lru_w_x', 'lru_b_x', 'lru_lambda', 'ssd_conv_w', 'ssd_conv_b', 'ssd_dt_bias', 'ssd_a_log', 'ssd_d', 'ssd_norm', 'even_w_out', 'odd_norm', 'odd_w_in', 'odd_w_out', 'final_norm', 'loss_target', 'm_meta', 'm_even_norm', 'm_even_w_in', 'm_lru_conv_w', 'm_lru_conv_b', 'm_lru_w_a', 'm_lru_b_a', 'm_lru_w_x', 'm_lru_b_x', 'm_lru_lambda', 'm_ssd_conv_w', 'm_ssd_conv_b', 'm_ssd_dt_bias', 'm_ssd_a_log', 'm_ssd_d', 'm_ssd_norm', 'm_even_w_out', 'm_odd_norm', 'm_odd_w_in', 'm_odd_w_out', 'm_final_norm', 'v_meta', 'v_even_norm', 'v_even_w_in', 'v_lru_conv_w', 'v_lru_conv_b', 'v_lru_w_a', 'v_lru_b_a', 'v_lru_w_x', 'v_lru_b_x', 'v_lru_lambda', 'v_ssd_conv_w', 'v_ssd_conv_b', 'v_ssd_dt_bias', 'v_ssd_a_log', 'v_ssd_d', 'v_ssd_norm', 'v_even_w_out', 'v_odd_norm', 'v_odd_w_in', 'v_odd_w_out', 'v_final_norm']
TWIN_OUTPUTS = ['loss', 'grad_x', 'grad_meta', 'grad_even_norm', 'grad_even_w_in', 'grad_lru_conv_w', 'grad_lru_conv_b', 'grad_lru_w_a', 'grad_lru_b_a', 'grad_lru_w_x', 'grad_lru_b_x', 'grad_lru_lambda', 'grad_ssd_conv_w', 'grad_ssd_conv_b', 'grad_ssd_dt_bias', 'grad_ssd_a_log', 'grad_ssd_d', 'grad_ssd_norm', 'grad_even_w_out', 'grad_odd_norm', 'grad_odd_w_in', 'grad_odd_w_out', 'grad_final_norm', 'delta_meta', 'delta_even_norm', 'delta_even_w_in', 'delta_lru_conv_w', 'delta_lru_conv_b', 'delta_lru_w_a', 'delta_lru_b_a', 'delta_lru_w_x', 'delta_lru_b_x', 'delta_lru_lambda', 'delta_ssd_conv_w', 'delta_ssd_conv_b', 'delta_ssd_dt_bias', 'delta_ssd_a_log', 'delta_ssd_d', 'delta_ssd_norm', 'delta_even_w_out', 'delta_odd_norm', 'delta_odd_w_in', 'delta_odd_w_out', 'delta_final_norm', 'new_m_meta', 'new_m_even_norm', 'new_m_even_w_in', 'new_m_lru_conv_w', 'new_m_lru_conv_b', 'new_m_lru_w_a', 'new_m_lru_b_a', 'new_m_lru_w_x', 'new_m_lru_b_x', 'new_m_lru_lambda', 'new_m_ssd_conv_w', 'new_m_ssd_conv_b', 'new_m_ssd_dt_bias', 'new_m_ssd_a_log', 'new_m_ssd_d', 'new_m_ssd_norm', 'new_m_even_w_out', 'new_m_odd_norm', 'new_m_odd_w_in', 'new_m_odd_w_out', 'new_m_final_norm', 'new_v_meta', 'new_v_even_norm', 'new_v_even_w_in', 'new_v_lru_conv_w', 'new_v_lru_conv_b', 'new_v_lru_w_a', 'new_v_lru_b_a', 'new_v_lru_w_x', 'new_v_lru_b_x', 'new_v_lru_lambda', 'new_v_ssd_conv_w', 'new_v_ssd_conv_b', 'new_v_ssd_dt_bias', 'new_v_ssd_a_log', 'new_v_ssd_d', 'new_v_ssd_norm', 'new_v_even_w_out', 'new_v_odd_norm', 'new_v_odd_w_in', 'new_v_odd_w_out', 'new_v_final_norm']
TWIN_LEAF_KINDS = {'loss': 'loss', 'grad_x': 'grad_x', 'grad_meta': 'grad_w', 'grad_even_norm': 'grad_w', 'grad_even_w_in': 'grad_w', 'grad_lru_conv_w': 'grad_w', 'grad_lru_conv_b': 'grad_w', 'grad_lru_w_a': 'grad_w', 'grad_lru_b_a': 'grad_w', 'grad_lru_w_x': 'grad_w', 'grad_lru_b_x': 'grad_w', 'grad_lru_lambda': 'grad_w', 'grad_ssd_conv_w': 'grad_w', 'grad_ssd_conv_b': 'grad_w', 'grad_ssd_dt_bias': 'grad_w', 'grad_ssd_a_log': 'grad_w', 'grad_ssd_d': 'grad_w', 'grad_ssd_norm': 'grad_w', 'grad_even_w_out': 'grad_w', 'grad_odd_norm': 'grad_w', 'grad_odd_w_in': 'grad_w', 'grad_odd_w_out': 'grad_w', 'grad_final_norm': 'grad_w', 'delta_meta': 'delta_w', 'delta_even_norm': 'delta_w', 'delta_even_w_in': 'delta_w', 'delta_lru_conv_w': 'delta_w', 'delta_lru_conv_b': 'delta_w', 'delta_lru_w_a': 'delta_w', 'delta_lru_b_a': 'delta_w', 'delta_lru_w_x': 'delta_w', 'delta_lru_b_x': 'delta_w', 'delta_lru_lambda': 'delta_w', 'delta_ssd_conv_w': 'delta_w', 'delta_ssd_conv_b': 'delta_w', 'delta_ssd_dt_bias': 'delta_w', 'delta_ssd_a_log': 'delta_w', 'delta_ssd_d': 'delta_w', 'delta_ssd_norm': 'delta_w', 'delta_even_w_out': 'delta_w', 'delta_odd_norm': 'delta_w', 'delta_odd_w_in': 'delta_w', 'delta_odd_w_out': 'delta_w', 'delta_final_norm': 'delta_w', 'new_m_meta': 'new_m', 'new_m_even_norm': 'new_m', 'new_m_even_w_in': 'new_m', 'new_m_lru_conv_w': 'new_m', 'new_m_lru_conv_b': 'new_m', 'new_m_lru_w_a': 'new_m', 'new_m_lru_b_a': 'new_m', 'new_m_lru_w_x': 'new_m', 'new_m_lru_b_x': 'new_m', 'new_m_lru_lambda': 'new_m', 'new_m_ssd_conv_w': 'new_m', 'new_m_ssd_conv_b': 'new_m', 'new_m_ssd_dt_bias': 'new_m', 'new_m_ssd_a_log': 'new_m', 'new_m_ssd_d': 'new_m', 'new_m_ssd_norm': 'new_m', 'new_m_even_w_out': 'new_m', 'new_m_odd_norm': 'new_m', 'new_m_odd_w_in': 'new_m', 'new_m_odd_w_out': 'new_m', 'new_m_final_norm': 'new_m', 'new_v_meta': 'new_v', 'new_v_even_norm': 'new_v', 'new_v_even_w_in': 'new_v', 'new_v_lru_conv_w': 'new_v', 'new_v_lru_conv_b': 'new_v', 'new_v_lru_w_a': 'new_v', 'new_v_lru_b_a': 'new_v', 'new_v_lru_w_x': 'new_v', 'new_v_lru_b_x': 'new_v', 'new_v_lru_lambda': 'new_v', 'new_v_ssd_conv_w': 'new_v', 'new_v_ssd_conv_b': 'new_v', 'new_v_ssd_dt_bias': 'new_v', 'new_v_ssd_a_log': 'new_v', 'new_v_ssd_d': 'new_v', 'new_v_ssd_norm': 'new_v', 'new_v_even_w_out': 'new_v', 'new_v_odd_norm': 'new_v', 'new_v_odd_w_in': 'new_v', 'new_v_odd_w_out': 'new_v', 'new_v_final_norm': 'new_v'}


def _forward(args):
    return _fwd_reference(*[args[k] for k in FWD_PARAMS])


def _output_shape():
    out = _jax.eval_shape(lambda: _forward(_fwd_setup_inputs(0)))
    return out.shape, out.dtype

N_MICROBATCH = 1
ADAM_LR = 0.001
ADAM_B1 = 0.9
ADAM_B2 = 0.999
ADAM_EPS = 1e-08
ADAM_WD = 0.01
ADAM_STEP = 10
PER_EXAMPLE_BATCH_AXIS = {'x': 0, 'loss_target': 0}
SHARED_INPUTS = []
_WEIGHT_DTYPES = {'meta': _jnp.float32, 'even_norm': _jnp.float32, 'even_w_in': _jnp.float32, 'lru_conv_w': _jnp.float32, 'lru_conv_b': _jnp.float32, 'lru_w_a': _jnp.float32, 'lru_b_a': _jnp.float32, 'lru_w_x': _jnp.float32, 'lru_b_x': _jnp.float32, 'lru_lambda': _jnp.float32, 'ssd_conv_w': _jnp.float32, 'ssd_conv_b': _jnp.float32, 'ssd_dt_bias': _jnp.float32, 'ssd_a_log': _jnp.float32, 'ssd_d': _jnp.float32, 'ssd_norm': _jnp.float32, 'even_w_out': _jnp.float32, 'odd_norm': _jnp.float32, 'odd_w_in': _jnp.float32, 'odd_w_out': _jnp.float32, 'final_norm': _jnp.float32}
MOMENT_SCALE = {'meta': 5.717471e-03, 'even_norm': 2.529965e-01, 'even_w_in': 1.161728e-01, 'lru_conv_w': 6.450784e-02, 'lru_conv_b': 9.481270e-01, 'lru_w_a': 1.939526e-02, 'lru_b_a': 1.291687e-02, 'lru_w_x': 3.516693e-02, 'lru_b_x': 3.321649e-02, 'lru_lambda': 2.669825e-02, 'ssd_conv_w': 1.376995e-01, 'ssd_conv_b': 1.844764e-01, 'ssd_dt_bias': 1.155667e+00, 'ssd_a_log': 1.156599e+00, 'ssd_d': 9.586369e-01, 'ssd_norm': 1.562311e-01, 'even_w_out': 1.730571e-01, 'odd_norm': 1.158318e-01, 'odd_w_in': 5.962752e-02, 'odd_w_out': 7.587001e-02, 'final_norm': 6.410381e+01}


def _to_microbatches(a, axis):
    t = _jnp.moveaxis(a, axis, 0)
    t = t.reshape((N_MICROBATCH, t.shape[0] // N_MICROBATCH) + t.shape[1:])
    return _jnp.moveaxis(t, 1, axis + 1)


def setup_inputs(seed: int = 0) -> dict:
    inp = _fwd_setup_inputs(seed)
    key = _jax.random.fold_in(_jax.random.key(seed), 7919)
    shape, _ = _output_shape()
    out = dict(inp)
    out["loss_target"] = _jax.random.normal(_jax.random.fold_in(key, 0), shape, _jnp.float32)
    for i, name in enumerate(TWIN_WEIGHTS):
        w = inp[name].astype(_jnp.float32)
        if MOMENT_SCALE is None:
            s = _jnp.sqrt(_jnp.mean(_jnp.square(w)) + 1e-30)
        else:
            s = MOMENT_SCALE[name]
        km, kv = _jax.random.split(_jax.random.fold_in(key, i + 1))
        out[name] = w
        out["m_" + name] = s * _jax.random.normal(km, w.shape, _jnp.float32)
        out["v_" + name] = (s * s) * _jax.random.uniform(kv, w.shape, _jnp.float32, 0.5, 1.5)
    if N_MICROBATCH > 1:
        for name, axis in PER_EXAMPLE_BATCH_AXIS.items():
            out[name] = _to_microbatches(out[name], axis)
    return {'x': out['x'], 'meta': out['meta'], 'even_norm': out['even_norm'], 'even_w_in': out['even_w_in'], 'lru_conv_w': out['lru_conv_w'], 'lru_conv_b': out['lru_conv_b'], 'lru_w_a': out['lru_w_a'], 'lru_b_a': out['lru_b_a'], 'lru_w_x': out['lru_w_x'], 'lru_b_x': out['lru_b_x'], 'lru_lambda': out['lru_lambda'], 'ssd_conv_w': out['ssd_conv_w'], 'ssd_conv_b': out['ssd_conv_b'], 'ssd_dt_bias': out['ssd_dt_bias'], 'ssd_a_log': out['ssd_a_log'], 'ssd_d': out['ssd_d'], 'ssd_norm': out['ssd_norm'], 'even_w_out': out['even_w_out'], 'odd_norm': out['odd_norm'], 'odd_w_in': out['odd_w_in'], 'odd_w_out': out['odd_w_out'], 'final_norm': out['final_norm'], 'loss_target': out['loss_target'], 'm_meta': out['m_meta'], 'm_even_norm': out['m_even_norm'], 'm_even_w_in': out['m_even_w_in'], 'm_lru_conv_w': out['m_lru_conv_w'], 'm_lru_conv_b': out['m_lru_conv_b'], 'm_lru_w_a': out['m_lru_w_a'], 'm_lru_b_a': out['m_lru_b_a'], 'm_lru_w_x': out['m_lru_w_x'], 'm_lru_b_x': out['m_lru_b_x'], 'm_lru_lambda': out['m_lru_lambda'], 'm_ssd_conv_w': out['m_ssd_conv_w'], 'm_ssd_conv_b': out['m_ssd_conv_b'], 'm_ssd_dt_bias': out['m_ssd_dt_bias'], 'm_ssd_a_log': out['m_ssd_a_log'], 'm_ssd_d': out['m_ssd_d'], 'm_ssd_norm': out['m_ssd_norm'], 'm_even_w_out': out['m_even_w_out'], 'm_odd_norm': out['m_odd_norm'], 'm_odd_w_in': out['m_odd_w_in'], 'm_odd_w_out': out['m_odd_w_out'], 'm_final_norm': out['m_final_norm'], 'v_meta': out['v_meta'], 'v_even_norm': out['v_even_norm'], 'v_even_w_in': out['v_even_w_in'], 'v_lru_conv_w': out['v_lru_conv_w'], 'v_lru_conv_b': out['v_lru_conv_b'], 'v_lru_w_a': out['v_lru_w_a'], 'v_lru_b_a': out['v_lru_b_a'], 'v_lru_w_x': out['v_lru_w_x'], 'v_lru_b_x': out['v_lru_b_x'], 'v_lru_lambda': out['v_lru_lambda'], 'v_ssd_conv_w': out['v_ssd_conv_w'], 'v_ssd_conv_b': out['v_ssd_conv_b'], 'v_ssd_dt_bias': out['v_ssd_dt_bias'], 'v_ssd_a_log': out['v_ssd_a_log'], 'v_ssd_d': out['v_ssd_d'], 'v_ssd_norm': out['v_ssd_norm'], 'v_even_w_out': out['v_even_w_out'], 'v_odd_norm': out['v_odd_norm'], 'v_odd_w_in': out['v_odd_w_in'], 'v_odd_w_out': out['v_odd_w_out'], 'v_final_norm': out['v_final_norm']}


def _loss(weights, diff, rest, loss_target):
    with _jax.named_scope("forward"):
        args = {**rest, TWIN_DIFF_INPUT: diff, **{k: w.astype(_WEIGHT_DTYPES[k]) for k, w in weights.items()}}
        y = _forward(args)
    with _jax.named_scope("loss_head"):
        err = _jnp.square(y.astype(_jnp.float32) - loss_target)
        return 0.5 * _jnp.sum(_jnp.mean(err, axis=-1)) if err.ndim else 0.5 * err


def _adamw(w, g, m, v):
    m = ADAM_B1 * m + (1.0 - ADAM_B1) * g
    v = ADAM_B2 * v + (1.0 - ADAM_B2) * _jnp.square(g)
    m_hat = m / (1.0 - ADAM_B1 ** ADAM_STEP)
    v_hat = v / (1.0 - ADAM_B2 ** ADAM_STEP)
    delta = -ADAM_LR * (m_hat / (_jnp.sqrt(v_hat) + ADAM_EPS) + ADAM_WD * w)
    return delta, m, v


def reference(x, meta, even_norm, even_w_in, lru_conv_w, lru_conv_b, lru_w_a, lru_b_a, lru_w_x, lru_b_x, lru_lambda, ssd_conv_w, ssd_conv_b, ssd_dt_bias, ssd_a_log, ssd_d, ssd_norm, even_w_out, odd_norm, odd_w_in, odd_w_out, final_norm, loss_target, m_meta, m_even_norm, m_even_w_in, m_lru_conv_w, m_lru_conv_b, m_lru_w_a, m_lru_b_a, m_lru_w_x, m_lru_b_x, m_lru_lambda, m_ssd_conv_w, m_ssd_conv_b, m_ssd_dt_bias, m_ssd_a_log, m_ssd_d, m_ssd_norm, m_even_w_out, m_odd_norm, m_odd_w_in, m_odd_w_out, m_final_norm, v_meta, v_even_norm, v_even_w_in, v_lru_conv_w, v_lru_conv_b, v_lru_w_a, v_lru_b_a, v_lru_w_x, v_lru_b_x, v_lru_lambda, v_ssd_conv_w, v_ssd_conv_b, v_ssd_dt_bias, v_ssd_a_log, v_ssd_d, v_ssd_norm, v_even_w_out, v_odd_norm, v_odd_w_in, v_odd_w_out, v_final_norm):
    given = dict(x=x, meta=meta, even_norm=even_norm, even_w_in=even_w_in, lru_conv_w=lru_conv_w, lru_conv_b=lru_conv_b, lru_w_a=lru_w_a, lru_b_a=lru_b_a, lru_w_x=lru_w_x, lru_b_x=lru_b_x, lru_lambda=lru_lambda, ssd_conv_w=ssd_conv_w, ssd_conv_b=ssd_conv_b, ssd_dt_bias=ssd_dt_bias, ssd_a_log=ssd_a_log, ssd_d=ssd_d, ssd_norm=ssd_norm, even_w_out=even_w_out, odd_norm=odd_norm, odd_w_in=odd_w_in, odd_w_out=odd_w_out, final_norm=final_norm, loss_target=loss_target, m_meta=m_meta, m_even_norm=m_even_norm, m_even_w_in=m_even_w_in, m_lru_conv_w=m_lru_conv_w, m_lru_conv_b=m_lru_conv_b, m_lru_w_a=m_lru_w_a, m_lru_b_a=m_lru_b_a, m_lru_w_x=m_lru_w_x, m_lru_b_x=m_lru_b_x, m_lru_lambda=m_lru_lambda, m_ssd_conv_w=m_ssd_conv_w, m_ssd_conv_b=m_ssd_conv_b, m_ssd_dt_bias=m_ssd_dt_bias, m_ssd_a_log=m_ssd_a_log, m_ssd_d=m_ssd_d, m_ssd_norm=m_ssd_norm, m_even_w_out=m_even_w_out, m_odd_norm=m_odd_norm, m_odd_w_in=m_odd_w_in, m_odd_w_out=m_odd_w_out, m_final_norm=m_final_norm, v_meta=v_meta, v_even_norm=v_even_norm, v_even_w_in=v_even_w_in, v_lru_conv_w=v_lru_conv_w, v_lru_conv_b=v_lru_conv_b, v_lru_w_a=v_lru_w_a, v_lru_b_a=v_lru_b_a, v_lru_w_x=v_lru_w_x, v_lru_b_x=v_lru_b_x, v_lru_lambda=v_lru_lambda, v_ssd_conv_w=v_ssd_conv_w, v_ssd_conv_b=v_ssd_conv_b, v_ssd_dt_bias=v_ssd_dt_bias, v_ssd_a_log=v_ssd_a_log, v_ssd_d=v_ssd_d, v_ssd_norm=v_ssd_norm, v_even_w_out=v_even_w_out, v_odd_norm=v_odd_norm, v_odd_w_in=v_odd_w_in, v_odd_w_out=v_odd_w_out, v_final_norm=v_final_norm)
    weights = {n: given[n] for n in TWIN_WEIGHTS}
    shared = {n: given[n] for n in SHARED_INPUTS}
    per_example = {n: given[n] for n in ['x']}
    grad_fn = _jax.value_and_grad(_loss, argnums=(0, 1))

    def one_microbatch(ex, loss_target):
        ex = dict(ex)
        diff = ex.pop(TWIN_DIFF_INPUT)
        return grad_fn(weights, diff, {**shared, **ex}, loss_target)

    if N_MICROBATCH == 1:
        loss, (grad_w, grad_x) = one_microbatch(per_example, given["loss_target"])
    else:
        def body(carry, xs):
            loss_sum, grad_sum = carry
            l_k, (gw_k, gx_k) = one_microbatch(xs[0], xs[1])
            with _jax.named_scope("update"):
                return (loss_sum + l_k, _jax.tree.map(_jnp.add, grad_sum, gw_k)), gx_k

        init = (_jnp.zeros((), _jnp.float32), _jax.tree.map(_jnp.zeros_like, weights))
        (loss, grad_w), grad_x = _jax.lax.scan(body, init, (per_example, given["loss_target"]))
    with _jax.named_scope("update"):
        delta_w, new_m, new_v = {}, {}, {}
        for n in TWIN_WEIGHTS:
            delta_w[n], new_m[n], new_v[n] = _adamw(weights[n], grad_w[n], given["m_" + n], given["v_" + n])
    return (loss, grad_x, *[grad_w[n] for n in TWIN_WEIGHTS], *[delta_w[n] for n in TWIN_WEIGHTS],
            *[new_m[n] for n in TWIN_WEIGHTS], *[new_v[n] for n in TWIN_WEIGHTS])
```

```python
import functools

import jax
import jax.numpy as jnp
from jax import lax
from jax.experimental import pallas as pl
from jax.experimental.pallas import tpu as pltpu

F32 = jnp.float32
BF16 = jnp.bfloat16

D_MODEL = 1024
N_META = 16
PAD = 112
EPS = 1e-6
CONV_W = 4
LRU_BLOCKS = 4
LRU_BLOCK = 256
RG_LRU_C = 8.0
SSD_HEADS = 16
SSD_P = 64
SSD_N = 128
SSD_Q = 64
SSD_GROUPS = 2
SSD_GW = 512
SSD_CONV_DIM = 1536
SB_HEADS = 16
SB_D = 64
SB_T = 128
EVEN_IN = 4624
EVEN_NP = 4864
DT_W = 256
ODD_IN = 4096
N_DEV = 8

ADAM_LR = 0.001
ADAM_B1 = 0.9
ADAM_B2 = 0.999
ADAM_EPS = 1e-08
ADAM_WD = 0.01
ADAM_STEP = 10

VMEM_LIMIT = 56 * 1024 * 1024


def _cparams(*sem):
    return pltpu.CompilerParams(dimension_semantics=sem, vmem_limit_bytes=VMEM_LIMIT)


def _row_tile(t):
    for c in (512, 256, 128):
        if t % c == 0:
            return c
    raise ValueError(t)


def _col_tile(n):
    for c in (512, 256, 128):
        if n % c == 0:
            return c
    raise ValueError(n)


def _dot(a, b):
    return jnp.dot(a, b, preferred_element_type=F32)


def _dot_nt(a, b):
    return lax.dot_general(a, b, (((1,), (1,)), ((), ())), preferred_element_type=F32)


def _dot_tn(a, b):
    return lax.dot_general(a, b, (((0,), (0,)), ((), ())), preferred_element_type=F32)


def _split3(a):
    a1 = a.astype(BF16)
    r1 = a - a1.astype(F32)
    a2 = r1.astype(BF16)
    a3 = (r1 - a2.astype(F32)).astype(BF16)
    return a1, a2, a3


def _xdot_r(a, m01):
    a1, a2, a3 = _split3(a)
    return _dot(a1, m01) + _dot(a2, m01) + _dot(a3, m01)


def _xdot_l(m01, a):
    a1, a2, a3 = _split3(a)
    return _dot(m01, a1) + _dot(m01, a2) + _dot(m01, a3)


def _xdot_tn_l(m01, a):
    a1, a2, a3 = _split3(a)
    return _dot_tn(m01, a1) + _dot_tn(m01, a2) + _dot_tn(m01, a3)


def _sigmoid(x):
    return 1.0 / (1.0 + jnp.exp(-x))


def _softplus(x):
    return jnp.maximum(x, 0.0) + jnp.log(1.0 + jnp.exp(-jnp.abs(x)))


def _expm1(x):
    p = x * (1.0 + x * (1.0 / 2) * (1.0 + x * (1.0 / 3) * (1.0 + x * (1.0 / 4) * (1.0 + x * (1.0 / 5) * (1.0 + x * (1.0 / 6))))))
    return jnp.where(jnp.abs(x) < 0.25, p, jnp.exp(x) - 1.0)


def _iota(shape, dim):
    return lax.broadcasted_iota(jnp.int32, shape, dim)


def _mm_nn(a, b, out_dtype, name, res=None):
    m, k = a.shape
    _, n = b.shape
    tm, tn = _row_tile(m), _col_tile(n)

    def body(*refs):
        if res is None:
            a_ref, b_ref, o_ref = refs
        else:
            a_ref, b_ref, r_ref, o_ref = refs
        acc = _dot(a_ref[...].astype(BF16), b_ref[...].astype(BF16))
        if res is not None:
            acc = acc + r_ref[...]
        o_ref[...] = acc.astype(out_dtype)

    in_specs = [pl.BlockSpec((tm, k), lambda i, j: (i, 0)), pl.BlockSpec((k, tn), lambda i, j: (0, j))]
    args = [a, b]
    if res is not None:
        in_specs.append(pl.BlockSpec((tm, tn), lambda i, j: (i, j)))
        args.append(res)
    return pl.pallas_call(
        body, grid=(m // tm, n // tn), in_specs=in_specs,
        out_specs=pl.BlockSpec((tm, tn), lambda i, j: (i, j)),
        out_shape=jax.ShapeDtypeStruct((m, n), out_dtype),
        compiler_params=_cparams("parallel", "parallel"), name=name)(*args)


def _mm_nt(a, b, out_dtype, name):
    m, k = a.shape
    n, _ = b.shape
    tm, tn = _row_tile(m), _col_tile(n)

    def body(a_ref, b_ref, o_ref):
        o_ref[...] = _dot_nt(a_ref[...].astype(BF16), b_ref[...].astype(BF16)).astype(out_dtype)

    return pl.pallas_call(
        body, grid=(m // tm, n // tn),
        in_specs=[pl.BlockSpec((tm, k), lambda i, j: (i, 0)), pl.BlockSpec((tn, k), lambda i, j: (j, 0))],
        out_specs=pl.BlockSpec((tm, tn), lambda i, j: (i, j)),
        out_shape=jax.ShapeDtypeStruct((m, n), out_dtype),
        compiler_params=_cparams("parallel", "parallel"), name=name)(a, b)


def _mm_tn(a, b, name):
    t, m = a.shape
    _, n = b.shape
    tk, tm, tn = _row_tile(t), _col_tile(m), _col_tile(n)
    if m % 1024 == 0:
        tm = 1024

    def body(a_ref, b_ref, o_ref):
        @pl.when(pl.program_id(2) == 0)
        def _():
            o_ref[...] = jnp.zeros_like(o_ref)
        o_ref[...] += _dot_tn(a_ref[...].astype(BF16), b_ref[...].astype(BF16))

    return pl.pallas_call(
        body, grid=(m // tm, n // tn, t // tk),
        in_specs=[pl.BlockSpec((tk, tm), lambda i, j, s: (s, i)), pl.BlockSpec((tk, tn), lambda i, j, s: (s, j))],
        out_specs=pl.BlockSpec((tm, tn), lambda i, j, s: (i, j)),
        out_shape=jax.ShapeDtypeStruct((m, n), F32),
        compiler_params=_cparams("parallel", "parallel", "arbitrary"), name=name)(a, b)


def _norm_fwd(h, w, name):
    t, d = h.shape
    tm = _row_tile(t)

    def body(h_ref, w_ref, u_ref):
        x = h_ref[...]
        r = lax.rsqrt(jnp.mean(x * x, axis=-1, keepdims=True) + EPS)
        u_ref[...] = (x * r * w_ref[...]).astype(BF16)

    return pl.pallas_call(
        body, grid=(t // tm,),
        in_specs=[pl.BlockSpec((tm, d), lambda i: (i, 0)), pl.BlockSpec((1, d), lambda i: (0, 0))],
        out_specs=pl.BlockSpec((tm, d), lambda i: (i, 0)),
        out_shape=jax.ShapeDtypeStruct((t, d), BF16),
        compiler_params=_cparams("parallel"), name=name)(h, w)


def _norm_bwd(h, w, du, dres, name):
    t, d = h.shape
    tm = _row_tile(t)

    def body(h_ref, w_ref, du_ref, dr_ref, dh_ref, dw_ref):
        @pl.when(pl.program_id(0) == 0)
        def _():
            dw_ref[...] = jnp.zeros_like(dw_ref)
        x = h_ref[...]
        r = lax.rsqrt(jnp.mean(x * x, axis=-1, keepdims=True) + EPS)
        xh = x * r
        du_ = du_ref[...]
        g = du_ * w_ref[...]
        dh_ref[...] = dr_ref[...] + r * (g - xh * jnp.mean(g * xh, axis=-1, keepdims=True))
        dw_ref[...] += jnp.sum(du_ * xh, axis=0, keepdims=True)

    return pl.pallas_call(
        body, grid=(t // tm,),
        in_specs=[pl.BlockSpec((tm, d), lambda i: (i, 0)), pl.BlockSpec((1, d), lambda i: (0, 0)),
                  pl.BlockSpec((tm, d), lambda i: (i, 0)), pl.BlockSpec((tm, d), lambda i: (i, 0))],
        out_specs=[pl.BlockSpec((tm, d), lambda i: (i, 0)), pl.BlockSpec((1, d), lambda i: (0, 0))],
        out_shape=[jax.ShapeDtypeStruct((t, d), F32), jax.ShapeDtypeStruct((1, d), F32)],
        compiler_params=_cparams("arbitrary"), name=name)(h, w, du, dres)


def _final_loss(h2, w, tgt, lp):
    t, d = h2.shape
    bsz, seq, _ = tgt.shape
    nblk = lp // SB_T

    def body(h_ref, w_ref, t_ref, loss_ref, dh_ref, dw_ref):
        b, i = pl.program_id(0), pl.program_id(1)

        @pl.when((b == 0) & (i == 0))
        def _():
            loss_ref[...] = jnp.zeros_like(loss_ref)
            dw_ref[...] = jnp.zeros_like(dw_ref)

        @pl.when(i == 0)
        def _():
            dh_ref[...] = jnp.zeros_like(dh_ref)

        @pl.when(i > 0)
        def _():
            x = h_ref[...]
            r = lax.rsqrt(jnp.mean(x * x, axis=-1, keepdims=True) + EPS)
            xh = x * r
            wv = w_ref[...]
            diff = xh * wv - t_ref[0]
            loss_ref[...] += 0.5 * jnp.sum(jnp.mean(diff * diff, axis=-1, keepdims=True), axis=0, keepdims=True)
            dy = diff * (1.0 / d)
            g = dy * wv
            dh_ref[...] = r * (g - xh * jnp.mean(g * xh, axis=-1, keepdims=True))
            dw_ref[...] += jnp.sum(dy * xh, axis=0, keepdims=True)

    return pl.pallas_call(
        body, grid=(bsz, nblk),
        in_specs=[pl.BlockSpec((SB_T, d), lambda b, i: (b * nblk + i, 0)), pl.BlockSpec((1, d), lambda b, i: (0, 0)),
                  pl.BlockSpec((1, SB_T, d), lambda b, i: (b, jnp.maximum(i - 1, 0), 0))],
        out_specs=[pl.BlockSpec((1, 128), lambda b, i: (0, 0)), pl.BlockSpec((SB_T, d), lambda b, i: (b * nblk + i, 0)),
                   pl.BlockSpec((1, d), lambda b, i: (0, 0))],
        out_shape=[jax.ShapeDtypeStruct((1, 128), F32), jax.ShapeDtypeStruct((t, d), F32), jax.ShapeDtypeStruct((1, d), F32)],
        compiler_params=_cparams("arbitrary", "arbitrary"), name="final_loss")(h2, w, tgt)


def _shift_down(x, j):
    return x if j == 0 else pltpu.roll(x, j, 0)


def _shift_up(x, j):
    return x if j == 0 else pltpu.roll(x, x.shape[0] - j, 0)


def _conv(x, cw, cb):
    out = cb + cw[CONV_W - 1:CONV_W, :] * x
    for j in range(1, CONV_W):
        out = out + cw[CONV_W - 1 - j:CONV_W - j, :] * _shift_down(x, j)
    return out


def _conv_bwd_x(dy, cw):
    out = cw[CONV_W - 1:CONV_W, :] * dy
    for j in range(1, CONV_W):
        out = out + cw[CONV_W - 1 - j:CONV_W - j, :] * _shift_up(dy, j)
    return out


def _conv_bwd_w(dcw_ref, dy, x):
    for j in range(CONV_W):
        k = CONV_W - 1 - j
        dcw_ref[k:k + 1, :] += jnp.sum(dy * _shift_down(x, j), axis=0, keepdims=True)


def _lru_forward_block(x, cw, cb, wa, ba, wx, bx, lam, a_ref, b_ref, h_ref):
    lp = x.shape[0]
    lx = _conv(x, cw, cb)
    lxb = lx.astype(BF16)
    r = _sigmoid(_dot(lxb, wa) + ba)
    i = _sigmoid(_dot(lxb, wx) + bx)
    spl = _softplus(-lam)
    log_a = (-RG_LRU_C) * r * spl
    a = jnp.exp(log_a)
    mult = jnp.sqrt(-_expm1(2.0 * log_a))
    valid = _iota((lp, 1), 0) >= PAD
    a_ref[...] = a
    b_ref[...] = jnp.where(valid, mult * i * lx, 0.0)

    def step(t, h):
        h = a_ref[pl.ds(t, 1), :] * h + b_ref[pl.ds(t, 1), :]
        h_ref[pl.ds(t, 1), :] = h
        return h

    lax.fori_loop(0, lp, step, jnp.zeros((1, x.shape[1]), F32), unroll=8)
    return lx, lxb, r, i, spl, a, mult, valid


def _lru_specs(lp):
    w = LRU_BLOCK
    return [
        pl.BlockSpec((lp, w), lambda g, b: (b, g)),
        pl.BlockSpec((lp, w), lambda g, b: (b, LRU_BLOCKS + g)),
        pl.BlockSpec((CONV_W, w), lambda g, b: (0, g)),
        pl.BlockSpec((1, w), lambda g, b: (0, g)),
        pl.BlockSpec((1, w, w), lambda g, b: (g, 0, 0)),
        pl.BlockSpec((1, w), lambda g, b: (0, g)),
        pl.BlockSpec((1, w, w), lambda g, b: (g, 0, 0)),
        pl.BlockSpec((1, w), lambda g, b: (0, g)),
        pl.BlockSpec((1, w), lambda g, b: (0, g)),
    ]


def _lru_fwd(proj, cw, cb, wa, ba, wx, bx, lam, bsz, lp):
    w = LRU_BLOCK

    def body(x_ref, g_ref, cw_ref, cb_ref, wa_ref, ba_ref, wx_ref, bx_ref, lam_ref, y_ref, a_s, b_s, h_s):
        _lru_forward_block(x_ref[...], cw_ref[...], cb_ref[...], wa_ref[0], ba_ref[...], wx_ref[0], bx_ref[...],
                           lam_ref[...], a_s, b_s, h_s)
        gate = g_ref[...]
        y_ref[...] = (h_s[...] * gate * _sigmoid(gate)).astype(BF16)

    return pl.pallas_call(
        body, grid=(LRU_BLOCKS, bsz), in_specs=_lru_specs(lp),
        out_specs=pl.BlockSpec((lp, w), lambda g, b: (b, g)),
        out_shape=jax.ShapeDtypeStruct((bsz * lp, LRU_BLOCKS * w), BF16),
        scratch_shapes=[pltpu.VMEM((lp, w), F32)] * 3,
        compiler_params=_cparams("parallel", "arbitrary"), name="lru_fwd")(proj, proj, cw, cb, wa, ba, wx, bx, lam)


def _lru_bwd(proj, dmixed, cw, cb, wa, ba, wx, bx, lam, bsz, lp):
    w = LRU_BLOCK

    def body(x_ref, g_ref, cw_ref, cb_ref, wa_ref, ba_ref, wx_ref, bx_ref, lam_ref, dy_ref,
             dx_ref, dg_ref, dcw_ref, dcb_ref, dwa_ref, dba_ref, dwx_ref, dbx_ref, dlam_ref, a_s, b_s, h_s, dh_s):
        @pl.when(pl.program_id(1) == 0)
        def _():
            for ref in (dcw_ref, dcb_ref, dwa_ref, dba_ref, dwx_ref, dbx_ref, dlam_ref):
                ref[...] = jnp.zeros_like(ref)

        x = x_ref[...]
        cwv = cw_ref[...]
        wav, wxv, lam_ = wa_ref[0], wx_ref[0], lam_ref[...]
        lx, lxb, r, i, spl, a, mult, valid = _lru_forward_block(
            x, cwv, cb_ref[...], wav, ba_ref[...], wxv, bx_ref[...], lam_, a_s, b_s, h_s)
        gate = g_ref[...]
        sg = _sigmoid(gate)
        dy = dy_ref[...]
        h = h_s[...]
        dg_ref[...] = (dy * h * sg * (1.0 + gate * (1.0 - sg))).astype(BF16)
        b_s[...] = dy * gate * sg

        def step(s, c):
            t = lp - 1 - s
            dh = b_s[pl.ds(t, 1), :] + c
            dh_s[pl.ds(t, 1), :] = dh
            return a_s[pl.ds(t, 1), :] * dh

        lax.fori_loop(0, lp, step, jnp.zeros((1, w), F32), unroll=8)
        dh = dh_s[...]
        hprev = jnp.where(_iota((lp, 1), 0) >= 1, _shift_down(h, 1), 0.0)
        db = jnp.where(valid, dh, 0.0)
        dmult = db * i * lx
        di = db * mult * lx
        dlx = db * mult * i
        inv_mult = jnp.where(mult > 0.0, 1.0 / mult, 0.0)
        dlog_a = dh * hprev * a - dmult * (a * a) * inv_mult
        drp = dlog_a * ((-RG_LRU_C) * spl) * r * (1.0 - r)
        dip = di * i * (1.0 - i)
        dspl = jnp.sum(dlog_a * ((-RG_LRU_C) * r), axis=0, keepdims=True)
        dlam_ref[...] += dspl * (-_sigmoid(-lam_))
        dba_ref[...] += jnp.sum(drp, axis=0, keepdims=True)
        dbx_ref[...] += jnp.sum(dip, axis=0, keepdims=True)
        drpb, dipb = drp.astype(BF16), dip.astype(BF16)
        dwa_ref[0] += _dot_tn(lxb, drpb)
        dwx_ref[0] += _dot_tn(lxb, dipb)
        dlx = dlx + _dot_nt(drpb, wav) + _dot_nt(dipb, wxv)
        dcb_ref[...] += jnp.sum(dlx, axis=0, keepdims=True)
        _conv_bwd_w(dcw_ref, dlx, x)
        dx_ref[...] = _conv_bwd_x(dlx, cwv).astype(BF16)

    t = bsz * lp
    vec = pl.BlockSpec((1, w), lambda g, b: (0, g))
    mat = pl.BlockSpec((1, w, w), lambda g, b: (g, 0, 0))
    act = pl.BlockSpec((lp, w), lambda g, b: (b, g))
    return pl.pallas_call(
        body, grid=(LRU_BLOCKS, bsz), in_specs=_lru_specs(lp) + [act],
        out_specs=[act, act, pl.BlockSpec((CONV_W, w), lambda g, b: (0, g)), vec, mat, vec, mat, vec, vec],
        out_shape=[jax.ShapeDtypeStruct((t, 1024), BF16), jax.ShapeDtypeStruct((t, 1024), BF16),
                   jax.ShapeDtypeStruct((CONV_W, 1024), F32), jax.ShapeDtypeStruct((1, 1024), F32),
                   jax.ShapeDtypeStruct((LRU_BLOCKS, w, w), F32), jax.ShapeDtypeStruct((1, 1024), F32),
                   jax.ShapeDtypeStruct((LRU_BLOCKS, w, w), F32), jax.ShapeDtypeStruct((1, 1024), F32),
                   jax.ShapeDtypeStruct((1, 1024), F32)],
        scratch_shapes=[pltpu.VMEM((lp, w), F32)] * 4,
        compiler_params=_cparams("parallel", "arbitrary"), name="lru_bwd")(
            proj, proj, cw, cb, wa, ba, wx, bx, lam, dmixed)


def _sb_masks():
    j = _iota((SB_T, 2 * SB_T), 0)
    s = _iota((SB_T, 2 * SB_T), 1)
    strict = ((s < SB_T) & (j > s)) | (s >= SB_T)
    incl = ((s < SB_T) & (j >= s)) | (s >= SB_T)
    return strict.astype(BF16), incl.astype(BF16)


def _xdot2(a, m01):
    a1 = a.astype(BF16)
    a2 = (a - a1.astype(F32)).astype(BF16)
    return _dot(a1, m01) + _dot(a2, m01)


def _sb_tile(qs, k_ref, qi, kj, carry, m_strict):
    rows_k = pl.ds(pl.multiple_of(kj * SB_T, SB_T), SB_T)
    z = _dot_nt(qs, k_ref[rows_k, :].astype(BF16))
    sp = _softplus(z)
    t_pos = qi * SB_T + _iota((SB_T, SB_T), 0)
    s_pos = kj * SB_T + _iota((SB_T, SB_T), 1)
    valid = (s_pos < t_pos) & (s_pos >= PAD)
    lk = jnp.where(valid, -sp, 0.0)
    rs = _xdot2(lk, m_strict)
    logw = (z - sp) + rs[:, :SB_T] + carry
    wgt = jnp.where(valid, jnp.exp(logw), 0.0)
    return rows_k, z, sp, valid, wgt, rs[:, SB_T:]


def _sb_specs(lp):
    nh = SB_HEADS // 2
    return [pl.BlockSpec((lp, 128), lambda b, p: (b, p)), pl.BlockSpec((lp, 128), lambda b, p: (b, nh + p)),
            pl.BlockSpec((lp, 128), lambda b, p: (b, 2 * nh + p)), pl.BlockSpec((lp, 128), lambda b, p: (b, 3 * nh + p))]


def _sb_fwd(qkvg, bsz, lp):
    nq = lp // SB_T
    scale = SB_D ** -0.5

    def body(q_ref, k_ref, v_ref, g_ref, o_ref, og_ref):
        m_strict, _ = _sb_masks()
        lane = _iota((SB_T, 128), 1)
        for hh in range(2):
            mine = lax.shift_right_logical(lane, 6) == hh

            def q_block(qi, _):
                rows_q = pl.ds(pl.multiple_of(qi * SB_T, SB_T), SB_T)
                qs = jnp.where(mine, q_ref[rows_q, :] * scale, 0.0).astype(BF16)

                def k_block(it, c):
                    acc, carry = c
                    rows_k, _, _, _, wgt, tot = _sb_tile(qs, k_ref, qi, qi - it, carry, m_strict)
                    acc = acc + _dot(wgt.astype(BF16), v_ref[rows_k, :].astype(BF16))
                    return acc, carry + tot

                zero = jnp.zeros((SB_T, SB_T), F32)
                acc, _ = lax.fori_loop(0, qi + 1, k_block, (zero, zero))
                if hh == 0:
                    o_ref[rows_q, :] = acc
                else:
                    o_ref[rows_q, :] = jnp.where(mine, acc, o_ref[rows_q, :])
                return 0

            lax.fori_loop(0, nq, q_block, 0)
        gate = g_ref[...]
        og_ref[...] = (o_ref[...] * gate * _sigmoid(gate)).astype(BF16)

    t = bsz * lp
    blk = pl.BlockSpec((lp, 128), lambda b, p: (b, p))
    return pl.pallas_call(
        body, grid=(bsz, SB_HEADS // 2), in_specs=_sb_specs(lp), out_specs=[blk, blk],
        out_shape=[jax.ShapeDtypeStruct((t, 1024), F32), jax.ShapeDtypeStruct((t, 1024), BF16)],
        compiler_params=_cparams("parallel", "parallel"), name="sb_fwd")(qkvg, qkvg, qkvg, qkvg)


def _sb_bwd(qkvg, o, dog, bsz, lp):
    nq = lp // SB_T
    scale = SB_D ** -0.5

    def body(q_ref, k_ref, v_ref, g_ref, o_ref, dog_ref, dq_ref, dk_ref, dv_ref, dg_ref, do_s, dk_s, dv_s, e_s, sig_s):
        m_strict, _ = _sb_masks()
        j = _iota((SB_T, 2 * SB_T), 0)
        s = _iota((SB_T, 2 * SB_T), 1)
        m_left = (((s < SB_T) & (j < s)) | (s >= SB_T)).astype(BF16)
        lane = _iota((SB_T, 128), 1)
        gate = g_ref[...]
        sg = _sigmoid(gate)
        dog = dog_ref[...]
        do_s[...] = dog * gate * sg
        dg_ref[...] = (dog * o_ref[...] * sg * (1.0 + gate * (1.0 - sg))).astype(BF16)
        dk_s[...] = jnp.zeros_like(dk_s)
        dv_s[...] = jnp.zeros_like(dv_s)
        for hh in range(2):
            mine = lax.shift_right_logical(lane, 6) == hh

            def q_block(qi, _):
                rows_q = pl.ds(pl.multiple_of(qi * SB_T, SB_T), SB_T)
                qs = jnp.where(mine, q_ref[rows_q, :] * scale, 0.0).astype(BF16)
                do_b = jnp.where(mine, do_s[rows_q, :], 0.0).astype(BF16)

                def sweep_left(it, carry):
                    rows_k, z, sp, _, wgt, tot = _sb_tile(qs, k_ref, qi, qi - it, carry, m_strict)
                    e_s[rows_k, :] = wgt * _dot_nt(do_b, v_ref[rows_k, :].astype(BF16))
                    sig_s[rows_k, :] = jnp.exp(z - sp)
                    dv_s[rows_k, :] += _dot_tn(wgt.astype(BF16), do_b)
                    return carry + tot

                zero = jnp.zeros((SB_T, SB_T), F32)
                lax.fori_loop(0, qi + 1, sweep_left, zero)

                def sweep_right(kj, c):
                    dq, ecarry = c
                    rows_k = pl.ds(pl.multiple_of(kj * SB_T, SB_T), SB_T)
                    e = e_s[rows_k, :]
                    se = _xdot2(e, m_left)
                    t_pos = qi * SB_T + _iota((SB_T, SB_T), 0)
                    s_pos = kj * SB_T + _iota((SB_T, SB_T), 1)
                    valid = (s_pos < t_pos) & (s_pos >= PAD)
                    dz = jnp.where(valid, e - sig_s[rows_k, :] * (e + se[:, :SB_T] + ecarry), 0.0).astype(BF16)
                    dq = dq + _dot(dz, k_ref[rows_k, :].astype(BF16))
                    dk_s[rows_k, :] += _dot_tn(dz, qs)
                    return dq, ecarry + se[:, SB_T:]

                dq, _ = lax.fori_loop(0, qi + 1, sweep_right, (zero, zero))
                dq = (dq * scale).astype(BF16)
                if hh == 0:
                    dq_ref[rows_q, :] = dq
                else:
                    dq_ref[rows_q, :] = jnp.where(mine, dq, dq_ref[rows_q, :])
                return 0

            lax.fori_loop(0, nq, q_block, 0)
        dk_ref[...] = dk_s[...].astype(BF16)
        dv_ref[...] = dv_s[...].astype(BF16)

    t = bsz * lp
    blk = pl.BlockSpec((lp, 128), lambda b, p: (b, p))
    shp = jax.ShapeDtypeStruct((t, 1024), BF16)
    return pl.pallas_call(
        body, grid=(bsz, SB_HEADS // 2), in_specs=_sb_specs(lp) + [blk, blk], out_specs=[blk] * 4,
        out_shape=[shp] * 4, scratch_shapes=[pltpu.VMEM((lp, 128), F32)] * 5,
        compiler_params=_cparams("parallel", "parallel"), name="sb_bwd")(qkvg, qkvg, qkvg, qkvg, o, dog)


XBC_COL0 = 3072 // 256
DT_COL0 = 4608 // 128
DT_L = 128


def _ssd_pre_fwd(proj, cw, cb, bsz, lp):
    def body(x_ref, cw_ref, cb_ref, o_ref):
        pre = _conv(x_ref[...], cw_ref[...], cb_ref[...])
        o_ref[...] = pre * _sigmoid(pre)

    return pl.pallas_call(
        body, grid=(bsz, SSD_CONV_DIM // 256),
        in_specs=[pl.BlockSpec((lp, 256), lambda b, j: (b, XBC_COL0 + j)), pl.BlockSpec((CONV_W, 256), lambda b, j: (0, j)),
                  pl.BlockSpec((1, 256), lambda b, j: (0, j))],
        out_specs=pl.BlockSpec((lp, 256), lambda b, j: (b, j)),
        out_shape=jax.ShapeDtypeStruct((bsz * lp, SSD_CONV_DIM), F32),
        compiler_params=_cparams("parallel", "parallel"), name="ssd_pre_fwd")(proj, cw, cb)


def _ssd_pre_bwd(proj, dact, cw, cb, bsz, lp):
    def body(x_ref, d_ref, cw_ref, cb_ref, dx_ref, dcw_ref, dcb_ref):
        @pl.when(pl.program_id(1) == 0)
        def _():
            dcw_ref[...] = jnp.zeros_like(dcw_ref)
            dcb_ref[...] = jnp.zeros_like(dcb_ref)
        x = x_ref[...]
        cwv = cw_ref[...]
        pre = _conv(x, cwv, cb_ref[...])
        s = _sigmoid(pre)
        dpre = d_ref[...].astype(F32) * s * (1.0 + pre * (1.0 - s))
        dcb_ref[...] += jnp.sum(dpre, axis=0, keepdims=True)
        _conv_bwd_w(dcw_ref, dpre, x)
        dx_ref[...] = _conv_bwd_x(dpre, cwv).astype(BF16)

    return pl.pallas_call(
        body, grid=(SSD_CONV_DIM // 256, bsz),
        in_specs=[pl.BlockSpec((lp, 256), lambda j, b: (b, XBC_COL0 + j)), pl.BlockSpec((lp, 256), lambda j, b: (b, j)),
                  pl.BlockSpec((CONV_W, 256), lambda j, b: (0, j)), pl.BlockSpec((1, 256), lambda j, b: (0, j))],
        out_specs=[pl.BlockSpec((lp, 256), lambda j, b: (b, j)), pl.BlockSpec((CONV_W, 256), lambda j, b: (0, j)),
                   pl.BlockSpec((1, 256), lambda j, b: (0, j))],
        out_shape=[jax.ShapeDtypeStruct((bsz * lp, SSD_CONV_DIM), BF16), jax.ShapeDtypeStruct((CONV_W, SSD_CONV_DIM), F32),
                   jax.ShapeDtypeStruct((1, SSD_CONV_DIM), F32)],
        compiler_params=_cparams("parallel", "arbitrary"), name="ssd_pre_bwd")(proj, dact, cw, cb)


def _xdot_nt_r(a, m01):
    a1, a2, a3 = _split3(a)
    return _dot_nt(a1, m01) + _dot_nt(a2, m01) + _dot_nt(a3, m01)


class _SsdConsts:
    def __init__(self, g):
        q, gw = SSD_Q, SSD_GW
        head_of_lane = lax.shift_right_logical(_iota((DT_L, gw), 1), 6)
        self.sel = (_iota((DT_L, gw), 0) == 8 * g + head_of_lane).astype(BF16)
        r = _iota((q, gw), 0)
        c = jnp.bitwise_and(_iota((q, gw), 1), q - 1)
        self.diag = r == c
        self.diag_b = self.diag.astype(BF16)
        self.lower = c <= r
        self.upper = c >= r
        self.bd = lax.shift_right_logical(_iota((gw, gw), 0), 6) == lax.shift_right_logical(_iota((gw, gw), 1), 6)
        jj, ll = _iota((q, q), 1), _iota((q, q), 0)
        self.tri = (jj <= ll).astype(BF16)
        self.tri_t = (jj >= ll).astype(BF16)
        self.ones = jnp.ones((q, q), BF16)
        self.last = _iota((q, 1), 0) == q - 1


def _ssd_prepass(k, dt_ref, bias_ref, alog_ref, dtm_s, acum_s, lp):
    valid = _iota((lp, 1), 0) >= PAD
    sp_in = dt_ref[...] + bias_ref[...]
    dtm = jnp.where(valid, _softplus(sp_in), 0.0)
    aneg = -jnp.exp(alog_ref[...])
    dtm_s[...] = dtm
    acum_s[...] = dtm * aneg

    def cum(c, _):
        rows = pl.ds(pl.multiple_of(c * SSD_Q, SSD_Q), SSD_Q)
        acum_s[rows, :] = _xdot_l(k.tri, acum_s[rows, :])
        return 0

    lax.fori_loop(0, lp // SSD_Q, cum, 0)
    return valid, sp_in, aneg


def _ssd_chunk(k, rows, xs_ref, b_ref, c_ref, dtm_s, acum_s):
    bc = _xdot_r(acum_s[rows, :], k.sel)
    tt = _xdot_l(k.ones, jnp.where(k.diag, bc, 0.0))
    dtbc = _xdot_r(dtm_s[rows, :], k.sel)
    xs = xs_ref[rows, :]
    x = xs * dtbc
    bb = b_ref[rows, :].astype(BF16)
    cc = c_ref[rows, :].astype(BF16)
    tot = bc[SSD_Q - 1:SSD_Q, :]
    xbd = jnp.where(k.bd, jnp.concatenate([x] * 8, axis=0), 0.0).astype(BF16)
    return bc, tt, dtbc, xs, x, bb, cc, tot, xbd


def _ssd_specs(lp, order):
    ix = (lambda f: (lambda b, g: f(b, g))) if order == "bg" else (lambda f: (lambda g, b: f(b, g)))
    return [pl.BlockSpec((lp, SSD_GW), ix(lambda b, g: (b, g))),
            pl.BlockSpec((lp, SSD_N), ix(lambda b, g: (b, 1024 // SSD_N + g))),
            pl.BlockSpec((lp, SSD_N), ix(lambda b, g: (b, 1280 // SSD_N + g))),
            pl.BlockSpec((lp, DT_L), ix(lambda b, g: (b, DT_COL0))),
            pl.BlockSpec((1, DT_L), ix(lambda b, g: (0, 0))),
            pl.BlockSpec((1, DT_L), ix(lambda b, g: (0, 0))),
            pl.BlockSpec((1, DT_L), ix(lambda b, g: (0, 0)))]


def _ssd_fwd(xbc, proj, dt_bias, a_log, dskip, bsz, lp):
    nc = lp // SSD_Q

    def body(xs_ref, b_ref, c_ref, dt_ref, bias_ref, alog_ref, dsk_ref, y_ref, dtm_s, acum_s):
        k = _SsdConsts(pl.program_id(1))
        _ssd_prepass(k, dt_ref, bias_ref, alog_ref, dtm_s, acum_s, lp)
        dvec = _xdot_r(jnp.broadcast_to(dsk_ref[...], (8, DT_L)), k.sel)[0:1, :]

        def chunk(c, state):
            rows = pl.ds(pl.multiple_of(c * SSD_Q, SSD_Q), SSD_Q)
            bc, tt, _, xs, x, bb, cc, tot, xbd = _ssd_chunk(k, rows, xs_ref, b_ref, c_ref, dtm_s, acum_s)
            lm = jnp.where(k.lower, jnp.exp(jnp.minimum(bc - tt, 0.0)), 0.0)
            g_all = _dot(_dot_nt(cc, bb).astype(BF16), k.diag_b) * lm
            y = _dot(g_all.astype(BF16), xbd) + jnp.exp(bc) * _dot(cc, state.astype(BF16)) + dvec * xs
            y_ref[rows, :] = y
            return jnp.exp(tot) * state + _dot_tn(bb, (jnp.exp(tot - bc) * x).astype(BF16))

        lax.fori_loop(0, nc, chunk, jnp.zeros((SSD_N, SSD_GW), F32))

    return pl.pallas_call(
        body, grid=(bsz, SSD_GROUPS), in_specs=_ssd_specs(lp, "bg"),
        out_specs=pl.BlockSpec((lp, SSD_GW), lambda b, g: (b, g)),
        out_shape=jax.ShapeDtypeStruct((bsz * lp, 1024), F32),
        scratch_shapes=[pltpu.VMEM((lp, DT_L), F32)] * 2,
        compiler_params=_cparams("parallel", "parallel"), name="ssd_fwd")(xbc, xbc, xbc, proj, dt_bias, a_log, dskip)


def _ssd_bwd(xbc, proj, dt_bias, a_log, dskip, dy, bsz, lp):
    nc = lp // SSD_Q

    def body(xs_ref, b_ref, c_ref, dt_ref, bias_ref, alog_ref, dsk_ref, dy_ref,
             dxs_ref, db_ref, dc_ref, ddt_ref, dbias_ref, dalog_ref, ddsk_ref, dtm_s, acum_s, st_s):
        @pl.when(pl.program_id(1) == 0)
        def _():
            for ref in (dbias_ref, dalog_ref, ddsk_ref):
                ref[...] = jnp.zeros_like(ref)

        k = _SsdConsts(pl.program_id(0))
        valid, sp_in, aneg = _ssd_prepass(k, dt_ref, bias_ref, alog_ref, dtm_s, acum_s, lp)
        dvec = _xdot_r(jnp.broadcast_to(dsk_ref[...], (8, DT_L)), k.sel)[0:1, :]

        def fwd_chunk(c, state):
            rows = pl.ds(pl.multiple_of(c * SSD_Q, SSD_Q), SSD_Q)
            st_s[c] = state.astype(BF16)
            bc, _, _, _, x, bb, _, tot, _ = _ssd_chunk(k, rows, xs_ref, b_ref, c_ref, dtm_s, acum_s)
            return jnp.exp(tot) * state + _dot_tn(bb, (jnp.exp(tot - bc) * x).astype(BF16))

        lax.fori_loop(0, nc, fwd_chunk, jnp.zeros((SSD_N, SSD_GW), F32))

        def bwd_chunk(i, carry):
            dstate, daneg, ddsk = carry
            c = nc - 1 - i
            rows = pl.ds(pl.multiple_of(c * SSD_Q, SSD_Q), SSD_Q)
            bc, tt, dtbc, xs, x, bb, cc, tot, xbd = _ssd_chunk(k, rows, xs_ref, b_ref, c_ref, dtm_s, acum_s)
            sprev = st_s[c]
            dyc = dy_ref[rows, :]
            dyb = dyc.astype(BF16)
            seg = bc - tt
            lm = jnp.where(k.lower, jnp.exp(jnp.minimum(seg, 0.0)), 0.0)
            lm_t = jnp.where(k.upper, jnp.exp(jnp.minimum(-seg, 0.0)), 0.0)
            cb_all = _dot(_dot_nt(cc, bb).astype(BF16), k.diag_b)
            cbt_all = _dot(_dot_nt(bb, cc).astype(BF16), k.diag_b)
            g_all = cb_all * lm
            dybd = jnp.where(k.bd, jnp.concatenate([dyc] * 8, axis=0), 0.0).astype(BF16)
            dg = _dot_nt(dyb, xbd)
            dx = _dot((cbt_all * lm_t).astype(BF16), dybd)
            hh = dg * g_all
            ea = jnp.exp(bc)
            yo = ea * _dot(cc, sprev)
            dacum = _xdot_nt_r(hh + dyc * yo, k.sel)
            col_h = _xdot_l(k.ones, hh)
            dacum = dacum - _xdot_nt_r(jnp.where(k.diag, col_h, 0.0), k.sel)
            dcb = _dot_nt((dg * lm).astype(BF16), k.diag_b).astype(BF16)
            dcs = (ea * dyc).astype(BF16)
            dstb = dstate.astype(BF16)
            dec = jnp.exp(tot - bc)
            w = dec * x
            dw = _dot(bb, dstb)
            dc_ref[rows, :] = (_dot(dcb, bb) + _dot_nt(dcs, sprev)).astype(BF16)
            db_ref[rows, :] = (_dot_tn(dcb, cc) + _dot_nt(w.astype(BF16), dstb)).astype(BF16)
            dx = dx + dec * dw
            kk = _xdot_nt_r(dw * w, k.sel)
            etot = jnp.exp(tot)
            dtot = _xdot_nt_r(_xdot_l(jnp.ones((8, SSD_N), BF16), dstate * sprev.astype(F32) * etot), k.sel)[0:1, :]
            dtot = dtot + jnp.sum(kk, axis=0, keepdims=True)
            dacum = dacum - kk + jnp.where(k.last, dtot, 0.0)
            da = _xdot_l(k.tri_t, dacum)
            dtm_c = dtm_s[rows, :]
            ddtm = da * aneg + _xdot_nt_r(dx * xs, k.sel)
            vrow = (c * SSD_Q + _iota((SSD_Q, 1), 0)) >= PAD
            ddt_ref[rows, :] = jnp.where(vrow, ddtm * _sigmoid(dt_ref[rows, :] + bias_ref[...]), 0.0)
            dxs_ref[rows, :] = (dx * dtbc + dvec * dyc).astype(BF16)
            daneg = daneg + jnp.sum(da * dtm_c, axis=0, keepdims=True)
            ddsk = ddsk + jnp.sum(_xdot_nt_r(dyc * xs, k.sel), axis=0, keepdims=True)
            dstate = etot * dstate + _dot_tn(cc, dcs)
            return dstate, daneg, ddsk

        zrow = jnp.zeros((1, DT_L), F32)
        _, daneg, ddsk = lax.fori_loop(0, nc, bwd_chunk, (jnp.zeros((SSD_N, SSD_GW), F32), zrow, zrow))
        dbias_ref[...] += jnp.broadcast_to(jnp.sum(ddt_ref[...], axis=0, keepdims=True), (8, DT_L))
        dalog_ref[...] += jnp.broadcast_to(daneg * aneg, (8, DT_L))
        ddsk_ref[...] += jnp.broadcast_to(ddsk, (8, DT_L))

    t = bsz * lp
    par = pl.BlockSpec((8, DT_L), lambda g, b: (g, 0))
    par_shape = jax.ShapeDtypeStruct((8 * SSD_GROUPS, DT_L), F32)
    return pl.pallas_call(
        body, grid=(SSD_GROUPS, bsz),
        in_specs=_ssd_specs(lp, "gb") + [pl.BlockSpec((lp, SSD_GW), lambda g, b: (b, g))],
        out_specs=[pl.BlockSpec((lp, SSD_GW), lambda g, b: (b, g)), pl.BlockSpec((lp, SSD_N), lambda g, b: (b, g)),
                   pl.BlockSpec((lp, SSD_N), lambda g, b: (b, g)), pl.BlockSpec((lp, DT_L), lambda g, b: (b, g)), par, par, par],
        out_shape=[jax.ShapeDtypeStruct((t, 1024), BF16), jax.ShapeDtypeStruct((t, 256), BF16),
                   jax.ShapeDtypeStruct((t, 256), BF16), jax.ShapeDtypeStruct((t, SSD_GROUPS * DT_L), F32),
                   par_shape, par_shape, par_shape],
        scratch_shapes=[pltpu.VMEM((lp, DT_L), F32), pltpu.VMEM((lp, DT_L), F32), pltpu.VMEM((nc, SSD_N, SSD_GW), BF16)],
        compiler_params=_cparams("parallel", "arbitrary"), name="ssd_bwd")(
            xbc, xbc, xbc, proj, dt_bias, a_log, dskip, dy)


Z_COL0 = 2048 // SSD_GW


def _gnorm_fwd(y, proj, w, name="gnorm_fwd"):
    t = y.shape[0]
    tm = _row_tile(t)

    def body(y_ref, z_ref, w_ref, o_ref):
        z = z_ref[...]
        gt = y_ref[...] * z * _sigmoid(z)
        r = lax.rsqrt(jnp.mean(gt * gt, axis=-1, keepdims=True) + EPS)
        o_ref[...] = (gt * r * w_ref[...]).astype(BF16)

    return pl.pallas_call(
        body, grid=(t // tm, SSD_GROUPS),
        in_specs=[pl.BlockSpec((tm, SSD_GW), lambda i, g: (i, g)), pl.BlockSpec((tm, SSD_GW), lambda i, g: (i, Z_COL0 + g)),
                  pl.BlockSpec((1, SSD_GW), lambda i, g: (0, g))],
        out_specs=pl.BlockSpec((tm, SSD_GW), lambda i, g: (i, g)),
        out_shape=jax.ShapeDtypeStruct((t, 1024), BF16),
        compiler_params=_cparams("parallel", "parallel"), name=name)(y, proj, w)


def _gnorm_bwd(y, proj, w, dmixed):
    t = y.shape[0]
    tm = _row_tile(t)

    def body(y_ref, z_ref, w_ref, d_ref, dy_ref, dz_ref, dw_ref):
        @pl.when(pl.program_id(1) == 0)
        def _():
            dw_ref[...] = jnp.zeros_like(dw_ref)
        z, yv, d = z_ref[...], y_ref[...], d_ref[...]
        s = _sigmoid(z)
        sz = z * s
        gt = yv * sz
        r = lax.rsqrt(jnp.mean(gt * gt, axis=-1, keepdims=True) + EPS)
        gh = gt * r
        dgn = d * w_ref[...]
        dgt = r * (dgn - gh * jnp.mean(dgn * gh, axis=-1, keepdims=True))
        dw_ref[...] += jnp.sum(d * gh, axis=0, keepdims=True)
        dy_ref[...] = dgt * sz
        dz_ref[...] = (dgt * yv * s * (1.0 + z * (1.0 - s))).astype(BF16)

    blk = pl.BlockSpec((tm, SSD_GW), lambda g, i: (i, g))
    return pl.pallas_call(
        body, grid=(SSD_GROUPS, t // tm),
        in_specs=[blk, pl.BlockSpec((tm, SSD_GW), lambda g, i: (i, Z_COL0 + g)), pl.BlockSpec((1, SSD_GW), lambda g, i: (0, g)),
                  pl.BlockSpec((tm, SSD_GW), lambda g, i: (i, 1024 // SSD_GW + g))],
        out_specs=[blk, blk, pl.BlockSpec((1, SSD_GW), lambda g, i: (0, g))],
        out_shape=[jax.ShapeDtypeStruct((t, 1024), F32), jax.ShapeDtypeStruct((t, 1024), BF16),
                   jax.ShapeDtypeStruct((1, 1024), F32)],
        compiler_params=_cparams("parallel", "arbitrary"), name="gnorm_bwd")(y, proj, w, dmixed)


def _meta_grad(dh0, bsz, lp):
    def body(d_ref, o_ref):
        @pl.when(pl.program_id(0) == 0)
        def _():
            o_ref[...] = jnp.zeros_like(o_ref)
        o_ref[...] += d_ref[...]

    return pl.pallas_call(
        body, grid=(bsz,),
        in_specs=[pl.BlockSpec((N_META, D_MODEL), lambda b: (b * (lp // N_META) + PAD // N_META, 0))],
        out_specs=pl.BlockSpec((N_META, D_MODEL), lambda b: (0, 0)),
        out_shape=jax.ShapeDtypeStruct((N_META, D_MODEL), F32),
        compiler_params=_cparams("arbitrary"), name="meta_grad")(dh0)


def _exchange(gather, scatter, name):
    ng, ns = len(gather), len(scatter)
    n = ng + ns

    def body(*refs):
        ins, outs = refs[:n], refs[n:2 * n]
        send_sems, recv_sems, local_sems = refs[2 * n:]
        x, y, c = lax.axis_index("x"), lax.axis_index("y"), lax.axis_index("c")
        me = 4 * x + 2 * y + c

        def peer(k):
            px = 1 - x if k & 4 else x
            py = 1 - y if k & 2 else y
            pc = 1 - c if k & 1 else c
            return (px, py, pc), 4 * px + 2 * py + pc

        local, sends, recvs = [], [], []
        for a in range(n):
            is_gather = a < ng
            src_own = ins[a] if is_gather else ins[a].at[me]
            local.append(pltpu.make_async_copy(src_own, outs[a].at[me], local_sems.at[a]))
            for k in range(1, N_DEV):
                dev, pid = peer(k)
                src = ins[a] if is_gather else ins[a].at[pid]
                sends.append(pltpu.make_async_remote_copy(
                    src_ref=src, dst_ref=outs[a].at[me], send_sem=send_sems.at[a, k - 1], recv_sem=recv_sems.at[a, k - 1],
                    device_id=dev, device_id_type=pl.DeviceIdType.MESH))
                recvs.append(pltpu.make_async_remote_copy(
                    src_ref=src, dst_ref=outs[a].at[pid], send_sem=send_sems.at[a, k - 1], recv_sem=recv_sems.at[a, k - 1],
                    device_id=dev, device_id_type=pl.DeviceIdType.MESH))
        for cp in local + sends:
            cp.start()
        for cp in recvs:
            cp.wait_recv()
        for cp in sends:
            cp.wait_send()
        for cp in local:
            cp.wait()

    anyspec = pl.BlockSpec(memory_space=pl.ANY)
    out_shape = [jax.ShapeDtypeStruct((N_DEV,) + a.shape, a.dtype) for a in gather]
    out_shape += [jax.ShapeDtypeStruct(a.shape, a.dtype) for a in scatter]
    return pl.pallas_call(
        body, in_specs=[anyspec] * n, out_specs=[anyspec] * n, out_shape=out_shape,
        scratch_shapes=[pltpu.SemaphoreType.DMA((n, N_DEV - 1)), pltpu.SemaphoreType.DMA((n, N_DEV - 1)),
                        pltpu.SemaphoreType.DMA((n,))],
        compiler_params=pltpu.CompilerParams(has_side_effects=True), name=name)(*gather, *scatter)


def _adamw(parts, w, m, v, name):
    r, c = w.shape
    tr = r
    for cand in (128, 64, 32, 16, 8):
        if r % cand == 0 and r > cand:
            tr = cand
            break

    def body(p_ref, w_ref, m_ref, v_ref, g_ref, d_ref, nm_ref, nv_ref):
        g = p_ref[0]
        for k in range(1, N_DEV):
            g = g + p_ref[k]
        m_new = ADAM_B1 * m_ref[...] + (1.0 - ADAM_B1) * g
        v_new = ADAM_B2 * v_ref[...] + (1.0 - ADAM_B2) * (g * g)
        m_hat = m_new / (1.0 - ADAM_B1 ** ADAM_STEP)
        v_hat = v_new / (1.0 - ADAM_B2 ** ADAM_STEP)
        g_ref[...] = g
        d_ref[...] = -ADAM_LR * (m_hat / (jnp.sqrt(v_hat) + ADAM_EPS) + ADAM_WD * w_ref[...])
        nm_ref[...] = m_new
        nv_ref[...] = v_new

    blk = pl.BlockSpec((tr, c), lambda i: (i, 0))
    shp = jax.ShapeDtypeStruct((r, c), F32)
    return pl.pallas_call(
        body, grid=(r // tr,), in_specs=[pl.BlockSpec((N_DEV, tr, c), lambda i: (0, i, 0)), blk, blk, blk],
        out_specs=[blk] * 4, out_shape=[shp] * 4, compiler_params=_cparams("parallel"), name=name)(parts, w, m, v)


def _rows128(a):
    return a.reshape(-1, 128)


def _pad_rows(a, rows):
    return jnp.pad(a, ((0, rows - a.shape[0]), (0, 0)))


def _lane16(a):
    return jnp.pad(a.reshape(1, -1), ((0, 0), (0, 128 - a.size)))


SHARD_PACK_ROWS = 544
REPL_PACK_ROWS = 72


def _pack_shard(meta, lru_conv_w, odd_norm, ssd_conv_w, lru_w_a, lru_w_x):
    parts = [meta.reshape(16, 128), lru_conv_w.reshape(4, 128), odd_norm.reshape(1, 128), _rows128(ssd_conv_w.reshape(4, 192)),
             _rows128(lru_w_a.reshape(4, 32, 256)), _rows128(lru_w_x.reshape(4, 32, 256))]
    return _pad_rows(jnp.concatenate(parts, axis=0), SHARD_PACK_ROWS)


def _unpack_shard(p):
    return (p[0:16], p[16:20].reshape(1, 4, 128), p[20:21], p[21:27].reshape(1, 4, 192),
            p[27:283].reshape(1, 4, 32, 256), p[283:539].reshape(1, 4, 32, 256))


def _pack_repl(even_norm, lru_conv_b, lru_b_a, lru_b_x, lru_lambda, ssd_norm, final_norm, ssd_conv_b, dt_bias, a_log, ssd_d):
    parts = [_rows128(v) for v in (even_norm, lru_conv_b, lru_b_a, lru_b_x, lru_lambda, ssd_norm, final_norm, ssd_conv_b)]
    parts += [_lane16(dt_bias), _lane16(a_log), _lane16(ssd_d)]
    return _pad_rows(jnp.concatenate(parts, axis=0), REPL_PACK_ROWS)


def _unpack_repl(p):
    vec = lambda i: p[8 * i:8 * i + 8].reshape(1, 1024)
    return (vec(0), vec(1), vec(2), vec(3), vec(4), vec(5), p[48:56].reshape(1024), p[56:68].reshape(1, 1536),
            p[68:69, :16], p[69:70, :16], p[70:71, :16])


def _local_step(x, tgt, meta, even_norm, w_in_e, lru_conv_w, lru_conv_b, lru_w_a, lru_b_a, lru_w_x, lru_b_x, lru_lambda,
                ssd_conv_w, ssd_conv_b, dt_bias, a_log, ssd_d, ssd_norm, w_out_e, odd_norm, w_in_o, w_out_o, final_norm):
    bsz, seq, d = x.shape
    lp = PAD + N_META + seq
    t = bsz * lp
    h0 = jnp.concatenate([jnp.zeros((bsz, PAD, d), F32), jnp.broadcast_to(meta[None], (bsz, N_META, d)), x], axis=1).reshape(t, d)

    u0 = _norm_fwd(h0, even_norm, "norm0_fwd")
    proj = _mm_nn(u0, w_in_e, F32, "even_in")
    lru = (lru_conv_w, lru_conv_b, lru_w_a, lru_b_a, lru_w_x, lru_b_x, lru_lambda)
    ya = _lru_fwd(proj, *lru, bsz, lp)
    xbc = _ssd_pre_fwd(proj, ssd_conv_w, ssd_conv_b, bsz, lp)
    y = _ssd_fwd(xbc, proj, dt_bias, a_log, ssd_d, bsz, lp)
    yb = _gnorm_fwd(y, proj, ssd_norm)
    mixed = jnp.concatenate([ya, yb], axis=1)
    h1 = _mm_nn(mixed, w_out_e, F32, "even_out", res=h0)
    u2 = _norm_fwd(h1, odd_norm, "norm1_fwd")
    qkvg = _mm_nn(u2, w_in_o, F32, "odd_in")
    o, og = _sb_fwd(qkvg, bsz, lp)
    h2 = _mm_nn(og, w_out_o, F32, "odd_out", res=h1)
    loss, dh2, d_final_norm = _final_loss(h2, final_norm, tgt, lp)

    dog = _mm_nt(dh2, w_out_o, F32, "odd_out_dx")
    d_w_out_o = _mm_tn(og, dh2, "odd_out_dw")
    dqkvg = jnp.concatenate(_sb_bwd(qkvg, o, dog, bsz, lp), axis=1)
    du2 = _mm_nt(dqkvg, w_in_o, F32, "odd_in_dx")
    d_w_in_o = _mm_tn(u2, dqkvg, "odd_in_dw")
    dh1, d_odd_norm = _norm_bwd(h1, odd_norm, du2, dh2, "norm1_bwd")

    dmixed = _mm_nt(dh1, w_out_e, F32, "even_out_dx")
    d_w_out_e = _mm_tn(mixed, dh1, "even_out_dw")
    dlx, dgate, d_lru_conv_w, d_lru_conv_b, d_w_a, d_b_a, d_w_x, d_b_x, d_lambda = _lru_bwd(proj, dmixed, *lru, bsz, lp)
    dy, dz, d_ssd_norm = _gnorm_bwd(y, proj, ssd_norm, dmixed)
    dxs, dbm, dcm, ddt, d_dt_bias, d_a_log, d_ssd_d = _ssd_bwd(xbc, proj, dt_bias, a_log, ssd_d, dy, bsz, lp)
    dxbc, d_ssd_conv_w, d_ssd_conv_b = _ssd_pre_bwd(proj, jnp.concatenate([dxs, dbm, dcm], axis=1), ssd_conv_w, ssd_conv_b, bsz, lp)
    ddt = (ddt[:, :DT_L] + ddt[:, DT_L:]).astype(BF16)
    dproj = jnp.concatenate([dlx, dgate, dz, dxbc, ddt, jnp.zeros((t, EVEN_NP - 4608 - DT_L), BF16)], axis=1)
    du0 = _mm_nt(dproj, w_in_e, F32, "even_in_dx")
    d_w_in_e = _mm_tn(u0, dproj, "even_in_dw")
    dh0, d_even_norm = _norm_bwd(h0, even_norm, du0, dh1, "norm0_bwd")
    grad_x = dh0.reshape(bsz, lp, d)[:, PAD + N_META:]
    d_meta = _meta_grad(dh0, bsz, lp)
    heads = lambda p: (p[0:1] + p[8:9])[:, :SSD_HEADS]
    grads = dict(meta=d_meta, even_norm=d_even_norm, even_w_in=d_w_in_e[:, :EVEN_IN], lru_conv_w=d_lru_conv_w,
                 lru_conv_b=d_lru_conv_b, lru_w_a=d_w_a, lru_b_a=d_b_a, lru_w_x=d_w_x, lru_b_x=d_b_x, lru_lambda=d_lambda,
                 ssd_conv_w=d_ssd_conv_w, ssd_conv_b=d_ssd_conv_b, ssd_dt_bias=heads(d_dt_bias), ssd_a_log=heads(d_a_log),
                 ssd_d=heads(d_ssd_d), ssd_norm=d_ssd_norm, even_w_out=d_w_out_e, odd_norm=d_odd_norm, odd_w_in=d_w_in_o,
                 odd_w_out=d_w_out_o, final_norm=d_final_norm)
    return loss[0, 0], grad_x, grads


WEIGHTS = ['meta', 'even_norm', 'even_w_in', 'lru_conv_w', 'lru_conv_b', 'lru_w_a', 'lru_b_a', 'lru_w_x', 'lru_b_x', 'lru_lambda',
           'ssd_conv_w', 'ssd_conv_b', 'ssd_dt_bias', 'ssd_a_log', 'ssd_d', 'ssd_norm', 'even_w_out', 'odd_norm', 'odd_w_in',
           'odd_w_out', 'final_norm']


def _blocks_of(a, axis):
    shp = a.shape
    a = a.reshape(shp[:axis] + (N_DEV, shp[axis] // N_DEV) + shp[axis + 1:])
    return jnp.moveaxis(a, axis, 0)


def _unblock(a, axis):
    a = jnp.moveaxis(a, 0, axis)
    shp = a.shape
    return a.reshape(shp[:axis] + (shp[axis] * shp[axis + 1],) + shp[axis + 2:])


def kernel(x, meta, even_norm, even_w_in, lru_conv_w, lru_conv_b, lru_w_a, lru_b_a, lru_w_x, lru_b_x, lru_lambda, ssd_conv_w, ssd_conv_b, ssd_dt_bias, ssd_a_log, ssd_d, ssd_norm, even_w_out, odd_norm, odd_w_in, odd_w_out, final_norm, loss_target, m_meta, m_even_norm, m_even_w_in, m_lru_conv_w, m_lru_conv_b, m_lru_w_a, m_lru_b_a, m_lru_w_x, m_lru_b_x, m_lru_lambda, m_ssd_conv_w, m_ssd_conv_b, m_ssd_dt_bias, m_ssd_a_log, m_ssd_d, m_ssd_norm, m_even_w_out, m_odd_norm, m_odd_w_in, m_odd_w_out, m_final_norm, v_meta, v_even_norm, v_even_w_in, v_lru_conv_w, v_lru_conv_b, v_lru_w_a, v_lru_b_a, v_lru_w_x, v_lru_b_x, v_lru_lambda, v_ssd_conv_w, v_ssd_conv_b, v_ssd_dt_bias, v_ssd_a_log, v_ssd_d, v_ssd_norm, v_even_w_out, v_odd_norm, v_odd_w_in, v_odd_w_out, v_final_norm):
    w = dict(meta=meta, even_norm=even_norm, even_w_in=even_w_in, lru_conv_w=lru_conv_w, lru_conv_b=lru_conv_b, lru_w_a=lru_w_a,
             lru_b_a=lru_b_a, lru_w_x=lru_w_x, lru_b_x=lru_b_x, lru_lambda=lru_lambda, ssd_conv_w=ssd_conv_w,
             ssd_conv_b=ssd_conv_b, ssd_dt_bias=ssd_dt_bias, ssd_a_log=ssd_a_log, ssd_d=ssd_d, ssd_norm=ssd_norm,
             even_w_out=even_w_out, odd_norm=odd_norm, odd_w_in=odd_w_in, odd_w_out=odd_w_out, final_norm=final_norm)
    m = dict(meta=m_meta, even_norm=m_even_norm, even_w_in=m_even_w_in, lru_conv_w=m_lru_conv_w, lru_conv_b=m_lru_conv_b,
             lru_w_a=m_lru_w_a, lru_b_a=m_lru_b_a, lru_w_x=m_lru_w_x, lru_b_x=m_lru_b_x, lru_lambda=m_lru_lambda,
             ssd_conv_w=m_ssd_conv_w, ssd_conv_b=m_ssd_conv_b, ssd_dt_bias=m_ssd_dt_bias, ssd_a_log=m_ssd_a_log, ssd_d=m_ssd_d,
             ssd_norm=m_ssd_norm, even_w_out=m_even_w_out, odd_norm=m_odd_norm, odd_w_in=m_odd_w_in, odd_w_out=m_odd_w_out,
             final_norm=m_final_norm)
    v = dict(meta=v_meta, even_norm=v_even_norm, even_w_in=v_even_w_in, lru_conv_w=v_lru_conv_w, lru_conv_b=v_lru_conv_b,
             lru_w_a=v_lru_w_a, lru_b_a=v_lru_b_a, lru_w_x=v_lru_w_x, lru_b_x=v_lru_b_x, lru_lambda=v_lru_lambda,
             ssd_conv_w=v_ssd_conv_w, ssd_conv_b=v_ssd_conv_b, ssd_dt_bias=v_ssd_dt_bias, ssd_a_log=v_ssd_a_log, ssd_d=v_ssd_d,
             ssd_norm=v_ssd_norm, even_w_out=v_even_w_out, odd_norm=v_odd_norm, odd_w_in=v_odd_w_in, odd_w_out=v_odd_w_out,
             final_norm=v_final_norm)
    shard_names = ('meta', 'lru_conv_w', 'odd_norm', 'ssd_conv_w', 'lru_w_a', 'lru_w_x')
    repl_names = ('even_norm', 'lru_conv_b', 'lru_b_a', 'lru_b_x', 'lru_lambda', 'ssd_norm', 'final_norm', 'ssd_conv_b',
                  'ssd_dt_bias', 'ssd_a_log', 'ssd_d')
    big_names = ('even_w_in', 'even_w_out', 'odd_w_in', 'odd_w_out')

    gates = jnp.concatenate([lru_w_a.reshape(128, 256), lru_w_x.reshape(128, 256)], axis=0).astype(BF16)
    small = _pack_shard(*[w[k] for k in shard_names])
    g_in_e, g_out_e, g_in_o, g_out_o, g_gates, g_small = _exchange(
        [even_w_in[0].astype(BF16), even_w_out[0].astype(BF16), odd_w_in[0].astype(BF16), odd_w_out[0].astype(BF16), gates, small],
        [], "gather_weights")
    w_in_e = jnp.pad(_unblock(g_in_e, 1), ((0, 0), (0, EVEN_NP - EVEN_IN)))
    w_out_e = g_out_e.reshape(2048, 1024)
    w_in_o = _unblock(g_in_o, 1)
    w_out_o = g_out_o.reshape(1024, 1024)
    gates_full = jnp.moveaxis(g_gates.reshape(N_DEV, 2, 4, 32, 256), 0, 2).reshape(2, 4, 256, 256)
    f_meta = _unblock(g_small[:, 0:16], 1)
    f_lru_conv_w = _unblock(g_small[:, 16:20], 1)
    f_odd_norm = _unblock(g_small[:, 20:21], 1)
    f_ssd_conv_w = _unblock(g_small[:, 21:27].reshape(N_DEV, 4, 192), 1)

    loss, grad_x, g = _local_step(
        x, loss_target, f_meta, even_norm, w_in_e, f_lru_conv_w, lru_conv_b, gates_full[0], lru_b_a, gates_full[1], lru_b_x,
        lru_lambda, f_ssd_conv_w, ssd_conv_b, _lane16(ssd_dt_bias), _lane16(ssd_a_log), _lane16(ssd_d), ssd_norm, w_out_e,
        f_odd_norm, w_in_o, w_out_o, final_norm.reshape(1, -1))
    loss = lax.psum(loss, ("x", "y", "c"))

    s_small = jnp.stack([_pack_shard(g['meta'][:, 128 * p:128 * (p + 1)], g['lru_conv_w'][:, 128 * p:128 * (p + 1)],
                                     g['odd_norm'][:, 128 * p:128 * (p + 1)], g['ssd_conv_w'][:, 192 * p:192 * (p + 1)],
                                     g['lru_w_a'][:, 32 * p:32 * (p + 1)], g['lru_w_x'][:, 32 * p:32 * (p + 1)])
                         for p in range(N_DEV)])
    r_pack = _pack_repl(*[g[k] for k in repl_names])
    p_repl, p_in_e, p_out_e, p_in_o, p_out_o, p_small = _exchange(
        [r_pack],
        [_blocks_of(g['even_w_in'], 1), g['even_w_out'].reshape(N_DEV, 256, 1024), _blocks_of(g['odd_w_in'], 1),
         g['odd_w_out'].reshape(N_DEV, 128, 1024), s_small], "reduce_grads")

    res = {}
    for k, parts in zip(big_names, (p_in_e, p_out_e, p_in_o, p_out_o)):
        outs = _adamw(parts, w[k][0], m[k][0], v[k][0], "adamw_" + k)
        res[k] = [o[None] for o in outs]
    outs = _adamw(p_small, small, _pack_shard(*[m[k] for k in shard_names]), _pack_shard(*[v[k] for k in shard_names]), "adamw_sharded")
    unpacked = [_unpack_shard(o) for o in outs]
    for i, k in enumerate(shard_names):
        res[k] = [u[i] for u in unpacked]
    outs = _adamw(p_repl, _pack_repl(*[w[k] for k in repl_names]), _pack_repl(*[m[k] for k in repl_names]),
                  _pack_repl(*[v[k] for k in repl_names]), "adamw_replicated")
    unpacked = [_unpack_repl(o) for o in outs]
    for i, k in enumerate(repl_names):
        res[k] = [u[i] for u in unpacked]
    return (loss, grad_x, *[res[k][0] for k in WEIGHTS], *[res[k][1] for k in WEIGHTS], *[res[k][2] for k in WEIGHTS],
            *[res[k][3] for k in WEIGHTS])
```

```python
import functools

import jax
import jax.numpy as jnp
from jax import lax
from jax.experimental import pallas as pl
from jax.experimental.pallas import tpu as pltpu

F32 = jnp.float32
BF16 = jnp.bfloat16

D_MODEL = 1024
N_META = 16
PAD = 112
EPS = 1e-6
CONV_W = 4
LRU_BLOCKS = 4
LRU_BLOCK = 256
RG_LRU_C = 8.0
SSD_HEADS = 16
SSD_P = 64
SSD_N = 128
SSD_Q = 64
SSD_GROUPS = 2
SSD_GW = 512
SSD_CONV_DIM = 1536
SB_HEADS = 16
SB_D = 64
SB_T = 128
EVEN_IN = 4624
EVEN_NP = 4864
DT_W = 256
ODD_IN = 4096
N_DEV = 8

ADAM_LR = 0.001
ADAM_B1 = 0.9
ADAM_B2 = 0.999
ADAM_EPS = 1e-08
ADAM_WD = 0.01
ADAM_STEP = 10

VMEM_LIMIT = 56 * 1024 * 1024


def _cparams(*sem):
    return pltpu.CompilerParams(dimension_semantics=sem, vmem_limit_bytes=VMEM_LIMIT)


def _row_tile(t):
    for c in (512, 256, 128):
        if t % c == 0:
            return c
    raise ValueError(t)


def _col_tile(n):
    for c in (512, 256, 128):
        if n % c == 0:
            return c
    raise ValueError(n)


def _dot(a, b):
    return jnp.dot(a, b, preferred_element_type=F32)


def _dot_nt(a, b):
    return lax.dot_general(a, b, (((1,), (1,)), ((), ())), preferred_element_type=F32)


def _dot_tn(a, b):
    return lax.dot_general(a, b, (((0,), (0,)), ((), ())), preferred_element_type=F32)


def _split3(a):
    a1 = a.astype(BF16)
    r1 = a - a1.astype(F32)
    a2 = r1.astype(BF16)
    a3 = (r1 - a2.astype(F32)).astype(BF16)
    return a1, a2, a3


def _xdot_r(a, m01):
    a1, a2, a3 = _split3(a)
    return _dot(a1, m01) + _dot(a2, m01) + _dot(a3, m01)


def _xdot_l(m01, a):
    a1, a2, a3 = _split3(a)
    return _dot(m01, a1) + _dot(m01, a2) + _dot(m01, a3)


def _xdot_tn_l(m01, a):
    a1, a2, a3 = _split3(a)
    return _dot_tn(m01, a1) + _dot_tn(m01, a2) + _dot_tn(m01, a3)


def _sigmoid(x):
    return 1.0 / (1.0 + jnp.exp(-x))


def _softplus(x):
    return jnp.maximum(x, 0.0) + jnp.log(1.0 + jnp.exp(-jnp.abs(x)))


def _expm1(x):
    p = x * (1.0 + x * (1.0 / 2) * (1.0 + x * (1.0 / 3) * (1.0 + x * (1.0 / 4) * (1.0 + x * (1.0 / 5) * (1.0 + x * (1.0 / 6))))))
    return jnp.where(jnp.abs(x) < 0.25, p, jnp.exp(x) - 1.0)


def _iota(shape, dim):
    return lax.broadcasted_iota(jnp.int32, shape, dim)


def _mm_nn(a, b, out_dtype, name, res=None):
    m, k = a.shape
    _, n = b.shape
    tm, tn = _row_tile(m), _col_tile(n)

    def body(*refs):
        if res is None:
            a_ref, b_ref, o_ref = refs
        else:
            a_ref, b_ref, r_ref, o_ref = refs
        acc = _dot(a_ref[...].astype(BF16), b_ref[...].astype(BF16))
        if res is not None:
            acc = acc + r_ref[...]
        o_ref[...] = acc.astype(out_dtype)

    in_specs = [pl.BlockSpec((tm, k), lambda i, j: (i, 0)), pl.BlockSpec((k, tn), lambda i, j: (0, j))]
    args = [a, b]
    if res is not None:
        in_specs.append(pl.BlockSpec((tm, tn), lambda i, j: (i, j)))
        args.append(res)
    return pl.pallas_call(
        body, grid=(m // tm, n // tn), in_specs=in_specs,
        out_specs=pl.BlockSpec((tm, tn), lambda i, j: (i, j)),
        out_shape=jax.ShapeDtypeStruct((m, n), out_dtype),
        compiler_params=_cparams("parallel", "parallel"), name=name)(*args)


def _mm_nt(a, b, out_dtype, name):
    m, k = a.shape
    n, _ = b.shape
    tm, tn = _row_tile(m), _col_tile(n)

    def body(a_ref, b_ref, o_ref):
        o_ref[...] = _dot_nt(a_ref[...].astype(BF16), b_ref[...].astype(BF16)).astype(out_dtype)

    return pl.pallas_call(
        body, grid=(m // tm, n // tn),
        in_specs=[pl.BlockSpec((tm, k), lambda i, j: (i, 0)), pl.BlockSpec((tn, k), lambda i, j: (j, 0))],
        out_specs=pl.BlockSpec((tm, tn), lambda i, j: (i, j)),
        out_shape=jax.ShapeDtypeStruct((m, n), out_dtype),
        compiler_params=_cparams("parallel", "parallel"), name=name)(a, b)


def _mm_tn(a, b, name):
    t, m = a.shape
    _, n = b.shape
    tk, tm, tn = _row_tile(t), _col_tile(m), _col_tile(n)
    if m % 1024 == 0:
        tm = 1024

    def body(a_ref, b_ref, o_ref):
        @pl.when(pl.program_id(2) == 0)
        def _():
            o_ref[...] = jnp.zeros_like(o_ref)
        o_ref[...] += _dot_tn(a_ref[...].astype(BF16), b_ref[...].astype(BF16))

    return pl.pallas_call(
        body, grid=(m // tm, n // tn, t // tk),
        in_specs=[pl.BlockSpec((tk, tm), lambda i, j, s: (s, i)), pl.BlockSpec((tk, tn), lambda i, j, s: (s, j))],
        out_specs=pl.BlockSpec((tm, tn), lambda i, j, s: (i, j)),
        out_shape=jax.ShapeDtypeStruct((m, n), F32),
        compiler_params=_cparams("parallel", "parallel", "arbitrary"), name=name)(a, b)


def _norm_fwd(h, w, name):
    t, d = h.shape
    tm = _row_tile(t)

    def body(h_ref, w_ref, u_ref):
        x = h_ref[...]
        r = lax.rsqrt(jnp.mean(x * x, axis=-1, keepdims=True) + EPS)
        u_ref[...] = (x * r * w_ref[...]).astype(BF16)

    return pl.pallas_call(
        body, grid=(t // tm,),
        in_specs=[pl.BlockSpec((tm, d), lambda i: (i, 0)), pl.BlockSpec((1, d), lambda i: (0, 0))],
        out_specs=pl.BlockSpec((tm, d), lambda i: (i, 0)),
        out_shape=jax.ShapeDtypeStruct((t, d), BF16),
        compiler_params=_cparams("parallel"), name=name)(h, w)


def _norm_bwd(h, w, du, dres, name):
    t, d = h.shape
    tm = _row_tile(t)

    def body(h_ref, w_ref, du_ref, dr_ref, dh_ref, dw_ref):
        @pl.when(pl.program_id(0) == 0)
        def _():
            dw_ref[...] = jnp.zeros_like(dw_ref)
        x = h_ref[...]
        r = lax.rsqrt(jnp.mean(x * x, axis=-1, keepdims=True) + EPS)
        xh = x * r
        du_ = du_ref[...]
        g = du_ * w_ref[...]
        dh_ref[...] = dr_ref[...] + r * (g - xh * jnp.mean(g * xh, axis=-1, keepdims=True))
        dw_ref[...] += jnp.sum(du_ * xh, axis=0, keepdims=True)

    return pl.pallas_call(
        body, grid=(t // tm,),
        in_specs=[pl.BlockSpec((tm, d), lambda i: (i, 0)), pl.BlockSpec((1, d), lambda i: (0, 0)),
                  pl.BlockSpec((tm, d), lambda i: (i, 0)), pl.BlockSpec((tm, d), lambda i: (i, 0))],
        out_specs=[pl.BlockSpec((tm, d), lambda i: (i, 0)), pl.BlockSpec((1, d), lambda i: (0, 0))],
        out_shape=[jax.ShapeDtypeStruct((t, d), F32), jax.ShapeDtypeStruct((1, d), F32)],
        compiler_params=_cparams("arbitrary"), name=name)(h, w, du, dres)


def _final_loss(h2, w, tgt, lp):
    t, d = h2.shape
    bsz, seq, _ = tgt.shape
    nblk = lp // SB_T

    def body(h_ref, w_ref, t_ref, loss_ref, dh_ref, dw_ref):
        b, i = pl.program_id(0), pl.program_id(1)

        @pl.when((b == 0) & (i == 0))
        def _():
            loss_ref[...] = jnp.zeros_like(loss_ref)
            dw_ref[...] = jnp.zeros_like(dw_ref)

        @pl.when(i == 0)
        def _():
            dh_ref[...] = jnp.zeros_like(dh_ref)

        @pl.when(i > 0)
        def _():
            x = h_ref[...]
            r = lax.rsqrt(jnp.mean(x * x, axis=-1, keepdims=True) + EPS)
            xh = x * r
            wv = w_ref[...]
            diff = xh * wv - t_ref[0]
            loss_ref[...] += 0.5 * jnp.sum(jnp.mean(diff * diff, axis=-1, keepdims=True), axis=0, keepdims=True)
            dy = diff * (1.0 / d)
            g = dy * wv
            dh_ref[...] = r * (g - xh * jnp.mean(g * xh, axis=-1, keepdims=True))
            dw_ref[...] += jnp.sum(dy * xh, axis=0, keepdims=True)

    return pl.pallas_call(
        body, grid=(bsz, nblk),
        in_specs=[pl.BlockSpec((SB_T, d), lambda b, i: (b * nblk + i, 0)), pl.BlockSpec((1, d), lambda b, i: (0, 0)),
                  pl.BlockSpec((1, SB_T, d), lambda b, i: (b, jnp.maximum(i - 1, 0), 0))],
        out_specs=[pl.BlockSpec((1, 128), lambda b, i: (0, 0)), pl.BlockSpec((SB_T, d), lambda b, i: (b * nblk + i, 0)),
                   pl.BlockSpec((1, d), lambda b, i: (0, 0))],
        out_shape=[jax.ShapeDtypeStruct((1, 128), F32), jax.ShapeDtypeStruct((t, d), F32), jax.ShapeDtypeStruct((1, d), F32)],
        compiler_params=_cparams("arbitrary", "arbitrary"), name="final_loss")(h2, w, tgt)


def _shift_down(x, j):
    return x if j == 0 else pltpu.roll(x, j, 0)


def _shift_up(x, j):
    return x if j == 0 else pltpu.roll(x, x.shape[0] - j, 0)


def _conv(x, cw, cb):
    out = cb + cw[CONV_W - 1:CONV_W, :] * x
    for j in range(1, CONV_W):
        out = out + cw[CONV_W - 1 - j:CONV_W - j, :] * _shift_down(x, j)
    return out


def _conv_bwd_x(dy, cw):
    out = cw[CONV_W - 1:CONV_W, :] * dy
    for j in range(1, CONV_W):
        out = out + cw[CONV_W - 1 - j:CONV_W - j, :] * _shift_up(dy, j)
    return out


def _conv_bwd_w(dcw_ref, dy, x):
    for j in range(CONV_W):
        k = CONV_W - 1 - j
        dcw_ref[k:k + 1, :] += jnp.sum(dy * _shift_down(x, j), axis=0, keepdims=True)


def _lru_forward_block(x, cw, cb, wa, ba, wx, bx, lam, a_ref, b_ref, h_ref):
    lp = x.shape[0]
    lx = _conv(x, cw, cb)
    lxb = lx.astype(BF16)
    r = _sigmoid(_dot(lxb, wa) + ba)
    i = _sigmoid(_dot(lxb, wx) + bx)
    spl = _softplus(-lam)
    log_a = (-RG_LRU_C) * r * spl
    a = jnp.exp(log_a)
    mult = jnp.sqrt(-_expm1(2.0 * log_a))
    valid = _iota((lp, 1), 0) >= PAD
    a_ref[...] = a
    b_ref[...] = jnp.where(valid, mult * i * lx, 0.0)

    def step(t, h):
        h = a_ref[pl.ds(t, 1), :] * h + b_ref[pl.ds(t, 1), :]
        h_ref[pl.ds(t, 1), :] = h
        return h

    lax.fori_loop(0, lp, step, jnp.zeros((1, x.shape[1]), F32), unroll=8)
    return lx, lxb, r, i, spl, a, mult, valid


def _lru_specs(lp):
    w = LRU_BLOCK
    return [
        pl.BlockSpec((lp, w), lambda g, b: (b, g)),
        pl.BlockSpec((lp, w), lambda g, b: (b, LRU_BLOCKS + g)),
        pl.BlockSpec((CONV_W, w), lambda g, b: (0, g)),
        pl.BlockSpec((1, w), lambda g, b: (0, g)),
        pl.BlockSpec((1, w, w), lambda g, b: (g, 0, 0)),
        pl.BlockSpec((1, w), lambda g, b: (0, g)),
        pl.BlockSpec((1, w, w), lambda g, b: (g, 0, 0)),
        pl.BlockSpec((1, w), lambda g, b: (0, g)),
        pl.BlockSpec((1, w), lambda g, b: (0, g)),
    ]


def _lru_fwd(proj, cw, cb, wa, ba, wx, bx, lam, bsz, lp):
    w = LRU_BLOCK

    def body(x_ref, g_ref, cw_ref, cb_ref, wa_ref, ba_ref, wx_ref, bx_ref, lam_ref, y_ref, a_s, b_s, h_s):
        _lru_forward_block(x_ref[...], cw_ref[...], cb_ref[...], wa_ref[0], ba_ref[...], wx_ref[0], bx_ref[...],
                           lam_ref[...], a_s, b_s, h_s)
        gate = g_ref[...]
        y_ref[...] = (h_s[...] * gate * _sigmoid(gate)).astype(BF16)

    return pl.pallas_call(
        body, grid=(LRU_BLOCKS, bsz), in_specs=_lru_specs(lp),
        out_specs=pl.BlockSpec((lp, w), lambda g, b: (b, g)),
        out_shape=jax.ShapeDtypeStruct((bsz * lp, LRU_BLOCKS * w), BF16),
        scratch_shapes=[pltpu.VMEM((lp, w), F32)] * 3,
        compiler_params=_cparams("parallel", "arbitrary"), name="lru_fwd")(proj, proj, cw, cb, wa, ba, wx, bx, lam)


def _lru_bwd(proj, dmixed, cw, cb, wa, ba, wx, bx, lam, bsz, lp):
    w = LRU_BLOCK

    def body(x_ref, g_ref, cw_ref, cb_ref, wa_ref, ba_ref, wx_ref, bx_ref, lam_ref, dy_ref,
             dx_ref, dg_ref, dcw_ref, dcb_ref, dwa_ref, dba_ref, dwx_ref, dbx_ref, dlam_ref, a_s, b_s, h_s, dh_s):
        @pl.when(pl.program_id(1) == 0)
        def _():
            for ref in (dcw_ref, dcb_ref, dwa_ref, dba_ref, dwx_ref, dbx_ref, dlam_ref):
                ref[...] = jnp.zeros_like(ref)

        x = x_ref[...]
        cwv = cw_ref[...]
        wav, wxv, lam_ = wa_ref[0], wx_ref[0], lam_ref[...]
        lx, lxb, r, i, spl, a, mult, valid = _lru_forward_block(
            x, cwv, cb_ref[...], wav, ba_ref[...], wxv, bx_ref[...], lam_, a_s, b_s, h_s)
        gate = g_ref[...]
        sg = _sigmoid(gate)
        dy = dy_ref[...]
        h = h_s[...]
        dg_ref[...] = (dy * h * sg * (1.0 + gate * (1.0 - sg))).astype(BF16)
        b_s[...] = dy * gate * sg

        def step(s, c):
            t = lp - 1 - s
            dh = b_s[pl.ds(t, 1), :] + c
            dh_s[pl.ds(t, 1), :] = dh
            return a_s[pl.ds(t, 1), :] * dh

        lax.fori_loop(0, lp, step, jnp.zeros((1, w), F32), unroll=8)
        dh = dh_s[...]
        hprev = jnp.where(_iota((lp, 1), 0) >= 1, _shift_down(h, 1), 0.0)
        db = jnp.where(valid, dh, 0.0)
        dmult = db * i * lx
        di = db * mult * lx
        dlx = db * mult * i
        inv_mult = jnp.where(mult > 0.0, 1.0 / mult, 0.0)
        dlog_a = dh * hprev * a - dmult * (a * a) * inv_mult
        drp = dlog_a * ((-RG_LRU_C) * spl) * r * (1.0 - r)
        dip = di * i * (1.0 - i)
        dspl = jnp.sum(dlog_a * ((-RG_LRU_C) * r), axis=0, keepdims=True)
        dlam_ref[...] += dspl * (-_sigmoid(-lam_))
        dba_ref[...] += jnp.sum(drp, axis=0, keepdims=True)
        dbx_ref[...] += jnp.sum(dip, axis=0, keepdims=True)
        drpb, dipb = drp.astype(BF16), dip.astype(BF16)
        dwa_ref[0] += _dot_tn(lxb, drpb)
        dwx_ref[0] += _dot_tn(lxb, dipb)
        dlx = dlx + _dot_nt(drpb, wav) + _dot_nt(dipb, wxv)
        dcb_ref[...] += jnp.sum(dlx, axis=0, keepdims=True)
        _conv_bwd_w(dcw_ref, dlx, x)
        dx_ref[...] = _conv_bwd_x(dlx, cwv).astype(BF16)

    t = bsz * lp
    vec = pl.BlockSpec((1, w), lambda g, b: (0, g))
    mat = pl.BlockSpec((1, w, w), lambda g, b: (g, 0, 0))
    act = pl.BlockSpec((lp, w), lambda g, b: (b, g))
    return pl.pallas_call(
        body, grid=(LRU_BLOCKS, bsz), in_specs=_lru_specs(lp) + [act],
        out_specs=[act, act, pl.BlockSpec((CONV_W, w), lambda g, b: (0, g)), vec, mat, vec, mat, vec, vec],
        out_shape=[jax.ShapeDtypeStruct((t, 1024), BF16), jax.ShapeDtypeStruct((t, 1024), BF16),
                   jax.ShapeDtypeStruct((CONV_W, 1024), F32), jax.ShapeDtypeStruct((1, 1024), F32),
                   jax.ShapeDtypeStruct((LRU_BLOCKS, w, w), F32), jax.ShapeDtypeStruct((1, 1024), F32),
                   jax.ShapeDtypeStruct((LRU_BLOCKS, w, w), F32), jax.ShapeDtypeStruct((1, 1024), F32),
                   jax.ShapeDtypeStruct((1, 1024), F32)],
        scratch_shapes=[pltpu.VMEM((lp, w), F32)] * 4,
        compiler_params=_cparams("parallel", "arbitrary"), name="lru_bwd")(
            proj, proj, cw, cb, wa, ba, wx, bx, lam, dmixed)


def _sb_masks():
    j = _iota((SB_T, 2 * SB_T), 0)
    s = _iota((SB_T, 2 * SB_T), 1)
    right = ((s < SB_T) & (j > s)) | (s >= SB_T)
    left = ((s < SB_T) & (j < s)) | (s >= SB_T)
    return jnp.where(right, -1.0, 0.0).astype(BF16), left.astype(BF16)


def _xdot2(a, m01):
    a1 = a.astype(BF16)
    a2 = (a - a1.astype(F32)).astype(BF16)
    return _dot(a1, m01) + _dot(a2, m01)


def _sb_query_block(lp):
    for tq in (512, 256, 128):
        if (lp - SB_T) % tq == 0:
            return tq, (lp - SB_T) // tq
    raise ValueError(lp)


def _sb_key_rows(kj):
    return pl.ds(kj * SB_T if isinstance(kj, int) else pl.multiple_of(kj * SB_T, SB_T), SB_T)


def _sb_valid(q0, tq, kj):
    t_pos = q0 + _iota((tq, SB_T), 0)
    s_pos = kj * SB_T + _iota((tq, SB_T), 1)
    return (s_pos < t_pos) & (s_pos >= PAD)


def _sb_by_head(ref, rows):
    t = ref[rows, :]
    head0 = _iota((SB_T, 128), 1) < SB_D
    return jnp.concatenate([jnp.where(head0, t, 0.0), jnp.where(head0, 0.0, t)], axis=0).astype(BF16)


def _sb_tile(qb, k_ref, q0, kj, carries, m_right_neg, masked):
    rows_k = _sb_key_rows(kj)
    z2 = _dot_nt(qb, _sb_by_head(k_ref, rows_k))
    valid = _sb_valid(q0, qb.shape[0], kj) if masked else None
    out = []
    for h in range(2):
        z = z2[:, h * SB_T:(h + 1) * SB_T]
        sp = _softplus(z)
        rs = _xdot2(jnp.where(valid, sp, 0.0) if masked else sp, m_right_neg)
        lb = z - sp
        wgt = jnp.exp(lb + rs[:, :SB_T] + carries[h])
        if masked:
            wgt = jnp.where(valid, wgt, 0.0)
        out.append((lb, wgt, carries[h] + rs[:, SB_T:]))
    return rows_k, valid, out


def _sb_sweep(step, c, lo, tq, first, leftwards):
    if first:
        return step(0, True)(c)
    r = tq // SB_T
    diag = lambda c: lax.fori_loop(0, r, lambda i, c: step(lo + r - 1 - i if leftwards else lo + i, True)(c), c)
    inner = lambda c: lax.fori_loop(0, lo - 1, lambda i, c: step(lo - 1 - i if leftwards else 1 + i, False)(c), c)
    if leftwards:
        return step(0, True)(inner(diag(c)))
    return diag(inner(step(0, True)(c)))


def _sb_specs(lp):
    nh = SB_HEADS // 2
    return [pl.BlockSpec((lp, 128), lambda b, p: (b, p)), pl.BlockSpec((lp, 128), lambda b, p: (b, nh + p)),
            pl.BlockSpec((lp, 128), lambda b, p: (b, 2 * nh + p)), pl.BlockSpec((lp, 128), lambda b, p: (b, 3 * nh + p))]


def _sb_fwd(qkvg, bsz, lp):
    tq, nb = _sb_query_block(lp)
    scale = SB_D ** -0.5

    def body(q_ref, k_ref, v_ref, g_ref, o_ref, og_ref):
        m_right_neg, _ = _sb_masks()

        def q_block(lo, rows, first):
            q0 = lo * SB_T
            rows_q = pl.ds(q0 if first else pl.multiple_of(q0, SB_T), rows)
            qb = (q_ref[rows_q, :] * scale).astype(BF16)

            def step(kj, masked):
                def run(c):
                    acc, car0, car1 = c
                    rows_k, _, ((_, w0, car0), (_, w1, car1)) = _sb_tile(qb, k_ref, q0, kj, (car0, car1), m_right_neg, masked)
                    w2 = jnp.concatenate([w0.astype(BF16), w1.astype(BF16)], axis=1)
                    return acc + _dot(w2, _sb_by_head(v_ref, rows_k)), car0, car1
                return run

            zero = jnp.zeros((rows, SB_T), F32)
            o_ref[rows_q, :] = _sb_sweep(step, (zero, zero, zero), lo, rows, first, True)[0]

        q_block(0, SB_T, True)

        def big_block(i, _):
            q_block(1 + (tq // SB_T) * i, tq, False)
            return 0

        lax.fori_loop(0, nb, big_block, 0)
        gate = g_ref[...]
        og_ref[...] = (o_ref[...] * gate * _sigmoid(gate)).astype(BF16)

    t = bsz * lp
    blk = pl.BlockSpec((lp, 128), lambda b, p: (b, p))
    return pl.pallas_call(
        body, grid=(bsz, SB_HEADS // 2), in_specs=_sb_specs(lp), out_specs=[blk, blk],
        out_shape=[jax.ShapeDtypeStruct((t, 1024), F32), jax.ShapeDtypeStruct((t, 1024), BF16)],
        compiler_params=_cparams("parallel", "parallel"), name="sb_fwd")(qkvg, qkvg, qkvg, qkvg)


def _sb_bwd(qkvg, o, dog, bsz, lp):
    tq, nb = _sb_query_block(lp)
    nk = lp // SB_T
    scale = SB_D ** -0.5

    def body(q_ref, k_ref, v_ref, g_ref, o_ref, dog_ref, dq_ref, dk_ref, dv_ref, dg_ref, do_s, dk_s, dv_s, e_s, sig_s):
        m_right_neg, m_left = _sb_masks()
        gate = g_ref[...]
        sg = _sigmoid(gate)
        dog = dog_ref[...]
        do_s[...] = dog * gate * sg
        dg_ref[...] = (dog * o_ref[...] * sg * (1.0 + gate * (1.0 - sg))).astype(BF16)
        dk_s[...] = jnp.zeros_like(dk_s)
        dv_s[...] = jnp.zeros_like(dv_s)
        def q_block(lo, rows, first):
            q0 = lo * SB_T
            rows_q = pl.ds(q0 if first else pl.multiple_of(q0, SB_T), rows)
            head0 = _iota((rows, 128), 1) < SB_D
            qf = q_ref[rows_q, :] * scale
            qb = qf.astype(BF16)
            q_h = (jnp.where(head0, qf, 0.0).astype(BF16), jnp.where(head0, 0.0, qf).astype(BF16))
            do_f = do_s[rows_q, :]
            do_b = do_f.astype(BF16)
            do_h = (jnp.where(head0, do_f, 0.0).astype(BF16), jnp.where(head0, 0.0, do_f).astype(BF16))

            def left(kj, masked):
                def run(c):
                    rows_k, _, heads = _sb_tile(qb, k_ref, q0, kj, c, m_right_neg, masked)
                    dw2 = _dot_nt(do_b, _sb_by_head(v_ref, rows_k))
                    dv = None
                    for h, (lb, wgt, _) in enumerate(heads):
                        e_s[2 * kj + h, 0:rows, :] = wgt * dw2[:, h * SB_T:(h + 1) * SB_T]
                        sig_s[2 * kj + h, 0:rows, :] = jnp.exp(lb)
                        part = _dot_tn(wgt.astype(BF16), do_h[h])
                        dv = part if dv is None else dv + part
                    dv_s[rows_k, :] += dv
                    return heads[0][2], heads[1][2]
                return run

            zero = jnp.zeros((rows, SB_T), F32)
            _sb_sweep(left, (zero, zero), lo, rows, first, True)

            def right(kj, masked):
                def run(c):
                    dq, ecar = c[0], list(c[1:])
                    rows_k = _sb_key_rows(kj)
                    valid = _sb_valid(q0, rows, kj) if masked else None
                    dzs, dk = [], None
                    for h in range(2):
                        e = e_s[2 * kj + h, 0:rows, :]
                        se = _dot(e.astype(BF16), m_left)
                        dz = e - sig_s[2 * kj + h, 0:rows, :] * (e + se[:, :SB_T] + ecar[h])
                        if masked:
                            dz = jnp.where(valid, dz, 0.0)
                        dz = dz.astype(BF16)
                        part = _dot_tn(dz, q_h[h])
                        dk = part if dk is None else dk + part
                        dzs.append(dz)
                        ecar[h] = ecar[h] + se[:, SB_T:]
                    dk_s[rows_k, :] += dk
                    return dq + _dot(jnp.concatenate(dzs, axis=1), _sb_by_head(k_ref, rows_k)), ecar[0], ecar[1]
                return run

            dq = _sb_sweep(right, (zero, zero, zero), lo, rows, first, False)[0]
            dq_ref[rows_q, :] = (dq * scale).astype(BF16)

        q_block(0, SB_T, True)

        def big_block(i, _):
            q_block(1 + (tq // SB_T) * i, tq, False)
            return 0

        lax.fori_loop(0, nb, big_block, 0)
        dk_ref[...] = dk_s[...].astype(BF16)
        dv_ref[...] = dv_s[...].astype(BF16)

    t = bsz * lp
    blk = pl.BlockSpec((lp, 128), lambda b, p: (b, p))
    shp = jax.ShapeDtypeStruct((t, 1024), BF16)
    return pl.pallas_call(
        body, grid=(bsz, SB_HEADS // 2), in_specs=_sb_specs(lp) + [blk, blk], out_specs=[blk] * 4,
        out_shape=[shp] * 4,
        scratch_shapes=[pltpu.VMEM((lp, 128), F32)] * 3 + [pltpu.VMEM((2 * nk, tq, SB_T), F32)] * 2,
        compiler_params=_cparams("parallel", "parallel"), name="sb_bwd")(qkvg, qkvg, qkvg, qkvg, o, dog)


XBC_COL0 = 3072 // 256
DT_COL0 = 4608 // 128
DT_L = 128


def _ssd_pre_fwd(proj, cw, cb, bsz, lp):
    def body(x_ref, cw_ref, cb_ref, o_ref):
        pre = _conv(x_ref[...], cw_ref[...], cb_ref[...])
        o_ref[...] = pre * _sigmoid(pre)

    return pl.pallas_call(
        body, grid=(bsz, SSD_CONV_DIM // 256),
        in_specs=[pl.BlockSpec((lp, 256), lambda b, j: (b, XBC_COL0 + j)), pl.BlockSpec((CONV_W, 256), lambda b, j: (0, j)),
                  pl.BlockSpec((1, 256), lambda b, j: (0, j))],
        out_specs=pl.BlockSpec((lp, 256), lambda b, j: (b, j)),
        out_shape=jax.ShapeDtypeStruct((bsz * lp, SSD_CONV_DIM), F32),
        compiler_params=_cparams("parallel", "parallel"), name="ssd_pre_fwd")(proj, cw, cb)


def _ssd_pre_bwd(proj, dact, cw, cb, bsz, lp):
    def body(x_ref, d_ref, cw_ref, cb_ref, dx_ref, dcw_ref, dcb_ref):
        @pl.when(pl.program_id(1) == 0)
        def _():
            dcw_ref[...] = jnp.zeros_like(dcw_ref)
            dcb_ref[...] = jnp.zeros_like(dcb_ref)
        x = x_ref[...]
        cwv = cw_ref[...]
        pre = _conv(x, cwv, cb_ref[...])
        s = _sigmoid(pre)
        dpre = d_ref[...].astype(F32) * s * (1.0 + pre * (1.0 - s))
        dcb_ref[...] += jnp.sum(dpre, axis=0, keepdims=True)
        _conv_bwd_w(dcw_ref, dpre, x)
        dx_ref[...] = _conv_bwd_x(dpre, cwv).astype(BF16)

    return pl.pallas_call(
        body, grid=(SSD_CONV_DIM // 256, bsz),
        in_specs=[pl.BlockSpec((lp, 256), lambda j, b: (b, XBC_COL0 + j)), pl.BlockSpec((lp, 256), lambda j, b: (b, j)),
                  pl.BlockSpec((CONV_W, 256), lambda j, b: (0, j)), pl.BlockSpec((1, 256), lambda j, b: (0, j))],
        out_specs=[pl.BlockSpec((lp, 256), lambda j, b: (b, j)), pl.BlockSpec((CONV_W, 256), lambda j, b: (0, j)),
                   pl.BlockSpec((1, 256), lambda j, b: (0, j))],
        out_shape=[jax.ShapeDtypeStruct((bsz * lp, SSD_CONV_DIM), BF16), jax.ShapeDtypeStruct((CONV_W, SSD_CONV_DIM), F32),
                   jax.ShapeDtypeStruct((1, SSD_CONV_DIM), F32)],
        compiler_params=_cparams("parallel", "arbitrary"), name="ssd_pre_bwd")(proj, dact, cw, cb)


def _xdot_nt_r(a, m01):
    a1, a2, a3 = _split3(a)
    return _dot_nt(a1, m01) + _dot_nt(a2, m01) + _dot_nt(a3, m01)


class _SsdConsts:
    def __init__(self, g):
        q, gw = SSD_Q, SSD_GW
        head_of_lane = lax.shift_right_logical(_iota((DT_L, gw), 1), 6)
        self.sel = (_iota((DT_L, gw), 0) == 8 * g + head_of_lane).astype(BF16)
        r = _iota((q, gw), 0)
        c = jnp.bitwise_and(_iota((q, gw), 1), q - 1)
        self.diag = r == c
        self.diag_b = self.diag.astype(BF16)
        self.lower = c <= r
        self.upper = c >= r
        self.bd = lax.shift_right_logical(_iota((gw, gw), 0), 6) == lax.shift_right_logical(_iota((gw, gw), 1), 6)
        jj, ll = _iota((q, q), 1), _iota((q, q), 0)
        self.tri = (jj <= ll).astype(BF16)
        self.tri_t = (jj >= ll).astype(BF16)
        self.ones = jnp.ones((q, q), BF16)
        self.last = _iota((q, 1), 0) == q - 1


def _ssd_prepass(k, dt_ref, bias_ref, alog_ref, dtm_s, acum_s, lp):
    valid = _iota((lp, 1), 0) >= PAD
    sp_in = dt_ref[...] + bias_ref[...]
    dtm = jnp.where(valid, _softplus(sp_in), 0.0)
    aneg = -jnp.exp(alog_ref[...])
    dtm_s[...] = dtm
    acum_s[...] = dtm * aneg

    def cum(c, _):
        rows = pl.ds(pl.multiple_of(c * SSD_Q, SSD_Q), SSD_Q)
        acum_s[rows, :] = _xdot_l(k.tri, acum_s[rows, :])
        return 0

    lax.fori_loop(0, lp // SSD_Q, cum, 0)
    return valid, sp_in, aneg


def _ssd_chunk(k, rows, xs_ref, b_ref, c_ref, dtm_s, acum_s):
    bc = _xdot_r(acum_s[rows, :], k.sel)
    tt = _xdot_l(k.ones, jnp.where(k.diag, bc, 0.0))
    dtbc = _xdot_r(dtm_s[rows, :], k.sel)
    xs = xs_ref[rows, :]
    x = xs * dtbc
    bb = b_ref[rows, :].astype(BF16)
    cc = c_ref[rows, :].astype(BF16)
    tot = bc[SSD_Q - 1:SSD_Q, :]
    xbd = jnp.where(k.bd, jnp.concatenate([x] * 8, axis=0), 0.0).astype(BF16)
    return bc, tt, dtbc, xs, x, bb, cc, tot, xbd


def _ssd_specs(lp, order):
    ix = (lambda f: (lambda b, g: f(b, g))) if order == "bg" else (lambda f: (lambda g, b: f(b, g)))
    return [pl.BlockSpec((lp, SSD_GW), ix(lambda b, g: (b, g))),
            pl.BlockSpec((lp, SSD_N), ix(lambda b, g: (b, 1024 // SSD_N + g))),
            pl.BlockSpec((lp, SSD_N), ix(lambda b, g: (b, 1280 // SSD_N + g))),
            pl.BlockSpec((lp, DT_L), ix(lambda b, g: (b, DT_COL0))),
            pl.BlockSpec((1, DT_L), ix(lambda b, g: (0, 0))),
            pl.BlockSpec((1, DT_L), ix(lambda b, g: (0, 0))),
            pl.BlockSpec((1, DT_L), ix(lambda b, g: (0, 0)))]


def _ssd_fwd(xbc, proj, dt_bias, a_log, dskip, bsz, lp):
    nc = lp // SSD_Q

    def body(xs_ref, b_ref, c_ref, dt_ref, bias_ref, alog_ref, dsk_ref, y_ref, dtm_s, acum_s):
        k = _SsdConsts(pl.program_id(1))
        _ssd_prepass(k, dt_ref, bias_ref, alog_ref, dtm_s, acum_s, lp)
        dvec = _xdot_r(jnp.broadcast_to(dsk_ref[...], (8, DT_L)), k.sel)[0:1, :]

        def chunk(c, state):
            rows = pl.ds(pl.multiple_of(c * SSD_Q, SSD_Q), SSD_Q)
            bc, tt, _, xs, x, bb, cc, tot, xbd = _ssd_chunk(k, rows, xs_ref, b_ref, c_ref, dtm_s, acum_s)
            lm = jnp.where(k.lower, jnp.exp(jnp.minimum(bc - tt, 0.0)), 0.0)
            g_all = _dot(_dot_nt(cc, bb).astype(BF16), k.diag_b) * lm
            y = _dot(g_all.astype(BF16), xbd) + jnp.exp(bc) * _dot(cc, state.astype(BF16)) + dvec * xs
            y_ref[rows, :] = y
            return jnp.exp(tot) * state + _dot_tn(bb, (jnp.exp(tot - bc) * x).astype(BF16))

        lax.fori_loop(0, nc, chunk, jnp.zeros((SSD_N, SSD_GW), F32))

    return pl.pallas_call(
        body, grid=(bsz, SSD_GROUPS), in_specs=_ssd_specs(lp, "bg"),
        out_specs=pl.BlockSpec((lp, SSD_GW), lambda b, g: (b, g)),
        out_shape=jax.ShapeDtypeStruct((bsz * lp, 1024), F32),
        scratch_shapes=[pltpu.VMEM((lp, DT_L), F32)] * 2,
        compiler_params=_cparams("parallel", "parallel"), name="ssd_fwd")(xbc, xbc, xbc, proj, dt_bias, a_log, dskip)


def _ssd_bwd(xbc, proj, dt_bias, a_log, dskip, dy, bsz, lp):
    nc = lp // SSD_Q

    def body(xs_ref, b_ref, c_ref, dt_ref, bias_ref, alog_ref, dsk_ref, dy_ref,
             dxs_ref, db_ref, dc_ref, ddt_ref, dbias_ref, dalog_ref, ddsk_ref, dtm_s, acum_s, st_s):
        @pl.when(pl.program_id(1) == 0)
        def _():
            for ref in (dbias_ref, dalog_ref, ddsk_ref):
                ref[...] = jnp.zeros_like(ref)

        k = _SsdConsts(pl.program_id(0))
        valid, sp_in, aneg = _ssd_prepass(k, dt_ref, bias_ref, alog_ref, dtm_s, acum_s, lp)
        dvec = _xdot_r(jnp.broadcast_to(dsk_ref[...], (8, DT_L)), k.sel)[0:1, :]

        def fwd_chunk(c, state):
            rows = pl.ds(pl.multiple_of(c * SSD_Q, SSD_Q), SSD_Q)
            st_s[c] = state.astype(BF16)
            bc, _, _, _, x, bb, _, tot, _ = _ssd_chunk(k, rows, xs_ref, b_ref, c_ref, dtm_s, acum_s)
            return jnp.exp(tot) * state + _dot_tn(bb, (jnp.exp(tot - bc) * x).astype(BF16))

        lax.fori_loop(0, nc, fwd_chunk, jnp.zeros((SSD_N, SSD_GW), F32))

        def bwd_chunk(i, carry):
            dstate, daneg, ddsk = carry
            c = nc - 1 - i
            rows = pl.ds(pl.multiple_of(c * SSD_Q, SSD_Q), SSD_Q)
            bc, tt, dtbc, xs, x, bb, cc, tot, xbd = _ssd_chunk(k, rows, xs_ref, b_ref, c_ref, dtm_s, acum_s)
            sprev = st_s[c]
            dyc = dy_ref[rows, :]
            dyb = dyc.astype(BF16)
            seg = bc - tt
            lm = jnp.where(k.lower, jnp.exp(jnp.minimum(seg, 0.0)), 0.0)
            lm_t = jnp.where(k.upper, jnp.exp(jnp.minimum(-seg, 0.0)), 0.0)
            cb_all = _dot(_dot_nt(cc, bb).astype(BF16), k.diag_b)
            cbt_all = _dot(_dot_nt(bb, cc).astype(BF16), k.diag_b)
            g_all = cb_all * lm
            dybd = jnp.where(k.bd, jnp.concatenate([dyc] * 8, axis=0), 0.0).astype(BF16)
            dg = _dot_nt(dyb, xbd)
            dx = _dot((cbt_all * lm_t).astype(BF16), dybd)
            hh = dg * g_all
            ea = jnp.exp(bc)
            yo = ea * _dot(cc, sprev)
            dacum = _xdot_nt_r(hh + dyc * yo, k.sel)
            col_h = _xdot_l(k.ones, hh)
            dacum = dacum - _xdot_nt_r(jnp.where(k.diag, col_h, 0.0), k.sel)
            dcb = _dot_nt((dg * lm).astype(BF16), k.diag_b).astype(BF16)
            dcs = (ea * dyc).astype(BF16)
            dstb = dstate.astype(BF16)
            dec = jnp.exp(tot - bc)
            w = dec * x
            dw = _dot(bb, dstb)
            dc_ref[rows, :] = (_dot(dcb, bb) + _dot_nt(dcs, sprev)).astype(BF16)
            db_ref[rows, :] = (_dot_tn(dcb, cc) + _dot_nt(w.astype(BF16), dstb)).astype(BF16)
            dx = dx + dec * dw
            kk = _xdot_nt_r(dw * w, k.sel)
            etot = jnp.exp(tot)
            dtot = _xdot_nt_r(_xdot_l(jnp.ones((8, SSD_N), BF16), dstate * sprev.astype(F32) * etot), k.sel)[0:1, :]
            dtot = dtot + jnp.sum(kk, axis=0, keepdims=True)
            dacum = dacum - kk + jnp.where(k.last, dtot, 0.0)
            da = _xdot_l(k.tri_t, dacum)
            dtm_c = dtm_s[rows, :]
            ddtm = da * aneg + _xdot_nt_r(dx * xs, k.sel)
            vrow = (c * SSD_Q + _iota((SSD_Q, 1), 0)) >= PAD
            ddt_ref[rows, :] = jnp.where(vrow, ddtm * _sigmoid(dt_ref[rows, :] + bias_ref[...]), 0.0)
            dxs_ref[rows, :] = (dx * dtbc + dvec * dyc).astype(BF16)
            daneg = daneg + jnp.sum(da * dtm_c, axis=0, keepdims=True)
            ddsk = ddsk + jnp.sum(_xdot_nt_r(dyc * xs, k.sel), axis=0, keepdims=True)
            dstate = etot * dstate + _dot_tn(cc, dcs)
            return dstate, daneg, ddsk

        zrow = jnp.zeros((1, DT_L), F32)
        _, daneg, ddsk = lax.fori_loop(0, nc, bwd_chunk, (jnp.zeros((SSD_N, SSD_GW), F32), zrow, zrow))
        dbias_ref[...] += jnp.broadcast_to(jnp.sum(ddt_ref[...], axis=0, keepdims=True), (8, DT_L))
        dalog_ref[...] += jnp.broadcast_to(daneg * aneg, (8, DT_L))
        ddsk_ref[...] += jnp.broadcast_to(ddsk, (8, DT_L))

    t = bsz * lp
    par = pl.BlockSpec((8, DT_L), lambda g, b: (g, 0))
    par_shape = jax.ShapeDtypeStruct((8 * SSD_GROUPS, DT_L), F32)
    return pl.pallas_call(
        body, grid=(SSD_GROUPS, bsz),
        in_specs=_ssd_specs(lp, "gb") + [pl.BlockSpec((lp, SSD_GW), lambda g, b: (b, g))],
        out_specs=[pl.BlockSpec((lp, SSD_GW), lambda g, b: (b, g)), pl.BlockSpec((lp, SSD_N), lambda g, b: (b, g)),
                   pl.BlockSpec((lp, SSD_N), lambda g, b: (b, g)), pl.BlockSpec((lp, DT_L), lambda g, b: (b, g)), par, par, par],
        out_shape=[jax.ShapeDtypeStruct((t, 1024), BF16), jax.ShapeDtypeStruct((t, 256), BF16),
                   jax.ShapeDtypeStruct((t, 256), BF16), jax.ShapeDtypeStruct((t, SSD_GROUPS * DT_L), F32),
                   par_shape, par_shape, par_shape],
        scratch_shapes=[pltpu.VMEM((lp, DT_L), F32), pltpu.VMEM((lp, DT_L), F32), pltpu.VMEM((nc, SSD_N, SSD_GW), BF16)],
        compiler_params=_cparams("parallel", "arbitrary"), name="ssd_bwd")(
            xbc, xbc, xbc, proj, dt_bias, a_log, dskip, dy)


Z_COL0 = 2048 // SSD_GW


def _gnorm_fwd(y, proj, w, name="gnorm_fwd"):
    t = y.shape[0]
    tm = _row_tile(t)

    def body(y_ref, z_ref, w_ref, o_ref):
        z = z_ref[...]
        gt = y_ref[...] * z * _sigmoid(z)
        r = lax.rsqrt(jnp.mean(gt * gt, axis=-1, keepdims=True) + EPS)
        o_ref[...] = (gt * r * w_ref[...]).astype(BF16)

    return pl.pallas_call(
        body, grid=(t // tm, SSD_GROUPS),
        in_specs=[pl.BlockSpec((tm, SSD_GW), lambda i, g: (i, g)), pl.BlockSpec((tm, SSD_GW), lambda i, g: (i, Z_COL0 + g)),
                  pl.BlockSpec((1, SSD_GW), lambda i, g: (0, g))],
        out_specs=pl.BlockSpec((tm, SSD_GW), lambda i, g: (i, g)),
        out_shape=jax.ShapeDtypeStruct((t, 1024), BF16),
        compiler_params=_cparams("parallel", "parallel"), name=name)(y, proj, w)


def _gnorm_bwd(y, proj, w, dmixed):
    t = y.shape[0]
    tm = _row_tile(t)

    def body(y_ref, z_ref, w_ref, d_ref, dy_ref, dz_ref, dw_ref):
        @pl.when(pl.program_id(1) == 0)
        def _():
            dw_ref[...] = jnp.zeros_like(dw_ref)
        z, yv, d = z_ref[...], y_ref[...], d_ref[...]
        s = _sigmoid(z)
        sz = z * s
        gt = yv * sz
        r = lax.rsqrt(jnp.mean(gt * gt, axis=-1, keepdims=True) + EPS)
        gh = gt * r
        dgn = d * w_ref[...]
        dgt = r * (dgn - gh * jnp.mean(dgn * gh, axis=-1, keepdims=True))
        dw_ref[...] += jnp.sum(d * gh, axis=0, keepdims=True)
        dy_ref[...] = dgt * sz
        dz_ref[...] = (dgt * yv * s * (1.0 + z * (1.0 - s))).astype(BF16)

    blk = pl.BlockSpec((tm, SSD_GW), lambda g, i: (i, g))
    return pl.pallas_call(
        body, grid=(SSD_GROUPS, t // tm),
        in_specs=[blk, pl.BlockSpec((tm, SSD_GW), lambda g, i: (i, Z_COL0 + g)), pl.BlockSpec((1, SSD_GW), lambda g, i: (0, g)),
                  pl.BlockSpec((tm, SSD_GW), lambda g, i: (i, 1024 // SSD_GW + g))],
        out_specs=[blk, blk, pl.BlockSpec((1, SSD_GW), lambda g, i: (0, g))],
        out_shape=[jax.ShapeDtypeStruct((t, 1024), F32), jax.ShapeDtypeStruct((t, 1024), BF16),
                   jax.ShapeDtypeStruct((1, 1024), F32)],
        compiler_params=_cparams("parallel", "arbitrary"), name="gnorm_bwd")(y, proj, w, dmixed)


def _meta_grad(dh0, bsz, lp):
    def body(d_ref, o_ref):
        @pl.when(pl.program_id(0) == 0)
        def _():
            o_ref[...] = jnp.zeros_like(o_ref)
        o_ref[...] += d_ref[...]

    return pl.pallas_call(
        body, grid=(bsz,),
        in_specs=[pl.BlockSpec((N_META, D_MODEL), lambda b: (b * (lp // N_META) + PAD // N_META, 0))],
        out_specs=pl.BlockSpec((N_META, D_MODEL), lambda b: (0, 0)),
        out_shape=jax.ShapeDtypeStruct((N_META, D_MODEL), F32),
        compiler_params=_cparams("arbitrary"), name="meta_grad")(dh0)


def _exchange(gather, scatter, name):
    ng, ns = len(gather), len(scatter)
    n = ng + ns

    def body(*refs):
        ins, outs = refs[:n], refs[n:2 * n]
        send_sems, recv_sems, local_sems = refs[2 * n:]
        x, y, c = lax.axis_index("x"), lax.axis_index("y"), lax.axis_index("c")
        me = 4 * x + 2 * y + c

        def peer(k):
            px = 1 - x if k & 4 else x
            py = 1 - y if k & 2 else y
            pc = 1 - c if k & 1 else c
            return (px, py, pc), 4 * px + 2 * py + pc

        local, sends, recvs = [], [], []
        for a in range(n):
            is_gather = a < ng
            src_own = ins[a] if is_gather else ins[a].at[me]
            local.append(pltpu.make_async_copy(src_own, outs[a].at[me], local_sems.at[a]))
            for k in range(1, N_DEV):
                dev, pid = peer(k)
                src = ins[a] if is_gather else ins[a].at[pid]
                sends.append(pltpu.make_async_remote_copy(
                    src_ref=src, dst_ref=outs[a].at[me], send_sem=send_sems.at[a, k - 1], recv_sem=recv_sems.at[a, k - 1],
                    device_id=dev, device_id_type=pl.DeviceIdType.MESH))
                recvs.append(pltpu.make_async_remote_copy(
                    src_ref=src, dst_ref=outs[a].at[pid], send_sem=send_sems.at[a, k - 1], recv_sem=recv_sems.at[a, k - 1],
                    device_id=dev, device_id_type=pl.DeviceIdType.MESH))
        for cp in local + sends:
            cp.start()
        for cp in recvs:
            cp.wait_recv()
        for cp in sends:
            cp.wait_send()
        for cp in local:
            cp.wait()

    anyspec = pl.BlockSpec(memory_space=pl.ANY)
    out_shape = [jax.ShapeDtypeStruct((N_DEV,) + a.shape, a.dtype) for a in gather]
    out_shape += [jax.ShapeDtypeStruct(a.shape, a.dtype) for a in scatter]
    return pl.pallas_call(
        body, in_specs=[anyspec] * n, out_specs=[anyspec] * n, out_shape=out_shape,
        scratch_shapes=[pltpu.SemaphoreType.DMA((n, N_DEV - 1)), pltpu.SemaphoreType.DMA((n, N_DEV - 1)),
                        pltpu.SemaphoreType.DMA((n,))],
        compiler_params=pltpu.CompilerParams(has_side_effects=True), name=name)(*gather, *scatter)


def _adamw(parts, w, m, v, name):
    r, c = w.shape
    tr = r
    for cand in (128, 64, 32, 16, 8):
        if r % cand == 0 and r > cand:
            tr = cand
            break

    def body(p_ref, w_ref, m_ref, v_ref, g_ref, d_ref, nm_ref, nv_ref):
        g = p_ref[0].astype(F32)
        for k in range(1, N_DEV):
            g = g + p_ref[k].astype(F32)
        m_new = ADAM_B1 * m_ref[...] + (1.0 - ADAM_B1) * g
        v_new = ADAM_B2 * v_ref[...] + (1.0 - ADAM_B2) * (g * g)
        m_hat = m_new / (1.0 - ADAM_B1 ** ADAM_STEP)
        v_hat = v_new / (1.0 - ADAM_B2 ** ADAM_STEP)
        g_ref[...] = g
        d_ref[...] = -ADAM_LR * (m_hat / (jnp.sqrt(v_hat) + ADAM_EPS) + ADAM_WD * w_ref[...])
        nm_ref[...] = m_new
        nv_ref[...] = v_new

    blk = pl.BlockSpec((tr, c), lambda i: (i, 0))
    shp = jax.ShapeDtypeStruct((r, c), F32)
    return pl.pallas_call(
        body, grid=(r // tr,), in_specs=[pl.BlockSpec((N_DEV, tr, c), lambda i: (0, i, 0)), blk, blk, blk],
        out_specs=[blk] * 4, out_shape=[shp] * 4, compiler_params=_cparams("parallel"), name=name)(parts, w, m, v)


def _rows128(a):
    return a.reshape(-1, 128)


def _pad_rows(a, rows):
    return jnp.pad(a, ((0, rows - a.shape[0]), (0, 0)))


def _lane16(a):
    return jnp.pad(a.reshape(1, -1), ((0, 0), (0, 128 - a.size)))


SHARD_PACK_ROWS = 544
REPL_PACK_ROWS = 72


def _pack_shard(meta, lru_conv_w, odd_norm, ssd_conv_w, lru_w_a, lru_w_x):
    parts = [meta.reshape(16, 128), lru_conv_w.reshape(4, 128), odd_norm.reshape(1, 128), _rows128(ssd_conv_w.reshape(4, 192)),
             _rows128(lru_w_a.reshape(4, 32, 256)), _rows128(lru_w_x.reshape(4, 32, 256))]
    return _pad_rows(jnp.concatenate(parts, axis=0), SHARD_PACK_ROWS)


def _unpack_shard(p):
    return (p[0:16], p[16:20].reshape(1, 4, 128), p[20:21], p[21:27].reshape(1, 4, 192),
            p[27:283].reshape(1, 4, 32, 256), p[283:539].reshape(1, 4, 32, 256))


def _pack_repl(even_norm, lru_conv_b, lru_b_a, lru_b_x, lru_lambda, ssd_norm, final_norm, ssd_conv_b, dt_bias, a_log, ssd_d):
    parts = [_rows128(v) for v in (even_norm, lru_conv_b, lru_b_a, lru_b_x, lru_lambda, ssd_norm, final_norm, ssd_conv_b)]
    parts += [_lane16(dt_bias), _lane16(a_log), _lane16(ssd_d)]
    return _pad_rows(jnp.concatenate(parts, axis=0), REPL_PACK_ROWS)


def _unpack_repl(p):
    vec = lambda i: p[8 * i:8 * i + 8].reshape(1, 1024)
    return (vec(0), vec(1), vec(2), vec(3), vec(4), vec(5), p[48:56].reshape(1024), p[56:68].reshape(1, 1536),
            p[68:69, :16], p[69:70, :16], p[70:71, :16])


def _local_step(x, tgt, meta, even_norm, w_in_e, lru_conv_w, lru_conv_b, lru_w_a, lru_b_a, lru_w_x, lru_b_x, lru_lambda,
                ssd_conv_w, ssd_conv_b, dt_bias, a_log, ssd_d, ssd_norm, w_out_e, odd_norm, w_in_o, w_out_o, final_norm):
    bsz, seq, d = x.shape
    lp = PAD + N_META + seq
    t = bsz * lp
    h0 = jnp.concatenate([jnp.zeros((bsz, PAD, d), F32), jnp.broadcast_to(meta[None], (bsz, N_META, d)), x], axis=1).reshape(t, d)

    u0 = _norm_fwd(h0, even_norm, "norm0_fwd")
    proj = _mm_nn(u0, w_in_e, F32, "even_in")
    lru = (lru_conv_w, lru_conv_b, lru_w_a, lru_b_a, lru_w_x, lru_b_x, lru_lambda)
    ya = _lru_fwd(proj, *lru, bsz, lp)
    xbc = _ssd_pre_fwd(proj, ssd_conv_w, ssd_conv_b, bsz, lp)
    y = _ssd_fwd(xbc, proj, dt_bias, a_log, ssd_d, bsz, lp)
    yb = _gnorm_fwd(y, proj, ssd_norm)
    mixed = jnp.concatenate([ya, yb], axis=1)
    h1 = _mm_nn(mixed, w_out_e, F32, "even_out", res=h0)
    u2 = _norm_fwd(h1, odd_norm, "norm1_fwd")
    qkvg = _mm_nn(u2, w_in_o, F32, "odd_in")
    o, og = _sb_fwd(qkvg, bsz, lp)
    h2 = _mm_nn(og, w_out_o, F32, "odd_out", res=h1)
    loss, dh2, d_final_norm = _final_loss(h2, final_norm, tgt, lp)

    dog = _mm_nt(dh2, w_out_o, F32, "odd_out_dx")
    d_w_out_o = _mm_tn(og, dh2, "odd_out_dw")
    dqkvg = jnp.concatenate(_sb_bwd(qkvg, o, dog, bsz, lp), axis=1)
    du2 = _mm_nt(dqkvg, w_in_o, F32, "odd_in_dx")
    d_w_in_o = _mm_tn(u2, dqkvg, "odd_in_dw")
    dh1, d_odd_norm = _norm_bwd(h1, odd_norm, du2, dh2, "norm1_bwd")

    dmixed = _mm_nt(dh1, w_out_e, F32, "even_out_dx")
    d_w_out_e = _mm_tn(mixed, dh1, "even_out_dw")
    dlx, dgate, d_lru_conv_w, d_lru_conv_b, d_w_a, d_b_a, d_w_x, d_b_x, d_lambda = _lru_bwd(proj, dmixed, *lru, bsz, lp)
    dy, dz, d_ssd_norm = _gnorm_bwd(y, proj, ssd_norm, dmixed)
    dxs, dbm, dcm, ddt, d_dt_bias, d_a_log, d_ssd_d = _ssd_bwd(xbc, proj, dt_bias, a_log, ssd_d, dy, bsz, lp)
    dxbc, d_ssd_conv_w, d_ssd_conv_b = _ssd_pre_bwd(proj, jnp.concatenate([dxs, dbm, dcm], axis=1), ssd_conv_w, ssd_conv_b, bsz, lp)
    ddt = (ddt[:, :DT_L] + ddt[:, DT_L:]).astype(BF16)
    dproj = jnp.concatenate([dlx, dgate, dz, dxbc, ddt, jnp.zeros((t, EVEN_NP - 4608 - DT_L), BF16)], axis=1)
    du0 = _mm_nt(dproj, w_in_e, F32, "even_in_dx")
    d_w_in_e = _mm_tn(u0, dproj, "even_in_dw")
    dh0, d_even_norm = _norm_bwd(h0, even_norm, du0, dh1, "norm0_bwd")
    grad_x = dh0.reshape(bsz, lp, d)[:, PAD + N_META:]
    d_meta = _meta_grad(dh0, bsz, lp)
    heads = lambda p: (p[0:1] + p[8:9])[:, :SSD_HEADS]
    grads = dict(meta=d_meta, even_norm=d_even_norm, even_w_in=d_w_in_e[:, :EVEN_IN], lru_conv_w=d_lru_conv_w,
                 lru_conv_b=d_lru_conv_b, lru_w_a=d_w_a, lru_b_a=d_b_a, lru_w_x=d_w_x, lru_b_x=d_b_x, lru_lambda=d_lambda,
                 ssd_conv_w=d_ssd_conv_w, ssd_conv_b=d_ssd_conv_b, ssd_dt_bias=heads(d_dt_bias), ssd_a_log=heads(d_a_log),
                 ssd_d=heads(d_ssd_d), ssd_norm=d_ssd_norm, even_w_out=d_w_out_e, odd_norm=d_odd_norm, odd_w_in=d_w_in_o,
                 odd_w_out=d_w_out_o, final_norm=d_final_norm)
    return loss[0, 0], grad_x, grads


WEIGHTS = ['meta', 'even_norm', 'even_w_in', 'lru_conv_w', 'lru_conv_b', 'lru_w_a', 'lru_b_a', 'lru_w_x', 'lru_b_x', 'lru_lambda',
           'ssd_conv_w', 'ssd_conv_b', 'ssd_dt_bias', 'ssd_a_log', 'ssd_d', 'ssd_norm', 'even_w_out', 'odd_norm', 'odd_w_in',
           'odd_w_out', 'final_norm']


def _blocks_of(a, axis):
    shp = a.shape
    a = a.reshape(shp[:axis] + (N_DEV, shp[axis] // N_DEV) + shp[axis + 1:])
    return jnp.moveaxis(a, axis, 0)


def _unblock(a, axis):
    a = jnp.moveaxis(a, 0, axis)
    shp = a.shape
    return a.reshape(shp[:axis] + (shp[axis] * shp[axis + 1],) + shp[axis + 2:])


def kernel(x, meta, even_norm, even_w_in, lru_conv_w, lru_conv_b, lru_w_a, lru_b_a, lru_w_x, lru_b_x, lru_lambda, ssd_conv_w, ssd_conv_b, ssd_dt_bias, ssd_a_log, ssd_d, ssd_norm, even_w_out, odd_norm, odd_w_in, odd_w_out, final_norm, loss_target, m_meta, m_even_norm, m_even_w_in, m_lru_conv_w, m_lru_conv_b, m_lru_w_a, m_lru_b_a, m_lru_w_x, m_lru_b_x, m_lru_lambda, m_ssd_conv_w, m_ssd_conv_b, m_ssd_dt_bias, m_ssd_a_log, m_ssd_d, m_ssd_norm, m_even_w_out, m_odd_norm, m_odd_w_in, m_odd_w_out, m_final_norm, v_meta, v_even_norm, v_even_w_in, v_lru_conv_w, v_lru_conv_b, v_lru_w_a, v_lru_b_a, v_lru_w_x, v_lru_b_x, v_lru_lambda, v_ssd_conv_w, v_ssd_conv_b, v_ssd_dt_bias, v_ssd_a_log, v_ssd_d, v_ssd_norm, v_even_w_out, v_odd_norm, v_odd_w_in, v_odd_w_out, v_final_norm):
    w = dict(meta=meta, even_norm=even_norm, even_w_in=even_w_in, lru_conv_w=lru_conv_w, lru_conv_b=lru_conv_b, lru_w_a=lru_w_a,
             lru_b_a=lru_b_a, lru_w_x=lru_w_x, lru_b_x=lru_b_x, lru_lambda=lru_lambda, ssd_conv_w=ssd_conv_w,
             ssd_conv_b=ssd_conv_b, ssd_dt_bias=ssd_dt_bias, ssd_a_log=ssd_a_log, ssd_d=ssd_d, ssd_norm=ssd_norm,
             even_w_out=even_w_out, odd_norm=odd_norm, odd_w_in=odd_w_in, odd_w_out=odd_w_out, final_norm=final_norm)
    m = dict(meta=m_meta, even_norm=m_even_norm, even_w_in=m_even_w_in, lru_conv_w=m_lru_conv_w, lru_conv_b=m_lru_conv_b,
             lru_w_a=m_lru_w_a, lru_b_a=m_lru_b_a, lru_w_x=m_lru_w_x, lru_b_x=m_lru_b_x, lru_lambda=m_lru_lambda,
             ssd_conv_w=m_ssd_conv_w, ssd_conv_b=m_ssd_conv_b, ssd_dt_bias=m_ssd_dt_bias, ssd_a_log=m_ssd_a_log, ssd_d=m_ssd_d,
             ssd_norm=m_ssd_norm, even_w_out=m_even_w_out, odd_norm=m_odd_norm, odd_w_in=m_odd_w_in, odd_w_out=m_odd_w_out,
             final_norm=m_final_norm)
    v = dict(meta=v_meta, even_norm=v_even_norm, even_w_in=v_even_w_in, lru_conv_w=v_lru_conv_w, lru_conv_b=v_lru_conv_b,
             lru_w_a=v_lru_w_a, lru_b_a=v_lru_b_a, lru_w_x=v_lru_w_x, lru_b_x=v_lru_b_x, lru_lambda=v_lru_lambda,
             ssd_conv_w=v_ssd_conv_w, ssd_conv_b=v_ssd_conv_b, ssd_dt_bias=v_ssd_dt_bias, ssd_a_log=v_ssd_a_log, ssd_d=v_ssd_d,
             ssd_norm=v_ssd_norm, even_w_out=v_even_w_out, odd_norm=v_odd_norm, odd_w_in=v_odd_w_in, odd_w_out=v_odd_w_out,
             final_norm=v_final_norm)
    shard_names = ('meta', 'lru_conv_w', 'odd_norm', 'ssd_conv_w', 'lru_w_a', 'lru_w_x')
    repl_names = ('even_norm', 'lru_conv_b', 'lru_b_a', 'lru_b_x', 'lru_lambda', 'ssd_norm', 'final_norm', 'ssd_conv_b',
                  'ssd_dt_bias', 'ssd_a_log', 'ssd_d')
    big_names = ('even_w_in', 'even_w_out', 'odd_w_in', 'odd_w_out')

    gates = jnp.concatenate([lru_w_a.reshape(128, 256), lru_w_x.reshape(128, 256)], axis=0).astype(BF16)
    small = _pack_shard(*[w[k] for k in shard_names])
    g_in_e, g_out_e, g_in_o, g_out_o, g_gates, g_small = _exchange(
        [even_w_in[0].astype(BF16), even_w_out[0].astype(BF16), odd_w_in[0].astype(BF16), odd_w_out[0].astype(BF16), gates, small],
        [], "gather_weights")
    w_in_e = jnp.pad(_unblock(g_in_e, 1), ((0, 0), (0, EVEN_NP - EVEN_IN)))
    w_out_e = g_out_e.reshape(2048, 1024)
    w_in_o = _unblock(g_in_o, 1)
    w_out_o = g_out_o.reshape(1024, 1024)
    gates_full = jnp.moveaxis(g_gates.reshape(N_DEV, 2, 4, 32, 256), 0, 2).reshape(2, 4, 256, 256)
    f_meta = _unblock(g_small[:, 0:16], 1)
    f_lru_conv_w = _unblock(g_small[:, 16:20], 1)
    f_odd_norm = _unblock(g_small[:, 20:21], 1)
    f_ssd_conv_w = _unblock(g_small[:, 21:27].reshape(N_DEV, 4, 192), 1)

    loss, grad_x, g = _local_step(
        x, loss_target, f_meta, even_norm, w_in_e, f_lru_conv_w, lru_conv_b, gates_full[0], lru_b_a, gates_full[1], lru_b_x,
        lru_lambda, f_ssd_conv_w, ssd_conv_b, _lane16(ssd_dt_bias), _lane16(ssd_a_log), _lane16(ssd_d), ssd_norm, w_out_e,
        f_odd_norm, w_in_o, w_out_o, final_norm.reshape(1, -1))
    loss = lax.psum(loss, ("x", "y", "c"))

    s_small = jnp.stack([_pack_shard(g['meta'][:, 128 * p:128 * (p + 1)], g['lru_conv_w'][:, 128 * p:128 * (p + 1)],
                                     g['odd_norm'][:, 128 * p:128 * (p + 1)], g['ssd_conv_w'][:, 192 * p:192 * (p + 1)],
                                     g['lru_w_a'][:, 32 * p:32 * (p + 1)], g['lru_w_x'][:, 32 * p:32 * (p + 1)])
                         for p in range(N_DEV)])
    r_pack = _pack_repl(*[g[k] for k in repl_names])
    p_repl, p_in_e, p_out_e, p_in_o, p_out_o, p_small = _exchange(
        [r_pack],
        [_blocks_of(g['even_w_in'], 1).astype(BF16), g['even_w_out'].reshape(N_DEV, 256, 1024).astype(BF16),
         _blocks_of(g['odd_w_in'], 1).astype(BF16), g['odd_w_out'].reshape(N_DEV, 128, 1024).astype(BF16), s_small],
        "reduce_grads")

    res = {}
    for k, parts in zip(big_names, (p_in_e, p_out_e, p_in_o, p_out_o)):
        outs = _adamw(parts, w[k][0], m[k][0], v[k][0], "adamw_" + k)
        res[k] = [o[None] for o in outs]
    outs = _adamw(p_small, small, _pack_shard(*[m[k] for k in shard_names]), _pack_shard(*[v[k] for k in shard_names]), "adamw_sharded")
    unpacked = [_unpack_shard(o) for o in outs]
    for i, k in enumerate(shard_names):
        res[k] = [u[i] for u in unpacked]
    outs = _adamw(p_repl, _pack_repl(*[w[k] for k in repl_names]), _pack_repl(*[m[k] for k in repl_names]),
                  _pack_repl(*[v[k] for k in repl_names]), "adamw_replicated")
    unpacked = [_unpack_repl(o) for o in outs]
    for i, k in enumerate(repl_names):
        res[k] = [u[i] for u in unpacked]
    return (loss, grad_x, *[res[k][0] for k in WEIGHTS], *[res[k][1] for k in WEIGHTS], *[res[k][2] for k in WEIGHTS],
            *[res[k][3] for k in WEIGHTS])
```

```python
import functools

import jax
import jax.numpy as jnp
from jax import lax
from jax.experimental import pallas as pl
from jax.experimental.pallas import tpu as pltpu

F32 = jnp.float32
BF16 = jnp.bfloat16

D_MODEL = 1024
N_META = 16
PAD = 112
EPS = 1e-6
CONV_W = 4
LRU_BLOCKS = 4
LRU_BLOCK = 256
RG_LRU_C = 8.0
SSD_HEADS = 16
SSD_P = 64
SSD_N = 128
SSD_Q = 64
SSD_GROUPS = 2
SSD_GW = 512
SSD_CONV_DIM = 1536
SB_HEADS = 16
SB_D = 64
SB_T = 128
EVEN_IN = 4624
EVEN_NP = 4864
DT_W = 256
ODD_IN = 4096
N_DEV = 8

ADAM_LR = 0.001
ADAM_B1 = 0.9
ADAM_B2 = 0.999
ADAM_EPS = 1e-08
ADAM_WD = 0.01
ADAM_STEP = 10

VMEM_LIMIT = 56 * 1024 * 1024


def _cparams(*sem):
    return pltpu.CompilerParams(dimension_semantics=sem, vmem_limit_bytes=VMEM_LIMIT)


def _row_tile(t):
    for c in (512, 256, 128):
        if t % c == 0:
            return c
    raise ValueError(t)


def _col_tile(n):
    for c in (512, 256, 128):
        if n % c == 0:
            return c
    raise ValueError(n)


def _dot(a, b):
    return jnp.dot(a, b, preferred_element_type=F32)


def _dot_nt(a, b):
    return lax.dot_general(a, b, (((1,), (1,)), ((), ())), preferred_element_type=F32)


def _dot_tn(a, b):
    return lax.dot_general(a, b, (((0,), (0,)), ((), ())), preferred_element_type=F32)


def _split3(a):
    a1 = a.astype(BF16)
    r1 = a - a1.astype(F32)
    a2 = r1.astype(BF16)
    a3 = (r1 - a2.astype(F32)).astype(BF16)
    return a1, a2, a3


def _xdot_r(a, m01):
    a1, a2, a3 = _split3(a)
    return _dot(a1, m01) + _dot(a2, m01) + _dot(a3, m01)


def _xdot_l(m01, a):
    a1, a2, a3 = _split3(a)
    return _dot(m01, a1) + _dot(m01, a2) + _dot(m01, a3)


def _xdot_tn_l(m01, a):
    a1, a2, a3 = _split3(a)
    return _dot_tn(m01, a1) + _dot_tn(m01, a2) + _dot_tn(m01, a3)


def _sigmoid(x):
    return 0.5 * jnp.tanh(0.5 * x) + 0.5


def _softplus(x):
    return jnp.maximum(x, 0.0) + jnp.log(1.0 + jnp.exp(-jnp.abs(x)))


def _iota(shape, dim):
    return lax.broadcasted_iota(jnp.int32, shape, dim)


def _block_scan(a_ref, b_ref, o_ref, reverse=False):
    n, w = a_ref.shape
    nb = n // 8
    unroll = 4 if nb % 4 == 0 else 1
    row = _iota((8, w), 0)

    def block(blk, carry):
        rows = pl.ds(pl.multiple_of(blk * 8, 8), 8)
        a, b = a_ref[rows, :], b_ref[rows, :]
        for k in (1, 2, 4):
            keep = (row < 8 - k) if reverse else (row >= k)
            shift = 8 - k if reverse else k
            a_sh = jnp.where(keep, pltpu.roll(a, shift, 0), 1.0)
            b_sh = jnp.where(keep, pltpu.roll(b, shift, 0), 0.0)
            b = a * b_sh + b
            a = a * a_sh
        o = a * carry + b
        o_ref[rows, :] = o
        return o[0:1, :] if reverse else o[7:8, :]

    def trip(i, carry):
        for u in range(unroll):
            j = i * unroll + u
            carry = block(nb - 1 - j if reverse else j, carry)
        return carry

    lax.fori_loop(0, nb // unroll, trip, jnp.zeros((1, w), F32))


def _chunks(n, c):
    return [(s, min(c, n - s)) for s in range(0, n, c)]


def _mm_nn(a, b, out_dtype, name, res=None):
    m, k = a.shape
    _, n = b.shape
    tm = _row_tile(m)

    def body(*refs):
        if res is None:
            a_ref, b_ref, o_ref = refs
        else:
            a_ref, b_ref, r_ref, o_ref = refs
        av = a_ref[...].astype(BF16)
        for s, w in _chunks(n, 512):
            acc = _dot(av, b_ref[:, s:s + w])
            if res is not None:
                acc = acc + r_ref[:, s:s + w]
            o_ref[:, s:s + w] = acc.astype(out_dtype)

    in_specs = [pl.BlockSpec((tm, k), lambda i: (i, 0)), pl.BlockSpec((k, n), lambda i: (0, 0))]
    args = [a, b]
    if res is not None:
        in_specs.append(pl.BlockSpec((tm, n), lambda i: (i, 0)))
        args.append(res)
    return pl.pallas_call(
        body, grid=(m // tm,), in_specs=in_specs,
        out_specs=pl.BlockSpec((tm, n), lambda i: (i, 0)),
        out_shape=jax.ShapeDtypeStruct((m, n), out_dtype),
        compiler_params=_cparams("parallel"), name=name)(*args)


def _mm_nt(a, b, out_dtype, name):
    m, k = a.shape
    n, _ = b.shape
    tm, tn = _row_tile(m), _col_tile(n)

    def body(a_ref, b_ref, o_ref):
        o_ref[...] = _dot_nt(a_ref[...].astype(BF16), b_ref[...].astype(BF16)).astype(out_dtype)

    return pl.pallas_call(
        body, grid=(m // tm, n // tn),
        in_specs=[pl.BlockSpec((tm, k), lambda i, j: (i, 0)), pl.BlockSpec((tn, k), lambda i, j: (j, 0))],
        out_specs=pl.BlockSpec((tm, tn), lambda i, j: (i, j)),
        out_shape=jax.ShapeDtypeStruct((m, n), out_dtype),
        compiler_params=_cparams("parallel", "parallel"), name=name)(a, b)


def _mm_tn(a, b, name):
    t, m = a.shape
    _, n = b.shape
    tk = _row_tile(t)
    halves = 2 if (m * n * 4 > 8 * 1024 * 1024 and n % 256 == 0) else 1
    tn = n // halves

    def body(a_ref, b_ref, o_ref):
        @pl.when(pl.program_id(1) == 0)
        def _():
            o_ref[...] = jnp.zeros_like(o_ref)
        at = a_ref[...].astype(BF16).T
        for s, w in _chunks(tn, 512):
            o_ref[:, s:s + w] += _dot(at, b_ref[:, s:s + w].astype(BF16))

    return pl.pallas_call(
        body, grid=(halves, t // tk),
        in_specs=[pl.BlockSpec((tk, m), lambda j, s: (s, 0)), pl.BlockSpec((tk, tn), lambda j, s: (s, j))],
        out_specs=pl.BlockSpec((m, tn), lambda j, s: (0, j)),
        out_shape=jax.ShapeDtypeStruct((m, n), F32),
        compiler_params=_cparams("parallel", "arbitrary"), name=name)(a, b)


def _norm_fwd(h, w, name):
    t, d = h.shape
    tm = _row_tile(t)

    def body(h_ref, w_ref, u_ref):
        x = h_ref[...]
        r = lax.rsqrt(jnp.mean(x * x, axis=-1, keepdims=True) + EPS)
        u_ref[...] = (x * r * w_ref[...]).astype(BF16)

    return pl.pallas_call(
        body, grid=(t // tm,),
        in_specs=[pl.BlockSpec((tm, d), lambda i: (i, 0)), pl.BlockSpec((1, d), lambda i: (0, 0))],
        out_specs=pl.BlockSpec((tm, d), lambda i: (i, 0)),
        out_shape=jax.ShapeDtypeStruct((t, d), BF16),
        compiler_params=_cparams("parallel"), name=name)(h, w)


def _norm_bwd(h, w, du, dres, name):
    t, d = h.shape
    tm = _row_tile(t)

    def body(h_ref, w_ref, du_ref, dr_ref, dh_ref, dw_ref):
        @pl.when(pl.program_id(0) == 0)
        def _():
            dw_ref[...] = jnp.zeros_like(dw_ref)
        x = h_ref[...]
        r = lax.rsqrt(jnp.mean(x * x, axis=-1, keepdims=True) + EPS)
        xh = x * r
        du_ = du_ref[...]
        g = du_ * w_ref[...]
        dh_ref[...] = dr_ref[...] + r * (g - xh * jnp.mean(g * xh, axis=-1, keepdims=True))
        dw_ref[...] += jnp.sum(du_ * xh, axis=0, keepdims=True)

    return pl.pallas_call(
        body, grid=(t // tm,),
        in_specs=[pl.BlockSpec((tm, d), lambda i: (i, 0)), pl.BlockSpec((1, d), lambda i: (0, 0)),
                  pl.BlockSpec((tm, d), lambda i: (i, 0)), pl.BlockSpec((tm, d), lambda i: (i, 0))],
        out_specs=[pl.BlockSpec((tm, d), lambda i: (i, 0)), pl.BlockSpec((1, d), lambda i: (0, 0))],
        out_shape=[jax.ShapeDtypeStruct((t, d), F32), jax.ShapeDtypeStruct((1, d), F32)],
        compiler_params=_cparams("arbitrary"), name=name)(h, w, du, dres)


def _final_loss(h2, w, tgt, lp):
    t, d = h2.shape
    bsz, seq, _ = tgt.shape
    nblk = lp // SB_T

    def body(h_ref, w_ref, t_ref, loss_ref, dh_ref, dw_ref):
        b, i = pl.program_id(0), pl.program_id(1)

        @pl.when((b == 0) & (i == 0))
        def _():
            loss_ref[...] = jnp.zeros_like(loss_ref)
            dw_ref[...] = jnp.zeros_like(dw_ref)

        @pl.when(i == 0)
        def _():
            dh_ref[...] = jnp.zeros_like(dh_ref)

        @pl.when(i > 0)
        def _():
            x = h_ref[...]
            r = lax.rsqrt(jnp.mean(x * x, axis=-1, keepdims=True) + EPS)
            xh = x * r
            wv = w_ref[...]
            diff = xh * wv - t_ref[0]
            loss_ref[...] += 0.5 * jnp.sum(jnp.mean(diff * diff, axis=-1, keepdims=True), axis=0, keepdims=True)
            dy = diff * (1.0 / d)
            g = dy * wv
            dh_ref[...] = r * (g - xh * jnp.mean(g * xh, axis=-1, keepdims=True))
            dw_ref[...] += jnp.sum(dy * xh, axis=0, keepdims=True)

    return pl.pallas_call(
        body, grid=(bsz, nblk),
        in_specs=[pl.BlockSpec((SB_T, d), lambda b, i: (b * nblk + i, 0)), pl.BlockSpec((1, d), lambda b, i: (0, 0)),
                  pl.BlockSpec((1, SB_T, d), lambda b, i: (b, jnp.maximum(i - 1, 0), 0))],
        out_specs=[pl.BlockSpec((1, 128), lambda b, i: (0, 0)), pl.BlockSpec((SB_T, d), lambda b, i: (b * nblk + i, 0)),
                   pl.BlockSpec((1, d), lambda b, i: (0, 0))],
        out_shape=[jax.ShapeDtypeStruct((1, 128), F32), jax.ShapeDtypeStruct((t, d), F32), jax.ShapeDtypeStruct((1, d), F32)],
        compiler_params=_cparams("arbitrary", "arbitrary"), name="final_loss")(h2, w, tgt)


def _shift_down(x, j):
    return x if j == 0 else pltpu.roll(x, j, 0)


def _shift_up(x, j):
    return x if j == 0 else pltpu.roll(x, x.shape[0] - j, 0)


def _conv(x, cw, cb):
    out = cb + cw[CONV_W - 1:CONV_W, :] * x
    for j in range(1, CONV_W):
        out = out + cw[CONV_W - 1 - j:CONV_W - j, :] * _shift_down(x, j)
    return out


def _conv_bwd_x(dy, cw):
    out = cw[CONV_W - 1:CONV_W, :] * dy
    for j in range(1, CONV_W):
        out = out + cw[CONV_W - 1 - j:CONV_W - j, :] * _shift_up(dy, j)
    return out


def _conv_bwd_w(dcw_ref, dy, x):
    for j in range(CONV_W):
        k = CONV_W - 1 - j
        dcw_ref[k:k + 1, :] += jnp.sum(dy * _shift_down(x, j), axis=0, keepdims=True)


def _lru_forward_block(x, cw, cb, wa, ba, wx, bx, lam, a_ref, b_ref, h_ref):
    lp = x.shape[0]
    lx = _conv(x, cw, cb)
    lxb = lx.astype(BF16)
    r = _sigmoid(_dot(lxb, wa) + ba)
    i = _sigmoid(_dot(lxb, wx) + bx)
    spl = _softplus(-lam)
    log_a = (-RG_LRU_C) * r * spl
    a = jnp.exp(log_a)
    mult = jnp.sqrt(-jnp.tanh(log_a) * (1.0 + a * a))
    valid = _iota((lp, 1), 0) >= PAD
    a_ref[...] = a
    b_ref[...] = jnp.where(valid, mult * i * lx, 0.0)
    _block_scan(a_ref, b_ref, h_ref)
    return lx, lxb, r, i, spl, a, mult, valid


def _lru_specs(lp):
    w = LRU_BLOCK
    return [
        pl.BlockSpec((lp, w), lambda g, b: (b, g)),
        pl.BlockSpec((lp, w), lambda g, b: (b, LRU_BLOCKS + g)),
        pl.BlockSpec((CONV_W, w), lambda g, b: (0, g)),
        pl.BlockSpec((1, w), lambda g, b: (0, g)),
        pl.BlockSpec((1, w, w), lambda g, b: (g, 0, 0)),
        pl.BlockSpec((1, w), lambda g, b: (0, g)),
        pl.BlockSpec((1, w, w), lambda g, b: (g, 0, 0)),
        pl.BlockSpec((1, w), lambda g, b: (0, g)),
        pl.BlockSpec((1, w), lambda g, b: (0, g)),
    ]


def _lru_fwd(proj, cw, cb, wa, ba, wx, bx, lam, bsz, lp):
    w = LRU_BLOCK

    def body(x_ref, g_ref, cw_ref, cb_ref, wa_ref, ba_ref, wx_ref, bx_ref, lam_ref, y_ref, a_s, b_s, h_s):
        _lru_forward_block(x_ref[...], cw_ref[...], cb_ref[...], wa_ref[0], ba_ref[...], wx_ref[0], bx_ref[...],
                           lam_ref[...], a_s, b_s, h_s)
        gate = g_ref[...]
        y_ref[...] = (h_s[...] * gate * _sigmoid(gate)).astype(BF16)

    return pl.pallas_call(
        body, grid=(LRU_BLOCKS, bsz), in_specs=_lru_specs(lp),
        out_specs=pl.BlockSpec((lp, w), lambda g, b: (b, g)),
        out_shape=jax.ShapeDtypeStruct((bsz * lp, LRU_BLOCKS * w), BF16),
        scratch_shapes=[pltpu.VMEM((lp, w), F32)] * 3,
        compiler_params=_cparams("parallel", "arbitrary"), name="lru_fwd")(proj, proj, cw, cb, wa, ba, wx, bx, lam)


def _lru_bwd(proj, dmixed, cw, cb, wa, ba, wx, bx, lam, bsz, lp):
    w = LRU_BLOCK

    def body(x_ref, g_ref, cw_ref, cb_ref, wa_ref, ba_ref, wx_ref, bx_ref, lam_ref, dy_ref,
             dx_ref, dg_ref, dcw_ref, dcb_ref, dwa_ref, dba_ref, dwx_ref, dbx_ref, dlam_ref, a_s, b_s, h_s, dh_s):
        @pl.when(pl.program_id(1) == 0)
        def _():
            for ref in (dcw_ref, dcb_ref, dwa_ref, dba_ref, dwx_ref, dbx_ref, dlam_ref):
                ref[...] = jnp.zeros_like(ref)

        x = x_ref[...]
        cwv = cw_ref[...]
        wav, wxv, lam_ = wa_ref[0], wx_ref[0], lam_ref[...]
        lx, lxb, r, i, spl, a, mult, valid = _lru_forward_block(
            x, cwv, cb_ref[...], wav, ba_ref[...], wxv, bx_ref[...], lam_, a_s, b_s, h_s)
        gate = g_ref[...]
        sg = _sigmoid(gate)
        dy = dy_ref[...]
        h = h_s[...]
        dg_ref[...] = (dy * h * sg * (1.0 + gate * (1.0 - sg))).astype(BF16)
        b_s[...] = dy * gate * sg
        a_s[...] = jnp.where(_iota((lp, 1), 0) < lp - 1, _shift_up(a, 1), 0.0)
        _block_scan(a_s, b_s, dh_s, reverse=True)
        dh = dh_s[...]
        hprev = jnp.where(_iota((lp, 1), 0) >= 1, _shift_down(h, 1), 0.0)
        db = jnp.where(valid, dh, 0.0)
        dmult = db * i * lx
        di = db * mult * lx
        dlx = db * mult * i
        inv_mult = jnp.where(mult > 0.0, 1.0 / mult, 0.0)
        dlog_a = dh * hprev * a - dmult * (a * a) * inv_mult
        drp = dlog_a * ((-RG_LRU_C) * spl) * r * (1.0 - r)
        dip = di * i * (1.0 - i)
        dspl = jnp.sum(dlog_a * ((-RG_LRU_C) * r), axis=0, keepdims=True)
        dlam_ref[...] += dspl * (-_sigmoid(-lam_))
        dba_ref[...] += jnp.sum(drp, axis=0, keepdims=True)
        dbx_ref[...] += jnp.sum(dip, axis=0, keepdims=True)
        drpb, dipb = drp.astype(BF16), dip.astype(BF16)
        dwa_ref[0] += _dot_tn(lxb, drpb)
        dwx_ref[0] += _dot_tn(lxb, dipb)
        dlx = dlx + _dot_nt(drpb, wav) + _dot_nt(dipb, wxv)
        dcb_ref[...] += jnp.sum(dlx, axis=0, keepdims=True)
        _conv_bwd_w(dcw_ref, dlx, x)
        dx_ref[...] = _conv_bwd_x(dlx, cwv).astype(BF16)

    t = bsz * lp
    vec = pl.BlockSpec((1, w), lambda g, b: (0, g))
    mat = pl.BlockSpec((1, w, w), lambda g, b: (g, 0, 0))
    act = pl.BlockSpec((lp, w), lambda g, b: (b, g))
    return pl.pallas_call(
        body, grid=(LRU_BLOCKS, bsz), in_specs=_lru_specs(lp) + [act],
        out_specs=[act, act, pl.BlockSpec((CONV_W, w), lambda g, b: (0, g)), vec, mat, vec, mat, vec, vec],
        out_shape=[jax.ShapeDtypeStruct((t, 1024), BF16), jax.ShapeDtypeStruct((t, 1024), BF16),
                   jax.ShapeDtypeStruct((CONV_W, 1024), F32), jax.ShapeDtypeStruct((1, 1024), F32),
                   jax.ShapeDtypeStruct((LRU_BLOCKS, w, w), F32), jax.ShapeDtypeStruct((1, 1024), F32),
                   jax.ShapeDtypeStruct((LRU_BLOCKS, w, w), F32), jax.ShapeDtypeStruct((1, 1024), F32),
                   jax.ShapeDtypeStruct((1, 1024), F32)],
        scratch_shapes=[pltpu.VMEM((lp, w), F32)] * 4,
        compiler_params=_cparams("parallel", "arbitrary"), name="lru_bwd")(
            proj, proj, cw, cb, wa, ba, wx, bx, lam, dmixed)


def _sb_masks():
    j = _iota((SB_T, 2 * SB_T), 0)
    s = _iota((SB_T, 2 * SB_T), 1)
    right = ((s < SB_T) & (j > s)) | (s >= SB_T)
    left = ((s < SB_T) & (j < s)) | (s >= SB_T)
    return jnp.where(right, -1.0, 0.0).astype(BF16), left.astype(BF16)


def _xdot2(a, m01):
    a1 = a.astype(BF16)
    a2 = (a - a1.astype(F32)).astype(BF16)
    return _dot(a1, m01) + _dot(a2, m01)


def _sb_query_block(lp):
    for tq in (512, 256, 128):
        if (lp - SB_T) % tq == 0:
            return tq, (lp - SB_T) // tq
    raise ValueError(lp)


def _sb_key_rows(kj):
    return pl.ds(kj * SB_T if isinstance(kj, int) else pl.multiple_of(kj * SB_T, SB_T), SB_T)


def _sb_valid(q0, tq, kj):
    t_pos = q0 + _iota((tq, SB_T), 0)
    s_pos = kj * SB_T + _iota((tq, SB_T), 1)
    return (s_pos < t_pos) & (s_pos >= PAD)


def _sb_by_head(ref, rows):
    t = ref[rows, :]
    head0 = _iota((SB_T, 128), 1) < SB_D
    return jnp.concatenate([jnp.where(head0, t, 0.0), jnp.where(head0, 0.0, t)], axis=0).astype(BF16)


def _sb_tile(qb, k_ref, q0, kj, carries, m_right_neg, masked):
    rows_k = _sb_key_rows(kj)
    z2 = _dot_nt(qb, _sb_by_head(k_ref, rows_k))
    valid = _sb_valid(q0, qb.shape[0], kj) if masked else None
    out = []
    for h in range(2):
        z = z2[:, h * SB_T:(h + 1) * SB_T]
        sp = _softplus(z)
        rs = _dot((jnp.where(valid, sp, 0.0) if masked else sp).astype(BF16), m_right_neg)
        lb = z - sp
        wgt = jnp.exp(lb + rs[:, :SB_T] + carries[h])
        if masked:
            wgt = jnp.where(valid, wgt, 0.0)
        out.append((lb, wgt, carries[h] + rs[:, SB_T:]))
    return rows_k, valid, out


def _sb_sweep(step, c, blk, tq, first, leftwards):
    if first:
        return step(0, True)(c)
    r = tq // SB_T
    u = 2 if r % 2 == 0 else 1
    lo = 1 + r * blk

    def trips(n, first_tile, masked):
        def trip(i, c):
            for j in range(u):
                d = u * i + j
                c = step(first_tile - d if leftwards else first_tile + d, masked)(c)
            return c
        return lambda c: lax.fori_loop(0, n, trip, c)

    diag = trips(r // u, lo + r - 1 if leftwards else lo, True)
    inner = trips((r // u) * blk, lo - 1 if leftwards else 1, False)
    if leftwards:
        return step(0, True)(inner(diag(c)))
    return diag(inner(step(0, True)(c)))


def _sb_specs(lp):
    nh = SB_HEADS // 2
    return [pl.BlockSpec((lp, 128), lambda b, p: (b, p)), pl.BlockSpec((lp, 128), lambda b, p: (b, nh + p)),
            pl.BlockSpec((lp, 128), lambda b, p: (b, 2 * nh + p)), pl.BlockSpec((lp, 128), lambda b, p: (b, 3 * nh + p))]


def _sb_fwd(qkvg, bsz, lp):
    tq, nb = _sb_query_block(lp)
    scale = SB_D ** -0.5

    def body(q_ref, k_ref, v_ref, g_ref, o_ref, og_ref):
        m_right_neg, _ = _sb_masks()

        def q_block(blk, rows, first):
            q0 = 0 if first else (1 + (rows // SB_T) * blk) * SB_T
            rows_q = pl.ds(q0 if first else pl.multiple_of(q0, SB_T), rows)
            qb = (q_ref[rows_q, :] * scale).astype(BF16)

            def step(kj, masked):
                def run(c):
                    acc, car0, car1 = c
                    rows_k, _, ((_, w0, car0), (_, w1, car1)) = _sb_tile(qb, k_ref, q0, kj, (car0, car1), m_right_neg, masked)
                    w2 = jnp.concatenate([w0.astype(BF16), w1.astype(BF16)], axis=1)
                    return acc + _dot(w2, _sb_by_head(v_ref, rows_k)), car0, car1
                return run

            zero = jnp.zeros((rows, SB_T), F32)
            o_ref[rows_q, :] = _sb_sweep(step, (zero, zero, zero), blk, rows, first, True)[0]

        q_block(0, SB_T, True)

        def big_block(i, _):
            q_block(i, tq, False)
            return 0

        lax.fori_loop(0, nb, big_block, 0)
        gate = g_ref[...]
        og_ref[...] = (o_ref[...] * gate * _sigmoid(gate)).astype(BF16)

    t = bsz * lp
    blk = pl.BlockSpec((lp, 128), lambda b, p: (b, p))
    return pl.pallas_call(
        body, grid=(bsz, SB_HEADS // 2), in_specs=_sb_specs(lp), out_specs=[blk, blk],
        out_shape=[jax.ShapeDtypeStruct((t, 1024), F32), jax.ShapeDtypeStruct((t, 1024), BF16)],
        compiler_params=_cparams("parallel", "parallel"), name="sb_fwd")(qkvg, qkvg, qkvg, qkvg)


def _sb_bwd(qkvg, o, dog, bsz, lp):
    tq, nb = _sb_query_block(lp)
    nk = lp // SB_T
    scale = SB_D ** -0.5

    def body(q_ref, k_ref, v_ref, g_ref, o_ref, dog_ref, dq_ref, dk_ref, dv_ref, dg_ref, do_s, dk_s, dv_s, e_s, sig_s):
        m_right_neg, m_left = _sb_masks()
        gate = g_ref[...]
        sg = _sigmoid(gate)
        dog = dog_ref[...]
        do_s[...] = dog * gate * sg
        dg_ref[...] = (dog * o_ref[...] * sg * (1.0 + gate * (1.0 - sg))).astype(BF16)
        dk_s[...] = jnp.zeros_like(dk_s)
        dv_s[...] = jnp.zeros_like(dv_s)
        def q_block(blk, rows, first):
            q0 = 0 if first else (1 + (rows // SB_T) * blk) * SB_T
            rows_q = pl.ds(q0 if first else pl.multiple_of(q0, SB_T), rows)
            head0 = _iota((rows, 128), 1) < SB_D
            qf = q_ref[rows_q, :] * scale
            qb = qf.astype(BF16)
            q_h = (jnp.where(head0, qf, 0.0).astype(BF16), jnp.where(head0, 0.0, qf).astype(BF16))
            do_f = do_s[rows_q, :]
            do_b = do_f.astype(BF16)
            do_h = (jnp.where(head0, do_f, 0.0).astype(BF16), jnp.where(head0, 0.0, do_f).astype(BF16))

            def left(kj, masked):
                def run(c):
                    rows_k, _, heads = _sb_tile(qb, k_ref, q0, kj, c, m_right_neg, masked)
                    dw2 = _dot_nt(do_b, _sb_by_head(v_ref, rows_k))
                    dv = None
                    for h, (lb, wgt, _) in enumerate(heads):
                        e_s[2 * kj + h, 0:rows, :] = wgt * dw2[:, h * SB_T:(h + 1) * SB_T]
                        sig_s[2 * kj + h, 0:rows, :] = jnp.exp(lb)
                        part = _dot_tn(wgt.astype(BF16), do_h[h])
                        dv = part if dv is None else dv + part
                    dv_s[rows_k, :] += dv
                    return heads[0][2], heads[1][2]
                return run

            zero = jnp.zeros((rows, SB_T), F32)
            _sb_sweep(left, (zero, zero), blk, rows, first, True)

            def right(kj, masked):
                def run(c):
                    dq, ecar = c[0], list(c[1:])
                    rows_k = _sb_key_rows(kj)
                    valid = _sb_valid(q0, rows, kj) if masked else None
                    dzs, dk = [], None
                    for h in range(2):
                        e = e_s[2 * kj + h, 0:rows, :]
                        se = _dot(e.astype(BF16), m_left)
                        dz = e - sig_s[2 * kj + h, 0:rows, :] * (e + se[:, :SB_T] + ecar[h])
                        if masked:
                            dz = jnp.where(valid, dz, 0.0)
                        dz = dz.astype(BF16)
                        part = _dot_tn(dz, q_h[h])
                        dk = part if dk is None else dk + part
                        dzs.append(dz)
                        ecar[h] = ecar[h] + se[:, SB_T:]
                    dk_s[rows_k, :] += dk
                    return dq + _dot(jnp.concatenate(dzs, axis=1), _sb_by_head(k_ref, rows_k)), ecar[0], ecar[1]
                return run

            dq = _sb_sweep(right, (zero, zero, zero), blk, rows, first, False)[0]
            dq_ref[rows_q, :] = (dq * scale).astype(BF16)

        q_block(0, SB_T, True)

        def big_block(i, _):
            q_block(i, tq, False)
            return 0

        lax.fori_loop(0, nb, big_block, 0)
        dk_ref[...] = dk_s[...].astype(BF16)
        dv_ref[...] = dv_s[...].astype(BF16)

    t = bsz * lp
    blk = pl.BlockSpec((lp, 128), lambda b, p: (b, p))
    shp = jax.ShapeDtypeStruct((t, 1024), BF16)
    return pl.pallas_call(
        body, grid=(bsz, SB_HEADS // 2), in_specs=_sb_specs(lp) + [blk, blk], out_specs=[blk] * 4,
        out_shape=[shp] * 4,
        scratch_shapes=[pltpu.VMEM((lp, 128), F32)] * 3 + [pltpu.VMEM((2 * nk, tq, SB_T), F32)] * 2,
        compiler_params=_cparams("parallel", "parallel"), name="sb_bwd")(qkvg, qkvg, qkvg, qkvg, o, dog)


XBC_COL0 = 3072 // 256
DT_COL0 = 4608 // 128
DT_L = 128


def _ssd_pre_fwd(proj, cw, cb, bsz, lp):
    def body(x_ref, cw_ref, cb_ref, o_ref):
        pre = _conv(x_ref[...], cw_ref[...], cb_ref[...])
        o_ref[...] = pre * _sigmoid(pre)

    return pl.pallas_call(
        body, grid=(bsz, SSD_CONV_DIM // 256),
        in_specs=[pl.BlockSpec((lp, 256), lambda b, j: (b, XBC_COL0 + j)), pl.BlockSpec((CONV_W, 256), lambda b, j: (0, j)),
                  pl.BlockSpec((1, 256), lambda b, j: (0, j))],
        out_specs=pl.BlockSpec((lp, 256), lambda b, j: (b, j)),
        out_shape=jax.ShapeDtypeStruct((bsz * lp, SSD_CONV_DIM), F32),
        compiler_params=_cparams("parallel", "parallel"), name="ssd_pre_fwd")(proj, cw, cb)


def _ssd_pre_bwd(proj, dact, cw, cb, bsz, lp):
    def body(x_ref, d_ref, cw_ref, cb_ref, dx_ref, dcw_ref, dcb_ref):
        @pl.when(pl.program_id(1) == 0)
        def _():
            dcw_ref[...] = jnp.zeros_like(dcw_ref)
            dcb_ref[...] = jnp.zeros_like(dcb_ref)
        x = x_ref[...]
        cwv = cw_ref[...]
        pre = _conv(x, cwv, cb_ref[...])
        s = _sigmoid(pre)
        dpre = d_ref[...].astype(F32) * s * (1.0 + pre * (1.0 - s))
        dcb_ref[...] += jnp.sum(dpre, axis=0, keepdims=True)
        _conv_bwd_w(dcw_ref, dpre, x)
        dx_ref[...] = _conv_bwd_x(dpre, cwv).astype(BF16)

    return pl.pallas_call(
        body, grid=(SSD_CONV_DIM // 256, bsz),
        in_specs=[pl.BlockSpec((lp, 256), lambda j, b: (b, XBC_COL0 + j)), pl.BlockSpec((lp, 256), lambda j, b: (b, j)),
                  pl.BlockSpec((CONV_W, 256), lambda j, b: (0, j)), pl.BlockSpec((1, 256), lambda j, b: (0, j))],
        out_specs=[pl.BlockSpec((lp, 256), lambda j, b: (b, j)), pl.BlockSpec((CONV_W, 256), lambda j, b: (0, j)),
                   pl.BlockSpec((1, 256), lambda j, b: (0, j))],
        out_shape=[jax.ShapeDtypeStruct((bsz * lp, SSD_CONV_DIM), BF16), jax.ShapeDtypeStruct((CONV_W, SSD_CONV_DIM), F32),
                   jax.ShapeDtypeStruct((1, SSD_CONV_DIM), F32)],
        compiler_params=_cparams("parallel", "arbitrary"), name="ssd_pre_bwd")(proj, dact, cw, cb)


def _xdot_nt_r(a, m01):
    a1, a2, a3 = _split3(a)
    return _dot_nt(a1, m01) + _dot_nt(a2, m01) + _dot_nt(a3, m01)


class _SsdConsts:
    def __init__(self, g):
        q, gw = SSD_Q, SSD_GW
        head_of_lane = lax.shift_right_logical(_iota((DT_L, gw), 1), 6)
        self.sel = (_iota((DT_L, gw), 0) == 8 * g + head_of_lane).astype(BF16)
        r = _iota((q, gw), 0)
        c = jnp.bitwise_and(_iota((q, gw), 1), q - 1)
        self.diag = r == c
        self.diag_b = self.diag.astype(BF16)
        self.lower = c <= r
        self.upper = c >= r
        self.bd = lax.shift_right_logical(_iota((gw, gw), 0), 6) == lax.shift_right_logical(_iota((gw, gw), 1), 6)
        jj, ll = _iota((q, q), 1), _iota((q, q), 0)
        self.tri = (jj <= ll).astype(BF16)
        self.tri_t = (jj >= ll).astype(BF16)
        self.ones = jnp.ones((q, q), BF16)
        self.last = _iota((q, 1), 0) == q - 1


def _ssd_prepass(k, dt_ref, bias_ref, alog_ref, dtm_s, abc_s, lp):
    valid = _iota((lp, 1), 0) >= PAD
    sp_in = dt_ref[...] + bias_ref[...]
    dtm = jnp.where(valid, _softplus(sp_in), 0.0)
    aneg = -jnp.exp(alog_ref[...])
    dtm_s[...] = dtm
    abc_s[...] = _xdot_r(dtm * aneg, k.sel)
    return valid, sp_in, aneg


def _ssd_loop(nc, chunk, init):
    u = 2 if nc % 2 == 0 else 1

    def trip(i, carry):
        for j in range(u):
            carry = chunk(i * u + j, carry)
        return carry

    return lax.fori_loop(0, nc // u, trip, init)


def _ssd_chunk(k, rows, xs_ref, b_ref, c_ref, dtm_s, abc_s):
    bc = _xdot_l(k.tri, abc_s[rows, :])
    tt = _xdot_l(k.ones, jnp.where(k.diag, bc, 0.0))
    dtbc = _xdot_r(dtm_s[rows, :], k.sel)
    xs = xs_ref[rows, :]
    x = xs * dtbc
    bb = b_ref[rows, :].astype(BF16)
    cc = c_ref[rows, :].astype(BF16)
    tot = bc[SSD_Q - 1:SSD_Q, :]
    xbd = jnp.where(k.bd, jnp.concatenate([x] * 8, axis=0), 0.0).astype(BF16)
    return bc, tt, dtbc, xs, x, bb, cc, tot, xbd


def _ssd_specs(lp, order):
    ix = (lambda f: (lambda b, g: f(b, g))) if order == "bg" else (lambda f: (lambda g, b: f(b, g)))
    return [pl.BlockSpec((lp, SSD_GW), ix(lambda b, g: (b, g))),
            pl.BlockSpec((lp, SSD_N), ix(lambda b, g: (b, 1024 // SSD_N + g))),
            pl.BlockSpec((lp, SSD_N), ix(lambda b, g: (b, 1280 // SSD_N + g))),
            pl.BlockSpec((lp, DT_L), ix(lambda b, g: (b, DT_COL0))),
            pl.BlockSpec((1, DT_L), ix(lambda b, g: (0, 0))),
            pl.BlockSpec((1, DT_L), ix(lambda b, g: (0, 0))),
            pl.BlockSpec((1, DT_L), ix(lambda b, g: (0, 0)))]


def _ssd_fwd(xbc, proj, dt_bias, a_log, dskip, bsz, lp):
    nc = lp // SSD_Q

    def body(xs_ref, b_ref, c_ref, dt_ref, bias_ref, alog_ref, dsk_ref, y_ref, dtm_s, abc_s):
        k = _SsdConsts(pl.program_id(1))
        _ssd_prepass(k, dt_ref, bias_ref, alog_ref, dtm_s, abc_s, lp)
        dvec = _xdot_r(jnp.broadcast_to(dsk_ref[...], (8, DT_L)), k.sel)[0:1, :]

        def chunk(c, state):
            rows = pl.ds(pl.multiple_of(c * SSD_Q, SSD_Q), SSD_Q)
            bc, tt, _, xs, x, bb, cc, tot, xbd = _ssd_chunk(k, rows, xs_ref, b_ref, c_ref, dtm_s, abc_s)
            lm = jnp.where(k.lower, jnp.exp(jnp.minimum(bc - tt, 0.0)), 0.0)
            g_all = _dot(_dot_nt(cc, bb).astype(BF16), k.diag_b) * lm
            y = _dot(g_all.astype(BF16), xbd) + jnp.exp(bc) * _dot(cc, state.astype(BF16)) + dvec * xs
            y_ref[rows, :] = y
            return jnp.exp(tot) * state + _dot_tn(bb, (jnp.exp(tot - bc) * x).astype(BF16))

        _ssd_loop(nc, chunk, jnp.zeros((SSD_N, SSD_GW), F32))

    return pl.pallas_call(
        body, grid=(bsz, SSD_GROUPS), in_specs=_ssd_specs(lp, "bg"),
        out_specs=pl.BlockSpec((lp, SSD_GW), lambda b, g: (b, g)),
        out_shape=jax.ShapeDtypeStruct((bsz * lp, 1024), F32),
        scratch_shapes=[pltpu.VMEM((lp, DT_L), F32), pltpu.VMEM((lp, SSD_GW), F32)],
        compiler_params=_cparams("parallel", "parallel"), name="ssd_fwd")(xbc, xbc, xbc, proj, dt_bias, a_log, dskip)


def _ssd_bwd(xbc, proj, dt_bias, a_log, dskip, dy, bsz, lp):
    nc = lp // SSD_Q

    def body(xs_ref, b_ref, c_ref, dt_ref, bias_ref, alog_ref, dsk_ref, dy_ref,
             dxs_ref, db_ref, dc_ref, ddt_ref, dbias_ref, dalog_ref, ddsk_ref, dtm_s, abc_s, st_s):
        @pl.when(pl.program_id(1) == 0)
        def _():
            for ref in (dbias_ref, dalog_ref, ddsk_ref):
                ref[...] = jnp.zeros_like(ref)

        k = _SsdConsts(pl.program_id(0))
        valid, sp_in, aneg = _ssd_prepass(k, dt_ref, bias_ref, alog_ref, dtm_s, abc_s, lp)
        dvec = _xdot_r(jnp.broadcast_to(dsk_ref[...], (8, DT_L)), k.sel)[0:1, :]

        def fwd_chunk(c, state):
            rows = pl.ds(pl.multiple_of(c * SSD_Q, SSD_Q), SSD_Q)
            st_s[c] = state.astype(BF16)
            bc, _, _, _, x, bb, _, tot, _ = _ssd_chunk(k, rows, xs_ref, b_ref, c_ref, dtm_s, abc_s)
            return jnp.exp(tot) * state + _dot_tn(bb, (jnp.exp(tot - bc) * x).astype(BF16))

        _ssd_loop(nc, fwd_chunk, jnp.zeros((SSD_N, SSD_GW), F32))

        def bwd_chunk(i, carry):
            dstate, daneg, ddsk = carry
            c = nc - 1 - i
            rows = pl.ds(pl.multiple_of(c * SSD_Q, SSD_Q), SSD_Q)
            bc, tt, dtbc, xs, x, bb, cc, tot, xbd = _ssd_chunk(k, rows, xs_ref, b_ref, c_ref, dtm_s, abc_s)
            sprev = st_s[c]
            dyc = dy_ref[rows, :]
            dyb = dyc.astype(BF16)
            seg = bc - tt
            lm = jnp.where(k.lower, jnp.exp(jnp.minimum(seg, 0.0)), 0.0)
            lm_t = jnp.where(k.upper, jnp.exp(jnp.minimum(-seg, 0.0)), 0.0)
            cb_all = _dot(_dot_nt(cc, bb).astype(BF16), k.diag_b)
            cbt_all = _dot(_dot_nt(bb, cc).astype(BF16), k.diag_b)
            g_all = cb_all * lm
            dybd = jnp.where(k.bd, jnp.concatenate([dyc] * 8, axis=0), 0.0).astype(BF16)
            dg = _dot_nt(dyb, xbd)
            dx = _dot((cbt_all * lm_t).astype(BF16), dybd)
            hh = dg * g_all
            ea = jnp.exp(bc)
            yo = ea * _dot(cc, sprev)
            dacum = _xdot_nt_r(hh + dyc * yo, k.sel)
            col_h = _xdot_l(k.ones, hh)
            dacum = dacum - _xdot_nt_r(jnp.where(k.diag, col_h, 0.0), k.sel)
            dcb = _dot_nt((dg * lm).astype(BF16), k.diag_b).astype(BF16)
            dcs = (ea * dyc).astype(BF16)
            dstb = dstate.astype(BF16)
            dec = jnp.exp(tot - bc)
            w = dec * x
            dw = _dot(bb, dstb)
            dc_ref[rows, :] = (_dot(dcb, bb) + _dot_nt(dcs, sprev)).astype(BF16)
            db_ref[rows, :] = (_dot_tn(dcb, cc) + _dot_nt(w.astype(BF16), dstb)).astype(BF16)
            dx = dx + dec * dw
            kk = _xdot_nt_r(dw * w, k.sel)
            etot = jnp.exp(tot)
            dtot = _xdot_nt_r(_xdot_l(jnp.ones((8, SSD_N), BF16), dstate * sprev.astype(F32) * etot), k.sel)[0:1, :]
            dtot = dtot + jnp.sum(kk, axis=0, keepdims=True)
            dacum = dacum - kk + jnp.where(k.last, dtot, 0.0)
            da = _xdot_l(k.tri_t, dacum)
            dtm_c = dtm_s[rows, :]
            ddtm = da * aneg + _xdot_nt_r(dx * xs, k.sel)
            vrow = (c * SSD_Q + _iota((SSD_Q, 1), 0)) >= PAD
            ddt_ref[rows, :] = jnp.where(vrow, ddtm * _sigmoid(dt_ref[rows, :] + bias_ref[...]), 0.0)
            dxs_ref[rows, :] = (dx * dtbc + dvec * dyc).astype(BF16)
            daneg = daneg + jnp.sum(da * dtm_c, axis=0, keepdims=True)
            ddsk = ddsk + jnp.sum(_xdot_nt_r(dyc * xs, k.sel), axis=0, keepdims=True)
            dstate = etot * dstate + _dot_tn(cc, dcs)
            return dstate, daneg, ddsk

        zrow = jnp.zeros((1, DT_L), F32)
        _, daneg, ddsk = _ssd_loop(nc, bwd_chunk, (jnp.zeros((SSD_N, SSD_GW), F32), zrow, zrow))
        dbias_ref[...] += jnp.broadcast_to(jnp.sum(ddt_ref[...], axis=0, keepdims=True), (8, DT_L))
        dalog_ref[...] += jnp.broadcast_to(daneg * aneg, (8, DT_L))
        ddsk_ref[...] += jnp.broadcast_to(ddsk, (8, DT_L))

    t = bsz * lp
    par = pl.BlockSpec((8, DT_L), lambda g, b: (g, 0))
    par_shape = jax.ShapeDtypeStruct((8 * SSD_GROUPS, DT_L), F32)
    return pl.pallas_call(
        body, grid=(SSD_GROUPS, bsz),
        in_specs=_ssd_specs(lp, "gb") + [pl.BlockSpec((lp, SSD_GW), lambda g, b: (b, g))],
        out_specs=[pl.BlockSpec((lp, SSD_GW), lambda g, b: (b, g)), pl.BlockSpec((lp, SSD_N), lambda g, b: (b, g)),
                   pl.BlockSpec((lp, SSD_N), lambda g, b: (b, g)), pl.BlockSpec((lp, DT_L), lambda g, b: (b, g)), par, par, par],
        out_shape=[jax.ShapeDtypeStruct((t, 1024), BF16), jax.ShapeDtypeStruct((t, 256), BF16),
                   jax.ShapeDtypeStruct((t, 256), BF16), jax.ShapeDtypeStruct((t, SSD_GROUPS * DT_L), F32),
                   par_shape, par_shape, par_shape],
        scratch_shapes=[pltpu.VMEM((lp, DT_L), F32), pltpu.VMEM((lp, SSD_GW), F32), pltpu.VMEM((nc, SSD_N, SSD_GW), BF16)],
        compiler_params=_cparams("parallel", "arbitrary"), name="ssd_bwd")(
            xbc, xbc, xbc, proj, dt_bias, a_log, dskip, dy)


Z_COL0 = 2048 // SSD_GW


def _gnorm_fwd(y, proj, w, name="gnorm_fwd"):
    t = y.shape[0]
    tm = _row_tile(t)

    def body(y_ref, z_ref, w_ref, o_ref):
        z = z_ref[...]
        gt = y_ref[...] * z * _sigmoid(z)
        r = lax.rsqrt(jnp.mean(gt * gt, axis=-1, keepdims=True) + EPS)
        o_ref[...] = (gt * r * w_ref[...]).astype(BF16)

    return pl.pallas_call(
        body, grid=(t // tm, SSD_GROUPS),
        in_specs=[pl.BlockSpec((tm, SSD_GW), lambda i, g: (i, g)), pl.BlockSpec((tm, SSD_GW), lambda i, g: (i, Z_COL0 + g)),
                  pl.BlockSpec((1, SSD_GW), lambda i, g: (0, g))],
        out_specs=pl.BlockSpec((tm, SSD_GW), lambda i, g: (i, g)),
        out_shape=jax.ShapeDtypeStruct((t, 1024), BF16),
        compiler_params=_cparams("parallel", "parallel"), name=name)(y, proj, w)


def _gnorm_bwd(y, proj, w, dmixed):
    t = y.shape[0]
    tm = _row_tile(t)

    def body(y_ref, z_ref, w_ref, d_ref, dy_ref, dz_ref, dw_ref):
        @pl.when(pl.program_id(1) == 0)
        def _():
            dw_ref[...] = jnp.zeros_like(dw_ref)
        z, yv, d = z_ref[...], y_ref[...], d_ref[...]
        s = _sigmoid(z)
        sz = z * s
        gt = yv * sz
        r = lax.rsqrt(jnp.mean(gt * gt, axis=-1, keepdims=True) + EPS)
        gh = gt * r
        dgn = d * w_ref[...]
        dgt = r * (dgn - gh * jnp.mean(dgn * gh, axis=-1, keepdims=True))
        dw_ref[...] += jnp.sum(d * gh, axis=0, keepdims=True)
        dy_ref[...] = dgt * sz
        dz_ref[...] = (dgt * yv * s * (1.0 + z * (1.0 - s))).astype(BF16)

    blk = pl.BlockSpec((tm, SSD_GW), lambda g, i: (i, g))
    return pl.pallas_call(
        body, grid=(SSD_GROUPS, t // tm),
        in_specs=[blk, pl.BlockSpec((tm, SSD_GW), lambda g, i: (i, Z_COL0 + g)), pl.BlockSpec((1, SSD_GW), lambda g, i: (0, g)),
                  pl.BlockSpec((tm, SSD_GW), lambda g, i: (i, 1024 // SSD_GW + g))],
        out_specs=[blk, blk, pl.BlockSpec((1, SSD_GW), lambda g, i: (0, g))],
        out_shape=[jax.ShapeDtypeStruct((t, 1024), F32), jax.ShapeDtypeStruct((t, 1024), BF16),
                   jax.ShapeDtypeStruct((1, 1024), F32)],
        compiler_params=_cparams("parallel", "arbitrary"), name="gnorm_bwd")(y, proj, w, dmixed)


def _meta_grad(dh0, bsz, lp):
    def body(d_ref, o_ref):
        @pl.when(pl.program_id(0) == 0)
        def _():
            o_ref[...] = jnp.zeros_like(o_ref)
        o_ref[...] += d_ref[...]

    return pl.pallas_call(
        body, grid=(bsz,),
        in_specs=[pl.BlockSpec((N_META, D_MODEL), lambda b: (b * (lp // N_META) + PAD // N_META, 0))],
        out_specs=pl.BlockSpec((N_META, D_MODEL), lambda b: (0, 0)),
        out_shape=jax.ShapeDtypeStruct((N_META, D_MODEL), F32),
        compiler_params=_cparams("arbitrary"), name="meta_grad")(dh0)


def _exchange(gather, scatter, name):
    ng, ns = len(gather), len(scatter)
    n = ng + ns

    def body(*refs):
        ins, outs = refs[:n], refs[n:2 * n]
        send_sems, recv_sems, local_sems = refs[2 * n:]
        x, y, c = lax.axis_index("x"), lax.axis_index("y"), lax.axis_index("c")
        me = 4 * x + 2 * y + c

        def peer(k):
            px = 1 - x if k & 4 else x
            py = 1 - y if k & 2 else y
            pc = 1 - c if k & 1 else c
            return (px, py, pc), 4 * px + 2 * py + pc

        local, sends, recvs = [], [], []
        for a in range(n):
            is_gather = a < ng
            src_own = ins[a] if is_gather else ins[a].at[me]
            local.append(pltpu.make_async_copy(src_own, outs[a].at[me], local_sems.at[a]))
            for k in range(1, N_DEV):
                dev, pid = peer(k)
                src = ins[a] if is_gather else ins[a].at[pid]
                sends.append(pltpu.make_async_remote_copy(
                    src_ref=src, dst_ref=outs[a].at[me], send_sem=send_sems.at[a, k - 1], recv_sem=recv_sems.at[a, k - 1],
                    device_id=dev, device_id_type=pl.DeviceIdType.MESH))
                recvs.append(pltpu.make_async_remote_copy(
                    src_ref=src, dst_ref=outs[a].at[pid], send_sem=send_sems.at[a, k - 1], recv_sem=recv_sems.at[a, k - 1],
                    device_id=dev, device_id_type=pl.DeviceIdType.MESH))
        for cp in local + sends:
            cp.start()
        for cp in recvs:
            cp.wait_recv()
        for cp in sends:
            cp.wait_send()
        for cp in local:
            cp.wait()

    anyspec = pl.BlockSpec(memory_space=pl.ANY)
    out_shape = [jax.ShapeDtypeStruct((N_DEV,) + a.shape, a.dtype) for a in gather]
    out_shape += [jax.ShapeDtypeStruct(a.shape, a.dtype) for a in scatter]
    return pl.pallas_call(
        body, in_specs=[anyspec] * n, out_specs=[anyspec] * n, out_shape=out_shape,
        scratch_shapes=[pltpu.SemaphoreType.DMA((n, N_DEV - 1)), pltpu.SemaphoreType.DMA((n, N_DEV - 1)),
                        pltpu.SemaphoreType.DMA((n,))],
        compiler_params=pltpu.CompilerParams(has_side_effects=True), name=name)(*gather, *scatter)


def _adamw(parts, w, m, v, name):
    r, c = w.shape
    tr = r
    for cand in (128, 64, 32, 16, 8):
        if r % cand == 0 and r > cand:
            tr = cand
            break

    def body(p_ref, w_ref, m_ref, v_ref, g_ref, d_ref, nm_ref, nv_ref):
        g = p_ref[0].astype(F32)
        for k in range(1, N_DEV):
            g = g + p_ref[k].astype(F32)
        m_new = ADAM_B1 * m_ref[...] + (1.0 - ADAM_B1) * g
        v_new = ADAM_B2 * v_ref[...] + (1.0 - ADAM_B2) * (g * g)
        m_hat = m_new / (1.0 - ADAM_B1 ** ADAM_STEP)
        v_hat = v_new / (1.0 - ADAM_B2 ** ADAM_STEP)
        g_ref[...] = g
        d_ref[...] = -ADAM_LR * (m_hat / (jnp.sqrt(v_hat) + ADAM_EPS) + ADAM_WD * w_ref[...])
        nm_ref[...] = m_new
        nv_ref[...] = v_new

    blk = pl.BlockSpec((tr, c), lambda i: (i, 0))
    shp = jax.ShapeDtypeStruct((r, c), F32)
    return pl.pallas_call(
        body, grid=(r // tr,), in_specs=[pl.BlockSpec((N_DEV, tr, c), lambda i: (0, i, 0)), blk, blk, blk],
        out_specs=[blk] * 4, out_shape=[shp] * 4, compiler_params=_cparams("parallel"), name=name)(parts, w, m, v)


def _rows128(a):
    return a.reshape(-1, 128)


def _pad_rows(a, rows):
    return jnp.pad(a, ((0, rows - a.shape[0]), (0, 0)))


def _lane16(a):
    return jnp.pad(a.reshape(1, -1), ((0, 0), (0, 128 - a.size)))


SHARD_PACK_ROWS = 544
REPL_PACK_ROWS = 72


def _pack_shard(meta, lru_conv_w, odd_norm, ssd_conv_w, lru_w_a, lru_w_x):
    parts = [meta.reshape(16, 128), lru_conv_w.reshape(4, 128), odd_norm.reshape(1, 128), _rows128(ssd_conv_w.reshape(4, 192)),
             _rows128(lru_w_a.reshape(4, 32, 256)), _rows128(lru_w_x.reshape(4, 32, 256))]
    return _pad_rows(jnp.concatenate(parts, axis=0), SHARD_PACK_ROWS)


def _unpack_shard(p):
    return (p[0:16], p[16:20].reshape(1, 4, 128), p[20:21], p[21:27].reshape(1, 4, 192),
            p[27:283].reshape(1, 4, 32, 256), p[283:539].reshape(1, 4, 32, 256))


def _pack_repl(even_norm, lru_conv_b, lru_b_a, lru_b_x, lru_lambda, ssd_norm, final_norm, ssd_conv_b, dt_bias, a_log, ssd_d):
    parts = [_rows128(v) for v in (even_norm, lru_conv_b, lru_b_a, lru_b_x, lru_lambda, ssd_norm, final_norm, ssd_conv_b)]
    parts += [_lane16(dt_bias), _lane16(a_log), _lane16(ssd_d)]
    return _pad_rows(jnp.concatenate(parts, axis=0), REPL_PACK_ROWS)


def _unpack_repl(p):
    vec = lambda i: p[8 * i:8 * i + 8].reshape(1, 1024)
    return (vec(0), vec(1), vec(2), vec(3), vec(4), vec(5), p[48:56].reshape(1024), p[56:68].reshape(1, 1536),
            p[68:69, :16], p[69:70, :16], p[70:71, :16])


def _local_step(x, tgt, meta, even_norm, w_in_e, lru_conv_w, lru_conv_b, lru_w_a, lru_b_a, lru_w_x, lru_b_x, lru_lambda,
                ssd_conv_w, ssd_conv_b, dt_bias, a_log, ssd_d, ssd_norm, w_out_e, odd_norm, w_in_o, w_out_o, final_norm):
    bsz, seq, d = x.shape
    lp = PAD + N_META + seq
    t = bsz * lp
    h0 = jnp.concatenate([jnp.zeros((bsz, PAD, d), F32), jnp.broadcast_to(meta[None], (bsz, N_META, d)), x], axis=1).reshape(t, d)

    u0 = _norm_fwd(h0, even_norm, "norm0_fwd")
    proj = _mm_nn(u0, w_in_e, F32, "even_in")
    lru = (lru_conv_w, lru_conv_b, lru_w_a, lru_b_a, lru_w_x, lru_b_x, lru_lambda)
    ya = _lru_fwd(proj, *lru, bsz, lp)
    xbc = _ssd_pre_fwd(proj, ssd_conv_w, ssd_conv_b, bsz, lp)
    y = _ssd_fwd(xbc, proj, dt_bias, a_log, ssd_d, bsz, lp)
    yb = _gnorm_fwd(y, proj, ssd_norm)
    mixed = jnp.concatenate([ya, yb], axis=1)
    h1 = _mm_nn(mixed, w_out_e, F32, "even_out", res=h0)
    u2 = _norm_fwd(h1, odd_norm, "norm1_fwd")
    qkvg = _mm_nn(u2, w_in_o, F32, "odd_in")
    o, og = _sb_fwd(qkvg, bsz, lp)
    h2 = _mm_nn(og, w_out_o, F32, "odd_out", res=h1)
    loss, dh2, d_final_norm = _final_loss(h2, final_norm, tgt, lp)

    dog = _mm_nt(dh2, w_out_o, F32, "odd_out_dx")
    d_w_out_o = _mm_tn(og, dh2, "odd_out_dw")
    dqkvg = jnp.concatenate(_sb_bwd(qkvg, o, dog, bsz, lp), axis=1)
    du2 = _mm_nt(dqkvg, w_in_o, F32, "odd_in_dx")
    d_w_in_o = _mm_tn(u2, dqkvg, "odd_in_dw")
    dh1, d_odd_norm = _norm_bwd(h1, odd_norm, du2, dh2, "norm1_bwd")

    dmixed = _mm_nt(dh1, w_out_e, F32, "even_out_dx")
    d_w_out_e = _mm_tn(mixed, dh1, "even_out_dw")
    dlx, dgate, d_lru_conv_w, d_lru_conv_b, d_w_a, d_b_a, d_w_x, d_b_x, d_lambda = _lru_bwd(proj, dmixed, *lru, bsz, lp)
    dy, dz, d_ssd_norm = _gnorm_bwd(y, proj, ssd_norm, dmixed)
    dxs, dbm, dcm, ddt, d_dt_bias, d_a_log, d_ssd_d = _ssd_bwd(xbc, proj, dt_bias, a_log, ssd_d, dy, bsz, lp)
    dxbc, d_ssd_conv_w, d_ssd_conv_b = _ssd_pre_bwd(proj, jnp.concatenate([dxs, dbm, dcm], axis=1), ssd_conv_w, ssd_conv_b, bsz, lp)
    ddt = (ddt[:, :DT_L] + ddt[:, DT_L:]).astype(BF16)
    dproj = jnp.concatenate([dlx, dgate, dz, dxbc, ddt, jnp.zeros((t, EVEN_NP - 4608 - DT_L), BF16)], axis=1)
    du0 = _mm_nt(dproj, w_in_e, F32, "even_in_dx")
    d_w_in_e = _mm_tn(u0, dproj, "even_in_dw")
    dh0, d_even_norm = _norm_bwd(h0, even_norm, du0, dh1, "norm0_bwd")
    grad_x = dh0.reshape(bsz, lp, d)[:, PAD + N_META:]
    d_meta = _meta_grad(dh0, bsz, lp)
    heads = lambda p: (p[0:1] + p[8:9])[:, :SSD_HEADS]
    grads = dict(meta=d_meta, even_norm=d_even_norm, even_w_in=d_w_in_e[:, :EVEN_IN], lru_conv_w=d_lru_conv_w,
                 lru_conv_b=d_lru_conv_b, lru_w_a=d_w_a, lru_b_a=d_b_a, lru_w_x=d_w_x, lru_b_x=d_b_x, lru_lambda=d_lambda,
                 ssd_conv_w=d_ssd_conv_w, ssd_conv_b=d_ssd_conv_b, ssd_dt_bias=heads(d_dt_bias), ssd_a_log=heads(d_a_log),
                 ssd_d=heads(d_ssd_d), ssd_norm=d_ssd_norm, even_w_out=d_w_out_e, odd_norm=d_odd_norm, odd_w_in=d_w_in_o,
                 odd_w_out=d_w_out_o, final_norm=d_final_norm)
    return loss[0, 0], grad_x, grads


WEIGHTS = ['meta', 'even_norm', 'even_w_in', 'lru_conv_w', 'lru_conv_b', 'lru_w_a', 'lru_b_a', 'lru_w_x', 'lru_b_x', 'lru_lambda',
           'ssd_conv_w', 'ssd_conv_b', 'ssd_dt_bias', 'ssd_a_log', 'ssd_d', 'ssd_norm', 'even_w_out', 'odd_norm', 'odd_w_in',
           'odd_w_out', 'final_norm']


def _blocks_of(a, axis):
    shp = a.shape
    a = a.reshape(shp[:axis] + (N_DEV, shp[axis] // N_DEV) + shp[axis + 1:])
    return jnp.moveaxis(a, axis, 0)


def _unblock(a, axis):
    a = jnp.moveaxis(a, 0, axis)
    shp = a.shape
    return a.reshape(shp[:axis] + (shp[axis] * shp[axis + 1],) + shp[axis + 2:])


def kernel(x, meta, even_norm, even_w_in, lru_conv_w, lru_conv_b, lru_w_a, lru_b_a, lru_w_x, lru_b_x, lru_lambda, ssd_conv_w, ssd_conv_b, ssd_dt_bias, ssd_a_log, ssd_d, ssd_norm, even_w_out, odd_norm, odd_w_in, odd_w_out, final_norm, loss_target, m_meta, m_even_norm, m_even_w_in, m_lru_conv_w, m_lru_conv_b, m_lru_w_a, m_lru_b_a, m_lru_w_x, m_lru_b_x, m_lru_lambda, m_ssd_conv_w, m_ssd_conv_b, m_ssd_dt_bias, m_ssd_a_log, m_ssd_d, m_ssd_norm, m_even_w_out, m_odd_norm, m_odd_w_in, m_odd_w_out, m_final_norm, v_meta, v_even_norm, v_even_w_in, v_lru_conv_w, v_lru_conv_b, v_lru_w_a, v_lru_b_a, v_lru_w_x, v_lru_b_x, v_lru_lambda, v_ssd_conv_w, v_ssd_conv_b, v_ssd_dt_bias, v_ssd_a_log, v_ssd_d, v_ssd_norm, v_even_w_out, v_odd_norm, v_odd_w_in, v_odd_w_out, v_final_norm):
    w = dict(meta=meta, even_norm=even_norm, even_w_in=even_w_in, lru_conv_w=lru_conv_w, lru_conv_b=lru_conv_b, lru_w_a=lru_w_a,
             lru_b_a=lru_b_a, lru_w_x=lru_w_x, lru_b_x=lru_b_x, lru_lambda=lru_lambda, ssd_conv_w=ssd_conv_w,
             ssd_conv_b=ssd_conv_b, ssd_dt_bias=ssd_dt_bias, ssd_a_log=ssd_a_log, ssd_d=ssd_d, ssd_norm=ssd_norm,
             even_w_out=even_w_out, odd_norm=odd_norm, odd_w_in=odd_w_in, odd_w_out=odd_w_out, final_norm=final_norm)
    m = dict(meta=m_meta, even_norm=m_even_norm, even_w_in=m_even_w_in, lru_conv_w=m_lru_conv_w, lru_conv_b=m_lru_conv_b,
             lru_w_a=m_lru_w_a, lru_b_a=m_lru_b_a, lru_w_x=m_lru_w_x, lru_b_x=m_lru_b_x, lru_lambda=m_lru_lambda,
             ssd_conv_w=m_ssd_conv_w, ssd_conv_b=m_ssd_conv_b, ssd_dt_bias=m_ssd_dt_bias, ssd_a_log=m_ssd_a_log, ssd_d=m_ssd_d,
             ssd_norm=m_ssd_norm, even_w_out=m_even_w_out, odd_norm=m_odd_norm, odd_w_in=m_odd_w_in, odd_w_out=m_odd_w_out,
             final_norm=m_final_norm)
    v = dict(meta=v_meta, even_norm=v_even_norm, even_w_in=v_even_w_in, lru_conv_w=v_lru_conv_w, lru_conv_b=v_lru_conv_b,
             lru_w_a=v_lru_w_a, lru_b_a=v_lru_b_a, lru_w_x=v_lru_w_x, lru_b_x=v_lru_b_x, lru_lambda=v_lru_lambda,
             ssd_conv_w=v_ssd_conv_w, ssd_conv_b=v_ssd_conv_b, ssd_dt_bias=v_ssd_dt_bias, ssd_a_log=v_ssd_a_log, ssd_d=v_ssd_d,
             ssd_norm=v_ssd_norm, even_w_out=v_even_w_out, odd_norm=v_odd_norm, odd_w_in=v_odd_w_in, odd_w_out=v_odd_w_out,
             final_norm=v_final_norm)
    shard_names = ('meta', 'lru_conv_w', 'odd_norm', 'ssd_conv_w', 'lru_w_a', 'lru_w_x')
    repl_names = ('even_norm', 'lru_conv_b', 'lru_b_a', 'lru_b_x', 'lru_lambda', 'ssd_norm', 'final_norm', 'ssd_conv_b',
                  'ssd_dt_bias', 'ssd_a_log', 'ssd_d')
    big_names = ('even_w_in', 'even_w_out', 'odd_w_in', 'odd_w_out')

    gates = jnp.concatenate([lru_w_a.reshape(128, 256), lru_w_x.reshape(128, 256)], axis=0).astype(BF16)
    small = _pack_shard(*[w[k] for k in shard_names])
    g_in_e, g_out_e, g_in_o, g_out_o, g_gates, g_small = _exchange(
        [even_w_in[0].astype(BF16), even_w_out[0].astype(BF16), odd_w_in[0].astype(BF16), odd_w_out[0].astype(BF16), gates, small],
        [], "gather_weights")
    w_in_e = jnp.pad(_unblock(g_in_e, 1), ((0, 0), (0, EVEN_NP - EVEN_IN)))
    w_out_e = g_out_e.reshape(2048, 1024)
    w_in_o = _unblock(g_in_o, 1)
    w_out_o = g_out_o.reshape(1024, 1024)
    gates_full = jnp.moveaxis(g_gates.reshape(N_DEV, 2, 4, 32, 256), 0, 2).reshape(2, 4, 256, 256)
    f_meta = _unblock(g_small[:, 0:16], 1)
    f_lru_conv_w = _unblock(g_small[:, 16:20], 1)
    f_odd_norm = _unblock(g_small[:, 20:21], 1)
    f_ssd_conv_w = _unblock(g_small[:, 21:27].reshape(N_DEV, 4, 192), 1)

    loss, grad_x, g = _local_step(
        x, loss_target, f_meta, even_norm, w_in_e, f_lru_conv_w, lru_conv_b, gates_full[0], lru_b_a, gates_full[1], lru_b_x,
        lru_lambda, f_ssd_conv_w, ssd_conv_b, _lane16(ssd_dt_bias), _lane16(ssd_a_log), _lane16(ssd_d), ssd_norm, w_out_e,
        f_odd_norm, w_in_o, w_out_o, final_norm.reshape(1, -1))
    loss = lax.psum(loss, ("x", "y", "c"))

    s_small = jnp.stack([_pack_shard(g['meta'][:, 128 * p:128 * (p + 1)], g['lru_conv_w'][:, 128 * p:128 * (p + 1)],
                                     g['odd_norm'][:, 128 * p:128 * (p + 1)], g['ssd_conv_w'][:, 192 * p:192 * (p + 1)],
                                     g['lru_w_a'][:, 32 * p:32 * (p + 1)], g['lru_w_x'][:, 32 * p:32 * (p + 1)])
                         for p in range(N_DEV)])
    r_pack = _pack_repl(*[g[k] for k in repl_names])
    p_repl, p_in_e, p_out_e, p_in_o, p_out_o, p_small = _exchange(
        [r_pack],
        [_blocks_of(g['even_w_in'], 1).astype(BF16), g['even_w_out'].reshape(N_DEV, 256, 1024).astype(BF16),
         _blocks_of(g['odd_w_in'], 1).astype(BF16), g['odd_w_out'].reshape(N_DEV, 128, 1024).astype(BF16), s_small],
        "reduce_grads")

    res = {}
    for k, parts in zip(big_names, (p_in_e, p_out_e, p_in_o, p_out_o)):
        outs = _adamw(parts, w[k][0], m[k][0], v[k][0], "adamw_" + k)
        res[k] = [o[None] for o in outs]
    outs = _adamw(p_small, small, _pack_shard(*[m[k] for k in shard_names]), _pack_shard(*[v[k] for k in shard_names]), "adamw_sharded")
    unpacked = [_unpack_shard(o) for o in outs]
    for i, k in enumerate(shard_names):
        res[k] = [u[i] for u in unpacked]
    outs = _adamw(p_repl, _pack_repl(*[w[k] for k in repl_names]), _pack_repl(*[m[k] for k in repl_names]),
                  _pack_repl(*[v[k] for k in repl_names]), "adamw_replicated")
    unpacked = [_unpack_repl(o) for o in outs]
    for i, k in enumerate(repl_names):
        res[k] = [u[i] for u in unpacked]
    return (loss, grad_x, *[res[k][0] for k in WEIGHTS], *[res[k][1] for k in WEIGHTS], *[res[k][2] for k in WEIGHTS],
            *[res[k][3] for k in WEIGHTS])
```

```python
import functools

import jax
import jax.numpy as jnp
from jax import lax
from jax.experimental import pallas as pl
from jax.experimental.pallas import tpu as pltpu

F32 = jnp.float32
BF16 = jnp.bfloat16

D_MODEL = 1024
N_META = 16
PAD = 112
EPS = 1e-6
CONV_W = 4
LRU_BLOCKS = 4
LRU_BLOCK = 256
RG_LRU_C = 8.0
SSD_HEADS = 16
SSD_P = 64
SSD_N = 128
SSD_Q = 64
SSD_GROUPS = 2
SSD_GW = 512
SSD_CONV_DIM = 1536
SB_HEADS = 16
SB_D = 64
SB_T = 128
EVEN_IN = 4624
EVEN_NP = 4864
DT_W = 256
ODD_IN = 4096
N_DEV = 8

ADAM_LR = 0.001
ADAM_B1 = 0.9
ADAM_B2 = 0.999
ADAM_EPS = 1e-08
ADAM_WD = 0.01
ADAM_STEP = 10

VMEM_LIMIT = 56 * 1024 * 1024


def _cparams(*sem):
    return pltpu.CompilerParams(dimension_semantics=sem, vmem_limit_bytes=VMEM_LIMIT)


def _row_tile(t):
    for c in (512, 256, 128):
        if t % c == 0:
            return c
    raise ValueError(t)


def _col_tile(n):
    for c in (512, 256, 128):
        if n % c == 0:
            return c
    raise ValueError(n)


def _dot(a, b):
    return jnp.dot(a, b, preferred_element_type=F32)


def _dot_nt(a, b):
    return lax.dot_general(a, b, (((1,), (1,)), ((), ())), preferred_element_type=F32)


def _dot_tn(a, b):
    return lax.dot_general(a, b, (((0,), (0,)), ((), ())), preferred_element_type=F32)


def _split3(a):
    a1 = a.astype(BF16)
    r1 = a - a1.astype(F32)
    a2 = r1.astype(BF16)
    a3 = (r1 - a2.astype(F32)).astype(BF16)
    return a1, a2, a3


def _xdot_r(a, m01):
    a1, a2, a3 = _split3(a)
    return _dot(a1, m01) + _dot(a2, m01) + _dot(a3, m01)


def _xdot_l(m01, a):
    a1, a2, a3 = _split3(a)
    return _dot(m01, a1) + _dot(m01, a2) + _dot(m01, a3)


def _xdot_tn_l(m01, a):
    a1, a2, a3 = _split3(a)
    return _dot_tn(m01, a1) + _dot_tn(m01, a2) + _dot_tn(m01, a3)


def _sigmoid(x):
    return 0.5 * jnp.tanh(0.5 * x) + 0.5


def _softplus(x):
    return jnp.maximum(x, 0.0) + jnp.log(1.0 + jnp.exp(-jnp.abs(x)))


def _iota(shape, dim):
    return lax.broadcasted_iota(jnp.int32, shape, dim)


def _block_scan(a_ref, b_ref, o_ref, reverse=False):
    n, w = a_ref.shape
    nb = n // 8
    unroll = 4 if nb % 4 == 0 else 1
    row = _iota((8, w), 0)

    def block(blk, carry):
        rows = pl.ds(pl.multiple_of(blk * 8, 8), 8)
        a, b = a_ref[rows, :], b_ref[rows, :]
        for k in (1, 2, 4):
            keep = (row < 8 - k) if reverse else (row >= k)
            shift = 8 - k if reverse else k
            a_sh = jnp.where(keep, pltpu.roll(a, shift, 0), 1.0)
            b_sh = jnp.where(keep, pltpu.roll(b, shift, 0), 0.0)
            b = a * b_sh + b
            a = a * a_sh
        o = a * carry + b
        o_ref[rows, :] = o
        return o[0:1, :] if reverse else o[7:8, :]

    def trip(i, carry):
        for u in range(unroll):
            j = i * unroll + u
            carry = block(nb - 1 - j if reverse else j, carry)
        return carry

    lax.fori_loop(0, nb // unroll, trip, jnp.zeros((1, w), F32))


def _chunks(n, c):
    return [(s, min(c, n - s)) for s in range(0, n, c)]


def _mm_nn(a, b, out_dtype, name, res=None):
    m, k = a.shape
    _, n = b.shape
    tm = _row_tile(m)

    def body(*refs):
        if res is None:
            a_ref, b_ref, o_ref = refs
        else:
            a_ref, b_ref, r_ref, o_ref = refs
        av = a_ref[...].astype(BF16)
        for s, w in _chunks(n, 512):
            acc = _dot(av, b_ref[:, s:s + w])
            if res is not None:
                acc = acc + r_ref[:, s:s + w]
            o_ref[:, s:s + w] = acc.astype(out_dtype)

    in_specs = [pl.BlockSpec((tm, k), lambda i: (i, 0)), pl.BlockSpec((k, n), lambda i: (0, 0))]
    args = [a, b]
    if res is not None:
        in_specs.append(pl.BlockSpec((tm, n), lambda i: (i, 0)))
        args.append(res)
    return pl.pallas_call(
        body, grid=(m // tm,), in_specs=in_specs,
        out_specs=pl.BlockSpec((tm, n), lambda i: (i, 0)),
        out_shape=jax.ShapeDtypeStruct((m, n), out_dtype),
        compiler_params=_cparams("parallel"), name=name)(*args)


def _mm_nt(a, b, out_dtype, name):
    m, k = a.shape
    n, _ = b.shape
    tm, tn = _row_tile(m), _col_tile(n)

    def body(a_ref, b_ref, o_ref):
        o_ref[...] = _dot_nt(a_ref[...].astype(BF16), b_ref[...].astype(BF16)).astype(out_dtype)

    return pl.pallas_call(
        body, grid=(m // tm, n // tn),
        in_specs=[pl.BlockSpec((tm, k), lambda i, j: (i, 0)), pl.BlockSpec((tn, k), lambda i, j: (j, 0))],
        out_specs=pl.BlockSpec((tm, tn), lambda i, j: (i, j)),
        out_shape=jax.ShapeDtypeStruct((m, n), out_dtype),
        compiler_params=_cparams("parallel", "parallel"), name=name)(a, b)


def _mm_tn(a, b, name):
    t, m = a.shape
    _, n = b.shape
    tk = _row_tile(t)
    halves = 2 if (m * n * 4 > 8 * 1024 * 1024 and n % 256 == 0) else 1
    tn = n // halves

    def body(a_ref, b_ref, o_ref):
        @pl.when(pl.program_id(1) == 0)
        def _():
            o_ref[...] = jnp.zeros_like(o_ref)
        at = a_ref[...].astype(BF16).T
        for s, w in _chunks(tn, 512):
            o_ref[:, s:s + w] += _dot(at, b_ref[:, s:s + w].astype(BF16))

    return pl.pallas_call(
        body, grid=(halves, t // tk),
        in_specs=[pl.BlockSpec((tk, m), lambda j, s: (s, 0)), pl.BlockSpec((tk, tn), lambda j, s: (s, j))],
        out_specs=pl.BlockSpec((m, tn), lambda j, s: (0, j)),
        out_shape=jax.ShapeDtypeStruct((m, n), F32),
        compiler_params=_cparams("parallel", "arbitrary"), name=name)(a, b)


def _norm_fwd(h, w, name):
    t, d = h.shape
    tm = _row_tile(t)

    def body(h_ref, w_ref, u_ref):
        x = h_ref[...]
        r = lax.rsqrt(jnp.mean(x * x, axis=-1, keepdims=True) + EPS)
        u_ref[...] = (x * r * w_ref[...]).astype(BF16)

    return pl.pallas_call(
        body, grid=(t // tm,),
        in_specs=[pl.BlockSpec((tm, d), lambda i: (i, 0)), pl.BlockSpec((1, d), lambda i: (0, 0))],
        out_specs=pl.BlockSpec((tm, d), lambda i: (i, 0)),
        out_shape=jax.ShapeDtypeStruct((t, d), BF16),
        compiler_params=_cparams("parallel"), name=name)(h, w)


def _norm_bwd(h, w, du, dres, name):
    t, d = h.shape
    tm = _row_tile(t)

    def body(h_ref, w_ref, du_ref, dr_ref, dh_ref, dw_ref):
        @pl.when(pl.program_id(0) == 0)
        def _():
            dw_ref[...] = jnp.zeros_like(dw_ref)
        x = h_ref[...]
        r = lax.rsqrt(jnp.mean(x * x, axis=-1, keepdims=True) + EPS)
        xh = x * r
        du_ = du_ref[...]
        g = du_ * w_ref[...]
        dh_ref[...] = dr_ref[...] + r * (g - xh * jnp.mean(g * xh, axis=-1, keepdims=True))
        dw_ref[...] += jnp.sum(du_ * xh, axis=0, keepdims=True)

    return pl.pallas_call(
        body, grid=(t // tm,),
        in_specs=[pl.BlockSpec((tm, d), lambda i: (i, 0)), pl.BlockSpec((1, d), lambda i: (0, 0)),
                  pl.BlockSpec((tm, d), lambda i: (i, 0)), pl.BlockSpec((tm, d), lambda i: (i, 0))],
        out_specs=[pl.BlockSpec((tm, d), lambda i: (i, 0)), pl.BlockSpec((1, d), lambda i: (0, 0))],
        out_shape=[jax.ShapeDtypeStruct((t, d), F32), jax.ShapeDtypeStruct((1, d), F32)],
        compiler_params=_cparams("arbitrary"), name=name)(h, w, du, dres)


def _final_loss(h2, w, tgt, lp):
    t, d = h2.shape
    bsz, seq, _ = tgt.shape
    nblk = lp // SB_T

    def body(h_ref, w_ref, t_ref, loss_ref, dh_ref, dw_ref):
        b, i = pl.program_id(0), pl.program_id(1)

        @pl.when((b == 0) & (i == 0))
        def _():
            loss_ref[...] = jnp.zeros_like(loss_ref)
            dw_ref[...] = jnp.zeros_like(dw_ref)

        @pl.when(i == 0)
        def _():
            dh_ref[...] = jnp.zeros_like(dh_ref)

        @pl.when(i > 0)
        def _():
            x = h_ref[...]
            r = lax.rsqrt(jnp.mean(x * x, axis=-1, keepdims=True) + EPS)
            xh = x * r
            wv = w_ref[...]
            diff = xh * wv - t_ref[0]
            loss_ref[...] += 0.5 * jnp.sum(jnp.mean(diff * diff, axis=-1, keepdims=True), axis=0, keepdims=True)
            dy = diff * (1.0 / d)
            g = dy * wv
            dh_ref[...] = r * (g - xh * jnp.mean(g * xh, axis=-1, keepdims=True))
            dw_ref[...] += jnp.sum(dy * xh, axis=0, keepdims=True)

    return pl.pallas_call(
        body, grid=(bsz, nblk),
        in_specs=[pl.BlockSpec((SB_T, d), lambda b, i: (b * nblk + i, 0)), pl.BlockSpec((1, d), lambda b, i: (0, 0)),
                  pl.BlockSpec((1, SB_T, d), lambda b, i: (b, jnp.maximum(i - 1, 0), 0))],
        out_specs=[pl.BlockSpec((1, 128), lambda b, i: (0, 0)), pl.BlockSpec((SB_T, d), lambda b, i: (b * nblk + i, 0)),
                   pl.BlockSpec((1, d), lambda b, i: (0, 0))],
        out_shape=[jax.ShapeDtypeStruct((1, 128), F32), jax.ShapeDtypeStruct((t, d), F32), jax.ShapeDtypeStruct((1, d), F32)],
        compiler_params=_cparams("arbitrary", "arbitrary"), name="final_loss")(h2, w, tgt)


def _shift_down(x, j):
    return x if j == 0 else pltpu.roll(x, j, 0)


def _shift_up(x, j):
    return x if j == 0 else pltpu.roll(x, x.shape[0] - j, 0)


def _conv(x, cw, cb):
    out = cb + cw[CONV_W - 1:CONV_W, :] * x
    for j in range(1, CONV_W):
        out = out + cw[CONV_W - 1 - j:CONV_W - j, :] * _shift_down(x, j)
    return out


def _conv_bwd_x(dy, cw):
    out = cw[CONV_W - 1:CONV_W, :] * dy
    for j in range(1, CONV_W):
        out = out + cw[CONV_W - 1 - j:CONV_W - j, :] * _shift_up(dy, j)
    return out


def _conv_bwd_w(dcw_ref, dy, x):
    for j in range(CONV_W):
        k = CONV_W - 1 - j
        dcw_ref[k:k + 1, :] += jnp.sum(dy * _shift_down(x, j), axis=0, keepdims=True)


def _lru_forward_block(x, cw, cb, wa, ba, wx, bx, lam, a_ref, b_ref, h_ref):
    lp = x.shape[0]
    lx = _conv(x, cw, cb)
    lxb = lx.astype(BF16)
    r = _sigmoid(_dot(lxb, wa) + ba)
    i = _sigmoid(_dot(lxb, wx) + bx)
    spl = _softplus(-lam)
    log_a = (-RG_LRU_C) * r * spl
    a = jnp.exp(log_a)
    mult = jnp.sqrt(-jnp.tanh(log_a) * (1.0 + a * a))
    valid = _iota((lp, 1), 0) >= PAD
    a_ref[...] = a
    b_ref[...] = jnp.where(valid, mult * i * lx, 0.0)
    _block_scan(a_ref, b_ref, h_ref)
    return lx, lxb, r, i, spl, a, mult, valid


def _lru_specs(lp):
    w = LRU_BLOCK
    return [
        pl.BlockSpec((lp, w), lambda g, b: (b, g)),
        pl.BlockSpec((lp, w), lambda g, b: (b, LRU_BLOCKS + g)),
        pl.BlockSpec((CONV_W, w), lambda g, b: (0, g)),
        pl.BlockSpec((1, w), lambda g, b: (0, g)),
        pl.BlockSpec((1, w, w), lambda g, b: (g, 0, 0)),
        pl.BlockSpec((1, w), lambda g, b: (0, g)),
        pl.BlockSpec((1, w, w), lambda g, b: (g, 0, 0)),
        pl.BlockSpec((1, w), lambda g, b: (0, g)),
        pl.BlockSpec((1, w), lambda g, b: (0, g)),
    ]


def _lru_fwd(proj, cw, cb, wa, ba, wx, bx, lam, bsz, lp):
    w = LRU_BLOCK

    def body(x_ref, g_ref, cw_ref, cb_ref, wa_ref, ba_ref, wx_ref, bx_ref, lam_ref, y_ref, a_s, b_s, h_s):
        _lru_forward_block(x_ref[...], cw_ref[...], cb_ref[...], wa_ref[0], ba_ref[...], wx_ref[0], bx_ref[...],
                           lam_ref[...], a_s, b_s, h_s)
        gate = g_ref[...]
        y_ref[...] = (h_s[...] * gate * _sigmoid(gate)).astype(BF16)

    return pl.pallas_call(
        body, grid=(LRU_BLOCKS, bsz), in_specs=_lru_specs(lp),
        out_specs=pl.BlockSpec((lp, w), lambda g, b: (b, g)),
        out_shape=jax.ShapeDtypeStruct((bsz * lp, LRU_BLOCKS * w), BF16),
        scratch_shapes=[pltpu.VMEM((lp, w), F32)] * 3,
        compiler_params=_cparams("parallel", "arbitrary"), name="lru_fwd")(proj, proj, cw, cb, wa, ba, wx, bx, lam)


def _lru_bwd(proj, dmixed, cw, cb, wa, ba, wx, bx, lam, bsz, lp):
    w = LRU_BLOCK

    def body(x_ref, g_ref, cw_ref, cb_ref, wa_ref, ba_ref, wx_ref, bx_ref, lam_ref, dy_ref,
             dx_ref, dg_ref, dcw_ref, dcb_ref, dwa_ref, dba_ref, dwx_ref, dbx_ref, dlam_ref, a_s, b_s, h_s, dh_s):
        @pl.when(pl.program_id(1) == 0)
        def _():
            for ref in (dcw_ref, dcb_ref, dwa_ref, dba_ref, dwx_ref, dbx_ref, dlam_ref):
                ref[...] = jnp.zeros_like(ref)

        x = x_ref[...]
        cwv = cw_ref[...]
        wav, wxv, lam_ = wa_ref[0], wx_ref[0], lam_ref[...]
        lx, lxb, r, i, spl, a, mult, valid = _lru_forward_block(
            x, cwv, cb_ref[...], wav, ba_ref[...], wxv, bx_ref[...], lam_, a_s, b_s, h_s)
        gate = g_ref[...]
        sg = _sigmoid(gate)
        dy = dy_ref[...]
        h = h_s[...]
        dg_ref[...] = (dy * h * sg * (1.0 + gate * (1.0 - sg))).astype(BF16)
        b_s[...] = dy * gate * sg
        a_s[...] = jnp.where(_iota((lp, 1), 0) < lp - 1, _shift_up(a, 1), 0.0)
        _block_scan(a_s, b_s, dh_s, reverse=True)
        dh = dh_s[...]
        hprev = jnp.where(_iota((lp, 1), 0) >= 1, _shift_down(h, 1), 0.0)
        db = jnp.where(valid, dh, 0.0)
        dmult = db * i * lx
        di = db * mult * lx
        dlx = db * mult * i
        inv_mult = jnp.where(mult > 0.0, 1.0 / mult, 0.0)
        dlog_a = dh * hprev * a - dmult * (a * a) * inv_mult
        drp = dlog_a * ((-RG_LRU_C) * spl) * r * (1.0 - r)
        dip = di * i * (1.0 - i)
        dspl = jnp.sum(dlog_a * ((-RG_LRU_C) * r), axis=0, keepdims=True)
        dlam_ref[...] += dspl * (-_sigmoid(-lam_))
        dba_ref[...] += jnp.sum(drp, axis=0, keepdims=True)
        dbx_ref[...] += jnp.sum(dip, axis=0, keepdims=True)
        drpb, dipb = drp.astype(BF16), dip.astype(BF16)
        dwa_ref[0] += _dot_tn(lxb, drpb)
        dwx_ref[0] += _dot_tn(lxb, dipb)
        dlx = dlx + _dot_nt(drpb, wav) + _dot_nt(dipb, wxv)
        dcb_ref[...] += jnp.sum(dlx, axis=0, keepdims=True)
        _conv_bwd_w(dcw_ref, dlx, x)
        dx_ref[...] = _conv_bwd_x(dlx, cwv).astype(BF16)

    t = bsz * lp
    vec = pl.BlockSpec((1, w), lambda g, b: (0, g))
    mat = pl.BlockSpec((1, w, w), lambda g, b: (g, 0, 0))
    act = pl.BlockSpec((lp, w), lambda g, b: (b, g))
    return pl.pallas_call(
        body, grid=(LRU_BLOCKS, bsz), in_specs=_lru_specs(lp) + [act],
        out_specs=[act, act, pl.BlockSpec((CONV_W, w), lambda g, b: (0, g)), vec, mat, vec, mat, vec, vec],
        out_shape=[jax.ShapeDtypeStruct((t, 1024), BF16), jax.ShapeDtypeStruct((t, 1024), BF16),
                   jax.ShapeDtypeStruct((CONV_W, 1024), F32), jax.ShapeDtypeStruct((1, 1024), F32),
                   jax.ShapeDtypeStruct((LRU_BLOCKS, w, w), F32), jax.ShapeDtypeStruct((1, 1024), F32),
                   jax.ShapeDtypeStruct((LRU_BLOCKS, w, w), F32), jax.ShapeDtypeStruct((1, 1024), F32),
                   jax.ShapeDtypeStruct((1, 1024), F32)],
        scratch_shapes=[pltpu.VMEM((lp, w), F32)] * 4,
        compiler_params=_cparams("parallel", "arbitrary"), name="lru_bwd")(
            proj, proj, cw, cb, wa, ba, wx, bx, lam, dmixed)


def _sb_masks():
    j = _iota((SB_T, 2 * SB_T), 0)
    s = _iota((SB_T, 2 * SB_T), 1)
    right = ((s < SB_T) & (j > s)) | (s >= SB_T)
    left = ((s < SB_T) & (j < s)) | (s >= SB_T)
    return jnp.where(right, -1.0, 0.0).astype(BF16), left.astype(BF16)


def _xdot2(a, m01):
    a1 = a.astype(BF16)
    a2 = (a - a1.astype(F32)).astype(BF16)
    return _dot(a1, m01) + _dot(a2, m01)


def _sb_query_block(lp):
    for tq in (512, 256, 128):
        if (lp - SB_T) % tq == 0:
            return tq, (lp - SB_T) // tq
    raise ValueError(lp)


def _sb_key_rows(kj):
    return pl.ds(kj * SB_T if isinstance(kj, int) else pl.multiple_of(kj * SB_T, SB_T), SB_T)


def _sb_valid(q0, tq, kj):
    t_pos = q0 + _iota((tq, SB_T), 0)
    s_pos = kj * SB_T + _iota((tq, SB_T), 1)
    return (s_pos < t_pos) & (s_pos >= PAD)


def _sb_by_head(ref, rows):
    t = ref[rows, :]
    head0 = _iota((SB_T, 128), 1) < SB_D
    return jnp.concatenate([jnp.where(head0, t, 0.0), jnp.where(head0, 0.0, t)], axis=0).astype(BF16)


def _sb_tile(qb, k_ref, q0, kj, carries, m_right_neg, masked):
    rows_k = _sb_key_rows(kj)
    z2 = _dot_nt(qb, _sb_by_head(k_ref, rows_k))
    valid = _sb_valid(q0, qb.shape[0], kj) if masked else None
    out = []
    for h in range(2):
        z = z2[:, h * SB_T:(h + 1) * SB_T]
        sp = _softplus(z)
        rs = _dot((jnp.where(valid, sp, 0.0) if masked else sp).astype(BF16), m_right_neg)
        lb = z - sp
        wgt = jnp.exp(lb + rs[:, :SB_T] + carries[h])
        if masked:
            wgt = jnp.where(valid, wgt, 0.0)
        out.append((lb, wgt, carries[h] + rs[:, SB_T:]))
    return rows_k, valid, out


def _tail(x, row0):
    return x if row0 == 0 else x[row0:]


def _merge(old, new_tail, row0):
    return new_tail if row0 == 0 else jnp.concatenate([old[:row0], new_tail], axis=0)


def _sb_sweep(step, c, blk, tq, first, leftwards):
    if first:
        return step(0, True, 0)(c)
    r = tq // SB_T
    u = 2 if r % 2 == 0 else 1
    lo = 1 + r * blk

    def diag(c):
        for d in (range(r - 1, -1, -1) if leftwards else range(r)):
            c = step(lo + d, True, SB_T * d)(c)
        return c

    def inner(c):
        def trip(i, c):
            for j in range(u):
                d = u * i + j
                c = step(lo - 1 - d if leftwards else 1 + d, False, 0)(c)
            return c
        return lax.fori_loop(0, (r // u) * blk, trip, c)

    if leftwards:
        return step(0, True, 0)(inner(diag(c)))
    return diag(inner(step(0, True, 0)(c)))


def _sb_specs(lp):
    nh = SB_HEADS // 2
    return [pl.BlockSpec((lp, 128), lambda b, p: (b, p)), pl.BlockSpec((lp, 128), lambda b, p: (b, nh + p)),
            pl.BlockSpec((lp, 128), lambda b, p: (b, 2 * nh + p)), pl.BlockSpec((lp, 128), lambda b, p: (b, 3 * nh + p))]


def _sb_fwd(qkvg, bsz, lp):
    tq, nb = _sb_query_block(lp)
    scale = SB_D ** -0.5

    def body(q_ref, k_ref, v_ref, g_ref, o_ref, og_ref):
        m_right_neg, _ = _sb_masks()

        def q_block(blk, rows, first):
            q0 = 0 if first else (1 + (rows // SB_T) * blk) * SB_T
            rows_q = pl.ds(q0 if first else pl.multiple_of(q0, SB_T), rows)
            qb = (q_ref[rows_q, :] * scale).astype(BF16)

            def step(kj, masked, row0):
                def run(c):
                    acc, car0, car1 = c
                    rows_k, _, ((_, w0, new0), (_, w1, new1)) = _sb_tile(
                        _tail(qb, row0), k_ref, q0 + row0, kj, (_tail(car0, row0), _tail(car1, row0)), m_right_neg, masked)
                    w2 = jnp.concatenate([w0.astype(BF16), w1.astype(BF16)], axis=1)
                    upd = _tail(acc, row0) + _dot(w2, _sb_by_head(v_ref, rows_k))
                    return _merge(acc, upd, row0), _merge(car0, new0, row0), _merge(car1, new1, row0)
                return run

            zero = jnp.zeros((rows, SB_T), F32)
            o_ref[rows_q, :] = _sb_sweep(step, (zero, zero, zero), blk, rows, first, True)[0]

        q_block(0, SB_T, True)

        def big_block(i, _):
            q_block(i, tq, False)
            return 0

        lax.fori_loop(0, nb, big_block, 0)
        gate = g_ref[...]
        og_ref[...] = (o_ref[...] * gate * _sigmoid(gate)).astype(BF16)

    t = bsz * lp
    blk = pl.BlockSpec((lp, 128), lambda b, p: (b, p))
    return pl.pallas_call(
        body, grid=(bsz, SB_HEADS // 2), in_specs=_sb_specs(lp), out_specs=[blk, blk],
        out_shape=[jax.ShapeDtypeStruct((t, 1024), F32), jax.ShapeDtypeStruct((t, 1024), BF16)],
        compiler_params=_cparams("parallel", "parallel"), name="sb_fwd")(qkvg, qkvg, qkvg, qkvg)


def _sb_bwd(qkvg, o, dog, bsz, lp):
    tq, nb = _sb_query_block(lp)
    nk = lp // SB_T
    scale = SB_D ** -0.5

    def body(q_ref, k_ref, v_ref, g_ref, o_ref, dog_ref, dq_ref, dk_ref, dv_ref, dg_ref, do_s, dk_s, dv_s, e_s, sig_s):
        m_right_neg, m_left = _sb_masks()
        gate = g_ref[...]
        sg = _sigmoid(gate)
        dog = dog_ref[...]
        do_s[...] = dog * gate * sg
        dg_ref[...] = (dog * o_ref[...] * sg * (1.0 + gate * (1.0 - sg))).astype(BF16)
        dk_s[...] = jnp.zeros_like(dk_s)
        dv_s[...] = jnp.zeros_like(dv_s)
        def q_block(blk, rows, first):
            q0 = 0 if first else (1 + (rows // SB_T) * blk) * SB_T
            rows_q = pl.ds(q0 if first else pl.multiple_of(q0, SB_T), rows)
            head0 = _iota((rows, 128), 1) < SB_D
            qf = q_ref[rows_q, :] * scale
            qb = qf.astype(BF16)
            q_h = (jnp.where(head0, qf, 0.0).astype(BF16), jnp.where(head0, 0.0, qf).astype(BF16))
            do_f = do_s[rows_q, :]
            do_b = do_f.astype(BF16)
            do_h = (jnp.where(head0, do_f, 0.0).astype(BF16), jnp.where(head0, 0.0, do_f).astype(BF16))

            def left(kj, masked, row0):
                def run(c):
                    rows_k, _, heads = _sb_tile(_tail(qb, row0), k_ref, q0 + row0, kj, (_tail(c[0], row0), _tail(c[1], row0)),
                                                m_right_neg, masked)
                    dw2 = _dot_nt(_tail(do_b, row0), _sb_by_head(v_ref, rows_k))
                    dv = None
                    for h, (lb, wgt, _) in enumerate(heads):
                        e_s[2 * kj + h, row0:rows, :] = wgt * dw2[:, h * SB_T:(h + 1) * SB_T]
                        sig_s[2 * kj + h, row0:rows, :] = jnp.exp(lb)
                        part = _dot_tn(wgt.astype(BF16), _tail(do_h[h], row0))
                        dv = part if dv is None else dv + part
                    dv_s[rows_k, :] += dv
                    return _merge(c[0], heads[0][2], row0), _merge(c[1], heads[1][2], row0)
                return run

            zero = jnp.zeros((rows, SB_T), F32)
            _sb_sweep(left, (zero, zero), blk, rows, first, True)

            def right(kj, masked, row0):
                def run(c):
                    dq, ecar = c[0], list(c[1:])
                    rows_k = _sb_key_rows(kj)
                    valid = _sb_valid(q0 + row0, rows - row0, kj) if masked else None
                    dzs, dk = [], None
                    for h in range(2):
                        e = e_s[2 * kj + h, row0:rows, :]
                        se = _dot(e.astype(BF16), m_left)
                        ec = _tail(ecar[h], row0)
                        dz = e - sig_s[2 * kj + h, row0:rows, :] * (e + se[:, :SB_T] + ec)
                        if masked:
                            dz = jnp.where(valid, dz, 0.0)
                        dz = dz.astype(BF16)
                        part = _dot_tn(dz, _tail(q_h[h], row0))
                        dk = part if dk is None else dk + part
                        dzs.append(dz)
                        ecar[h] = _merge(ecar[h], ec + se[:, SB_T:], row0)
                    dk_s[rows_k, :] += dk
                    upd = _tail(dq, row0) + _dot(jnp.concatenate(dzs, axis=1), _sb_by_head(k_ref, rows_k))
                    return _merge(dq, upd, row0), ecar[0], ecar[1]
                return run

            dq = _sb_sweep(right, (zero, zero, zero), blk, rows, first, False)[0]
            dq_ref[rows_q, :] = (dq * scale).astype(BF16)

        q_block(0, SB_T, True)

        def big_block(i, _):
            q_block(i, tq, False)
            return 0

        lax.fori_loop(0, nb, big_block, 0)
        dk_ref[...] = dk_s[...].astype(BF16)
        dv_ref[...] = dv_s[...].astype(BF16)

    t = bsz * lp
    blk = pl.BlockSpec((lp, 128), lambda b, p: (b, p))
    shp = jax.ShapeDtypeStruct((t, 1024), BF16)
    return pl.pallas_call(
        body, grid=(bsz, SB_HEADS // 2), in_specs=_sb_specs(lp) + [blk, blk], out_specs=[blk] * 4,
        out_shape=[shp] * 4,
        scratch_shapes=[pltpu.VMEM((lp, 128), F32)] * 3 + [pltpu.VMEM((2 * nk, tq, SB_T), F32)] * 2,
        compiler_params=_cparams("parallel", "parallel"), name="sb_bwd")(qkvg, qkvg, qkvg, qkvg, o, dog)


XBC_COL0 = 3072 // 256
DT_COL0 = 4608 // 128
DT_L = 128


def _ssd_pre_fwd(proj, cw, cb, bsz, lp):
    def body(x_ref, cw_ref, cb_ref, o_ref):
        pre = _conv(x_ref[...], cw_ref[...], cb_ref[...])
        o_ref[...] = pre * _sigmoid(pre)

    return pl.pallas_call(
        body, grid=(bsz, SSD_CONV_DIM // 256),
        in_specs=[pl.BlockSpec((lp, 256), lambda b, j: (b, XBC_COL0 + j)), pl.BlockSpec((CONV_W, 256), lambda b, j: (0, j)),
                  pl.BlockSpec((1, 256), lambda b, j: (0, j))],
        out_specs=pl.BlockSpec((lp, 256), lambda b, j: (b, j)),
        out_shape=jax.ShapeDtypeStruct((bsz * lp, SSD_CONV_DIM), F32),
        compiler_params=_cparams("parallel", "parallel"), name="ssd_pre_fwd")(proj, cw, cb)


def _ssd_pre_bwd(proj, dact, cw, cb, bsz, lp):
    def body(x_ref, d_ref, cw_ref, cb_ref, dx_ref, dcw_ref, dcb_ref):
        @pl.when(pl.program_id(1) == 0)
        def _():
            dcw_ref[...] = jnp.zeros_like(dcw_ref)
            dcb_ref[...] = jnp.zeros_like(dcb_ref)
        x = x_ref[...]
        cwv = cw_ref[...]
        pre = _conv(x, cwv, cb_ref[...])
        s = _sigmoid(pre)
        dpre = d_ref[...].astype(F32) * s * (1.0 + pre * (1.0 - s))
        dcb_ref[...] += jnp.sum(dpre, axis=0, keepdims=True)
        _conv_bwd_w(dcw_ref, dpre, x)
        dx_ref[...] = _conv_bwd_x(dpre, cwv).astype(BF16)

    return pl.pallas_call(
        body, grid=(SSD_CONV_DIM // 256, bsz),
        in_specs=[pl.BlockSpec((lp, 256), lambda j, b: (b, XBC_COL0 + j)), pl.BlockSpec((lp, 256), lambda j, b: (b, j)),
                  pl.BlockSpec((CONV_W, 256), lambda j, b: (0, j)), pl.BlockSpec((1, 256), lambda j, b: (0, j))],
        out_specs=[pl.BlockSpec((lp, 256), lambda j, b: (b, j)), pl.BlockSpec((CONV_W, 256), lambda j, b: (0, j)),
                   pl.BlockSpec((1, 256), lambda j, b: (0, j))],
        out_shape=[jax.ShapeDtypeStruct((bsz * lp, SSD_CONV_DIM), BF16), jax.ShapeDtypeStruct((CONV_W, SSD_CONV_DIM), F32),
                   jax.ShapeDtypeStruct((1, SSD_CONV_DIM), F32)],
        compiler_params=_cparams("parallel", "arbitrary"), name="ssd_pre_bwd")(proj, dact, cw, cb)


def _split2(a):
    a1 = a.astype(BF16)
    return a1, (a - a1.astype(F32)).astype(BF16)


def _xdot2_nt(a, m01):
    a1, a2 = _split2(a)
    return _dot_nt(a1, m01) + _dot_nt(a2, m01)


def _xdot2_l(m01, a):
    a1, a2 = _split2(a)
    return _dot(m01, a1) + _dot(m01, a2)


class _SsdConsts:
    def __init__(self, g):
        q, gw = SSD_Q, SSD_GW
        head_of_lane = lax.shift_right_logical(_iota((DT_L, gw), 1), 6)
        self.sel = (_iota((DT_L, gw), 0) == 8 * g + head_of_lane).astype(BF16)
        r = _iota((q, gw), 0)
        c = jnp.bitwise_and(_iota((q, gw), 1), q - 1)
        self.diag = r == c
        self.diag_b = self.diag.astype(BF16)
        self.lower = c <= r
        self.upper = c >= r
        self.bd = lax.shift_right_logical(_iota((gw, gw), 0), 6) == lax.shift_right_logical(_iota((gw, gw), 1), 6)
        jj, ll = _iota((q, q), 1), _iota((q, q), 0)
        self.tri = (jj <= ll).astype(BF16)
        self.tri_t = (jj >= ll).astype(BF16)
        self.ones = jnp.ones((q, q), BF16)
        self.last = _iota((q, 1), 0) == q - 1


def _ssd_prepass(k, dt_ref, bias_ref, alog_ref, dtm_s, abc_s, lp):
    valid = _iota((lp, 1), 0) >= PAD
    sp_in = dt_ref[...] + bias_ref[...]
    dtm = jnp.where(valid, _softplus(sp_in), 0.0)
    aneg = -jnp.exp(alog_ref[...])
    dtm_s[...] = dtm
    abc_s[...] = _xdot_r(dtm * aneg, k.sel)
    return valid, sp_in, aneg


def _ssd_loop(nc, chunk, init):
    u = 2 if nc % 2 == 0 else 1

    def trip(i, carry):
        for j in range(u):
            carry = chunk(i * u + j, carry)
        return carry

    return lax.fori_loop(0, nc // u, trip, init)


def _ssd_chunk(k, rows, xs_ref, b_ref, c_ref, dtm_s, abc_s):
    bc = _xdot_l(k.tri, abc_s[rows, :])
    tt = jnp.sum(jnp.where(k.diag, bc, 0.0), axis=0, keepdims=True)
    dtbc = _xdot2(dtm_s[rows, :], k.sel)
    xs = xs_ref[rows, :]
    x = xs * dtbc
    bb = b_ref[rows, :].astype(BF16)
    cc = c_ref[rows, :].astype(BF16)
    tot = bc[SSD_Q - 1:SSD_Q, :]
    xbd = jnp.where(k.bd, jnp.concatenate([x] * 8, axis=0), 0.0).astype(BF16)
    return bc, tt, dtbc, xs, x, bb, cc, tot, xbd


def _ssd_specs(lp, order):
    ix = (lambda f: (lambda b, g: f(b, g))) if order == "bg" else (lambda f: (lambda g, b: f(b, g)))
    return [pl.BlockSpec((lp, SSD_GW), ix(lambda b, g: (b, g))),
            pl.BlockSpec((lp, SSD_N), ix(lambda b, g: (b, 1024 // SSD_N + g))),
            pl.BlockSpec((lp, SSD_N), ix(lambda b, g: (b, 1280 // SSD_N + g))),
            pl.BlockSpec((lp, DT_L), ix(lambda b, g: (b, DT_COL0))),
            pl.BlockSpec((1, DT_L), ix(lambda b, g: (0, 0))),
            pl.BlockSpec((1, DT_L), ix(lambda b, g: (0, 0))),
            pl.BlockSpec((1, DT_L), ix(lambda b, g: (0, 0)))]


def _ssd_fwd(xbc, proj, dt_bias, a_log, dskip, bsz, lp):
    nc = lp // SSD_Q

    def body(xs_ref, b_ref, c_ref, dt_ref, bias_ref, alog_ref, dsk_ref, y_ref, dtm_s, abc_s):
        k = _SsdConsts(pl.program_id(1))
        _ssd_prepass(k, dt_ref, bias_ref, alog_ref, dtm_s, abc_s, lp)
        dvec = _xdot_r(jnp.broadcast_to(dsk_ref[...], (8, DT_L)), k.sel)[0:1, :]

        def chunk(c, state):
            rows = pl.ds(pl.multiple_of(c * SSD_Q, SSD_Q), SSD_Q)
            bc, tt, _, xs, x, bb, cc, tot, xbd = _ssd_chunk(k, rows, xs_ref, b_ref, c_ref, dtm_s, abc_s)
            lm = jnp.where(k.lower, jnp.exp(jnp.minimum(bc - tt, 0.0)), 0.0)
            g_all = _dot(_dot_nt(cc, bb).astype(BF16), k.diag_b) * lm
            y = _dot(g_all.astype(BF16), xbd) + jnp.exp(bc) * _dot(cc, state.astype(BF16)) + dvec * xs
            y_ref[rows, :] = y
            return jnp.exp(tot) * state + _dot_tn(bb, (jnp.exp(tot - bc) * x).astype(BF16))

        _ssd_loop(nc, chunk, jnp.zeros((SSD_N, SSD_GW), F32))

    return pl.pallas_call(
        body, grid=(bsz, SSD_GROUPS), in_specs=_ssd_specs(lp, "bg"),
        out_specs=pl.BlockSpec((lp, SSD_GW), lambda b, g: (b, g)),
        out_shape=jax.ShapeDtypeStruct((bsz * lp, 1024), F32),
        scratch_shapes=[pltpu.VMEM((lp, DT_L), F32), pltpu.VMEM((lp, SSD_GW), F32)],
        compiler_params=_cparams("parallel", "parallel"), name="ssd_fwd")(xbc, xbc, xbc, proj, dt_bias, a_log, dskip)


def _ssd_bwd(xbc, proj, dt_bias, a_log, dskip, dy, bsz, lp):
    nc = lp // SSD_Q

    def body(xs_ref, b_ref, c_ref, dt_ref, bias_ref, alog_ref, dsk_ref, dy_ref,
             dxs_ref, db_ref, dc_ref, ddt_ref, dbias_ref, dalog_ref, ddsk_ref, dtm_s, abc_s, st_s):
        @pl.when(pl.program_id(1) == 0)
        def _():
            for ref in (dbias_ref, dalog_ref, ddsk_ref):
                ref[...] = jnp.zeros_like(ref)

        k = _SsdConsts(pl.program_id(0))
        valid, sp_in, aneg = _ssd_prepass(k, dt_ref, bias_ref, alog_ref, dtm_s, abc_s, lp)
        dvec = _xdot_r(jnp.broadcast_to(dsk_ref[...], (8, DT_L)), k.sel)[0:1, :]

        def fwd_chunk(c, state):
            rows = pl.ds(pl.multiple_of(c * SSD_Q, SSD_Q), SSD_Q)
            st_s[c] = state.astype(BF16)
            bc, _, _, _, x, bb, _, tot, _ = _ssd_chunk(k, rows, xs_ref, b_ref, c_ref, dtm_s, abc_s)
            return jnp.exp(tot) * state + _dot_tn(bb, (jnp.exp(tot - bc) * x).astype(BF16))

        _ssd_loop(nc, fwd_chunk, jnp.zeros((SSD_N, SSD_GW), F32))

        def bwd_chunk(i, carry):
            dstate, daneg, ddsk = carry
            c = nc - 1 - i
            rows = pl.ds(pl.multiple_of(c * SSD_Q, SSD_Q), SSD_Q)
            bc, tt, dtbc, xs, x, bb, cc, tot, xbd = _ssd_chunk(k, rows, xs_ref, b_ref, c_ref, dtm_s, abc_s)
            sprev = st_s[c]
            dyc = dy_ref[rows, :]
            dyb = dyc.astype(BF16)
            seg = bc - tt
            lm = jnp.where(k.lower, jnp.exp(jnp.minimum(seg, 0.0)), 0.0)
            lm_t = jnp.where(k.upper, jnp.exp(jnp.minimum(-seg, 0.0)), 0.0)
            cb_all = _dot(_dot_nt(cc, bb).astype(BF16), k.diag_b)
            cbt_all = _dot(_dot_nt(bb, cc).astype(BF16), k.diag_b)
            g_all = cb_all * lm
            dybd = jnp.where(k.bd, jnp.concatenate([dyc] * 8, axis=0), 0.0).astype(BF16)
            dg = _dot_nt(dyb, xbd)
            dx = _dot((cbt_all * lm_t).astype(BF16), dybd)
            hh = dg * g_all
            ea = jnp.exp(bc)
            yo = ea * _dot(cc, sprev)
            col_h = jnp.sum(hh, axis=0, keepdims=True)
            dcb = _dot_nt((dg * lm).astype(BF16), k.diag_b).astype(BF16)
            dcs = (ea * dyc).astype(BF16)
            dstb = dstate.astype(BF16)
            dec = jnp.exp(tot - bc)
            w = dec * x
            dw = _dot(bb, dstb)
            dc_ref[rows, :] = (_dot(dcb, bb) + _dot_nt(dcs, sprev)).astype(BF16)
            db_ref[rows, :] = (_dot_tn(dcb, cc) + _dot_nt(w.astype(BF16), dstb)).astype(BF16)
            dx = dx + dec * dw
            etot = jnp.exp(tot)
            r8 = _iota((8, SSD_GW), 0)
            st_row = jnp.sum(dstate * sprev.astype(F32), axis=0, keepdims=True) * etot
            sk_row = jnp.sum(dyc * xs, axis=0, keepdims=True)
            small = jnp.where(r8 == 0, st_row, jnp.where(r8 == 1, sk_row, 0.0))
            q = SSD_Q
            sums = _xdot2_nt(jnp.concatenate([hh + dyc * yo - jnp.where(k.diag, col_h, 0.0) - dw * w, dw * w, dx * xs, small],
                                             axis=0), k.sel)
            kk = sums[q:2 * q]
            dtot = sums[3 * q:3 * q + 1] + jnp.sum(kk, axis=0, keepdims=True)
            dacum = sums[0:q] + jnp.where(k.last, dtot, 0.0)
            da = _xdot2_l(k.tri_t, dacum)
            dtm_c = dtm_s[rows, :]
            ddtm = da * aneg + sums[2 * q:3 * q]
            vrow = (c * SSD_Q + _iota((SSD_Q, 1), 0)) >= PAD
            ddt_ref[rows, :] = jnp.where(vrow, ddtm * _sigmoid(dt_ref[rows, :] + bias_ref[...]), 0.0)
            dxs_ref[rows, :] = (dx * dtbc + dvec * dyc).astype(BF16)
            daneg = daneg + jnp.sum(da * dtm_c, axis=0, keepdims=True)
            ddsk = ddsk + sums[3 * q + 1:3 * q + 2]
            dstate = etot * dstate + _dot_tn(cc, dcs)
            return dstate, daneg, ddsk

        zrow = jnp.zeros((1, DT_L), F32)
        _, daneg, ddsk = _ssd_loop(nc, bwd_chunk, (jnp.zeros((SSD_N, SSD_GW), F32), zrow, zrow))
        dbias_ref[...] += jnp.broadcast_to(jnp.sum(ddt_ref[...], axis=0, keepdims=True), (8, DT_L))
        dalog_ref[...] += jnp.broadcast_to(daneg * aneg, (8, DT_L))
        ddsk_ref[...] += jnp.broadcast_to(ddsk, (8, DT_L))

    t = bsz * lp
    par = pl.BlockSpec((8, DT_L), lambda g, b: (g, 0))
    par_shape = jax.ShapeDtypeStruct((8 * SSD_GROUPS, DT_L), F32)
    return pl.pallas_call(
        body, grid=(SSD_GROUPS, bsz),
        in_specs=_ssd_specs(lp, "gb") + [pl.BlockSpec((lp, SSD_GW), lambda g, b: (b, g))],
        out_specs=[pl.BlockSpec((lp, SSD_GW), lambda g, b: (b, g)), pl.BlockSpec((lp, SSD_N), lambda g, b: (b, g)),
                   pl.BlockSpec((lp, SSD_N), lambda g, b: (b, g)), pl.BlockSpec((lp, DT_L), lambda g, b: (b, g)), par, par, par],
        out_shape=[jax.ShapeDtypeStruct((t, 1024), BF16), jax.ShapeDtypeStruct((t, 256), BF16),
                   jax.ShapeDtypeStruct((t, 256), BF16), jax.ShapeDtypeStruct((t, SSD_GROUPS * DT_L), F32),
                   par_shape, par_shape, par_shape],
        scratch_shapes=[pltpu.VMEM((lp, DT_L), F32), pltpu.VMEM((lp, SSD_GW), F32), pltpu.VMEM((nc, SSD_N, SSD_GW), BF16)],
        compiler_params=_cparams("parallel", "arbitrary"), name="ssd_bwd")(
            xbc, xbc, xbc, proj, dt_bias, a_log, dskip, dy)


Z_COL0 = 2048 // SSD_GW


def _gnorm_fwd(y, proj, w, name="gnorm_fwd"):
    t = y.shape[0]
    tm = _row_tile(t)

    def body(y_ref, z_ref, w_ref, o_ref):
        z = z_ref[...]
        gt = y_ref[...] * z * _sigmoid(z)
        r = lax.rsqrt(jnp.mean(gt * gt, axis=-1, keepdims=True) + EPS)
        o_ref[...] = (gt * r * w_ref[...]).astype(BF16)

    return pl.pallas_call(
        body, grid=(t // tm, SSD_GROUPS),
        in_specs=[pl.BlockSpec((tm, SSD_GW), lambda i, g: (i, g)), pl.BlockSpec((tm, SSD_GW), lambda i, g: (i, Z_COL0 + g)),
                  pl.BlockSpec((1, SSD_GW), lambda i, g: (0, g))],
        out_specs=pl.BlockSpec((tm, SSD_GW), lambda i, g: (i, g)),
        out_shape=jax.ShapeDtypeStruct((t, 1024), BF16),
        compiler_params=_cparams("parallel", "parallel"), name=name)(y, proj, w)


def _gnorm_bwd(y, proj, w, dmixed):
    t = y.shape[0]
    tm = _row_tile(t)

    def body(y_ref, z_ref, w_ref, d_ref, dy_ref, dz_ref, dw_ref):
        @pl.when(pl.program_id(1) == 0)
        def _():
            dw_ref[...] = jnp.zeros_like(dw_ref)
        z, yv, d = z_ref[...], y_ref[...], d_ref[...]
        s = _sigmoid(z)
        sz = z * s
        gt = yv * sz
        r = lax.rsqrt(jnp.mean(gt * gt, axis=-1, keepdims=True) + EPS)
        gh = gt * r
        dgn = d * w_ref[...]
        dgt = r * (dgn - gh * jnp.mean(dgn * gh, axis=-1, keepdims=True))
        dw_ref[...] += jnp.sum(d * gh, axis=0, keepdims=True)
        dy_ref[...] = dgt * sz
        dz_ref[...] = (dgt * yv * s * (1.0 + z * (1.0 - s))).astype(BF16)

    blk = pl.BlockSpec((tm, SSD_GW), lambda g, i: (i, g))
    return pl.pallas_call(
        body, grid=(SSD_GROUPS, t // tm),
        in_specs=[blk, pl.BlockSpec((tm, SSD_GW), lambda g, i: (i, Z_COL0 + g)), pl.BlockSpec((1, SSD_GW), lambda g, i: (0, g)),
                  pl.BlockSpec((tm, SSD_GW), lambda g, i: (i, 1024 // SSD_GW + g))],
        out_specs=[blk, blk, pl.BlockSpec((1, SSD_GW), lambda g, i: (0, g))],
        out_shape=[jax.ShapeDtypeStruct((t, 1024), F32), jax.ShapeDtypeStruct((t, 1024), BF16),
                   jax.ShapeDtypeStruct((1, 1024), F32)],
        compiler_params=_cparams("parallel", "arbitrary"), name="gnorm_bwd")(y, proj, w, dmixed)


def _meta_grad(dh0, bsz, lp):
    def body(d_ref, o_ref):
        @pl.when(pl.program_id(0) == 0)
        def _():
            o_ref[...] = jnp.zeros_like(o_ref)
        o_ref[...] += d_ref[...]

    return pl.pallas_call(
        body, grid=(bsz,),
        in_specs=[pl.BlockSpec((N_META, D_MODEL), lambda b: (b * (lp // N_META) + PAD // N_META, 0))],
        out_specs=pl.BlockSpec((N_META, D_MODEL), lambda b: (0, 0)),
        out_shape=jax.ShapeDtypeStruct((N_META, D_MODEL), F32),
        compiler_params=_cparams("arbitrary"), name="meta_grad")(dh0)


def _exchange(gather, scatter, name):
    ng, ns = len(gather), len(scatter)
    n = ng + ns

    def body(*refs):
        ins, outs = refs[:n], refs[n:2 * n]
        send_sems, recv_sems, local_sems = refs[2 * n:]
        x, y, c = lax.axis_index("x"), lax.axis_index("y"), lax.axis_index("c")
        me = 4 * x + 2 * y + c

        def peer(k):
            px = 1 - x if k & 4 else x
            py = 1 - y if k & 2 else y
            pc = 1 - c if k & 1 else c
            return (px, py, pc), 4 * px + 2 * py + pc

        local, sends, recvs = [], [], []
        for a in range(n):
            is_gather = a < ng
            src_own = ins[a] if is_gather else ins[a].at[me]
            local.append(pltpu.make_async_copy(src_own, outs[a].at[me], local_sems.at[a]))
            for k in range(1, N_DEV):
                dev, pid = peer(k)
                src = ins[a] if is_gather else ins[a].at[pid]
                sends.append(pltpu.make_async_remote_copy(
                    src_ref=src, dst_ref=outs[a].at[me], send_sem=send_sems.at[a, k - 1], recv_sem=recv_sems.at[a, k - 1],
                    device_id=dev, device_id_type=pl.DeviceIdType.MESH))
                recvs.append(pltpu.make_async_remote_copy(
                    src_ref=src, dst_ref=outs[a].at[pid], send_sem=send_sems.at[a, k - 1], recv_sem=recv_sems.at[a, k - 1],
                    device_id=dev, device_id_type=pl.DeviceIdType.MESH))
        for cp in local + sends:
            cp.start()
        for cp in recvs:
            cp.wait_recv()
        for cp in sends:
            cp.wait_send()
        for cp in local:
            cp.wait()

    anyspec = pl.BlockSpec(memory_space=pl.ANY)
    out_shape = [jax.ShapeDtypeStruct((N_DEV,) + a.shape, a.dtype) for a in gather]
    out_shape += [jax.ShapeDtypeStruct(a.shape, a.dtype) for a in scatter]
    return pl.pallas_call(
        body, in_specs=[anyspec] * n, out_specs=[anyspec] * n, out_shape=out_shape,
        scratch_shapes=[pltpu.SemaphoreType.DMA((n, N_DEV - 1)), pltpu.SemaphoreType.DMA((n, N_DEV - 1)),
                        pltpu.SemaphoreType.DMA((n,))],
        compiler_params=pltpu.CompilerParams(has_side_effects=True), name=name)(*gather, *scatter)


def _adamw(parts, w, m, v, name):
    r, c = w.shape
    tr = r
    for cand in (128, 64, 32, 16, 8):
        if r % cand == 0 and r > cand:
            tr = cand
            break

    def body(p_ref, w_ref, m_ref, v_ref, g_ref, d_ref, nm_ref, nv_ref):
        g = p_ref[0].astype(F32)
        for k in range(1, N_DEV):
            g = g + p_ref[k].astype(F32)
        m_new = ADAM_B1 * m_ref[...] + (1.0 - ADAM_B1) * g
        v_new = ADAM_B2 * v_ref[...] + (1.0 - ADAM_B2) * (g * g)
        m_hat = m_new / (1.0 - ADAM_B1 ** ADAM_STEP)
        v_hat = v_new / (1.0 - ADAM_B2 ** ADAM_STEP)
        g_ref[...] = g
        d_ref[...] = -ADAM_LR * (m_hat / (jnp.sqrt(v_hat) + ADAM_EPS) + ADAM_WD * w_ref[...])
        nm_ref[...] = m_new
        nv_ref[...] = v_new

    blk = pl.BlockSpec((tr, c), lambda i: (i, 0))
    shp = jax.ShapeDtypeStruct((r, c), F32)
    return pl.pallas_call(
        body, grid=(r // tr,), in_specs=[pl.BlockSpec((N_DEV, tr, c), lambda i: (0, i, 0)), blk, blk, blk],
        out_specs=[blk] * 4, out_shape=[shp] * 4, compiler_params=_cparams("parallel"), name=name)(parts, w, m, v)


def _rows128(a):
    return a.reshape(-1, 128)


def _pad_rows(a, rows):
    return jnp.pad(a, ((0, rows - a.shape[0]), (0, 0)))


def _lane16(a):
    return jnp.pad(a.reshape(1, -1), ((0, 0), (0, 128 - a.size)))


SHARD_PACK_ROWS = 544
REPL_PACK_ROWS = 72


def _pack_shard(meta, lru_conv_w, odd_norm, ssd_conv_w, lru_w_a, lru_w_x):
    parts = [meta.reshape(16, 128), lru_conv_w.reshape(4, 128), odd_norm.reshape(1, 128), _rows128(ssd_conv_w.reshape(4, 192)),
             _rows128(lru_w_a.reshape(4, 32, 256)), _rows128(lru_w_x.reshape(4, 32, 256))]
    return _pad_rows(jnp.concatenate(parts, axis=0), SHARD_PACK_ROWS)


def _unpack_shard(p):
    return (p[0:16], p[16:20].reshape(1, 4, 128), p[20:21], p[21:27].reshape(1, 4, 192),
            p[27:283].reshape(1, 4, 32, 256), p[283:539].reshape(1, 4, 32, 256))


def _pack_repl(even_norm, lru_conv_b, lru_b_a, lru_b_x, lru_lambda, ssd_norm, final_norm, ssd_conv_b, dt_bias, a_log, ssd_d):
    parts = [_rows128(v) for v in (even_norm, lru_conv_b, lru_b_a, lru_b_x, lru_lambda, ssd_norm, final_norm, ssd_conv_b)]
    parts += [_lane16(dt_bias), _lane16(a_log), _lane16(ssd_d)]
    return _pad_rows(jnp.concatenate(parts, axis=0), REPL_PACK_ROWS)


def _unpack_repl(p):
    vec = lambda i: p[8 * i:8 * i + 8].reshape(1, 1024)
    return (vec(0), vec(1), vec(2), vec(3), vec(4), vec(5), p[48:56].reshape(1024), p[56:68].reshape(1, 1536),
            p[68:69, :16], p[69:70, :16], p[70:71, :16])


def _local_step(x, tgt, meta, even_norm, w_in_e, lru_conv_w, lru_conv_b, lru_w_a, lru_b_a, lru_w_x, lru_b_x, lru_lambda,
                ssd_conv_w, ssd_conv_b, dt_bias, a_log, ssd_d, ssd_norm, w_out_e, odd_norm, w_in_o, w_out_o, final_norm):
    bsz, seq, d = x.shape
    lp = PAD + N_META + seq
    t = bsz * lp
    h0 = jnp.concatenate([jnp.zeros((bsz, PAD, d), F32), jnp.broadcast_to(meta[None], (bsz, N_META, d)), x], axis=1).reshape(t, d)

    u0 = _norm_fwd(h0, even_norm, "norm0_fwd")
    proj = _mm_nn(u0, w_in_e, F32, "even_in")
    lru = (lru_conv_w, lru_conv_b, lru_w_a, lru_b_a, lru_w_x, lru_b_x, lru_lambda)
    ya = _lru_fwd(proj, *lru, bsz, lp)
    xbc = _ssd_pre_fwd(proj, ssd_conv_w, ssd_conv_b, bsz, lp)
    y = _ssd_fwd(xbc, proj, dt_bias, a_log, ssd_d, bsz, lp)
    yb = _gnorm_fwd(y, proj, ssd_norm)
    mixed = jnp.concatenate([ya, yb], axis=1)
    h1 = _mm_nn(mixed, w_out_e, F32, "even_out", res=h0)
    u2 = _norm_fwd(h1, odd_norm, "norm1_fwd")
    qkvg = _mm_nn(u2, w_in_o, F32, "odd_in")
    o, og = _sb_fwd(qkvg, bsz, lp)
    h2 = _mm_nn(og, w_out_o, F32, "odd_out", res=h1)
    loss, dh2, d_final_norm = _final_loss(h2, final_norm, tgt, lp)

    dog = _mm_nt(dh2, w_out_o, F32, "odd_out_dx")
    d_w_out_o = _mm_tn(og, dh2, "odd_out_dw")
    dqkvg = jnp.concatenate(_sb_bwd(qkvg, o, dog, bsz, lp), axis=1)
    du2 = _mm_nt(dqkvg, w_in_o, F32, "odd_in_dx")
    d_w_in_o = _mm_tn(u2, dqkvg, "odd_in_dw")
    dh1, d_odd_norm = _norm_bwd(h1, odd_norm, du2, dh2, "norm1_bwd")

    dmixed = _mm_nt(dh1, w_out_e, F32, "even_out_dx")
    d_w_out_e = _mm_tn(mixed, dh1, "even_out_dw")
    dlx, dgate, d_lru_conv_w, d_lru_conv_b, d_w_a, d_b_a, d_w_x, d_b_x, d_lambda = _lru_bwd(proj, dmixed, *lru, bsz, lp)
    dy, dz, d_ssd_norm = _gnorm_bwd(y, proj, ssd_norm, dmixed)
    dxs, dbm, dcm, ddt, d_dt_bias, d_a_log, d_ssd_d = _ssd_bwd(xbc, proj, dt_bias, a_log, ssd_d, dy, bsz, lp)
    dxbc, d_ssd_conv_w, d_ssd_conv_b = _ssd_pre_bwd(proj, jnp.concatenate([dxs, dbm, dcm], axis=1), ssd_conv_w, ssd_conv_b, bsz, lp)
    ddt = (ddt[:, :DT_L] + ddt[:, DT_L:]).astype(BF16)
    dproj = jnp.concatenate([dlx, dgate, dz, dxbc, ddt, jnp.zeros((t, EVEN_NP - 4608 - DT_L), BF16)], axis=1)
    du0 = _mm_nt(dproj, w_in_e, F32, "even_in_dx")
    d_w_in_e = _mm_tn(u0, dproj, "even_in_dw")
    dh0, d_even_norm = _norm_bwd(h0, even_norm, du0, dh1, "norm0_bwd")
    grad_x = dh0.reshape(bsz, lp, d)[:, PAD + N_META:]
    d_meta = _meta_grad(dh0, bsz, lp)
    heads = lambda p: (p[0:1] + p[8:9])[:, :SSD_HEADS]
    grads = dict(meta=d_meta, even_norm=d_even_norm, even_w_in=d_w_in_e[:, :EVEN_IN], lru_conv_w=d_lru_conv_w,
                 lru_conv_b=d_lru_conv_b, lru_w_a=d_w_a, lru_b_a=d_b_a, lru_w_x=d_w_x, lru_b_x=d_b_x, lru_lambda=d_lambda,
                 ssd_conv_w=d_ssd_conv_w, ssd_conv_b=d_ssd_conv_b, ssd_dt_bias=heads(d_dt_bias), ssd_a_log=heads(d_a_log),
                 ssd_d=heads(d_ssd_d), ssd_norm=d_ssd_norm, even_w_out=d_w_out_e, odd_norm=d_odd_norm, odd_w_in=d_w_in_o,
                 odd_w_out=d_w_out_o, final_norm=d_final_norm)
    return loss[0, 0], grad_x, grads


WEIGHTS = ['meta', 'even_norm', 'even_w_in', 'lru_conv_w', 'lru_conv_b', 'lru_w_a', 'lru_b_a', 'lru_w_x', 'lru_b_x', 'lru_lambda',
           'ssd_conv_w', 'ssd_conv_b', 'ssd_dt_bias', 'ssd_a_log', 'ssd_d', 'ssd_norm', 'even_w_out', 'odd_norm', 'odd_w_in',
           'odd_w_out', 'final_norm']


def _blocks_of(a, axis):
    shp = a.shape
    a = a.reshape(shp[:axis] + (N_DEV, shp[axis] // N_DEV) + shp[axis + 1:])
    return jnp.moveaxis(a, axis, 0)


def _unblock(a, axis):
    a = jnp.moveaxis(a, 0, axis)
    shp = a.shape
    return a.reshape(shp[:axis] + (shp[axis] * shp[axis + 1],) + shp[axis + 2:])


def kernel(x, meta, even_norm, even_w_in, lru_conv_w, lru_conv_b, lru_w_a, lru_b_a, lru_w_x, lru_b_x, lru_lambda, ssd_conv_w, ssd_conv_b, ssd_dt_bias, ssd_a_log, ssd_d, ssd_norm, even_w_out, odd_norm, odd_w_in, odd_w_out, final_norm, loss_target, m_meta, m_even_norm, m_even_w_in, m_lru_conv_w, m_lru_conv_b, m_lru_w_a, m_lru_b_a, m_lru_w_x, m_lru_b_x, m_lru_lambda, m_ssd_conv_w, m_ssd_conv_b, m_ssd_dt_bias, m_ssd_a_log, m_ssd_d, m_ssd_norm, m_even_w_out, m_odd_norm, m_odd_w_in, m_odd_w_out, m_final_norm, v_meta, v_even_norm, v_even_w_in, v_lru_conv_w, v_lru_conv_b, v_lru_w_a, v_lru_b_a, v_lru_w_x, v_lru_b_x, v_lru_lambda, v_ssd_conv_w, v_ssd_conv_b, v_ssd_dt_bias, v_ssd_a_log, v_ssd_d, v_ssd_norm, v_even_w_out, v_odd_norm, v_odd_w_in, v_odd_w_out, v_final_norm):
    w = dict(meta=meta, even_norm=even_norm, even_w_in=even_w_in, lru_conv_w=lru_conv_w, lru_conv_b=lru_conv_b, lru_w_a=lru_w_a,
             lru_b_a=lru_b_a, lru_w_x=lru_w_x, lru_b_x=lru_b_x, lru_lambda=lru_lambda, ssd_conv_w=ssd_conv_w,
             ssd_conv_b=ssd_conv_b, ssd_dt_bias=ssd_dt_bias, ssd_a_log=ssd_a_log, ssd_d=ssd_d, ssd_norm=ssd_norm,
             even_w_out=even_w_out, odd_norm=odd_norm, odd_w_in=odd_w_in, odd_w_out=odd_w_out, final_norm=final_norm)
    m = dict(meta=m_meta, even_norm=m_even_norm, even_w_in=m_even_w_in, lru_conv_w=m_lru_conv_w, lru_conv_b=m_lru_conv_b,
             lru_w_a=m_lru_w_a, lru_b_a=m_lru_b_a, lru_w_x=m_lru_w_x, lru_b_x=m_lru_b_x, lru_lambda=m_lru_lambda,
             ssd_conv_w=m_ssd_conv_w, ssd_conv_b=m_ssd_conv_b, ssd_dt_bias=m_ssd_dt_bias, ssd_a_log=m_ssd_a_log, ssd_d=m_ssd_d,
             ssd_norm=m_ssd_norm, even_w_out=m_even_w_out, odd_norm=m_odd_norm, odd_w_in=m_odd_w_in, odd_w_out=m_odd_w_out,
             final_norm=m_final_norm)
    v = dict(meta=v_meta, even_norm=v_even_norm, even_w_in=v_even_w_in, lru_conv_w=v_lru_conv_w, lru_conv_b=v_lru_conv_b,
             lru_w_a=v_lru_w_a, lru_b_a=v_lru_b_a, lru_w_x=v_lru_w_x, lru_b_x=v_lru_b_x, lru_lambda=v_lru_lambda,
             ssd_conv_w=v_ssd_conv_w, ssd_conv_b=v_ssd_conv_b, ssd_dt_bias=v_ssd_dt_bias, ssd_a_log=v_ssd_a_log, ssd_d=v_ssd_d,
             ssd_norm=v_ssd_norm, even_w_out=v_even_w_out, odd_norm=v_odd_norm, odd_w_in=v_odd_w_in, odd_w_out=v_odd_w_out,
             final_norm=v_final_norm)
    shard_names = ('meta', 'lru_conv_w', 'odd_norm', 'ssd_conv_w', 'lru_w_a', 'lru_w_x')
    repl_names = ('even_norm', 'lru_conv_b', 'lru_b_a', 'lru_b_x', 'lru_lambda', 'ssd_norm', 'final_norm', 'ssd_conv_b',
                  'ssd_dt_bias', 'ssd_a_log', 'ssd_d')
    big_names = ('even_w_in', 'even_w_out', 'odd_w_in', 'odd_w_out')

    gates = jnp.concatenate([lru_w_a.reshape(128, 256), lru_w_x.reshape(128, 256)], axis=0).astype(BF16)
    small = _pack_shard(*[w[k] for k in shard_names])
    g_in_e, g_out_e, g_in_o, g_out_o, g_gates, g_small = _exchange(
        [even_w_in[0].astype(BF16), even_w_out[0].astype(BF16), odd_w_in[0].astype(BF16), odd_w_out[0].astype(BF16), gates, small],
        [], "gather_weights")
    w_in_e = jnp.pad(_unblock(g_in_e, 1), ((0, 0), (0, EVEN_NP - EVEN_IN)))
    w_out_e = g_out_e.reshape(2048, 1024)
    w_in_o = _unblock(g_in_o, 1)
    w_out_o = g_out_o.reshape(1024, 1024)
    gates_full = jnp.moveaxis(g_gates.reshape(N_DEV, 2, 4, 32, 256), 0, 2).reshape(2, 4, 256, 256)
    f_meta = _unblock(g_small[:, 0:16], 1)
    f_lru_conv_w = _unblock(g_small[:, 16:20], 1)
    f_odd_norm = _unblock(g_small[:, 20:21], 1)
    f_ssd_conv_w = _unblock(g_small[:, 21:27].reshape(N_DEV, 4, 192), 1)

    loss, grad_x, g = _local_step(
        x, loss_target, f_meta, even_norm, w_in_e, f_lru_conv_w, lru_conv_b, gates_full[0], lru_b_a, gates_full[1], lru_b_x,
        lru_lambda, f_ssd_conv_w, ssd_conv_b, _lane16(ssd_dt_bias), _lane16(ssd_a_log), _lane16(ssd_d), ssd_norm, w_out_e,
        f_odd_norm, w_in_o, w_out_o, final_norm.reshape(1, -1))
    loss = lax.psum(loss, ("x", "y", "c"))

    s_small = jnp.stack([_pack_shard(g['meta'][:, 128 * p:128 * (p + 1)], g['lru_conv_w'][:, 128 * p:128 * (p + 1)],
                                     g['odd_norm'][:, 128 * p:128 * (p + 1)], g['ssd_conv_w'][:, 192 * p:192 * (p + 1)],
                                     g['lru_w_a'][:, 32 * p:32 * (p + 1)], g['lru_w_x'][:, 32 * p:32 * (p + 1)])
                         for p in range(N_DEV)])
    r_pack = _pack_repl(*[g[k] for k in repl_names])
    p_repl, p_in_e, p_out_e, p_in_o, p_out_o, p_small = _exchange(
        [r_pack],
        [_blocks_of(g['even_w_in'], 1).astype(BF16), g['even_w_out'].reshape(N_DEV, 256, 1024).astype(BF16),
         _blocks_of(g['odd_w_in'], 1).astype(BF16), g['odd_w_out'].reshape(N_DEV, 128, 1024).astype(BF16), s_small],
        "reduce_grads")

    res = {}
    for k, parts in zip(big_names, (p_in_e, p_out_e, p_in_o, p_out_o)):
        outs = _adamw(parts, w[k][0], m[k][0], v[k][0], "adamw_" + k)
        res[k] = [o[None] for o in outs]
    outs = _adamw(p_small, small, _pack_shard(*[m[k] for k in shard_names]), _pack_shard(*[v[k] for k in shard_names]), "adamw_sharded")
    unpacked = [_unpack_shard(o) for o in outs]
    for i, k in enumerate(shard_names):
        res[k] = [u[i] for u in unpacked]
    outs = _adamw(p_repl, _pack_repl(*[w[k] for k in repl_names]), _pack_repl(*[m[k] for k in repl_names]),
                  _pack_repl(*[v[k] for k in repl_names]), "adamw_replicated")
    unpacked = [_unpack_repl(o) for o in outs]
    for i, k in enumerate(repl_names):
        res[k] = [u[i] for u in unpacked]
    return (loss, grad_x, *[res[k][0] for k in WEIGHTS], *[res[k][1] for k in WEIGHTS], *[res[k][2] for k in WEIGHTS],
            *[res[k][3] for k in WEIGHTS])
```

```python
import functools

import jax
import jax.numpy as jnp
from jax import lax
from jax.experimental import pallas as pl
from jax.experimental.pallas import tpu as pltpu

F32 = jnp.float32
BF16 = jnp.bfloat16

D_MODEL = 1024
N_META = 16
PAD = 112
EPS = 1e-6
CONV_W = 4
LRU_BLOCKS = 4
LRU_BLOCK = 256
RG_LRU_C = 8.0
SSD_HEADS = 16
SSD_P = 64
SSD_N = 128
SSD_Q = 64
SSD_GROUPS = 2
SSD_GW = 512
SSD_CONV_DIM = 1536
SB_HEADS = 16
SB_D = 64
SB_T = 128
EVEN_IN = 4624
EVEN_NP = 4864
DT_W = 256
ODD_IN = 4096
N_DEV = 8

ADAM_LR = 0.001
ADAM_B1 = 0.9
ADAM_B2 = 0.999
ADAM_EPS = 1e-08
ADAM_WD = 0.01
ADAM_STEP = 10

VMEM_LIMIT = 56 * 1024 * 1024


def _cparams(*sem):
    return pltpu.CompilerParams(dimension_semantics=sem, vmem_limit_bytes=VMEM_LIMIT)


def _row_tile(t):
    for c in (512, 256, 128):
        if t % c == 0:
            return c
    raise ValueError(t)


def _col_tile(n):
    for c in (512, 256, 128):
        if n % c == 0:
            return c
    raise ValueError(n)


def _dot(a, b):
    return jnp.dot(a, b, preferred_element_type=F32)


def _dot_nt(a, b):
    return lax.dot_general(a, b, (((1,), (1,)), ((), ())), preferred_element_type=F32)


def _dot_tn(a, b):
    return lax.dot_general(a, b, (((0,), (0,)), ((), ())), preferred_element_type=F32)


def _split3(a):
    a1 = a.astype(BF16)
    r1 = a - a1.astype(F32)
    a2 = r1.astype(BF16)
    a3 = (r1 - a2.astype(F32)).astype(BF16)
    return a1, a2, a3


def _xdot_r(a, m01):
    a1, a2, a3 = _split3(a)
    return _dot(a1, m01) + _dot(a2, m01) + _dot(a3, m01)


def _xdot_l(m01, a):
    a1, a2, a3 = _split3(a)
    return _dot(m01, a1) + _dot(m01, a2) + _dot(m01, a3)


def _xdot_tn_l(m01, a):
    a1, a2, a3 = _split3(a)
    return _dot_tn(m01, a1) + _dot_tn(m01, a2) + _dot_tn(m01, a3)


def _sigmoid(x):
    return 0.5 * jnp.tanh(0.5 * x) + 0.5


def _softplus(x):
    return jnp.maximum(x, 0.0) + jnp.log(1.0 + jnp.exp(-jnp.abs(x)))


def _iota(shape, dim):
    return lax.broadcasted_iota(jnp.int32, shape, dim)


def _block_scan(a_ref, b_ref, o_ref, reverse=False):
    n, w = a_ref.shape
    nb = n // 8
    unroll = 4 if nb % 4 == 0 else 1
    row = _iota((8, w), 0)

    def block(blk, carry):
        rows = pl.ds(pl.multiple_of(blk * 8, 8), 8)
        a, b = a_ref[rows, :], b_ref[rows, :]
        for k in (1, 2, 4):
            keep = (row < 8 - k) if reverse else (row >= k)
            shift = 8 - k if reverse else k
            a_sh = jnp.where(keep, pltpu.roll(a, shift, 0), 1.0)
            b_sh = jnp.where(keep, pltpu.roll(b, shift, 0), 0.0)
            b = a * b_sh + b
            a = a * a_sh
        o = a * carry + b
        o_ref[rows, :] = o
        return o[0:1, :] if reverse else o[7:8, :]

    def trip(i, carry):
        for u in range(unroll):
            j = i * unroll + u
            carry = block(nb - 1 - j if reverse else j, carry)
        return carry

    lax.fori_loop(0, nb // unroll, trip, jnp.zeros((1, w), F32))


def _chunks(n, c):
    return [(s, min(c, n - s)) for s in range(0, n, c)]


def _mm_nn(a, b, out_dtype, name, res=None):
    m, k = a.shape
    _, n = b.shape
    tm = _row_tile(m)

    def body(*refs):
        if res is None:
            a_ref, b_ref, o_ref = refs
        else:
            a_ref, b_ref, r_ref, o_ref = refs
        av = a_ref[...].astype(BF16)
        for s, w in _chunks(n, 512):
            acc = _dot(av, b_ref[:, s:s + w])
            if res is not None:
                acc = acc + r_ref[:, s:s + w]
            o_ref[:, s:s + w] = acc.astype(out_dtype)

    in_specs = [pl.BlockSpec((tm, k), lambda i: (i, 0)), pl.BlockSpec((k, n), lambda i: (0, 0))]
    args = [a, b]
    if res is not None:
        in_specs.append(pl.BlockSpec((tm, n), lambda i: (i, 0)))
        args.append(res)
    return pl.pallas_call(
        body, grid=(m // tm,), in_specs=in_specs,
        out_specs=pl.BlockSpec((tm, n), lambda i: (i, 0)),
        out_shape=jax.ShapeDtypeStruct((m, n), out_dtype),
        compiler_params=_cparams("parallel"), name=name)(*args)


def _mm_nt(a, b, out_dtype, name):
    m, k = a.shape
    n, _ = b.shape
    tm, tn = _row_tile(m), _col_tile(n)

    def body(a_ref, b_ref, o_ref):
        o_ref[...] = _dot_nt(a_ref[...].astype(BF16), b_ref[...].astype(BF16)).astype(out_dtype)

    return pl.pallas_call(
        body, grid=(m // tm, n // tn),
        in_specs=[pl.BlockSpec((tm, k), lambda i, j: (i, 0)), pl.BlockSpec((tn, k), lambda i, j: (j, 0))],
        out_specs=pl.BlockSpec((tm, tn), lambda i, j: (i, j)),
        out_shape=jax.ShapeDtypeStruct((m, n), out_dtype),
        compiler_params=_cparams("parallel", "parallel"), name=name)(a, b)


def _mm_tn(a, b, name):
    t, m = a.shape
    _, n = b.shape
    tk = _row_tile(t)
    halves = 2 if (m * n * 4 > 8 * 1024 * 1024 and n % 256 == 0) else 1
    tn = n // halves

    def body(a_ref, b_ref, o_ref):
        @pl.when(pl.program_id(1) == 0)
        def _():
            o_ref[...] = jnp.zeros_like(o_ref)
        at = a_ref[...].astype(BF16).T
        for s, w in _chunks(tn, 512):
            o_ref[:, s:s + w] += _dot(at, b_ref[:, s:s + w].astype(BF16))

    return pl.pallas_call(
        body, grid=(halves, t // tk),
        in_specs=[pl.BlockSpec((tk, m), lambda j, s: (s, 0)), pl.BlockSpec((tk, tn), lambda j, s: (s, j))],
        out_specs=pl.BlockSpec((m, tn), lambda j, s: (0, j)),
        out_shape=jax.ShapeDtypeStruct((m, n), F32),
        compiler_params=_cparams("parallel", "arbitrary"), name=name)(a, b)


def _norm_fwd(h, w, name):
    t, d = h.shape
    tm = _row_tile(t)

    def body(h_ref, w_ref, u_ref):
        x = h_ref[...]
        r = lax.rsqrt(jnp.mean(x * x, axis=-1, keepdims=True) + EPS)
        u_ref[...] = (x * r * w_ref[...]).astype(BF16)

    return pl.pallas_call(
        body, grid=(t // tm,),
        in_specs=[pl.BlockSpec((tm, d), lambda i: (i, 0)), pl.BlockSpec((1, d), lambda i: (0, 0))],
        out_specs=pl.BlockSpec((tm, d), lambda i: (i, 0)),
        out_shape=jax.ShapeDtypeStruct((t, d), BF16),
        compiler_params=_cparams("parallel"), name=name)(h, w)


def _norm_bwd(h, w, du, dres, name):
    t, d = h.shape
    tm = _row_tile(t)

    def body(h_ref, w_ref, du_ref, dr_ref, dh_ref, dw_ref):
        @pl.when(pl.program_id(0) == 0)
        def _():
            dw_ref[...] = jnp.zeros_like(dw_ref)
        x = h_ref[...]
        r = lax.rsqrt(jnp.mean(x * x, axis=-1, keepdims=True) + EPS)
        xh = x * r
        du_ = du_ref[...]
        g = du_ * w_ref[...]
        dh_ref[...] = dr_ref[...] + r * (g - xh * jnp.mean(g * xh, axis=-1, keepdims=True))
        dw_ref[...] += jnp.sum(du_ * xh, axis=0, keepdims=True)

    return pl.pallas_call(
        body, grid=(t // tm,),
        in_specs=[pl.BlockSpec((tm, d), lambda i: (i, 0)), pl.BlockSpec((1, d), lambda i: (0, 0)),
                  pl.BlockSpec((tm, d), lambda i: (i, 0)), pl.BlockSpec((tm, d), lambda i: (i, 0))],
        out_specs=[pl.BlockSpec((tm, d), lambda i: (i, 0)), pl.BlockSpec((1, d), lambda i: (0, 0))],
        out_shape=[jax.ShapeDtypeStruct((t, d), F32), jax.ShapeDtypeStruct((1, d), F32)],
        compiler_params=_cparams("arbitrary"), name=name)(h, w, du, dres)


def _final_loss(h2, w, tgt, lp):
    t, d = h2.shape
    bsz, seq, _ = tgt.shape
    nblk = lp // SB_T

    def body(h_ref, w_ref, t_ref, loss_ref, dh_ref, dw_ref):
        b, i = pl.program_id(0), pl.program_id(1)

        @pl.when((b == 0) & (i == 0))
        def _():
            loss_ref[...] = jnp.zeros_like(loss_ref)
            dw_ref[...] = jnp.zeros_like(dw_ref)

        @pl.when(i == 0)
        def _():
            dh_ref[...] = jnp.zeros_like(dh_ref)

        @pl.when(i > 0)
        def _():
            x = h_ref[...]
            r = lax.rsqrt(jnp.mean(x * x, axis=-1, keepdims=True) + EPS)
            xh = x * r
            wv = w_ref[...]
            diff = xh * wv - t_ref[0]
            loss_ref[...] += 0.5 * jnp.sum(jnp.mean(diff * diff, axis=-1, keepdims=True), axis=0, keepdims=True)
            dy = diff * (1.0 / d)
            g = dy * wv
            dh_ref[...] = r * (g - xh * jnp.mean(g * xh, axis=-1, keepdims=True))
            dw_ref[...] += jnp.sum(dy * xh, axis=0, keepdims=True)

    return pl.pallas_call(
        body, grid=(bsz, nblk),
        in_specs=[pl.BlockSpec((SB_T, d), lambda b, i: (b * nblk + i, 0)), pl.BlockSpec((1, d), lambda b, i: (0, 0)),
                  pl.BlockSpec((1, SB_T, d), lambda b, i: (b, jnp.maximum(i - 1, 0), 0))],
        out_specs=[pl.BlockSpec((1, 128), lambda b, i: (0, 0)), pl.BlockSpec((SB_T, d), lambda b, i: (b * nblk + i, 0)),
                   pl.BlockSpec((1, d), lambda b, i: (0, 0))],
        out_shape=[jax.ShapeDtypeStruct((1, 128), F32), jax.ShapeDtypeStruct((t, d), F32), jax.ShapeDtypeStruct((1, d), F32)],
        compiler_params=_cparams("arbitrary", "arbitrary"), name="final_loss")(h2, w, tgt)


def _shift_down(x, j):
    return x if j == 0 else pltpu.roll(x, j, 0)


def _shift_up(x, j):
    return x if j == 0 else pltpu.roll(x, x.shape[0] - j, 0)


def _conv(x, cw, cb):
    out = cb + cw[CONV_W - 1:CONV_W, :] * x
    for j in range(1, CONV_W):
        out = out + cw[CONV_W - 1 - j:CONV_W - j, :] * _shift_down(x, j)
    return out


def _conv_bwd_x(dy, cw):
    out = cw[CONV_W - 1:CONV_W, :] * dy
    for j in range(1, CONV_W):
        out = out + cw[CONV_W - 1 - j:CONV_W - j, :] * _shift_up(dy, j)
    return out


def _conv_bwd_w(dcw_ref, dy, x):
    for j in range(CONV_W):
        k = CONV_W - 1 - j
        dcw_ref[k:k + 1, :] += jnp.sum(dy * _shift_down(x, j), axis=0, keepdims=True)


def _lru_forward_block(x, cw, cb, wa, ba, wx, bx, lam, a_ref, b_ref, h_ref):
    lp = x.shape[0]
    lx = _conv(x, cw, cb)
    lxb = lx.astype(BF16)
    r = _sigmoid(_dot(lxb, wa) + ba)
    i = _sigmoid(_dot(lxb, wx) + bx)
    spl = _softplus(-lam)
    log_a = (-RG_LRU_C) * r * spl
    a = jnp.exp(log_a)
    mult = jnp.sqrt(-jnp.tanh(log_a) * (1.0 + a * a))
    valid = _iota((lp, 1), 0) >= PAD
    a_ref[...] = a
    b_ref[...] = jnp.where(valid, mult * i * lx, 0.0)
    _block_scan(a_ref, b_ref, h_ref)
    return lx, lxb, r, i, spl, a, mult, valid


def _lru_specs(lp):
    w = LRU_BLOCK
    return [
        pl.BlockSpec((lp, w), lambda g, b: (b, g)),
        pl.BlockSpec((lp, w), lambda g, b: (b, LRU_BLOCKS + g)),
        pl.BlockSpec((CONV_W, w), lambda g, b: (0, g)),
        pl.BlockSpec((1, w), lambda g, b: (0, g)),
        pl.BlockSpec((1, w, w), lambda g, b: (g, 0, 0)),
        pl.BlockSpec((1, w), lambda g, b: (0, g)),
        pl.BlockSpec((1, w, w), lambda g, b: (g, 0, 0)),
        pl.BlockSpec((1, w), lambda g, b: (0, g)),
        pl.BlockSpec((1, w), lambda g, b: (0, g)),
    ]


def _lru_fwd(proj, cw, cb, wa, ba, wx, bx, lam, bsz, lp):
    w = LRU_BLOCK

    def body(x_ref, g_ref, cw_ref, cb_ref, wa_ref, ba_ref, wx_ref, bx_ref, lam_ref, y_ref, a_s, b_s, h_s):
        _lru_forward_block(x_ref[...], cw_ref[...], cb_ref[...], wa_ref[0], ba_ref[...], wx_ref[0], bx_ref[...],
                           lam_ref[...], a_s, b_s, h_s)
        gate = g_ref[...]
        y_ref[...] = (h_s[...] * gate * _sigmoid(gate)).astype(BF16)

    return pl.pallas_call(
        body, grid=(LRU_BLOCKS, bsz), in_specs=_lru_specs(lp),
        out_specs=pl.BlockSpec((lp, w), lambda g, b: (b, g)),
        out_shape=jax.ShapeDtypeStruct((bsz * lp, LRU_BLOCKS * w), BF16),
        scratch_shapes=[pltpu.VMEM((lp, w), F32)] * 3,
        compiler_params=_cparams("parallel", "arbitrary"), name="lru_fwd")(proj, proj, cw, cb, wa, ba, wx, bx, lam)


def _lru_bwd(proj, dmixed, cw, cb, wa, ba, wx, bx, lam, bsz, lp):
    w = LRU_BLOCK

    def body(x_ref, g_ref, cw_ref, cb_ref, wa_ref, ba_ref, wx_ref, bx_ref, lam_ref, dy_ref,
             dx_ref, dg_ref, dcw_ref, dcb_ref, dwa_ref, dba_ref, dwx_ref, dbx_ref, dlam_ref, a_s, b_s, h_s, dh_s):
        @pl.when(pl.program_id(1) == 0)
        def _():
            for ref in (dcw_ref, dcb_ref, dwa_ref, dba_ref, dwx_ref, dbx_ref, dlam_ref):
                ref[...] = jnp.zeros_like(ref)

        x = x_ref[...]
        cwv = cw_ref[...]
        wav, wxv, lam_ = wa_ref[0], wx_ref[0], lam_ref[...]
        lx, lxb, r, i, spl, a, mult, valid = _lru_forward_block(
            x, cwv, cb_ref[...], wav, ba_ref[...], wxv, bx_ref[...], lam_, a_s, b_s, h_s)
        gate = g_ref[...]
        sg = _sigmoid(gate)
        dy = dy_ref[...]
        h = h_s[...]
        dg_ref[...] = (dy * h * sg * (1.0 + gate * (1.0 - sg))).astype(BF16)
        b_s[...] = dy * gate * sg
        a_s[...] = jnp.where(_iota((lp, 1), 0) < lp - 1, _shift_up(a, 1), 0.0)
        _block_scan(a_s, b_s, dh_s, reverse=True)
        dh = dh_s[...]
        hprev = jnp.where(_iota((lp, 1), 0) >= 1, _shift_down(h, 1), 0.0)
        db = jnp.where(valid, dh, 0.0)
        dmult = db * i * lx
        di = db * mult * lx
        dlx = db * mult * i
        inv_mult = jnp.where(mult > 0.0, 1.0 / mult, 0.0)
        dlog_a = dh * hprev * a - dmult * (a * a) * inv_mult
        drp = dlog_a * ((-RG_LRU_C) * spl) * r * (1.0 - r)
        dip = di * i * (1.0 - i)
        dspl = jnp.sum(dlog_a * ((-RG_LRU_C) * r), axis=0, keepdims=True)
        dlam_ref[...] += dspl * (-_sigmoid(-lam_))
        dba_ref[...] += jnp.sum(drp, axis=0, keepdims=True)
        dbx_ref[...] += jnp.sum(dip, axis=0, keepdims=True)
        drpb, dipb = drp.astype(BF16), dip.astype(BF16)
        dwa_ref[0] += _dot_tn(lxb, drpb)
        dwx_ref[0] += _dot_tn(lxb, dipb)
        dlx = dlx + _dot_nt(drpb, wav) + _dot_nt(dipb, wxv)
        dcb_ref[...] += jnp.sum(dlx, axis=0, keepdims=True)
        _conv_bwd_w(dcw_ref, dlx, x)
        dx_ref[...] = _conv_bwd_x(dlx, cwv).astype(BF16)

    t = bsz * lp
    vec = pl.BlockSpec((1, w), lambda g, b: (0, g))
    mat = pl.BlockSpec((1, w, w), lambda g, b: (g, 0, 0))
    act = pl.BlockSpec((lp, w), lambda g, b: (b, g))
    return pl.pallas_call(
        body, grid=(LRU_BLOCKS, bsz), in_specs=_lru_specs(lp) + [act],
        out_specs=[act, act, pl.BlockSpec((CONV_W, w), lambda g, b: (0, g)), vec, mat, vec, mat, vec, vec],
        out_shape=[jax.ShapeDtypeStruct((t, 1024), BF16), jax.ShapeDtypeStruct((t, 1024), BF16),
                   jax.ShapeDtypeStruct((CONV_W, 1024), F32), jax.ShapeDtypeStruct((1, 1024), F32),
                   jax.ShapeDtypeStruct((LRU_BLOCKS, w, w), F32), jax.ShapeDtypeStruct((1, 1024), F32),
                   jax.ShapeDtypeStruct((LRU_BLOCKS, w, w), F32), jax.ShapeDtypeStruct((1, 1024), F32),
                   jax.ShapeDtypeStruct((1, 1024), F32)],
        scratch_shapes=[pltpu.VMEM((lp, w), F32)] * 4,
        compiler_params=_cparams("parallel", "arbitrary"), name="lru_bwd")(
            proj, proj, cw, cb, wa, ba, wx, bx, lam, dmixed)


def _sb_masks():
    j = _iota((SB_T, 2 * SB_T), 0)
    s = _iota((SB_T, 2 * SB_T), 1)
    right = ((s < SB_T) & (j > s)) | (s >= SB_T)
    left = ((s < SB_T) & (j < s)) | (s >= SB_T)
    return jnp.where(right, -1.0, 0.0).astype(BF16), left.astype(BF16)


def _xdot2(a, m01):
    a1 = a.astype(BF16)
    a2 = (a - a1.astype(F32)).astype(BF16)
    return _dot(a1, m01) + _dot(a2, m01)


def _sb_query_block(lp):
    for tq in (512, 256, 128):
        if (lp - SB_T) % tq == 0:
            return tq, (lp - SB_T) // tq
    raise ValueError(lp)


def _sb_key_rows(kj):
    return pl.ds(kj * SB_T if isinstance(kj, int) else pl.multiple_of(kj * SB_T, SB_T), SB_T)


def _sb_valid(q0, tq, kj):
    t_pos = q0 + _iota((tq, SB_T), 0)
    s_pos = kj * SB_T + _iota((tq, SB_T), 1)
    return (s_pos < t_pos) & (s_pos >= PAD)


def _sb_by_head(ref, rows):
    t = ref[rows, :]
    head0 = _iota((SB_T, 128), 1) < SB_D
    return jnp.concatenate([jnp.where(head0, t, 0.0), jnp.where(head0, 0.0, t)], axis=0).astype(BF16)


def _sb_tile(qb, k_ref, q0, kj, carries, m_right_neg, masked):
    rows_k = _sb_key_rows(kj)
    z2 = _dot_nt(qb, _sb_by_head(k_ref, rows_k))
    valid = _sb_valid(q0, qb.shape[0], kj) if masked else None
    out = []
    for h in range(2):
        z = z2[:, h * SB_T:(h + 1) * SB_T]
        sp = _softplus(z)
        rs = _dot((jnp.where(valid, sp, 0.0) if masked else sp).astype(BF16), m_right_neg)
        lb = z - sp
        wgt = jnp.exp(lb + rs[:, :SB_T] + carries[h])
        if masked:
            wgt = jnp.where(valid, wgt, 0.0)
        out.append((lb, wgt, carries[h] + rs[:, SB_T:]))
    return rows_k, valid, out


def _tail(x, row0):
    return x if row0 == 0 else x[row0:]


def _merge(old, new_tail, row0):
    return new_tail if row0 == 0 else jnp.concatenate([old[:row0], new_tail], axis=0)


def _sb_sweep(step, c, blk, tq, first, leftwards):
    if first:
        return step(0, True, 0)(c)
    r = tq // SB_T
    u = 2 if r % 2 == 0 else 1
    lo = 1 + r * blk

    def diag(c):
        for d in (range(r - 1, -1, -1) if leftwards else range(r)):
            c = step(lo + d, True, SB_T * d)(c)
        return c

    def inner(c):
        def trip(i, c):
            for j in range(u):
                d = u * i + j
                c = step(lo - 1 - d if leftwards else 1 + d, False, 0)(c)
            return c
        return lax.fori_loop(0, (r // u) * blk, trip, c)

    if leftwards:
        return step(0, True, 0)(inner(diag(c)))
    return diag(inner(step(0, True, 0)(c)))


def _sb_specs(lp):
    nh = SB_HEADS // 2
    return [pl.BlockSpec((lp, 128), lambda b, p: (b, p)), pl.BlockSpec((lp, 128), lambda b, p: (b, nh + p)),
            pl.BlockSpec((lp, 128), lambda b, p: (b, 2 * nh + p)), pl.BlockSpec((lp, 128), lambda b, p: (b, 3 * nh + p))]


def _sb_fwd(qkvg, bsz, lp):
    tq, nb = _sb_query_block(lp)
    scale = SB_D ** -0.5

    def body(q_ref, k_ref, v_ref, g_ref, o_ref, og_ref):
        m_right_neg, _ = _sb_masks()

        def q_block(blk, rows, first):
            q0 = 0 if first else (1 + (rows // SB_T) * blk) * SB_T
            rows_q = pl.ds(q0 if first else pl.multiple_of(q0, SB_T), rows)
            qb = (q_ref[rows_q, :] * scale).astype(BF16)

            def step(kj, masked, row0):
                def run(c):
                    acc, car0, car1 = c
                    rows_k, _, ((_, w0, new0), (_, w1, new1)) = _sb_tile(
                        _tail(qb, row0), k_ref, q0 + row0, kj, (_tail(car0, row0), _tail(car1, row0)), m_right_neg, masked)
                    w2 = jnp.concatenate([w0.astype(BF16), w1.astype(BF16)], axis=1)
                    upd = _tail(acc, row0) + _dot(w2, _sb_by_head(v_ref, rows_k))
                    return _merge(acc, upd, row0), _merge(car0, new0, row0), _merge(car1, new1, row0)
                return run

            zero = jnp.zeros((rows, SB_T), F32)
            o_ref[rows_q, :] = _sb_sweep(step, (zero, zero, zero), blk, rows, first, True)[0]

        q_block(0, SB_T, True)

        def big_block(i, _):
            q_block(i, tq, False)
            return 0

        lax.fori_loop(0, nb, big_block, 0)
        gate = g_ref[...]
        og_ref[...] = (o_ref[...] * gate * _sigmoid(gate)).astype(BF16)

    t = bsz * lp
    blk = pl.BlockSpec((lp, 128), lambda b, p: (b, p))
    return pl.pallas_call(
        body, grid=(bsz, SB_HEADS // 2), in_specs=_sb_specs(lp), out_specs=[blk, blk],
        out_shape=[jax.ShapeDtypeStruct((t, 1024), F32), jax.ShapeDtypeStruct((t, 1024), BF16)],
        compiler_params=_cparams("parallel", "parallel"), name="sb_fwd")(qkvg, qkvg, qkvg, qkvg)


def _sb_bwd(qkvg, o, dog, bsz, lp):
    tq, nb = _sb_query_block(lp)
    nk = lp // SB_T
    scale = SB_D ** -0.5

    def body(q_ref, k_ref, v_ref, g_ref, o_ref, dog_ref, dq_ref, dk_ref, dv_ref, dg_ref, do_s, dk_s, dv_s, e_s, sig_s):
        m_right_neg, m_left = _sb_masks()
        gate = g_ref[...]
        sg = _sigmoid(gate)
        dog = dog_ref[...]
        do_s[...] = dog * gate * sg
        dg_ref[...] = (dog * o_ref[...] * sg * (1.0 + gate * (1.0 - sg))).astype(BF16)
        dk_s[...] = jnp.zeros_like(dk_s)
        dv_s[...] = jnp.zeros_like(dv_s)
        def q_block(blk, rows, first):
            q0 = 0 if first else (1 + (rows // SB_T) * blk) * SB_T
            rows_q = pl.ds(q0 if first else pl.multiple_of(q0, SB_T), rows)
            head0 = _iota((rows, 128), 1) < SB_D
            qf = q_ref[rows_q, :] * scale
            qb = qf.astype(BF16)
            q_h = (jnp.where(head0, qf, 0.0).astype(BF16), jnp.where(head0, 0.0, qf).astype(BF16))
            do_f = do_s[rows_q, :]
            do_b = do_f.astype(BF16)
            do_h = (jnp.where(head0, do_f, 0.0).astype(BF16), jnp.where(head0, 0.0, do_f).astype(BF16))

            def left(kj, masked, row0):
                def run(c):
                    rows_k, _, heads = _sb_tile(_tail(qb, row0), k_ref, q0 + row0, kj, (_tail(c[0], row0), _tail(c[1], row0)),
                                                m_right_neg, masked)
                    dw2 = _dot_nt(_tail(do_b, row0), _sb_by_head(v_ref, rows_k))
                    dv = None
                    for h, (lb, wgt, _) in enumerate(heads):
                        e_s[2 * kj + h, row0:rows, :] = wgt * dw2[:, h * SB_T:(h + 1) * SB_T]
                        sig_s[2 * kj + h, row0:rows, :] = jnp.exp(lb)
                        part = _dot_tn(wgt.astype(BF16), _tail(do_h[h], row0))
                        dv = part if dv is None else dv + part
                    dv_s[rows_k, :] += dv
                    return _merge(c[0], heads[0][2], row0), _merge(c[1], heads[1][2], row0)
                return run

            zero = jnp.zeros((rows, SB_T), F32)
            _sb_sweep(left, (zero, zero), blk, rows, first, True)

            def right(kj, masked, row0):
                def run(c):
                    dq, ecar = c[0], list(c[1:])
                    rows_k = _sb_key_rows(kj)
                    valid = _sb_valid(q0 + row0, rows - row0, kj) if masked else None
                    dzs, dk = [], None
                    for h in range(2):
                        e = e_s[2 * kj + h, row0:rows, :]
                        se = _dot(e.astype(BF16), m_left)
                        ec = _tail(ecar[h], row0)
                        dz = e - sig_s[2 * kj + h, row0:rows, :] * (e + se[:, :SB_T] + ec)
                        if masked:
                            dz = jnp.where(valid, dz, 0.0)
                        dz = dz.astype(BF16)
                        part = _dot_tn(dz, _tail(q_h[h], row0))
                        dk = part if dk is None else dk + part
                        dzs.append(dz)
                        ecar[h] = _merge(ecar[h], ec + se[:, SB_T:], row0)
                    dk_s[rows_k, :] += dk
                    upd = _tail(dq, row0) + _dot(jnp.concatenate(dzs, axis=1), _sb_by_head(k_ref, rows_k))
                    return _merge(dq, upd, row0), ecar[0], ecar[1]
                return run

            dq = _sb_sweep(right, (zero, zero, zero), blk, rows, first, False)[0]
            dq_ref[rows_q, :] = (dq * scale).astype(BF16)

        q_block(0, SB_T, True)

        def big_block(i, _):
            q_block(i, tq, False)
            return 0

        lax.fori_loop(0, nb, big_block, 0)
        dk_ref[...] = dk_s[...].astype(BF16)
        dv_ref[...] = dv_s[...].astype(BF16)

    t = bsz * lp
    blk = pl.BlockSpec((lp, 128), lambda b, p: (b, p))
    shp = jax.ShapeDtypeStruct((t, 1024), BF16)
    return pl.pallas_call(
        body, grid=(bsz, SB_HEADS // 2), in_specs=_sb_specs(lp) + [blk, blk], out_specs=[blk] * 4,
        out_shape=[shp] * 4,
        scratch_shapes=[pltpu.VMEM((lp, 128), F32)] * 3 + [pltpu.VMEM((2 * nk, tq, SB_T), F32)] * 2,
        compiler_params=_cparams("parallel", "parallel"), name="sb_bwd")(qkvg, qkvg, qkvg, qkvg, o, dog)


XBC_COL0 = 3072 // 256
DT_COL0 = 4608 // 128
DT_L = 128


def _ssd_pre_fwd(proj, cw, cb, bsz, lp):
    def body(x_ref, cw_ref, cb_ref, o_ref):
        pre = _conv(x_ref[...], cw_ref[...], cb_ref[...])
        o_ref[...] = pre * _sigmoid(pre)

    return pl.pallas_call(
        body, grid=(bsz, SSD_CONV_DIM // 256),
        in_specs=[pl.BlockSpec((lp, 256), lambda b, j: (b, XBC_COL0 + j)), pl.BlockSpec((CONV_W, 256), lambda b, j: (0, j)),
                  pl.BlockSpec((1, 256), lambda b, j: (0, j))],
        out_specs=pl.BlockSpec((lp, 256), lambda b, j: (b, j)),
        out_shape=jax.ShapeDtypeStruct((bsz * lp, SSD_CONV_DIM), F32),
        compiler_params=_cparams("parallel", "parallel"), name="ssd_pre_fwd")(proj, cw, cb)


def _ssd_pre_bwd(proj, dact, cw, cb, bsz, lp):
    def body(x_ref, d_ref, cw_ref, cb_ref, dx_ref, dcw_ref, dcb_ref):
        @pl.when(pl.program_id(1) == 0)
        def _():
            dcw_ref[...] = jnp.zeros_like(dcw_ref)
            dcb_ref[...] = jnp.zeros_like(dcb_ref)
        x = x_ref[...]
        cwv = cw_ref[...]
        pre = _conv(x, cwv, cb_ref[...])
        s = _sigmoid(pre)
        dpre = d_ref[...].astype(F32) * s * (1.0 + pre * (1.0 - s))
        dcb_ref[...] += jnp.sum(dpre, axis=0, keepdims=True)
        _conv_bwd_w(dcw_ref, dpre, x)
        dx_ref[...] = _conv_bwd_x(dpre, cwv).astype(BF16)

    return pl.pallas_call(
        body, grid=(SSD_CONV_DIM // 256, bsz),
        in_specs=[pl.BlockSpec((lp, 256), lambda j, b: (b, XBC_COL0 + j)), pl.BlockSpec((lp, 256), lambda j, b: (b, j)),
                  pl.BlockSpec((CONV_W, 256), lambda j, b: (0, j)), pl.BlockSpec((1, 256), lambda j, b: (0, j))],
        out_specs=[pl.BlockSpec((lp, 256), lambda j, b: (b, j)), pl.BlockSpec((CONV_W, 256), lambda j, b: (0, j)),
                   pl.BlockSpec((1, 256), lambda j, b: (0, j))],
        out_shape=[jax.ShapeDtypeStruct((bsz * lp, SSD_CONV_DIM), BF16), jax.ShapeDtypeStruct((CONV_W, SSD_CONV_DIM), F32),
                   jax.ShapeDtypeStruct((1, SSD_CONV_DIM), F32)],
        compiler_params=_cparams("parallel", "arbitrary"), name="ssd_pre_bwd")(proj, dact, cw, cb)


def _split2(a):
    a1 = a.astype(BF16)
    return a1, (a - a1.astype(F32)).astype(BF16)


def _xdot2_nt(a, m01):
    a1, a2 = _split2(a)
    return _dot_nt(a1, m01) + _dot_nt(a2, m01)


def _xdot2_l(m01, a):
    a1, a2 = _split2(a)
    return _dot(m01, a1) + _dot(m01, a2)


class _SsdConsts:
    def __init__(self, g):
        q, gw = SSD_Q, SSD_GW
        head_of_lane = lax.shift_right_logical(_iota((DT_L, gw), 1), 6)
        self.sel = (_iota((DT_L, gw), 0) == 8 * g + head_of_lane).astype(BF16)
        r = _iota((q, gw), 0)
        c = jnp.bitwise_and(_iota((q, gw), 1), q - 1)
        self.diag = r == c
        self.diag_b = self.diag.astype(BF16)
        self.lower = c <= r
        self.upper = c >= r
        self.bd = lax.shift_right_logical(_iota((gw, gw), 0), 6) == lax.shift_right_logical(_iota((gw, gw), 1), 6)
        jj, ll = _iota((q, q), 1), _iota((q, q), 0)
        self.tri = (jj <= ll).astype(BF16)
        self.tri_t = (jj >= ll).astype(BF16)
        self.ones = jnp.ones((q, q), BF16)
        self.last = _iota((q, 1), 0) == q - 1


def _ssd_prepass(k, dt_ref, bias_ref, alog_ref, dtm_s, abc_s, lp):
    valid = _iota((lp, 1), 0) >= PAD
    sp_in = dt_ref[...] + bias_ref[...]
    dtm = jnp.where(valid, _softplus(sp_in), 0.0)
    aneg = -jnp.exp(alog_ref[...])
    dtm_s[...] = dtm
    abc_s[...] = _xdot_r(dtm * aneg, k.sel)
    return valid, sp_in, aneg


def _ssd_loop(nc, chunk, init):
    u = 2 if nc % 2 == 0 else 1

    def trip(i, carry):
        for j in range(u):
            carry = chunk(i * u + j, carry)
        return carry

    return lax.fori_loop(0, nc // u, trip, init)


def _ssd_chunk(k, rows, xs_ref, b_ref, c_ref, dtm_s, abc_s):
    bc = _xdot_l(k.tri, abc_s[rows, :])
    tt = jnp.sum(jnp.where(k.diag, bc, 0.0), axis=0, keepdims=True)
    dtbc = _xdot2(dtm_s[rows, :], k.sel)
    xs = xs_ref[rows, :]
    x = xs * dtbc
    bb = b_ref[rows, :].astype(BF16)
    cc = c_ref[rows, :].astype(BF16)
    tot = bc[SSD_Q - 1:SSD_Q, :]
    xbd = jnp.where(k.bd, jnp.concatenate([x] * 8, axis=0), 0.0).astype(BF16)
    return bc, tt, dtbc, xs, x, bb, cc, tot, xbd


def _ssd_specs(lp, order):
    ix = (lambda f: (lambda b, g: f(b, g))) if order == "bg" else (lambda f: (lambda g, b: f(b, g)))
    return [pl.BlockSpec((lp, SSD_GW), ix(lambda b, g: (b, g))),
            pl.BlockSpec((lp, SSD_N), ix(lambda b, g: (b, 1024 // SSD_N + g))),
            pl.BlockSpec((lp, SSD_N), ix(lambda b, g: (b, 1280 // SSD_N + g))),
            pl.BlockSpec((lp, DT_L), ix(lambda b, g: (b, DT_COL0))),
            pl.BlockSpec((1, DT_L), ix(lambda b, g: (0, 0))),
            pl.BlockSpec((1, DT_L), ix(lambda b, g: (0, 0))),
            pl.BlockSpec((1, DT_L), ix(lambda b, g: (0, 0)))]


def _ssd_fwd(xbc, proj, dt_bias, a_log, dskip, bsz, lp):
    nc = lp // SSD_Q

    def body(xs_ref, b_ref, c_ref, dt_ref, bias_ref, alog_ref, dsk_ref, y_ref, dtm_s, abc_s):
        k = _SsdConsts(pl.program_id(1))
        _ssd_prepass(k, dt_ref, bias_ref, alog_ref, dtm_s, abc_s, lp)
        dvec = _xdot_r(jnp.broadcast_to(dsk_ref[...], (8, DT_L)), k.sel)[0:1, :]

        def chunk(c, state):
            rows = pl.ds(pl.multiple_of(c * SSD_Q, SSD_Q), SSD_Q)
            bc, tt, _, xs, x, bb, cc, tot, xbd = _ssd_chunk(k, rows, xs_ref, b_ref, c_ref, dtm_s, abc_s)
            lm = jnp.where(k.lower, jnp.exp(jnp.minimum(bc - tt, 0.0)), 0.0)
            g_all = _dot(_dot_nt(cc, bb).astype(BF16), k.diag_b) * lm
            y = _dot(g_all.astype(BF16), xbd) + jnp.exp(bc) * _dot(cc, state.astype(BF16)) + dvec * xs
            y_ref[rows, :] = y
            return jnp.exp(tot) * state + _dot_tn(bb, (jnp.exp(tot - bc) * x).astype(BF16))

        _ssd_loop(nc, chunk, jnp.zeros((SSD_N, SSD_GW), F32))

    return pl.pallas_call(
        body, grid=(bsz, SSD_GROUPS), in_specs=_ssd_specs(lp, "bg"),
        out_specs=pl.BlockSpec((lp, SSD_GW), lambda b, g: (b, g)),
        out_shape=jax.ShapeDtypeStruct((bsz * lp, 1024), F32),
        scratch_shapes=[pltpu.VMEM((lp, DT_L), F32), pltpu.VMEM((lp, SSD_GW), F32)],
        compiler_params=_cparams("parallel", "parallel"), name="ssd_fwd")(xbc, xbc, xbc, proj, dt_bias, a_log, dskip)


def _ssd_bwd(xbc, proj, dt_bias, a_log, dskip, dy, bsz, lp):
    nc = lp // SSD_Q

    def body(xs_ref, b_ref, c_ref, dt_ref, bias_ref, alog_ref, dsk_ref, dy_ref,
             dxs_ref, db_ref, dc_ref, ddt_ref, dbias_ref, dalog_ref, ddsk_ref, dtm_s, abc_s, st_s):
        @pl.when(pl.program_id(1) == 0)
        def _():
            for ref in (dbias_ref, dalog_ref, ddsk_ref):
                ref[...] = jnp.zeros_like(ref)

        k = _SsdConsts(pl.program_id(0))
        valid, sp_in, aneg = _ssd_prepass(k, dt_ref, bias_ref, alog_ref, dtm_s, abc_s, lp)
        dvec = _xdot_r(jnp.broadcast_to(dsk_ref[...], (8, DT_L)), k.sel)[0:1, :]

        def fwd_chunk(c, state):
            rows = pl.ds(pl.multiple_of(c * SSD_Q, SSD_Q), SSD_Q)
            st_s[c] = state.astype(BF16)
            bc, _, _, _, x, bb, _, tot, _ = _ssd_chunk(k, rows, xs_ref, b_ref, c_ref, dtm_s, abc_s)
            return jnp.exp(tot) * state + _dot_tn(bb, (jnp.exp(tot - bc) * x).astype(BF16))

        _ssd_loop(nc, fwd_chunk, jnp.zeros((SSD_N, SSD_GW), F32))

        def bwd_chunk(i, carry):
            dstate, daneg, ddsk = carry
            c = nc - 1 - i
            rows = pl.ds(pl.multiple_of(c * SSD_Q, SSD_Q), SSD_Q)
            bc, tt, dtbc, xs, x, bb, cc, tot, xbd = _ssd_chunk(k, rows, xs_ref, b_ref, c_ref, dtm_s, abc_s)
            sprev = st_s[c]
            dyc = dy_ref[rows, :]
            dyb = dyc.astype(BF16)
            seg = bc - tt
            lm = jnp.where(k.lower, jnp.exp(jnp.minimum(seg, 0.0)), 0.0)
            lm_t = jnp.where(k.upper, jnp.exp(jnp.minimum(-seg, 0.0)), 0.0)
            cb_all = _dot(_dot_nt(cc, bb).astype(BF16), k.diag_b)
            cbt_all = _dot(_dot_nt(bb, cc).astype(BF16), k.diag_b)
            g_all = cb_all * lm
            dybd = jnp.where(k.bd, jnp.concatenate([dyc] * 8, axis=0), 0.0).astype(BF16)
            dg = _dot_nt(dyb, xbd)
            dx = _dot((cbt_all * lm_t).astype(BF16), dybd)
            hh = dg * g_all
            ea = jnp.exp(bc)
            yo = ea * _dot(cc, sprev)
            col_h = jnp.sum(hh, axis=0, keepdims=True)
            dcb = _dot_nt((dg * lm).astype(BF16), k.diag_b).astype(BF16)
            dcs = (ea * dyc).astype(BF16)
            dstb = dstate.astype(BF16)
            dec = jnp.exp(tot - bc)
            w = dec * x
            dw = _dot(bb, dstb)
            dc_ref[rows, :] = (_dot(dcb, bb) + _dot_nt(dcs, sprev)).astype(BF16)
            db_ref[rows, :] = (_dot_tn(dcb, cc) + _dot_nt(w.astype(BF16), dstb)).astype(BF16)
            dx = dx + dec * dw
            etot = jnp.exp(tot)
            r8 = _iota((8, SSD_GW), 0)
            st_row = jnp.sum(dstate * sprev.astype(F32), axis=0, keepdims=True) * etot
            sk_row = jnp.sum(dyc * xs, axis=0, keepdims=True)
            small = jnp.where(r8 == 0, st_row, jnp.where(r8 == 1, sk_row, 0.0))
            q = SSD_Q
            sums = _xdot2_nt(jnp.concatenate([hh + dyc * yo - jnp.where(k.diag, col_h, 0.0) - dw * w, dw * w, dx * xs, small],
                                             axis=0), k.sel)
            kk = sums[q:2 * q]
            dtot = sums[3 * q:3 * q + 1] + jnp.sum(kk, axis=0, keepdims=True)
            dacum = sums[0:q] + jnp.where(k.last, dtot, 0.0)
            da = _xdot2_l(k.tri_t, dacum)
            dtm_c = dtm_s[rows, :]
            ddtm = da * aneg + sums[2 * q:3 * q]
            vrow = (c * SSD_Q + _iota((SSD_Q, 1), 0)) >= PAD
            ddt_ref[rows, :] = jnp.where(vrow, ddtm * _sigmoid(dt_ref[rows, :] + bias_ref[...]), 0.0)
            dxs_ref[rows, :] = (dx * dtbc + dvec * dyc).astype(BF16)
            daneg = daneg + jnp.sum(da * dtm_c, axis=0, keepdims=True)
            ddsk = ddsk + sums[3 * q + 1:3 * q + 2]
            dstate = etot * dstate + _dot_tn(cc, dcs)
            return dstate, daneg, ddsk

        zrow = jnp.zeros((1, DT_L), F32)
        _, daneg, ddsk = _ssd_loop(nc, bwd_chunk, (jnp.zeros((SSD_N, SSD_GW), F32), zrow, zrow))
        dbias_ref[...] += jnp.broadcast_to(jnp.sum(ddt_ref[...], axis=0, keepdims=True), (8, DT_L))
        dalog_ref[...] += jnp.broadcast_to(daneg * aneg, (8, DT_L))
        ddsk_ref[...] += jnp.broadcast_to(ddsk, (8, DT_L))

    t = bsz * lp
    par = pl.BlockSpec((8, DT_L), lambda g, b: (g, 0))
    par_shape = jax.ShapeDtypeStruct((8 * SSD_GROUPS, DT_L), F32)
    return pl.pallas_call(
        body, grid=(SSD_GROUPS, bsz),
        in_specs=_ssd_specs(lp, "gb") + [pl.BlockSpec((lp, SSD_GW), lambda g, b: (b, g))],
        out_specs=[pl.BlockSpec((lp, SSD_GW), lambda g, b: (b, g)), pl.BlockSpec((lp, SSD_N), lambda g, b: (b, g)),
                   pl.BlockSpec((lp, SSD_N), lambda g, b: (b, g)), pl.BlockSpec((lp, DT_L), lambda g, b: (b, g)), par, par, par],
        out_shape=[jax.ShapeDtypeStruct((t, 1024), BF16), jax.ShapeDtypeStruct((t, 256), BF16),
                   jax.ShapeDtypeStruct((t, 256), BF16), jax.ShapeDtypeStruct((t, SSD_GROUPS * DT_L), F32),
                   par_shape, par_shape, par_shape],
        scratch_shapes=[pltpu.VMEM((lp, DT_L), F32), pltpu.VMEM((lp, SSD_GW), F32), pltpu.VMEM((nc, SSD_N, SSD_GW), BF16)],
        compiler_params=_cparams("parallel", "arbitrary"), name="ssd_bwd")(
            xbc, xbc, xbc, proj, dt_bias, a_log, dskip, dy)


Z_COL0 = 2048 // SSD_GW


def _gnorm_fwd(y, proj, w, name="gnorm_fwd"):
    t = y.shape[0]
    tm = _row_tile(t)

    def body(y_ref, z_ref, w_ref, o_ref):
        z = z_ref[...]
        gt = y_ref[...] * z * _sigmoid(z)
        r = lax.rsqrt(jnp.mean(gt * gt, axis=-1, keepdims=True) + EPS)
        o_ref[...] = (gt * r * w_ref[...]).astype(BF16)

    return pl.pallas_call(
        body, grid=(t // tm, SSD_GROUPS),
        in_specs=[pl.BlockSpec((tm, SSD_GW), lambda i, g: (i, g)), pl.BlockSpec((tm, SSD_GW), lambda i, g: (i, Z_COL0 + g)),
                  pl.BlockSpec((1, SSD_GW), lambda i, g: (0, g))],
        out_specs=pl.BlockSpec((tm, SSD_GW), lambda i, g: (i, g)),
        out_shape=jax.ShapeDtypeStruct((t, 1024), BF16),
        compiler_params=_cparams("parallel", "parallel"), name=name)(y, proj, w)


def _gnorm_bwd(y, proj, w, dmixed):
    t = y.shape[0]
    tm = _row_tile(t)

    def body(y_ref, z_ref, w_ref, d_ref, dy_ref, dz_ref, dw_ref):
        @pl.when(pl.program_id(1) == 0)
        def _():
            dw_ref[...] = jnp.zeros_like(dw_ref)
        z, yv, d = z_ref[...], y_ref[...], d_ref[...]
        s = _sigmoid(z)
        sz = z * s
        gt = yv * sz
        r = lax.rsqrt(jnp.mean(gt * gt, axis=-1, keepdims=True) + EPS)
        gh = gt * r
        dgn = d * w_ref[...]
        dgt = r * (dgn - gh * jnp.mean(dgn * gh, axis=-1, keepdims=True))
        dw_ref[...] += jnp.sum(d * gh, axis=0, keepdims=True)
        dy_ref[...] = dgt * sz
        dz_ref[...] = (dgt * yv * s * (1.0 + z * (1.0 - s))).astype(BF16)

    blk = pl.BlockSpec((tm, SSD_GW), lambda g, i: (i, g))
    return pl.pallas_call(
        body, grid=(SSD_GROUPS, t // tm),
        in_specs=[blk, pl.BlockSpec((tm, SSD_GW), lambda g, i: (i, Z_COL0 + g)), pl.BlockSpec((1, SSD_GW), lambda g, i: (0, g)),
                  pl.BlockSpec((tm, SSD_GW), lambda g, i: (i, 1024 // SSD_GW + g))],
        out_specs=[blk, blk, pl.BlockSpec((1, SSD_GW), lambda g, i: (0, g))],
        out_shape=[jax.ShapeDtypeStruct((t, 1024), F32), jax.ShapeDtypeStruct((t, 1024), BF16),
                   jax.ShapeDtypeStruct((1, 1024), F32)],
        compiler_params=_cparams("parallel", "arbitrary"), name="gnorm_bwd")(y, proj, w, dmixed)


def _meta_grad(dh0, bsz, lp):
    def body(d_ref, o_ref):
        @pl.when(pl.program_id(0) == 0)
        def _():
            o_ref[...] = jnp.zeros_like(o_ref)
        o_ref[...] += d_ref[...]

    return pl.pallas_call(
        body, grid=(bsz,),
        in_specs=[pl.BlockSpec((N_META, D_MODEL), lambda b: (b * (lp // N_META) + PAD // N_META, 0))],
        out_specs=pl.BlockSpec((N_META, D_MODEL), lambda b: (0, 0)),
        out_shape=jax.ShapeDtypeStruct((N_META, D_MODEL), F32),
        compiler_params=_cparams("arbitrary"), name="meta_grad")(dh0)


OTHER_CHIPS = ((1, 0), (0, 1), (1, 1))
ANY_SPEC = pl.BlockSpec(memory_space=pl.ANY)


class _Mesh:
    def __init__(self):
        self.x, self.y, self.c = lax.axis_index("x"), lax.axis_index("y"), lax.axis_index("c")

    def dev(self, fx, fy, fc):
        return (1 - self.x if fx else self.x, 1 - self.y if fy else self.y, 1 - self.c if fc else self.c)

    def slot(self, fx, fy, fc):
        px, py, pc = self.dev(fx, fy, fc)
        return 4 * px + 2 * py + pc

    def chip(self, fx, fy):
        px, py, _ = self.dev(fx, fy, 0)
        return 2 * px + py


def _remote(src, dst, send_sems, recv_sems, idx, dev):
    return pltpu.make_async_remote_copy(src_ref=src, dst_ref=dst, send_sem=send_sems.at[idx], recv_sem=recv_sems.at[idx],
                                        device_id=dev, device_id_type=pl.DeviceIdType.MESH)


def _gather_all(arrs, name):
    n = len(arrs)

    def body(*refs):
        ins, outs = refs[:n], refs[n:2 * n]
        send_sems, recv_sems, local_sems = refs[2 * n:]
        me = _Mesh()
        local = [pltpu.make_async_copy(ins[a], outs[a].at[me.slot(0, 0, 0)], local_sems.at[a]) for a in range(n)]
        first, passed, arrivals = [], [], []
        for a in range(n):
            mine = outs[a].at[me.slot(0, 0, 0)]
            first.append(_remote(ins[a], mine, send_sems, recv_sems, (a, 0), me.dev(0, 0, 1)))
            for j, (fx, fy) in enumerate(OTHER_CHIPS):
                first.append(_remote(ins[a], mine, send_sems, recv_sems, (a, 1 + j), me.dev(fx, fy, 0)))
        for cp in local + first:
            cp.start()
        for a in range(n):
            for j, (fx, fy) in enumerate(OTHER_CHIPS):
                blk = outs[a].at[me.slot(fx, fy, 0)]
                _remote(ins[a], blk, send_sems, recv_sems, (a, 1 + j), me.dev(fx, fy, 0)).wait_recv()
                cp = _remote(blk, blk, send_sems, recv_sems, (a, 4 + j), me.dev(0, 0, 1))
                cp.start()
                passed.append(cp)
        for a in range(n):
            _remote(ins[a], outs[a].at[me.slot(0, 0, 1)], send_sems, recv_sems, (a, 0), me.dev(0, 0, 1)).wait_recv()
            for j, (fx, fy) in enumerate(OTHER_CHIPS):
                _remote(ins[a], outs[a].at[me.slot(fx, fy, 1)], send_sems, recv_sems, (a, 4 + j), me.dev(0, 0, 1)).wait_recv()
        for cp in first + passed:
            cp.wait_send()
        for cp in local:
            cp.wait()

    return pl.pallas_call(
        body, in_specs=[ANY_SPEC] * n, out_specs=[ANY_SPEC] * n,
        out_shape=[jax.ShapeDtypeStruct((N_DEV,) + a.shape, a.dtype) for a in arrs],
        scratch_shapes=[pltpu.SemaphoreType.DMA((n, 7)), pltpu.SemaphoreType.DMA((n, 7)), pltpu.SemaphoreType.DMA((n,))],
        compiler_params=pltpu.CompilerParams(has_side_effects=True), name=name)(*arrs)


def _swap_in_chip(arrs, name):
    n = len(arrs)

    def body(*refs):
        ins, kept, got = refs[:n], refs[n:2 * n], refs[2 * n:3 * n]
        send_sems, recv_sems, local_sems = refs[3 * n:]
        me = _Mesh()
        local, sends = [], []
        for a in range(n):
            for j in range(4):
                local.append(pltpu.make_async_copy(ins[a].at[2 * j + me.c], kept[a].at[j], local_sems.at[a, j]))
                sends.append(_remote(ins[a].at[2 * j + (1 - me.c)], got[a].at[j], send_sems, recv_sems, (a, j), me.dev(0, 0, 1)))
        for cp in local + sends:
            cp.start()
        for cp in sends:
            cp.wait()
        for cp in local:
            cp.wait()

    shapes = [jax.ShapeDtypeStruct((4,) + a.shape[1:], a.dtype) for a in arrs]
    outs = pl.pallas_call(
        body, in_specs=[ANY_SPEC] * n, out_specs=[ANY_SPEC] * (2 * n), out_shape=shapes + shapes,
        scratch_shapes=[pltpu.SemaphoreType.DMA((n, 4)), pltpu.SemaphoreType.DMA((n, 4)), pltpu.SemaphoreType.DMA((n, 4))],
        compiler_params=pltpu.CompilerParams(has_side_effects=True), name=name)(*arrs)
    return outs[:n], outs[n:]


def _add_pair(a, b, name):
    _, r, c = a.shape
    tr = _adam_rows(r)

    def body(a_ref, b_ref, o_ref):
        o_ref[...] = (a_ref[...].astype(F32) + b_ref[...].astype(F32)).astype(o_ref.dtype)

    blk = pl.BlockSpec((4, tr, c), lambda i: (0, i, 0))
    return pl.pallas_call(body, grid=(r // tr,), in_specs=[blk, blk], out_specs=blk,
                          out_shape=jax.ShapeDtypeStruct(a.shape, a.dtype), compiler_params=_cparams("parallel"), name=name)(a, b)


def _swap_chips(sums, gather, name):
    ns, ng = len(sums), len(gather)
    n = ns + ng

    def body(*refs):
        ins, outs = refs[:n], refs[n:2 * n]
        send_sems, recv_sems, local_sems = refs[2 * n:]
        me = _Mesh()
        local, sends, recvs = [], [], []
        for a in range(ns):
            here = me.chip(0, 0)
            local.append(pltpu.make_async_copy(ins[a].at[here], outs[a].at[here], local_sems.at[a]))
            for j, (fx, fy) in enumerate(OTHER_CHIPS):
                there = me.chip(fx, fy)
                sends.append(_remote(ins[a].at[there], outs[a].at[here], send_sems, recv_sems, (a, j), me.dev(fx, fy, 0)))
                recvs.append(_remote(ins[a].at[there], outs[a].at[there], send_sems, recv_sems, (a, j), me.dev(fx, fy, 0)))
        for a in range(ns, n):
            local.append(pltpu.make_async_copy(ins[a], outs[a].at[me.slot(0, 0, 0)], local_sems.at[a]))
            for k in range(1, N_DEV):
                flips = (k >> 2 & 1, k >> 1 & 1, k & 1)
                sends.append(_remote(ins[a], outs[a].at[me.slot(0, 0, 0)], send_sems, recv_sems, (a, k - 1), me.dev(*flips)))
                recvs.append(_remote(ins[a], outs[a].at[me.slot(*flips)], send_sems, recv_sems, (a, k - 1), me.dev(*flips)))
        for cp in local + sends:
            cp.start()
        for cp in recvs:
            cp.wait_recv()
        for cp in sends:
            cp.wait_send()
        for cp in local:
            cp.wait()

    out_shape = [jax.ShapeDtypeStruct(a.shape, a.dtype) for a in sums]
    out_shape += [jax.ShapeDtypeStruct((N_DEV,) + a.shape, a.dtype) for a in gather]
    return pl.pallas_call(
        body, in_specs=[ANY_SPEC] * n, out_specs=[ANY_SPEC] * n, out_shape=out_shape,
        scratch_shapes=[pltpu.SemaphoreType.DMA((n, N_DEV - 1)), pltpu.SemaphoreType.DMA((n, N_DEV - 1)),
                        pltpu.SemaphoreType.DMA((n,))],
        compiler_params=pltpu.CompilerParams(has_side_effects=True), name=name)(*sums, *gather)


def _adam_rows(r):
    for cand in (128, 64, 32, 16, 8):
        if r % cand == 0 and r > cand:
            return cand
    return r


def _adamw(parts, w, m, v, name):
    r, c = w.shape
    tr = _adam_rows(r)
    n_parts = parts.shape[0]

    def body(p_ref, w_ref, m_ref, v_ref, g_ref, d_ref, nm_ref, nv_ref):
        g = p_ref[0].astype(F32)
        for k in range(1, n_parts):
            g = g + p_ref[k].astype(F32)
        m_new = ADAM_B1 * m_ref[...] + (1.0 - ADAM_B1) * g
        v_new = ADAM_B2 * v_ref[...] + (1.0 - ADAM_B2) * (g * g)
        m_hat = m_new / (1.0 - ADAM_B1 ** ADAM_STEP)
        v_hat = v_new / (1.0 - ADAM_B2 ** ADAM_STEP)
        g_ref[...] = g
        d_ref[...] = -ADAM_LR * (m_hat / (jnp.sqrt(v_hat) + ADAM_EPS) + ADAM_WD * w_ref[...])
        nm_ref[...] = m_new
        nv_ref[...] = v_new

    blk = pl.BlockSpec((tr, c), lambda i: (i, 0))
    shp = jax.ShapeDtypeStruct((r, c), F32)
    return pl.pallas_call(
        body, grid=(r // tr,), in_specs=[pl.BlockSpec((n_parts, tr, c), lambda i: (0, i, 0)), blk, blk, blk],
        out_specs=[blk] * 4, out_shape=[shp] * 4, compiler_params=_cparams("parallel"), name=name)(parts, w, m, v)


def _rows128(a):
    return a.reshape(-1, 128)


def _pad_rows(a, rows):
    return jnp.pad(a, ((0, rows - a.shape[0]), (0, 0)))


def _lane16(a):
    return jnp.pad(a.reshape(1, -1), ((0, 0), (0, 128 - a.size)))


SHARD_PACK_ROWS = 544
REPL_PACK_ROWS = 72


def _pack_shard(meta, lru_conv_w, odd_norm, ssd_conv_w, lru_w_a, lru_w_x):
    parts = [meta.reshape(16, 128), lru_conv_w.reshape(4, 128), odd_norm.reshape(1, 128), _rows128(ssd_conv_w.reshape(4, 192)),
             _rows128(lru_w_a.reshape(4, 32, 256)), _rows128(lru_w_x.reshape(4, 32, 256))]
    return _pad_rows(jnp.concatenate(parts, axis=0), SHARD_PACK_ROWS)


def _unpack_shard(p):
    return (p[0:16], p[16:20].reshape(1, 4, 128), p[20:21], p[21:27].reshape(1, 4, 192),
            p[27:283].reshape(1, 4, 32, 256), p[283:539].reshape(1, 4, 32, 256))


def _pack_repl(even_norm, lru_conv_b, lru_b_a, lru_b_x, lru_lambda, ssd_norm, final_norm, ssd_conv_b, dt_bias, a_log, ssd_d):
    parts = [_rows128(v) for v in (even_norm, lru_conv_b, lru_b_a, lru_b_x, lru_lambda, ssd_norm, final_norm, ssd_conv_b)]
    parts += [_lane16(dt_bias), _lane16(a_log), _lane16(ssd_d)]
    return _pad_rows(jnp.concatenate(parts, axis=0), REPL_PACK_ROWS)


def _unpack_repl(p):
    vec = lambda i: p[8 * i:8 * i + 8].reshape(1, 1024)
    return (vec(0), vec(1), vec(2), vec(3), vec(4), vec(5), p[48:56].reshape(1024), p[56:68].reshape(1, 1536),
            p[68:69, :16], p[69:70, :16], p[70:71, :16])


def _local_step(x, tgt, meta, even_norm, w_in_e, lru_conv_w, lru_conv_b, lru_w_a, lru_b_a, lru_w_x, lru_b_x, lru_lambda,
                ssd_conv_w, ssd_conv_b, dt_bias, a_log, ssd_d, ssd_norm, w_out_e, odd_norm, w_in_o, w_out_o, final_norm):
    bsz, seq, d = x.shape
    lp = PAD + N_META + seq
    t = bsz * lp
    h0 = jnp.concatenate([jnp.zeros((bsz, PAD, d), F32), jnp.broadcast_to(meta[None], (bsz, N_META, d)), x], axis=1).reshape(t, d)

    u0 = _norm_fwd(h0, even_norm, "norm0_fwd")
    proj = _mm_nn(u0, w_in_e, F32, "even_in")
    lru = (lru_conv_w, lru_conv_b, lru_w_a, lru_b_a, lru_w_x, lru_b_x, lru_lambda)
    ya = _lru_fwd(proj, *lru, bsz, lp)
    xbc = _ssd_pre_fwd(proj, ssd_conv_w, ssd_conv_b, bsz, lp)
    y = _ssd_fwd(xbc, proj, dt_bias, a_log, ssd_d, bsz, lp)
    yb = _gnorm_fwd(y, proj, ssd_norm)
    mixed = jnp.concatenate([ya, yb], axis=1)
    h1 = _mm_nn(mixed, w_out_e, F32, "even_out", res=h0)
    u2 = _norm_fwd(h1, odd_norm, "norm1_fwd")
    qkvg = _mm_nn(u2, w_in_o, F32, "odd_in")
    o, og = _sb_fwd(qkvg, bsz, lp)
    h2 = _mm_nn(og, w_out_o, F32, "odd_out", res=h1)
    loss, dh2, d_final_norm = _final_loss(h2, final_norm, tgt, lp)

    dog = _mm_nn(dh2, w_out_o.T, F32, "odd_out_dx")
    d_w_out_o = _mm_tn(og, dh2, "odd_out_dw")
    dqkvg = jnp.concatenate(_sb_bwd(qkvg, o, dog, bsz, lp), axis=1)
    du2 = _mm_nn(dqkvg, w_in_o.T, F32, "odd_in_dx")
    d_w_in_o = _mm_tn(u2, dqkvg, "odd_in_dw")
    dh1, d_odd_norm = _norm_bwd(h1, odd_norm, du2, dh2, "norm1_bwd")

    dmixed = _mm_nn(dh1, w_out_e.T, F32, "even_out_dx")
    d_w_out_e = _mm_tn(mixed, dh1, "even_out_dw")
    dlx, dgate, d_lru_conv_w, d_lru_conv_b, d_w_a, d_b_a, d_w_x, d_b_x, d_lambda = _lru_bwd(proj, dmixed, *lru, bsz, lp)
    dy, dz, d_ssd_norm = _gnorm_bwd(y, proj, ssd_norm, dmixed)
    dxs, dbm, dcm, ddt, d_dt_bias, d_a_log, d_ssd_d = _ssd_bwd(xbc, proj, dt_bias, a_log, ssd_d, dy, bsz, lp)
    dxbc, d_ssd_conv_w, d_ssd_conv_b = _ssd_pre_bwd(proj, jnp.concatenate([dxs, dbm, dcm], axis=1), ssd_conv_w, ssd_conv_b, bsz, lp)
    ddt = (ddt[:, :DT_L] + ddt[:, DT_L:]).astype(BF16)
    dproj = jnp.concatenate([dlx, dgate, dz, dxbc, ddt, jnp.zeros((t, EVEN_NP - 4608 - DT_L), BF16)], axis=1)
    du0 = _mm_nn(dproj, w_in_e.T, F32, "even_in_dx")
    d_w_in_e = _mm_tn(u0, dproj, "even_in_dw")
    dh0, d_even_norm = _norm_bwd(h0, even_norm, du0, dh1, "norm0_bwd")
    grad_x = dh0.reshape(bsz, lp, d)[:, PAD + N_META:]
    d_meta = _meta_grad(dh0, bsz, lp)
    heads = lambda p: (p[0:1] + p[8:9])[:, :SSD_HEADS]
    grads = dict(meta=d_meta, even_norm=d_even_norm, even_w_in=d_w_in_e[:, :EVEN_IN], lru_conv_w=d_lru_conv_w,
                 lru_conv_b=d_lru_conv_b, lru_w_a=d_w_a, lru_b_a=d_b_a, lru_w_x=d_w_x, lru_b_x=d_b_x, lru_lambda=d_lambda,
                 ssd_conv_w=d_ssd_conv_w, ssd_conv_b=d_ssd_conv_b, ssd_dt_bias=heads(d_dt_bias), ssd_a_log=heads(d_a_log),
                 ssd_d=heads(d_ssd_d), ssd_norm=d_ssd_norm, even_w_out=d_w_out_e, odd_norm=d_odd_norm, odd_w_in=d_w_in_o,
                 odd_w_out=d_w_out_o, final_norm=d_final_norm)
    return loss[0, 0], grad_x, grads


WEIGHTS = ['meta', 'even_norm', 'even_w_in', 'lru_conv_w', 'lru_conv_b', 'lru_w_a', 'lru_b_a', 'lru_w_x', 'lru_b_x', 'lru_lambda',
           'ssd_conv_w', 'ssd_conv_b', 'ssd_dt_bias', 'ssd_a_log', 'ssd_d', 'ssd_norm', 'even_w_out', 'odd_norm', 'odd_w_in',
           'odd_w_out', 'final_norm']


def _blocks_of(a, axis):
    shp = a.shape
    a = a.reshape(shp[:axis] + (N_DEV, shp[axis] // N_DEV) + shp[axis + 1:])
    return jnp.moveaxis(a, axis, 0)


def _unblock(a, axis):
    a = jnp.moveaxis(a, 0, axis)
    shp = a.shape
    return a.reshape(shp[:axis] + (shp[axis] * shp[axis + 1],) + shp[axis + 2:])


def kernel(x, meta, even_norm, even_w_in, lru_conv_w, lru_conv_b, lru_w_a, lru_b_a, lru_w_x, lru_b_x, lru_lambda, ssd_conv_w, ssd_conv_b, ssd_dt_bias, ssd_a_log, ssd_d, ssd_norm, even_w_out, odd_norm, odd_w_in, odd_w_out, final_norm, loss_target, m_meta, m_even_norm, m_even_w_in, m_lru_conv_w, m_lru_conv_b, m_lru_w_a, m_lru_b_a, m_lru_w_x, m_lru_b_x, m_lru_lambda, m_ssd_conv_w, m_ssd_conv_b, m_ssd_dt_bias, m_ssd_a_log, m_ssd_d, m_ssd_norm, m_even_w_out, m_odd_norm, m_odd_w_in, m_odd_w_out, m_final_norm, v_meta, v_even_norm, v_even_w_in, v_lru_conv_w, v_lru_conv_b, v_lru_w_a, v_lru_b_a, v_lru_w_x, v_lru_b_x, v_lru_lambda, v_ssd_conv_w, v_ssd_conv_b, v_ssd_dt_bias, v_ssd_a_log, v_ssd_d, v_ssd_norm, v_even_w_out, v_odd_norm, v_odd_w_in, v_odd_w_out, v_final_norm):
    w = dict(meta=meta, even_norm=even_norm, even_w_in=even_w_in, lru_conv_w=lru_conv_w, lru_conv_b=lru_conv_b, lru_w_a=lru_w_a,
             lru_b_a=lru_b_a, lru_w_x=lru_w_x, lru_b_x=lru_b_x, lru_lambda=lru_lambda, ssd_conv_w=ssd_conv_w,
             ssd_conv_b=ssd_conv_b, ssd_dt_bias=ssd_dt_bias, ssd_a_log=ssd_a_log, ssd_d=ssd_d, ssd_norm=ssd_norm,
             even_w_out=even_w_out, odd_norm=odd_norm, odd_w_in=odd_w_in, odd_w_out=odd_w_out, final_norm=final_norm)
    m = dict(meta=m_meta, even_norm=m_even_norm, even_w_in=m_even_w_in, lru_conv_w=m_lru_conv_w, lru_conv_b=m_lru_conv_b,
             lru_w_a=m_lru_w_a, lru_b_a=m_lru_b_a, lru_w_x=m_lru_w_x, lru_b_x=m_lru_b_x, lru_lambda=m_lru_lambda,
             ssd_conv_w=m_ssd_conv_w, ssd_conv_b=m_ssd_conv_b, ssd_dt_bias=m_ssd_dt_bias, ssd_a_log=m_ssd_a_log, ssd_d=m_ssd_d,
             ssd_norm=m_ssd_norm, even_w_out=m_even_w_out, odd_norm=m_odd_norm, odd_w_in=m_odd_w_in, odd_w_out=m_odd_w_out,
             final_norm=m_final_norm)
    v = dict(meta=v_meta, even_norm=v_even_norm, even_w_in=v_even_w_in, lru_conv_w=v_lru_conv_w, lru_conv_b=v_lru_conv_b,
             lru_w_a=v_lru_w_a, lru_b_a=v_lru_b_a, lru_w_x=v_lru_w_x, lru_b_x=v_lru_b_x, lru_lambda=v_lru_lambda,
             ssd_conv_w=v_ssd_conv_w, ssd_conv_b=v_ssd_conv_b, ssd_dt_bias=v_ssd_dt_bias, ssd_a_log=v_ssd_a_log, ssd_d=v_ssd_d,
             ssd_norm=v_ssd_norm, even_w_out=v_even_w_out, odd_norm=v_odd_norm, odd_w_in=v_odd_w_in, odd_w_out=v_odd_w_out,
             final_norm=v_final_norm)
    shard_names = ('meta', 'lru_conv_w', 'odd_norm', 'ssd_conv_w', 'lru_w_a', 'lru_w_x')
    repl_names = ('even_norm', 'lru_conv_b', 'lru_b_a', 'lru_b_x', 'lru_lambda', 'ssd_norm', 'final_norm', 'ssd_conv_b',
                  'ssd_dt_bias', 'ssd_a_log', 'ssd_d')
    big_names = ('even_w_in', 'even_w_out', 'odd_w_in', 'odd_w_out')

    gates = jnp.concatenate([lru_w_a.reshape(128, 256), lru_w_x.reshape(128, 256)], axis=0).astype(BF16)
    small = _pack_shard(*[w[k] for k in shard_names])
    g_in_e, g_out_e, g_in_o, g_out_o, g_gates, g_small = _gather_all(
        [even_w_in[0].astype(BF16), even_w_out[0].astype(BF16), odd_w_in[0].astype(BF16), odd_w_out[0].astype(BF16), gates, small],
        "gather_weights")
    w_in_e = jnp.pad(_unblock(g_in_e, 1), ((0, 0), (0, EVEN_NP - EVEN_IN)))
    w_out_e = g_out_e.reshape(2048, 1024)
    w_in_o = _unblock(g_in_o, 1)
    w_out_o = g_out_o.reshape(1024, 1024)
    gates_full = jnp.moveaxis(g_gates.reshape(N_DEV, 2, 4, 32, 256), 0, 2).reshape(2, 4, 256, 256)
    f_meta = _unblock(g_small[:, 0:16], 1)
    f_lru_conv_w = _unblock(g_small[:, 16:20], 1)
    f_odd_norm = _unblock(g_small[:, 20:21], 1)
    f_ssd_conv_w = _unblock(g_small[:, 21:27].reshape(N_DEV, 4, 192), 1)

    loss, grad_x, g = _local_step(
        x, loss_target, f_meta, even_norm, w_in_e, f_lru_conv_w, lru_conv_b, gates_full[0], lru_b_a, gates_full[1], lru_b_x,
        lru_lambda, f_ssd_conv_w, ssd_conv_b, _lane16(ssd_dt_bias), _lane16(ssd_a_log), _lane16(ssd_d), ssd_norm, w_out_e,
        f_odd_norm, w_in_o, w_out_o, final_norm.reshape(1, -1))
    loss = lax.psum(loss, ("x", "y", "c"))

    s_small = jnp.stack([_pack_shard(g['meta'][:, 128 * p:128 * (p + 1)], g['lru_conv_w'][:, 128 * p:128 * (p + 1)],
                                     g['odd_norm'][:, 128 * p:128 * (p + 1)], g['ssd_conv_w'][:, 192 * p:192 * (p + 1)],
                                     g['lru_w_a'][:, 32 * p:32 * (p + 1)], g['lru_w_x'][:, 32 * p:32 * (p + 1)])
                         for p in range(N_DEV)])
    r_pack = _pack_repl(*[g[k] for k in repl_names])
    kept, got = _swap_in_chip(
        [_blocks_of(g['even_w_in'], 1).astype(BF16), g['even_w_out'].reshape(N_DEV, 256, 1024).astype(BF16),
         _blocks_of(g['odd_w_in'], 1).astype(BF16), g['odd_w_out'].reshape(N_DEV, 128, 1024).astype(BF16), s_small],
        "reduce_in_chip")
    chip_sums = [_add_pair(a, b, "chip_sum_%d" % i) for i, (a, b) in enumerate(zip(kept, got))]
    p_in_e, p_out_e, p_in_o, p_out_o, p_small, p_repl = _swap_chips(chip_sums, [r_pack], "reduce_across_chips")

    res = {}
    for k, parts in zip(big_names, (p_in_e, p_out_e, p_in_o, p_out_o)):
        outs = _adamw(parts, w[k][0], m[k][0], v[k][0], "adamw_" + k)
        res[k] = [o[None] for o in outs]
    outs = _adamw(p_small, small, _pack_shard(*[m[k] for k in shard_names]), _pack_shard(*[v[k] for k in shard_names]), "adamw_sharded")
    unpacked = [_unpack_shard(o) for o in outs]
    for i, k in enumerate(shard_names):
        res[k] = [u[i] for u in unpacked]
    outs = _adamw(p_repl, _pack_repl(*[w[k] for k in repl_names]), _pack_repl(*[m[k] for k in repl_names]),
                  _pack_repl(*[v[k] for k in repl_names]), "adamw_replicated")
    unpacked = [_unpack_repl(o) for o in outs]
    for i, k in enumerate(repl_names):
        res[k] = [u[i] for u in unpacked]
    return (loss, grad_x, *[res[k][0] for k in WEIGHTS], *[res[k][1] for k in WEIGHTS], *[res[k][2] for k in WEIGHTS],
            *[res[k][3] for k in WEIGHTS])
```

```python
import functools

import jax
import jax.numpy as jnp
from jax import lax
from jax.experimental import pallas as pl
from jax.experimental.pallas import tpu as pltpu

F32 = jnp.float32
BF16 = jnp.bfloat16

D_MODEL = 1024
N_META = 16
PAD = 112
EPS = 1e-6
CONV_W = 4
LRU_BLOCKS = 4
LRU_BLOCK = 256
RG_LRU_C = 8.0
SSD_HEADS = 16
SSD_P = 64
SSD_N = 128
SSD_Q = 64
SSD_GROUPS = 2
SSD_GW = 512
SSD_CONV_DIM = 1536
SB_HEADS = 16
SB_D = 64
SB_T = 128
EVEN_IN = 4624
EVEN_NP = 4864
DT_W = 256
ODD_IN = 4096
N_DEV = 8

ADAM_LR = 0.001
ADAM_B1 = 0.9
ADAM_B2 = 0.999
ADAM_EPS = 1e-08
ADAM_WD = 0.01
ADAM_STEP = 10

VMEM_LIMIT = 56 * 1024 * 1024


def _cparams(*sem):
    return pltpu.CompilerParams(dimension_semantics=sem, vmem_limit_bytes=VMEM_LIMIT)


def _row_tile(t):
    for c in (512, 256, 128):
        if t % c == 0:
            return c
    raise ValueError(t)


def _col_tile(n):
    for c in (512, 256, 128):
        if n % c == 0:
            return c
    raise ValueError(n)


def _dot(a, b):
    return jnp.dot(a, b, preferred_element_type=F32)


def _dot_nt(a, b):
    return lax.dot_general(a, b, (((1,), (1,)), ((), ())), preferred_element_type=F32)


def _dot_tn(a, b):
    return lax.dot_general(a, b, (((0,), (0,)), ((), ())), preferred_element_type=F32)


def _split3(a):
    a1 = a.astype(BF16)
    r1 = a - a1.astype(F32)
    a2 = r1.astype(BF16)
    a3 = (r1 - a2.astype(F32)).astype(BF16)
    return a1, a2, a3


def _xdot_r(a, m01):
    a1, a2, a3 = _split3(a)
    return _dot(a1, m01) + _dot(a2, m01) + _dot(a3, m01)


def _xdot_l(m01, a):
    a1, a2, a3 = _split3(a)
    return _dot(m01, a1) + _dot(m01, a2) + _dot(m01, a3)


def _xdot_tn_l(m01, a):
    a1, a2, a3 = _split3(a)
    return _dot_tn(m01, a1) + _dot_tn(m01, a2) + _dot_tn(m01, a3)


def _sigmoid(x):
    return 0.5 * jnp.tanh(0.5 * x) + 0.5


def _softplus(x):
    return jnp.maximum(x, 0.0) + jnp.log(1.0 + jnp.exp(-jnp.abs(x)))


def _iota(shape, dim):
    return lax.broadcasted_iota(jnp.int32, shape, dim)


def _block_scan(a_ref, b_ref, o_ref, reverse=False):
    n, w = a_ref.shape
    nb = n // 8
    unroll = 4 if nb % 4 == 0 else 1
    row = _iota((8, w), 0)

    def block(blk, carry):
        rows = pl.ds(pl.multiple_of(blk * 8, 8), 8)
        a, b = a_ref[rows, :], b_ref[rows, :]
        for k in (1, 2, 4):
            keep = (row < 8 - k) if reverse else (row >= k)
            shift = 8 - k if reverse else k
            a_sh = jnp.where(keep, pltpu.roll(a, shift, 0), 1.0)
            b_sh = jnp.where(keep, pltpu.roll(b, shift, 0), 0.0)
            b = a * b_sh + b
            a = a * a_sh
        o = a * carry + b
        o_ref[rows, :] = o
        return o[0:1, :] if reverse else o[7:8, :]

    def trip(i, carry):
        for u in range(unroll):
            j = i * unroll + u
            carry = block(nb - 1 - j if reverse else j, carry)
        return carry

    lax.fori_loop(0, nb // unroll, trip, jnp.zeros((1, w), F32))


def _chunks(n, c):
    return [(s, min(c, n - s)) for s in range(0, n, c)]


def _mm_nn(a, b, out_dtype, name, res=None):
    m, k = a.shape
    _, n = b.shape
    tm = _row_tile(m)

    def body(*refs):
        if res is None:
            a_ref, b_ref, o_ref = refs
        else:
            a_ref, b_ref, r_ref, o_ref = refs
        av = a_ref[...].astype(BF16)
        for s, w in _chunks(n, 512):
            acc = _dot(av, b_ref[:, s:s + w])
            if res is not None:
                acc = acc + r_ref[:, s:s + w]
            o_ref[:, s:s + w] = acc.astype(out_dtype)

    in_specs = [pl.BlockSpec((tm, k), lambda i: (i, 0)), pl.BlockSpec((k, n), lambda i: (0, 0))]
    args = [a, b]
    if res is not None:
        in_specs.append(pl.BlockSpec((tm, n), lambda i: (i, 0)))
        args.append(res)
    return pl.pallas_call(
        body, grid=(m // tm,), in_specs=in_specs,
        out_specs=pl.BlockSpec((tm, n), lambda i: (i, 0)),
        out_shape=jax.ShapeDtypeStruct((m, n), out_dtype),
        compiler_params=_cparams("parallel"), name=name)(*args)


def _mm_nt(a, b, out_dtype, name):
    m, k = a.shape
    n, _ = b.shape
    tm, tn = _row_tile(m), _col_tile(n)

    def body(a_ref, b_ref, o_ref):
        o_ref[...] = _dot_nt(a_ref[...].astype(BF16), b_ref[...].astype(BF16)).astype(out_dtype)

    return pl.pallas_call(
        body, grid=(m // tm, n // tn),
        in_specs=[pl.BlockSpec((tm, k), lambda i, j: (i, 0)), pl.BlockSpec((tn, k), lambda i, j: (j, 0))],
        out_specs=pl.BlockSpec((tm, tn), lambda i, j: (i, j)),
        out_shape=jax.ShapeDtypeStruct((m, n), out_dtype),
        compiler_params=_cparams("parallel", "parallel"), name=name)(a, b)


def _mm_tn(a, b, name):
    t, m = a.shape
    _, n = b.shape
    tk = _row_tile(t)
    halves = 2 if (m * n * 4 > 8 * 1024 * 1024 and n % 256 == 0) else 1
    tn = n // halves

    def body(a_ref, b_ref, o_ref):
        @pl.when(pl.program_id(1) == 0)
        def _():
            o_ref[...] = jnp.zeros_like(o_ref)
        at = a_ref[...].astype(BF16).T
        for s, w in _chunks(tn, 512):
            o_ref[:, s:s + w] += _dot(at, b_ref[:, s:s + w].astype(BF16))

    return pl.pallas_call(
        body, grid=(halves, t // tk),
        in_specs=[pl.BlockSpec((tk, m), lambda j, s: (s, 0)), pl.BlockSpec((tk, tn), lambda j, s: (s, j))],
        out_specs=pl.BlockSpec((m, tn), lambda j, s: (0, j)),
        out_shape=jax.ShapeDtypeStruct((m, n), F32),
        compiler_params=_cparams("parallel", "arbitrary"), name=name)(a, b)


def _norm_fwd(h, w, name):
    t, d = h.shape
    tm = _row_tile(t)

    def body(h_ref, w_ref, u_ref):
        x = h_ref[...]
        r = lax.rsqrt(jnp.mean(x * x, axis=-1, keepdims=True) + EPS)
        u_ref[...] = (x * r * w_ref[...]).astype(BF16)

    return pl.pallas_call(
        body, grid=(t // tm,),
        in_specs=[pl.BlockSpec((tm, d), lambda i: (i, 0)), pl.BlockSpec((1, d), lambda i: (0, 0))],
        out_specs=pl.BlockSpec((tm, d), lambda i: (i, 0)),
        out_shape=jax.ShapeDtypeStruct((t, d), BF16),
        compiler_params=_cparams("parallel"), name=name)(h, w)


def _norm_bwd(h, w, du, dres, name):
    t, d = h.shape
    tm = _row_tile(t)

    def body(h_ref, w_ref, du_ref, dr_ref, dh_ref, dw_ref):
        @pl.when(pl.program_id(0) == 0)
        def _():
            dw_ref[...] = jnp.zeros_like(dw_ref)
        x = h_ref[...]
        r = lax.rsqrt(jnp.mean(x * x, axis=-1, keepdims=True) + EPS)
        xh = x * r
        du_ = du_ref[...]
        g = du_ * w_ref[...]
        dh_ref[...] = dr_ref[...] + r * (g - xh * jnp.mean(g * xh, axis=-1, keepdims=True))
        dw_ref[...] += jnp.sum(du_ * xh, axis=0, keepdims=True)

    return pl.pallas_call(
        body, grid=(t // tm,),
        in_specs=[pl.BlockSpec((tm, d), lambda i: (i, 0)), pl.BlockSpec((1, d), lambda i: (0, 0)),
                  pl.BlockSpec((tm, d), lambda i: (i, 0)), pl.BlockSpec((tm, d), lambda i: (i, 0))],
        out_specs=[pl.BlockSpec((tm, d), lambda i: (i, 0)), pl.BlockSpec((1, d), lambda i: (0, 0))],
        out_shape=[jax.ShapeDtypeStruct((t, d), F32), jax.ShapeDtypeStruct((1, d), F32)],
        compiler_params=_cparams("arbitrary"), name=name)(h, w, du, dres)


def _final_loss(h2, w, tgt, lp):
    t, d = h2.shape
    bsz, seq, _ = tgt.shape
    nblk = lp // SB_T

    def body(h_ref, w_ref, t_ref, loss_ref, dh_ref, dw_ref):
        b, i = pl.program_id(0), pl.program_id(1)

        @pl.when((b == 0) & (i == 0))
        def _():
            loss_ref[...] = jnp.zeros_like(loss_ref)
            dw_ref[...] = jnp.zeros_like(dw_ref)

        @pl.when(i == 0)
        def _():
            dh_ref[...] = jnp.zeros_like(dh_ref)

        @pl.when(i > 0)
        def _():
            x = h_ref[...]
            r = lax.rsqrt(jnp.mean(x * x, axis=-1, keepdims=True) + EPS)
            xh = x * r
            wv = w_ref[...]
            diff = xh * wv - t_ref[0]
            loss_ref[...] += 0.5 * jnp.sum(jnp.mean(diff * diff, axis=-1, keepdims=True), axis=0, keepdims=True)
            dy = diff * (1.0 / d)
            g = dy * wv
            dh_ref[...] = r * (g - xh * jnp.mean(g * xh, axis=-1, keepdims=True))
            dw_ref[...] += jnp.sum(dy * xh, axis=0, keepdims=True)

    return pl.pallas_call(
        body, grid=(bsz, nblk),
        in_specs=[pl.BlockSpec((SB_T, d), lambda b, i: (b * nblk + i, 0)), pl.BlockSpec((1, d), lambda b, i: (0, 0)),
                  pl.BlockSpec((1, SB_T, d), lambda b, i: (b, jnp.maximum(i - 1, 0), 0))],
        out_specs=[pl.BlockSpec((1, 128), lambda b, i: (0, 0)), pl.BlockSpec((SB_T, d), lambda b, i: (b * nblk + i, 0)),
                   pl.BlockSpec((1, d), lambda b, i: (0, 0))],
        out_shape=[jax.ShapeDtypeStruct((1, 128), F32), jax.ShapeDtypeStruct((t, d), F32), jax.ShapeDtypeStruct((1, d), F32)],
        compiler_params=_cparams("arbitrary", "arbitrary"), name="final_loss")(h2, w, tgt)


def _shift_down(x, j):
    return x if j == 0 else pltpu.roll(x, j, 0)


def _shift_up(x, j):
    return x if j == 0 else pltpu.roll(x, x.shape[0] - j, 0)


def _conv(x, cw, cb):
    out = cb + cw[CONV_W - 1:CONV_W, :] * x
    for j in range(1, CONV_W):
        out = out + cw[CONV_W - 1 - j:CONV_W - j, :] * _shift_down(x, j)
    return out


def _conv_bwd_x(dy, cw):
    out = cw[CONV_W - 1:CONV_W, :] * dy
    for j in range(1, CONV_W):
        out = out + cw[CONV_W - 1 - j:CONV_W - j, :] * _shift_up(dy, j)
    return out


def _conv_bwd_w(dcw_ref, dy, x):
    for j in range(CONV_W):
        k = CONV_W - 1 - j
        dcw_ref[k:k + 1, :] += jnp.sum(dy * _shift_down(x, j), axis=0, keepdims=True)


def _lru_forward_block(x, cw, cb, wa, ba, wx, bx, lam, a_ref, b_ref, h_ref):
    lp = x.shape[0]
    lx = _conv(x, cw, cb)
    lxb = lx.astype(BF16)
    r = _sigmoid(_dot(lxb, wa) + ba)
    i = _sigmoid(_dot(lxb, wx) + bx)
    spl = _softplus(-lam)
    log_a = (-RG_LRU_C) * r * spl
    a = jnp.exp(log_a)
    mult = jnp.sqrt(-jnp.tanh(log_a) * (1.0 + a * a))
    valid = _iota((lp, 1), 0) >= PAD
    a_ref[...] = a
    b_ref[...] = jnp.where(valid, mult * i * lx, 0.0)
    _block_scan(a_ref, b_ref, h_ref)
    return lx, lxb, r, i, spl, a, mult, valid


def _lru_specs(lp):
    w = LRU_BLOCK
    return [
        pl.BlockSpec((lp, w), lambda g, b: (b, g)),
        pl.BlockSpec((lp, w), lambda g, b: (b, LRU_BLOCKS + g)),
        pl.BlockSpec((CONV_W, w), lambda g, b: (0, g)),
        pl.BlockSpec((1, w), lambda g, b: (0, g)),
        pl.BlockSpec((1, w, w), lambda g, b: (g, 0, 0)),
        pl.BlockSpec((1, w), lambda g, b: (0, g)),
        pl.BlockSpec((1, w, w), lambda g, b: (g, 0, 0)),
        pl.BlockSpec((1, w), lambda g, b: (0, g)),
        pl.BlockSpec((1, w), lambda g, b: (0, g)),
    ]


def _lru_fwd(proj, cw, cb, wa, ba, wx, bx, lam, bsz, lp):
    w = LRU_BLOCK

    def body(x_ref, g_ref, cw_ref, cb_ref, wa_ref, ba_ref, wx_ref, bx_ref, lam_ref, y_ref, a_s, b_s, h_s):
        _lru_forward_block(x_ref[...], cw_ref[...], cb_ref[...], wa_ref[0], ba_ref[...], wx_ref[0], bx_ref[...],
                           lam_ref[...], a_s, b_s, h_s)
        gate = g_ref[...]
        y_ref[...] = (h_s[...] * gate * _sigmoid(gate)).astype(BF16)

    return pl.pallas_call(
        body, grid=(LRU_BLOCKS, bsz), in_specs=_lru_specs(lp),
        out_specs=pl.BlockSpec((lp, w), lambda g, b: (b, g)),
        out_shape=jax.ShapeDtypeStruct((bsz * lp, LRU_BLOCKS * w), BF16),
        scratch_shapes=[pltpu.VMEM((lp, w), F32)] * 3,
        compiler_params=_cparams("parallel", "arbitrary"), name="lru_fwd")(proj, proj, cw, cb, wa, ba, wx, bx, lam)


def _lru_bwd(proj, dmixed, cw, cb, wa, ba, wx, bx, lam, bsz, lp):
    w = LRU_BLOCK

    def body(x_ref, g_ref, cw_ref, cb_ref, wa_ref, ba_ref, wx_ref, bx_ref, lam_ref, dy_ref,
             dx_ref, dg_ref, dcw_ref, dcb_ref, dwa_ref, dba_ref, dwx_ref, dbx_ref, dlam_ref, a_s, b_s, h_s, dh_s):
        @pl.when(pl.program_id(1) == 0)
        def _():
            for ref in (dcw_ref, dcb_ref, dwa_ref, dba_ref, dwx_ref, dbx_ref, dlam_ref):
                ref[...] = jnp.zeros_like(ref)

        x = x_ref[...]
        cwv = cw_ref[...]
        wav, wxv, lam_ = wa_ref[0], wx_ref[0], lam_ref[...]
        lx, lxb, r, i, spl, a, mult, valid = _lru_forward_block(
            x, cwv, cb_ref[...], wav, ba_ref[...], wxv, bx_ref[...], lam_, a_s, b_s, h_s)
        gate = g_ref[...]
        sg = _sigmoid(gate)
        dy = dy_ref[...]
        h = h_s[...]
        dg_ref[...] = (dy * h * sg * (1.0 + gate * (1.0 - sg))).astype(BF16)
        b_s[...] = dy * gate * sg
        a_s[...] = jnp.where(_iota((lp, 1), 0) < lp - 1, _shift_up(a, 1), 0.0)
        _block_scan(a_s, b_s, dh_s, reverse=True)
        dh = dh_s[...]
        hprev = jnp.where(_iota((lp, 1), 0) >= 1, _shift_down(h, 1), 0.0)
        db = jnp.where(valid, dh, 0.0)
        dmult = db * i * lx
        di = db * mult * lx
        dlx = db * mult * i
        inv_mult = jnp.where(mult > 0.0, 1.0 / mult, 0.0)
        dlog_a = dh * hprev * a - dmult * (a * a) * inv_mult
        drp = dlog_a * ((-RG_LRU_C) * spl) * r * (1.0 - r)
        dip = di * i * (1.0 - i)
        dspl = jnp.sum(dlog_a * ((-RG_LRU_C) * r), axis=0, keepdims=True)
        dlam_ref[...] += dspl * (-_sigmoid(-lam_))
        dba_ref[...] += jnp.sum(drp, axis=0, keepdims=True)
        dbx_ref[...] += jnp.sum(dip, axis=0, keepdims=True)
        drpb, dipb = drp.astype(BF16), dip.astype(BF16)
        dwa_ref[0] += _dot_tn(lxb, drpb)
        dwx_ref[0] += _dot_tn(lxb, dipb)
        dlx = dlx + _dot_nt(drpb, wav) + _dot_nt(dipb, wxv)
        dcb_ref[...] += jnp.sum(dlx, axis=0, keepdims=True)
        _conv_bwd_w(dcw_ref, dlx, x)
        dx_ref[...] = _conv_bwd_x(dlx, cwv).astype(BF16)

    t = bsz * lp
    vec = pl.BlockSpec((1, w), lambda g, b: (0, g))
    mat = pl.BlockSpec((1, w, w), lambda g, b: (g, 0, 0))
    act = pl.BlockSpec((lp, w), lambda g, b: (b, g))
    return pl.pallas_call(
        body, grid=(LRU_BLOCKS, bsz), in_specs=_lru_specs(lp) + [act],
        out_specs=[act, act, pl.BlockSpec((CONV_W, w), lambda g, b: (0, g)), vec, mat, vec, mat, vec, vec],
        out_shape=[jax.ShapeDtypeStruct((t, 1024), BF16), jax.ShapeDtypeStruct((t, 1024), BF16),
                   jax.ShapeDtypeStruct((CONV_W, 1024), F32), jax.ShapeDtypeStruct((1, 1024), F32),
                   jax.ShapeDtypeStruct((LRU_BLOCKS, w, w), F32), jax.ShapeDtypeStruct((1, 1024), F32),
                   jax.ShapeDtypeStruct((LRU_BLOCKS, w, w), F32), jax.ShapeDtypeStruct((1, 1024), F32),
                   jax.ShapeDtypeStruct((1, 1024), F32)],
        scratch_shapes=[pltpu.VMEM((lp, w), F32)] * 4,
        compiler_params=_cparams("parallel", "arbitrary"), name="lru_bwd")(
            proj, proj, cw, cb, wa, ba, wx, bx, lam, dmixed)


def _sb_masks():
    j = _iota((SB_T, 2 * SB_T), 0)
    s = _iota((SB_T, 2 * SB_T), 1)
    right = ((s < SB_T) & (j > s)) | (s >= SB_T)
    left = ((s < SB_T) & (j < s)) | (s >= SB_T)
    return jnp.where(right, -1.0, 0.0).astype(BF16), left.astype(BF16)


def _xdot2(a, m01):
    a1 = a.astype(BF16)
    a2 = (a - a1.astype(F32)).astype(BF16)
    return _dot(a1, m01) + _dot(a2, m01)


def _sb_query_block(lp):
    for tq in (512, 256, 128):
        if (lp - SB_T) % tq == 0:
            return tq, (lp - SB_T) // tq
    raise ValueError(lp)


def _sb_key_rows(kj):
    return pl.ds(kj * SB_T if isinstance(kj, int) else pl.multiple_of(kj * SB_T, SB_T), SB_T)


def _sb_valid(q0, tq, kj):
    t_pos = q0 + _iota((tq, SB_T), 0)
    s_pos = kj * SB_T + _iota((tq, SB_T), 1)
    return (s_pos < t_pos) & (s_pos >= PAD)


def _sb_by_head(ref, rows):
    t = ref[rows, :]
    head0 = _iota((SB_T, 128), 1) < SB_D
    return jnp.concatenate([jnp.where(head0, t, 0.0), jnp.where(head0, 0.0, t)], axis=0).astype(BF16)


def _sb_tile(qb, k_ref, q0, kj, carries, m_right_neg, masked):
    rows_k = _sb_key_rows(kj)
    z2 = _dot_nt(qb, _sb_by_head(k_ref, rows_k))
    valid = _sb_valid(q0, qb.shape[0], kj) if masked else None
    out = []
    for h in range(2):
        z = z2[:, h * SB_T:(h + 1) * SB_T]
        sp = _softplus(z)
        rs = _dot((jnp.where(valid, sp, 0.0) if masked else sp).astype(BF16), m_right_neg)
        lb = z - sp
        wgt = jnp.exp(lb + rs[:, :SB_T] + carries[h])
        if masked:
            wgt = jnp.where(valid, wgt, 0.0)
        out.append((lb, wgt, carries[h] + rs[:, SB_T:]))
    return rows_k, valid, out


def _tail(x, row0):
    return x if row0 == 0 else x[row0:]


def _merge(old, new_tail, row0):
    return new_tail if row0 == 0 else jnp.concatenate([old[:row0], new_tail], axis=0)


def _sb_sweep(step, c, blk, tq, first, leftwards):
    if first:
        return step(0, True, 0)(c)
    r = tq // SB_T
    u = 2 if r % 2 == 0 else 1
    lo = 1 + r * blk

    def diag(c):
        for d in (range(r - 1, -1, -1) if leftwards else range(r)):
            c = step(lo + d, True, SB_T * d)(c)
        return c

    def inner(c):
        def trip(i, c):
            for j in range(u):
                d = u * i + j
                c = step(lo - 1 - d if leftwards else 1 + d, False, 0)(c)
            return c
        return lax.fori_loop(0, (r // u) * blk, trip, c)

    if leftwards:
        return step(0, True, 0)(inner(diag(c)))
    return diag(inner(step(0, True, 0)(c)))


def _sb_specs(lp):
    nh = SB_HEADS // 2
    return [pl.BlockSpec((lp, 128), lambda b, p: (b, p)), pl.BlockSpec((lp, 128), lambda b, p: (b, nh + p)),
            pl.BlockSpec((lp, 128), lambda b, p: (b, 2 * nh + p)), pl.BlockSpec((lp, 128), lambda b, p: (b, 3 * nh + p))]


def _sb_fwd(qkvg, bsz, lp):
    tq, nb = _sb_query_block(lp)
    scale = SB_D ** -0.5

    def body(q_ref, k_ref, v_ref, g_ref, o_ref, og_ref):
        m_right_neg, _ = _sb_masks()

        def q_block(blk, rows, first):
            q0 = 0 if first else (1 + (rows // SB_T) * blk) * SB_T
            rows_q = pl.ds(q0 if first else pl.multiple_of(q0, SB_T), rows)
            qb = (q_ref[rows_q, :] * scale).astype(BF16)

            def step(kj, masked, row0):
                def run(c):
                    acc, car0, car1 = c
                    rows_k, _, ((_, w0, new0), (_, w1, new1)) = _sb_tile(
                        _tail(qb, row0), k_ref, q0 + row0, kj, (_tail(car0, row0), _tail(car1, row0)), m_right_neg, masked)
                    w2 = jnp.concatenate([w0.astype(BF16), w1.astype(BF16)], axis=1)
                    upd = _tail(acc, row0) + _dot(w2, _sb_by_head(v_ref, rows_k))
                    return _merge(acc, upd, row0), _merge(car0, new0, row0), _merge(car1, new1, row0)
                return run

            zero = jnp.zeros((rows, SB_T), F32)
            o_ref[rows_q, :] = _sb_sweep(step, (zero, zero, zero), blk, rows, first, True)[0]

        q_block(0, SB_T, True)

        def big_block(i, _):
            q_block(i, tq, False)
            return 0

        lax.fori_loop(0, nb, big_block, 0)
        gate = g_ref[...]
        og_ref[...] = (o_ref[...] * gate * _sigmoid(gate)).astype(BF16)

    t = bsz * lp
    blk = pl.BlockSpec((lp, 128), lambda b, p: (b, p))
    return pl.pallas_call(
        body, grid=(bsz, SB_HEADS // 2), in_specs=_sb_specs(lp), out_specs=[blk, blk],
        out_shape=[jax.ShapeDtypeStruct((t, 1024), F32), jax.ShapeDtypeStruct((t, 1024), BF16)],
        compiler_params=_cparams("parallel", "parallel"), name="sb_fwd")(qkvg, qkvg, qkvg, qkvg)


def _sb_bwd(qkvg, o, dog, bsz, lp):
    tq, nb = _sb_query_block(lp)
    nk = lp // SB_T
    scale = SB_D ** -0.5

    def body(q_ref, k_ref, v_ref, g_ref, o_ref, dog_ref, dq_ref, dk_ref, dv_ref, dg_ref, do_s, dk_s, dv_s, e_s, sig_s):
        m_right_neg, m_left = _sb_masks()
        gate = g_ref[...]
        sg = _sigmoid(gate)
        dog = dog_ref[...]
        do_s[...] = dog * gate * sg
        dg_ref[...] = (dog * o_ref[...] * sg * (1.0 + gate * (1.0 - sg))).astype(BF16)
        dk_s[...] = jnp.zeros_like(dk_s)
        dv_s[...] = jnp.zeros_like(dv_s)
        def q_block(blk, rows, first):
            q0 = 0 if first else (1 + (rows // SB_T) * blk) * SB_T
            rows_q = pl.ds(q0 if first else pl.multiple_of(q0, SB_T), rows)
            head0 = _iota((rows, 128), 1) < SB_D
            qf = q_ref[rows_q, :] * scale
            qb = qf.astype(BF16)
            q_h = (jnp.where(head0, qf, 0.0).astype(BF16), jnp.where(head0, 0.0, qf).astype(BF16))
            do_f = do_s[rows_q, :]
            do_b = do_f.astype(BF16)
            do_h = (jnp.where(head0, do_f, 0.0).astype(BF16), jnp.where(head0, 0.0, do_f).astype(BF16))

            def left(kj, masked, row0):
                def run(c):
                    rows_k, _, heads = _sb_tile(_tail(qb, row0), k_ref, q0 + row0, kj, (_tail(c[0], row0), _tail(c[1], row0)),
                                                m_right_neg, masked)
                    dw2 = _dot_nt(_tail(do_b, row0), _sb_by_head(v_ref, rows_k))
                    dv = None
                    for h, (lb, wgt, _) in enumerate(heads):
                        e_s[2 * kj + h, row0:rows, :] = wgt * dw2[:, h * SB_T:(h + 1) * SB_T]
                        sig_s[2 * kj + h, row0:rows, :] = jnp.exp(lb)
                        part = _dot_tn(wgt.astype(BF16), _tail(do_h[h], row0))
                        dv = part if dv is None else dv + part
                    dv_s[rows_k, :] += dv
                    return _merge(c[0], heads[0][2], row0), _merge(c[1], heads[1][2], row0)
                return run

            zero = jnp.zeros((rows, SB_T), F32)
            _sb_sweep(left, (zero, zero), blk, rows, first, True)

            def right(kj, masked, row0):
                def run(c):
                    dq, ecar = c[0], list(c[1:])
                    rows_k = _sb_key_rows(kj)
                    valid = _sb_valid(q0 + row0, rows - row0, kj) if masked else None
                    dzs, dk = [], None
                    for h in range(2):
                        e = e_s[2 * kj + h, row0:rows, :]
                        se = _dot(e.astype(BF16), m_left)
                        ec = _tail(ecar[h], row0)
                        dz = e - sig_s[2 * kj + h, row0:rows, :] * (e + se[:, :SB_T] + ec)
                        if masked:
                            dz = jnp.where(valid, dz, 0.0)
                        dz = dz.astype(BF16)
                        part = _dot_tn(dz, _tail(q_h[h], row0))
                        dk = part if dk is None else dk + part
                        dzs.append(dz)
                        ecar[h] = _merge(ecar[h], ec + se[:, SB_T:], row0)
                    dk_s[rows_k, :] += dk
                    upd = _tail(dq, row0) + _dot(jnp.concatenate(dzs, axis=1), _sb_by_head(k_ref, rows_k))
                    return _merge(dq, upd, row0), ecar[0], ecar[1]
                return run

            dq = _sb_sweep(right, (zero, zero, zero), blk, rows, first, False)[0]
            dq_ref[rows_q, :] = (dq * scale).astype(BF16)

        q_block(0, SB_T, True)

        def big_block(i, _):
            q_block(i, tq, False)
            return 0

        lax.fori_loop(0, nb, big_block, 0)
        dk_ref[...] = dk_s[...].astype(BF16)
        dv_ref[...] = dv_s[...].astype(BF16)

    t = bsz * lp
    blk = pl.BlockSpec((lp, 128), lambda b, p: (b, p))
    shp = jax.ShapeDtypeStruct((t, 1024), BF16)
    return pl.pallas_call(
        body, grid=(bsz, SB_HEADS // 2), in_specs=_sb_specs(lp) + [blk, blk], out_specs=[blk] * 4,
        out_shape=[shp] * 4,
        scratch_shapes=[pltpu.VMEM((lp, 128), F32)] * 3 + [pltpu.VMEM((2 * nk, tq, SB_T), F32)] * 2,
        compiler_params=_cparams("parallel", "parallel"), name="sb_bwd")(qkvg, qkvg, qkvg, qkvg, o, dog)


XBC_COL0 = 3072 // 256
DT_COL0 = 4608 // 128
DT_L = 128


def _ssd_pre_fwd(proj, cw, cb, bsz, lp):
    def body(x_ref, cw_ref, cb_ref, o_ref):
        pre = _conv(x_ref[...], cw_ref[...], cb_ref[...])
        o_ref[...] = pre * _sigmoid(pre)

    return pl.pallas_call(
        body, grid=(bsz, SSD_CONV_DIM // 256),
        in_specs=[pl.BlockSpec((lp, 256), lambda b, j: (b, XBC_COL0 + j)), pl.BlockSpec((CONV_W, 256), lambda b, j: (0, j)),
                  pl.BlockSpec((1, 256), lambda b, j: (0, j))],
        out_specs=pl.BlockSpec((lp, 256), lambda b, j: (b, j)),
        out_shape=jax.ShapeDtypeStruct((bsz * lp, SSD_CONV_DIM), F32),
        compiler_params=_cparams("parallel", "parallel"), name="ssd_pre_fwd")(proj, cw, cb)


def _ssd_pre_bwd(proj, dact, cw, cb, bsz, lp):
    def body(x_ref, d_ref, cw_ref, cb_ref, dx_ref, dcw_ref, dcb_ref):
        @pl.when(pl.program_id(1) == 0)
        def _():
            dcw_ref[...] = jnp.zeros_like(dcw_ref)
            dcb_ref[...] = jnp.zeros_like(dcb_ref)
        x = x_ref[...]
        cwv = cw_ref[...]
        pre = _conv(x, cwv, cb_ref[...])
        s = _sigmoid(pre)
        dpre = d_ref[...].astype(F32) * s * (1.0 + pre * (1.0 - s))
        dcb_ref[...] += jnp.sum(dpre, axis=0, keepdims=True)
        _conv_bwd_w(dcw_ref, dpre, x)
        dx_ref[...] = _conv_bwd_x(dpre, cwv).astype(BF16)

    return pl.pallas_call(
        body, grid=(SSD_CONV_DIM // 256, bsz),
        in_specs=[pl.BlockSpec((lp, 256), lambda j, b: (b, XBC_COL0 + j)), pl.BlockSpec((lp, 256), lambda j, b: (b, j)),
                  pl.BlockSpec((CONV_W, 256), lambda j, b: (0, j)), pl.BlockSpec((1, 256), lambda j, b: (0, j))],
        out_specs=[pl.BlockSpec((lp, 256), lambda j, b: (b, j)), pl.BlockSpec((CONV_W, 256), lambda j, b: (0, j)),
                   pl.BlockSpec((1, 256), lambda j, b: (0, j))],
        out_shape=[jax.ShapeDtypeStruct((bsz * lp, SSD_CONV_DIM), BF16), jax.ShapeDtypeStruct((CONV_W, SSD_CONV_DIM), F32),
                   jax.ShapeDtypeStruct((1, SSD_CONV_DIM), F32)],
        compiler_params=_cparams("parallel", "arbitrary"), name="ssd_pre_bwd")(proj, dact, cw, cb)


def _split2(a):
    a1 = a.astype(BF16)
    return a1, (a - a1.astype(F32)).astype(BF16)


def _xdot2_nt(a, m01):
    a1, a2 = _split2(a)
    return _dot_nt(a1, m01) + _dot_nt(a2, m01)


def _xdot2_l(m01, a):
    a1, a2 = _split2(a)
    return _dot(m01, a1) + _dot(m01, a2)


class _SsdConsts:
    def __init__(self, g):
        q, gw = SSD_Q, SSD_GW
        head_of_lane = lax.shift_right_logical(_iota((DT_L, gw), 1), 6)
        self.sel = (_iota((DT_L, gw), 0) == 8 * g + head_of_lane).astype(BF16)
        r = _iota((q, gw), 0)
        c = jnp.bitwise_and(_iota((q, gw), 1), q - 1)
        self.diag = r == c
        self.diag_b = self.diag.astype(BF16)
        self.lower = c <= r
        self.upper = c >= r
        self.bd = lax.shift_right_logical(_iota((gw, gw), 0), 6) == lax.shift_right_logical(_iota((gw, gw), 1), 6)
        jj, ll = _iota((q, q), 1), _iota((q, q), 0)
        self.tri = (jj <= ll).astype(BF16)
        self.tri_t = (jj >= ll).astype(BF16)
        self.ones = jnp.ones((q, q), BF16)
        self.last = _iota((q, 1), 0) == q - 1


def _ssd_prepass(k, dt_ref, bias_ref, alog_ref, dtm_s, abc_s, lp):
    valid = _iota((lp, 1), 0) >= PAD
    sp_in = dt_ref[...] + bias_ref[...]
    dtm = jnp.where(valid, _softplus(sp_in), 0.0)
    aneg = -jnp.exp(alog_ref[...])
    dtm_s[...] = dtm
    abc_s[...] = _xdot_r(dtm * aneg, k.sel)
    return valid, sp_in, aneg


def _ssd_loop(nc, chunk, init):
    u = 2 if nc % 2 == 0 else 1

    def trip(i, carry):
        for j in range(u):
            carry = chunk(i * u + j, carry)
        return carry

    return lax.fori_loop(0, nc // u, trip, init)


def _ssd_chunk(k, rows, xs_ref, b_ref, c_ref, dtm_s, abc_s):
    bc = _xdot_l(k.tri, abc_s[rows, :])
    tt = jnp.sum(jnp.where(k.diag, bc, 0.0), axis=0, keepdims=True)
    dtbc = _xdot2(dtm_s[rows, :], k.sel)
    xs = xs_ref[rows, :]
    x = xs * dtbc
    bb = b_ref[rows, :].astype(BF16)
    cc = c_ref[rows, :].astype(BF16)
    tot = bc[SSD_Q - 1:SSD_Q, :]
    xbd = jnp.where(k.bd, jnp.concatenate([x] * 8, axis=0), 0.0).astype(BF16)
    return bc, tt, dtbc, xs, x, bb, cc, tot, xbd


def _ssd_specs(lp, order):
    ix = (lambda f: (lambda b, g: f(b, g))) if order == "bg" else (lambda f: (lambda g, b: f(b, g)))
    return [pl.BlockSpec((lp, SSD_GW), ix(lambda b, g: (b, g))),
            pl.BlockSpec((lp, SSD_N), ix(lambda b, g: (b, 1024 // SSD_N + g))),
            pl.BlockSpec((lp, SSD_N), ix(lambda b, g: (b, 1280 // SSD_N + g))),
            pl.BlockSpec((lp, DT_L), ix(lambda b, g: (b, DT_COL0))),
            pl.BlockSpec((1, DT_L), ix(lambda b, g: (0, 0))),
            pl.BlockSpec((1, DT_L), ix(lambda b, g: (0, 0))),
            pl.BlockSpec((1, DT_L), ix(lambda b, g: (0, 0)))]


def _ssd_fwd(xbc, proj, dt_bias, a_log, dskip, bsz, lp):
    nc = lp // SSD_Q

    def body(xs_ref, b_ref, c_ref, dt_ref, bias_ref, alog_ref, dsk_ref, y_ref, dtm_s, abc_s):
        k = _SsdConsts(pl.program_id(1))
        _ssd_prepass(k, dt_ref, bias_ref, alog_ref, dtm_s, abc_s, lp)
        dvec = _xdot_r(jnp.broadcast_to(dsk_ref[...], (8, DT_L)), k.sel)[0:1, :]

        def chunk(c, state):
            rows = pl.ds(pl.multiple_of(c * SSD_Q, SSD_Q), SSD_Q)
            bc, tt, _, xs, x, bb, cc, tot, xbd = _ssd_chunk(k, rows, xs_ref, b_ref, c_ref, dtm_s, abc_s)
            lm = jnp.where(k.lower, jnp.exp(jnp.minimum(bc - tt, 0.0)), 0.0)
            g_all = _dot(_dot_nt(cc, bb).astype(BF16), k.diag_b) * lm
            y = _dot(g_all.astype(BF16), xbd) + jnp.exp(bc) * _dot(cc, state.astype(BF16)) + dvec * xs
            y_ref[rows, :] = y
            return jnp.exp(tot) * state + _dot_tn(bb, (jnp.exp(tot - bc) * x).astype(BF16))

        _ssd_loop(nc, chunk, jnp.zeros((SSD_N, SSD_GW), F32))

    return pl.pallas_call(
        body, grid=(bsz, SSD_GROUPS), in_specs=_ssd_specs(lp, "bg"),
        out_specs=pl.BlockSpec((lp, SSD_GW), lambda b, g: (b, g)),
        out_shape=jax.ShapeDtypeStruct((bsz * lp, 1024), F32),
        scratch_shapes=[pltpu.VMEM((lp, DT_L), F32), pltpu.VMEM((lp, SSD_GW), F32)],
        compiler_params=_cparams("parallel", "parallel"), name="ssd_fwd")(xbc, xbc, xbc, proj, dt_bias, a_log, dskip)


def _ssd_bwd(xbc, proj, dt_bias, a_log, dskip, dy, bsz, lp):
    nc = lp // SSD_Q

    def body(xs_ref, b_ref, c_ref, dt_ref, bias_ref, alog_ref, dsk_ref, dy_ref,
             dxs_ref, db_ref, dc_ref, ddt_ref, dbias_ref, dalog_ref, ddsk_ref, dtm_s, abc_s, st_s):
        @pl.when(pl.program_id(1) == 0)
        def _():
            for ref in (dbias_ref, dalog_ref, ddsk_ref):
                ref[...] = jnp.zeros_like(ref)

        k = _SsdConsts(pl.program_id(0))
        valid, sp_in, aneg = _ssd_prepass(k, dt_ref, bias_ref, alog_ref, dtm_s, abc_s, lp)
        dvec = _xdot_r(jnp.broadcast_to(dsk_ref[...], (8, DT_L)), k.sel)[0:1, :]

        def fwd_chunk(c, state):
            rows = pl.ds(pl.multiple_of(c * SSD_Q, SSD_Q), SSD_Q)
            st_s[c] = state.astype(BF16)
            bc, _, _, _, x, bb, _, tot, _ = _ssd_chunk(k, rows, xs_ref, b_ref, c_ref, dtm_s, abc_s)
            return jnp.exp(tot) * state + _dot_tn(bb, (jnp.exp(tot - bc) * x).astype(BF16))

        _ssd_loop(nc, fwd_chunk, jnp.zeros((SSD_N, SSD_GW), F32))

        def bwd_chunk(i, carry):
            dstate, daneg, ddsk = carry
            c = nc - 1 - i
            rows = pl.ds(pl.multiple_of(c * SSD_Q, SSD_Q), SSD_Q)
            bc, tt, dtbc, xs, x, bb, cc, tot, xbd = _ssd_chunk(k, rows, xs_ref, b_ref, c_ref, dtm_s, abc_s)
            sprev = st_s[c]
            dyc = dy_ref[rows, :]
            dyb = dyc.astype(BF16)
            seg = bc - tt
            lm = jnp.where(k.lower, jnp.exp(jnp.minimum(seg, 0.0)), 0.0)
            lm_t = jnp.where(k.upper, jnp.exp(jnp.minimum(-seg, 0.0)), 0.0)
            cb_all = _dot(_dot_nt(cc, bb).astype(BF16), k.diag_b)
            cbt_all = _dot(_dot_nt(bb, cc).astype(BF16), k.diag_b)
            g_all = cb_all * lm
            dybd = jnp.where(k.bd, jnp.concatenate([dyc] * 8, axis=0), 0.0).astype(BF16)
            dg = _dot_nt(dyb, xbd)
            dx = _dot((cbt_all * lm_t).astype(BF16), dybd)
            hh = dg * g_all
            ea = jnp.exp(bc)
            yo = ea * _dot(cc, sprev)
            col_h = jnp.sum(hh, axis=0, keepdims=True)
            dcb = _dot_nt((dg * lm).astype(BF16), k.diag_b).astype(BF16)
            dcs = (ea * dyc).astype(BF16)
            dstb = dstate.astype(BF16)
            dec = jnp.exp(tot - bc)
            w = dec * x
            dw = _dot(bb, dstb)
            dc_ref[rows, :] = (_dot(dcb, bb) + _dot_nt(dcs, sprev)).astype(BF16)
            db_ref[rows, :] = (_dot_tn(dcb, cc) + _dot_nt(w.astype(BF16), dstb)).astype(BF16)
            dx = dx + dec * dw
            etot = jnp.exp(tot)
            r8 = _iota((8, SSD_GW), 0)
            st_row = jnp.sum(dstate * sprev.astype(F32), axis=0, keepdims=True) * etot
            sk_row = jnp.sum(dyc * xs, axis=0, keepdims=True)
            small = jnp.where(r8 == 0, st_row, jnp.where(r8 == 1, sk_row, 0.0))
            q = SSD_Q
            sums = _xdot2_nt(jnp.concatenate([hh + dyc * yo - jnp.where(k.diag, col_h, 0.0) - dw * w, dw * w, dx * xs, small],
                                             axis=0), k.sel)
            kk = sums[q:2 * q]
            dtot = sums[3 * q:3 * q + 1] + jnp.sum(kk, axis=0, keepdims=True)
            dacum = sums[0:q] + jnp.where(k.last, dtot, 0.0)
            da = _xdot2_l(k.tri_t, dacum)
            dtm_c = dtm_s[rows, :]
            ddtm = da * aneg + sums[2 * q:3 * q]
            vrow = (c * SSD_Q + _iota((SSD_Q, 1), 0)) >= PAD
            ddt_ref[rows, :] = jnp.where(vrow, ddtm * _sigmoid(dt_ref[rows, :] + bias_ref[...]), 0.0)
            dxs_ref[rows, :] = (dx * dtbc + dvec * dyc).astype(BF16)
            daneg = daneg + jnp.sum(da * dtm_c, axis=0, keepdims=True)
            ddsk = ddsk + sums[3 * q + 1:3 * q + 2]
            dstate = etot * dstate + _dot_tn(cc, dcs)
            return dstate, daneg, ddsk

        zrow = jnp.zeros((1, DT_L), F32)
        _, daneg, ddsk = _ssd_loop(nc, bwd_chunk, (jnp.zeros((SSD_N, SSD_GW), F32), zrow, zrow))
        dbias_ref[...] += jnp.broadcast_to(jnp.sum(ddt_ref[...], axis=0, keepdims=True), (8, DT_L))
        dalog_ref[...] += jnp.broadcast_to(daneg * aneg, (8, DT_L))
        ddsk_ref[...] += jnp.broadcast_to(ddsk, (8, DT_L))

    t = bsz * lp
    par = pl.BlockSpec((8, DT_L), lambda g, b: (g, 0))
    par_shape = jax.ShapeDtypeStruct((8 * SSD_GROUPS, DT_L), F32)
    return pl.pallas_call(
        body, grid=(SSD_GROUPS, bsz),
        in_specs=_ssd_specs(lp, "gb") + [pl.BlockSpec((lp, SSD_GW), lambda g, b: (b, g))],
        out_specs=[pl.BlockSpec((lp, SSD_GW), lambda g, b: (b, g)), pl.BlockSpec((lp, SSD_N), lambda g, b: (b, g)),
                   pl.BlockSpec((lp, SSD_N), lambda g, b: (b, g)), pl.BlockSpec((lp, DT_L), lambda g, b: (b, g)), par, par, par],
        out_shape=[jax.ShapeDtypeStruct((t, 1024), BF16), jax.ShapeDtypeStruct((t, 256), BF16),
                   jax.ShapeDtypeStruct((t, 256), BF16), jax.ShapeDtypeStruct((t, SSD_GROUPS * DT_L), F32),
                   par_shape, par_shape, par_shape],
        scratch_shapes=[pltpu.VMEM((lp, DT_L), F32), pltpu.VMEM((lp, SSD_GW), F32), pltpu.VMEM((nc, SSD_N, SSD_GW), BF16)],
        compiler_params=_cparams("parallel", "arbitrary"), name="ssd_bwd")(
            xbc, xbc, xbc, proj, dt_bias, a_log, dskip, dy)


Z_COL0 = 2048 // SSD_GW


def _gnorm_fwd(y, proj, w, name="gnorm_fwd"):
    t = y.shape[0]
    tm = _row_tile(t)

    def body(y_ref, z_ref, w_ref, o_ref):
        z = z_ref[...]
        gt = y_ref[...] * z * _sigmoid(z)
        r = lax.rsqrt(jnp.mean(gt * gt, axis=-1, keepdims=True) + EPS)
        o_ref[...] = (gt * r * w_ref[...]).astype(BF16)

    return pl.pallas_call(
        body, grid=(t // tm, SSD_GROUPS),
        in_specs=[pl.BlockSpec((tm, SSD_GW), lambda i, g: (i, g)), pl.BlockSpec((tm, SSD_GW), lambda i, g: (i, Z_COL0 + g)),
                  pl.BlockSpec((1, SSD_GW), lambda i, g: (0, g))],
        out_specs=pl.BlockSpec((tm, SSD_GW), lambda i, g: (i, g)),
        out_shape=jax.ShapeDtypeStruct((t, 1024), BF16),
        compiler_params=_cparams("parallel", "parallel"), name=name)(y, proj, w)


def _gnorm_bwd(y, proj, w, dmixed):
    t = y.shape[0]
    tm = _row_tile(t)

    def body(y_ref, z_ref, w_ref, d_ref, dy_ref, dz_ref, dw_ref):
        @pl.when(pl.program_id(1) == 0)
        def _():
            dw_ref[...] = jnp.zeros_like(dw_ref)
        z, yv, d = z_ref[...], y_ref[...], d_ref[...]
        s = _sigmoid(z)
        sz = z * s
        gt = yv * sz
        r = lax.rsqrt(jnp.mean(gt * gt, axis=-1, keepdims=True) + EPS)
        gh = gt * r
        dgn = d * w_ref[...]
        dgt = r * (dgn - gh * jnp.mean(dgn * gh, axis=-1, keepdims=True))
        dw_ref[...] += jnp.sum(d * gh, axis=0, keepdims=True)
        dy_ref[...] = dgt * sz
        dz_ref[...] = (dgt * yv * s * (1.0 + z * (1.0 - s))).astype(BF16)

    blk = pl.BlockSpec((tm, SSD_GW), lambda g, i: (i, g))
    return pl.pallas_call(
        body, grid=(SSD_GROUPS, t // tm),
        in_specs=[blk, pl.BlockSpec((tm, SSD_GW), lambda g, i: (i, Z_COL0 + g)), pl.BlockSpec((1, SSD_GW), lambda g, i: (0, g)),
                  pl.BlockSpec((tm, SSD_GW), lambda g, i: (i, 1024 // SSD_GW + g))],
        out_specs=[blk, blk, pl.BlockSpec((1, SSD_GW), lambda g, i: (0, g))],
        out_shape=[jax.ShapeDtypeStruct((t, 1024), F32), jax.ShapeDtypeStruct((t, 1024), BF16),
                   jax.ShapeDtypeStruct((1, 1024), F32)],
        compiler_params=_cparams("parallel", "arbitrary"), name="gnorm_bwd")(y, proj, w, dmixed)


def _meta_grad(dh0, bsz, lp):
    def body(d_ref, o_ref):
        @pl.when(pl.program_id(0) == 0)
        def _():
            o_ref[...] = jnp.zeros_like(o_ref)
        o_ref[...] += d_ref[...]

    return pl.pallas_call(
        body, grid=(bsz,),
        in_specs=[pl.BlockSpec((N_META, D_MODEL), lambda b: (b * (lp // N_META) + PAD // N_META, 0))],
        out_specs=pl.BlockSpec((N_META, D_MODEL), lambda b: (0, 0)),
        out_shape=jax.ShapeDtypeStruct((N_META, D_MODEL), F32),
        compiler_params=_cparams("arbitrary"), name="meta_grad")(dh0)


OTHER_CHIPS = ((1, 0), (0, 1), (1, 1))
ANY_SPEC = pl.BlockSpec(memory_space=pl.ANY)


class _Mesh:
    def __init__(self):
        self.x, self.y, self.c = lax.axis_index("x"), lax.axis_index("y"), lax.axis_index("c")

    def dev(self, fx, fy, fc):
        return (1 - self.x if fx else self.x, 1 - self.y if fy else self.y, 1 - self.c if fc else self.c)

    def slot(self, fx, fy, fc):
        px, py, pc = self.dev(fx, fy, fc)
        return 4 * px + 2 * py + pc

    def chip(self, fx, fy):
        px, py, _ = self.dev(fx, fy, 0)
        return 2 * px + py


def _remote(src, dst, send_sems, recv_sems, idx, dev):
    return pltpu.make_async_remote_copy(src_ref=src, dst_ref=dst, send_sem=send_sems.at[idx], recv_sem=recv_sems.at[idx],
                                        device_id=dev, device_id_type=pl.DeviceIdType.MESH)


def _gather_all(arrs, name):
    n = len(arrs)

    def body(*refs):
        ins, outs = refs[:n], refs[n:2 * n]
        send_sems, recv_sems = refs[2 * n:]
        me = _Mesh()
        first, passed = [], []
        for a in range(n):
            mine = outs[a].at[me.slot(0, 0, 0)]
            first.append(_remote(ins[a], mine, send_sems, recv_sems, (a, 0), me.dev(0, 0, 1)))
            for j, (fx, fy) in enumerate(OTHER_CHIPS):
                first.append(_remote(ins[a], mine, send_sems, recv_sems, (a, 1 + j), me.dev(fx, fy, 0)))
        for cp in first:
            cp.start()
        for a in range(n):
            for j, (fx, fy) in enumerate(OTHER_CHIPS):
                blk = outs[a].at[me.slot(fx, fy, 0)]
                _remote(ins[a], blk, send_sems, recv_sems, (a, 1 + j), me.dev(fx, fy, 0)).wait_recv()
                cp = _remote(blk, blk, send_sems, recv_sems, (a, 4 + j), me.dev(0, 0, 1))
                cp.start()
                passed.append(cp)
        for a in range(n):
            _remote(ins[a], outs[a].at[me.slot(0, 0, 1)], send_sems, recv_sems, (a, 0), me.dev(0, 0, 1)).wait_recv()
            for j, (fx, fy) in enumerate(OTHER_CHIPS):
                _remote(ins[a], outs[a].at[me.slot(fx, fy, 1)], send_sems, recv_sems, (a, 4 + j), me.dev(0, 0, 1)).wait_recv()
        for cp in first + passed:
            cp.wait_send()

    stacks = pl.pallas_call(
        body, in_specs=[ANY_SPEC] * n, out_specs=[ANY_SPEC] * n,
        out_shape=[jax.ShapeDtypeStruct((N_DEV,) + a.shape, a.dtype) for a in arrs],
        scratch_shapes=[pltpu.SemaphoreType.DMA((n, 7)), pltpu.SemaphoreType.DMA((n, 7))],
        compiler_params=pltpu.CompilerParams(has_side_effects=True), name=name)(*arrs)
    me = 4 * lax.axis_index("x") + 2 * lax.axis_index("y") + lax.axis_index("c")
    return [lax.dynamic_update_slice_in_dim(s, a[None], me, axis=0) for s, a in zip(stacks, arrs)]


def _swap_in_chip(arrs, name):
    n = len(arrs)

    def body(*refs):
        ins, got = refs[:n], refs[n:2 * n]
        send_sems, recv_sems = refs[2 * n:]
        me = _Mesh()
        sends = [_remote(ins[a].at[2 * j + (1 - me.c)], got[a].at[j], send_sems, recv_sems, (a, j), me.dev(0, 0, 1))
                 for a in range(n) for j in range(4)]
        for cp in sends:
            cp.start()
        for cp in sends:
            cp.wait()

    return pl.pallas_call(
        body, in_specs=[ANY_SPEC] * n, out_specs=[ANY_SPEC] * n,
        out_shape=[jax.ShapeDtypeStruct((4,) + a.shape[1:], a.dtype) for a in arrs],
        scratch_shapes=[pltpu.SemaphoreType.DMA((n, 4)), pltpu.SemaphoreType.DMA((n, 4))],
        compiler_params=pltpu.CompilerParams(has_side_effects=True), name=name)(*arrs)


def _add_pair(a, b, name):
    _, r, c = a.shape
    tr = _adam_rows(r)

    def body(a_ref, b_ref, o_ref):
        o_ref[...] = (a_ref[...].astype(F32) + b_ref[...].astype(F32)).astype(o_ref.dtype)

    blk = pl.BlockSpec((4, tr, c), lambda i: (0, i, 0))
    return pl.pallas_call(body, grid=(r // tr,), in_specs=[blk, blk], out_specs=blk,
                          out_shape=jax.ShapeDtypeStruct(a.shape, a.dtype), compiler_params=_cparams("parallel"), name=name)(a, b)


def _swap_chips(sums, gather, name):
    ns, ng = len(sums), len(gather)
    n = ns + ng

    def body(*refs):
        ins, outs = refs[:n], refs[n:2 * n]
        send_sems, recv_sems, local_sems = refs[2 * n:]
        me = _Mesh()
        local, sends, recvs = [], [], []
        for a in range(ns):
            here = me.chip(0, 0)
            for j, (fx, fy) in enumerate(OTHER_CHIPS):
                there = me.chip(fx, fy)
                sends.append(_remote(ins[a].at[there], outs[a].at[here], send_sems, recv_sems, (a, j), me.dev(fx, fy, 0)))
                recvs.append(_remote(ins[a].at[there], outs[a].at[there], send_sems, recv_sems, (a, j), me.dev(fx, fy, 0)))
        for a in range(ns, n):
            local.append(pltpu.make_async_copy(ins[a], outs[a].at[me.slot(0, 0, 0)], local_sems.at[a]))
            for k in range(1, N_DEV):
                flips = (k >> 2 & 1, k >> 1 & 1, k & 1)
                sends.append(_remote(ins[a], outs[a].at[me.slot(0, 0, 0)], send_sems, recv_sems, (a, k - 1), me.dev(*flips)))
                recvs.append(_remote(ins[a], outs[a].at[me.slot(*flips)], send_sems, recv_sems, (a, k - 1), me.dev(*flips)))
        for cp in local + sends:
            cp.start()
        for cp in recvs:
            cp.wait_recv()
        for cp in sends:
            cp.wait_send()
        for cp in local:
            cp.wait()

    out_shape = [jax.ShapeDtypeStruct(a.shape, a.dtype) for a in sums]
    out_shape += [jax.ShapeDtypeStruct((N_DEV,) + a.shape, a.dtype) for a in gather]
    outs = pl.pallas_call(
        body, in_specs=[ANY_SPEC] * n, out_specs=[ANY_SPEC] * n, out_shape=out_shape,
        scratch_shapes=[pltpu.SemaphoreType.DMA((n, N_DEV - 1)), pltpu.SemaphoreType.DMA((n, N_DEV - 1)),
                        pltpu.SemaphoreType.DMA((n,))],
        compiler_params=pltpu.CompilerParams(has_side_effects=True), name=name)(*sums, *gather)
    here = 2 * lax.axis_index("x") + lax.axis_index("y")
    mine = [lax.dynamic_update_slice_in_dim(o, lax.dynamic_slice_in_dim(s, here, 1, axis=0), here, axis=0)
            for o, s in zip(outs[:ns], sums)]
    return mine + list(outs[ns:])


def _adam_rows(r):
    for cand in (128, 64, 32, 16, 8):
        if r % cand == 0 and r > cand:
            return cand
    return r


def _adamw(parts, w, m, v, name):
    r, c = w.shape
    tr = _adam_rows(r)
    n_parts = parts.shape[0]

    def body(p_ref, w_ref, m_ref, v_ref, g_ref, d_ref, nm_ref, nv_ref):
        g = p_ref[0].astype(F32)
        for k in range(1, n_parts):
            g = g + p_ref[k].astype(F32)
        m_new = ADAM_B1 * m_ref[...] + (1.0 - ADAM_B1) * g
        v_new = ADAM_B2 * v_ref[...] + (1.0 - ADAM_B2) * (g * g)
        m_hat = m_new / (1.0 - ADAM_B1 ** ADAM_STEP)
        v_hat = v_new / (1.0 - ADAM_B2 ** ADAM_STEP)
        g_ref[...] = g
        d_ref[...] = -ADAM_LR * (m_hat / (jnp.sqrt(v_hat) + ADAM_EPS) + ADAM_WD * w_ref[...])
        nm_ref[...] = m_new
        nv_ref[...] = v_new

    blk = pl.BlockSpec((tr, c), lambda i: (i, 0))
    shp = jax.ShapeDtypeStruct((r, c), F32)
    return pl.pallas_call(
        body, grid=(r // tr,), in_specs=[pl.BlockSpec((n_parts, tr, c), lambda i: (0, i, 0)), blk, blk, blk],
        out_specs=[blk] * 4, out_shape=[shp] * 4, compiler_params=_cparams("parallel"), name=name)(parts, w, m, v)


def _rows128(a):
    return a.reshape(-1, 128)


def _pad_rows(a, rows):
    return jnp.pad(a, ((0, rows - a.shape[0]), (0, 0)))


def _lane16(a):
    return jnp.pad(a.reshape(1, -1), ((0, 0), (0, 128 - a.size)))


SHARD_PACK_ROWS = 544
REPL_PACK_ROWS = 72


def _pack_shard(meta, lru_conv_w, odd_norm, ssd_conv_w, lru_w_a, lru_w_x):
    parts = [meta.reshape(16, 128), lru_conv_w.reshape(4, 128), odd_norm.reshape(1, 128), _rows128(ssd_conv_w.reshape(4, 192)),
             _rows128(lru_w_a.reshape(4, 32, 256)), _rows128(lru_w_x.reshape(4, 32, 256))]
    return _pad_rows(jnp.concatenate(parts, axis=0), SHARD_PACK_ROWS)


def _unpack_shard(p):
    return (p[0:16], p[16:20].reshape(1, 4, 128), p[20:21], p[21:27].reshape(1, 4, 192),
            p[27:283].reshape(1, 4, 32, 256), p[283:539].reshape(1, 4, 32, 256))


def _pack_repl(even_norm, lru_conv_b, lru_b_a, lru_b_x, lru_lambda, ssd_norm, final_norm, ssd_conv_b, dt_bias, a_log, ssd_d):
    parts = [_rows128(v) for v in (even_norm, lru_conv_b, lru_b_a, lru_b_x, lru_lambda, ssd_norm, final_norm, ssd_conv_b)]
    parts += [_lane16(dt_bias), _lane16(a_log), _lane16(ssd_d)]
    return _pad_rows(jnp.concatenate(parts, axis=0), REPL_PACK_ROWS)


def _unpack_repl(p):
    vec = lambda i: p[8 * i:8 * i + 8].reshape(1, 1024)
    return (vec(0), vec(1), vec(2), vec(3), vec(4), vec(5), p[48:56].reshape(1024), p[56:68].reshape(1, 1536),
            p[68:69, :16], p[69:70, :16], p[70:71, :16])


def _local_step(x, tgt, meta, even_norm, w_in_e, lru_conv_w, lru_conv_b, lru_w_a, lru_b_a, lru_w_x, lru_b_x, lru_lambda,
                ssd_conv_w, ssd_conv_b, dt_bias, a_log, ssd_d, ssd_norm, w_out_e, odd_norm, w_in_o, w_out_o, final_norm):
    bsz, seq, d = x.shape
    lp = PAD + N_META + seq
    t = bsz * lp
    h0 = jnp.concatenate([jnp.zeros((bsz, PAD, d), F32), jnp.broadcast_to(meta[None], (bsz, N_META, d)), x], axis=1).reshape(t, d)

    u0 = _norm_fwd(h0, even_norm, "norm0_fwd")
    proj = _mm_nn(u0, w_in_e, F32, "even_in")
    lru = (lru_conv_w, lru_conv_b, lru_w_a, lru_b_a, lru_w_x, lru_b_x, lru_lambda)
    ya = _lru_fwd(proj, *lru, bsz, lp)
    xbc = _ssd_pre_fwd(proj, ssd_conv_w, ssd_conv_b, bsz, lp)
    y = _ssd_fwd(xbc, proj, dt_bias, a_log, ssd_d, bsz, lp)
    yb = _gnorm_fwd(y, proj, ssd_norm)
    mixed = jnp.concatenate([ya, yb], axis=1)
    h1 = _mm_nn(mixed, w_out_e, F32, "even_out", res=h0)
    u2 = _norm_fwd(h1, odd_norm, "norm1_fwd")
    qkvg = _mm_nn(u2, w_in_o, F32, "odd_in")
    o, og = _sb_fwd(qkvg, bsz, lp)
    h2 = _mm_nn(og, w_out_o, F32, "odd_out", res=h1)
    loss, dh2, d_final_norm = _final_loss(h2, final_norm, tgt, lp)

    dog = _mm_nn(dh2, w_out_o.T, F32, "odd_out_dx")
    d_w_out_o = _mm_tn(og, dh2, "odd_out_dw")
    dqkvg = jnp.concatenate(_sb_bwd(qkvg, o, dog, bsz, lp), axis=1)
    du2 = _mm_nn(dqkvg, w_in_o.T, F32, "odd_in_dx")
    d_w_in_o = _mm_tn(u2, dqkvg, "odd_in_dw")
    dh1, d_odd_norm = _norm_bwd(h1, odd_norm, du2, dh2, "norm1_bwd")

    dmixed = _mm_nn(dh1, w_out_e.T, F32, "even_out_dx")
    d_w_out_e = _mm_tn(mixed, dh1, "even_out_dw")
    dlx, dgate, d_lru_conv_w, d_lru_conv_b, d_w_a, d_b_a, d_w_x, d_b_x, d_lambda = _lru_bwd(proj, dmixed, *lru, bsz, lp)
    dy, dz, d_ssd_norm = _gnorm_bwd(y, proj, ssd_norm, dmixed)
    dxs, dbm, dcm, ddt, d_dt_bias, d_a_log, d_ssd_d = _ssd_bwd(xbc, proj, dt_bias, a_log, ssd_d, dy, bsz, lp)
    dxbc, d_ssd_conv_w, d_ssd_conv_b = _ssd_pre_bwd(proj, jnp.concatenate([dxs, dbm, dcm], axis=1), ssd_conv_w, ssd_conv_b, bsz, lp)
    ddt = (ddt[:, :DT_L] + ddt[:, DT_L:]).astype(BF16)
    dproj = jnp.concatenate([dlx, dgate, dz, dxbc, ddt, jnp.zeros((t, EVEN_NP - 4608 - DT_L), BF16)], axis=1)
    du0 = _mm_nn(dproj, w_in_e.T, F32, "even_in_dx")
    d_w_in_e = _mm_tn(u0, dproj, "even_in_dw")
    dh0, d_even_norm = _norm_bwd(h0, even_norm, du0, dh1, "norm0_bwd")
    grad_x = dh0.reshape(bsz, lp, d)[:, PAD + N_META:]
    d_meta = _meta_grad(dh0, bsz, lp)
    heads = lambda p: (p[0:1] + p[8:9])[:, :SSD_HEADS]
    grads = dict(meta=d_meta, even_norm=d_even_norm, even_w_in=d_w_in_e[:, :EVEN_IN], lru_conv_w=d_lru_conv_w,
                 lru_conv_b=d_lru_conv_b, lru_w_a=d_w_a, lru_b_a=d_b_a, lru_w_x=d_w_x, lru_b_x=d_b_x, lru_lambda=d_lambda,
                 ssd_conv_w=d_ssd_conv_w, ssd_conv_b=d_ssd_conv_b, ssd_dt_bias=heads(d_dt_bias), ssd_a_log=heads(d_a_log),
                 ssd_d=heads(d_ssd_d), ssd_norm=d_ssd_norm, even_w_out=d_w_out_e, odd_norm=d_odd_norm, odd_w_in=d_w_in_o,
                 odd_w_out=d_w_out_o, final_norm=d_final_norm)
    return loss[0, 0], grad_x, grads


WEIGHTS = ['meta', 'even_norm', 'even_w_in', 'lru_conv_w', 'lru_conv_b', 'lru_w_a', 'lru_b_a', 'lru_w_x', 'lru_b_x', 'lru_lambda',
           'ssd_conv_w', 'ssd_conv_b', 'ssd_dt_bias', 'ssd_a_log', 'ssd_d', 'ssd_norm', 'even_w_out', 'odd_norm', 'odd_w_in',
           'odd_w_out', 'final_norm']


def _blocks_of(a, axis):
    shp = a.shape
    a = a.reshape(shp[:axis] + (N_DEV, shp[axis] // N_DEV) + shp[axis + 1:])
    return jnp.moveaxis(a, axis, 0)


def _unblock(a, axis):
    a = jnp.moveaxis(a, 0, axis)
    shp = a.shape
    return a.reshape(shp[:axis] + (shp[axis] * shp[axis + 1],) + shp[axis + 2:])


def kernel(x, meta, even_norm, even_w_in, lru_conv_w, lru_conv_b, lru_w_a, lru_b_a, lru_w_x, lru_b_x, lru_lambda, ssd_conv_w, ssd_conv_b, ssd_dt_bias, ssd_a_log, ssd_d, ssd_norm, even_w_out, odd_norm, odd_w_in, odd_w_out, final_norm, loss_target, m_meta, m_even_norm, m_even_w_in, m_lru_conv_w, m_lru_conv_b, m_lru_w_a, m_lru_b_a, m_lru_w_x, m_lru_b_x, m_lru_lambda, m_ssd_conv_w, m_ssd_conv_b, m_ssd_dt_bias, m_ssd_a_log, m_ssd_d, m_ssd_norm, m_even_w_out, m_odd_norm, m_odd_w_in, m_odd_w_out, m_final_norm, v_meta, v_even_norm, v_even_w_in, v_lru_conv_w, v_lru_conv_b, v_lru_w_a, v_lru_b_a, v_lru_w_x, v_lru_b_x, v_lru_lambda, v_ssd_conv_w, v_ssd_conv_b, v_ssd_dt_bias, v_ssd_a_log, v_ssd_d, v_ssd_norm, v_even_w_out, v_odd_norm, v_odd_w_in, v_odd_w_out, v_final_norm):
    w = dict(meta=meta, even_norm=even_norm, even_w_in=even_w_in, lru_conv_w=lru_conv_w, lru_conv_b=lru_conv_b, lru_w_a=lru_w_a,
             lru_b_a=lru_b_a, lru_w_x=lru_w_x, lru_b_x=lru_b_x, lru_lambda=lru_lambda, ssd_conv_w=ssd_conv_w,
             ssd_conv_b=ssd_conv_b, ssd_dt_bias=ssd_dt_bias, ssd_a_log=ssd_a_log, ssd_d=ssd_d, ssd_norm=ssd_norm,
             even_w_out=even_w_out, odd_norm=odd_norm, odd_w_in=odd_w_in, odd_w_out=odd_w_out, final_norm=final_norm)
    m = dict(meta=m_meta, even_norm=m_even_norm, even_w_in=m_even_w_in, lru_conv_w=m_lru_conv_w, lru_conv_b=m_lru_conv_b,
             lru_w_a=m_lru_w_a, lru_b_a=m_lru_b_a, lru_w_x=m_lru_w_x, lru_b_x=m_lru_b_x, lru_lambda=m_lru_lambda,
             ssd_conv_w=m_ssd_conv_w, ssd_conv_b=m_ssd_conv_b, ssd_dt_bias=m_ssd_dt_bias, ssd_a_log=m_ssd_a_log, ssd_d=m_ssd_d,
             ssd_norm=m_ssd_norm, even_w_out=m_even_w_out, odd_norm=m_odd_norm, odd_w_in=m_odd_w_in, odd_w_out=m_odd_w_out,
             final_norm=m_final_norm)
    v = dict(meta=v_meta, even_norm=v_even_norm, even_w_in=v_even_w_in, lru_conv_w=v_lru_conv_w, lru_conv_b=v_lru_conv_b,
             lru_w_a=v_lru_w_a, lru_b_a=v_lru_b_a, lru_w_x=v_lru_w_x, lru_b_x=v_lru_b_x, lru_lambda=v_lru_lambda,
             ssd_conv_w=v_ssd_conv_w, ssd_conv_b=v_ssd_conv_b, ssd_dt_bias=v_ssd_dt_bias, ssd_a_log=v_ssd_a_log, ssd_d=v_ssd_d,
             ssd_norm=v_ssd_norm, even_w_out=v_even_w_out, odd_norm=v_odd_norm, odd_w_in=v_odd_w_in, odd_w_out=v_odd_w_out,
             final_norm=v_final_norm)
    shard_names = ('meta', 'lru_conv_w', 'odd_norm', 'ssd_conv_w', 'lru_w_a', 'lru_w_x')
    repl_names = ('even_norm', 'lru_conv_b', 'lru_b_a', 'lru_b_x', 'lru_lambda', 'ssd_norm', 'final_norm', 'ssd_conv_b',
                  'ssd_dt_bias', 'ssd_a_log', 'ssd_d')
    big_names = ('even_w_in', 'even_w_out', 'odd_w_in', 'odd_w_out')

    gates = jnp.concatenate([lru_w_a.reshape(128, 256), lru_w_x.reshape(128, 256)], axis=0).astype(BF16)
    small = _pack_shard(*[w[k] for k in shard_names])
    g_in_e, g_out_e, g_in_o, g_out_o, g_gates, g_small = _gather_all(
        [even_w_in[0].astype(BF16), even_w_out[0].astype(BF16), odd_w_in[0].astype(BF16), odd_w_out[0].astype(BF16), gates, small],
        "gather_weights")
    w_in_e = jnp.pad(_unblock(g_in_e, 1), ((0, 0), (0, EVEN_NP - EVEN_IN)))
    w_out_e = g_out_e.reshape(2048, 1024)
    w_in_o = _unblock(g_in_o, 1)
    w_out_o = g_out_o.reshape(1024, 1024)
    gates_full = jnp.moveaxis(g_gates.reshape(N_DEV, 2, 4, 32, 256), 0, 2).reshape(2, 4, 256, 256)
    f_meta = _unblock(g_small[:, 0:16], 1)
    f_lru_conv_w = _unblock(g_small[:, 16:20], 1)
    f_odd_norm = _unblock(g_small[:, 20:21], 1)
    f_ssd_conv_w = _unblock(g_small[:, 21:27].reshape(N_DEV, 4, 192), 1)

    loss, grad_x, g = _local_step(
        x, loss_target, f_meta, even_norm, w_in_e, f_lru_conv_w, lru_conv_b, gates_full[0], lru_b_a, gates_full[1], lru_b_x,
        lru_lambda, f_ssd_conv_w, ssd_conv_b, _lane16(ssd_dt_bias), _lane16(ssd_a_log), _lane16(ssd_d), ssd_norm, w_out_e,
        f_odd_norm, w_in_o, w_out_o, final_norm.reshape(1, -1))
    loss = lax.psum(loss, ("x", "y", "c"))

    s_small = jnp.stack([_pack_shard(g['meta'][:, 128 * p:128 * (p + 1)], g['lru_conv_w'][:, 128 * p:128 * (p + 1)],
                                     g['odd_norm'][:, 128 * p:128 * (p + 1)], g['ssd_conv_w'][:, 192 * p:192 * (p + 1)],
                                     g['lru_w_a'][:, 32 * p:32 * (p + 1)], g['lru_w_x'][:, 32 * p:32 * (p + 1)])
                         for p in range(N_DEV)])
    r_pack = _pack_repl(*[g[k] for k in repl_names])
    by_owner = [_blocks_of(g['even_w_in'], 1).astype(BF16), g['even_w_out'].reshape(N_DEV, 256, 1024).astype(BF16),
                _blocks_of(g['odd_w_in'], 1).astype(BF16), g['odd_w_out'].reshape(N_DEV, 128, 1024).astype(BF16), s_small]
    got = _swap_in_chip(by_owner, "reduce_in_chip")
    my_c = lax.axis_index("c")
    kept = [lax.dynamic_index_in_dim(a.reshape((4, 2) + a.shape[1:]), my_c, axis=1, keepdims=False) for a in by_owner]
    chip_sums = [_add_pair(a, b, "chip_sum_%d" % i) for i, (a, b) in enumerate(zip(kept, got))]
    p_in_e, p_out_e, p_in_o, p_out_o, p_small, p_repl = _swap_chips(chip_sums, [r_pack], "reduce_across_chips")

    res = {}
    for k, parts in zip(big_names, (p_in_e, p_out_e, p_in_o, p_out_o)):
        outs = _adamw(parts, w[k][0], m[k][0], v[k][0], "adamw_" + k)
        res[k] = [o[None] for o in outs]
    outs = _adamw(p_small, small, _pack_shard(*[m[k] for k in shard_names]), _pack_shard(*[v[k] for k in shard_names]), "adamw_sharded")
    unpacked = [_unpack_shard(o) for o in outs]
    for i, k in enumerate(shard_names):
        res[k] = [u[i] for u in unpacked]
    outs = _adamw(p_repl, _pack_repl(*[w[k] for k in repl_names]), _pack_repl(*[m[k] for k in repl_names]),
                  _pack_repl(*[v[k] for k in repl_names]), "adamw_replicated")
    unpacked = [_unpack_repl(o) for o in outs]
    for i, k in enumerate(repl_names):
        res[k] = [u[i] for u in unpacked]
    return (loss, grad_x, *[res[k][0] for k in WEIGHTS], *[res[k][1] for k in WEIGHTS], *[res[k][2] for k in WEIGHTS],
            *[res[k][3] for k in WEIGHTS])
```

```python
import functools

import jax
import jax.numpy as jnp
from jax import lax
from jax.experimental import pallas as pl
from jax.experimental.pallas import tpu as pltpu

F32 = jnp.float32
BF16 = jnp.bfloat16

D_MODEL = 1024
N_META = 16
PAD = 112
EPS = 1e-6
CONV_W = 4
LRU_BLOCKS = 4
LRU_BLOCK = 256
RG_LRU_C = 8.0
SSD_HEADS = 16
SSD_P = 64
SSD_N = 128
SSD_Q = 64
SSD_GROUPS = 2
SSD_GW = 512
SSD_CONV_DIM = 1536
SB_HEADS = 16
SB_D = 64
SB_T = 128
EVEN_IN = 4624
EVEN_NP = 4864
DT_W = 256
ODD_IN = 4096
N_DEV = 8

ADAM_LR = 0.001
ADAM_B1 = 0.9
ADAM_B2 = 0.999
ADAM_EPS = 1e-08
ADAM_WD = 0.01
ADAM_STEP = 10

VMEM_LIMIT = 56 * 1024 * 1024


def _cparams(*sem):
    return pltpu.CompilerParams(dimension_semantics=sem, vmem_limit_bytes=VMEM_LIMIT)


def _row_tile(t):
    for c in (512, 256, 128):
        if t % c == 0:
            return c
    raise ValueError(t)


def _col_tile(n):
    for c in (512, 256, 128):
        if n % c == 0:
            return c
    raise ValueError(n)


def _dot(a, b):
    return jnp.dot(a, b, preferred_element_type=F32)


def _dot_nt(a, b):
    return lax.dot_general(a, b, (((1,), (1,)), ((), ())), preferred_element_type=F32)


def _dot_tn(a, b):
    return lax.dot_general(a, b, (((0,), (0,)), ((), ())), preferred_element_type=F32)


def _split3(a):
    a1 = a.astype(BF16)
    r1 = a - a1.astype(F32)
    a2 = r1.astype(BF16)
    a3 = (r1 - a2.astype(F32)).astype(BF16)
    return a1, a2, a3


def _xdot_r(a, m01):
    a1, a2, a3 = _split3(a)
    return _dot(a1, m01) + _dot(a2, m01) + _dot(a3, m01)


def _xdot_l(m01, a):
    a1, a2, a3 = _split3(a)
    return _dot(m01, a1) + _dot(m01, a2) + _dot(m01, a3)


def _xdot_tn_l(m01, a):
    a1, a2, a3 = _split3(a)
    return _dot_tn(m01, a1) + _dot_tn(m01, a2) + _dot_tn(m01, a3)


def _sigmoid(x):
    return 0.5 * jnp.tanh(0.5 * x) + 0.5


def _softplus(x):
    return jnp.maximum(x, 0.0) + jnp.log(1.0 + jnp.exp(-jnp.abs(x)))


def _iota(shape, dim):
    return lax.broadcasted_iota(jnp.int32, shape, dim)


def _block_scan(a_ref, b_ref, o_ref, reverse=False):
    n, w = a_ref.shape
    nb = n // 8
    unroll = 4 if nb % 4 == 0 else 1
    row = _iota((8, w), 0)

    def block(blk, carry):
        rows = pl.ds(pl.multiple_of(blk * 8, 8), 8)
        a, b = a_ref[rows, :], b_ref[rows, :]
        for k in (1, 2, 4):
            keep = (row < 8 - k) if reverse else (row >= k)
            shift = 8 - k if reverse else k
            a_sh = jnp.where(keep, pltpu.roll(a, shift, 0), 1.0)
            b_sh = jnp.where(keep, pltpu.roll(b, shift, 0), 0.0)
            b = a * b_sh + b
            a = a * a_sh
        o = a * carry + b
        o_ref[rows, :] = o
        return o[0:1, :] if reverse else o[7:8, :]

    def trip(i, carry):
        for u in range(unroll):
            j = i * unroll + u
            carry = block(nb - 1 - j if reverse else j, carry)
        return carry

    lax.fori_loop(0, nb // unroll, trip, jnp.zeros((1, w), F32))


def _chunks(n, c):
    return [(s, min(c, n - s)) for s in range(0, n, c)]


def _mm_nn(a, b, out_dtype, name, res=None):
    m, k = a.shape
    _, n = b.shape
    tm = _row_tile(m)

    def body(*refs):
        if res is None:
            a_ref, b_ref, o_ref = refs
        else:
            a_ref, b_ref, r_ref, o_ref = refs
        av = a_ref[...].astype(BF16)
        for s, w in _chunks(n, 512):
            acc = _dot(av, b_ref[:, s:s + w])
            if res is not None:
                acc = acc + r_ref[:, s:s + w]
            o_ref[:, s:s + w] = acc.astype(out_dtype)

    in_specs = [pl.BlockSpec((tm, k), lambda i: (i, 0)), pl.BlockSpec((k, n), lambda i: (0, 0))]
    args = [a, b]
    if res is not None:
        in_specs.append(pl.BlockSpec((tm, n), lambda i: (i, 0)))
        args.append(res)
    return pl.pallas_call(
        body, grid=(m // tm,), in_specs=in_specs,
        out_specs=pl.BlockSpec((tm, n), lambda i: (i, 0)),
        out_shape=jax.ShapeDtypeStruct((m, n), out_dtype),
        compiler_params=_cparams("parallel"), name=name)(*args)


def _mm_nt(a, b, out_dtype, name):
    m, k = a.shape
    n, _ = b.shape
    tm, tn = _row_tile(m), _col_tile(n)

    def body(a_ref, b_ref, o_ref):
        o_ref[...] = _dot_nt(a_ref[...].astype(BF16), b_ref[...].astype(BF16)).astype(out_dtype)

    return pl.pallas_call(
        body, grid=(m // tm, n // tn),
        in_specs=[pl.BlockSpec((tm, k), lambda i, j: (i, 0)), pl.BlockSpec((tn, k), lambda i, j: (j, 0))],
        out_specs=pl.BlockSpec((tm, tn), lambda i, j: (i, j)),
        out_shape=jax.ShapeDtypeStruct((m, n), out_dtype),
        compiler_params=_cparams("parallel", "parallel"), name=name)(a, b)


def _mm_tn(a, b, name):
    t, m = a.shape
    _, n = b.shape
    tk = _row_tile(t)
    halves = 2 if (m * n * 4 > 8 * 1024 * 1024 and n % 256 == 0) else 1
    tn = n // halves

    def body(a_ref, b_ref, o_ref):
        @pl.when(pl.program_id(1) == 0)
        def _():
            o_ref[...] = jnp.zeros_like(o_ref)
        at = a_ref[...].astype(BF16).T
        for s, w in _chunks(tn, 512):
            o_ref[:, s:s + w] += _dot(at, b_ref[:, s:s + w].astype(BF16))

    return pl.pallas_call(
        body, grid=(halves, t // tk),
        in_specs=[pl.BlockSpec((tk, m), lambda j, s: (s, 0)), pl.BlockSpec((tk, tn), lambda j, s: (s, j))],
        out_specs=pl.BlockSpec((m, tn), lambda j, s: (0, j)),
        out_shape=jax.ShapeDtypeStruct((m, n), F32),
        compiler_params=_cparams("parallel", "arbitrary"), name=name)(a, b)


def _norm_fwd(h, w, name):
    t, d = h.shape
    tm = _row_tile(t)

    def body(h_ref, w_ref, u_ref):
        x = h_ref[...]
        r = lax.rsqrt(jnp.mean(x * x, axis=-1, keepdims=True) + EPS)
        u_ref[...] = (x * r * w_ref[...]).astype(BF16)

    return pl.pallas_call(
        body, grid=(t // tm,),
        in_specs=[pl.BlockSpec((tm, d), lambda i: (i, 0)), pl.BlockSpec((1, d), lambda i: (0, 0))],
        out_specs=pl.BlockSpec((tm, d), lambda i: (i, 0)),
        out_shape=jax.ShapeDtypeStruct((t, d), BF16),
        compiler_params=_cparams("parallel"), name=name)(h, w)


def _norm_bwd(h, w, du, dres, name):
    t, d = h.shape
    tm = _row_tile(t)

    def body(h_ref, w_ref, du_ref, dr_ref, dh_ref, dw_ref):
        @pl.when(pl.program_id(0) == 0)
        def _():
            dw_ref[...] = jnp.zeros_like(dw_ref)
        x = h_ref[...]
        r = lax.rsqrt(jnp.mean(x * x, axis=-1, keepdims=True) + EPS)
        xh = x * r
        du_ = du_ref[...]
        g = du_ * w_ref[...]
        dh_ref[...] = dr_ref[...] + r * (g - xh * jnp.mean(g * xh, axis=-1, keepdims=True))
        dw_ref[...] += jnp.sum(du_ * xh, axis=0, keepdims=True)

    return pl.pallas_call(
        body, grid=(t // tm,),
        in_specs=[pl.BlockSpec((tm, d), lambda i: (i, 0)), pl.BlockSpec((1, d), lambda i: (0, 0)),
                  pl.BlockSpec((tm, d), lambda i: (i, 0)), pl.BlockSpec((tm, d), lambda i: (i, 0))],
        out_specs=[pl.BlockSpec((tm, d), lambda i: (i, 0)), pl.BlockSpec((1, d), lambda i: (0, 0))],
        out_shape=[jax.ShapeDtypeStruct((t, d), F32), jax.ShapeDtypeStruct((1, d), F32)],
        compiler_params=_cparams("arbitrary"), name=name)(h, w, du, dres)


def _final_loss(h2, w, tgt, lp):
    t, d = h2.shape
    bsz, seq, _ = tgt.shape
    nblk = lp // SB_T

    def body(h_ref, w_ref, t_ref, loss_ref, dh_ref, dw_ref):
        b, i = pl.program_id(0), pl.program_id(1)

        @pl.when((b == 0) & (i == 0))
        def _():
            loss_ref[...] = jnp.zeros_like(loss_ref)
            dw_ref[...] = jnp.zeros_like(dw_ref)

        @pl.when(i == 0)
        def _():
            dh_ref[...] = jnp.zeros_like(dh_ref)

        @pl.when(i > 0)
        def _():
            x = h_ref[...]
            r = lax.rsqrt(jnp.mean(x * x, axis=-1, keepdims=True) + EPS)
            xh = x * r
            wv = w_ref[...]
            diff = xh * wv - t_ref[0]
            loss_ref[...] += 0.5 * jnp.sum(jnp.mean(diff * diff, axis=-1, keepdims=True), axis=0, keepdims=True)
            dy = diff * (1.0 / d)
            g = dy * wv
            dh_ref[...] = r * (g - xh * jnp.mean(g * xh, axis=-1, keepdims=True))
            dw_ref[...] += jnp.sum(dy * xh, axis=0, keepdims=True)

    return pl.pallas_call(
        body, grid=(bsz, nblk),
        in_specs=[pl.BlockSpec((SB_T, d), lambda b, i: (b * nblk + i, 0)), pl.BlockSpec((1, d), lambda b, i: (0, 0)),
                  pl.BlockSpec((1, SB_T, d), lambda b, i: (b, jnp.maximum(i - 1, 0), 0))],
        out_specs=[pl.BlockSpec((1, 128), lambda b, i: (0, 0)), pl.BlockSpec((SB_T, d), lambda b, i: (b * nblk + i, 0)),
                   pl.BlockSpec((1, d), lambda b, i: (0, 0))],
        out_shape=[jax.ShapeDtypeStruct((1, 128), F32), jax.ShapeDtypeStruct((t, d), F32), jax.ShapeDtypeStruct((1, d), F32)],
        compiler_params=_cparams("arbitrary", "arbitrary"), name="final_loss")(h2, w, tgt)


def _shift_down(x, j):
    return x if j == 0 else pltpu.roll(x, j, 0)


def _shift_up(x, j):
    return x if j == 0 else pltpu.roll(x, x.shape[0] - j, 0)


def _conv(x, cw, cb):
    out = cb + cw[CONV_W - 1:CONV_W, :] * x
    for j in range(1, CONV_W):
        out = out + cw[CONV_W - 1 - j:CONV_W - j, :] * _shift_down(x, j)
    return out


def _conv_bwd_x(dy, cw):
    out = cw[CONV_W - 1:CONV_W, :] * dy
    for j in range(1, CONV_W):
        out = out + cw[CONV_W - 1 - j:CONV_W - j, :] * _shift_up(dy, j)
    return out


def _conv_bwd_w(dcw_ref, dy, x):
    for j in range(CONV_W):
        k = CONV_W - 1 - j
        dcw_ref[k:k + 1, :] += jnp.sum(dy * _shift_down(x, j), axis=0, keepdims=True)


def _lru_forward_block(x, cw, cb, wa, ba, wx, bx, lam, a_ref, b_ref, h_ref):
    lp = x.shape[0]
    lx = _conv(x, cw, cb)
    lxb = lx.astype(BF16)
    r = _sigmoid(_dot(lxb, wa) + ba)
    i = _sigmoid(_dot(lxb, wx) + bx)
    spl = _softplus(-lam)
    log_a = (-RG_LRU_C) * r * spl
    a = jnp.exp(log_a)
    mult = jnp.sqrt(-jnp.tanh(log_a) * (1.0 + a * a))
    valid = _iota((lp, 1), 0) >= PAD
    a_ref[...] = a
    b_ref[...] = jnp.where(valid, mult * i * lx, 0.0)
    _block_scan(a_ref, b_ref, h_ref)
    return lx, lxb, r, i, spl, a, mult, valid


def _lru_specs(lp):
    w = LRU_BLOCK
    return [
        pl.BlockSpec((lp, w), lambda g, b: (b, g)),
        pl.BlockSpec((lp, w), lambda g, b: (b, LRU_BLOCKS + g)),
        pl.BlockSpec((CONV_W, w), lambda g, b: (0, g)),
        pl.BlockSpec((1, w), lambda g, b: (0, g)),
        pl.BlockSpec((1, w, w), lambda g, b: (g, 0, 0)),
        pl.BlockSpec((1, w), lambda g, b: (0, g)),
        pl.BlockSpec((1, w, w), lambda g, b: (g, 0, 0)),
        pl.BlockSpec((1, w), lambda g, b: (0, g)),
        pl.BlockSpec((1, w), lambda g, b: (0, g)),
    ]


def _lru_fwd(proj, cw, cb, wa, ba, wx, bx, lam, bsz, lp):
    w = LRU_BLOCK

    def body(x_ref, g_ref, cw_ref, cb_ref, wa_ref, ba_ref, wx_ref, bx_ref, lam_ref, y_ref, a_s, b_s, h_s):
        _lru_forward_block(x_ref[...], cw_ref[...], cb_ref[...], wa_ref[0], ba_ref[...], wx_ref[0], bx_ref[...],
                           lam_ref[...], a_s, b_s, h_s)
        gate = g_ref[...]
        y_ref[...] = (h_s[...] * gate * _sigmoid(gate)).astype(BF16)

    return pl.pallas_call(
        body, grid=(LRU_BLOCKS, bsz), in_specs=_lru_specs(lp),
        out_specs=pl.BlockSpec((lp, w), lambda g, b: (b, g)),
        out_shape=jax.ShapeDtypeStruct((bsz * lp, 2 * LRU_BLOCKS * w), BF16),
        scratch_shapes=[pltpu.VMEM((lp, w), F32)] * 3,
        compiler_params=_cparams("parallel", "arbitrary"), name="lru_fwd")(proj, proj, cw, cb, wa, ba, wx, bx, lam)


def _lru_bwd(proj, dmixed, cw, cb, wa, ba, wx, bx, lam, bsz, lp):
    w = LRU_BLOCK

    def body(x_ref, g_ref, cw_ref, cb_ref, wa_ref, ba_ref, wx_ref, bx_ref, lam_ref, dy_ref,
             dx_ref, dg_ref, dcw_ref, dcb_ref, dwa_ref, dba_ref, dwx_ref, dbx_ref, dlam_ref, a_s, b_s, h_s, dh_s):
        @pl.when(pl.program_id(1) == 0)
        def _():
            for ref in (dcw_ref, dcb_ref, dwa_ref, dba_ref, dwx_ref, dbx_ref, dlam_ref):
                ref[...] = jnp.zeros_like(ref)

        x = x_ref[...]
        cwv = cw_ref[...]
        wav, wxv, lam_ = wa_ref[0], wx_ref[0], lam_ref[...]
        lx, lxb, r, i, spl, a, mult, valid = _lru_forward_block(
            x, cwv, cb_ref[...], wav, ba_ref[...], wxv, bx_ref[...], lam_, a_s, b_s, h_s)
        gate = g_ref[...]
        sg = _sigmoid(gate)
        dy = dy_ref[...]
        h = h_s[...]
        dg_ref[...] = (dy * h * sg * (1.0 + gate * (1.0 - sg))).astype(BF16)
        b_s[...] = dy * gate * sg
        a_s[...] = jnp.where(_iota((lp, 1), 0) < lp - 1, _shift_up(a, 1), 0.0)
        _block_scan(a_s, b_s, dh_s, reverse=True)
        dh = dh_s[...]
        hprev = jnp.where(_iota((lp, 1), 0) >= 1, _shift_down(h, 1), 0.0)
        db = jnp.where(valid, dh, 0.0)
        dmult = db * i * lx
        di = db * mult * lx
        dlx = db * mult * i
        inv_mult = jnp.where(mult > 0.0, 1.0 / mult, 0.0)
        dlog_a = dh * hprev * a - dmult * (a * a) * inv_mult
        drp = dlog_a * ((-RG_LRU_C) * spl) * r * (1.0 - r)
        dip = di * i * (1.0 - i)
        dspl = jnp.sum(dlog_a * ((-RG_LRU_C) * r), axis=0, keepdims=True)
        dlam_ref[...] += dspl * (-_sigmoid(-lam_))
        dba_ref[...] += jnp.sum(drp, axis=0, keepdims=True)
        dbx_ref[...] += jnp.sum(dip, axis=0, keepdims=True)
        drpb, dipb = drp.astype(BF16), dip.astype(BF16)
        dwa_ref[0] += _dot_tn(lxb, drpb)
        dwx_ref[0] += _dot_tn(lxb, dipb)
        dlx = dlx + _dot_nt(drpb, wav) + _dot_nt(dipb, wxv)
        dcb_ref[...] += jnp.sum(dlx, axis=0, keepdims=True)
        _conv_bwd_w(dcw_ref, dlx, x)
        dx_ref[...] = _conv_bwd_x(dlx, cwv).astype(BF16)

    t = bsz * lp
    vec = pl.BlockSpec((1, w), lambda g, b: (0, g))
    mat = pl.BlockSpec((1, w, w), lambda g, b: (g, 0, 0))
    act = pl.BlockSpec((lp, w), lambda g, b: (b, g))
    return pl.pallas_call(
        body, grid=(LRU_BLOCKS, bsz), in_specs=_lru_specs(lp) + [act],
        out_specs=[act, act, pl.BlockSpec((CONV_W, w), lambda g, b: (0, g)), vec, mat, vec, mat, vec, vec],
        out_shape=[jax.ShapeDtypeStruct((t, 1024), BF16), jax.ShapeDtypeStruct((t, 1024), BF16),
                   jax.ShapeDtypeStruct((CONV_W, 1024), F32), jax.ShapeDtypeStruct((1, 1024), F32),
                   jax.ShapeDtypeStruct((LRU_BLOCKS, w, w), F32), jax.ShapeDtypeStruct((1, 1024), F32),
                   jax.ShapeDtypeStruct((LRU_BLOCKS, w, w), F32), jax.ShapeDtypeStruct((1, 1024), F32),
                   jax.ShapeDtypeStruct((1, 1024), F32)],
        scratch_shapes=[pltpu.VMEM((lp, w), F32)] * 4,
        compiler_params=_cparams("parallel", "arbitrary"), name="lru_bwd")(
            proj, proj, cw, cb, wa, ba, wx, bx, lam, dmixed)


def _sb_masks():
    j = _iota((SB_T, 2 * SB_T), 0)
    s = _iota((SB_T, 2 * SB_T), 1)
    right = ((s < SB_T) & (j > s)) | (s >= SB_T)
    left = ((s < SB_T) & (j < s)) | (s >= SB_T)
    return jnp.where(right, -1.0, 0.0).astype(BF16), left.astype(BF16)


def _xdot2(a, m01):
    a1 = a.astype(BF16)
    a2 = (a - a1.astype(F32)).astype(BF16)
    return _dot(a1, m01) + _dot(a2, m01)


def _sb_query_block(lp):
    for tq in (512, 256, 128):
        if (lp - SB_T) % tq == 0:
            return tq, (lp - SB_T) // tq
    raise ValueError(lp)


def _sb_key_rows(kj):
    return pl.ds(kj * SB_T if isinstance(kj, int) else pl.multiple_of(kj * SB_T, SB_T), SB_T)


def _sb_valid(q0, tq, kj):
    t_pos = q0 + _iota((tq, SB_T), 0)
    s_pos = kj * SB_T + _iota((tq, SB_T), 1)
    return (s_pos < t_pos) & (s_pos >= PAD)


def _sb_by_head(ref, rows):
    t = ref[rows, :]
    head0 = _iota((SB_T, 128), 1) < SB_D
    return jnp.concatenate([jnp.where(head0, t, 0.0), jnp.where(head0, 0.0, t)], axis=0).astype(BF16)


def _sb_tile(qb, k_ref, q0, kj, carries, m_right_neg, masked):
    rows_k = _sb_key_rows(kj)
    z2 = _dot_nt(qb, _sb_by_head(k_ref, rows_k))
    valid = _sb_valid(q0, qb.shape[0], kj) if masked else None
    out = []
    for h in range(2):
        z = z2[:, h * SB_T:(h + 1) * SB_T]
        sp = _softplus(z)
        rs = _dot((jnp.where(valid, sp, 0.0) if masked else sp).astype(BF16), m_right_neg)
        lb = z - sp
        wgt = jnp.exp(lb + rs[:, :SB_T] + carries[h])
        if masked:
            wgt = jnp.where(valid, wgt, 0.0)
        out.append((lb, wgt, carries[h] + rs[:, SB_T:]))
    return rows_k, valid, out


def _tail(x, row0):
    return x if row0 == 0 else x[row0:]


def _merge(old, new_tail, row0):
    return new_tail if row0 == 0 else jnp.concatenate([old[:row0], new_tail], axis=0)


def _sb_sweep(step, c, blk, tq, first, leftwards):
    if first:
        return step(0, True, 0)(c)
    r = tq // SB_T
    u = 4 if r % 4 == 0 else 1
    lo = 1 + r * blk

    def diag(c):
        for d in (range(r - 1, -1, -1) if leftwards else range(r)):
            c = step(lo + d, True, SB_T * d)(c)
        return c

    def inner(c):
        def trip(i, c):
            for j in range(u):
                d = u * i + j
                c = step(lo - 1 - d if leftwards else 1 + d, False, 0)(c)
            return c
        return lax.fori_loop(0, (r // u) * blk, trip, c)

    if leftwards:
        return step(0, True, 0)(inner(diag(c)))
    return diag(inner(step(0, True, 0)(c)))


def _sb_specs(lp):
    nh = SB_HEADS // 2
    return [pl.BlockSpec((lp, 128), lambda b, p: (b, p)), pl.BlockSpec((lp, 128), lambda b, p: (b, nh + p)),
            pl.BlockSpec((lp, 128), lambda b, p: (b, 2 * nh + p)), pl.BlockSpec((lp, 128), lambda b, p: (b, 3 * nh + p))]


def _sb_fwd(qkvg, bsz, lp):
    tq, nb = _sb_query_block(lp)
    scale = SB_D ** -0.5

    def body(q_ref, k_ref, v_ref, g_ref, o_ref, og_ref):
        m_right_neg, _ = _sb_masks()

        def q_block(blk, rows, first):
            q0 = 0 if first else (1 + (rows // SB_T) * blk) * SB_T
            rows_q = pl.ds(q0 if first else pl.multiple_of(q0, SB_T), rows)
            qb = (q_ref[rows_q, :] * scale).astype(BF16)

            def step(kj, masked, row0):
                def run(c):
                    acc, car0, car1 = c
                    rows_k, _, ((_, w0, new0), (_, w1, new1)) = _sb_tile(
                        _tail(qb, row0), k_ref, q0 + row0, kj, (_tail(car0, row0), _tail(car1, row0)), m_right_neg, masked)
                    w2 = jnp.concatenate([w0.astype(BF16), w1.astype(BF16)], axis=1)
                    upd = _tail(acc, row0) + _dot(w2, _sb_by_head(v_ref, rows_k))
                    return _merge(acc, upd, row0), _merge(car0, new0, row0), _merge(car1, new1, row0)
                return run

            zero = jnp.zeros((rows, SB_T), F32)
            o_ref[rows_q, :] = _sb_sweep(step, (zero, zero, zero), blk, rows, first, True)[0]

        q_block(0, SB_T, True)

        def big_block(i, _):
            q_block(i, tq, False)
            return 0

        lax.fori_loop(0, nb, big_block, 0)
        gate = g_ref[...]
        og_ref[...] = (o_ref[...] * gate * _sigmoid(gate)).astype(BF16)

    t = bsz * lp
    blk = pl.BlockSpec((lp, 128), lambda b, p: (b, p))
    return pl.pallas_call(
        body, grid=(bsz, SB_HEADS // 2), in_specs=_sb_specs(lp), out_specs=[blk, blk],
        out_shape=[jax.ShapeDtypeStruct((t, 1024), F32), jax.ShapeDtypeStruct((t, 1024), BF16)],
        compiler_params=_cparams("parallel", "parallel"), name="sb_fwd")(qkvg, qkvg, qkvg, qkvg)


def _sb_bwd(qkvg, o, dog, bsz, lp):
    tq, nb = _sb_query_block(lp)
    nk = lp // SB_T
    scale = SB_D ** -0.5

    def body(q_ref, k_ref, v_ref, g_ref, o_ref, dog_ref, dq_ref, dk_ref, dv_ref, dg_ref, do_s, dk_s, dv_s, e_s, sig_s):
        m_right_neg, m_left = _sb_masks()
        gate = g_ref[...]
        sg = _sigmoid(gate)
        dog = dog_ref[...]
        do_s[...] = dog * gate * sg
        dg_ref[...] = (dog * o_ref[...] * sg * (1.0 + gate * (1.0 - sg))).astype(BF16)
        dk_s[...] = jnp.zeros_like(dk_s)
        dv_s[...] = jnp.zeros_like(dv_s)
        def q_block(blk, rows, first):
            q0 = 0 if first else (1 + (rows // SB_T) * blk) * SB_T
            rows_q = pl.ds(q0 if first else pl.multiple_of(q0, SB_T), rows)
            head0 = _iota((rows, 128), 1) < SB_D
            qf = q_ref[rows_q, :] * scale
            qb = qf.astype(BF16)
            q_h = (jnp.where(head0, qf, 0.0).astype(BF16), jnp.where(head0, 0.0, qf).astype(BF16))
            do_f = do_s[rows_q, :]
            do_b = do_f.astype(BF16)
            do_h = (jnp.where(head0, do_f, 0.0).astype(BF16), jnp.where(head0, 0.0, do_f).astype(BF16))

            def left(kj, masked, row0):
                def run(c):
                    rows_k, _, heads = _sb_tile(_tail(qb, row0), k_ref, q0 + row0, kj, (_tail(c[0], row0), _tail(c[1], row0)),
                                                m_right_neg, masked)
                    dw2 = _dot_nt(_tail(do_b, row0), _sb_by_head(v_ref, rows_k))
                    dv = None
                    for h, (lb, wgt, _) in enumerate(heads):
                        e_s[2 * kj + h, row0:rows, :] = wgt * dw2[:, h * SB_T:(h + 1) * SB_T]
                        sig_s[2 * kj + h, row0:rows, :] = jnp.exp(lb)
                        part = _dot_tn(wgt.astype(BF16), _tail(do_h[h], row0))
                        dv = part if dv is None else dv + part
                    dv_s[rows_k, :] += dv
                    return _merge(c[0], heads[0][2], row0), _merge(c[1], heads[1][2], row0)
                return run

            zero = jnp.zeros((rows, SB_T), F32)
            _sb_sweep(left, (zero, zero), blk, rows, first, True)

            def right(kj, masked, row0):
                def run(c):
                    dq, ecar = c[0], list(c[1:])
                    rows_k = _sb_key_rows(kj)
                    valid = _sb_valid(q0 + row0, rows - row0, kj) if masked else None
                    dzs, dk = [], None
                    for h in range(2):
                        e = e_s[2 * kj + h, row0:rows, :]
                        se = _dot(e.astype(BF16), m_left)
                        ec = _tail(ecar[h], row0)
                        dz = e - sig_s[2 * kj + h, row0:rows, :] * (e + se[:, :SB_T] + ec)
                        if masked:
                            dz = jnp.where(valid, dz, 0.0)
                        dz = dz.astype(BF16)
                        part = _dot_tn(dz, _tail(q_h[h], row0))
                        dk = part if dk is None else dk + part
                        dzs.append(dz)
                        ecar[h] = _merge(ecar[h], ec + se[:, SB_T:], row0)
                    dk_s[rows_k, :] += dk
                    upd = _tail(dq, row0) + _dot(jnp.concatenate(dzs, axis=1), _sb_by_head(k_ref, rows_k))
                    return _merge(dq, upd, row0), ecar[0], ecar[1]
                return run

            dq = _sb_sweep(right, (zero, zero, zero), blk, rows, first, False)[0]
            dq_ref[rows_q, :] = (dq * scale).astype(BF16)

        q_block(0, SB_T, True)

        def big_block(i, _):
            q_block(i, tq, False)
            return 0

        lax.fori_loop(0, nb, big_block, 0)
        dk_ref[...] = dk_s[...].astype(BF16)
        dv_ref[...] = dv_s[...].astype(BF16)

    t = bsz * lp
    blk = pl.BlockSpec((lp, 128), lambda b, p: (b, p))
    shp = jax.ShapeDtypeStruct((t, 1024), BF16)
    return pl.pallas_call(
        body, grid=(bsz, SB_HEADS // 2), in_specs=_sb_specs(lp) + [blk, blk], out_specs=[blk] * 4,
        out_shape=[shp] * 4,
        scratch_shapes=[pltpu.VMEM((lp, 128), F32)] * 3 + [pltpu.VMEM((2 * nk, tq, SB_T), F32)] * 2,
        compiler_params=_cparams("parallel", "parallel"), name="sb_bwd")(qkvg, qkvg, qkvg, qkvg, o, dog)


XBC_COL0 = 3072 // 256
DT_COL0 = 4608 // 128
DT_L = 128


def _ssd_pre_fwd(proj, cw, cb, bsz, lp):
    def body(x_ref, cw_ref, cb_ref, o_ref):
        pre = _conv(x_ref[...], cw_ref[...], cb_ref[...])
        o_ref[...] = pre * _sigmoid(pre)

    return pl.pallas_call(
        body, grid=(bsz, SSD_CONV_DIM // 256),
        in_specs=[pl.BlockSpec((lp, 256), lambda b, j: (b, XBC_COL0 + j)), pl.BlockSpec((CONV_W, 256), lambda b, j: (0, j)),
                  pl.BlockSpec((1, 256), lambda b, j: (0, j))],
        out_specs=pl.BlockSpec((lp, 256), lambda b, j: (b, j)),
        out_shape=jax.ShapeDtypeStruct((bsz * lp, SSD_CONV_DIM), F32),
        compiler_params=_cparams("parallel", "parallel"), name="ssd_pre_fwd")(proj, cw, cb)


def _ssd_pre_bwd(proj, dact, cw, cb, bsz, lp):
    def body(x_ref, d_ref, cw_ref, cb_ref, dx_ref, dcw_ref, dcb_ref):
        @pl.when(pl.program_id(1) == 0)
        def _():
            dcw_ref[...] = jnp.zeros_like(dcw_ref)
            dcb_ref[...] = jnp.zeros_like(dcb_ref)
        x = x_ref[...]
        cwv = cw_ref[...]
        pre = _conv(x, cwv, cb_ref[...])
        s = _sigmoid(pre)
        dpre = d_ref[...].astype(F32) * s * (1.0 + pre * (1.0 - s))
        dcb_ref[...] += jnp.sum(dpre, axis=0, keepdims=True)
        _conv_bwd_w(dcw_ref, dpre, x)
        dx_ref[...] = _conv_bwd_x(dpre, cwv).astype(BF16)

    return pl.pallas_call(
        body, grid=(SSD_CONV_DIM // 256, bsz),
        in_specs=[pl.BlockSpec((lp, 256), lambda j, b: (b, XBC_COL0 + j)), pl.BlockSpec((lp, 256), lambda j, b: (b, j)),
                  pl.BlockSpec((CONV_W, 256), lambda j, b: (0, j)), pl.BlockSpec((1, 256), lambda j, b: (0, j))],
        out_specs=[pl.BlockSpec((lp, 256), lambda j, b: (b, j)), pl.BlockSpec((CONV_W, 256), lambda j, b: (0, j)),
                   pl.BlockSpec((1, 256), lambda j, b: (0, j))],
        out_shape=[jax.ShapeDtypeStruct((bsz * lp, SSD_CONV_DIM), BF16), jax.ShapeDtypeStruct((CONV_W, SSD_CONV_DIM), F32),
                   jax.ShapeDtypeStruct((1, SSD_CONV_DIM), F32)],
        compiler_params=_cparams("parallel", "arbitrary"), name="ssd_pre_bwd")(proj, dact, cw, cb)


def _split2(a):
    a1 = a.astype(BF16)
    return a1, (a - a1.astype(F32)).astype(BF16)


def _xdot2_nt(a, m01):
    a1, a2 = _split2(a)
    return _dot_nt(a1, m01) + _dot_nt(a2, m01)


def _xdot2_l(m01, a):
    a1, a2 = _split2(a)
    return _dot(m01, a1) + _dot(m01, a2)


class _SsdConsts:
    def __init__(self, g):
        q, gw = SSD_Q, SSD_GW
        head_of_lane = lax.shift_right_logical(_iota((DT_L, gw), 1), 6)
        self.sel = (_iota((DT_L, gw), 0) == 8 * g + head_of_lane).astype(BF16)
        r = _iota((q, gw), 0)
        c = jnp.bitwise_and(_iota((q, gw), 1), q - 1)
        self.diag = r == c
        self.diag_b = self.diag.astype(BF16)
        self.lower = c <= r
        self.upper = c >= r
        self.bd = lax.shift_right_logical(_iota((gw, gw), 0), 6) == lax.shift_right_logical(_iota((gw, gw), 1), 6)
        jj, ll = _iota((q, q), 1), _iota((q, q), 0)
        self.tri = (jj <= ll).astype(BF16)
        self.tri_t = (jj >= ll).astype(BF16)
        self.ones = jnp.ones((q, q), BF16)
        self.last = _iota((q, 1), 0) == q - 1


def _ssd_prepass(k, dt_ref, bias_ref, alog_ref, dtm_s, abc_s, lp):
    valid = _iota((lp, 1), 0) >= PAD
    sp_in = dt_ref[...] + bias_ref[...]
    dtm = jnp.where(valid, _softplus(sp_in), 0.0)
    aneg = -jnp.exp(alog_ref[...])
    dtm_s[...] = dtm
    abc_s[...] = _xdot_r(dtm * aneg, k.sel)
    return valid, sp_in, aneg


def _ssd_loop(nc, chunk, init):
    u = 2 if nc % 2 == 0 else 1

    def trip(i, carry):
        for j in range(u):
            carry = chunk(i * u + j, carry)
        return carry

    return lax.fori_loop(0, nc // u, trip, init)


def _ssd_chunk(k, rows, xs_ref, b_ref, c_ref, dtm_s, abc_s):
    bc = _xdot_l(k.tri, abc_s[rows, :])
    tt = jnp.sum(jnp.where(k.diag, bc, 0.0), axis=0, keepdims=True)
    dtbc = _xdot2(dtm_s[rows, :], k.sel)
    xs = xs_ref[rows, :]
    x = xs * dtbc
    bb = b_ref[rows, :].astype(BF16)
    cc = c_ref[rows, :].astype(BF16)
    tot = bc[SSD_Q - 1:SSD_Q, :]
    xbd = jnp.where(k.bd, jnp.concatenate([x] * 8, axis=0), 0.0).astype(BF16)
    return bc, tt, dtbc, xs, x, bb, cc, tot, xbd


def _ssd_specs(lp, order):
    ix = (lambda f: (lambda b, g: f(b, g))) if order == "bg" else (lambda f: (lambda g, b: f(b, g)))
    return [pl.BlockSpec((lp, SSD_GW), ix(lambda b, g: (b, g))),
            pl.BlockSpec((lp, SSD_N), ix(lambda b, g: (b, 1024 // SSD_N + g))),
            pl.BlockSpec((lp, SSD_N), ix(lambda b, g: (b, 1280 // SSD_N + g))),
            pl.BlockSpec((lp, DT_L), ix(lambda b, g: (b, DT_COL0))),
            pl.BlockSpec((1, DT_L), ix(lambda b, g: (0, 0))),
            pl.BlockSpec((1, DT_L), ix(lambda b, g: (0, 0))),
            pl.BlockSpec((1, DT_L), ix(lambda b, g: (0, 0)))]


def _ssd_fwd(xbc, proj, dt_bias, a_log, dskip, bsz, lp):
    nc = lp // SSD_Q

    def body(xs_ref, b_ref, c_ref, dt_ref, bias_ref, alog_ref, dsk_ref, y_ref, dtm_s, abc_s):
        k = _SsdConsts(pl.program_id(1))
        _ssd_prepass(k, dt_ref, bias_ref, alog_ref, dtm_s, abc_s, lp)
        dvec = _xdot_r(jnp.broadcast_to(dsk_ref[...], (8, DT_L)), k.sel)[0:1, :]

        def chunk(c, state):
            rows = pl.ds(pl.multiple_of(c * SSD_Q, SSD_Q), SSD_Q)
            bc, tt, _, xs, x, bb, cc, tot, xbd = _ssd_chunk(k, rows, xs_ref, b_ref, c_ref, dtm_s, abc_s)
            lm = jnp.where(k.lower, jnp.exp(jnp.minimum(bc - tt, 0.0)), 0.0)
            g_all = _dot(_dot_nt(cc, bb).astype(BF16), k.diag_b) * lm
            y = _dot(g_all.astype(BF16), xbd) + jnp.exp(bc) * _dot(cc, state.astype(BF16)) + dvec * xs
            y_ref[rows, :] = y
            return jnp.exp(tot) * state + _dot_tn(bb, (jnp.exp(tot - bc) * x).astype(BF16))

        _ssd_loop(nc, chunk, jnp.zeros((SSD_N, SSD_GW), F32))

    return pl.pallas_call(
        body, grid=(bsz, SSD_GROUPS), in_specs=_ssd_specs(lp, "bg"),
        out_specs=pl.BlockSpec((lp, SSD_GW), lambda b, g: (b, g)),
        out_shape=jax.ShapeDtypeStruct((bsz * lp, 1024), F32),
        scratch_shapes=[pltpu.VMEM((lp, DT_L), F32), pltpu.VMEM((lp, SSD_GW), F32)],
        compiler_params=_cparams("parallel", "parallel"), name="ssd_fwd")(xbc, xbc, xbc, proj, dt_bias, a_log, dskip)


def _ssd_bwd(xbc, proj, dt_bias, a_log, dskip, dy, bsz, lp):
    nc = lp // SSD_Q

    def body(xs_ref, b_ref, c_ref, dt_ref, bias_ref, alog_ref, dsk_ref, dy_ref,
             dxs_ref, db_ref, dc_ref, ddt_ref, dbias_ref, dalog_ref, ddsk_ref, dtm_s, abc_s, st_s):
        @pl.when(pl.program_id(1) == 0)
        def _():
            for ref in (dbias_ref, dalog_ref, ddsk_ref):
                ref[...] = jnp.zeros_like(ref)

        k = _SsdConsts(pl.program_id(0))
        valid, sp_in, aneg = _ssd_prepass(k, dt_ref, bias_ref, alog_ref, dtm_s, abc_s, lp)
        dvec = _xdot_r(jnp.broadcast_to(dsk_ref[...], (8, DT_L)), k.sel)[0:1, :]

        def fwd_chunk(c, state):
            rows = pl.ds(pl.multiple_of(c * SSD_Q, SSD_Q), SSD_Q)
            st_s[c] = state.astype(BF16)
            bc, _, _, _, x, bb, _, tot, _ = _ssd_chunk(k, rows, xs_ref, b_ref, c_ref, dtm_s, abc_s)
            return jnp.exp(tot) * state + _dot_tn(bb, (jnp.exp(tot - bc) * x).astype(BF16))

        _ssd_loop(nc, fwd_chunk, jnp.zeros((SSD_N, SSD_GW), F32))

        def bwd_chunk(i, carry):
            dstate, daneg, ddsk = carry
            c = nc - 1 - i
            rows = pl.ds(pl.multiple_of(c * SSD_Q, SSD_Q), SSD_Q)
            bc, tt, dtbc, xs, x, bb, cc, tot, xbd = _ssd_chunk(k, rows, xs_ref, b_ref, c_ref, dtm_s, abc_s)
            sprev = st_s[c]
            dyc = dy_ref[rows, :]
            dyb = dyc.astype(BF16)
            seg = bc - tt
            lm = jnp.where(k.lower, jnp.exp(jnp.minimum(seg, 0.0)), 0.0)
            lm_t = jnp.where(k.upper, jnp.exp(jnp.minimum(-seg, 0.0)), 0.0)
            cb_all = _dot(_dot_nt(cc, bb).astype(BF16), k.diag_b)
            cbt_all = _dot(_dot_nt(bb, cc).astype(BF16), k.diag_b)
            g_all = cb_all * lm
            dybd = jnp.where(k.bd, jnp.concatenate([dyc] * 8, axis=0), 0.0).astype(BF16)
            dg = _dot_nt(dyb, xbd)
            dx = _dot((cbt_all * lm_t).astype(BF16), dybd)
            hh = dg * g_all
            ea = jnp.exp(bc)
            yo = ea * _dot(cc, sprev)
            col_h = jnp.sum(hh, axis=0, keepdims=True)
            dcb = _dot_nt((dg * lm).astype(BF16), k.diag_b).astype(BF16)
            dcs = (ea * dyc).astype(BF16)
            dstb = dstate.astype(BF16)
            dec = jnp.exp(tot - bc)
            w = dec * x
            dw = _dot(bb, dstb)
            dc_ref[rows, :] = (_dot(dcb, bb) + _dot_nt(dcs, sprev)).astype(BF16)
            db_ref[rows, :] = (_dot_tn(dcb, cc) + _dot_nt(w.astype(BF16), dstb)).astype(BF16)
            dx = dx + dec * dw
            etot = jnp.exp(tot)
            r8 = _iota((8, SSD_GW), 0)
            st_row = jnp.sum(dstate * sprev.astype(F32), axis=0, keepdims=True) * etot
            sk_row = jnp.sum(dyc * xs, axis=0, keepdims=True)
            small = jnp.where(r8 == 0, st_row, jnp.where(r8 == 1, sk_row, 0.0))
            q = SSD_Q
            sums = _xdot2_nt(jnp.concatenate([hh + dyc * yo - jnp.where(k.diag, col_h, 0.0) - dw * w, dw * w, dx * xs, small],
                                             axis=0), k.sel)
            kk = sums[q:2 * q]
            dtot = sums[3 * q:3 * q + 1] + jnp.sum(kk, axis=0, keepdims=True)
            dacum = sums[0:q] + jnp.where(k.last, dtot, 0.0)
            da = _xdot2_l(k.tri_t, dacum)
            dtm_c = dtm_s[rows, :]
            ddtm = da * aneg + sums[2 * q:3 * q]
            vrow = (c * SSD_Q + _iota((SSD_Q, 1), 0)) >= PAD
            ddt_ref[rows, :] = jnp.where(vrow, ddtm * _sigmoid(dt_ref[rows, :] + bias_ref[...]), 0.0)
            dxs_ref[rows, :] = (dx * dtbc + dvec * dyc).astype(BF16)
            daneg = daneg + jnp.sum(da * dtm_c, axis=0, keepdims=True)
            ddsk = ddsk + sums[3 * q + 1:3 * q + 2]
            dstate = etot * dstate + _dot_tn(cc, dcs)
            return dstate, daneg, ddsk

        zrow = jnp.zeros((1, DT_L), F32)
        _, daneg, ddsk = _ssd_loop(nc, bwd_chunk, (jnp.zeros((SSD_N, SSD_GW), F32), zrow, zrow))
        dbias_ref[...] += jnp.broadcast_to(jnp.sum(ddt_ref[...], axis=0, keepdims=True), (8, DT_L))
        dalog_ref[...] += jnp.broadcast_to(daneg * aneg, (8, DT_L))
        ddsk_ref[...] += jnp.broadcast_to(ddsk, (8, DT_L))

    t = bsz * lp
    par = pl.BlockSpec((8, DT_L), lambda g, b: (g, 0))
    par_shape = jax.ShapeDtypeStruct((8 * SSD_GROUPS, DT_L), F32)
    return pl.pallas_call(
        body, grid=(SSD_GROUPS, bsz),
        in_specs=_ssd_specs(lp, "gb") + [pl.BlockSpec((lp, SSD_GW), lambda g, b: (b, g))],
        out_specs=[pl.BlockSpec((lp, SSD_GW), lambda g, b: (b, g)), pl.BlockSpec((lp, SSD_N), lambda g, b: (b, g)),
                   pl.BlockSpec((lp, SSD_N), lambda g, b: (b, g)), pl.BlockSpec((lp, DT_L), lambda g, b: (b, g)), par, par, par],
        out_shape=[jax.ShapeDtypeStruct((t, 1024), BF16), jax.ShapeDtypeStruct((t, 256), BF16),
                   jax.ShapeDtypeStruct((t, 256), BF16), jax.ShapeDtypeStruct((t, SSD_GROUPS * DT_L), F32),
                   par_shape, par_shape, par_shape],
        scratch_shapes=[pltpu.VMEM((lp, DT_L), F32), pltpu.VMEM((lp, SSD_GW), F32), pltpu.VMEM((nc, SSD_N, SSD_GW), BF16)],
        compiler_params=_cparams("parallel", "arbitrary"), name="ssd_bwd")(
            xbc, xbc, xbc, proj, dt_bias, a_log, dskip, dy)


Z_COL0 = 2048 // SSD_GW


def _gnorm_fwd(y, proj, w, mixed, name="gnorm_fwd"):
    t = y.shape[0]
    tm = _row_tile(t)

    def body(y_ref, z_ref, w_ref, mixed_ref, o_ref):
        z = z_ref[...]
        gt = y_ref[...] * z * _sigmoid(z)
        r = lax.rsqrt(jnp.mean(gt * gt, axis=-1, keepdims=True) + EPS)
        o_ref[...] = (gt * r * w_ref[...]).astype(BF16)

    return pl.pallas_call(
        body, grid=(t // tm, SSD_GROUPS),
        in_specs=[pl.BlockSpec((tm, SSD_GW), lambda i, g: (i, g)), pl.BlockSpec((tm, SSD_GW), lambda i, g: (i, Z_COL0 + g)),
                  pl.BlockSpec((1, SSD_GW), lambda i, g: (0, g)), ANY_SPEC],
        out_specs=pl.BlockSpec((tm, SSD_GW), lambda i, g: (i, 1024 // SSD_GW + g)),
        out_shape=jax.ShapeDtypeStruct((t, 2048), BF16), input_output_aliases={3: 0},
        compiler_params=_cparams("parallel", "parallel"), name=name)(y, proj, w, mixed)


def _gnorm_bwd(y, proj, w, dmixed):
    t = y.shape[0]
    tm = _row_tile(t)

    def body(y_ref, z_ref, w_ref, d_ref, dy_ref, dz_ref, dw_ref):
        @pl.when(pl.program_id(1) == 0)
        def _():
            dw_ref[...] = jnp.zeros_like(dw_ref)
        z, yv, d = z_ref[...], y_ref[...], d_ref[...]
        s = _sigmoid(z)
        sz = z * s
        gt = yv * sz
        r = lax.rsqrt(jnp.mean(gt * gt, axis=-1, keepdims=True) + EPS)
        gh = gt * r
        dgn = d * w_ref[...]
        dgt = r * (dgn - gh * jnp.mean(dgn * gh, axis=-1, keepdims=True))
        dw_ref[...] += jnp.sum(d * gh, axis=0, keepdims=True)
        dy_ref[...] = dgt * sz
        dz_ref[...] = (dgt * yv * s * (1.0 + z * (1.0 - s))).astype(BF16)

    blk = pl.BlockSpec((tm, SSD_GW), lambda g, i: (i, g))
    return pl.pallas_call(
        body, grid=(SSD_GROUPS, t // tm),
        in_specs=[blk, pl.BlockSpec((tm, SSD_GW), lambda g, i: (i, Z_COL0 + g)), pl.BlockSpec((1, SSD_GW), lambda g, i: (0, g)),
                  pl.BlockSpec((tm, SSD_GW), lambda g, i: (i, 1024 // SSD_GW + g))],
        out_specs=[blk, blk, pl.BlockSpec((1, SSD_GW), lambda g, i: (0, g))],
        out_shape=[jax.ShapeDtypeStruct((t, 1024), F32), jax.ShapeDtypeStruct((t, 1024), BF16),
                   jax.ShapeDtypeStruct((1, 1024), F32)],
        compiler_params=_cparams("parallel", "arbitrary"), name="gnorm_bwd")(y, proj, w, dmixed)


def _meta_grad(dh0, bsz, lp):
    def body(d_ref, o_ref):
        @pl.when(pl.program_id(0) == 0)
        def _():
            o_ref[...] = jnp.zeros_like(o_ref)
        o_ref[...] += d_ref[...]

    return pl.pallas_call(
        body, grid=(bsz,),
        in_specs=[pl.BlockSpec((N_META, D_MODEL), lambda b: (b * (lp // N_META) + PAD // N_META, 0))],
        out_specs=pl.BlockSpec((N_META, D_MODEL), lambda b: (0, 0)),
        out_shape=jax.ShapeDtypeStruct((N_META, D_MODEL), F32),
        compiler_params=_cparams("arbitrary"), name="meta_grad")(dh0)


OTHER_CHIPS = ((1, 0), (0, 1), (1, 1))
ANY_SPEC = pl.BlockSpec(memory_space=pl.ANY)


class _Mesh:
    def __init__(self):
        self.x, self.y, self.c = lax.axis_index("x"), lax.axis_index("y"), lax.axis_index("c")

    def dev(self, fx, fy, fc):
        return (1 - self.x if fx else self.x, 1 - self.y if fy else self.y, 1 - self.c if fc else self.c)

    def slot(self, fx, fy, fc):
        px, py, pc = self.dev(fx, fy, fc)
        return 4 * px + 2 * py + pc

    def chip(self, fx, fy):
        px, py, _ = self.dev(fx, fy, 0)
        return 2 * px + py


def _remote(src, dst, send_sems, recv_sems, idx, dev):
    return pltpu.make_async_remote_copy(src_ref=src, dst_ref=dst, send_sem=send_sems.at[idx], recv_sem=recv_sems.at[idx],
                                        device_id=dev, device_id_type=pl.DeviceIdType.MESH)


def _gather_all(arrs, name):
    n = len(arrs)

    def body(*refs):
        ins, outs = refs[:n], refs[n:2 * n]
        send_sems, recv_sems = refs[2 * n:]
        me = _Mesh()
        first, passed = [], []
        for a in range(n):
            mine = outs[a].at[me.slot(0, 0, 0)]
            first.append(_remote(ins[a], mine, send_sems, recv_sems, (a, 0), me.dev(0, 0, 1)))
            for j, (fx, fy) in enumerate(OTHER_CHIPS):
                first.append(_remote(ins[a], mine, send_sems, recv_sems, (a, 1 + j), me.dev(fx, fy, 0)))
        for cp in first:
            cp.start()
        for a in range(n):
            for j, (fx, fy) in enumerate(OTHER_CHIPS):
                blk = outs[a].at[me.slot(fx, fy, 0)]
                _remote(ins[a], blk, send_sems, recv_sems, (a, 1 + j), me.dev(fx, fy, 0)).wait_recv()
                cp = _remote(blk, blk, send_sems, recv_sems, (a, 4 + j), me.dev(0, 0, 1))
                cp.start()
                passed.append(cp)
        for a in range(n):
            _remote(ins[a], outs[a].at[me.slot(0, 0, 1)], send_sems, recv_sems, (a, 0), me.dev(0, 0, 1)).wait_recv()
            for j, (fx, fy) in enumerate(OTHER_CHIPS):
                _remote(ins[a], outs[a].at[me.slot(fx, fy, 1)], send_sems, recv_sems, (a, 4 + j), me.dev(0, 0, 1)).wait_recv()
        for cp in first + passed:
            cp.wait_send()

    stacks = pl.pallas_call(
        body, in_specs=[ANY_SPEC] * n, out_specs=[ANY_SPEC] * n,
        out_shape=[jax.ShapeDtypeStruct((N_DEV,) + a.shape, a.dtype) for a in arrs],
        scratch_shapes=[pltpu.SemaphoreType.DMA((n, 7)), pltpu.SemaphoreType.DMA((n, 7))],
        compiler_params=pltpu.CompilerParams(has_side_effects=True), name=name)(*arrs)
    me = 4 * lax.axis_index("x") + 2 * lax.axis_index("y") + lax.axis_index("c")
    return [lax.dynamic_update_slice_in_dim(s, a[None], me, axis=0) for s, a in zip(stacks, arrs)]


def _swap_in_chip(arrs, name):
    n = len(arrs)

    def body(*refs):
        ins, got = refs[:n], refs[n:2 * n]
        send_sems, recv_sems = refs[2 * n:]
        me = _Mesh()
        sends = [_remote(ins[a].at[2 * j + (1 - me.c)], got[a].at[j], send_sems, recv_sems, (a, j), me.dev(0, 0, 1))
                 for a in range(n) for j in range(4)]
        for cp in sends:
            cp.start()
        for cp in sends:
            cp.wait()

    return pl.pallas_call(
        body, in_specs=[ANY_SPEC] * n, out_specs=[ANY_SPEC] * n,
        out_shape=[jax.ShapeDtypeStruct((4,) + a.shape[1:], a.dtype) for a in arrs],
        scratch_shapes=[pltpu.SemaphoreType.DMA((n, 4)), pltpu.SemaphoreType.DMA((n, 4))],
        compiler_params=pltpu.CompilerParams(has_side_effects=True), name=name)(*arrs)


def _add_pair(a, b, name):
    _, r, c = a.shape
    tr = _adam_rows(r)

    def body(a_ref, b_ref, o_ref):
        o_ref[...] = (a_ref[...].astype(F32) + b_ref[...].astype(F32)).astype(o_ref.dtype)

    blk = pl.BlockSpec((4, tr, c), lambda i: (0, i, 0))
    return pl.pallas_call(body, grid=(r // tr,), in_specs=[blk, blk], out_specs=blk,
                          out_shape=jax.ShapeDtypeStruct(a.shape, a.dtype), compiler_params=_cparams("parallel"), name=name)(a, b)


def _swap_chips(sums, gather, name):
    ns, ng = len(sums), len(gather)
    n = ns + ng

    def body(*refs):
        ins, outs = refs[:n], refs[n:2 * n]
        send_sems, recv_sems, local_sems = refs[2 * n:]
        me = _Mesh()
        local, sends, recvs = [], [], []
        for a in range(ns):
            here = me.chip(0, 0)
            for j, (fx, fy) in enumerate(OTHER_CHIPS):
                there = me.chip(fx, fy)
                sends.append(_remote(ins[a].at[there], outs[a].at[here], send_sems, recv_sems, (a, j), me.dev(fx, fy, 0)))
                recvs.append(_remote(ins[a].at[there], outs[a].at[there], send_sems, recv_sems, (a, j), me.dev(fx, fy, 0)))
        for a in range(ns, n):
            local.append(pltpu.make_async_copy(ins[a], outs[a].at[me.slot(0, 0, 0)], local_sems.at[a]))
            for k in range(1, N_DEV):
                flips = (k >> 2 & 1, k >> 1 & 1, k & 1)
                sends.append(_remote(ins[a], outs[a].at[me.slot(0, 0, 0)], send_sems, recv_sems, (a, k - 1), me.dev(*flips)))
                recvs.append(_remote(ins[a], outs[a].at[me.slot(*flips)], send_sems, recv_sems, (a, k - 1), me.dev(*flips)))
        for cp in local + sends:
            cp.start()
        for cp in recvs:
            cp.wait_recv()
        for cp in sends:
            cp.wait_send()
        for cp in local:
            cp.wait()

    out_shape = [jax.ShapeDtypeStruct(a.shape, a.dtype) for a in sums]
    out_shape += [jax.ShapeDtypeStruct((N_DEV,) + a.shape, a.dtype) for a in gather]
    outs = pl.pallas_call(
        body, in_specs=[ANY_SPEC] * n, out_specs=[ANY_SPEC] * n, out_shape=out_shape,
        scratch_shapes=[pltpu.SemaphoreType.DMA((n, N_DEV - 1)), pltpu.SemaphoreType.DMA((n, N_DEV - 1)),
                        pltpu.SemaphoreType.DMA((n,))],
        compiler_params=pltpu.CompilerParams(has_side_effects=True), name=name)(*sums, *gather)
    here = 2 * lax.axis_index("x") + lax.axis_index("y")
    mine = [lax.dynamic_update_slice_in_dim(o, lax.dynamic_slice_in_dim(s, here, 1, axis=0), here, axis=0)
            for o, s in zip(outs[:ns], sums)]
    return mine + list(outs[ns:])


def _adam_rows(r):
    for cand in (128, 64, 32, 16, 8):
        if r % cand == 0 and r > cand:
            return cand
    return r


def _adamw(parts, w, m, v, name):
    r, c = w.shape
    tr = _adam_rows(r)
    n_parts = parts.shape[0]

    def body(p_ref, w_ref, m_ref, v_ref, g_ref, d_ref, nm_ref, nv_ref):
        g = p_ref[0].astype(F32)
        for k in range(1, n_parts):
            g = g + p_ref[k].astype(F32)
        m_new = ADAM_B1 * m_ref[...] + (1.0 - ADAM_B1) * g
        v_new = ADAM_B2 * v_ref[...] + (1.0 - ADAM_B2) * (g * g)
        m_hat = m_new / (1.0 - ADAM_B1 ** ADAM_STEP)
        v_hat = v_new / (1.0 - ADAM_B2 ** ADAM_STEP)
        g_ref[...] = g
        d_ref[...] = -ADAM_LR * (m_hat / (jnp.sqrt(v_hat) + ADAM_EPS) + ADAM_WD * w_ref[...])
        nm_ref[...] = m_new
        nv_ref[...] = v_new

    blk = pl.BlockSpec((tr, c), lambda i: (i, 0))
    shp = jax.ShapeDtypeStruct((r, c), F32)
    return pl.pallas_call(
        body, grid=(r // tr,), in_specs=[pl.BlockSpec((n_parts, tr, c), lambda i: (0, i, 0)), blk, blk, blk],
        out_specs=[blk] * 4, out_shape=[shp] * 4, compiler_params=_cparams("parallel"), name=name)(parts, w, m, v)


def _rows128(a):
    return a.reshape(-1, 128)


def _pad_rows(a, rows):
    return jnp.pad(a, ((0, rows - a.shape[0]), (0, 0)))


def _lane16(a):
    return jnp.pad(a.reshape(1, -1), ((0, 0), (0, 128 - a.size)))


SHARD_PACK_ROWS = 544
REPL_PACK_ROWS = 72
LOSS_ROW = 71


def _pack_shard(meta, lru_conv_w, odd_norm, ssd_conv_w, lru_w_a, lru_w_x):
    parts = [meta.reshape(16, 128), lru_conv_w.reshape(4, 128), odd_norm.reshape(1, 128), _rows128(ssd_conv_w.reshape(4, 192)),
             _rows128(lru_w_a.reshape(4, 32, 256)), _rows128(lru_w_x.reshape(4, 32, 256))]
    return _pad_rows(jnp.concatenate(parts, axis=0), SHARD_PACK_ROWS)


def _unpack_shard(p):
    return (p[0:16], p[16:20].reshape(1, 4, 128), p[20:21], p[21:27].reshape(1, 4, 192),
            p[27:283].reshape(1, 4, 32, 256), p[283:539].reshape(1, 4, 32, 256))


def _pack_repl(even_norm, lru_conv_b, lru_b_a, lru_b_x, lru_lambda, ssd_norm, final_norm, ssd_conv_b, dt_bias, a_log, ssd_d):
    parts = [_rows128(v) for v in (even_norm, lru_conv_b, lru_b_a, lru_b_x, lru_lambda, ssd_norm, final_norm, ssd_conv_b)]
    parts += [_lane16(dt_bias), _lane16(a_log), _lane16(ssd_d)]
    return _pad_rows(jnp.concatenate(parts, axis=0), REPL_PACK_ROWS)


def _unpack_repl(p):
    vec = lambda i: p[8 * i:8 * i + 8].reshape(1, 1024)
    return (vec(0), vec(1), vec(2), vec(3), vec(4), vec(5), p[48:56].reshape(1024), p[56:68].reshape(1, 1536),
            p[68:69, :16], p[69:70, :16], p[70:71, :16])


def _local_step(x, tgt, meta, even_norm, w_in_e, lru_conv_w, lru_conv_b, lru_w_a, lru_b_a, lru_w_x, lru_b_x, lru_lambda,
                ssd_conv_w, ssd_conv_b, dt_bias, a_log, ssd_d, ssd_norm, w_out_e, odd_norm, w_in_o, w_out_o, final_norm):
    bsz, seq, d = x.shape
    lp = PAD + N_META + seq
    t = bsz * lp
    h0 = jnp.concatenate([jnp.zeros((bsz, PAD, d), F32), jnp.broadcast_to(meta[None], (bsz, N_META, d)), x], axis=1).reshape(t, d)

    u0 = _norm_fwd(h0, even_norm, "norm0_fwd")
    proj = _mm_nn(u0, w_in_e, F32, "even_in")
    lru = (lru_conv_w, lru_conv_b, lru_w_a, lru_b_a, lru_w_x, lru_b_x, lru_lambda)
    ya = _lru_fwd(proj, *lru, bsz, lp)
    xbc = _ssd_pre_fwd(proj, ssd_conv_w, ssd_conv_b, bsz, lp)
    y = _ssd_fwd(xbc, proj, dt_bias, a_log, ssd_d, bsz, lp)
    mixed = _gnorm_fwd(y, proj, ssd_norm, ya)
    h1 = _mm_nn(mixed, w_out_e, F32, "even_out", res=h0)
    u2 = _norm_fwd(h1, odd_norm, "norm1_fwd")
    qkvg = _mm_nn(u2, w_in_o, F32, "odd_in")
    o, og = _sb_fwd(qkvg, bsz, lp)
    h2 = _mm_nn(og, w_out_o, F32, "odd_out", res=h1)
    loss, dh2, d_final_norm = _final_loss(h2, final_norm, tgt, lp)

    dog = _mm_nn(dh2, w_out_o.T, F32, "odd_out_dx")
    d_w_out_o = _mm_tn(og, dh2, "odd_out_dw")
    dqkvg = jnp.concatenate(_sb_bwd(qkvg, o, dog, bsz, lp), axis=1)
    du2 = _mm_nn(dqkvg, w_in_o.T, F32, "odd_in_dx")
    d_w_in_o = _mm_tn(u2, dqkvg, "odd_in_dw")
    dh1, d_odd_norm = _norm_bwd(h1, odd_norm, du2, dh2, "norm1_bwd")

    dmixed = _mm_nn(dh1, w_out_e.T, F32, "even_out_dx")
    d_w_out_e = _mm_tn(mixed, dh1, "even_out_dw")
    dlx, dgate, d_lru_conv_w, d_lru_conv_b, d_w_a, d_b_a, d_w_x, d_b_x, d_lambda = _lru_bwd(proj, dmixed, *lru, bsz, lp)
    dy, dz, d_ssd_norm = _gnorm_bwd(y, proj, ssd_norm, dmixed)
    dxs, dbm, dcm, ddt, d_dt_bias, d_a_log, d_ssd_d = _ssd_bwd(xbc, proj, dt_bias, a_log, ssd_d, dy, bsz, lp)
    dxbc, d_ssd_conv_w, d_ssd_conv_b = _ssd_pre_bwd(proj, jnp.concatenate([dxs, dbm, dcm], axis=1), ssd_conv_w, ssd_conv_b, bsz, lp)
    ddt = (ddt[:, :DT_L] + ddt[:, DT_L:]).astype(BF16)
    dproj = jnp.concatenate([dlx, dgate, dz, dxbc, ddt, jnp.zeros((t, EVEN_NP - 4608 - DT_L), BF16)], axis=1)
    du0 = _mm_nn(dproj, w_in_e.T, F32, "even_in_dx")
    d_w_in_e = _mm_tn(u0, dproj, "even_in_dw")
    dh0, d_even_norm = _norm_bwd(h0, even_norm, du0, dh1, "norm0_bwd")
    grad_x = dh0.reshape(bsz, lp, d)[:, PAD + N_META:]
    d_meta = _meta_grad(dh0, bsz, lp)
    heads = lambda p: (p[0:1] + p[8:9])[:, :SSD_HEADS]
    grads = dict(meta=d_meta, even_norm=d_even_norm, even_w_in=d_w_in_e[:, :EVEN_IN], lru_conv_w=d_lru_conv_w,
                 lru_conv_b=d_lru_conv_b, lru_w_a=d_w_a, lru_b_a=d_b_a, lru_w_x=d_w_x, lru_b_x=d_b_x, lru_lambda=d_lambda,
                 ssd_conv_w=d_ssd_conv_w, ssd_conv_b=d_ssd_conv_b, ssd_dt_bias=heads(d_dt_bias), ssd_a_log=heads(d_a_log),
                 ssd_d=heads(d_ssd_d), ssd_norm=d_ssd_norm, even_w_out=d_w_out_e, odd_norm=d_odd_norm, odd_w_in=d_w_in_o,
                 odd_w_out=d_w_out_o, final_norm=d_final_norm)
    return loss[0, 0], grad_x, grads


WEIGHTS = ['meta', 'even_norm', 'even_w_in', 'lru_conv_w', 'lru_conv_b', 'lru_w_a', 'lru_b_a', 'lru_w_x', 'lru_b_x', 'lru_lambda',
           'ssd_conv_w', 'ssd_conv_b', 'ssd_dt_bias', 'ssd_a_log', 'ssd_d', 'ssd_norm', 'even_w_out', 'odd_norm', 'odd_w_in',
           'odd_w_out', 'final_norm']


def _blocks_of(a, axis):
    shp = a.shape
    a = a.reshape(shp[:axis] + (N_DEV, shp[axis] // N_DEV) + shp[axis + 1:])
    return jnp.moveaxis(a, axis, 0)


def _unblock(a, axis):
    a = jnp.moveaxis(a, 0, axis)
    shp = a.shape
    return a.reshape(shp[:axis] + (shp[axis] * shp[axis + 1],) + shp[axis + 2:])


def kernel(x, meta, even_norm, even_w_in, lru_conv_w, lru_conv_b, lru_w_a, lru_b_a, lru_w_x, lru_b_x, lru_lambda, ssd_conv_w, ssd_conv_b, ssd_dt_bias, ssd_a_log, ssd_d, ssd_norm, even_w_out, odd_norm, odd_w_in, odd_w_out, final_norm, loss_target, m_meta, m_even_norm, m_even_w_in, m_lru_conv_w, m_lru_conv_b, m_lru_w_a, m_lru_b_a, m_lru_w_x, m_lru_b_x, m_lru_lambda, m_ssd_conv_w, m_ssd_conv_b, m_ssd_dt_bias, m_ssd_a_log, m_ssd_d, m_ssd_norm, m_even_w_out, m_odd_norm, m_odd_w_in, m_odd_w_out, m_final_norm, v_meta, v_even_norm, v_even_w_in, v_lru_conv_w, v_lru_conv_b, v_lru_w_a, v_lru_b_a, v_lru_w_x, v_lru_b_x, v_lru_lambda, v_ssd_conv_w, v_ssd_conv_b, v_ssd_dt_bias, v_ssd_a_log, v_ssd_d, v_ssd_norm, v_even_w_out, v_odd_norm, v_odd_w_in, v_odd_w_out, v_final_norm):
    w = dict(meta=meta, even_norm=even_norm, even_w_in=even_w_in, lru_conv_w=lru_conv_w, lru_conv_b=lru_conv_b, lru_w_a=lru_w_a,
             lru_b_a=lru_b_a, lru_w_x=lru_w_x, lru_b_x=lru_b_x, lru_lambda=lru_lambda, ssd_conv_w=ssd_conv_w,
             ssd_conv_b=ssd_conv_b, ssd_dt_bias=ssd_dt_bias, ssd_a_log=ssd_a_log, ssd_d=ssd_d, ssd_norm=ssd_norm,
             even_w_out=even_w_out, odd_norm=odd_norm, odd_w_in=odd_w_in, odd_w_out=odd_w_out, final_norm=final_norm)
    m = dict(meta=m_meta, even_norm=m_even_norm, even_w_in=m_even_w_in, lru_conv_w=m_lru_conv_w, lru_conv_b=m_lru_conv_b,
             lru_w_a=m_lru_w_a, lru_b_a=m_lru_b_a, lru_w_x=m_lru_w_x, lru_b_x=m_lru_b_x, lru_lambda=m_lru_lambda,
             ssd_conv_w=m_ssd_conv_w, ssd_conv_b=m_ssd_conv_b, ssd_dt_bias=m_ssd_dt_bias, ssd_a_log=m_ssd_a_log, ssd_d=m_ssd_d,
             ssd_norm=m_ssd_norm, even_w_out=m_even_w_out, odd_norm=m_odd_norm, odd_w_in=m_odd_w_in, odd_w_out=m_odd_w_out,
             final_norm=m_final_norm)
    v = dict(meta=v_meta, even_norm=v_even_norm, even_w_in=v_even_w_in, lru_conv_w=v_lru_conv_w, lru_conv_b=v_lru_conv_b,
             lru_w_a=v_lru_w_a, lru_b_a=v_lru_b_a, lru_w_x=v_lru_w_x, lru_b_x=v_lru_b_x, lru_lambda=v_lru_lambda,
             ssd_conv_w=v_ssd_conv_w, ssd_conv_b=v_ssd_conv_b, ssd_dt_bias=v_ssd_dt_bias, ssd_a_log=v_ssd_a_log, ssd_d=v_ssd_d,
             ssd_norm=v_ssd_norm, even_w_out=v_even_w_out, odd_norm=v_odd_norm, odd_w_in=v_odd_w_in, odd_w_out=v_odd_w_out,
             final_norm=v_final_norm)
    shard_names = ('meta', 'lru_conv_w', 'odd_norm', 'ssd_conv_w', 'lru_w_a', 'lru_w_x')
    repl_names = ('even_norm', 'lru_conv_b', 'lru_b_a', 'lru_b_x', 'lru_lambda', 'ssd_norm', 'final_norm', 'ssd_conv_b',
                  'ssd_dt_bias', 'ssd_a_log', 'ssd_d')
    big_names = ('even_w_in', 'even_w_out', 'odd_w_in', 'odd_w_out')

    gates = jnp.concatenate([lru_w_a.reshape(128, 256), lru_w_x.reshape(128, 256)], axis=0).astype(BF16)
    small = _pack_shard(*[w[k] for k in shard_names])
    g_in_e, g_out_e, g_in_o, g_out_o, g_gates, g_small = _gather_all(
        [even_w_in[0].astype(BF16), even_w_out[0].astype(BF16), odd_w_in[0].astype(BF16), odd_w_out[0].astype(BF16), gates, small],
        "gather_weights")
    w_in_e = jnp.pad(_unblock(g_in_e, 1), ((0, 0), (0, EVEN_NP - EVEN_IN)))
    w_out_e = g_out_e.reshape(2048, 1024)
    w_in_o = _unblock(g_in_o, 1)
    w_out_o = g_out_o.reshape(1024, 1024)
    gates_full = jnp.moveaxis(g_gates.reshape(N_DEV, 2, 4, 32, 256), 0, 2).reshape(2, 4, 256, 256)
    f_meta = _unblock(g_small[:, 0:16], 1)
    f_lru_conv_w = _unblock(g_small[:, 16:20], 1)
    f_odd_norm = _unblock(g_small[:, 20:21], 1)
    f_ssd_conv_w = _unblock(g_small[:, 21:27].reshape(N_DEV, 4, 192), 1)

    loss, grad_x, g = _local_step(
        x, loss_target, f_meta, even_norm, w_in_e, f_lru_conv_w, lru_conv_b, gates_full[0], lru_b_a, gates_full[1], lru_b_x,
        lru_lambda, f_ssd_conv_w, ssd_conv_b, _lane16(ssd_dt_bias), _lane16(ssd_a_log), _lane16(ssd_d), ssd_norm, w_out_e,
        f_odd_norm, w_in_o, w_out_o, final_norm.reshape(1, -1))

    s_small = jnp.stack([_pack_shard(g['meta'][:, 128 * p:128 * (p + 1)], g['lru_conv_w'][:, 128 * p:128 * (p + 1)],
                                     g['odd_norm'][:, 128 * p:128 * (p + 1)], g['ssd_conv_w'][:, 192 * p:192 * (p + 1)],
                                     g['lru_w_a'][:, 32 * p:32 * (p + 1)], g['lru_w_x'][:, 32 * p:32 * (p + 1)])
                         for p in range(N_DEV)])
    r_pack = _pack_repl(*[g[k] for k in repl_names]).at[LOSS_ROW, 0].set(loss)
    by_owner = [_blocks_of(g['even_w_in'], 1).astype(BF16), g['even_w_out'].reshape(N_DEV, 256, 1024).astype(BF16),
                _blocks_of(g['odd_w_in'], 1).astype(BF16), g['odd_w_out'].reshape(N_DEV, 128, 1024).astype(BF16), s_small]
    got = _swap_in_chip(by_owner, "reduce_in_chip")
    my_c = lax.axis_index("c")
    kept = [lax.dynamic_index_in_dim(a.reshape((4, 2) + a.shape[1:]), my_c, axis=1, keepdims=False) for a in by_owner]
    chip_sums = [_add_pair(a, b, "chip_sum_%d" % i) for i, (a, b) in enumerate(zip(kept, got))]
    p_in_e, p_out_e, p_in_o, p_out_o, p_small, p_repl = _swap_chips(chip_sums, [r_pack], "reduce_across_chips")

    res = {}
    for k, parts in zip(big_names, (p_in_e, p_out_e, p_in_o, p_out_o)):
        outs = _adamw(parts, w[k][0], m[k][0], v[k][0], "adamw_" + k)
        res[k] = [o[None] for o in outs]
    outs = _adamw(p_small, small, _pack_shard(*[m[k] for k in shard_names]), _pack_shard(*[v[k] for k in shard_names]), "adamw_sharded")
    unpacked = [_unpack_shard(o) for o in outs]
    for i, k in enumerate(shard_names):
        res[k] = [u[i] for u in unpacked]
    outs = _adamw(p_repl, _pack_repl(*[w[k] for k in repl_names]), _pack_repl(*[m[k] for k in repl_names]),
                  _pack_repl(*[v[k] for k in repl_names]), "adamw_replicated")
    unpacked = [_unpack_repl(o) for o in outs]
    for i, k in enumerate(repl_names):
        res[k] = [u[i] for u in unpacked]
    loss = outs[0][LOSS_ROW, 0]
    return (loss, grad_x, *[res[k][0] for k in WEIGHTS], *[res[k][1] for k in WEIGHTS], *[res[k][2] for k in WEIGHTS],
            *[res[k][3] for k in WEIGHTS])
```

```python
import jax
import jax.numpy as jnp
from jax import lax
from jax.experimental import pallas as pl
from jax.experimental.pallas import tpu as pltpu

F32 = jnp.float32
BF16 = jnp.bfloat16

D_MODEL = 1024
N_META = 16
PAD = 112
EPS = 1e-6
CONV_W = 4
LRU_BLOCKS = 4
LRU_BLOCK = 256
RG_LRU_C = 8.0
SSD_HEADS = 16
SSD_P = 64
SSD_N = 128
SSD_Q = 64
SSD_GROUPS = 2
SSD_GW = 512
SSD_CONV_DIM = 1536
SB_HEADS = 16
SB_D = 64
SB_T = 128
EVEN_IN = 4624
EVEN_NP = 4864
N_DEV = 8

ADAM_LR = 0.001
ADAM_B1 = 0.9
ADAM_B2 = 0.999
ADAM_EPS = 1e-08
ADAM_WD = 0.01
ADAM_STEP = 10

VMEM_LIMIT = 56 * 1024 * 1024


def _cparams(*sem):
    return pltpu.CompilerParams(dimension_semantics=sem, vmem_limit_bytes=VMEM_LIMIT)


def _row_tile(t):
    for c in (512, 256, 128):
        if t % c == 0:
            return c
    raise ValueError(t)


def _dot(a, b):
    return jnp.dot(a, b, preferred_element_type=F32)


def _dot_nt(a, b):
    return lax.dot_general(a, b, (((1,), (1,)), ((), ())), preferred_element_type=F32)


def _dot_tn(a, b):
    return lax.dot_general(a, b, (((0,), (0,)), ((), ())), preferred_element_type=F32)


def _split3(a):
    a1 = a.astype(BF16)
    r1 = a - a1.astype(F32)
    a2 = r1.astype(BF16)
    a3 = (r1 - a2.astype(F32)).astype(BF16)
    return a1, a2, a3


def _xdot_r(a, m01):
    a1, a2, a3 = _split3(a)
    return _dot(a1, m01) + _dot(a2, m01) + _dot(a3, m01)


def _xdot_l(m01, a):
    a1, a2, a3 = _split3(a)
    return _dot(m01, a1) + _dot(m01, a2) + _dot(m01, a3)


def _sigmoid(x):
    return 0.5 * jnp.tanh(0.5 * x) + 0.5


def _softplus(x):
    return jnp.maximum(x, 0.0) + jnp.log(1.0 + jnp.exp(-jnp.abs(x)))


def _iota(shape, dim):
    return lax.broadcasted_iota(jnp.int32, shape, dim)


def _block_scan(a_ref, b_ref, o_ref, reverse=False):
    n, w = a_ref.shape
    nb = n // 8
    unroll = 4 if nb % 4 == 0 else 1
    row = _iota((8, w), 0)

    def block(blk, carry):
        rows = pl.ds(pl.multiple_of(blk * 8, 8), 8)
        a, b = a_ref[rows, :], b_ref[rows, :]
        for k in (1, 2, 4):
            keep = (row < 8 - k) if reverse else (row >= k)
            shift = 8 - k if reverse else k
            a_sh = jnp.where(keep, pltpu.roll(a, shift, 0), 1.0)
            b_sh = jnp.where(keep, pltpu.roll(b, shift, 0), 0.0)
            b = a * b_sh + b
            a = a * a_sh
        o = a * carry + b
        o_ref[rows, :] = o
        return o[0:1, :] if reverse else o[7:8, :]

    def trip(i, carry):
        for u in range(unroll):
            j = i * unroll + u
            carry = block(nb - 1 - j if reverse else j, carry)
        return carry

    lax.fori_loop(0, nb // unroll, trip, jnp.zeros((1, w), F32))


def _chunks(n, c):
    return [(s, min(c, n - s)) for s in range(0, n, c)]


def _mm_nn(a, b, out_dtype, name, res=None):
    m, k = a.shape
    _, n = b.shape
    tm = _row_tile(m)

    def body(*refs):
        if res is None:
            a_ref, b_ref, o_ref = refs
        else:
            a_ref, b_ref, r_ref, o_ref = refs
        av = a_ref[...].astype(BF16)
        for s, w in _chunks(n, 512):
            acc = _dot(av, b_ref[:, s:s + w])
            if res is not None:
                acc = acc + r_ref[:, s:s + w]
            o_ref[:, s:s + w] = acc.astype(out_dtype)

    in_specs = [pl.BlockSpec((tm, k), lambda i: (i, 0)), pl.BlockSpec((k, n), lambda i: (0, 0))]
    args = [a, b]
    if res is not None:
        in_specs.append(pl.BlockSpec((tm, n), lambda i: (i, 0)))
        args.append(res)
    return pl.pallas_call(
        body, grid=(m // tm,), in_specs=in_specs,
        out_specs=pl.BlockSpec((tm, n), lambda i: (i, 0)),
        out_shape=jax.ShapeDtypeStruct((m, n), out_dtype),
        compiler_params=_cparams("parallel"), name=name)(*args)


def _mm_tn(a, b, name):
    t, m = a.shape
    _, n = b.shape
    tk = _row_tile(t)
    halves = 2 if (m * n * 4 > 8 * 1024 * 1024 and n % 256 == 0) else 1
    tn = n // halves

    def body(a_ref, b_ref, o_ref):
        @pl.when(pl.program_id(1) == 0)
        def _():
            o_ref[...] = jnp.zeros_like(o_ref)
        at = a_ref[...].astype(BF16).T
        for s, w in _chunks(tn, 512):
            o_ref[:, s:s + w] += _dot(at, b_ref[:, s:s + w].astype(BF16))

    return pl.pallas_call(
        body, grid=(halves, t // tk),
        in_specs=[pl.BlockSpec((tk, m), lambda j, s: (s, 0)), pl.BlockSpec((tk, tn), lambda j, s: (s, j))],
        out_specs=pl.BlockSpec((m, tn), lambda j, s: (0, j)),
        out_shape=jax.ShapeDtypeStruct((m, n), F32),
        compiler_params=_cparams("parallel", "arbitrary"), name=name)(a, b)


def _norm_fwd(h, w, name):
    t, d = h.shape
    tm = _row_tile(t)

    def body(h_ref, w_ref, u_ref):
        x = h_ref[...]
        r = lax.rsqrt(jnp.mean(x * x, axis=-1, keepdims=True) + EPS)
        u_ref[...] = (x * r * w_ref[...]).astype(BF16)

    return pl.pallas_call(
        body, grid=(t // tm,),
        in_specs=[pl.BlockSpec((tm, d), lambda i: (i, 0)), pl.BlockSpec((1, d), lambda i: (0, 0))],
        out_specs=pl.BlockSpec((tm, d), lambda i: (i, 0)),
        out_shape=jax.ShapeDtypeStruct((t, d), BF16),
        compiler_params=_cparams("parallel"), name=name)(h, w)


def _norm_bwd(h, w, du, dres, name):
    t, d = h.shape
    tm = _row_tile(t)

    def body(h_ref, w_ref, du_ref, dr_ref, dh_ref, dw_ref):
        @pl.when(pl.program_id(0) == 0)
        def _():
            dw_ref[...] = jnp.zeros_like(dw_ref)
        x = h_ref[...]
        r = lax.rsqrt(jnp.mean(x * x, axis=-1, keepdims=True) + EPS)
        xh = x * r
        du_ = du_ref[...]
        g = du_ * w_ref[...]
        dh_ref[...] = dr_ref[...] + r * (g - xh * jnp.mean(g * xh, axis=-1, keepdims=True))
        dw_ref[...] += jnp.sum(du_ * xh, axis=0, keepdims=True)

    return pl.pallas_call(
        body, grid=(t // tm,),
        in_specs=[pl.BlockSpec((tm, d), lambda i: (i, 0)), pl.BlockSpec((1, d), lambda i: (0, 0)),
                  pl.BlockSpec((tm, d), lambda i: (i, 0)), pl.BlockSpec((tm, d), lambda i: (i, 0))],
        out_specs=[pl.BlockSpec((tm, d), lambda i: (i, 0)), pl.BlockSpec((1, d), lambda i: (0, 0))],
        out_shape=[jax.ShapeDtypeStruct((t, d), F32), jax.ShapeDtypeStruct((1, d), F32)],
        compiler_params=_cparams("arbitrary"), name=name)(h, w, du, dres)


def _final_loss(h2, w, tgt, lp):
    t, d = h2.shape
    bsz, seq, _ = tgt.shape
    nblk = lp // SB_T

    def body(h_ref, w_ref, t_ref, loss_ref, dh_ref, dw_ref):
        b, i = pl.program_id(0), pl.program_id(1)

        @pl.when((b == 0) & (i == 0))
        def _():
            loss_ref[...] = jnp.zeros_like(loss_ref)
            dw_ref[...] = jnp.zeros_like(dw_ref)

        @pl.when(i == 0)
        def _():
            dh_ref[...] = jnp.zeros_like(dh_ref)

        @pl.when(i > 0)
        def _():
            x = h_ref[...]
            r = lax.rsqrt(jnp.mean(x * x, axis=-1, keepdims=True) + EPS)
            xh = x * r
            wv = w_ref[...]
            diff = xh * wv - t_ref[0]
            loss_ref[...] += 0.5 * jnp.sum(jnp.mean(diff * diff, axis=-1, keepdims=True), axis=0, keepdims=True)
            dy = diff * (1.0 / d)
            g = dy * wv
            dh_ref[...] = r * (g - xh * jnp.mean(g * xh, axis=-1, keepdims=True))
            dw_ref[...] += jnp.sum(dy * xh, axis=0, keepdims=True)

    return pl.pallas_call(
        body, grid=(bsz, nblk),
        in_specs=[pl.BlockSpec((SB_T, d), lambda b, i: (b * nblk + i, 0)), pl.BlockSpec((1, d), lambda b, i: (0, 0)),
                  pl.BlockSpec((1, SB_T, d), lambda b, i: (b, jnp.maximum(i - 1, 0), 0))],
        out_specs=[pl.BlockSpec((1, 128), lambda b, i: (0, 0)), pl.BlockSpec((SB_T, d), lambda b, i: (b * nblk + i, 0)),
                   pl.BlockSpec((1, d), lambda b, i: (0, 0))],
        out_shape=[jax.ShapeDtypeStruct((1, 128), F32), jax.ShapeDtypeStruct((t, d), F32), jax.ShapeDtypeStruct((1, d), F32)],
        compiler_params=_cparams("arbitrary", "arbitrary"), name="final_loss")(h2, w, tgt)


def _shift_down(x, j):
    return x if j == 0 else pltpu.roll(x, j, 0)


def _shift_up(x, j):
    return x if j == 0 else pltpu.roll(x, x.shape[0] - j, 0)


def _conv(x, cw, cb):
    out = cb + cw[CONV_W - 1:CONV_W, :] * x
    for j in range(1, CONV_W):
        out = out + cw[CONV_W - 1 - j:CONV_W - j, :] * _shift_down(x, j)
    return out


def _conv_bwd_x(dy, cw):
    out = cw[CONV_W - 1:CONV_W, :] * dy
    for j in range(1, CONV_W):
        out = out + cw[CONV_W - 1 - j:CONV_W - j, :] * _shift_up(dy, j)
    return out


def _conv_bwd_w(dcw_ref, dy, x):
    for j in range(CONV_W):
        k = CONV_W - 1 - j
        dcw_ref[k:k + 1, :] += jnp.sum(dy * _shift_down(x, j), axis=0, keepdims=True)


def _lru_forward_block(x, cw, cb, wa, ba, wx, bx, lam, a_ref, b_ref, h_ref):
    lp = x.shape[0]
    lx = _conv(x, cw, cb)
    lxb = lx.astype(BF16)
    r = _sigmoid(_dot(lxb, wa) + ba)
    i = _sigmoid(_dot(lxb, wx) + bx)
    spl = _softplus(-lam)
    log_a = (-RG_LRU_C) * r * spl
    a = jnp.exp(log_a)
    mult = jnp.sqrt(-jnp.tanh(log_a) * (1.0 + a * a))
    valid = _iota((lp, 1), 0) >= PAD
    a_ref[...] = a
    b_ref[...] = jnp.where(valid, mult * i * lx, 0.0)
    _block_scan(a_ref, b_ref, h_ref)
    return lx, lxb, r, i, spl, a, mult, valid


def _lru_specs(lp):
    w = LRU_BLOCK
    return [
        pl.BlockSpec((lp, w), lambda g, b: (b, g)),
        pl.BlockSpec((lp, w), lambda g, b: (b, LRU_BLOCKS + g)),
        pl.BlockSpec((CONV_W, w), lambda g, b: (0, g)),
        pl.BlockSpec((1, w), lambda g, b: (0, g)),
        pl.BlockSpec((1, w, w), lambda g, b: (g, 0, 0)),
        pl.BlockSpec((1, w), lambda g, b: (0, g)),
        pl.BlockSpec((1, w, w), lambda g, b: (g, 0, 0)),
        pl.BlockSpec((1, w), lambda g, b: (0, g)),
        pl.BlockSpec((1, w), lambda g, b: (0, g)),
    ]


def _lru_fwd(proj, cw, cb, wa, ba, wx, bx, lam, bsz, lp):
    w = LRU_BLOCK

    def body(x_ref, g_ref, cw_ref, cb_ref, wa_ref, ba_ref, wx_ref, bx_ref, lam_ref, y_ref, a_s, b_s, h_s):
        _lru_forward_block(x_ref[...], cw_ref[...], cb_ref[...], wa_ref[0], ba_ref[...], wx_ref[0], bx_ref[...],
                           lam_ref[...], a_s, b_s, h_s)
        gate = g_ref[...]
        y_ref[...] = (h_s[...] * gate * _sigmoid(gate)).astype(BF16)

    return pl.pallas_call(
        body, grid=(LRU_BLOCKS, bsz), in_specs=_lru_specs(lp),
        out_specs=pl.BlockSpec((lp, w), lambda g, b: (b, g)),
        out_shape=jax.ShapeDtypeStruct((bsz * lp, 2 * LRU_BLOCKS * w), BF16),
        scratch_shapes=[pltpu.VMEM((lp, w), F32)] * 3,
        compiler_params=_cparams("parallel", "arbitrary"), name="lru_fwd")(proj, proj, cw, cb, wa, ba, wx, bx, lam)


def _lru_bwd(proj, dmixed, cw, cb, wa, ba, wx, bx, lam, bsz, lp):
    w = LRU_BLOCK

    def body(x_ref, g_ref, cw_ref, cb_ref, wa_ref, ba_ref, wx_ref, bx_ref, lam_ref, dy_ref,
             dx_ref, dg_ref, dcw_ref, dcb_ref, dwa_ref, dba_ref, dwx_ref, dbx_ref, dlam_ref, a_s, b_s, h_s, dh_s):
        @pl.when(pl.program_id(1) == 0)
        def _():
            for ref in (dcw_ref, dcb_ref, dwa_ref, dba_ref, dwx_ref, dbx_ref, dlam_ref):
                ref[...] = jnp.zeros_like(ref)

        x = x_ref[...]
        cwv = cw_ref[...]
        wav, wxv, lam_ = wa_ref[0], wx_ref[0], lam_ref[...]
        lx, lxb, r, i, spl, a, mult, valid = _lru_forward_block(
            x, cwv, cb_ref[...], wav, ba_ref[...], wxv, bx_ref[...], lam_, a_s, b_s, h_s)
        gate = g_ref[...]
        sg = _sigmoid(gate)
        dy = dy_ref[...]
        h = h_s[...]
        dg_ref[...] = (dy * h * sg * (1.0 + gate * (1.0 - sg))).astype(BF16)
        b_s[...] = dy * gate * sg
        a_s[...] = jnp.where(_iota((lp, 1), 0) < lp - 1, _shift_up(a, 1), 0.0)
        _block_scan(a_s, b_s, dh_s, reverse=True)
        dh = dh_s[...]
        hprev = jnp.where(_iota((lp, 1), 0) >= 1, _shift_down(h, 1), 0.0)
        db = jnp.where(valid, dh, 0.0)
        dmult = db * i * lx
        di = db * mult * lx
        dlx = db * mult * i
        inv_mult = jnp.where(mult > 0.0, 1.0 / mult, 0.0)
        dlog_a = dh * hprev * a - dmult * (a * a) * inv_mult
        drp = dlog_a * ((-RG_LRU_C) * spl) * r * (1.0 - r)
        dip = di * i * (1.0 - i)
        dspl = jnp.sum(dlog_a * ((-RG_LRU_C) * r), axis=0, keepdims=True)
        dlam_ref[...] += dspl * (-_sigmoid(-lam_))
        dba_ref[...] += jnp.sum(drp, axis=0, keepdims=True)
        dbx_ref[...] += jnp.sum(dip, axis=0, keepdims=True)
        drpb, dipb = drp.astype(BF16), dip.astype(BF16)
        dwa_ref[0] += _dot_tn(lxb, drpb)
        dwx_ref[0] += _dot_tn(lxb, dipb)
        dlx = dlx + _dot_nt(drpb, wav) + _dot_nt(dipb, wxv)
        dcb_ref[...] += jnp.sum(dlx, axis=0, keepdims=True)
        _conv_bwd_w(dcw_ref, dlx, x)
        dx_ref[...] = _conv_bwd_x(dlx, cwv).astype(BF16)

    t = bsz * lp
    vec = pl.BlockSpec((1, w), lambda g, b: (0, g))
    mat = pl.BlockSpec((1, w, w), lambda g, b: (g, 0, 0))
    act = pl.BlockSpec((lp, w), lambda g, b: (b, g))
    return pl.pallas_call(
        body, grid=(LRU_BLOCKS, bsz), in_specs=_lru_specs(lp) + [act],
        out_specs=[act, act, pl.BlockSpec((CONV_W, w), lambda g, b: (0, g)), vec, mat, vec, mat, vec, vec],
        out_shape=[jax.ShapeDtypeStruct((t, 1024), BF16), jax.ShapeDtypeStruct((t, 1024), BF16),
                   jax.ShapeDtypeStruct((CONV_W, 1024), F32), jax.ShapeDtypeStruct((1, 1024), F32),
                   jax.ShapeDtypeStruct((LRU_BLOCKS, w, w), F32), jax.ShapeDtypeStruct((1, 1024), F32),
                   jax.ShapeDtypeStruct((LRU_BLOCKS, w, w), F32), jax.ShapeDtypeStruct((1, 1024), F32),
                   jax.ShapeDtypeStruct((1, 1024), F32)],
        scratch_shapes=[pltpu.VMEM((lp, w), F32)] * 4,
        compiler_params=_cparams("parallel", "arbitrary"), name="lru_bwd")(
            proj, proj, cw, cb, wa, ba, wx, bx, lam, dmixed)


def _sb_masks():
    j = _iota((SB_T, 2 * SB_T), 0)
    s = _iota((SB_T, 2 * SB_T), 1)
    right = ((s < SB_T) & (j > s)) | (s >= SB_T)
    left = ((s < SB_T) & (j < s)) | (s >= SB_T)
    return jnp.where(right, -1.0, 0.0).astype(BF16), left.astype(BF16)


def _xdot2(a, m01):
    a1 = a.astype(BF16)
    a2 = (a - a1.astype(F32)).astype(BF16)
    return _dot(a1, m01) + _dot(a2, m01)


def _sb_query_block(lp):
    for tq in (512, 256, 128):
        if (lp - SB_T) % tq == 0:
            return tq, (lp - SB_T) // tq
    raise ValueError(lp)


def _sb_key_rows(kj):
    return pl.ds(kj * SB_T if isinstance(kj, int) else pl.multiple_of(kj * SB_T, SB_T), SB_T)


def _sb_valid(q0, tq, kj):
    t_pos = q0 + _iota((tq, SB_T), 0)
    s_pos = kj * SB_T + _iota((tq, SB_T), 1)
    return (s_pos < t_pos) & (s_pos >= PAD)


def _sb_by_head(ref, rows):
    t = ref[rows, :]
    head0 = _iota((SB_T, 128), 1) < SB_D
    return jnp.concatenate([jnp.where(head0, t, 0.0), jnp.where(head0, 0.0, t)], axis=0).astype(BF16)


def _sb_tile(qb, k_ref, q0, kj, carries, m_right_neg, masked):
    rows_k = _sb_key_rows(kj)
    z2 = _dot_nt(qb, _sb_by_head(k_ref, rows_k))
    valid = _sb_valid(q0, qb.shape[0], kj) if masked else None
    out = []
    for h in range(2):
        z = z2[:, h * SB_T:(h + 1) * SB_T]
        sp = _softplus(z)
        rs = _dot((jnp.where(valid, sp, 0.0) if masked else sp).astype(BF16), m_right_neg)
        lb = z - sp
        wgt = jnp.exp(lb + rs[:, :SB_T] + carries[h])
        if masked:
            wgt = jnp.where(valid, wgt, 0.0)
        out.append((lb, wgt, carries[h] + rs[:, SB_T:]))
    return rows_k, valid, out


def _tail(x, row0):
    return x if row0 == 0 else x[row0:]


def _merge(old, new_tail, row0):
    return new_tail if row0 == 0 else jnp.concatenate([old[:row0], new_tail], axis=0)


def _sb_sweep(step, c, blk, tq, first, leftwards):
    if first:
        return step(0, True, 0)(c)
    r = tq // SB_T
    u = 4 if r % 4 == 0 else 1
    lo = 1 + r * blk

    def diag(c):
        for d in (range(r - 1, -1, -1) if leftwards else range(r)):
            c = step(lo + d, True, SB_T * d)(c)
        return c

    def inner(c):
        def trip(i, c):
            for j in range(u):
                d = u * i + j
                c = step(lo - 1 - d if leftwards else 1 + d, False, 0)(c)
            return c
        return lax.fori_loop(0, (r // u) * blk, trip, c)

    if leftwards:
        return step(0, True, 0)(inner(diag(c)))
    return diag(inner(step(0, True, 0)(c)))


def _sb_specs(lp):
    nh = SB_HEADS // 2
    return [pl.BlockSpec((lp, 128), lambda b, p: (b, p)), pl.BlockSpec((lp, 128), lambda b, p: (b, nh + p)),
            pl.BlockSpec((lp, 128), lambda b, p: (b, 2 * nh + p)), pl.BlockSpec((lp, 128), lambda b, p: (b, 3 * nh + p))]


def _sb_fwd(qkvg, bsz, lp):
    tq, nb = _sb_query_block(lp)
    scale = SB_D ** -0.5

    def body(q_ref, k_ref, v_ref, g_ref, o_ref, og_ref):
        m_right_neg, _ = _sb_masks()

        def q_block(blk, rows, first):
            q0 = 0 if first else (1 + (rows // SB_T) * blk) * SB_T
            rows_q = pl.ds(q0 if first else pl.multiple_of(q0, SB_T), rows)
            qb = (q_ref[rows_q, :] * scale).astype(BF16)

            def step(kj, masked, row0):
                def run(c):
                    acc, car0, car1 = c
                    rows_k, _, ((_, w0, new0), (_, w1, new1)) = _sb_tile(
                        _tail(qb, row0), k_ref, q0 + row0, kj, (_tail(car0, row0), _tail(car1, row0)), m_right_neg, masked)
                    w2 = jnp.concatenate([w0.astype(BF16), w1.astype(BF16)], axis=1)
                    upd = _tail(acc, row0) + _dot(w2, _sb_by_head(v_ref, rows_k))
                    return _merge(acc, upd, row0), _merge(car0, new0, row0), _merge(car1, new1, row0)
                return run

            zero = jnp.zeros((rows, SB_T), F32)
            o_ref[rows_q, :] = _sb_sweep(step, (zero, zero, zero), blk, rows, first, True)[0]

        q_block(0, SB_T, True)

        def big_block(i, _):
            q_block(i, tq, False)
            return 0

        lax.fori_loop(0, nb, big_block, 0)
        gate = g_ref[...]
        og_ref[...] = (o_ref[...] * gate * _sigmoid(gate)).astype(BF16)

    t = bsz * lp
    blk = pl.BlockSpec((lp, 128), lambda b, p: (b, p))
    return pl.pallas_call(
        body, grid=(bsz, SB_HEADS // 2), in_specs=_sb_specs(lp), out_specs=[blk, blk],
        out_shape=[jax.ShapeDtypeStruct((t, 1024), F32), jax.ShapeDtypeStruct((t, 1024), BF16)],
        compiler_params=_cparams("parallel", "parallel"), name="sb_fwd")(qkvg, qkvg, qkvg, qkvg)


def _sb_bwd(qkvg, o, dog, bsz, lp):
    tq, nb = _sb_query_block(lp)
    nk = lp // SB_T
    scale = SB_D ** -0.5

    def body(q_ref, k_ref, v_ref, g_ref, o_ref, dog_ref, dq_ref, dk_ref, dv_ref, dg_ref, do_s, dk_s, dv_s, e_s, sig_s):
        m_right_neg, m_left = _sb_masks()
        gate = g_ref[...]
        sg = _sigmoid(gate)
        dog = dog_ref[...]
        do_s[...] = dog * gate * sg
        dg_ref[...] = (dog * o_ref[...] * sg * (1.0 + gate * (1.0 - sg))).astype(BF16)
        dk_s[...] = jnp.zeros_like(dk_s)
        dv_s[...] = jnp.zeros_like(dv_s)
        head0_rows = _iota((128, SB_T), 0) < SB_D

        def add_transposed(acc_ref, kj, lhs_t, both):
            res = _dot(lhs_t, both)
            acc_ref[kj] += jnp.where(head0_rows, res[:, :SB_T], res[:, SB_T:])

        def q_block(blk, rows, first):
            q0 = 0 if first else (1 + (rows // SB_T) * blk) * SB_T
            rows_q = pl.ds(q0 if first else pl.multiple_of(q0, SB_T), rows)
            qf = q_ref[rows_q, :] * scale
            qb = qf.astype(BF16)
            q_t = qf.T.astype(BF16)
            do_f = do_s[rows_q, :]
            do_b = do_f.astype(BF16)
            do_t = do_f.T.astype(BF16)

            def left(kj, masked, row0):
                def run(c):
                    rows_k, _, heads = _sb_tile(_tail(qb, row0), k_ref, q0 + row0, kj, (_tail(c[0], row0), _tail(c[1], row0)),
                                                m_right_neg, masked)
                    dw2 = _dot_nt(_tail(do_b, row0), _sb_by_head(v_ref, rows_k))
                    for h, (lb, wgt, _) in enumerate(heads):
                        e_s[2 * kj + h, row0:rows, :] = wgt * dw2[:, h * SB_T:(h + 1) * SB_T]
                        sig_s[2 * kj + h, row0:rows, :] = jnp.exp(lb)
                    add_transposed(dv_s, kj, do_t[:, row0:], jnp.concatenate([hd[1].astype(BF16) for hd in heads], axis=1))
                    return _merge(c[0], heads[0][2], row0), _merge(c[1], heads[1][2], row0)
                return run

            zero = jnp.zeros((rows, SB_T), F32)
            _sb_sweep(left, (zero, zero), blk, rows, first, True)

            def right(kj, masked, row0):
                def run(c):
                    dq, ecar = c[0], list(c[1:])
                    rows_k = _sb_key_rows(kj)
                    valid = _sb_valid(q0 + row0, rows - row0, kj) if masked else None
                    dzs = []
                    for h in range(2):
                        e = e_s[2 * kj + h, row0:rows, :]
                        se = _dot(e.astype(BF16), m_left)
                        ec = _tail(ecar[h], row0)
                        dz = e - sig_s[2 * kj + h, row0:rows, :] * (e + se[:, :SB_T] + ec)
                        if masked:
                            dz = jnp.where(valid, dz, 0.0)
                        dzs.append(dz.astype(BF16))
                        ecar[h] = _merge(ecar[h], ec + se[:, SB_T:], row0)
                    dz2 = jnp.concatenate(dzs, axis=1)
                    add_transposed(dk_s, kj, q_t[:, row0:], dz2)
                    upd = _tail(dq, row0) + _dot(dz2, _sb_by_head(k_ref, rows_k))
                    return _merge(dq, upd, row0), ecar[0], ecar[1]
                return run

            dq = _sb_sweep(right, (zero, zero, zero), blk, rows, first, False)[0]
            dq_ref[rows_q, :] = (dq * scale).astype(BF16)

        q_block(0, SB_T, True)

        def big_block(i, _):
            q_block(i, tq, False)
            return 0

        lax.fori_loop(0, nb, big_block, 0)
        for kj in range(nk):
            dk_ref[kj * SB_T:(kj + 1) * SB_T, :] = dk_s[kj].T.astype(BF16)
            dv_ref[kj * SB_T:(kj + 1) * SB_T, :] = dv_s[kj].T.astype(BF16)

    t = bsz * lp
    blk = pl.BlockSpec((lp, 128), lambda b, p: (b, p))
    shp = jax.ShapeDtypeStruct((t, 1024), BF16)
    return pl.pallas_call(
        body, grid=(bsz, SB_HEADS // 2), in_specs=_sb_specs(lp) + [blk, blk], out_specs=[blk] * 4,
        out_shape=[shp] * 4,
        scratch_shapes=[pltpu.VMEM((lp, 128), F32)] + [pltpu.VMEM((nk, 128, SB_T), F32)] * 2
        + [pltpu.VMEM((2 * nk, tq, SB_T), F32)] * 2,
        compiler_params=_cparams("parallel", "parallel"), name="sb_bwd")(qkvg, qkvg, qkvg, qkvg, o, dog)


XBC_COL0 = 3072 // 256
DT_COL0 = 4608 // 128
DT_L = 128


def _ssd_pre_fwd(proj, cw, cb, bsz, lp):
    def body(x_ref, cw_ref, cb_ref, o_ref):
        pre = _conv(x_ref[...], cw_ref[...], cb_ref[...])
        o_ref[...] = pre * _sigmoid(pre)

    return pl.pallas_call(
        body, grid=(bsz, SSD_CONV_DIM // 256),
        in_specs=[pl.BlockSpec((lp, 256), lambda b, j: (b, XBC_COL0 + j)), pl.BlockSpec((CONV_W, 256), lambda b, j: (0, j)),
                  pl.BlockSpec((1, 256), lambda b, j: (0, j))],
        out_specs=pl.BlockSpec((lp, 256), lambda b, j: (b, j)),
        out_shape=jax.ShapeDtypeStruct((bsz * lp, SSD_CONV_DIM), F32),
        compiler_params=_cparams("parallel", "parallel"), name="ssd_pre_fwd")(proj, cw, cb)


def _ssd_pre_bwd(proj, dact, cw, cb, bsz, lp):
    def body(x_ref, d_ref, cw_ref, cb_ref, dx_ref, dcw_ref, dcb_ref):
        @pl.when(pl.program_id(1) == 0)
        def _():
            dcw_ref[...] = jnp.zeros_like(dcw_ref)
            dcb_ref[...] = jnp.zeros_like(dcb_ref)
        x = x_ref[...]
        cwv = cw_ref[...]
        pre = _conv(x, cwv, cb_ref[...])
        s = _sigmoid(pre)
        dpre = d_ref[...].astype(F32) * s * (1.0 + pre * (1.0 - s))
        dcb_ref[...] += jnp.sum(dpre, axis=0, keepdims=True)
        _conv_bwd_w(dcw_ref, dpre, x)
        dx_ref[...] = _conv_bwd_x(dpre, cwv).astype(BF16)

    return pl.pallas_call(
        body, grid=(SSD_CONV_DIM // 256, bsz),
        in_specs=[pl.BlockSpec((lp, 256), lambda j, b: (b, XBC_COL0 + j)), pl.BlockSpec((lp, 256), lambda j, b: (b, j)),
                  pl.BlockSpec((CONV_W, 256), lambda j, b: (0, j)), pl.BlockSpec((1, 256), lambda j, b: (0, j))],
        out_specs=[pl.BlockSpec((lp, 256), lambda j, b: (b, j)), pl.BlockSpec((CONV_W, 256), lambda j, b: (0, j)),
                   pl.BlockSpec((1, 256), lambda j, b: (0, j))],
        out_shape=[jax.ShapeDtypeStruct((bsz * lp, SSD_CONV_DIM), BF16), jax.ShapeDtypeStruct((CONV_W, SSD_CONV_DIM), F32),
                   jax.ShapeDtypeStruct((1, SSD_CONV_DIM), F32)],
        compiler_params=_cparams("parallel", "arbitrary"), name="ssd_pre_bwd")(proj, dact, cw, cb)


def _split2(a):
    a1 = a.astype(BF16)
    return a1, (a - a1.astype(F32)).astype(BF16)


def _xdot2_nt(a, m01):
    a1, a2 = _split2(a)
    return _dot_nt(a1, m01) + _dot_nt(a2, m01)


def _xdot2_l(m01, a):
    a1, a2 = _split2(a)
    return _dot(m01, a1) + _dot(m01, a2)


class _SsdConsts:
    def __init__(self, g):
        q, gw = SSD_Q, SSD_GW
        head_of_lane = lax.shift_right_logical(_iota((DT_L, gw), 1), 6)
        self.sel = (_iota((DT_L, gw), 0) == 8 * g + head_of_lane).astype(BF16)
        r = _iota((q, gw), 0)
        c = jnp.bitwise_and(_iota((q, gw), 1), q - 1)
        self.diag = r == c
        self.diag_b = self.diag.astype(BF16)
        self.lower = c <= r
        self.upper = c >= r
        self.bd = lax.shift_right_logical(_iota((gw, gw), 0), 6) == lax.shift_right_logical(_iota((gw, gw), 1), 6)
        jj, ll = _iota((q, q), 1), _iota((q, q), 0)
        self.tri = (jj <= ll).astype(BF16)
        self.tri_t = (jj >= ll).astype(BF16)
        self.last = _iota((q, 1), 0) == q - 1


def _ssd_prepass(k, dt_ref, bias_ref, alog_ref, dtm_s, abc_s, lp):
    valid = _iota((lp, 1), 0) >= PAD
    sp_in = dt_ref[...] + bias_ref[...]
    dtm = jnp.where(valid, _softplus(sp_in), 0.0)
    aneg = -jnp.exp(alog_ref[...])
    dtm_s[...] = dtm
    abc_s[...] = _xdot_r(dtm * aneg, k.sel)
    return valid, sp_in, aneg


def _ssd_loop(nc, chunk, init):
    u = 2 if nc % 2 == 0 else 1

    def trip(i, carry):
        for j in range(u):
            carry = chunk(i * u + j, carry)
        return carry

    return lax.fori_loop(0, nc // u, trip, init)


def _ssd_chunk(k, rows, xs_ref, b_ref, c_ref, dtm_s, abc_s):
    bc = _xdot_l(k.tri, abc_s[rows, :])
    tt = jnp.sum(jnp.where(k.diag, bc, 0.0), axis=0, keepdims=True)
    dtbc = _xdot2(dtm_s[rows, :], k.sel)
    xs = xs_ref[rows, :]
    x = xs * dtbc
    bb = b_ref[rows, :].astype(BF16)
    cc = c_ref[rows, :].astype(BF16)
    tot = bc[SSD_Q - 1:SSD_Q, :]
    xbd = jnp.where(k.bd, jnp.concatenate([x] * 8, axis=0), 0.0).astype(BF16)
    return bc, tt, dtbc, xs, x, bb, cc, tot, xbd


def _ssd_specs(lp, order):
    ix = (lambda f: (lambda b, g: f(b, g))) if order == "bg" else (lambda f: (lambda g, b: f(b, g)))
    return [pl.BlockSpec((lp, SSD_GW), ix(lambda b, g: (b, g))),
            pl.BlockSpec((lp, SSD_N), ix(lambda b, g: (b, 1024 // SSD_N + g))),
            pl.BlockSpec((lp, SSD_N), ix(lambda b, g: (b, 1280 // SSD_N + g))),
            pl.BlockSpec((lp, DT_L), ix(lambda b, g: (b, DT_COL0))),
            pl.BlockSpec((1, DT_L), ix(lambda b, g: (0, 0))),
            pl.BlockSpec((1, DT_L), ix(lambda b, g: (0, 0))),
            pl.BlockSpec((1, DT_L), ix(lambda b, g: (0, 0)))]


def _ssd_fwd(xbc, proj, dt_bias, a_log, dskip, bsz, lp):
    nc = lp // SSD_Q

    def body(xs_ref, b_ref, c_ref, dt_ref, bias_ref, alog_ref, dsk_ref, y_ref, dtm_s, abc_s):
        k = _SsdConsts(pl.program_id(1))
        _ssd_prepass(k, dt_ref, bias_ref, alog_ref, dtm_s, abc_s, lp)
        dvec = _xdot_r(jnp.broadcast_to(dsk_ref[...], (8, DT_L)), k.sel)[0:1, :]

        def chunk(c, state):
            rows = pl.ds(pl.multiple_of(c * SSD_Q, SSD_Q), SSD_Q)
            bc, tt, _, xs, x, bb, cc, tot, xbd = _ssd_chunk(k, rows, xs_ref, b_ref, c_ref, dtm_s, abc_s)
            lm = jnp.where(k.lower, jnp.exp(jnp.minimum(bc - tt, 0.0)), 0.0)
            g_all = _dot(_dot_nt(cc, bb).astype(BF16), k.diag_b) * lm
            y = _dot(g_all.astype(BF16), xbd) + jnp.exp(bc) * _dot(cc, state.astype(BF16)) + dvec * xs
            y_ref[rows, :] = y
            return jnp.exp(tot) * state + _dot_tn(bb, (jnp.exp(tot - bc) * x).astype(BF16))

        _ssd_loop(nc, chunk, jnp.zeros((SSD_N, SSD_GW), F32))

    return pl.pallas_call(
        body, grid=(bsz, SSD_GROUPS), in_specs=_ssd_specs(lp, "bg"),
        out_specs=pl.BlockSpec((lp, SSD_GW), lambda b, g: (b, g)),
        out_shape=jax.ShapeDtypeStruct((bsz * lp, 1024), F32),
        scratch_shapes=[pltpu.VMEM((lp, DT_L), F32), pltpu.VMEM((lp, SSD_GW), F32)],
        compiler_params=_cparams("parallel", "parallel"), name="ssd_fwd")(xbc, xbc, xbc, proj, dt_bias, a_log, dskip)


def _ssd_bwd(xbc, proj, dt_bias, a_log, dskip, dy, bsz, lp):
    nc = lp // SSD_Q

    def body(xs_ref, b_ref, c_ref, dt_ref, bias_ref, alog_ref, dsk_ref, dy_ref,
             dxs_ref, db_ref, dc_ref, ddt_ref, dbias_ref, dalog_ref, ddsk_ref, dtm_s, abc_s, st_s):
        @pl.when(pl.program_id(1) == 0)
        def _():
            for ref in (dbias_ref, dalog_ref, ddsk_ref):
                ref[...] = jnp.zeros_like(ref)

        k = _SsdConsts(pl.program_id(0))
        valid, sp_in, aneg = _ssd_prepass(k, dt_ref, bias_ref, alog_ref, dtm_s, abc_s, lp)
        dvec = _xdot_r(jnp.broadcast_to(dsk_ref[...], (8, DT_L)), k.sel)[0:1, :]

        def fwd_chunk(c, state):
            rows = pl.ds(pl.multiple_of(c * SSD_Q, SSD_Q), SSD_Q)
            st_s[c] = state.astype(BF16)
            bc, _, _, _, x, bb, _, tot, _ = _ssd_chunk(k, rows, xs_ref, b_ref, c_ref, dtm_s, abc_s)
            return jnp.exp(tot) * state + _dot_tn(bb, (jnp.exp(tot - bc) * x).astype(BF16))

        _ssd_loop(nc, fwd_chunk, jnp.zeros((SSD_N, SSD_GW), F32))

        def bwd_chunk(i, carry):
            dstate, daneg, ddsk = carry
            c = nc - 1 - i
            rows = pl.ds(pl.multiple_of(c * SSD_Q, SSD_Q), SSD_Q)
            bc, tt, dtbc, xs, x, bb, cc, tot, xbd = _ssd_chunk(k, rows, xs_ref, b_ref, c_ref, dtm_s, abc_s)
            sprev = st_s[c]
            dyc = dy_ref[rows, :]
            dyb = dyc.astype(BF16)
            seg = bc - tt
            lm = jnp.where(k.lower, jnp.exp(jnp.minimum(seg, 0.0)), 0.0)
            lm_t = jnp.where(k.upper, jnp.exp(jnp.minimum(-seg, 0.0)), 0.0)
            cb_all = _dot(_dot_nt(cc, bb).astype(BF16), k.diag_b)
            cbt_all = _dot(_dot_nt(bb, cc).astype(BF16), k.diag_b)
            g_all = cb_all * lm
            dybd = jnp.where(k.bd, jnp.concatenate([dyc] * 8, axis=0), 0.0).astype(BF16)
            dg = _dot_nt(dyb, xbd)
            dx = _dot((cbt_all * lm_t).astype(BF16), dybd)
            hh = dg * g_all
            ea = jnp.exp(bc)
            yo = ea * _dot(cc, sprev)
            col_h = jnp.sum(hh, axis=0, keepdims=True)
            dcb = _dot_nt((dg * lm).astype(BF16), k.diag_b).astype(BF16)
            dcs = (ea * dyc).astype(BF16)
            dstb = dstate.astype(BF16)
            dec = jnp.exp(tot - bc)
            w = dec * x
            dw = _dot(bb, dstb)
            dc_ref[rows, :] = (_dot(dcb, bb) + _dot_nt(dcs, sprev)).astype(BF16)
            db_ref[rows, :] = (_dot_tn(dcb, cc) + _dot_nt(w.astype(BF16), dstb)).astype(BF16)
            dx = dx + dec * dw
            etot = jnp.exp(tot)
            r8 = _iota((8, SSD_GW), 0)
            st_row = jnp.sum(dstate * sprev.astype(F32), axis=0, keepdims=True) * etot
            sk_row = jnp.sum(dyc * xs, axis=0, keepdims=True)
            small = jnp.where(r8 == 0, st_row, jnp.where(r8 == 1, sk_row, 0.0))
            q = SSD_Q
            sums = _xdot2_nt(jnp.concatenate([hh + dyc * yo - jnp.where(k.diag, col_h, 0.0) - dw * w, dw * w, dx * xs, small],
                                             axis=0), k.sel)
            kk = sums[q:2 * q]
            dtot = sums[3 * q:3 * q + 1] + jnp.sum(kk, axis=0, keepdims=True)
            dacum = sums[0:q] + jnp.where(k.last, dtot, 0.0)
            da = _xdot2_l(k.tri_t, dacum)
            dtm_c = dtm_s[rows, :]
            ddtm = da * aneg + sums[2 * q:3 * q]
            vrow = (c * SSD_Q + _iota((SSD_Q, 1), 0)) >= PAD
            ddt_ref[rows, :] = jnp.where(vrow, ddtm * _sigmoid(dt_ref[rows, :] + bias_ref[...]), 0.0)
            dxs_ref[rows, :] = (dx * dtbc + dvec * dyc).astype(BF16)
            daneg = daneg + jnp.sum(da * dtm_c, axis=0, keepdims=True)
            ddsk = ddsk + sums[3 * q + 1:3 * q + 2]
            dstate = etot * dstate + _dot_tn(cc, dcs)
            return dstate, daneg, ddsk

        zrow = jnp.zeros((1, DT_L), F32)
        _, daneg, ddsk = _ssd_loop(nc, bwd_chunk, (jnp.zeros((SSD_N, SSD_GW), F32), zrow, zrow))
        dbias_ref[...] += jnp.broadcast_to(jnp.sum(ddt_ref[...], axis=0, keepdims=True), (8, DT_L))
        dalog_ref[...] += jnp.broadcast_to(daneg * aneg, (8, DT_L))
        ddsk_ref[...] += jnp.broadcast_to(ddsk, (8, DT_L))

    t = bsz * lp
    par = pl.BlockSpec((8, DT_L), lambda g, b: (g, 0))
    par_shape = jax.ShapeDtypeStruct((8 * SSD_GROUPS, DT_L), F32)
    return pl.pallas_call(
        body, grid=(SSD_GROUPS, bsz),
        in_specs=_ssd_specs(lp, "gb") + [pl.BlockSpec((lp, SSD_GW), lambda g, b: (b, g))],
        out_specs=[pl.BlockSpec((lp, SSD_GW), lambda g, b: (b, g)), pl.BlockSpec((lp, SSD_N), lambda g, b: (b, g)),
                   pl.BlockSpec((lp, SSD_N), lambda g, b: (b, g)), pl.BlockSpec((lp, DT_L), lambda g, b: (b, g)), par, par, par],
        out_shape=[jax.ShapeDtypeStruct((t, 1024), BF16), jax.ShapeDtypeStruct((t, 256), BF16),
                   jax.ShapeDtypeStruct((t, 256), BF16), jax.ShapeDtypeStruct((t, SSD_GROUPS * DT_L), F32),
                   par_shape, par_shape, par_shape],
        scratch_shapes=[pltpu.VMEM((lp, DT_L), F32), pltpu.VMEM((lp, SSD_GW), F32), pltpu.VMEM((nc, SSD_N, SSD_GW), BF16)],
        compiler_params=_cparams("parallel", "arbitrary"), name="ssd_bwd")(
            xbc, xbc, xbc, proj, dt_bias, a_log, dskip, dy)


Z_COL0 = 2048 // SSD_GW


def _gnorm_fwd(y, proj, w, mixed, name="gnorm_fwd"):
    t = y.shape[0]
    tm = _row_tile(t)

    def body(y_ref, z_ref, w_ref, mixed_ref, o_ref):
        z = z_ref[...]
        gt = y_ref[...] * z * _sigmoid(z)
        r = lax.rsqrt(jnp.mean(gt * gt, axis=-1, keepdims=True) + EPS)
        o_ref[...] = (gt * r * w_ref[...]).astype(BF16)

    return pl.pallas_call(
        body, grid=(t // tm, SSD_GROUPS),
        in_specs=[pl.BlockSpec((tm, SSD_GW), lambda i, g: (i, g)), pl.BlockSpec((tm, SSD_GW), lambda i, g: (i, Z_COL0 + g)),
                  pl.BlockSpec((1, SSD_GW), lambda i, g: (0, g)), ANY_SPEC],
        out_specs=pl.BlockSpec((tm, SSD_GW), lambda i, g: (i, 1024 // SSD_GW + g)),
        out_shape=jax.ShapeDtypeStruct((t, 2048), BF16), input_output_aliases={3: 0},
        compiler_params=_cparams("parallel", "parallel"), name=name)(y, proj, w, mixed)


def _gnorm_bwd(y, proj, w, dmixed):
    t = y.shape[0]
    tm = _row_tile(t)

    def body(y_ref, z_ref, w_ref, d_ref, dy_ref, dz_ref, dw_ref):
        @pl.when(pl.program_id(1) == 0)
        def _():
            dw_ref[...] = jnp.zeros_like(dw_ref)
        z, yv, d = z_ref[...], y_ref[...], d_ref[...]
        s = _sigmoid(z)
        sz = z * s
        gt = yv * sz
        r = lax.rsqrt(jnp.mean(gt * gt, axis=-1, keepdims=True) + EPS)
        gh = gt * r
        dgn = d * w_ref[...]
        dgt = r * (dgn - gh * jnp.mean(dgn * gh, axis=-1, keepdims=True))
        dw_ref[...] += jnp.sum(d * gh, axis=0, keepdims=True)
        dy_ref[...] = dgt * sz
        dz_ref[...] = (dgt * yv * s * (1.0 + z * (1.0 - s))).astype(BF16)

    blk = pl.BlockSpec((tm, SSD_GW), lambda g, i: (i, g))
    return pl.pallas_call(
        body, grid=(SSD_GROUPS, t // tm),
        in_specs=[blk, pl.BlockSpec((tm, SSD_GW), lambda g, i: (i, Z_COL0 + g)), pl.BlockSpec((1, SSD_GW), lambda g, i: (0, g)),
                  pl.BlockSpec((tm, SSD_GW), lambda g, i: (i, 1024 // SSD_GW + g))],
        out_specs=[blk, blk, pl.BlockSpec((1, SSD_GW), lambda g, i: (0, g))],
        out_shape=[jax.ShapeDtypeStruct((t, 1024), F32), jax.ShapeDtypeStruct((t, 1024), BF16),
                   jax.ShapeDtypeStruct((1, 1024), F32)],
        compiler_params=_cparams("parallel", "arbitrary"), name="gnorm_bwd")(y, proj, w, dmixed)


def _meta_grad(dh0, bsz, lp):
    def body(d_ref, o_ref):
        @pl.when(pl.program_id(0) == 0)
        def _():
            o_ref[...] = jnp.zeros_like(o_ref)
        o_ref[...] += d_ref[...]

    return pl.pallas_call(
        body, grid=(bsz,),
        in_specs=[pl.BlockSpec((N_META, D_MODEL), lambda b: (b * (lp // N_META) + PAD // N_META, 0))],
        out_specs=pl.BlockSpec((N_META, D_MODEL), lambda b: (0, 0)),
        out_shape=jax.ShapeDtypeStruct((N_META, D_MODEL), F32),
        compiler_params=_cparams("arbitrary"), name="meta_grad")(dh0)


OTHER_CHIPS = ((1, 0), (0, 1), (1, 1))
ANY_SPEC = pl.BlockSpec(memory_space=pl.ANY)


class _Mesh:
    def __init__(self):
        self.x, self.y, self.c = lax.axis_index("x"), lax.axis_index("y"), lax.axis_index("c")

    def dev(self, fx, fy, fc):
        return (1 - self.x if fx else self.x, 1 - self.y if fy else self.y, 1 - self.c if fc else self.c)

    def slot(self, fx, fy, fc):
        px, py, pc = self.dev(fx, fy, fc)
        return 4 * px + 2 * py + pc

    def chip(self, fx, fy):
        px, py, _ = self.dev(fx, fy, 0)
        return 2 * px + py


def _remote(src, dst, send_sems, recv_sems, idx, dev):
    return pltpu.make_async_remote_copy(src_ref=src, dst_ref=dst, send_sem=send_sems.at[idx], recv_sem=recv_sems.at[idx],
                                        device_id=dev, device_id_type=pl.DeviceIdType.MESH)


def _gather_all(arrs, name):
    n = len(arrs)

    def body(*refs):
        ins, outs = refs[:n], refs[n:2 * n]
        send_sems, recv_sems = refs[2 * n:]
        me = _Mesh()
        first, passed = [], []
        for a in range(n):
            mine = outs[a].at[me.slot(0, 0, 0)]
            first.append(_remote(ins[a], mine, send_sems, recv_sems, (a, 0), me.dev(0, 0, 1)))
            for j, (fx, fy) in enumerate(OTHER_CHIPS):
                first.append(_remote(ins[a], mine, send_sems, recv_sems, (a, 1 + j), me.dev(fx, fy, 0)))
        for cp in first:
            cp.start()
        for a in range(n):
            for j, (fx, fy) in enumerate(OTHER_CHIPS):
                blk = outs[a].at[me.slot(fx, fy, 0)]
                _remote(ins[a], blk, send_sems, recv_sems, (a, 1 + j), me.dev(fx, fy, 0)).wait_recv()
                cp = _remote(blk, blk, send_sems, recv_sems, (a, 4 + j), me.dev(0, 0, 1))
                cp.start()
                passed.append(cp)
        for a in range(n):
            _remote(ins[a], outs[a].at[me.slot(0, 0, 1)], send_sems, recv_sems, (a, 0), me.dev(0, 0, 1)).wait_recv()
            for j, (fx, fy) in enumerate(OTHER_CHIPS):
                _remote(ins[a], outs[a].at[me.slot(fx, fy, 1)], send_sems, recv_sems, (a, 4 + j), me.dev(0, 0, 1)).wait_recv()
        for cp in first + passed:
            cp.wait_send()

    stacks = pl.pallas_call(
        body, in_specs=[ANY_SPEC] * n, out_specs=[ANY_SPEC] * n,
        out_shape=[jax.ShapeDtypeStruct((N_DEV,) + a.shape, a.dtype) for a in arrs],
        scratch_shapes=[pltpu.SemaphoreType.DMA((n, 7)), pltpu.SemaphoreType.DMA((n, 7))],
        compiler_params=pltpu.CompilerParams(has_side_effects=True), name=name)(*arrs)
    me = 4 * lax.axis_index("x") + 2 * lax.axis_index("y") + lax.axis_index("c")
    return [lax.dynamic_update_slice_in_dim(s, a[None], me, axis=0) for s, a in zip(stacks, arrs)]


def _swap_in_chip(arrs, name):
    n = len(arrs)

    def body(*refs):
        ins, got = refs[:n], refs[n:2 * n]
        send_sems, recv_sems = refs[2 * n:]
        me = _Mesh()
        sends = [_remote(ins[a].at[2 * j + (1 - me.c)], got[a].at[j], send_sems, recv_sems, (a, j), me.dev(0, 0, 1))
                 for a in range(n) for j in range(4)]
        for cp in sends:
            cp.start()
        for cp in sends:
            cp.wait()

    return pl.pallas_call(
        body, in_specs=[ANY_SPEC] * n, out_specs=[ANY_SPEC] * n,
        out_shape=[jax.ShapeDtypeStruct((4,) + a.shape[1:], a.dtype) for a in arrs],
        scratch_shapes=[pltpu.SemaphoreType.DMA((n, 4)), pltpu.SemaphoreType.DMA((n, 4))],
        compiler_params=pltpu.CompilerParams(has_side_effects=True), name=name)(*arrs)


def _add_pair(a, b, name):
    _, r, c = a.shape
    tr = _adam_rows(r)

    def body(a_ref, b_ref, o_ref):
        o_ref[...] = (a_ref[...].astype(F32) + b_ref[...].astype(F32)).astype(o_ref.dtype)

    blk = pl.BlockSpec((4, tr, c), lambda i: (0, i, 0))
    return pl.pallas_call(body, grid=(r // tr,), in_specs=[blk, blk], out_specs=blk,
                          out_shape=jax.ShapeDtypeStruct(a.shape, a.dtype), compiler_params=_cparams("parallel"), name=name)(a, b)


def _swap_chips(sums, gather, name):
    ns, ng = len(sums), len(gather)
    n = ns + ng

    def body(*refs):
        ins, outs = refs[:n], refs[n:2 * n]
        send_sems, recv_sems, local_sems = refs[2 * n:]
        me = _Mesh()
        local, sends, recvs = [], [], []
        for a in range(ns):
            here = me.chip(0, 0)
            for j, (fx, fy) in enumerate(OTHER_CHIPS):
                there = me.chip(fx, fy)
                sends.append(_remote(ins[a].at[there], outs[a].at[here], send_sems, recv_sems, (a, j), me.dev(fx, fy, 0)))
                recvs.append(_remote(ins[a].at[there], outs[a].at[there], send_sems, recv_sems, (a, j), me.dev(fx, fy, 0)))
        for a in range(ns, n):
            local.append(pltpu.make_async_copy(ins[a], outs[a].at[me.slot(0, 0, 0)], local_sems.at[a]))
            for k in range(1, N_DEV):
                flips = (k >> 2 & 1, k >> 1 & 1, k & 1)
                sends.append(_remote(ins[a], outs[a].at[me.slot(0, 0, 0)], send_sems, recv_sems, (a, k - 1), me.dev(*flips)))
                recvs.append(_remote(ins[a], outs[a].at[me.slot(*flips)], send_sems, recv_sems, (a, k - 1), me.dev(*flips)))
        for cp in local + sends:
            cp.start()
        for cp in recvs:
            cp.wait_recv()
        for cp in sends:
            cp.wait_send()
        for cp in local:
            cp.wait()

    out_shape = [jax.ShapeDtypeStruct(a.shape, a.dtype) for a in sums]
    out_shape += [jax.ShapeDtypeStruct((N_DEV,) + a.shape, a.dtype) for a in gather]
    outs = pl.pallas_call(
        body, in_specs=[ANY_SPEC] * n, out_specs=[ANY_SPEC] * n, out_shape=out_shape,
        scratch_shapes=[pltpu.SemaphoreType.DMA((n, N_DEV - 1)), pltpu.SemaphoreType.DMA((n, N_DEV - 1)),
                        pltpu.SemaphoreType.DMA((n,))],
        compiler_params=pltpu.CompilerParams(has_side_effects=True), name=name)(*sums, *gather)
    here = 2 * lax.axis_index("x") + lax.axis_index("y")
    mine = [lax.dynamic_update_slice_in_dim(o, lax.dynamic_slice_in_dim(s, here, 1, axis=0), here, axis=0)
            for o, s in zip(outs[:ns], sums)]
    return mine + list(outs[ns:])


def _adam_rows(r):
    for cand in (128, 64, 32, 16, 8):
        if r % cand == 0 and r > cand:
            return cand
    return r


def _adamw(parts, w, m, v, name):
    r, c = w.shape
    tr = _adam_rows(r)
    n_parts = parts.shape[0]

    def body(p_ref, w_ref, m_ref, v_ref, g_ref, d_ref, nm_ref, nv_ref):
        g = p_ref[0].astype(F32)
        for k in range(1, n_parts):
            g = g + p_ref[k].astype(F32)
        m_new = ADAM_B1 * m_ref[...] + (1.0 - ADAM_B1) * g
        v_new = ADAM_B2 * v_ref[...] + (1.0 - ADAM_B2) * (g * g)
        m_hat = m_new / (1.0 - ADAM_B1 ** ADAM_STEP)
        v_hat = v_new / (1.0 - ADAM_B2 ** ADAM_STEP)
        g_ref[...] = g
        d_ref[...] = -ADAM_LR * (m_hat / (jnp.sqrt(v_hat) + ADAM_EPS) + ADAM_WD * w_ref[...])
        nm_ref[...] = m_new
        nv_ref[...] = v_new

    blk = pl.BlockSpec((tr, c), lambda i: (i, 0))
    shp = jax.ShapeDtypeStruct((r, c), F32)
    return pl.pallas_call(
        body, grid=(r // tr,), in_specs=[pl.BlockSpec((n_parts, tr, c), lambda i: (0, i, 0)), blk, blk, blk],
        out_specs=[blk] * 4, out_shape=[shp] * 4, compiler_params=_cparams("parallel"), name=name)(parts, w, m, v)


def _rows128(a):
    return a.reshape(-1, 128)


def _pad_rows(a, rows):
    return jnp.pad(a, ((0, rows - a.shape[0]), (0, 0)))


def _lane16(a):
    return jnp.pad(a.reshape(1, -1), ((0, 0), (0, 128 - a.size)))


SHARD_PACK_ROWS = 544
REPL_PACK_ROWS = 72
LOSS_ROW = 71


def _pack_shard(meta, lru_conv_w, odd_norm, ssd_conv_w, lru_w_a, lru_w_x):
    parts = [meta.reshape(16, 128), lru_conv_w.reshape(4, 128), odd_norm.reshape(1, 128), _rows128(ssd_conv_w.reshape(4, 192)),
             _rows128(lru_w_a.reshape(4, 32, 256)), _rows128(lru_w_x.reshape(4, 32, 256))]
    return _pad_rows(jnp.concatenate(parts, axis=0), SHARD_PACK_ROWS)


def _unpack_shard(p):
    return (p[0:16], p[16:20].reshape(1, 4, 128), p[20:21], p[21:27].reshape(1, 4, 192),
            p[27:283].reshape(1, 4, 32, 256), p[283:539].reshape(1, 4, 32, 256))


def _pack_repl(even_norm, lru_conv_b, lru_b_a, lru_b_x, lru_lambda, ssd_norm, final_norm, ssd_conv_b, dt_bias, a_log, ssd_d):
    parts = [_rows128(v) for v in (even_norm, lru_conv_b, lru_b_a, lru_b_x, lru_lambda, ssd_norm, final_norm, ssd_conv_b)]
    parts += [_lane16(dt_bias), _lane16(a_log), _lane16(ssd_d)]
    return _pad_rows(jnp.concatenate(parts, axis=0), REPL_PACK_ROWS)


def _unpack_repl(p):
    vec = lambda i: p[8 * i:8 * i + 8].reshape(1, 1024)
    return (vec(0), vec(1), vec(2), vec(3), vec(4), vec(5), p[48:56].reshape(1024), p[56:68].reshape(1, 1536),
            p[68:69, :16], p[69:70, :16], p[70:71, :16])


def _local_step(x, tgt, meta, even_norm, w_in_e, lru_conv_w, lru_conv_b, lru_w_a, lru_b_a, lru_w_x, lru_b_x, lru_lambda,
                ssd_conv_w, ssd_conv_b, dt_bias, a_log, ssd_d, ssd_norm, w_out_e, odd_norm, w_in_o, w_out_o, final_norm):
    bsz, seq, d = x.shape
    lp = PAD + N_META + seq
    t = bsz * lp
    h0 = jnp.concatenate([jnp.zeros((bsz, PAD, d), F32), jnp.broadcast_to(meta[None], (bsz, N_META, d)), x], axis=1).reshape(t, d)

    u0 = _norm_fwd(h0, even_norm, "norm0_fwd")
    proj = _mm_nn(u0, w_in_e, F32, "even_in")
    lru = (lru_conv_w, lru_conv_b, lru_w_a, lru_b_a, lru_w_x, lru_b_x, lru_lambda)
    ya = _lru_fwd(proj, *lru, bsz, lp)
    xbc = _ssd_pre_fwd(proj, ssd_conv_w, ssd_conv_b, bsz, lp)
    y = _ssd_fwd(xbc, proj, dt_bias, a_log, ssd_d, bsz, lp)
    mixed = _gnorm_fwd(y, proj, ssd_norm, ya)
    h1 = _mm_nn(mixed, w_out_e, F32, "even_out", res=h0)
    u2 = _norm_fwd(h1, odd_norm, "norm1_fwd")
    qkvg = _mm_nn(u2, w_in_o, F32, "odd_in")
    o, og = _sb_fwd(qkvg, bsz, lp)
    h2 = _mm_nn(og, w_out_o, F32, "odd_out", res=h1)
    loss, dh2, d_final_norm = _final_loss(h2, final_norm, tgt, lp)

    dog = _mm_nn(dh2, w_out_o.T, F32, "odd_out_dx")
    d_w_out_o = _mm_tn(og, dh2, "odd_out_dw")
    dqkvg = jnp.concatenate(_sb_bwd(qkvg, o, dog, bsz, lp), axis=1)
    du2 = _mm_nn(dqkvg, w_in_o.T, F32, "odd_in_dx")
    d_w_in_o = _mm_tn(u2, dqkvg, "odd_in_dw")
    dh1, d_odd_norm = _norm_bwd(h1, odd_norm, du2, dh2, "norm1_bwd")

    dmixed = _mm_nn(dh1, w_out_e.T, F32, "even_out_dx")
    d_w_out_e = _mm_tn(mixed, dh1, "even_out_dw")
    dlx, dgate, d_lru_conv_w, d_lru_conv_b, d_w_a, d_b_a, d_w_x, d_b_x, d_lambda = _lru_bwd(proj, dmixed, *lru, bsz, lp)
    dy, dz, d_ssd_norm = _gnorm_bwd(y, proj, ssd_norm, dmixed)
    dxs, dbm, dcm, ddt, d_dt_bias, d_a_log, d_ssd_d = _ssd_bwd(xbc, proj, dt_bias, a_log, ssd_d, dy, bsz, lp)
    dxbc, d_ssd_conv_w, d_ssd_conv_b = _ssd_pre_bwd(proj, jnp.concatenate([dxs, dbm, dcm], axis=1), ssd_conv_w, ssd_conv_b, bsz, lp)
    ddt = (ddt[:, :DT_L] + ddt[:, DT_L:]).astype(BF16)
    dproj = jnp.concatenate([dlx, dgate, dz, dxbc, ddt, jnp.zeros((t, EVEN_NP - 4608 - DT_L), BF16)], axis=1)
    du0 = _mm_nn(dproj, w_in_e.T, F32, "even_in_dx")
    d_w_in_e = _mm_tn(u0, dproj, "even_in_dw")
    dh0, d_even_norm = _norm_bwd(h0, even_norm, du0, dh1, "norm0_bwd")
    grad_x = dh0.reshape(bsz, lp, d)[:, PAD + N_META:]
    d_meta = _meta_grad(dh0, bsz, lp)
    heads = lambda p: (p[0:1] + p[8:9])[:, :SSD_HEADS]
    grads = dict(meta=d_meta, even_norm=d_even_norm, even_w_in=d_w_in_e[:, :EVEN_IN], lru_conv_w=d_lru_conv_w,
                 lru_conv_b=d_lru_conv_b, lru_w_a=d_w_a, lru_b_a=d_b_a, lru_w_x=d_w_x, lru_b_x=d_b_x, lru_lambda=d_lambda,
                 ssd_conv_w=d_ssd_conv_w, ssd_conv_b=d_ssd_conv_b, ssd_dt_bias=heads(d_dt_bias), ssd_a_log=heads(d_a_log),
                 ssd_d=heads(d_ssd_d), ssd_norm=d_ssd_norm, even_w_out=d_w_out_e, odd_norm=d_odd_norm, odd_w_in=d_w_in_o,
                 odd_w_out=d_w_out_o, final_norm=d_final_norm)
    return loss[0, 0], grad_x, grads


WEIGHTS = ['meta', 'even_norm', 'even_w_in', 'lru_conv_w', 'lru_conv_b', 'lru_w_a', 'lru_b_a', 'lru_w_x', 'lru_b_x', 'lru_lambda',
           'ssd_conv_w', 'ssd_conv_b', 'ssd_dt_bias', 'ssd_a_log', 'ssd_d', 'ssd_norm', 'even_w_out', 'odd_norm', 'odd_w_in',
           'odd_w_out', 'final_norm']


def _blocks_of(a, axis):
    shp = a.shape
    a = a.reshape(shp[:axis] + (N_DEV, shp[axis] // N_DEV) + shp[axis + 1:])
    return jnp.moveaxis(a, axis, 0)


def _unblock(a, axis):
    a = jnp.moveaxis(a, 0, axis)
    shp = a.shape
    return a.reshape(shp[:axis] + (shp[axis] * shp[axis + 1],) + shp[axis + 2:])


def kernel(x, meta, even_norm, even_w_in, lru_conv_w, lru_conv_b, lru_w_a, lru_b_a, lru_w_x, lru_b_x, lru_lambda, ssd_conv_w, ssd_conv_b, ssd_dt_bias, ssd_a_log, ssd_d, ssd_norm, even_w_out, odd_norm, odd_w_in, odd_w_out, final_norm, loss_target, m_meta, m_even_norm, m_even_w_in, m_lru_conv_w, m_lru_conv_b, m_lru_w_a, m_lru_b_a, m_lru_w_x, m_lru_b_x, m_lru_lambda, m_ssd_conv_w, m_ssd_conv_b, m_ssd_dt_bias, m_ssd_a_log, m_ssd_d, m_ssd_norm, m_even_w_out, m_odd_norm, m_odd_w_in, m_odd_w_out, m_final_norm, v_meta, v_even_norm, v_even_w_in, v_lru_conv_w, v_lru_conv_b, v_lru_w_a, v_lru_b_a, v_lru_w_x, v_lru_b_x, v_lru_lambda, v_ssd_conv_w, v_ssd_conv_b, v_ssd_dt_bias, v_ssd_a_log, v_ssd_d, v_ssd_norm, v_even_w_out, v_odd_norm, v_odd_w_in, v_odd_w_out, v_final_norm):
    w = dict(meta=meta, even_norm=even_norm, even_w_in=even_w_in, lru_conv_w=lru_conv_w, lru_conv_b=lru_conv_b, lru_w_a=lru_w_a,
             lru_b_a=lru_b_a, lru_w_x=lru_w_x, lru_b_x=lru_b_x, lru_lambda=lru_lambda, ssd_conv_w=ssd_conv_w,
             ssd_conv_b=ssd_conv_b, ssd_dt_bias=ssd_dt_bias, ssd_a_log=ssd_a_log, ssd_d=ssd_d, ssd_norm=ssd_norm,
             even_w_out=even_w_out, odd_norm=odd_norm, odd_w_in=odd_w_in, odd_w_out=odd_w_out, final_norm=final_norm)
    m = dict(meta=m_meta, even_norm=m_even_norm, even_w_in=m_even_w_in, lru_conv_w=m_lru_conv_w, lru_conv_b=m_lru_conv_b,
             lru_w_a=m_lru_w_a, lru_b_a=m_lru_b_a, lru_w_x=m_lru_w_x, lru_b_x=m_lru_b_x, lru_lambda=m_lru_lambda,
             ssd_conv_w=m_ssd_conv_w, ssd_conv_b=m_ssd_conv_b, ssd_dt_bias=m_ssd_dt_bias, ssd_a_log=m_ssd_a_log, ssd_d=m_ssd_d,
             ssd_norm=m_ssd_norm, even_w_out=m_even_w_out, odd_norm=m_odd_norm, odd_w_in=m_odd_w_in, odd_w_out=m_odd_w_out,
             final_norm=m_final_norm)
    v = dict(meta=v_meta, even_norm=v_even_norm, even_w_in=v_even_w_in, lru_conv_w=v_lru_conv_w, lru_conv_b=v_lru_conv_b,
             lru_w_a=v_lru_w_a, lru_b_a=v_lru_b_a, lru_w_x=v_lru_w_x, lru_b_x=v_lru_b_x, lru_lambda=v_lru_lambda,
             ssd_conv_w=v_ssd_conv_w, ssd_conv_b=v_ssd_conv_b, ssd_dt_bias=v_ssd_dt_bias, ssd_a_log=v_ssd_a_log, ssd_d=v_ssd_d,
             ssd_norm=v_ssd_norm, even_w_out=v_even_w_out, odd_norm=v_odd_norm, odd_w_in=v_odd_w_in, odd_w_out=v_odd_w_out,
             final_norm=v_final_norm)
    shard_names = ('meta', 'lru_conv_w', 'odd_norm', 'ssd_conv_w', 'lru_w_a', 'lru_w_x')
    repl_names = ('even_norm', 'lru_conv_b', 'lru_b_a', 'lru_b_x', 'lru_lambda', 'ssd_norm', 'final_norm', 'ssd_conv_b',
                  'ssd_dt_bias', 'ssd_a_log', 'ssd_d')
    big_names = ('even_w_in', 'even_w_out', 'odd_w_in', 'odd_w_out')

    gates = jnp.concatenate([lru_w_a.reshape(128, 256), lru_w_x.reshape(128, 256)], axis=0).astype(BF16)
    small = _pack_shard(*[w[k] for k in shard_names])
    g_in_e, g_out_e, g_in_o, g_out_o, g_gates, g_small = _gather_all(
        [even_w_in[0].astype(BF16), even_w_out[0].astype(BF16), odd_w_in[0].astype(BF16), odd_w_out[0].astype(BF16), gates, small],
        "gather_weights")
    w_in_e = jnp.pad(_unblock(g_in_e, 1), ((0, 0), (0, EVEN_NP - EVEN_IN)))
    w_out_e = g_out_e.reshape(2048, 1024)
    w_in_o = _unblock(g_in_o, 1)
    w_out_o = g_out_o.reshape(1024, 1024)
    gates_full = jnp.moveaxis(g_gates.reshape(N_DEV, 2, 4, 32, 256), 0, 2).reshape(2, 4, 256, 256)
    f_meta = _unblock(g_small[:, 0:16], 1)
    f_lru_conv_w = _unblock(g_small[:, 16:20], 1)
    f_odd_norm = _unblock(g_small[:, 20:21], 1)
    f_ssd_conv_w = _unblock(g_small[:, 21:27].reshape(N_DEV, 4, 192), 1)

    loss, grad_x, g = _local_step(
        x, loss_target, f_meta, even_norm, w_in_e, f_lru_conv_w, lru_conv_b, gates_full[0], lru_b_a, gates_full[1], lru_b_x,
        lru_lambda, f_ssd_conv_w, ssd_conv_b, _lane16(ssd_dt_bias), _lane16(ssd_a_log), _lane16(ssd_d), ssd_norm, w_out_e,
        f_odd_norm, w_in_o, w_out_o, final_norm.reshape(1, -1))

    s_small = jnp.stack([_pack_shard(g['meta'][:, 128 * p:128 * (p + 1)], g['lru_conv_w'][:, 128 * p:128 * (p + 1)],
                                     g['odd_norm'][:, 128 * p:128 * (p + 1)], g['ssd_conv_w'][:, 192 * p:192 * (p + 1)],
                                     g['lru_w_a'][:, 32 * p:32 * (p + 1)], g['lru_w_x'][:, 32 * p:32 * (p + 1)])
                         for p in range(N_DEV)])
    r_pack = _pack_repl(*[g[k] for k in repl_names]).at[LOSS_ROW, 0].set(loss)
    by_owner = [_blocks_of(g['even_w_in'], 1).astype(BF16), g['even_w_out'].reshape(N_DEV, 256, 1024).astype(BF16),
                _blocks_of(g['odd_w_in'], 1).astype(BF16), g['odd_w_out'].reshape(N_DEV, 128, 1024).astype(BF16), s_small]
    got = _swap_in_chip(by_owner, "reduce_in_chip")
    my_c = lax.axis_index("c")
    kept = [lax.dynamic_index_in_dim(a.reshape((4, 2) + a.shape[1:]), my_c, axis=1, keepdims=False) for a in by_owner]
    chip_sums = [_add_pair(a, b, "chip_sum_%d" % i) for i, (a, b) in enumerate(zip(kept, got))]
    p_in_e, p_out_e, p_in_o, p_out_o, p_small, p_repl = _swap_chips(chip_sums, [r_pack], "reduce_across_chips")

    res = {}
    for k, parts in zip(big_names, (p_in_e, p_out_e, p_in_o, p_out_o)):
        outs = _adamw(parts, w[k][0], m[k][0], v[k][0], "adamw_" + k)
        res[k] = [o[None] for o in outs]
    outs = _adamw(p_small, small, _pack_shard(*[m[k] for k in shard_names]), _pack_shard(*[v[k] for k in shard_names]), "adamw_sharded")
    unpacked = [_unpack_shard(o) for o in outs]
    for i, k in enumerate(shard_names):
        res[k] = [u[i] for u in unpacked]
    outs = _adamw(p_repl, _pack_repl(*[w[k] for k in repl_names]), _pack_repl(*[m[k] for k in repl_names]),
                  _pack_repl(*[v[k] for k in repl_names]), "adamw_replicated")
    unpacked = [_unpack_repl(o) for o in outs]
    for i, k in enumerate(repl_names):
        res[k] = [u[i] for u in unpacked]
    loss = outs[0][LOSS_ROW, 0]
    return (loss, grad_x, *[res[k][0] for k in WEIGHTS], *[res[k][1] for k in WEIGHTS], *[res[k][2] for k in WEIGHTS],
            *[res[k][3] for k in WEIGHTS])
```

```python
import jax
import jax.numpy as jnp
from jax import lax
from jax.experimental import pallas as pl
from jax.experimental.pallas import tpu as pltpu

F32 = jnp.float32
BF16 = jnp.bfloat16

D_MODEL = 1024
N_META = 16
PAD = 112
EPS = 1e-6
CONV_W = 4
LRU_BLOCKS = 4
LRU_BLOCK = 256
RG_LRU_C = 8.0
SSD_HEADS = 16
SSD_P = 64
SSD_N = 128
SSD_Q = 64
SSD_GROUPS = 2
SSD_GW = 512
SSD_CONV_DIM = 1536
SB_HEADS = 16
SB_D = 64
SB_T = 128
EVEN_IN = 4624
EVEN_NP = 4864
N_DEV = 8

ADAM_LR = 0.001
ADAM_B1 = 0.9
ADAM_B2 = 0.999
ADAM_EPS = 1e-08
ADAM_WD = 0.01
ADAM_STEP = 10

VMEM_LIMIT = 56 * 1024 * 1024


def _cparams(*sem):
    return pltpu.CompilerParams(dimension_semantics=sem, vmem_limit_bytes=VMEM_LIMIT)


def _row_tile(t):
    for c in (512, 256, 128):
        if t % c == 0:
            return c
    raise ValueError(t)


def _dot(a, b):
    return jnp.dot(a, b, preferred_element_type=F32)


def _dot_nt(a, b):
    return lax.dot_general(a, b, (((1,), (1,)), ((), ())), preferred_element_type=F32)


def _dot_tn(a, b):
    return lax.dot_general(a, b, (((0,), (0,)), ((), ())), preferred_element_type=F32)


def _split3(a):
    a1 = a.astype(BF16)
    r1 = a - a1.astype(F32)
    a2 = r1.astype(BF16)
    a3 = (r1 - a2.astype(F32)).astype(BF16)
    return a1, a2, a3


def _xdot_r(a, m01):
    a1, a2, a3 = _split3(a)
    return _dot(a1, m01) + _dot(a2, m01) + _dot(a3, m01)


def _xdot_l(m01, a):
    a1, a2, a3 = _split3(a)
    return _dot(m01, a1) + _dot(m01, a2) + _dot(m01, a3)


def _sigmoid(x):
    return 0.5 * jnp.tanh(0.5 * x) + 0.5


def _softplus(x):
    return jnp.maximum(x, 0.0) + jnp.log(1.0 + jnp.exp(-jnp.abs(x)))


def _iota(shape, dim):
    return lax.broadcasted_iota(jnp.int32, shape, dim)


def _block_scan(a_ref, b_ref, o_ref, reverse=False):
    n, w = a_ref.shape
    nb = n // 8
    unroll = 4 if nb % 4 == 0 else 1
    row = _iota((8, w), 0)

    def block(blk, carry):
        rows = pl.ds(pl.multiple_of(blk * 8, 8), 8)
        a, b = a_ref[rows, :], b_ref[rows, :]
        for k in (1, 2, 4):
            keep = (row < 8 - k) if reverse else (row >= k)
            shift = 8 - k if reverse else k
            a_sh = jnp.where(keep, pltpu.roll(a, shift, 0), 1.0)
            b_sh = jnp.where(keep, pltpu.roll(b, shift, 0), 0.0)
            b = a * b_sh + b
            a = a * a_sh
        o = a * carry + b
        o_ref[rows, :] = o
        return o[0:1, :] if reverse else o[7:8, :]

    def trip(i, carry):
        for u in range(unroll):
            j = i * unroll + u
            carry = block(nb - 1 - j if reverse else j, carry)
        return carry

    lax.fori_loop(0, nb // unroll, trip, jnp.zeros((1, w), F32))


def _chunks(n, c):
    return [(s, min(c, n - s)) for s in range(0, n, c)]


def _mm_nn(a, b, out_dtype, name, res=None):
    m, k = a.shape
    _, n = b.shape
    tm = _row_tile(m)

    def body(*refs):
        if res is None:
            a_ref, b_ref, o_ref = refs
        else:
            a_ref, b_ref, r_ref, o_ref = refs
        av = a_ref[...].astype(BF16)
        for s, w in _chunks(n, 512):
            acc = _dot(av, b_ref[:, s:s + w])
            if res is not None:
                acc = acc + r_ref[:, s:s + w]
            o_ref[:, s:s + w] = acc.astype(out_dtype)

    in_specs = [pl.BlockSpec((tm, k), lambda i: (i, 0)), pl.BlockSpec((k, n), lambda i: (0, 0))]
    args = [a, b]
    if res is not None:
        in_specs.append(pl.BlockSpec((tm, n), lambda i: (i, 0)))
        args.append(res)
    return pl.pallas_call(
        body, grid=(m // tm,), in_specs=in_specs,
        out_specs=pl.BlockSpec((tm, n), lambda i: (i, 0)),
        out_shape=jax.ShapeDtypeStruct((m, n), out_dtype),
        compiler_params=_cparams("parallel"), name=name)(*args)


def _mm_tn(a, b, name):
    t, m = a.shape
    _, n = b.shape
    tk = _row_tile(t)
    halves = 2 if (m * n * 4 > 8 * 1024 * 1024 and n % 256 == 0) else 1
    tn = n // halves

    def body(a_ref, b_ref, o_ref):
        @pl.when(pl.program_id(1) == 0)
        def _():
            o_ref[...] = jnp.zeros_like(o_ref)
        at = a_ref[...].astype(BF16).T
        for s, w in _chunks(tn, 512):
            o_ref[:, s:s + w] += _dot(at, b_ref[:, s:s + w].astype(BF16))

    return pl.pallas_call(
        body, grid=(halves, t // tk),
        in_specs=[pl.BlockSpec((tk, m), lambda j, s: (s, 0)), pl.BlockSpec((tk, tn), lambda j, s: (s, j))],
        out_specs=pl.BlockSpec((m, tn), lambda j, s: (0, j)),
        out_shape=jax.ShapeDtypeStruct((m, n), F32),
        compiler_params=_cparams("parallel", "arbitrary"), name=name)(a, b)


def _norm_fwd(h, w, name):
    t, d = h.shape
    tm = _row_tile(t)

    def body(h_ref, w_ref, u_ref):
        x = h_ref[...]
        r = lax.rsqrt(jnp.mean(x * x, axis=-1, keepdims=True) + EPS)
        u_ref[...] = (x * r * w_ref[...]).astype(BF16)

    return pl.pallas_call(
        body, grid=(t // tm,),
        in_specs=[pl.BlockSpec((tm, d), lambda i: (i, 0)), pl.BlockSpec((1, d), lambda i: (0, 0))],
        out_specs=pl.BlockSpec((tm, d), lambda i: (i, 0)),
        out_shape=jax.ShapeDtypeStruct((t, d), BF16),
        compiler_params=_cparams("parallel"), name=name)(h, w)


def _norm_bwd(h, w, du, dres, name):
    t, d = h.shape
    tm = _row_tile(t)

    def body(h_ref, w_ref, du_ref, dr_ref, dh_ref, dw_ref):
        @pl.when(pl.program_id(0) == 0)
        def _():
            dw_ref[...] = jnp.zeros_like(dw_ref)
        x = h_ref[...]
        r = lax.rsqrt(jnp.mean(x * x, axis=-1, keepdims=True) + EPS)
        xh = x * r
        du_ = du_ref[...]
        g = du_ * w_ref[...]
        dh_ref[...] = dr_ref[...] + r * (g - xh * jnp.mean(g * xh, axis=-1, keepdims=True))
        dw_ref[...] += jnp.sum(du_ * xh, axis=0, keepdims=True)

    return pl.pallas_call(
        body, grid=(t // tm,),
        in_specs=[pl.BlockSpec((tm, d), lambda i: (i, 0)), pl.BlockSpec((1, d), lambda i: (0, 0)),
                  pl.BlockSpec((tm, d), lambda i: (i, 0)), pl.BlockSpec((tm, d), lambda i: (i, 0))],
        out_specs=[pl.BlockSpec((tm, d), lambda i: (i, 0)), pl.BlockSpec((1, d), lambda i: (0, 0))],
        out_shape=[jax.ShapeDtypeStruct((t, d), F32), jax.ShapeDtypeStruct((1, d), F32)],
        compiler_params=_cparams("arbitrary"), name=name)(h, w, du, dres)


def _final_loss(h2, w, tgt, lp):
    t, d = h2.shape
    bsz, seq, _ = tgt.shape
    nblk = lp // SB_T

    def body(h_ref, w_ref, t_ref, loss_ref, dh_ref, dw_ref):
        b, i = pl.program_id(0), pl.program_id(1)

        @pl.when((b == 0) & (i == 0))
        def _():
            loss_ref[...] = jnp.zeros_like(loss_ref)
            dw_ref[...] = jnp.zeros_like(dw_ref)

        @pl.when(i == 0)
        def _():
            dh_ref[...] = jnp.zeros_like(dh_ref)

        @pl.when(i > 0)
        def _():
            x = h_ref[...]
            r = lax.rsqrt(jnp.mean(x * x, axis=-1, keepdims=True) + EPS)
            xh = x * r
            wv = w_ref[...]
            diff = xh * wv - t_ref[0]
            loss_ref[...] += 0.5 * jnp.sum(jnp.mean(diff * diff, axis=-1, keepdims=True), axis=0, keepdims=True)
            dy = diff * (1.0 / d)
            g = dy * wv
            dh_ref[...] = r * (g - xh * jnp.mean(g * xh, axis=-1, keepdims=True))
            dw_ref[...] += jnp.sum(dy * xh, axis=0, keepdims=True)

    return pl.pallas_call(
        body, grid=(bsz, nblk),
        in_specs=[pl.BlockSpec((SB_T, d), lambda b, i: (b * nblk + i, 0)), pl.BlockSpec((1, d), lambda b, i: (0, 0)),
                  pl.BlockSpec((1, SB_T, d), lambda b, i: (b, jnp.maximum(i - 1, 0), 0))],
        out_specs=[pl.BlockSpec((1, 128), lambda b, i: (0, 0)), pl.BlockSpec((SB_T, d), lambda b, i: (b * nblk + i, 0)),
                   pl.BlockSpec((1, d), lambda b, i: (0, 0))],
        out_shape=[jax.ShapeDtypeStruct((1, 128), F32), jax.ShapeDtypeStruct((t, d), F32), jax.ShapeDtypeStruct((1, d), F32)],
        compiler_params=_cparams("arbitrary", "arbitrary"), name="final_loss")(h2, w, tgt)


def _shift_down(x, j):
    return x if j == 0 else pltpu.roll(x, j, 0)


def _shift_up(x, j):
    return x if j == 0 else pltpu.roll(x, x.shape[0] - j, 0)


def _conv(x, cw, cb):
    out = cb + cw[CONV_W - 1:CONV_W, :] * x
    for j in range(1, CONV_W):
        out = out + cw[CONV_W - 1 - j:CONV_W - j, :] * _shift_down(x, j)
    return out


def _conv_bwd_x(dy, cw):
    out = cw[CONV_W - 1:CONV_W, :] * dy
    for j in range(1, CONV_W):
        out = out + cw[CONV_W - 1 - j:CONV_W - j, :] * _shift_up(dy, j)
    return out


def _conv_bwd_w(dcw_ref, dy, x):
    for j in range(CONV_W):
        k = CONV_W - 1 - j
        dcw_ref[k:k + 1, :] += jnp.sum(dy * _shift_down(x, j), axis=0, keepdims=True)


def _lru_forward_block(x, cw, cb, wa, ba, wx, bx, lam, a_ref, b_ref, h_ref):
    lp = x.shape[0]
    lx = _conv(x, cw, cb)
    lxb = lx.astype(BF16)
    r = _sigmoid(_dot(lxb, wa) + ba)
    i = _sigmoid(_dot(lxb, wx) + bx)
    spl = _softplus(-lam)
    log_a = (-RG_LRU_C) * r * spl
    a = jnp.exp(log_a)
    mult = jnp.sqrt(-jnp.tanh(log_a) * (1.0 + a * a))
    valid = _iota((lp, 1), 0) >= PAD
    a_ref[...] = a
    b_ref[...] = jnp.where(valid, mult * i * lx, 0.0)
    _block_scan(a_ref, b_ref, h_ref)
    return lx, lxb, r, i, spl, a, mult, valid


def _lru_specs(lp):
    w = LRU_BLOCK
    return [
        pl.BlockSpec((lp, w), lambda g, b: (b, g)),
        pl.BlockSpec((lp, w), lambda g, b: (b, LRU_BLOCKS + g)),
        pl.BlockSpec((CONV_W, w), lambda g, b: (0, g)),
        pl.BlockSpec((1, w), lambda g, b: (0, g)),
        pl.BlockSpec((1, w, w), lambda g, b: (g, 0, 0)),
        pl.BlockSpec((1, w), lambda g, b: (0, g)),
        pl.BlockSpec((1, w, w), lambda g, b: (g, 0, 0)),
        pl.BlockSpec((1, w), lambda g, b: (0, g)),
        pl.BlockSpec((1, w), lambda g, b: (0, g)),
    ]


def _lru_fwd(proj, cw, cb, wa, ba, wx, bx, lam, bsz, lp):
    w = LRU_BLOCK

    def body(x_ref, g_ref, cw_ref, cb_ref, wa_ref, ba_ref, wx_ref, bx_ref, lam_ref, y_ref, a_s, b_s, h_s):
        _lru_forward_block(x_ref[...], cw_ref[...], cb_ref[...], wa_ref[0], ba_ref[...], wx_ref[0], bx_ref[...],
                           lam_ref[...], a_s, b_s, h_s)
        gate = g_ref[...]
        y_ref[...] = (h_s[...] * gate * _sigmoid(gate)).astype(BF16)

    return pl.pallas_call(
        body, grid=(LRU_BLOCKS, bsz), in_specs=_lru_specs(lp),
        out_specs=pl.BlockSpec((lp, w), lambda g, b: (b, g)),
        out_shape=jax.ShapeDtypeStruct((bsz * lp, 2 * LRU_BLOCKS * w), BF16),
        scratch_shapes=[pltpu.VMEM((lp, w), F32)] * 3,
        compiler_params=_cparams("parallel", "arbitrary"), name="lru_fwd")(proj, proj, cw, cb, wa, ba, wx, bx, lam)


def _lru_bwd(proj, dmixed, cw, cb, wa, ba, wx, bx, lam, bsz, lp):
    w = LRU_BLOCK

    def body(x_ref, g_ref, cw_ref, cb_ref, wa_ref, ba_ref, wx_ref, bx_ref, lam_ref, dy_ref,
             dx_ref, dg_ref, dcw_ref, dcb_ref, dwa_ref, dba_ref, dwx_ref, dbx_ref, dlam_ref, a_s, b_s, h_s, dh_s):
        @pl.when(pl.program_id(1) == 0)
        def _():
            for ref in (dcw_ref, dcb_ref, dwa_ref, dba_ref, dwx_ref, dbx_ref, dlam_ref):
                ref[...] = jnp.zeros_like(ref)

        x = x_ref[...]
        cwv = cw_ref[...]
        wav, wxv, lam_ = wa_ref[0], wx_ref[0], lam_ref[...]
        lx, lxb, r, i, spl, a, mult, valid = _lru_forward_block(
            x, cwv, cb_ref[...], wav, ba_ref[...], wxv, bx_ref[...], lam_, a_s, b_s, h_s)
        gate = g_ref[...]
        sg = _sigmoid(gate)
        dy = dy_ref[...]
        h = h_s[...]
        dg_ref[...] = (dy * h * sg * (1.0 + gate * (1.0 - sg))).astype(BF16)
        b_s[...] = dy * gate * sg
        a_s[...] = jnp.where(_iota((lp, 1), 0) < lp - 1, _shift_up(a, 1), 0.0)
        _block_scan(a_s, b_s, dh_s, reverse=True)
        dh = dh_s[...]
        hprev = jnp.where(_iota((lp, 1), 0) >= 1, _shift_down(h, 1), 0.0)
        db = jnp.where(valid, dh, 0.0)
        dmult = db * i * lx
        di = db * mult * lx
        dlx = db * mult * i
        inv_mult = jnp.where(mult > 0.0, 1.0 / mult, 0.0)
        dlog_a = dh * hprev * a - dmult * (a * a) * inv_mult
        drp = dlog_a * ((-RG_LRU_C) * spl) * r * (1.0 - r)
        dip = di * i * (1.0 - i)
        dspl = jnp.sum(dlog_a * ((-RG_LRU_C) * r), axis=0, keepdims=True)
        dlam_ref[...] += dspl * (-_sigmoid(-lam_))
        dba_ref[...] += jnp.sum(drp, axis=0, keepdims=True)
        dbx_ref[...] += jnp.sum(dip, axis=0, keepdims=True)
        drpb, dipb = drp.astype(BF16), dip.astype(BF16)
        dwa_ref[0] += _dot_tn(lxb, drpb)
        dwx_ref[0] += _dot_tn(lxb, dipb)
        dlx = dlx + _dot_nt(drpb, wav) + _dot_nt(dipb, wxv)
        dcb_ref[...] += jnp.sum(dlx, axis=0, keepdims=True)
        _conv_bwd_w(dcw_ref, dlx, x)
        dx_ref[...] = _conv_bwd_x(dlx, cwv).astype(BF16)

    t = bsz * lp
    vec = pl.BlockSpec((1, w), lambda g, b: (0, g))
    mat = pl.BlockSpec((1, w, w), lambda g, b: (g, 0, 0))
    act = pl.BlockSpec((lp, w), lambda g, b: (b, g))
    return pl.pallas_call(
        body, grid=(LRU_BLOCKS, bsz), in_specs=_lru_specs(lp) + [act],
        out_specs=[act, act, pl.BlockSpec((CONV_W, w), lambda g, b: (0, g)), vec, mat, vec, mat, vec, vec],
        out_shape=[jax.ShapeDtypeStruct((t, 1024), BF16), jax.ShapeDtypeStruct((t, 1024), BF16),
                   jax.ShapeDtypeStruct((CONV_W, 1024), F32), jax.ShapeDtypeStruct((1, 1024), F32),
                   jax.ShapeDtypeStruct((LRU_BLOCKS, w, w), F32), jax.ShapeDtypeStruct((1, 1024), F32),
                   jax.ShapeDtypeStruct((LRU_BLOCKS, w, w), F32), jax.ShapeDtypeStruct((1, 1024), F32),
                   jax.ShapeDtypeStruct((1, 1024), F32)],
        scratch_shapes=[pltpu.VMEM((lp, w), F32)] * 4,
        compiler_params=_cparams("parallel", "arbitrary"), name="lru_bwd")(
            proj, proj, cw, cb, wa, ba, wx, bx, lam, dmixed)


def _sb_masks():
    j = _iota((SB_T, 2 * SB_T), 0)
    s = _iota((SB_T, 2 * SB_T), 1)
    right = ((s < SB_T) & (j > s)) | (s >= SB_T)
    left = ((s < SB_T) & (j < s)) | (s >= SB_T)
    return jnp.where(right, -1.0, 0.0).astype(BF16), left.astype(BF16)


def _xdot2(a, m01):
    a1 = a.astype(BF16)
    a2 = (a - a1.astype(F32)).astype(BF16)
    return _dot(a1, m01) + _dot(a2, m01)


def _sb_query_block(lp):
    for tq in (512, 256, 128):
        if (lp - SB_T) % tq == 0:
            return tq, (lp - SB_T) // tq
    raise ValueError(lp)


def _sb_key_rows(kj):
    return pl.ds(kj * SB_T if isinstance(kj, int) else pl.multiple_of(kj * SB_T, SB_T), SB_T)


def _sb_valid(q0, tq, kj):
    t_pos = q0 + _iota((tq, SB_T), 0)
    s_pos = kj * SB_T + _iota((tq, SB_T), 1)
    return (s_pos < t_pos) & (s_pos >= PAD)


def _sb_by_head(ref, rows):
    t = ref[rows, :]
    head0 = _iota((SB_T, 128), 1) < SB_D
    return jnp.concatenate([jnp.where(head0, t, 0.0), jnp.where(head0, 0.0, t)], axis=0).astype(BF16)


def _sb_tile(qb, k_ref, q0, kj, carries, m_right_neg, masked):
    rows_k = _sb_key_rows(kj)
    z2 = _dot_nt(qb, _sb_by_head(k_ref, rows_k))
    valid = _sb_valid(q0, qb.shape[0], kj) if masked else None
    out = []
    for h in range(2):
        z = z2[:, h * SB_T:(h + 1) * SB_T]
        sp = _softplus(z)
        rs = _dot((jnp.where(valid, sp, 0.0) if masked else sp).astype(BF16), m_right_neg)
        lb = z - sp
        wgt = jnp.exp(lb + rs[:, :SB_T] + carries[h])
        if masked:
            wgt = jnp.where(valid, wgt, 0.0)
        out.append((lb, wgt, carries[h] + rs[:, SB_T:]))
    return rows_k, valid, out


def _tail(x, row0):
    return x if row0 == 0 else x[row0:]


def _merge(old, new_tail, row0):
    return new_tail if row0 == 0 else jnp.concatenate([old[:row0], new_tail], axis=0)


def _sb_sweep(step, c, blk, tq, first, leftwards):
    if first:
        return step(0, True, 0)(c)
    r = tq // SB_T
    u = 4 if r % 4 == 0 else 1
    lo = 1 + r * blk

    def diag(c):
        for d in (range(r - 1, -1, -1) if leftwards else range(r)):
            c = step(lo + d, True, SB_T * d)(c)
        return c

    def inner(c):
        def trip(i, c):
            for j in range(u):
                d = u * i + j
                c = step(lo - 1 - d if leftwards else 1 + d, False, 0)(c)
            return c
        return lax.fori_loop(0, (r // u) * blk, trip, c)

    if leftwards:
        return step(0, True, 0)(inner(diag(c)))
    return diag(inner(step(0, True, 0)(c)))


def _sb_specs(lp):
    nh = SB_HEADS // 2
    return [pl.BlockSpec((lp, 128), lambda b, p: (b, p)), pl.BlockSpec((lp, 128), lambda b, p: (b, nh + p)),
            pl.BlockSpec((lp, 128), lambda b, p: (b, 2 * nh + p)), pl.BlockSpec((lp, 128), lambda b, p: (b, 3 * nh + p))]


def _sb_fwd(qkvg, bsz, lp):
    tq, nb = _sb_query_block(lp)
    scale = SB_D ** -0.5

    def body(q_ref, k_ref, v_ref, g_ref, o_ref, og_ref):
        m_right_neg, _ = _sb_masks()

        def q_block(blk, rows, first):
            q0 = 0 if first else (1 + (rows // SB_T) * blk) * SB_T
            rows_q = pl.ds(q0 if first else pl.multiple_of(q0, SB_T), rows)
            qb = (q_ref[rows_q, :] * scale).astype(BF16)

            def step(kj, masked, row0):
                def run(c):
                    acc, car0, car1 = c
                    rows_k, _, ((_, w0, new0), (_, w1, new1)) = _sb_tile(
                        _tail(qb, row0), k_ref, q0 + row0, kj, (_tail(car0, row0), _tail(car1, row0)), m_right_neg, masked)
                    w2 = jnp.concatenate([w0.astype(BF16), w1.astype(BF16)], axis=1)
                    upd = _tail(acc, row0) + _dot(w2, _sb_by_head(v_ref, rows_k))
                    return _merge(acc, upd, row0), _merge(car0, new0, row0), _merge(car1, new1, row0)
                return run

            zero = jnp.zeros((rows, SB_T), F32)
            o_ref[rows_q, :] = _sb_sweep(step, (zero, zero, zero), blk, rows, first, True)[0]

        q_block(0, SB_T, True)

        def big_block(i, _):
            q_block(i, tq, False)
            return 0

        lax.fori_loop(0, nb, big_block, 0)
        gate = g_ref[...]
        og_ref[...] = (o_ref[...] * gate * _sigmoid(gate)).astype(BF16)

    t = bsz * lp
    blk = pl.BlockSpec((lp, 128), lambda b, p: (b, p))
    return pl.pallas_call(
        body, grid=(bsz, SB_HEADS // 2), in_specs=_sb_specs(lp), out_specs=[blk, blk],
        out_shape=[jax.ShapeDtypeStruct((t, 1024), F32), jax.ShapeDtypeStruct((t, 1024), BF16)],
        compiler_params=_cparams("parallel", "parallel"), name="sb_fwd")(qkvg, qkvg, qkvg, qkvg)


def _sb_bwd(qkvg, o, dog, bsz, lp):
    tq, nb = _sb_query_block(lp)
    nk = lp // SB_T
    scale = SB_D ** -0.5

    def body(q_ref, k_ref, v_ref, g_ref, o_ref, dog_ref, dq_ref, dk_ref, dv_ref, dg_ref, do_s, dk_s, dv_s, e_s, sig_s):
        m_right_neg, m_left = _sb_masks()
        gate = g_ref[...]
        sg = _sigmoid(gate)
        dog = dog_ref[...]
        do_s[...] = dog * gate * sg
        dg_ref[...] = (dog * o_ref[...] * sg * (1.0 + gate * (1.0 - sg))).astype(BF16)
        dk_s[...] = jnp.zeros_like(dk_s)
        dv_s[...] = jnp.zeros_like(dv_s)
        head0_rows = _iota((128, SB_T), 0) < SB_D

        def add_transposed(acc_ref, kj, lhs_t, both):
            res = _dot(lhs_t, both)
            acc_ref[kj] += jnp.where(head0_rows, res[:, :SB_T], res[:, SB_T:])

        def q_block(blk, rows, first):
            q0 = 0 if first else (1 + (rows // SB_T) * blk) * SB_T
            rows_q = pl.ds(q0 if first else pl.multiple_of(q0, SB_T), rows)
            qf = q_ref[rows_q, :] * scale
            qb = qf.astype(BF16)
            q_t = qf.T.astype(BF16)
            do_f = do_s[rows_q, :]
            do_b = do_f.astype(BF16)
            do_t = do_f.T.astype(BF16)

            def left(kj, masked, row0):
                def run(c):
                    rows_k, _, heads = _sb_tile(_tail(qb, row0), k_ref, q0 + row0, kj, (_tail(c[0], row0), _tail(c[1], row0)),
                                                m_right_neg, masked)
                    dw2 = _dot_nt(_tail(do_b, row0), _sb_by_head(v_ref, rows_k))
                    for h, (lb, wgt, _) in enumerate(heads):
                        e_s[2 * kj + h, row0:rows, :] = wgt * dw2[:, h * SB_T:(h + 1) * SB_T]
                        sig_s[2 * kj + h, row0:rows, :] = jnp.exp(lb)
                    add_transposed(dv_s, kj, do_t[:, row0:], jnp.concatenate([hd[1].astype(BF16) for hd in heads], axis=1))
                    return _merge(c[0], heads[0][2], row0), _merge(c[1], heads[1][2], row0)
                return run

            zero = jnp.zeros((rows, SB_T), F32)
            _sb_sweep(left, (zero, zero), blk, rows, first, True)

            def right(kj, masked, row0):
                def run(c):
                    dq, ecar = c[0], list(c[1:])
                    rows_k = _sb_key_rows(kj)
                    valid = _sb_valid(q0 + row0, rows - row0, kj) if masked else None
                    dzs = []
                    for h in range(2):
                        e = e_s[2 * kj + h, row0:rows, :]
                        se = _dot(e.astype(BF16), m_left)
                        ec = _tail(ecar[h], row0)
                        dz = e - sig_s[2 * kj + h, row0:rows, :] * (e + se[:, :SB_T] + ec)
                        if masked:
                            dz = jnp.where(valid, dz, 0.0)
                        dzs.append(dz.astype(BF16))
                        ecar[h] = _merge(ecar[h], ec + se[:, SB_T:], row0)
                    dz2 = jnp.concatenate(dzs, axis=1)
                    add_transposed(dk_s, kj, q_t[:, row0:], dz2)
                    upd = _tail(dq, row0) + _dot(dz2, _sb_by_head(k_ref, rows_k))
                    return _merge(dq, upd, row0), ecar[0], ecar[1]
                return run

            dq = _sb_sweep(right, (zero, zero, zero), blk, rows, first, False)[0]
            dq_ref[rows_q, :] = (dq * scale).astype(BF16)

        q_block(0, SB_T, True)

        def big_block(i, _):
            q_block(i, tq, False)
            return 0

        lax.fori_loop(0, nb, big_block, 0)
        for kj in range(nk):
            dk_ref[kj * SB_T:(kj + 1) * SB_T, :] = dk_s[kj].T.astype(BF16)
            dv_ref[kj * SB_T:(kj + 1) * SB_T, :] = dv_s[kj].T.astype(BF16)

    t = bsz * lp
    blk = pl.BlockSpec((lp, 128), lambda b, p: (b, p))
    shp = jax.ShapeDtypeStruct((t, 1024), BF16)
    return pl.pallas_call(
        body, grid=(bsz, SB_HEADS // 2), in_specs=_sb_specs(lp) + [blk, blk], out_specs=[blk] * 4,
        out_shape=[shp] * 4,
        scratch_shapes=[pltpu.VMEM((lp, 128), F32)] + [pltpu.VMEM((nk, 128, SB_T), F32)] * 2
        + [pltpu.VMEM((2 * nk, tq, SB_T), F32)] * 2,
        compiler_params=_cparams("parallel", "parallel"), name="sb_bwd")(qkvg, qkvg, qkvg, qkvg, o, dog)


XBC_COL0 = 3072 // 256
DT_COL0 = 4608 // 128
DT_L = 128


def _ssd_pre_fwd(proj, cw, cb, bsz, lp):
    def body(x_ref, cw_ref, cb_ref, o_ref):
        pre = _conv(x_ref[...], cw_ref[...], cb_ref[...])
        o_ref[...] = pre * _sigmoid(pre)

    return pl.pallas_call(
        body, grid=(bsz, SSD_CONV_DIM // 256),
        in_specs=[pl.BlockSpec((lp, 256), lambda b, j: (b, XBC_COL0 + j)), pl.BlockSpec((CONV_W, 256), lambda b, j: (0, j)),
                  pl.BlockSpec((1, 256), lambda b, j: (0, j))],
        out_specs=pl.BlockSpec((lp, 256), lambda b, j: (b, j)),
        out_shape=jax.ShapeDtypeStruct((bsz * lp, SSD_CONV_DIM), F32),
        compiler_params=_cparams("parallel", "parallel"), name="ssd_pre_fwd")(proj, cw, cb)


def _ssd_pre_bwd(proj, dact, cw, cb, bsz, lp):
    def body(x_ref, d_ref, cw_ref, cb_ref, dx_ref, dcw_ref, dcb_ref):
        @pl.when(pl.program_id(1) == 0)
        def _():
            dcw_ref[...] = jnp.zeros_like(dcw_ref)
            dcb_ref[...] = jnp.zeros_like(dcb_ref)
        x = x_ref[...]
        cwv = cw_ref[...]
        pre = _conv(x, cwv, cb_ref[...])
        s = _sigmoid(pre)
        dpre = d_ref[...].astype(F32) * s * (1.0 + pre * (1.0 - s))
        dcb_ref[...] += jnp.sum(dpre, axis=0, keepdims=True)
        _conv_bwd_w(dcw_ref, dpre, x)
        dx_ref[...] = _conv_bwd_x(dpre, cwv).astype(BF16)

    return pl.pallas_call(
        body, grid=(SSD_CONV_DIM // 256, bsz),
        in_specs=[pl.BlockSpec((lp, 256), lambda j, b: (b, XBC_COL0 + j)), pl.BlockSpec((lp, 256), lambda j, b: (b, j)),
                  pl.BlockSpec((CONV_W, 256), lambda j, b: (0, j)), pl.BlockSpec((1, 256), lambda j, b: (0, j))],
        out_specs=[pl.BlockSpec((lp, 256), lambda j, b: (b, j)), pl.BlockSpec((CONV_W, 256), lambda j, b: (0, j)),
                   pl.BlockSpec((1, 256), lambda j, b: (0, j))],
        out_shape=[jax.ShapeDtypeStruct((bsz * lp, SSD_CONV_DIM), BF16), jax.ShapeDtypeStruct((CONV_W, SSD_CONV_DIM), F32),
                   jax.ShapeDtypeStruct((1, SSD_CONV_DIM), F32)],
        compiler_params=_cparams("parallel", "arbitrary"), name="ssd_pre_bwd")(proj, dact, cw, cb)


def _split2(a):
    a1 = a.astype(BF16)
    return a1, (a - a1.astype(F32)).astype(BF16)


def _xdot2_nt(a, m01):
    a1, a2 = _split2(a)
    return _dot_nt(a1, m01) + _dot_nt(a2, m01)


def _xdot2_l(m01, a):
    a1, a2 = _split2(a)
    return _dot(m01, a1) + _dot(m01, a2)


class _SsdConsts:
    def __init__(self, g):
        q, gw = SSD_Q, SSD_GW
        head_of_lane = lax.shift_right_logical(_iota((DT_L, gw), 1), 6)
        self.sel = (_iota((DT_L, gw), 0) == 8 * g + head_of_lane).astype(BF16)
        r = _iota((q, gw), 0)
        c = jnp.bitwise_and(_iota((q, gw), 1), q - 1)
        self.diag = r == c
        self.diag_b = self.diag.astype(BF16)
        self.lower = c <= r
        self.upper = c >= r
        self.bd = lax.shift_right_logical(_iota((128, 128), 0), 6) == lax.shift_right_logical(_iota((128, 128), 1), 6)
        jj, ll = _iota((q, q), 1), _iota((q, q), 0)
        self.tri = (jj <= ll).astype(BF16)
        self.tri_t = (jj >= ll).astype(BF16)
        self.last = _iota((q, 1), 0) == q - 1


def _ssd_prepass(k, dt_ref, bias_ref, alog_ref, dtm_s, abc_s, lp):
    valid = _iota((lp, 1), 0) >= PAD
    sp_in = dt_ref[...] + bias_ref[...]
    dtm = jnp.where(valid, _softplus(sp_in), 0.0)
    aneg = -jnp.exp(alog_ref[...])
    dtm_s[...] = dtm
    abc_s[...] = _xdot_r(dtm * aneg, k.sel)
    return valid, sp_in, aneg


def _ssd_loop(nc, chunk, init):
    u = 2 if nc % 2 == 0 else 1

    def trip(i, carry):
        for j in range(u):
            carry = chunk(i * u + j, carry)
        return carry

    return lax.fori_loop(0, nc // u, trip, init)


def _ssd_chunk(k, rows, xs_ref, b_ref, c_ref, dtm_s, abc_s):
    bc = _xdot_l(k.tri, abc_s[rows, :])
    tt = jnp.sum(jnp.where(k.diag, bc, 0.0), axis=0, keepdims=True)
    dtbc = _xdot2(dtm_s[rows, :], k.sel)
    xs = xs_ref[rows, :]
    x = xs * dtbc
    bb = b_ref[rows, :].astype(BF16)
    cc = c_ref[rows, :].astype(BF16)
    tot = bc[SSD_Q - 1:SSD_Q, :]
    return bc, tt, dtbc, xs, x, bb, cc, tot, _pair_blocks(k, x)


def _pair_blocks(k, x):
    out = []
    for p in range(SSD_GW // 128):
        xp = x[:, 128 * p:128 * (p + 1)]
        out.append(jnp.where(k.bd, jnp.concatenate([xp, xp], axis=0), 0.0).astype(BF16))
    return out


def _pair_dot(a, blocks, transposed=False):
    dot = _dot_nt if transposed else _dot
    return jnp.concatenate([dot(a[:, 128 * p:128 * (p + 1)], blk) for p, blk in enumerate(blocks)], axis=1)


def _ssd_specs(lp, order):
    ix = (lambda f: (lambda b, g: f(b, g))) if order == "bg" else (lambda f: (lambda g, b: f(b, g)))
    return [pl.BlockSpec((lp, SSD_GW), ix(lambda b, g: (b, g))),
            pl.BlockSpec((lp, SSD_N), ix(lambda b, g: (b, 1024 // SSD_N + g))),
            pl.BlockSpec((lp, SSD_N), ix(lambda b, g: (b, 1280 // SSD_N + g))),
            pl.BlockSpec((lp, DT_L), ix(lambda b, g: (b, DT_COL0))),
            pl.BlockSpec((1, DT_L), ix(lambda b, g: (0, 0))),
            pl.BlockSpec((1, DT_L), ix(lambda b, g: (0, 0))),
            pl.BlockSpec((1, DT_L), ix(lambda b, g: (0, 0)))]


def _ssd_fwd(xbc, proj, dt_bias, a_log, dskip, bsz, lp):
    nc = lp // SSD_Q

    def body(xs_ref, b_ref, c_ref, dt_ref, bias_ref, alog_ref, dsk_ref, y_ref, dtm_s, abc_s):
        k = _SsdConsts(pl.program_id(1))
        _ssd_prepass(k, dt_ref, bias_ref, alog_ref, dtm_s, abc_s, lp)
        dvec = _xdot_r(jnp.broadcast_to(dsk_ref[...], (8, DT_L)), k.sel)[0:1, :]

        def chunk(c, state):
            rows = pl.ds(pl.multiple_of(c * SSD_Q, SSD_Q), SSD_Q)
            bc, tt, _, xs, x, bb, cc, tot, xbd = _ssd_chunk(k, rows, xs_ref, b_ref, c_ref, dtm_s, abc_s)
            lm = jnp.where(k.lower, jnp.exp(jnp.minimum(bc - tt, 0.0)), 0.0)
            g_all = _dot(_dot_nt(cc, bb).astype(BF16), k.diag_b) * lm
            y = _pair_dot(g_all.astype(BF16), xbd) + jnp.exp(bc) * _dot(cc, state.astype(BF16)) + dvec * xs
            y_ref[rows, :] = y
            return jnp.exp(tot) * state + _dot_tn(bb, (jnp.exp(tot - bc) * x).astype(BF16))

        _ssd_loop(nc, chunk, jnp.zeros((SSD_N, SSD_GW), F32))

    return pl.pallas_call(
        body, grid=(bsz, SSD_GROUPS), in_specs=_ssd_specs(lp, "bg"),
        out_specs=pl.BlockSpec((lp, SSD_GW), lambda b, g: (b, g)),
        out_shape=jax.ShapeDtypeStruct((bsz * lp, 1024), F32),
        scratch_shapes=[pltpu.VMEM((lp, DT_L), F32), pltpu.VMEM((lp, SSD_GW), F32)],
        compiler_params=_cparams("parallel", "parallel"), name="ssd_fwd")(xbc, xbc, xbc, proj, dt_bias, a_log, dskip)


def _ssd_bwd(xbc, proj, dt_bias, a_log, dskip, dy, bsz, lp):
    nc = lp // SSD_Q

    def body(xs_ref, b_ref, c_ref, dt_ref, bias_ref, alog_ref, dsk_ref, dy_ref,
             dxs_ref, db_ref, dc_ref, ddt_ref, dbias_ref, dalog_ref, ddsk_ref, dtm_s, abc_s, st_s):
        @pl.when(pl.program_id(1) == 0)
        def _():
            for ref in (dbias_ref, dalog_ref, ddsk_ref):
                ref[...] = jnp.zeros_like(ref)

        k = _SsdConsts(pl.program_id(0))
        valid, sp_in, aneg = _ssd_prepass(k, dt_ref, bias_ref, alog_ref, dtm_s, abc_s, lp)
        dvec = _xdot_r(jnp.broadcast_to(dsk_ref[...], (8, DT_L)), k.sel)[0:1, :]

        def fwd_chunk(c, state):
            rows = pl.ds(pl.multiple_of(c * SSD_Q, SSD_Q), SSD_Q)
            st_s[c] = state.astype(BF16)
            bc, _, _, _, x, bb, _, tot, _ = _ssd_chunk(k, rows, xs_ref, b_ref, c_ref, dtm_s, abc_s)
            return jnp.exp(tot) * state + _dot_tn(bb, (jnp.exp(tot - bc) * x).astype(BF16))

        _ssd_loop(nc, fwd_chunk, jnp.zeros((SSD_N, SSD_GW), F32))

        def bwd_chunk(i, carry):
            dstate, daneg, ddsk = carry
            c = nc - 1 - i
            rows = pl.ds(pl.multiple_of(c * SSD_Q, SSD_Q), SSD_Q)
            bc, tt, dtbc, xs, x, bb, cc, tot, xbd = _ssd_chunk(k, rows, xs_ref, b_ref, c_ref, dtm_s, abc_s)
            sprev = st_s[c]
            dyc = dy_ref[rows, :]
            dyb = dyc.astype(BF16)
            seg = bc - tt
            lm = jnp.where(k.lower, jnp.exp(jnp.minimum(seg, 0.0)), 0.0)
            lm_t = jnp.where(k.upper, jnp.exp(jnp.minimum(-seg, 0.0)), 0.0)
            cb_all = _dot(_dot_nt(cc, bb).astype(BF16), k.diag_b)
            cbt_all = _dot(_dot_nt(bb, cc).astype(BF16), k.diag_b)
            g_all = cb_all * lm
            dg = _pair_dot(dyb, xbd, transposed=True)
            dx = _pair_dot((cbt_all * lm_t).astype(BF16), _pair_blocks(k, dyc))
            hh = dg * g_all
            ea = jnp.exp(bc)
            yo = ea * _dot(cc, sprev)
            col_h = jnp.sum(hh, axis=0, keepdims=True)
            dcb = _dot_nt((dg * lm).astype(BF16), k.diag_b).astype(BF16)
            dcs = (ea * dyc).astype(BF16)
            dstb = dstate.astype(BF16)
            dec = jnp.exp(tot - bc)
            w = dec * x
            dw = _dot(bb, dstb)
            dc_ref[rows, :] = (_dot(dcb, bb) + _dot_nt(dcs, sprev)).astype(BF16)
            db_ref[rows, :] = (_dot_tn(dcb, cc) + _dot_nt(w.astype(BF16), dstb)).astype(BF16)
            dx = dx + dec * dw
            etot = jnp.exp(tot)
            r8 = _iota((8, SSD_GW), 0)
            st_row = jnp.sum(dstate * sprev.astype(F32), axis=0, keepdims=True) * etot
            sk_row = jnp.sum(dyc * xs, axis=0, keepdims=True)
            small = jnp.where(r8 == 0, st_row, jnp.where(r8 == 1, sk_row, 0.0))
            q = SSD_Q
            sums = _xdot2_nt(jnp.concatenate([hh + dyc * yo - jnp.where(k.diag, col_h, 0.0) - dw * w, dw * w, dx * xs, small],
                                             axis=0), k.sel)
            kk = sums[q:2 * q]
            dtot = sums[3 * q:3 * q + 1] + jnp.sum(kk, axis=0, keepdims=True)
            dacum = sums[0:q] + jnp.where(k.last, dtot, 0.0)
            da = _xdot2_l(k.tri_t, dacum)
            dtm_c = dtm_s[rows, :]
            ddtm = da * aneg + sums[2 * q:3 * q]
            vrow = (c * SSD_Q + _iota((SSD_Q, 1), 0)) >= PAD
            ddt_ref[rows, :] = jnp.where(vrow, ddtm * _sigmoid(dt_ref[rows, :] + bias_ref[...]), 0.0)
            dxs_ref[rows, :] = (dx * dtbc + dvec * dyc).astype(BF16)
            daneg = daneg + jnp.sum(da * dtm_c, axis=0, keepdims=True)
            ddsk = ddsk + sums[3 * q + 1:3 * q + 2]
            dstate = etot * dstate + _dot_tn(cc, dcs)
            return dstate, daneg, ddsk

        zrow = jnp.zeros((1, DT_L), F32)
        _, daneg, ddsk = _ssd_loop(nc, bwd_chunk, (jnp.zeros((SSD_N, SSD_GW), F32), zrow, zrow))
        dbias_ref[...] += jnp.broadcast_to(jnp.sum(ddt_ref[...], axis=0, keepdims=True), (8, DT_L))
        dalog_ref[...] += jnp.broadcast_to(daneg * aneg, (8, DT_L))
        ddsk_ref[...] += jnp.broadcast_to(ddsk, (8, DT_L))

    t = bsz * lp
    par = pl.BlockSpec((8, DT_L), lambda g, b: (g, 0))
    par_shape = jax.ShapeDtypeStruct((8 * SSD_GROUPS, DT_L), F32)
    return pl.pallas_call(
        body, grid=(SSD_GROUPS, bsz),
        in_specs=_ssd_specs(lp, "gb") + [pl.BlockSpec((lp, SSD_GW), lambda g, b: (b, g))],
        out_specs=[pl.BlockSpec((lp, SSD_GW), lambda g, b: (b, g)), pl.BlockSpec((lp, SSD_N), lambda g, b: (b, g)),
                   pl.BlockSpec((lp, SSD_N), lambda g, b: (b, g)), pl.BlockSpec((lp, DT_L), lambda g, b: (b, g)), par, par, par],
        out_shape=[jax.ShapeDtypeStruct((t, 1024), BF16), jax.ShapeDtypeStruct((t, 256), BF16),
                   jax.ShapeDtypeStruct((t, 256), BF16), jax.ShapeDtypeStruct((t, SSD_GROUPS * DT_L), F32),
                   par_shape, par_shape, par_shape],
        scratch_shapes=[pltpu.VMEM((lp, DT_L), F32), pltpu.VMEM((lp, SSD_GW), F32), pltpu.VMEM((nc, SSD_N, SSD_GW), BF16)],
        compiler_params=_cparams("parallel", "arbitrary"), name="ssd_bwd")(
            xbc, xbc, xbc, proj, dt_bias, a_log, dskip, dy)


Z_COL0 = 2048 // SSD_GW


def _gnorm_fwd(y, proj, w, mixed, name="gnorm_fwd"):
    t = y.shape[0]
    tm = _row_tile(t)

    def body(y_ref, z_ref, w_ref, mixed_ref, o_ref):
        z = z_ref[...]
        gt = y_ref[...] * z * _sigmoid(z)
        r = lax.rsqrt(jnp.mean(gt * gt, axis=-1, keepdims=True) + EPS)
        o_ref[...] = (gt * r * w_ref[...]).astype(BF16)

    return pl.pallas_call(
        body, grid=(t // tm, SSD_GROUPS),
        in_specs=[pl.BlockSpec((tm, SSD_GW), lambda i, g: (i, g)), pl.BlockSpec((tm, SSD_GW), lambda i, g: (i, Z_COL0 + g)),
                  pl.BlockSpec((1, SSD_GW), lambda i, g: (0, g)), ANY_SPEC],
        out_specs=pl.BlockSpec((tm, SSD_GW), lambda i, g: (i, 1024 // SSD_GW + g)),
        out_shape=jax.ShapeDtypeStruct((t, 2048), BF16), input_output_aliases={3: 0},
        compiler_params=_cparams("parallel", "parallel"), name=name)(y, proj, w, mixed)


def _gnorm_bwd(y, proj, w, dmixed):
    t = y.shape[0]
    tm = _row_tile(t)

    def body(y_ref, z_ref, w_ref, d_ref, dy_ref, dz_ref, dw_ref):
        @pl.when(pl.program_id(1) == 0)
        def _():
            dw_ref[...] = jnp.zeros_like(dw_ref)
        z, yv, d = z_ref[...], y_ref[...], d_ref[...]
        s = _sigmoid(z)
        sz = z * s
        gt = yv * sz
        r = lax.rsqrt(jnp.mean(gt * gt, axis=-1, keepdims=True) + EPS)
        gh = gt * r
        dgn = d * w_ref[...]
        dgt = r * (dgn - gh * jnp.mean(dgn * gh, axis=-1, keepdims=True))
        dw_ref[...] += jnp.sum(d * gh, axis=0, keepdims=True)
        dy_ref[...] = dgt * sz
        dz_ref[...] = (dgt * yv * s * (1.0 + z * (1.0 - s))).astype(BF16)

    blk = pl.BlockSpec((tm, SSD_GW), lambda g, i: (i, g))
    return pl.pallas_call(
        body, grid=(SSD_GROUPS, t // tm),
        in_specs=[blk, pl.BlockSpec((tm, SSD_GW), lambda g, i: (i, Z_COL0 + g)), pl.BlockSpec((1, SSD_GW), lambda g, i: (0, g)),
                  pl.BlockSpec((tm, SSD_GW), lambda g, i: (i, 1024 // SSD_GW + g))],
        out_specs=[blk, blk, pl.BlockSpec((1, SSD_GW), lambda g, i: (0, g))],
        out_shape=[jax.ShapeDtypeStruct((t, 1024), F32), jax.ShapeDtypeStruct((t, 1024), BF16),
                   jax.ShapeDtypeStruct((1, 1024), F32)],
        compiler_params=_cparams("parallel", "arbitrary"), name="gnorm_bwd")(y, proj, w, dmixed)


def _meta_grad(dh0, bsz, lp):
    def body(d_ref, o_ref):
        @pl.when(pl.program_id(0) == 0)
        def _():
            o_ref[...] = jnp.zeros_like(o_ref)
        o_ref[...] += d_ref[...]

    return pl.pallas_call(
        body, grid=(bsz,),
        in_specs=[pl.BlockSpec((N_META, D_MODEL), lambda b: (b * (lp // N_META) + PAD // N_META, 0))],
        out_specs=pl.BlockSpec((N_META, D_MODEL), lambda b: (0, 0)),
        out_shape=jax.ShapeDtypeStruct((N_META, D_MODEL), F32),
        compiler_params=_cparams("arbitrary"), name="meta_grad")(dh0)


OTHER_CHIPS = ((1, 0), (0, 1), (1, 1))
ANY_SPEC = pl.BlockSpec(memory_space=pl.ANY)


class _Mesh:
    def __init__(self):
        self.x, self.y, self.c = lax.axis_index("x"), lax.axis_index("y"), lax.axis_index("c")

    def dev(self, fx, fy, fc):
        return (1 - self.x if fx else self.x, 1 - self.y if fy else self.y, 1 - self.c if fc else self.c)

    def slot(self, fx, fy, fc):
        px, py, pc = self.dev(fx, fy, fc)
        return 4 * px + 2 * py + pc

    def chip(self, fx, fy):
        px, py, _ = self.dev(fx, fy, 0)
        return 2 * px + py


def _remote(src, dst, send_sems, recv_sems, idx, dev):
    return pltpu.make_async_remote_copy(src_ref=src, dst_ref=dst, send_sem=send_sems.at[idx], recv_sem=recv_sems.at[idx],
                                        device_id=dev, device_id_type=pl.DeviceIdType.MESH)


def _gather_phases(ins, outs, send_sems, recv_sems):
    n = len(ins)
    me = _Mesh()

    def first():
        cps = []
        for a in range(n):
            mine = outs[a].at[me.slot(0, 0, 0)]
            cps.append(_remote(ins[a], mine, send_sems, recv_sems, (a, 0), me.dev(0, 0, 1)))
            for j, (fx, fy) in enumerate(OTHER_CHIPS):
                cps.append(_remote(ins[a], mine, send_sems, recv_sems, (a, 1 + j), me.dev(fx, fy, 0)))
        return cps

    def passed():
        return [_remote(outs[a].at[me.slot(fx, fy, 0)], outs[a].at[me.slot(fx, fy, 0)], send_sems, recv_sems, (a, 4 + j),
                        me.dev(0, 0, 1)) for a in range(n) for j, (fx, fy) in enumerate(OTHER_CHIPS)]

    def start():
        for cp in first():
            cp.start()

    def forward():
        fwd = passed()
        for a in range(n):
            for j, (fx, fy) in enumerate(OTHER_CHIPS):
                _remote(ins[a], outs[a].at[me.slot(fx, fy, 0)], send_sems, recv_sems, (a, 1 + j), me.dev(fx, fy, 0)).wait_recv()
                fwd[3 * a + j].start()

    def finish():
        for a in range(n):
            _remote(ins[a], outs[a].at[me.slot(0, 0, 1)], send_sems, recv_sems, (a, 0), me.dev(0, 0, 1)).wait_recv()
            for j, (fx, fy) in enumerate(OTHER_CHIPS):
                _remote(ins[a], outs[a].at[me.slot(fx, fy, 1)], send_sems, recv_sems, (a, 4 + j), me.dev(0, 0, 1)).wait_recv()
        for cp in first() + passed():
            cp.wait_send()

    return start, forward, finish


def _own_blocks(stacks, arrs):
    me = 4 * lax.axis_index("x") + 2 * lax.axis_index("y") + lax.axis_index("c")
    return [lax.dynamic_update_slice_in_dim(s, a[None], me, axis=0) for s, a in zip(stacks, arrs)]


def _gather_all(arrs, name):
    n = len(arrs)

    def body(*refs):
        start, forward, finish = _gather_phases(refs[:n], refs[n:2 * n], *refs[2 * n:])
        start()
        forward()
        finish()

    stacks = pl.pallas_call(
        body, in_specs=[ANY_SPEC] * n, out_specs=[ANY_SPEC] * n,
        out_shape=[jax.ShapeDtypeStruct((N_DEV,) + a.shape, a.dtype) for a in arrs],
        scratch_shapes=[pltpu.SemaphoreType.DMA((n, 7)), pltpu.SemaphoreType.DMA((n, 7))],
        compiler_params=pltpu.CompilerParams(has_side_effects=True), name=name)(*arrs)
    return _own_blocks(stacks, arrs)


def _mm_nn_gather(a, b, out_dtype, name, shards):
    m, k = a.shape
    _, n = b.shape
    tm = _row_tile(m)
    steps = m // tm
    ns = len(shards)

    def body(a_ref, b_ref, *rest):
        o_ref = rest[ns]
        start, forward, finish = _gather_phases(rest[:ns], rest[ns + 1:2 * ns + 1], *rest[2 * ns + 1:])
        i = pl.program_id(0)
        pl.when(i == 0)(start)
        pl.when(i == steps // 2)(forward)
        av = a_ref[...].astype(BF16)
        for s, w in _chunks(n, 512):
            o_ref[:, s:s + w] = _dot(av, b_ref[:, s:s + w]).astype(out_dtype)
        pl.when(i == steps - 1)(finish)

    outs = pl.pallas_call(
        body, grid=(steps,),
        in_specs=[pl.BlockSpec((tm, k), lambda i: (i, 0)), pl.BlockSpec((k, n), lambda i: (0, 0))] + [ANY_SPEC] * ns,
        out_specs=[pl.BlockSpec((tm, n), lambda i: (i, 0))] + [ANY_SPEC] * ns,
        out_shape=[jax.ShapeDtypeStruct((m, n), out_dtype)] + [jax.ShapeDtypeStruct((N_DEV,) + s.shape, s.dtype) for s in shards],
        scratch_shapes=[pltpu.SemaphoreType.DMA((ns, 7)), pltpu.SemaphoreType.DMA((ns, 7))],
        compiler_params=pltpu.CompilerParams(dimension_semantics=("arbitrary",), vmem_limit_bytes=VMEM_LIMIT,
                                             has_side_effects=True), name=name)(a, b, *shards)
    return outs[0], _own_blocks(outs[1:], shards)


def _swap_in_chip(arrs, name):
    n = len(arrs)

    def body(*refs):
        ins, got = refs[:n], refs[n:2 * n]
        send_sems, recv_sems = refs[2 * n:]
        me = _Mesh()
        sends = [_remote(ins[a].at[2 * j + (1 - me.c)], got[a].at[j], send_sems, recv_sems, (a, j), me.dev(0, 0, 1))
                 for a in range(n) for j in range(4)]
        for cp in sends:
            cp.start()
        for cp in sends:
            cp.wait()

    return pl.pallas_call(
        body, in_specs=[ANY_SPEC] * n, out_specs=[ANY_SPEC] * n,
        out_shape=[jax.ShapeDtypeStruct((4,) + a.shape[1:], a.dtype) for a in arrs],
        scratch_shapes=[pltpu.SemaphoreType.DMA((n, 4)), pltpu.SemaphoreType.DMA((n, 4))],
        compiler_params=pltpu.CompilerParams(has_side_effects=True), name=name)(*arrs)


def _add_pair(a, b, name):
    _, r, c = a.shape
    tr = _adam_rows(r)

    def body(a_ref, b_ref, o_ref):
        o_ref[...] = (a_ref[...].astype(F32) + b_ref[...].astype(F32)).astype(o_ref.dtype)

    blk = pl.BlockSpec((4, tr, c), lambda i: (0, i, 0))
    return pl.pallas_call(body, grid=(r // tr,), in_specs=[blk, blk], out_specs=blk,
                          out_shape=jax.ShapeDtypeStruct(a.shape, a.dtype), compiler_params=_cparams("parallel"), name=name)(a, b)


def _swap_chips(sums, gather, name):
    ns, ng = len(sums), len(gather)
    n = ns + ng

    def body(*refs):
        ins, outs = refs[:n], refs[n:2 * n]
        send_sems, recv_sems, local_sems = refs[2 * n:]
        me = _Mesh()
        local, sends, recvs = [], [], []
        for a in range(ns):
            here = me.chip(0, 0)
            for j, (fx, fy) in enumerate(OTHER_CHIPS):
                there = me.chip(fx, fy)
                sends.append(_remote(ins[a].at[there], outs[a].at[here], send_sems, recv_sems, (a, j), me.dev(fx, fy, 0)))
                recvs.append(_remote(ins[a].at[there], outs[a].at[there], send_sems, recv_sems, (a, j), me.dev(fx, fy, 0)))
        for a in range(ns, n):
            local.append(pltpu.make_async_copy(ins[a], outs[a].at[me.slot(0, 0, 0)], local_sems.at[a]))
            for k in range(1, N_DEV):
                flips = (k >> 2 & 1, k >> 1 & 1, k & 1)
                sends.append(_remote(ins[a], outs[a].at[me.slot(0, 0, 0)], send_sems, recv_sems, (a, k - 1), me.dev(*flips)))
                recvs.append(_remote(ins[a], outs[a].at[me.slot(*flips)], send_sems, recv_sems, (a, k - 1), me.dev(*flips)))
        for cp in local + sends:
            cp.start()
        for cp in recvs:
            cp.wait_recv()
        for cp in sends:
            cp.wait_send()
        for cp in local:
            cp.wait()

    out_shape = [jax.ShapeDtypeStruct(a.shape, a.dtype) for a in sums]
    out_shape += [jax.ShapeDtypeStruct((N_DEV,) + a.shape, a.dtype) for a in gather]
    outs = pl.pallas_call(
        body, in_specs=[ANY_SPEC] * n, out_specs=[ANY_SPEC] * n, out_shape=out_shape,
        scratch_shapes=[pltpu.SemaphoreType.DMA((n, N_DEV - 1)), pltpu.SemaphoreType.DMA((n, N_DEV - 1)),
                        pltpu.SemaphoreType.DMA((n,))],
        compiler_params=pltpu.CompilerParams(has_side_effects=True), name=name)(*sums, *gather)
    here = 2 * lax.axis_index("x") + lax.axis_index("y")
    mine = [lax.dynamic_update_slice_in_dim(o, lax.dynamic_slice_in_dim(s, here, 1, axis=0), here, axis=0)
            for o, s in zip(outs[:ns], sums)]
    return mine + list(outs[ns:])


def _adam_rows(r):
    for cand in (128, 64, 32, 16, 8):
        if r % cand == 0 and r > cand:
            return cand
    return r


def _adamw(parts, w, m, v, name):
    r, c = w.shape
    tr = _adam_rows(r)
    n_parts = parts.shape[0]

    def body(p_ref, w_ref, m_ref, v_ref, g_ref, d_ref, nm_ref, nv_ref):
        g = p_ref[0].astype(F32)
        for k in range(1, n_parts):
            g = g + p_ref[k].astype(F32)
        m_new = ADAM_B1 * m_ref[...] + (1.0 - ADAM_B1) * g
        v_new = ADAM_B2 * v_ref[...] + (1.0 - ADAM_B2) * (g * g)
        m_hat = m_new / (1.0 - ADAM_B1 ** ADAM_STEP)
        v_hat = v_new / (1.0 - ADAM_B2 ** ADAM_STEP)
        g_ref[...] = g
        d_ref[...] = -ADAM_LR * (m_hat / (jnp.sqrt(v_hat) + ADAM_EPS) + ADAM_WD * w_ref[...])
        nm_ref[...] = m_new
        nv_ref[...] = v_new

    blk = pl.BlockSpec((tr, c), lambda i: (i, 0))
    shp = jax.ShapeDtypeStruct((r, c), F32)
    return pl.pallas_call(
        body, grid=(r // tr,), in_specs=[pl.BlockSpec((n_parts, tr, c), lambda i: (0, i, 0)), blk, blk, blk],
        out_specs=[blk] * 4, out_shape=[shp] * 4, compiler_params=_cparams("parallel"), name=name)(parts, w, m, v)


def _rows128(a):
    return a.reshape(-1, 128)


def _pad_rows(a, rows):
    return jnp.pad(a, ((0, rows - a.shape[0]), (0, 0)))


def _lane16(a):
    return jnp.pad(a.reshape(1, -1), ((0, 0), (0, 128 - a.size)))


SHARD_PACK_ROWS = 544
REPL_PACK_ROWS = 72
LOSS_ROW = 71


def _pack_shard(meta, lru_conv_w, odd_norm, ssd_conv_w, lru_w_a, lru_w_x):
    parts = [meta.reshape(16, 128), lru_conv_w.reshape(4, 128), odd_norm.reshape(1, 128), _rows128(ssd_conv_w.reshape(4, 192)),
             _rows128(lru_w_a.reshape(4, 32, 256)), _rows128(lru_w_x.reshape(4, 32, 256))]
    return _pad_rows(jnp.concatenate(parts, axis=0), SHARD_PACK_ROWS)


def _unpack_shard(p):
    return (p[0:16], p[16:20].reshape(1, 4, 128), p[20:21], p[21:27].reshape(1, 4, 192),
            p[27:283].reshape(1, 4, 32, 256), p[283:539].reshape(1, 4, 32, 256))


def _pack_repl(even_norm, lru_conv_b, lru_b_a, lru_b_x, lru_lambda, ssd_norm, final_norm, ssd_conv_b, dt_bias, a_log, ssd_d):
    parts = [_rows128(v) for v in (even_norm, lru_conv_b, lru_b_a, lru_b_x, lru_lambda, ssd_norm, final_norm, ssd_conv_b)]
    parts += [_lane16(dt_bias), _lane16(a_log), _lane16(ssd_d)]
    return _pad_rows(jnp.concatenate(parts, axis=0), REPL_PACK_ROWS)


def _unpack_repl(p):
    vec = lambda i: p[8 * i:8 * i + 8].reshape(1, 1024)
    return (vec(0), vec(1), vec(2), vec(3), vec(4), vec(5), p[48:56].reshape(1024), p[56:68].reshape(1, 1536),
            p[68:69, :16], p[69:70, :16], p[70:71, :16])


def _local_step(x, tgt, meta, even_norm, w_in_e, lru_conv_w, lru_conv_b, lru_w_a, lru_b_a, lru_w_x, lru_b_x, lru_lambda,
                ssd_conv_w, ssd_conv_b, dt_bias, a_log, ssd_d, ssd_norm, odd_norm, final_norm, even_in):
    bsz, seq, d = x.shape
    lp = PAD + N_META + seq
    t = bsz * lp
    h0 = jnp.concatenate([jnp.zeros((bsz, PAD, d), F32), jnp.broadcast_to(meta[None], (bsz, N_META, d)), x], axis=1).reshape(t, d)

    u0 = _norm_fwd(h0, even_norm, "norm0_fwd")
    proj, w_out_e, w_in_o, w_out_o = even_in(u0)
    lru = (lru_conv_w, lru_conv_b, lru_w_a, lru_b_a, lru_w_x, lru_b_x, lru_lambda)
    ya = _lru_fwd(proj, *lru, bsz, lp)
    xbc = _ssd_pre_fwd(proj, ssd_conv_w, ssd_conv_b, bsz, lp)
    y = _ssd_fwd(xbc, proj, dt_bias, a_log, ssd_d, bsz, lp)
    mixed = _gnorm_fwd(y, proj, ssd_norm, ya)
    h1 = _mm_nn(mixed, w_out_e, F32, "even_out", res=h0)
    u2 = _norm_fwd(h1, odd_norm, "norm1_fwd")
    qkvg = _mm_nn(u2, w_in_o, F32, "odd_in")
    o, og = _sb_fwd(qkvg, bsz, lp)
    h2 = _mm_nn(og, w_out_o, F32, "odd_out", res=h1)
    loss, dh2, d_final_norm = _final_loss(h2, final_norm, tgt, lp)

    dog = _mm_nn(dh2, w_out_o.T, F32, "odd_out_dx")
    d_w_out_o = _mm_tn(og, dh2, "odd_out_dw")
    dqkvg = jnp.concatenate(_sb_bwd(qkvg, o, dog, bsz, lp), axis=1)
    du2 = _mm_nn(dqkvg, w_in_o.T, F32, "odd_in_dx")
    d_w_in_o = _mm_tn(u2, dqkvg, "odd_in_dw")
    dh1, d_odd_norm = _norm_bwd(h1, odd_norm, du2, dh2, "norm1_bwd")

    dmixed = _mm_nn(dh1, w_out_e.T, F32, "even_out_dx")
    d_w_out_e = _mm_tn(mixed, dh1, "even_out_dw")
    dlx, dgate, d_lru_conv_w, d_lru_conv_b, d_w_a, d_b_a, d_w_x, d_b_x, d_lambda = _lru_bwd(proj, dmixed, *lru, bsz, lp)
    dy, dz, d_ssd_norm = _gnorm_bwd(y, proj, ssd_norm, dmixed)
    dxs, dbm, dcm, ddt, d_dt_bias, d_a_log, d_ssd_d = _ssd_bwd(xbc, proj, dt_bias, a_log, ssd_d, dy, bsz, lp)
    dxbc, d_ssd_conv_w, d_ssd_conv_b = _ssd_pre_bwd(proj, jnp.concatenate([dxs, dbm, dcm], axis=1), ssd_conv_w, ssd_conv_b, bsz, lp)
    ddt = (ddt[:, :DT_L] + ddt[:, DT_L:]).astype(BF16)
    dproj = jnp.concatenate([dlx, dgate, dz, dxbc, ddt, jnp.zeros((t, EVEN_NP - 4608 - DT_L), BF16)], axis=1)
    du0 = _mm_nn(dproj, w_in_e.T, F32, "even_in_dx")
    d_w_in_e = _mm_tn(u0, dproj, "even_in_dw")
    dh0, d_even_norm = _norm_bwd(h0, even_norm, du0, dh1, "norm0_bwd")
    grad_x = dh0.reshape(bsz, lp, d)[:, PAD + N_META:]
    d_meta = _meta_grad(dh0, bsz, lp)
    heads = lambda p: (p[0:1] + p[8:9])[:, :SSD_HEADS]
    grads = dict(meta=d_meta, even_norm=d_even_norm, even_w_in=d_w_in_e[:, :EVEN_IN], lru_conv_w=d_lru_conv_w,
                 lru_conv_b=d_lru_conv_b, lru_w_a=d_w_a, lru_b_a=d_b_a, lru_w_x=d_w_x, lru_b_x=d_b_x, lru_lambda=d_lambda,
                 ssd_conv_w=d_ssd_conv_w, ssd_conv_b=d_ssd_conv_b, ssd_dt_bias=heads(d_dt_bias), ssd_a_log=heads(d_a_log),
                 ssd_d=heads(d_ssd_d), ssd_norm=d_ssd_norm, even_w_out=d_w_out_e, odd_norm=d_odd_norm, odd_w_in=d_w_in_o,
                 odd_w_out=d_w_out_o, final_norm=d_final_norm)
    return loss[0, 0], grad_x, grads


WEIGHTS = ['meta', 'even_norm', 'even_w_in', 'lru_conv_w', 'lru_conv_b', 'lru_w_a', 'lru_b_a', 'lru_w_x', 'lru_b_x', 'lru_lambda',
           'ssd_conv_w', 'ssd_conv_b', 'ssd_dt_bias', 'ssd_a_log', 'ssd_d', 'ssd_norm', 'even_w_out', 'odd_norm', 'odd_w_in',
           'odd_w_out', 'final_norm']


def _blocks_of(a, axis):
    shp = a.shape
    a = a.reshape(shp[:axis] + (N_DEV, shp[axis] // N_DEV) + shp[axis + 1:])
    return jnp.moveaxis(a, axis, 0)


def _unblock(a, axis):
    a = jnp.moveaxis(a, 0, axis)
    shp = a.shape
    return a.reshape(shp[:axis] + (shp[axis] * shp[axis + 1],) + shp[axis + 2:])


def kernel(x, meta, even_norm, even_w_in, lru_conv_w, lru_conv_b, lru_w_a, lru_b_a, lru_w_x, lru_b_x, lru_lambda, ssd_conv_w, ssd_conv_b, ssd_dt_bias, ssd_a_log, ssd_d, ssd_norm, even_w_out, odd_norm, odd_w_in, odd_w_out, final_norm, loss_target, m_meta, m_even_norm, m_even_w_in, m_lru_conv_w, m_lru_conv_b, m_lru_w_a, m_lru_b_a, m_lru_w_x, m_lru_b_x, m_lru_lambda, m_ssd_conv_w, m_ssd_conv_b, m_ssd_dt_bias, m_ssd_a_log, m_ssd_d, m_ssd_norm, m_even_w_out, m_odd_norm, m_odd_w_in, m_odd_w_out, m_final_norm, v_meta, v_even_norm, v_even_w_in, v_lru_conv_w, v_lru_conv_b, v_lru_w_a, v_lru_b_a, v_lru_w_x, v_lru_b_x, v_lru_lambda, v_ssd_conv_w, v_ssd_conv_b, v_ssd_dt_bias, v_ssd_a_log, v_ssd_d, v_ssd_norm, v_even_w_out, v_odd_norm, v_odd_w_in, v_odd_w_out, v_final_norm):
    w = dict(meta=meta, even_norm=even_norm, even_w_in=even_w_in, lru_conv_w=lru_conv_w, lru_conv_b=lru_conv_b, lru_w_a=lru_w_a,
             lru_b_a=lru_b_a, lru_w_x=lru_w_x, lru_b_x=lru_b_x, lru_lambda=lru_lambda, ssd_conv_w=ssd_conv_w,
             ssd_conv_b=ssd_conv_b, ssd_dt_bias=ssd_dt_bias, ssd_a_log=ssd_a_log, ssd_d=ssd_d, ssd_norm=ssd_norm,
             even_w_out=even_w_out, odd_norm=odd_norm, odd_w_in=odd_w_in, odd_w_out=odd_w_out, final_norm=final_norm)
    m = dict(meta=m_meta, even_norm=m_even_norm, even_w_in=m_even_w_in, lru_conv_w=m_lru_conv_w, lru_conv_b=m_lru_conv_b,
             lru_w_a=m_lru_w_a, lru_b_a=m_lru_b_a, lru_w_x=m_lru_w_x, lru_b_x=m_lru_b_x, lru_lambda=m_lru_lambda,
             ssd_conv_w=m_ssd_conv_w, ssd_conv_b=m_ssd_conv_b, ssd_dt_bias=m_ssd_dt_bias, ssd_a_log=m_ssd_a_log, ssd_d=m_ssd_d,
             ssd_norm=m_ssd_norm, even_w_out=m_even_w_out, odd_norm=m_odd_norm, odd_w_in=m_odd_w_in, odd_w_out=m_odd_w_out,
             final_norm=m_final_norm)
    v = dict(meta=v_meta, even_norm=v_even_norm, even_w_in=v_even_w_in, lru_conv_w=v_lru_conv_w, lru_conv_b=v_lru_conv_b,
             lru_w_a=v_lru_w_a, lru_b_a=v_lru_b_a, lru_w_x=v_lru_w_x, lru_b_x=v_lru_b_x, lru_lambda=v_lru_lambda,
             ssd_conv_w=v_ssd_conv_w, ssd_conv_b=v_ssd_conv_b, ssd_dt_bias=v_ssd_dt_bias, ssd_a_log=v_ssd_a_log, ssd_d=v_ssd_d,
             ssd_norm=v_ssd_norm, even_w_out=v_even_w_out, odd_norm=v_odd_norm, odd_w_in=v_odd_w_in, odd_w_out=v_odd_w_out,
             final_norm=v_final_norm)
    shard_names = ('meta', 'lru_conv_w', 'odd_norm', 'ssd_conv_w', 'lru_w_a', 'lru_w_x')
    repl_names = ('even_norm', 'lru_conv_b', 'lru_b_a', 'lru_b_x', 'lru_lambda', 'ssd_norm', 'final_norm', 'ssd_conv_b',
                  'ssd_dt_bias', 'ssd_a_log', 'ssd_d')
    big_names = ('even_w_in', 'even_w_out', 'odd_w_in', 'odd_w_out')

    gates = jnp.concatenate([lru_w_a.reshape(128, 256), lru_w_x.reshape(128, 256)], axis=0).astype(BF16)
    small = _pack_shard(*[w[k] for k in shard_names])
    g_in_e, g_gates, g_small = _gather_all([even_w_in[0].astype(BF16), gates, small], "gather_weights")
    w_in_e = jnp.pad(_unblock(g_in_e, 1), ((0, 0), (0, EVEN_NP - EVEN_IN)))

    def even_in(u0):
        proj, (g_out_e, g_in_o, g_out_o) = _mm_nn_gather(
            u0, w_in_e, F32, "even_in", [even_w_out[0].astype(BF16), odd_w_in[0].astype(BF16), odd_w_out[0].astype(BF16)])
        return proj, g_out_e.reshape(2048, 1024), _unblock(g_in_o, 1), g_out_o.reshape(1024, 1024)

    gates_full = jnp.moveaxis(g_gates.reshape(N_DEV, 2, 4, 32, 256), 0, 2).reshape(2, 4, 256, 256)
    f_meta = _unblock(g_small[:, 0:16], 1)
    f_lru_conv_w = _unblock(g_small[:, 16:20], 1)
    f_odd_norm = _unblock(g_small[:, 20:21], 1)
    f_ssd_conv_w = _unblock(g_small[:, 21:27].reshape(N_DEV, 4, 192), 1)

    loss, grad_x, g = _local_step(
        x, loss_target, f_meta, even_norm, w_in_e, f_lru_conv_w, lru_conv_b, gates_full[0], lru_b_a, gates_full[1], lru_b_x,
        lru_lambda, f_ssd_conv_w, ssd_conv_b, _lane16(ssd_dt_bias), _lane16(ssd_a_log), _lane16(ssd_d), ssd_norm,
        f_odd_norm, final_norm.reshape(1, -1), even_in)

    s_small = jnp.stack([_pack_shard(g['meta'][:, 128 * p:128 * (p + 1)], g['lru_conv_w'][:, 128 * p:128 * (p + 1)],
                                     g['odd_norm'][:, 128 * p:128 * (p + 1)], g['ssd_conv_w'][:, 192 * p:192 * (p + 1)],
                                     g['lru_w_a'][:, 32 * p:32 * (p + 1)], g['lru_w_x'][:, 32 * p:32 * (p + 1)])
                         for p in range(N_DEV)])
    r_pack = _pack_repl(*[g[k] for k in repl_names]).at[LOSS_ROW, 0].set(loss)
    by_owner = [_blocks_of(g['even_w_in'], 1).astype(BF16), g['even_w_out'].reshape(N_DEV, 256, 1024).astype(BF16),
                _blocks_of(g['odd_w_in'], 1).astype(BF16), g['odd_w_out'].reshape(N_DEV, 128, 1024).astype(BF16), s_small]
    got = _swap_in_chip(by_owner, "reduce_in_chip")
    my_c = lax.axis_index("c")
    kept = [lax.dynamic_index_in_dim(a.reshape((4, 2) + a.shape[1:]), my_c, axis=1, keepdims=False) for a in by_owner]
    chip_sums = [_add_pair(a, b, "chip_sum_%d" % i) for i, (a, b) in enumerate(zip(kept, got))]
    p_in_e, p_out_e, p_in_o, p_out_o, p_small, p_repl = _swap_chips(chip_sums, [r_pack], "reduce_across_chips")

    res = {}
    for k, parts in zip(big_names, (p_in_e, p_out_e, p_in_o, p_out_o)):
        outs = _adamw(parts, w[k][0], m[k][0], v[k][0], "adamw_" + k)
        res[k] = [o[None] for o in outs]
    outs = _adamw(p_small, small, _pack_shard(*[m[k] for k in shard_names]), _pack_shard(*[v[k] for k in shard_names]), "adamw_sharded")
    unpacked = [_unpack_shard(o) for o in outs]
    for i, k in enumerate(shard_names):
        res[k] = [u[i] for u in unpacked]
    outs = _adamw(p_repl, _pack_repl(*[w[k] for k in repl_names]), _pack_repl(*[m[k] for k in repl_names]),
                  _pack_repl(*[v[k] for k in repl_names]), "adamw_replicated")
    unpacked = [_unpack_repl(o) for o in outs]
    for i, k in enumerate(repl_names):
        res[k] = [u[i] for u in unpacked]
    loss = outs[0][LOSS_ROW, 0]
    return (loss, grad_x, *[res[k][0] for k in WEIGHTS], *[res[k][1] for k in WEIGHTS], *[res[k][2] for k in WEIGHTS],
            *[res[k][3] for k in WEIGHTS])
```

```python
import jax
import jax.numpy as jnp
from jax import lax
from jax.experimental import pallas as pl
from jax.experimental.pallas import tpu as pltpu

F32 = jnp.float32
BF16 = jnp.bfloat16

D_MODEL = 1024
N_META = 16
PAD = 112
EPS = 1e-6
CONV_W = 4
LRU_BLOCKS = 4
LRU_BLOCK = 256
RG_LRU_C = 8.0
SSD_HEADS = 16
SSD_P = 64
SSD_N = 128
SSD_Q = 64
SSD_GROUPS = 2
SSD_GW = 512
SSD_CONV_DIM = 1536
SB_HEADS = 16
SB_D = 64
SB_T = 128
EVEN_IN = 4624
EVEN_NP = 4864
N_DEV = 8

ADAM_LR = 0.001
ADAM_B1 = 0.9
ADAM_B2 = 0.999
ADAM_EPS = 1e-08
ADAM_WD = 0.01
ADAM_STEP = 10

VMEM_LIMIT = 56 * 1024 * 1024


def _cparams(*sem):
    return pltpu.CompilerParams(dimension_semantics=sem, vmem_limit_bytes=VMEM_LIMIT)


def _row_tile(t):
    for c in (512, 256, 128):
        if t % c == 0:
            return c
    raise ValueError(t)


def _dot(a, b):
    return jnp.dot(a, b, preferred_element_type=F32)


def _dot_nt(a, b):
    return lax.dot_general(a, b, (((1,), (1,)), ((), ())), preferred_element_type=F32)


def _dot_tn(a, b):
    return lax.dot_general(a, b, (((0,), (0,)), ((), ())), preferred_element_type=F32)


def _split3(a):
    a1 = a.astype(BF16)
    r1 = a - a1.astype(F32)
    a2 = r1.astype(BF16)
    a3 = (r1 - a2.astype(F32)).astype(BF16)
    return a1, a2, a3


def _xdot_r(a, m01):
    a1, a2, a3 = _split3(a)
    return _dot(a1, m01) + _dot(a2, m01) + _dot(a3, m01)


def _xdot_l(m01, a):
    a1, a2, a3 = _split3(a)
    return _dot(m01, a1) + _dot(m01, a2) + _dot(m01, a3)


def _sigmoid(x):
    return 0.5 * jnp.tanh(0.5 * x) + 0.5


def _softplus(x):
    return jnp.maximum(x, 0.0) + jnp.log(1.0 + jnp.exp(-jnp.abs(x)))


def _iota(shape, dim):
    return lax.broadcasted_iota(jnp.int32, shape, dim)


def _block_scan(a_ref, b_ref, o_ref, reverse=False):
    n, w = a_ref.shape
    nb = n // 8
    unroll = 4 if nb % 4 == 0 else 1
    row = _iota((8, w), 0)

    def block(blk, carry):
        rows = pl.ds(pl.multiple_of(blk * 8, 8), 8)
        a, b = a_ref[rows, :], b_ref[rows, :]
        for k in (1, 2, 4):
            keep = (row < 8 - k) if reverse else (row >= k)
            shift = 8 - k if reverse else k
            a_sh = jnp.where(keep, pltpu.roll(a, shift, 0), 1.0)
            b_sh = jnp.where(keep, pltpu.roll(b, shift, 0), 0.0)
            b = a * b_sh + b
            a = a * a_sh
        o = a * carry + b
        o_ref[rows, :] = o
        return o[0:1, :] if reverse else o[7:8, :]

    def trip(i, carry):
        for u in range(unroll):
            j = i * unroll + u
            carry = block(nb - 1 - j if reverse else j, carry)
        return carry

    lax.fori_loop(0, nb // unroll, trip, jnp.zeros((1, w), F32))


def _chunks(n, c):
    return [(s, min(c, n - s)) for s in range(0, n, c)]


def _mm_nn(a, b, out_dtype, name, res=None):
    m, k = a.shape
    _, n = b.shape
    tm = _row_tile(m)

    def body(*refs):
        if res is None:
            a_ref, b_ref, o_ref = refs
        else:
            a_ref, b_ref, r_ref, o_ref = refs
        av = a_ref[...].astype(BF16)
        for s, w in _chunks(n, 512):
            acc = _dot(av, b_ref[:, s:s + w])
            if res is not None:
                acc = acc + r_ref[:, s:s + w]
            o_ref[:, s:s + w] = acc.astype(out_dtype)

    in_specs = [pl.BlockSpec((tm, k), lambda i: (i, 0)), pl.BlockSpec((k, n), lambda i: (0, 0))]
    args = [a, b]
    if res is not None:
        in_specs.append(pl.BlockSpec((tm, n), lambda i: (i, 0)))
        args.append(res)
    return pl.pallas_call(
        body, grid=(m // tm,), in_specs=in_specs,
        out_specs=pl.BlockSpec((tm, n), lambda i: (i, 0)),
        out_shape=jax.ShapeDtypeStruct((m, n), out_dtype),
        compiler_params=_cparams("parallel"), name=name)(*args)


def _mm_tn(a, b, name):
    t, m = a.shape
    _, n = b.shape
    tk = _row_tile(t)
    halves = 2 if (m * n * 4 > 8 * 1024 * 1024 and n % 256 == 0) else 1
    tn = n // halves

    def body(a_ref, b_ref, o_ref):
        @pl.when(pl.program_id(1) == 0)
        def _():
            o_ref[...] = jnp.zeros_like(o_ref)
        at = a_ref[...].astype(BF16).T
        for s, w in _chunks(tn, 512):
            o_ref[:, s:s + w] += _dot(at, b_ref[:, s:s + w].astype(BF16))

    return pl.pallas_call(
        body, grid=(halves, t // tk),
        in_specs=[pl.BlockSpec((tk, m), lambda j, s: (s, 0)), pl.BlockSpec((tk, tn), lambda j, s: (s, j))],
        out_specs=pl.BlockSpec((m, tn), lambda j, s: (0, j)),
        out_shape=jax.ShapeDtypeStruct((m, n), F32),
        compiler_params=_cparams("parallel", "arbitrary"), name=name)(a, b)


def _norm_fwd(h, w, name):
    t, d = h.shape
    tm = _row_tile(t)

    def body(h_ref, w_ref, u_ref):
        x = h_ref[...]
        r = lax.rsqrt(jnp.mean(x * x, axis=-1, keepdims=True) + EPS)
        u_ref[...] = (x * r * w_ref[...]).astype(BF16)

    return pl.pallas_call(
        body, grid=(t // tm,),
        in_specs=[pl.BlockSpec((tm, d), lambda i: (i, 0)), pl.BlockSpec((1, d), lambda i: (0, 0))],
        out_specs=pl.BlockSpec((tm, d), lambda i: (i, 0)),
        out_shape=jax.ShapeDtypeStruct((t, d), BF16),
        compiler_params=_cparams("parallel"), name=name)(h, w)


def _norm_bwd(h, w, du, dres, name):
    t, d = h.shape
    tm = _row_tile(t)

    def body(h_ref, w_ref, du_ref, dr_ref, dh_ref, dw_ref):
        @pl.when(pl.program_id(0) == 0)
        def _():
            dw_ref[...] = jnp.zeros_like(dw_ref)
        x = h_ref[...]
        r = lax.rsqrt(jnp.mean(x * x, axis=-1, keepdims=True) + EPS)
        xh = x * r
        du_ = du_ref[...]
        g = du_ * w_ref[...]
        dh_ref[...] = dr_ref[...] + r * (g - xh * jnp.mean(g * xh, axis=-1, keepdims=True))
        dw_ref[...] += jnp.sum(du_ * xh, axis=0, keepdims=True)

    return pl.pallas_call(
        body, grid=(t // tm,),
        in_specs=[pl.BlockSpec((tm, d), lambda i: (i, 0)), pl.BlockSpec((1, d), lambda i: (0, 0)),
                  pl.BlockSpec((tm, d), lambda i: (i, 0)), pl.BlockSpec((tm, d), lambda i: (i, 0))],
        out_specs=[pl.BlockSpec((tm, d), lambda i: (i, 0)), pl.BlockSpec((1, d), lambda i: (0, 0))],
        out_shape=[jax.ShapeDtypeStruct((t, d), F32), jax.ShapeDtypeStruct((1, d), F32)],
        compiler_params=_cparams("arbitrary"), name=name)(h, w, du, dres)


def _final_loss(h2, w, tgt, lp):
    t, d = h2.shape
    bsz, seq, _ = tgt.shape
    nblk = lp // SB_T

    def body(h_ref, w_ref, t_ref, loss_ref, dh_ref, dw_ref):
        b, i = pl.program_id(0), pl.program_id(1)

        @pl.when((b == 0) & (i == 0))
        def _():
            loss_ref[...] = jnp.zeros_like(loss_ref)
            dw_ref[...] = jnp.zeros_like(dw_ref)

        @pl.when(i == 0)
        def _():
            dh_ref[...] = jnp.zeros_like(dh_ref)

        @pl.when(i > 0)
        def _():
            x = h_ref[...]
            r = lax.rsqrt(jnp.mean(x * x, axis=-1, keepdims=True) + EPS)
            xh = x * r
            wv = w_ref[...]
            diff = xh * wv - t_ref[0]
            loss_ref[...] += 0.5 * jnp.sum(jnp.mean(diff * diff, axis=-1, keepdims=True), axis=0, keepdims=True)
            dy = diff * (1.0 / d)
            g = dy * wv
            dh_ref[...] = r * (g - xh * jnp.mean(g * xh, axis=-1, keepdims=True))
            dw_ref[...] += jnp.sum(dy * xh, axis=0, keepdims=True)

    return pl.pallas_call(
        body, grid=(bsz, nblk),
        in_specs=[pl.BlockSpec((SB_T, d), lambda b, i: (b * nblk + i, 0)), pl.BlockSpec((1, d), lambda b, i: (0, 0)),
                  pl.BlockSpec((1, SB_T, d), lambda b, i: (b, jnp.maximum(i - 1, 0), 0))],
        out_specs=[pl.BlockSpec((1, 128), lambda b, i: (0, 0)), pl.BlockSpec((SB_T, d), lambda b, i: (b * nblk + i, 0)),
                   pl.BlockSpec((1, d), lambda b, i: (0, 0))],
        out_shape=[jax.ShapeDtypeStruct((1, 128), F32), jax.ShapeDtypeStruct((t, d), F32), jax.ShapeDtypeStruct((1, d), F32)],
        compiler_params=_cparams("arbitrary", "arbitrary"), name="final_loss")(h2, w, tgt)


def _shift_down(x, j):
    return x if j == 0 else pltpu.roll(x, j, 0)


def _shift_up(x, j):
    return x if j == 0 else pltpu.roll(x, x.shape[0] - j, 0)


def _conv(x, cw, cb):
    out = cb + cw[CONV_W - 1:CONV_W, :] * x
    for j in range(1, CONV_W):
        out = out + cw[CONV_W - 1 - j:CONV_W - j, :] * _shift_down(x, j)
    return out


def _conv_bwd_x(dy, cw):
    out = cw[CONV_W - 1:CONV_W, :] * dy
    for j in range(1, CONV_W):
        out = out + cw[CONV_W - 1 - j:CONV_W - j, :] * _shift_up(dy, j)
    return out


def _conv_bwd_w(dcw_ref, dy, x):
    for j in range(CONV_W):
        k = CONV_W - 1 - j
        dcw_ref[k:k + 1, :] += jnp.sum(dy * _shift_down(x, j), axis=0, keepdims=True)


def _lru_forward_block(x, cw, cb, wa, ba, wx, bx, lam, a_ref, b_ref, h_ref):
    lp = x.shape[0]
    lx = _conv(x, cw, cb)
    lxb = lx.astype(BF16)
    r = _sigmoid(_dot(lxb, wa) + ba)
    i = _sigmoid(_dot(lxb, wx) + bx)
    spl = _softplus(-lam)
    log_a = (-RG_LRU_C) * r * spl
    a = jnp.exp(log_a)
    mult = jnp.sqrt(-jnp.tanh(log_a) * (1.0 + a * a))
    valid = _iota((lp, 1), 0) >= PAD
    a_ref[...] = a
    b_ref[...] = jnp.where(valid, mult * i * lx, 0.0)
    _block_scan(a_ref, b_ref, h_ref)
    return lx, lxb, r, i, spl, a, mult, valid


def _lru_specs(lp):
    w = LRU_BLOCK
    return [
        pl.BlockSpec((lp, w), lambda g, b: (b, g)),
        pl.BlockSpec((lp, w), lambda g, b: (b, LRU_BLOCKS + g)),
        pl.BlockSpec((CONV_W, w), lambda g, b: (0, g)),
        pl.BlockSpec((1, w), lambda g, b: (0, g)),
        pl.BlockSpec((1, w, w), lambda g, b: (g, 0, 0)),
        pl.BlockSpec((1, w), lambda g, b: (0, g)),
        pl.BlockSpec((1, w, w), lambda g, b: (g, 0, 0)),
        pl.BlockSpec((1, w), lambda g, b: (0, g)),
        pl.BlockSpec((1, w), lambda g, b: (0, g)),
    ]


def _lru_fwd(proj, cw, cb, wa, ba, wx, bx, lam, bsz, lp):
    w = LRU_BLOCK

    def body(x_ref, g_ref, cw_ref, cb_ref, wa_ref, ba_ref, wx_ref, bx_ref, lam_ref, y_ref, a_s, b_s, h_s):
        _lru_forward_block(x_ref[...], cw_ref[...], cb_ref[...], wa_ref[0], ba_ref[...], wx_ref[0], bx_ref[...],
                           lam_ref[...], a_s, b_s, h_s)
        gate = g_ref[...]
        y_ref[...] = (h_s[...] * gate * _sigmoid(gate)).astype(BF16)

    return pl.pallas_call(
        body, grid=(LRU_BLOCKS, bsz), in_specs=_lru_specs(lp),
        out_specs=pl.BlockSpec((lp, w), lambda g, b: (b, g)),
        out_shape=jax.ShapeDtypeStruct((bsz * lp, 2 * LRU_BLOCKS * w), BF16),
        scratch_shapes=[pltpu.VMEM((lp, w), F32)] * 3,
        compiler_params=_cparams("parallel", "arbitrary"), name="lru_fwd")(proj, proj, cw, cb, wa, ba, wx, bx, lam)


def _lru_bwd(proj, dmixed, cw, cb, wa, ba, wx, bx, lam, bsz, lp):
    w = LRU_BLOCK

    def body(x_ref, g_ref, cw_ref, cb_ref, wa_ref, ba_ref, wx_ref, bx_ref, lam_ref, dy_ref,
             dx_ref, dg_ref, dcw_ref, dcb_ref, dwa_ref, dba_ref, dwx_ref, dbx_ref, dlam_ref, a_s, b_s, h_s, dh_s):
        @pl.when(pl.program_id(1) == 0)
        def _():
            for ref in (dcw_ref, dcb_ref, dwa_ref, dba_ref, dwx_ref, dbx_ref, dlam_ref):
                ref[...] = jnp.zeros_like(ref)

        x = x_ref[...]
        cwv = cw_ref[...]
        wav, wxv, lam_ = wa_ref[0], wx_ref[0], lam_ref[...]
        lx, lxb, r, i, spl, a, mult, valid = _lru_forward_block(
            x, cwv, cb_ref[...], wav, ba_ref[...], wxv, bx_ref[...], lam_, a_s, b_s, h_s)
        gate = g_ref[...]
        sg = _sigmoid(gate)
        dy = dy_ref[...]
        h = h_s[...]
        dg_ref[...] = (dy * h * sg * (1.0 + gate * (1.0 - sg))).astype(BF16)
        b_s[...] = dy * gate * sg
        a_s[...] = jnp.where(_iota((lp, 1), 0) < lp - 1, _shift_up(a, 1), 0.0)
        _block_scan(a_s, b_s, dh_s, reverse=True)
        dh = dh_s[...]
        hprev = jnp.where(_iota((lp, 1), 0) >= 1, _shift_down(h, 1), 0.0)
        db = jnp.where(valid, dh, 0.0)
        dmult = db * i * lx
        di = db * mult * lx
        dlx = db * mult * i
        inv_mult = jnp.where(mult > 0.0, 1.0 / mult, 0.0)
        dlog_a = dh * hprev * a - dmult * (a * a) * inv_mult
        drp = dlog_a * ((-RG_LRU_C) * spl) * r * (1.0 - r)
        dip = di * i * (1.0 - i)
        dspl = jnp.sum(dlog_a * ((-RG_LRU_C) * r), axis=0, keepdims=True)
        dlam_ref[...] += dspl * (-_sigmoid(-lam_))
        dba_ref[...] += jnp.sum(drp, axis=0, keepdims=True)
        dbx_ref[...] += jnp.sum(dip, axis=0, keepdims=True)
        drpb, dipb = drp.astype(BF16), dip.astype(BF16)
        dwa_ref[0] += _dot_tn(lxb, drpb)
        dwx_ref[0] += _dot_tn(lxb, dipb)
        dlx = dlx + _dot_nt(drpb, wav) + _dot_nt(dipb, wxv)
        dcb_ref[...] += jnp.sum(dlx, axis=0, keepdims=True)
        _conv_bwd_w(dcw_ref, dlx, x)
        dx_ref[...] = _conv_bwd_x(dlx, cwv).astype(BF16)

    t = bsz * lp
    vec = pl.BlockSpec((1, w), lambda g, b: (0, g))
    mat = pl.BlockSpec((1, w, w), lambda g, b: (g, 0, 0))
    act = pl.BlockSpec((lp, w), lambda g, b: (b, g))
    return pl.pallas_call(
        body, grid=(LRU_BLOCKS, bsz), in_specs=_lru_specs(lp) + [act],
        out_specs=[act, act, pl.BlockSpec((CONV_W, w), lambda g, b: (0, g)), vec, mat, vec, mat, vec, vec],
        out_shape=[jax.ShapeDtypeStruct((t, 1024), BF16), jax.ShapeDtypeStruct((t, 1024), BF16),
                   jax.ShapeDtypeStruct((CONV_W, 1024), F32), jax.ShapeDtypeStruct((1, 1024), F32),
                   jax.ShapeDtypeStruct((LRU_BLOCKS, w, w), F32), jax.ShapeDtypeStruct((1, 1024), F32),
                   jax.ShapeDtypeStruct((LRU_BLOCKS, w, w), F32), jax.ShapeDtypeStruct((1, 1024), F32),
                   jax.ShapeDtypeStruct((1, 1024), F32)],
        scratch_shapes=[pltpu.VMEM((lp, w), F32)] * 4,
        compiler_params=_cparams("parallel", "arbitrary"), name="lru_bwd")(
            proj, proj, cw, cb, wa, ba, wx, bx, lam, dmixed)


def _sb_masks():
    j = _iota((SB_T, 2 * SB_T), 0)
    s = _iota((SB_T, 2 * SB_T), 1)
    right = ((s < SB_T) & (j > s)) | (s >= SB_T)
    left = ((s < SB_T) & (j < s)) | (s >= SB_T)
    return jnp.where(right, -1.0, 0.0).astype(BF16), left.astype(BF16)


def _xdot2(a, m01):
    a1 = a.astype(BF16)
    a2 = (a - a1.astype(F32)).astype(BF16)
    return _dot(a1, m01) + _dot(a2, m01)


def _sb_query_block(lp):
    for tq in (512, 256, 128):
        if (lp - SB_T) % tq == 0:
            return tq, (lp - SB_T) // tq
    raise ValueError(lp)


def _sb_key_rows(kj):
    return pl.ds(kj * SB_T if isinstance(kj, int) else pl.multiple_of(kj * SB_T, SB_T), SB_T)


def _sb_valid(q0, tq, kj):
    t_pos = q0 + _iota((tq, SB_T), 0)
    s_pos = kj * SB_T + _iota((tq, SB_T), 1)
    return (s_pos < t_pos) & (s_pos >= PAD)


def _sb_by_head(ref, rows):
    t = ref[rows, :]
    head0 = _iota((SB_T, 128), 1) < SB_D
    return jnp.concatenate([jnp.where(head0, t, 0.0), jnp.where(head0, 0.0, t)], axis=0).astype(BF16)


def _sb_tile(qb, k_ref, q0, kj, carries, m_right_neg, masked):
    rows_k = _sb_key_rows(kj)
    z2 = _dot_nt(qb, _sb_by_head(k_ref, rows_k))
    valid = _sb_valid(q0, qb.shape[0], kj) if masked else None
    out = []
    for h in range(2):
        z = z2[:, h * SB_T:(h + 1) * SB_T]
        sp = _softplus(z)
        rs = _dot((jnp.where(valid, sp, 0.0) if masked else sp).astype(BF16), m_right_neg)
        lb = z - sp
        wgt = jnp.exp(lb + rs[:, :SB_T] + carries[h])
        if masked:
            wgt = jnp.where(valid, wgt, 0.0)
        out.append((lb, wgt, carries[h] + rs[:, SB_T:]))
    return rows_k, valid, out


def _tail(x, row0):
    return x if row0 == 0 else x[row0:]


def _merge(old, new_tail, row0):
    return new_tail if row0 == 0 else jnp.concatenate([old[:row0], new_tail], axis=0)


def _sb_sweep(step, c, blk, tq, first, leftwards):
    if first:
        return step(0, True, 0)(c)
    r = tq // SB_T
    u = 4 if r % 4 == 0 else 1
    lo = 1 + r * blk

    def diag(c):
        for d in (range(r - 1, -1, -1) if leftwards else range(r)):
            c = step(lo + d, True, SB_T * d)(c)
        return c

    def inner(c):
        def trip(i, c):
            for j in range(u):
                d = u * i + j
                c = step(lo - 1 - d if leftwards else 1 + d, False, 0)(c)
            return c
        return lax.fori_loop(0, (r // u) * blk, trip, c)

    if leftwards:
        return step(0, True, 0)(inner(diag(c)))
    return diag(inner(step(0, True, 0)(c)))


def _sb_specs(lp):
    nh = SB_HEADS // 2
    return [pl.BlockSpec((lp, 128), lambda b, p: (b, p)), pl.BlockSpec((lp, 128), lambda b, p: (b, nh + p)),
            pl.BlockSpec((lp, 128), lambda b, p: (b, 2 * nh + p)), pl.BlockSpec((lp, 128), lambda b, p: (b, 3 * nh + p))]


def _sb_fwd(qkvg, bsz, lp):
    tq, nb = _sb_query_block(lp)
    scale = SB_D ** -0.5

    def body(q_ref, k_ref, v_ref, g_ref, o_ref, og_ref):
        m_right_neg, _ = _sb_masks()

        def q_block(blk, rows, first):
            q0 = 0 if first else (1 + (rows // SB_T) * blk) * SB_T
            rows_q = pl.ds(q0 if first else pl.multiple_of(q0, SB_T), rows)
            qb = (q_ref[rows_q, :] * scale).astype(BF16)

            def step(kj, masked, row0):
                def run(c):
                    acc, car0, car1 = c
                    rows_k, _, ((_, w0, new0), (_, w1, new1)) = _sb_tile(
                        _tail(qb, row0), k_ref, q0 + row0, kj, (_tail(car0, row0), _tail(car1, row0)), m_right_neg, masked)
                    w2 = jnp.concatenate([w0.astype(BF16), w1.astype(BF16)], axis=1)
                    upd = _tail(acc, row0) + _dot(w2, _sb_by_head(v_ref, rows_k))
                    return _merge(acc, upd, row0), _merge(car0, new0, row0), _merge(car1, new1, row0)
                return run

            zero = jnp.zeros((rows, SB_T), F32)
            o_ref[rows_q, :] = _sb_sweep(step, (zero, zero, zero), blk, rows, first, True)[0]

        q_block(0, SB_T, True)

        def big_block(i, _):
            q_block(i, tq, False)
            return 0

        lax.fori_loop(0, nb, big_block, 0)
        gate = g_ref[...]
        og_ref[...] = (o_ref[...] * gate * _sigmoid(gate)).astype(BF16)

    t = bsz * lp
    blk = pl.BlockSpec((lp, 128), lambda b, p: (b, p))
    return pl.pallas_call(
        body, grid=(bsz, SB_HEADS // 2), in_specs=_sb_specs(lp), out_specs=[blk, blk],
        out_shape=[jax.ShapeDtypeStruct((t, 1024), F32), jax.ShapeDtypeStruct((t, 1024), BF16)],
        compiler_params=_cparams("parallel", "parallel"), name="sb_fwd")(qkvg, qkvg, qkvg, qkvg)


def _sb_bwd(qkvg, o, dog, bsz, lp):
    tq, nb = _sb_query_block(lp)
    nk = lp // SB_T
    scale = SB_D ** -0.5

    def body(q_ref, k_ref, v_ref, g_ref, o_ref, dog_ref, dq_ref, dk_ref, dv_ref, dg_ref, do_s, dk_s, dv_s, e_s, sig_s):
        m_right_neg, m_left = _sb_masks()
        gate = g_ref[...]
        sg = _sigmoid(gate)
        dog = dog_ref[...]
        do_s[...] = dog * gate * sg
        dg_ref[...] = (dog * o_ref[...] * sg * (1.0 + gate * (1.0 - sg))).astype(BF16)
        dk_s[...] = jnp.zeros_like(dk_s)
        dv_s[...] = jnp.zeros_like(dv_s)
        head0_rows = _iota((128, SB_T), 0) < SB_D

        def add_transposed(acc_ref, kj, lhs_t, both):
            res = _dot(lhs_t, both)
            acc_ref[kj] += jnp.where(head0_rows, res[:, :SB_T], res[:, SB_T:])

        def q_block(blk, rows, first):
            q0 = 0 if first else (1 + (rows // SB_T) * blk) * SB_T
            rows_q = pl.ds(q0 if first else pl.multiple_of(q0, SB_T), rows)
            qf = q_ref[rows_q, :] * scale
            qb = qf.astype(BF16)
            q_t = qf.T.astype(BF16)
            do_f = do_s[rows_q, :]
            do_b = do_f.astype(BF16)
            do_t = do_f.T.astype(BF16)

            def left(kj, masked, row0):
                def run(c):
                    rows_k, _, heads = _sb_tile(_tail(qb, row0), k_ref, q0 + row0, kj, (_tail(c[0], row0), _tail(c[1], row0)),
                                                m_right_neg, masked)
                    dw2 = _dot_nt(_tail(do_b, row0), _sb_by_head(v_ref, rows_k))
                    for h, (lb, wgt, _) in enumerate(heads):
                        e_s[2 * kj + h, row0:rows, :] = wgt * dw2[:, h * SB_T:(h + 1) * SB_T]
                        sig_s[2 * kj + h, row0:rows, :] = jnp.exp(lb)
                    add_transposed(dv_s, kj, do_t[:, row0:], jnp.concatenate([hd[1].astype(BF16) for hd in heads], axis=1))
                    return _merge(c[0], heads[0][2], row0), _merge(c[1], heads[1][2], row0)
                return run

            zero = jnp.zeros((rows, SB_T), F32)
            _sb_sweep(left, (zero, zero), blk, rows, first, True)

            def right(kj, masked, row0):
                def run(c):
                    dq, ecar = c[0], list(c[1:])
                    rows_k = _sb_key_rows(kj)
                    valid = _sb_valid(q0 + row0, rows - row0, kj) if masked else None
                    dzs = []
                    for h in range(2):
                        e = e_s[2 * kj + h, row0:rows, :]
                        se = _dot(e.astype(BF16), m_left)
                        ec = _tail(ecar[h], row0)
                        dz = e - sig_s[2 * kj + h, row0:rows, :] * (e + se[:, :SB_T] + ec)
                        if masked:
                            dz = jnp.where(valid, dz, 0.0)
                        dzs.append(dz.astype(BF16))
                        ecar[h] = _merge(ecar[h], ec + se[:, SB_T:], row0)
                    dz2 = jnp.concatenate(dzs, axis=1)
                    add_transposed(dk_s, kj, q_t[:, row0:], dz2)
                    upd = _tail(dq, row0) + _dot(dz2, _sb_by_head(k_ref, rows_k))
                    return _merge(dq, upd, row0), ecar[0], ecar[1]
                return run

            dq = _sb_sweep(right, (zero, zero, zero), blk, rows, first, False)[0]
            dq_ref[rows_q, :] = (dq * scale).astype(BF16)

        q_block(0, SB_T, True)

        def big_block(i, _):
            q_block(i, tq, False)
            return 0

        lax.fori_loop(0, nb, big_block, 0)
        for kj in range(nk):
            dk_ref[kj * SB_T:(kj + 1) * SB_T, :] = dk_s[kj].T.astype(BF16)
            dv_ref[kj * SB_T:(kj + 1) * SB_T, :] = dv_s[kj].T.astype(BF16)

    t = bsz * lp
    blk = pl.BlockSpec((lp, 128), lambda b, p: (b, p))
    shp = jax.ShapeDtypeStruct((t, 1024), BF16)
    return pl.pallas_call(
        body, grid=(bsz, SB_HEADS // 2), in_specs=_sb_specs(lp) + [blk, blk], out_specs=[blk] * 4,
        out_shape=[shp] * 4,
        scratch_shapes=[pltpu.VMEM((lp, 128), F32)] + [pltpu.VMEM((nk, 128, SB_T), F32)] * 2
        + [pltpu.VMEM((2 * nk, tq, SB_T), F32)] * 2,
        compiler_params=_cparams("parallel", "parallel"), name="sb_bwd")(qkvg, qkvg, qkvg, qkvg, o, dog)


XBC_COL0 = 3072 // 256
DT_COL0 = 4608 // 128
DT_L = 128


def _ssd_pre_fwd(proj, cw, cb, bsz, lp):
    def body(x_ref, cw_ref, cb_ref, o_ref):
        pre = _conv(x_ref[...], cw_ref[...], cb_ref[...])
        o_ref[...] = pre * _sigmoid(pre)

    return pl.pallas_call(
        body, grid=(bsz, SSD_CONV_DIM // 256),
        in_specs=[pl.BlockSpec((lp, 256), lambda b, j: (b, XBC_COL0 + j)), pl.BlockSpec((CONV_W, 256), lambda b, j: (0, j)),
                  pl.BlockSpec((1, 256), lambda b, j: (0, j))],
        out_specs=pl.BlockSpec((lp, 256), lambda b, j: (b, j)),
        out_shape=jax.ShapeDtypeStruct((bsz * lp, SSD_CONV_DIM), F32),
        compiler_params=_cparams("parallel", "parallel"), name="ssd_pre_fwd")(proj, cw, cb)


def _ssd_pre_bwd(proj, dact, cw, cb, bsz, lp):
    def body(x_ref, d_ref, cw_ref, cb_ref, dx_ref, dcw_ref, dcb_ref):
        @pl.when(pl.program_id(1) == 0)
        def _():
            dcw_ref[...] = jnp.zeros_like(dcw_ref)
            dcb_ref[...] = jnp.zeros_like(dcb_ref)
        x = x_ref[...]
        cwv = cw_ref[...]
        pre = _conv(x, cwv, cb_ref[...])
        s = _sigmoid(pre)
        dpre = d_ref[...].astype(F32) * s * (1.0 + pre * (1.0 - s))
        dcb_ref[...] += jnp.sum(dpre, axis=0, keepdims=True)
        _conv_bwd_w(dcw_ref, dpre, x)
        dx_ref[...] = _conv_bwd_x(dpre, cwv).astype(BF16)

    return pl.pallas_call(
        body, grid=(SSD_CONV_DIM // 256, bsz),
        in_specs=[pl.BlockSpec((lp, 256), lambda j, b: (b, XBC_COL0 + j)), pl.BlockSpec((lp, 256), lambda j, b: (b, j)),
                  pl.BlockSpec((CONV_W, 256), lambda j, b: (0, j)), pl.BlockSpec((1, 256), lambda j, b: (0, j))],
        out_specs=[pl.BlockSpec((lp, 256), lambda j, b: (b, j)), pl.BlockSpec((CONV_W, 256), lambda j, b: (0, j)),
                   pl.BlockSpec((1, 256), lambda j, b: (0, j))],
        out_shape=[jax.ShapeDtypeStruct((bsz * lp, SSD_CONV_DIM), BF16), jax.ShapeDtypeStruct((CONV_W, SSD_CONV_DIM), F32),
                   jax.ShapeDtypeStruct((1, SSD_CONV_DIM), F32)],
        compiler_params=_cparams("parallel", "arbitrary"), name="ssd_pre_bwd")(proj, dact, cw, cb)


def _split2(a):
    a1 = a.astype(BF16)
    return a1, (a - a1.astype(F32)).astype(BF16)


def _xdot2_nt(a, m01):
    a1, a2 = _split2(a)
    return _dot_nt(a1, m01) + _dot_nt(a2, m01)


def _xdot2_l(m01, a):
    a1, a2 = _split2(a)
    return _dot(m01, a1) + _dot(m01, a2)


class _SsdConsts:
    def __init__(self, g):
        q, gw = SSD_Q, SSD_GW
        head_of_lane = lax.shift_right_logical(_iota((DT_L, gw), 1), 6)
        self.sel = (_iota((DT_L, gw), 0) == 8 * g + head_of_lane).astype(BF16)
        r = _iota((q, gw), 0)
        c = jnp.bitwise_and(_iota((q, gw), 1), q - 1)
        self.diag = r == c
        self.diag_b = self.diag.astype(BF16)
        self.lower = c <= r
        self.upper = c >= r
        self.bd = lax.shift_right_logical(_iota((128, 128), 0), 6) == lax.shift_right_logical(_iota((128, 128), 1), 6)
        jj, ll = _iota((q, q), 1), _iota((q, q), 0)
        self.tri = (jj <= ll).astype(BF16)
        self.tri_t = (jj >= ll).astype(BF16)
        self.last = _iota((q, 1), 0) == q - 1


def _ssd_prepass(k, dt_ref, bias_ref, alog_ref, dtm_s, abc_s, lp):
    valid = _iota((lp, 1), 0) >= PAD
    sp_in = dt_ref[...] + bias_ref[...]
    dtm = jnp.where(valid, _softplus(sp_in), 0.0)
    aneg = -jnp.exp(alog_ref[...])
    dtm_s[...] = dtm
    abc_s[...] = _xdot_r(dtm * aneg, k.sel)
    return valid, sp_in, aneg


def _ssd_loop(nc, chunk, init):
    u = 2 if nc % 2 == 0 else 1

    def trip(i, carry):
        for j in range(u):
            carry = chunk(i * u + j, carry)
        return carry

    return lax.fori_loop(0, nc // u, trip, init)


def _ssd_chunk(k, rows, xs_ref, b_ref, c_ref, dtm_s, abc_s):
    bc = _xdot_l(k.tri, abc_s[rows, :])
    tt = jnp.sum(jnp.where(k.diag, bc, 0.0), axis=0, keepdims=True)
    dtbc = _xdot2(dtm_s[rows, :], k.sel)
    xs = xs_ref[rows, :]
    x = xs * dtbc
    bb = b_ref[rows, :].astype(BF16)
    cc = c_ref[rows, :].astype(BF16)
    tot = bc[SSD_Q - 1:SSD_Q, :]
    return bc, tt, dtbc, xs, x, bb, cc, tot, _pair_blocks(k, x)


def _pair_blocks(k, x):
    out = []
    for p in range(SSD_GW // 128):
        xp = x[:, 128 * p:128 * (p + 1)]
        out.append(jnp.where(k.bd, jnp.concatenate([xp, xp], axis=0), 0.0).astype(BF16))
    return out


def _pair_dot(a, blocks, transposed=False):
    dot = _dot_nt if transposed else _dot
    return jnp.concatenate([dot(a[:, 128 * p:128 * (p + 1)], blk) for p, blk in enumerate(blocks)], axis=1)


def _ssd_specs(lp, order):
    ix = (lambda f: (lambda b, g: f(b, g))) if order == "bg" else (lambda f: (lambda g, b: f(b, g)))
    return [pl.BlockSpec((lp, SSD_GW), ix(lambda b, g: (b, g))),
            pl.BlockSpec((lp, SSD_N), ix(lambda b, g: (b, 1024 // SSD_N + g))),
            pl.BlockSpec((lp, SSD_N), ix(lambda b, g: (b, 1280 // SSD_N + g))),
            pl.BlockSpec((lp, DT_L), ix(lambda b, g: (b, DT_COL0))),
            pl.BlockSpec((1, DT_L), ix(lambda b, g: (0, 0))),
            pl.BlockSpec((1, DT_L), ix(lambda b, g: (0, 0))),
            pl.BlockSpec((1, DT_L), ix(lambda b, g: (0, 0)))]


def _ssd_fwd(xbc, proj, dt_bias, a_log, dskip, bsz, lp):
    nc = lp // SSD_Q

    def body(xs_ref, b_ref, c_ref, dt_ref, bias_ref, alog_ref, dsk_ref, y_ref, dtm_s, abc_s):
        k = _SsdConsts(pl.program_id(1))
        _ssd_prepass(k, dt_ref, bias_ref, alog_ref, dtm_s, abc_s, lp)
        dvec = _xdot_r(jnp.broadcast_to(dsk_ref[...], (8, DT_L)), k.sel)[0:1, :]

        def chunk(c, state):
            rows = pl.ds(pl.multiple_of(c * SSD_Q, SSD_Q), SSD_Q)
            bc, tt, _, xs, x, bb, cc, tot, xbd = _ssd_chunk(k, rows, xs_ref, b_ref, c_ref, dtm_s, abc_s)
            lm = jnp.where(k.lower, jnp.exp(jnp.minimum(bc - tt, 0.0)), 0.0)
            g_all = _dot(_dot_nt(cc, bb).astype(BF16), k.diag_b) * lm
            y = _pair_dot(g_all.astype(BF16), xbd) + jnp.exp(bc) * _dot(cc, state.astype(BF16)) + dvec * xs
            y_ref[rows, :] = y
            return jnp.exp(tot) * state + _dot_tn(bb, (jnp.exp(tot - bc) * x).astype(BF16))

        _ssd_loop(nc, chunk, jnp.zeros((SSD_N, SSD_GW), F32))

    return pl.pallas_call(
        body, grid=(bsz, SSD_GROUPS), in_specs=_ssd_specs(lp, "bg"),
        out_specs=pl.BlockSpec((lp, SSD_GW), lambda b, g: (b, g)),
        out_shape=jax.ShapeDtypeStruct((bsz * lp, 1024), F32),
        scratch_shapes=[pltpu.VMEM((lp, DT_L), F32), pltpu.VMEM((lp, SSD_GW), F32)],
        compiler_params=_cparams("parallel", "parallel"), name="ssd_fwd")(xbc, xbc, xbc, proj, dt_bias, a_log, dskip)


def _ssd_bwd(xbc, proj, dt_bias, a_log, dskip, dy, bsz, lp):
    nc = lp // SSD_Q

    def body(xs_ref, b_ref, c_ref, dt_ref, bias_ref, alog_ref, dsk_ref, dy_ref,
             dxs_ref, db_ref, dc_ref, ddt_ref, dbias_ref, dalog_ref, ddsk_ref, dtm_s, abc_s, st_s):
        @pl.when(pl.program_id(1) == 0)
        def _():
            for ref in (dbias_ref, dalog_ref, ddsk_ref):
                ref[...] = jnp.zeros_like(ref)

        k = _SsdConsts(pl.program_id(0))
        valid, sp_in, aneg = _ssd_prepass(k, dt_ref, bias_ref, alog_ref, dtm_s, abc_s, lp)
        dvec = _xdot_r(jnp.broadcast_to(dsk_ref[...], (8, DT_L)), k.sel)[0:1, :]

        def fwd_chunk(c, state):
            rows = pl.ds(pl.multiple_of(c * SSD_Q, SSD_Q), SSD_Q)
            st_s[c] = state.astype(BF16)
            bc, _, _, _, x, bb, _, tot, _ = _ssd_chunk(k, rows, xs_ref, b_ref, c_ref, dtm_s, abc_s)
            return jnp.exp(tot) * state + _dot_tn(bb, (jnp.exp(tot - bc) * x).astype(BF16))

        _ssd_loop(nc, fwd_chunk, jnp.zeros((SSD_N, SSD_GW), F32))

        def bwd_chunk(i, carry):
            dstate, daneg, ddsk = carry
            c = nc - 1 - i
            rows = pl.ds(pl.multiple_of(c * SSD_Q, SSD_Q), SSD_Q)
            bc, tt, dtbc, xs, x, bb, cc, tot, xbd = _ssd_chunk(k, rows, xs_ref, b_ref, c_ref, dtm_s, abc_s)
            sprev = st_s[c]
            dyc = dy_ref[rows, :]
            dyb = dyc.astype(BF16)
            seg = bc - tt
            lm = jnp.where(k.lower, jnp.exp(jnp.minimum(seg, 0.0)), 0.0)
            lm_t = jnp.where(k.upper, jnp.exp(jnp.minimum(-seg, 0.0)), 0.0)
            cb_all = _dot(_dot_nt(cc, bb).astype(BF16), k.diag_b)
            cbt_all = _dot(_dot_nt(bb, cc).astype(BF16), k.diag_b)
            g_all = cb_all * lm
            dg = _pair_dot(dyb, xbd, transposed=True)
            dx = _pair_dot((cbt_all * lm_t).astype(BF16), _pair_blocks(k, dyc))
            hh = dg * g_all
            ea = jnp.exp(bc)
            yo = ea * _dot(cc, sprev)
            col_h = jnp.sum(hh, axis=0, keepdims=True)
            dcb = _dot_nt((dg * lm).astype(BF16), k.diag_b).astype(BF16)
            dcs = (ea * dyc).astype(BF16)
            dstb = dstate.astype(BF16)
            dec = jnp.exp(tot - bc)
            w = dec * x
            dw = _dot(bb, dstb)
            dc_ref[rows, :] = (_dot(dcb, bb) + _dot_nt(dcs, sprev)).astype(BF16)
            db_ref[rows, :] = (_dot_tn(dcb, cc) + _dot_nt(w.astype(BF16), dstb)).astype(BF16)
            dx = dx + dec * dw
            etot = jnp.exp(tot)
            r8 = _iota((8, SSD_GW), 0)
            st_row = jnp.sum(dstate * sprev.astype(F32), axis=0, keepdims=True) * etot
            sk_row = jnp.sum(dyc * xs, axis=0, keepdims=True)
            small = jnp.where(r8 == 0, st_row, jnp.where(r8 == 1, sk_row, 0.0))
            q = SSD_Q
            sums = _xdot2_nt(jnp.concatenate([hh + dyc * yo - jnp.where(k.diag, col_h, 0.0) - dw * w, dw * w, dx * xs, small],
                                             axis=0), k.sel)
            kk = sums[q:2 * q]
            dtot = sums[3 * q:3 * q + 1] + jnp.sum(kk, axis=0, keepdims=True)
            dacum = sums[0:q] + jnp.where(k.last, dtot, 0.0)
            da = _xdot2_l(k.tri_t, dacum)
            dtm_c = dtm_s[rows, :]
            ddtm = da * aneg + sums[2 * q:3 * q]
            vrow = (c * SSD_Q + _iota((SSD_Q, 1), 0)) >= PAD
            ddt_ref[rows, :] = jnp.where(vrow, ddtm * _sigmoid(dt_ref[rows, :] + bias_ref[...]), 0.0)
            dxs_ref[rows, :] = (dx * dtbc + dvec * dyc).astype(BF16)
            daneg = daneg + jnp.sum(da * dtm_c, axis=0, keepdims=True)
            ddsk = ddsk + sums[3 * q + 1:3 * q + 2]
            dstate = etot * dstate + _dot_tn(cc, dcs)
            return dstate, daneg, ddsk

        zrow = jnp.zeros((1, DT_L), F32)
        _, daneg, ddsk = _ssd_loop(nc, bwd_chunk, (jnp.zeros((SSD_N, SSD_GW), F32), zrow, zrow))
        dbias_ref[...] += jnp.broadcast_to(jnp.sum(ddt_ref[...], axis=0, keepdims=True), (8, DT_L))
        dalog_ref[...] += jnp.broadcast_to(daneg * aneg, (8, DT_L))
        ddsk_ref[...] += jnp.broadcast_to(ddsk, (8, DT_L))

    t = bsz * lp
    par = pl.BlockSpec((8, DT_L), lambda g, b: (g, 0))
    par_shape = jax.ShapeDtypeStruct((8 * SSD_GROUPS, DT_L), F32)
    return pl.pallas_call(
        body, grid=(SSD_GROUPS, bsz),
        in_specs=_ssd_specs(lp, "gb") + [pl.BlockSpec((lp, SSD_GW), lambda g, b: (b, g))],
        out_specs=[pl.BlockSpec((lp, SSD_GW), lambda g, b: (b, g)), pl.BlockSpec((lp, SSD_N), lambda g, b: (b, g)),
                   pl.BlockSpec((lp, SSD_N), lambda g, b: (b, g)), pl.BlockSpec((lp, DT_L), lambda g, b: (b, g)), par, par, par],
        out_shape=[jax.ShapeDtypeStruct((t, 1024), BF16), jax.ShapeDtypeStruct((t, 256), BF16),
                   jax.ShapeDtypeStruct((t, 256), BF16), jax.ShapeDtypeStruct((t, SSD_GROUPS * DT_L), F32),
                   par_shape, par_shape, par_shape],
        scratch_shapes=[pltpu.VMEM((lp, DT_L), F32), pltpu.VMEM((lp, SSD_GW), F32), pltpu.VMEM((nc, SSD_N, SSD_GW), BF16)],
        compiler_params=_cparams("parallel", "arbitrary"), name="ssd_bwd")(
            xbc, xbc, xbc, proj, dt_bias, a_log, dskip, dy)


Z_COL0 = 2048 // SSD_GW


def _gnorm_fwd(y, proj, w, mixed, name="gnorm_fwd"):
    t = y.shape[0]
    tm = _row_tile(t)

    def body(y_ref, z_ref, w_ref, mixed_ref, o_ref):
        z = z_ref[...]
        gt = y_ref[...] * z * _sigmoid(z)
        r = lax.rsqrt(jnp.mean(gt * gt, axis=-1, keepdims=True) + EPS)
        o_ref[...] = (gt * r * w_ref[...]).astype(BF16)

    return pl.pallas_call(
        body, grid=(t // tm, SSD_GROUPS),
        in_specs=[pl.BlockSpec((tm, SSD_GW), lambda i, g: (i, g)), pl.BlockSpec((tm, SSD_GW), lambda i, g: (i, Z_COL0 + g)),
                  pl.BlockSpec((1, SSD_GW), lambda i, g: (0, g)), ANY_SPEC],
        out_specs=pl.BlockSpec((tm, SSD_GW), lambda i, g: (i, 1024 // SSD_GW + g)),
        out_shape=jax.ShapeDtypeStruct((t, 2048), BF16), input_output_aliases={3: 0},
        compiler_params=_cparams("parallel", "parallel"), name=name)(y, proj, w, mixed)


def _gnorm_bwd(y, proj, w, dmixed):
    t = y.shape[0]
    tm = _row_tile(t)

    def body(y_ref, z_ref, w_ref, d_ref, dy_ref, dz_ref, dw_ref):
        @pl.when(pl.program_id(1) == 0)
        def _():
            dw_ref[...] = jnp.zeros_like(dw_ref)
        z, yv, d = z_ref[...], y_ref[...], d_ref[...]
        s = _sigmoid(z)
        sz = z * s
        gt = yv * sz
        r = lax.rsqrt(jnp.mean(gt * gt, axis=-1, keepdims=True) + EPS)
        gh = gt * r
        dgn = d * w_ref[...]
        dgt = r * (dgn - gh * jnp.mean(dgn * gh, axis=-1, keepdims=True))
        dw_ref[...] += jnp.sum(d * gh, axis=0, keepdims=True)
        dy_ref[...] = dgt * sz
        dz_ref[...] = (dgt * yv * s * (1.0 + z * (1.0 - s))).astype(BF16)

    blk = pl.BlockSpec((tm, SSD_GW), lambda g, i: (i, g))
    return pl.pallas_call(
        body, grid=(SSD_GROUPS, t // tm),
        in_specs=[blk, pl.BlockSpec((tm, SSD_GW), lambda g, i: (i, Z_COL0 + g)), pl.BlockSpec((1, SSD_GW), lambda g, i: (0, g)),
                  pl.BlockSpec((tm, SSD_GW), lambda g, i: (i, 1024 // SSD_GW + g))],
        out_specs=[blk, blk, pl.BlockSpec((1, SSD_GW), lambda g, i: (0, g))],
        out_shape=[jax.ShapeDtypeStruct((t, 1024), F32), jax.ShapeDtypeStruct((t, 1024), BF16),
                   jax.ShapeDtypeStruct((1, 1024), F32)],
        compiler_params=_cparams("parallel", "arbitrary"), name="gnorm_bwd")(y, proj, w, dmixed)


def _meta_grad(dh0, bsz, lp):
    def body(d_ref, o_ref):
        @pl.when(pl.program_id(0) == 0)
        def _():
            o_ref[...] = jnp.zeros_like(o_ref)
        o_ref[...] += d_ref[...]

    return pl.pallas_call(
        body, grid=(bsz,),
        in_specs=[pl.BlockSpec((N_META, D_MODEL), lambda b: (b * (lp // N_META) + PAD // N_META, 0))],
        out_specs=pl.BlockSpec((N_META, D_MODEL), lambda b: (0, 0)),
        out_shape=jax.ShapeDtypeStruct((N_META, D_MODEL), F32),
        compiler_params=_cparams("arbitrary"), name="meta_grad")(dh0)


OTHER_CHIPS = ((1, 0), (0, 1), (1, 1))
ANY_SPEC = pl.BlockSpec(memory_space=pl.ANY)


class _Mesh:
    def __init__(self):
        self.x, self.y, self.c = lax.axis_index("x"), lax.axis_index("y"), lax.axis_index("c")

    def dev(self, fx, fy, fc):
        return (1 - self.x if fx else self.x, 1 - self.y if fy else self.y, 1 - self.c if fc else self.c)

    def slot(self, fx, fy, fc):
        px, py, pc = self.dev(fx, fy, fc)
        return 4 * px + 2 * py + pc

    def chip(self, fx, fy):
        px, py, _ = self.dev(fx, fy, 0)
        return 2 * px + py


def _remote(src, dst, send_sems, recv_sems, idx, dev):
    return pltpu.make_async_remote_copy(src_ref=src, dst_ref=dst, send_sem=send_sems.at[idx], recv_sem=recv_sems.at[idx],
                                        device_id=dev, device_id_type=pl.DeviceIdType.MESH)


def _gather_phases(ins, outs, send_sems, recv_sems):
    n = len(ins)
    me = _Mesh()

    def first():
        cps = []
        for a in range(n):
            mine = outs[a].at[me.slot(0, 0, 0)]
            cps.append(_remote(ins[a], mine, send_sems, recv_sems, (a, 0), me.dev(0, 0, 1)))
            for j, (fx, fy) in enumerate(OTHER_CHIPS):
                cps.append(_remote(ins[a], mine, send_sems, recv_sems, (a, 1 + j), me.dev(fx, fy, 0)))
        return cps

    def passed():
        return [_remote(outs[a].at[me.slot(fx, fy, 0)], outs[a].at[me.slot(fx, fy, 0)], send_sems, recv_sems, (a, 4 + j),
                        me.dev(0, 0, 1)) for a in range(n) for j, (fx, fy) in enumerate(OTHER_CHIPS)]

    def start():
        for cp in first():
            cp.start()

    def forward():
        fwd = passed()
        for a in range(n):
            for j, (fx, fy) in enumerate(OTHER_CHIPS):
                _remote(ins[a], outs[a].at[me.slot(fx, fy, 0)], send_sems, recv_sems, (a, 1 + j), me.dev(fx, fy, 0)).wait_recv()
                fwd[3 * a + j].start()

    def finish():
        for a in range(n):
            _remote(ins[a], outs[a].at[me.slot(0, 0, 1)], send_sems, recv_sems, (a, 0), me.dev(0, 0, 1)).wait_recv()
            for j, (fx, fy) in enumerate(OTHER_CHIPS):
                _remote(ins[a], outs[a].at[me.slot(fx, fy, 1)], send_sems, recv_sems, (a, 4 + j), me.dev(0, 0, 1)).wait_recv()
        for cp in first() + passed():
            cp.wait_send()

    return start, forward, finish


def _own_blocks(stacks, arrs):
    me = 4 * lax.axis_index("x") + 2 * lax.axis_index("y") + lax.axis_index("c")
    return [lax.dynamic_update_slice_in_dim(s, a[None], me, axis=0) for s, a in zip(stacks, arrs)]


def _gather_all(arrs, name):
    n = len(arrs)

    def body(*refs):
        start, forward, finish = _gather_phases(refs[:n], refs[n:2 * n], *refs[2 * n:])
        start()
        forward()
        finish()

    stacks = pl.pallas_call(
        body, in_specs=[ANY_SPEC] * n, out_specs=[ANY_SPEC] * n,
        out_shape=[jax.ShapeDtypeStruct((N_DEV,) + a.shape, a.dtype) for a in arrs],
        scratch_shapes=[pltpu.SemaphoreType.DMA((n, 7)), pltpu.SemaphoreType.DMA((n, 7))],
        compiler_params=pltpu.CompilerParams(has_side_effects=True), name=name)(*arrs)
    return _own_blocks(stacks, arrs)


def _mm_nn_gather(a, b, out_dtype, name, shards):
    m, k = a.shape
    _, n = b.shape
    tm = _row_tile(m)
    steps = m // tm
    ns = len(shards)

    def body(a_ref, b_ref, *rest):
        o_ref = rest[ns]
        start, forward, finish = _gather_phases(rest[:ns], rest[ns + 1:2 * ns + 1], *rest[2 * ns + 1:])
        i = pl.program_id(0)
        pl.when(i == 0)(start)
        pl.when(i == steps // 2)(forward)
        av = a_ref[...].astype(BF16)
        for s, w in _chunks(n, 512):
            o_ref[:, s:s + w] = _dot(av, b_ref[:, s:s + w]).astype(out_dtype)
        pl.when(i == steps - 1)(finish)

    outs = pl.pallas_call(
        body, grid=(steps,),
        in_specs=[pl.BlockSpec((tm, k), lambda i: (i, 0)), pl.BlockSpec((k, n), lambda i: (0, 0))] + [ANY_SPEC] * ns,
        out_specs=[pl.BlockSpec((tm, n), lambda i: (i, 0))] + [ANY_SPEC] * ns,
        out_shape=[jax.ShapeDtypeStruct((m, n), out_dtype)] + [jax.ShapeDtypeStruct((N_DEV,) + s.shape, s.dtype) for s in shards],
        scratch_shapes=[pltpu.SemaphoreType.DMA((ns, 7)), pltpu.SemaphoreType.DMA((ns, 7))],
        compiler_params=pltpu.CompilerParams(dimension_semantics=("arbitrary",), vmem_limit_bytes=VMEM_LIMIT,
                                             has_side_effects=True), name=name)(a, b, *shards)
    return outs[0], _own_blocks(outs[1:], shards)


def _swap_in_chip(arrs, name):
    n = len(arrs)

    def body(*refs):
        ins, got = refs[:n], refs[n:2 * n]
        send_sems, recv_sems = refs[2 * n:]
        me = _Mesh()
        sends = [_remote(ins[a].at[2 * j + (1 - me.c)], got[a].at[j], send_sems, recv_sems, (a, j), me.dev(0, 0, 1))
                 for a in range(n) for j in range(4)]
        for cp in sends:
            cp.start()
        for cp in sends:
            cp.wait()

    return pl.pallas_call(
        body, in_specs=[ANY_SPEC] * n, out_specs=[ANY_SPEC] * n,
        out_shape=[jax.ShapeDtypeStruct((4,) + a.shape[1:], a.dtype) for a in arrs],
        scratch_shapes=[pltpu.SemaphoreType.DMA((n, 4)), pltpu.SemaphoreType.DMA((n, 4))],
        compiler_params=pltpu.CompilerParams(has_side_effects=True), name=name)(*arrs)


def _add_pair(a, b, name):
    _, r, c = a.shape
    tr = _adam_rows(r)

    def body(a_ref, b_ref, o_ref):
        o_ref[...] = (a_ref[...].astype(F32) + b_ref[...].astype(F32)).astype(o_ref.dtype)

    blk = pl.BlockSpec((4, tr, c), lambda i: (0, i, 0))
    return pl.pallas_call(body, grid=(r // tr,), in_specs=[blk, blk], out_specs=blk,
                          out_shape=jax.ShapeDtypeStruct(a.shape, a.dtype), compiler_params=_cparams("parallel"), name=name)(a, b)


def _swap_chips(sums, gather, name):
    ns, ng = len(sums), len(gather)
    n = ns + ng

    def body(*refs):
        ins, outs = refs[:n], refs[n:2 * n]
        send_sems, recv_sems, local_sems = refs[2 * n:]
        me = _Mesh()
        sends, recvs = _chip_swap_copies(ins[:ns], outs[:ns], send_sems, recv_sems)
        local = []
        for a in range(ns, n):
            local.append(pltpu.make_async_copy(ins[a], outs[a].at[me.slot(0, 0, 0)], local_sems.at[a]))
            for k in range(1, N_DEV):
                flips = (k >> 2 & 1, k >> 1 & 1, k & 1)
                sends.append(_remote(ins[a], outs[a].at[me.slot(0, 0, 0)], send_sems, recv_sems, (a, k - 1), me.dev(*flips)))
                recvs.append(_remote(ins[a], outs[a].at[me.slot(*flips)], send_sems, recv_sems, (a, k - 1), me.dev(*flips)))
        for cp in local + sends:
            cp.start()
        for cp in recvs:
            cp.wait_recv()
        for cp in sends:
            cp.wait_send()
        for cp in local:
            cp.wait()

    out_shape = [jax.ShapeDtypeStruct(a.shape, a.dtype) for a in sums]
    out_shape += [jax.ShapeDtypeStruct((N_DEV,) + a.shape, a.dtype) for a in gather]
    outs = pl.pallas_call(
        body, in_specs=[ANY_SPEC] * n, out_specs=[ANY_SPEC] * n, out_shape=out_shape,
        scratch_shapes=[pltpu.SemaphoreType.DMA((n, N_DEV - 1)), pltpu.SemaphoreType.DMA((n, N_DEV - 1)),
                        pltpu.SemaphoreType.DMA((n,))],
        compiler_params=pltpu.CompilerParams(has_side_effects=True), name=name)(*sums, *gather)
    return _own_chip(outs[:ns], sums) + list(outs[ns:])


def _chip_swap_copies(ins, outs, send_sems, recv_sems, arrivals=True):
    me = _Mesh()
    here = me.chip(0, 0)
    sends, recvs = [], []
    for a in range(len(ins)):
        for j, (fx, fy) in enumerate(OTHER_CHIPS):
            there = me.chip(fx, fy)
            sends.append(_remote(ins[a].at[there], outs[a].at[here], send_sems, recv_sems, (a, j), me.dev(fx, fy, 0)))
            if arrivals:
                recvs.append(_remote(ins[a].at[there], outs[a].at[there], send_sems, recv_sems, (a, j), me.dev(fx, fy, 0)))
    return sends, recvs


def _own_chip(outs, sums):
    here = 2 * lax.axis_index("x") + lax.axis_index("y")
    return [lax.dynamic_update_slice_in_dim(o, lax.dynamic_slice_in_dim(s, here, 1, axis=0), here, axis=0)
            for o, s in zip(outs, sums)]


def _mm_tn_swap(a, b, name, sums):
    t, m = a.shape
    _, n = b.shape
    tk = _row_tile(t)
    halves = 2 if (m * n * 4 > 8 * 1024 * 1024 and n % 256 == 0) else 1
    tn = n // halves
    steps = t // tk
    ns = len(sums)

    def body(a_ref, b_ref, *rest):
        o_ref = rest[ns]
        j, s = pl.program_id(0), pl.program_id(1)

        def start():
            for cp in _chip_swap_copies(rest[:ns], rest[ns + 1:2 * ns + 1], *rest[2 * ns + 1:], arrivals=False)[0]:
                cp.start()

        def finish():
            sends, recvs = _chip_swap_copies(rest[:ns], rest[ns + 1:2 * ns + 1], *rest[2 * ns + 1:])
            for cp in recvs:
                cp.wait_recv()
            for cp in sends:
                cp.wait_send()

        pl.when((j == 0) & (s == 0))(start)

        @pl.when(s == 0)
        def _():
            o_ref[...] = jnp.zeros_like(o_ref)
        at = a_ref[...].astype(BF16).T
        for c0, w in _chunks(tn, 512):
            o_ref[:, c0:c0 + w] += _dot(at, b_ref[:, c0:c0 + w].astype(BF16))
        pl.when((j == halves - 1) & (s == steps - 1))(finish)

    outs = pl.pallas_call(
        body, grid=(halves, steps),
        in_specs=[pl.BlockSpec((tk, m), lambda j, s: (s, 0)), pl.BlockSpec((tk, tn), lambda j, s: (s, j))] + [ANY_SPEC] * ns,
        out_specs=[pl.BlockSpec((m, tn), lambda j, s: (0, j))] + [ANY_SPEC] * ns,
        out_shape=[jax.ShapeDtypeStruct((m, n), F32)] + [jax.ShapeDtypeStruct(x.shape, x.dtype) for x in sums],
        scratch_shapes=[pltpu.SemaphoreType.DMA((ns, 3)), pltpu.SemaphoreType.DMA((ns, 3))],
        compiler_params=pltpu.CompilerParams(dimension_semantics=("arbitrary", "arbitrary"), vmem_limit_bytes=VMEM_LIMIT,
                                             has_side_effects=True), name=name)(a, b, *sums)
    return outs[0], _own_chip(outs[1:], sums)


def _adam_rows(r):
    for cand in (128, 64, 32, 16, 8):
        if r % cand == 0 and r > cand:
            return cand
    return r


def _adamw(parts, w, m, v, name):
    r, c = w.shape
    tr = _adam_rows(r)
    n_parts = parts.shape[0]

    def body(p_ref, w_ref, m_ref, v_ref, g_ref, d_ref, nm_ref, nv_ref):
        g = p_ref[0].astype(F32)
        for k in range(1, n_parts):
            g = g + p_ref[k].astype(F32)
        m_new = ADAM_B1 * m_ref[...] + (1.0 - ADAM_B1) * g
        v_new = ADAM_B2 * v_ref[...] + (1.0 - ADAM_B2) * (g * g)
        m_hat = m_new / (1.0 - ADAM_B1 ** ADAM_STEP)
        v_hat = v_new / (1.0 - ADAM_B2 ** ADAM_STEP)
        g_ref[...] = g
        d_ref[...] = -ADAM_LR * (m_hat / (jnp.sqrt(v_hat) + ADAM_EPS) + ADAM_WD * w_ref[...])
        nm_ref[...] = m_new
        nv_ref[...] = v_new

    blk = pl.BlockSpec((tr, c), lambda i: (i, 0))
    shp = jax.ShapeDtypeStruct((r, c), F32)
    return pl.pallas_call(
        body, grid=(r // tr,), in_specs=[pl.BlockSpec((n_parts, tr, c), lambda i: (0, i, 0)), blk, blk, blk],
        out_specs=[blk] * 4, out_shape=[shp] * 4, compiler_params=_cparams("parallel"), name=name)(parts, w, m, v)


def _rows128(a):
    return a.reshape(-1, 128)


def _pad_rows(a, rows):
    return jnp.pad(a, ((0, rows - a.shape[0]), (0, 0)))


def _lane16(a):
    return jnp.pad(a.reshape(1, -1), ((0, 0), (0, 128 - a.size)))


SHARD_PACK_ROWS = 544
REPL_PACK_ROWS = 72
LOSS_ROW = 71


def _pack_shard(meta, lru_conv_w, odd_norm, ssd_conv_w, lru_w_a, lru_w_x):
    parts = [meta.reshape(16, 128), lru_conv_w.reshape(4, 128), odd_norm.reshape(1, 128), _rows128(ssd_conv_w.reshape(4, 192)),
             _rows128(lru_w_a.reshape(4, 32, 256)), _rows128(lru_w_x.reshape(4, 32, 256))]
    return _pad_rows(jnp.concatenate(parts, axis=0), SHARD_PACK_ROWS)


def _unpack_shard(p):
    return (p[0:16], p[16:20].reshape(1, 4, 128), p[20:21], p[21:27].reshape(1, 4, 192),
            p[27:283].reshape(1, 4, 32, 256), p[283:539].reshape(1, 4, 32, 256))


def _pack_repl(even_norm, lru_conv_b, lru_b_a, lru_b_x, lru_lambda, ssd_norm, final_norm, ssd_conv_b, dt_bias, a_log, ssd_d):
    parts = [_rows128(v) for v in (even_norm, lru_conv_b, lru_b_a, lru_b_x, lru_lambda, ssd_norm, final_norm, ssd_conv_b)]
    parts += [_lane16(dt_bias), _lane16(a_log), _lane16(ssd_d)]
    return _pad_rows(jnp.concatenate(parts, axis=0), REPL_PACK_ROWS)


def _unpack_repl(p):
    vec = lambda i: p[8 * i:8 * i + 8].reshape(1, 1024)
    return (vec(0), vec(1), vec(2), vec(3), vec(4), vec(5), p[48:56].reshape(1024), p[56:68].reshape(1, 1536),
            p[68:69, :16], p[69:70, :16], p[70:71, :16])


def _local_step(x, tgt, meta, even_norm, w_in_e, lru_conv_w, lru_conv_b, lru_w_a, lru_b_a, lru_w_x, lru_b_x, lru_lambda,
                ssd_conv_w, ssd_conv_b, dt_bias, a_log, ssd_d, ssd_norm, odd_norm, final_norm, even_in, even_in_dw):
    bsz, seq, d = x.shape
    lp = PAD + N_META + seq
    t = bsz * lp
    h0 = jnp.concatenate([jnp.zeros((bsz, PAD, d), F32), jnp.broadcast_to(meta[None], (bsz, N_META, d)), x], axis=1).reshape(t, d)

    u0 = _norm_fwd(h0, even_norm, "norm0_fwd")
    proj, w_out_e, w_in_o, w_out_o = even_in(u0)
    lru = (lru_conv_w, lru_conv_b, lru_w_a, lru_b_a, lru_w_x, lru_b_x, lru_lambda)
    ya = _lru_fwd(proj, *lru, bsz, lp)
    xbc = _ssd_pre_fwd(proj, ssd_conv_w, ssd_conv_b, bsz, lp)
    y = _ssd_fwd(xbc, proj, dt_bias, a_log, ssd_d, bsz, lp)
    mixed = _gnorm_fwd(y, proj, ssd_norm, ya)
    h1 = _mm_nn(mixed, w_out_e, F32, "even_out", res=h0)
    u2 = _norm_fwd(h1, odd_norm, "norm1_fwd")
    qkvg = _mm_nn(u2, w_in_o, F32, "odd_in")
    o, og = _sb_fwd(qkvg, bsz, lp)
    h2 = _mm_nn(og, w_out_o, F32, "odd_out", res=h1)
    loss, dh2, d_final_norm = _final_loss(h2, final_norm, tgt, lp)

    dog = _mm_nn(dh2, w_out_o.T, F32, "odd_out_dx")
    d_w_out_o = _mm_tn(og, dh2, "odd_out_dw")
    dqkvg = jnp.concatenate(_sb_bwd(qkvg, o, dog, bsz, lp), axis=1)
    du2 = _mm_nn(dqkvg, w_in_o.T, F32, "odd_in_dx")
    d_w_in_o = _mm_tn(u2, dqkvg, "odd_in_dw")
    dh1, d_odd_norm = _norm_bwd(h1, odd_norm, du2, dh2, "norm1_bwd")

    dmixed = _mm_nn(dh1, w_out_e.T, F32, "even_out_dx")
    d_w_out_e = _mm_tn(mixed, dh1, "even_out_dw")
    dlx, dgate, d_lru_conv_w, d_lru_conv_b, d_w_a, d_b_a, d_w_x, d_b_x, d_lambda = _lru_bwd(proj, dmixed, *lru, bsz, lp)
    dy, dz, d_ssd_norm = _gnorm_bwd(y, proj, ssd_norm, dmixed)
    dxs, dbm, dcm, ddt, d_dt_bias, d_a_log, d_ssd_d = _ssd_bwd(xbc, proj, dt_bias, a_log, ssd_d, dy, bsz, lp)
    dxbc, d_ssd_conv_w, d_ssd_conv_b = _ssd_pre_bwd(proj, jnp.concatenate([dxs, dbm, dcm], axis=1), ssd_conv_w, ssd_conv_b, bsz, lp)
    ddt = (ddt[:, :DT_L] + ddt[:, DT_L:]).astype(BF16)
    dproj = jnp.concatenate([dlx, dgate, dz, dxbc, ddt, jnp.zeros((t, EVEN_NP - 4608 - DT_L), BF16)], axis=1)
    du0 = _mm_nn(dproj, w_in_e.T, F32, "even_in_dx")
    d_w_in_e, early = even_in_dw(u0, dproj, (d_w_out_e, d_w_in_o, d_w_out_o))
    dh0, d_even_norm = _norm_bwd(h0, even_norm, du0, dh1, "norm0_bwd")
    grad_x = dh0.reshape(bsz, lp, d)[:, PAD + N_META:]
    d_meta = _meta_grad(dh0, bsz, lp)
    heads = lambda p: (p[0:1] + p[8:9])[:, :SSD_HEADS]
    grads = dict(meta=d_meta, even_norm=d_even_norm, even_w_in=d_w_in_e[:, :EVEN_IN], lru_conv_w=d_lru_conv_w,
                 lru_conv_b=d_lru_conv_b, lru_w_a=d_w_a, lru_b_a=d_b_a, lru_w_x=d_w_x, lru_b_x=d_b_x, lru_lambda=d_lambda,
                 ssd_conv_w=d_ssd_conv_w, ssd_conv_b=d_ssd_conv_b, ssd_dt_bias=heads(d_dt_bias), ssd_a_log=heads(d_a_log),
                 ssd_d=heads(d_ssd_d), ssd_norm=d_ssd_norm, even_w_out=d_w_out_e, odd_norm=d_odd_norm, odd_w_in=d_w_in_o,
                 odd_w_out=d_w_out_o, final_norm=d_final_norm)
    return loss[0, 0], grad_x, grads, early


WEIGHTS = ['meta', 'even_norm', 'even_w_in', 'lru_conv_w', 'lru_conv_b', 'lru_w_a', 'lru_b_a', 'lru_w_x', 'lru_b_x', 'lru_lambda',
           'ssd_conv_w', 'ssd_conv_b', 'ssd_dt_bias', 'ssd_a_log', 'ssd_d', 'ssd_norm', 'even_w_out', 'odd_norm', 'odd_w_in',
           'odd_w_out', 'final_norm']


def _blocks_of(a, axis):
    shp = a.shape
    a = a.reshape(shp[:axis] + (N_DEV, shp[axis] // N_DEV) + shp[axis + 1:])
    return jnp.moveaxis(a, axis, 0)


def _unblock(a, axis):
    a = jnp.moveaxis(a, 0, axis)
    shp = a.shape
    return a.reshape(shp[:axis] + (shp[axis] * shp[axis + 1],) + shp[axis + 2:])


def kernel(x, meta, even_norm, even_w_in, lru_conv_w, lru_conv_b, lru_w_a, lru_b_a, lru_w_x, lru_b_x, lru_lambda, ssd_conv_w, ssd_conv_b, ssd_dt_bias, ssd_a_log, ssd_d, ssd_norm, even_w_out, odd_norm, odd_w_in, odd_w_out, final_norm, loss_target, m_meta, m_even_norm, m_even_w_in, m_lru_conv_w, m_lru_conv_b, m_lru_w_a, m_lru_b_a, m_lru_w_x, m_lru_b_x, m_lru_lambda, m_ssd_conv_w, m_ssd_conv_b, m_ssd_dt_bias, m_ssd_a_log, m_ssd_d, m_ssd_norm, m_even_w_out, m_odd_norm, m_odd_w_in, m_odd_w_out, m_final_norm, v_meta, v_even_norm, v_even_w_in, v_lru_conv_w, v_lru_conv_b, v_lru_w_a, v_lru_b_a, v_lru_w_x, v_lru_b_x, v_lru_lambda, v_ssd_conv_w, v_ssd_conv_b, v_ssd_dt_bias, v_ssd_a_log, v_ssd_d, v_ssd_norm, v_even_w_out, v_odd_norm, v_odd_w_in, v_odd_w_out, v_final_norm):
    w = dict(meta=meta, even_norm=even_norm, even_w_in=even_w_in, lru_conv_w=lru_conv_w, lru_conv_b=lru_conv_b, lru_w_a=lru_w_a,
             lru_b_a=lru_b_a, lru_w_x=lru_w_x, lru_b_x=lru_b_x, lru_lambda=lru_lambda, ssd_conv_w=ssd_conv_w,
             ssd_conv_b=ssd_conv_b, ssd_dt_bias=ssd_dt_bias, ssd_a_log=ssd_a_log, ssd_d=ssd_d, ssd_norm=ssd_norm,
             even_w_out=even_w_out, odd_norm=odd_norm, odd_w_in=odd_w_in, odd_w_out=odd_w_out, final_norm=final_norm)
    m = dict(meta=m_meta, even_norm=m_even_norm, even_w_in=m_even_w_in, lru_conv_w=m_lru_conv_w, lru_conv_b=m_lru_conv_b,
             lru_w_a=m_lru_w_a, lru_b_a=m_lru_b_a, lru_w_x=m_lru_w_x, lru_b_x=m_lru_b_x, lru_lambda=m_lru_lambda,
             ssd_conv_w=m_ssd_conv_w, ssd_conv_b=m_ssd_conv_b, ssd_dt_bias=m_ssd_dt_bias, ssd_a_log=m_ssd_a_log, ssd_d=m_ssd_d,
             ssd_norm=m_ssd_norm, even_w_out=m_even_w_out, odd_norm=m_odd_norm, odd_w_in=m_odd_w_in, odd_w_out=m_odd_w_out,
             final_norm=m_final_norm)
    v = dict(meta=v_meta, even_norm=v_even_norm, even_w_in=v_even_w_in, lru_conv_w=v_lru_conv_w, lru_conv_b=v_lru_conv_b,
             lru_w_a=v_lru_w_a, lru_b_a=v_lru_b_a, lru_w_x=v_lru_w_x, lru_b_x=v_lru_b_x, lru_lambda=v_lru_lambda,
             ssd_conv_w=v_ssd_conv_w, ssd_conv_b=v_ssd_conv_b, ssd_dt_bias=v_ssd_dt_bias, ssd_a_log=v_ssd_a_log, ssd_d=v_ssd_d,
             ssd_norm=v_ssd_norm, even_w_out=v_even_w_out, odd_norm=v_odd_norm, odd_w_in=v_odd_w_in, odd_w_out=v_odd_w_out,
             final_norm=v_final_norm)
    shard_names = ('meta', 'lru_conv_w', 'odd_norm', 'ssd_conv_w', 'lru_w_a', 'lru_w_x')
    repl_names = ('even_norm', 'lru_conv_b', 'lru_b_a', 'lru_b_x', 'lru_lambda', 'ssd_norm', 'final_norm', 'ssd_conv_b',
                  'ssd_dt_bias', 'ssd_a_log', 'ssd_d')
    big_names = ('even_w_in', 'even_w_out', 'odd_w_in', 'odd_w_out')

    gates = jnp.concatenate([lru_w_a.reshape(128, 256), lru_w_x.reshape(128, 256)], axis=0).astype(BF16)
    small = _pack_shard(*[w[k] for k in shard_names])
    g_in_e, g_gates, g_small = _gather_all([even_w_in[0].astype(BF16), gates, small], "gather_weights")
    w_in_e = jnp.pad(_unblock(g_in_e, 1), ((0, 0), (0, EVEN_NP - EVEN_IN)))

    def even_in(u0):
        proj, (g_out_e, g_in_o, g_out_o) = _mm_nn_gather(
            u0, w_in_e, F32, "even_in", [even_w_out[0].astype(BF16), odd_w_in[0].astype(BF16), odd_w_out[0].astype(BF16)])
        return proj, g_out_e.reshape(2048, 1024), _unblock(g_in_o, 1), g_out_o.reshape(1024, 1024)

    gates_full = jnp.moveaxis(g_gates.reshape(N_DEV, 2, 4, 32, 256), 0, 2).reshape(2, 4, 256, 256)
    f_meta = _unblock(g_small[:, 0:16], 1)
    f_lru_conv_w = _unblock(g_small[:, 16:20], 1)
    f_odd_norm = _unblock(g_small[:, 20:21], 1)
    f_ssd_conv_w = _unblock(g_small[:, 21:27].reshape(N_DEV, 4, 192), 1)

    my_c = lax.axis_index("c")

    def chip_sums(by_owner, tag):
        got = _swap_in_chip(by_owner, "reduce_in_chip_" + tag)
        kept = [lax.dynamic_index_in_dim(a.reshape((4, 2) + a.shape[1:]), my_c, axis=1, keepdims=False) for a in by_owner]
        return [_add_pair(a, b, "chip_sum_%s%d" % (tag, i)) for i, (a, b) in enumerate(zip(kept, got))]

    def even_in_dw(u0, dproj, done):
        d_out_e, d_in_o, d_out_o = done
        sums = chip_sums([d_out_e.reshape(N_DEV, 256, 1024).astype(BF16), _blocks_of(d_in_o, 1).astype(BF16),
                          d_out_o.reshape(N_DEV, 128, 1024).astype(BF16)], "a")
        return _mm_tn_swap(u0, dproj, "even_in_dw", sums)

    loss, grad_x, g, (p_out_e, p_in_o, p_out_o) = _local_step(
        x, loss_target, f_meta, even_norm, w_in_e, f_lru_conv_w, lru_conv_b, gates_full[0], lru_b_a, gates_full[1], lru_b_x,
        lru_lambda, f_ssd_conv_w, ssd_conv_b, _lane16(ssd_dt_bias), _lane16(ssd_a_log), _lane16(ssd_d), ssd_norm,
        f_odd_norm, final_norm.reshape(1, -1), even_in, even_in_dw)

    s_small = jnp.stack([_pack_shard(g['meta'][:, 128 * p:128 * (p + 1)], g['lru_conv_w'][:, 128 * p:128 * (p + 1)],
                                     g['odd_norm'][:, 128 * p:128 * (p + 1)], g['ssd_conv_w'][:, 192 * p:192 * (p + 1)],
                                     g['lru_w_a'][:, 32 * p:32 * (p + 1)], g['lru_w_x'][:, 32 * p:32 * (p + 1)])
                         for p in range(N_DEV)])
    r_pack = _pack_repl(*[g[k] for k in repl_names]).at[LOSS_ROW, 0].set(loss)
    p_in_e, p_small, p_repl = _swap_chips(chip_sums([_blocks_of(g['even_w_in'], 1).astype(BF16), s_small], "b"), [r_pack],
                                          "reduce_across_chips")

    res = {}
    for k, parts in zip(big_names, (p_in_e, p_out_e, p_in_o, p_out_o)):
        outs = _adamw(parts, w[k][0], m[k][0], v[k][0], "adamw_" + k)
        res[k] = [o[None] for o in outs]
    outs = _adamw(p_small, small, _pack_shard(*[m[k] for k in shard_names]), _pack_shard(*[v[k] for k in shard_names]), "adamw_sharded")
    unpacked = [_unpack_shard(o) for o in outs]
    for i, k in enumerate(shard_names):
        res[k] = [u[i] for u in unpacked]
    outs = _adamw(p_repl, _pack_repl(*[w[k] for k in repl_names]), _pack_repl(*[m[k] for k in repl_names]),
                  _pack_repl(*[v[k] for k in repl_names]), "adamw_replicated")
    unpacked = [_unpack_repl(o) for o in outs]
    for i, k in enumerate(repl_names):
        res[k] = [u[i] for u in unpacked]
    loss = outs[0][LOSS_ROW, 0]
    return (loss, grad_x, *[res[k][0] for k in WEIGHTS], *[res[k][1] for k in WEIGHTS], *[res[k][2] for k in WEIGHTS],
            *[res[k][3] for k in WEIGHTS])
```

```python
import jax
import jax.numpy as jnp
from jax import lax
from jax.experimental import pallas as pl
from jax.experimental.pallas import tpu as pltpu

F32 = jnp.float32
BF16 = jnp.bfloat16

D_MODEL = 1024
N_META = 16
PAD = 112
EPS = 1e-6
CONV_W = 4
LRU_BLOCKS = 4
LRU_BLOCK = 256
RG_LRU_C = 8.0
SSD_HEADS = 16
SSD_P = 64
SSD_N = 128
SSD_Q = 64
SSD_GROUPS = 2
SSD_GW = 512
SSD_CONV_DIM = 1536
SB_HEADS = 16
SB_D = 64
SB_T = 128
EVEN_IN = 4624
EVEN_NP = 4864
N_DEV = 8

ADAM_LR = 0.001
ADAM_B1 = 0.9
ADAM_B2 = 0.999
ADAM_EPS = 1e-08
ADAM_WD = 0.01
ADAM_STEP = 10

VMEM_LIMIT = 56 * 1024 * 1024


def _cparams(*sem):
    return pltpu.CompilerParams(dimension_semantics=sem, vmem_limit_bytes=VMEM_LIMIT)


def _row_tile(t):
    for c in (512, 256, 128):
        if t % c == 0:
            return c
    raise ValueError(t)


def _dot(a, b):
    return jnp.dot(a, b, preferred_element_type=F32)


def _dot_nt(a, b):
    return lax.dot_general(a, b, (((1,), (1,)), ((), ())), preferred_element_type=F32)


def _dot_tn(a, b):
    return lax.dot_general(a, b, (((0,), (0,)), ((), ())), preferred_element_type=F32)


def _split3(a):
    a1 = a.astype(BF16)
    r1 = a - a1.astype(F32)
    a2 = r1.astype(BF16)
    a3 = (r1 - a2.astype(F32)).astype(BF16)
    return a1, a2, a3


def _xdot_r(a, m01):
    a1, a2, a3 = _split3(a)
    return _dot(a1, m01) + _dot(a2, m01) + _dot(a3, m01)


def _xdot_l(m01, a):
    a1, a2, a3 = _split3(a)
    return _dot(m01, a1) + _dot(m01, a2) + _dot(m01, a3)


def _sigmoid(x):
    return 0.5 * jnp.tanh(0.5 * x) + 0.5


def _softplus(x):
    return jnp.maximum(x, 0.0) + jnp.log(1.0 + jnp.exp(-jnp.abs(x)))


def _iota(shape, dim):
    return lax.broadcasted_iota(jnp.int32, shape, dim)


def _block_scan(a_ref, b_ref, o_ref, reverse=False):
    n, w = a_ref.shape
    nb = n // 8
    unroll = 4 if nb % 4 == 0 else 1
    row = _iota((8, w), 0)

    def block(blk, carry):
        rows = pl.ds(pl.multiple_of(blk * 8, 8), 8)
        a, b = a_ref[rows, :], b_ref[rows, :]
        for k in (1, 2, 4):
            keep = (row < 8 - k) if reverse else (row >= k)
            shift = 8 - k if reverse else k
            a_sh = jnp.where(keep, pltpu.roll(a, shift, 0), 1.0)
            b_sh = jnp.where(keep, pltpu.roll(b, shift, 0), 0.0)
            b = a * b_sh + b
            a = a * a_sh
        o = a * carry + b
        o_ref[rows, :] = o
        return o[0:1, :] if reverse else o[7:8, :]

    def trip(i, carry):
        for u in range(unroll):
            j = i * unroll + u
            carry = block(nb - 1 - j if reverse else j, carry)
        return carry

    lax.fori_loop(0, nb // unroll, trip, jnp.zeros((1, w), F32))


def _chunks(n, c):
    return [(s, min(c, n - s)) for s in range(0, n, c)]


def _mm_nn(a, b, out_dtype, name, res=None):
    m, k = a.shape
    _, n = b.shape
    tm = _row_tile(m)

    def body(*refs):
        if res is None:
            a_ref, b_ref, o_ref = refs
        else:
            a_ref, b_ref, r_ref, o_ref = refs
        av = a_ref[...].astype(BF16)
        for s, w in _chunks(n, 512):
            acc = _dot(av, b_ref[:, s:s + w])
            if res is not None:
                acc = acc + r_ref[:, s:s + w]
            o_ref[:, s:s + w] = acc.astype(out_dtype)

    in_specs = [pl.BlockSpec((tm, k), lambda i: (i, 0)), pl.BlockSpec((k, n), lambda i: (0, 0))]
    args = [a, b]
    if res is not None:
        in_specs.append(pl.BlockSpec((tm, n), lambda i: (i, 0)))
        args.append(res)
    return pl.pallas_call(
        body, grid=(m // tm,), in_specs=in_specs,
        out_specs=pl.BlockSpec((tm, n), lambda i: (i, 0)),
        out_shape=jax.ShapeDtypeStruct((m, n), out_dtype),
        compiler_params=_cparams("parallel"), name=name)(*args)


def _mm_tn(a, b, name):
    t, m = a.shape
    _, n = b.shape
    tk = _row_tile(t)
    halves = 2 if (m * n * 4 > 8 * 1024 * 1024 and n % 256 == 0) else 1
    tn = n // halves

    def body(a_ref, b_ref, o_ref):
        @pl.when(pl.program_id(1) == 0)
        def _():
            o_ref[...] = jnp.zeros_like(o_ref)
        at = a_ref[...].astype(BF16).T
        for s, w in _chunks(tn, 512):
            o_ref[:, s:s + w] += _dot(at, b_ref[:, s:s + w].astype(BF16))

    return pl.pallas_call(
        body, grid=(halves, t // tk),
        in_specs=[pl.BlockSpec((tk, m), lambda j, s: (s, 0)), pl.BlockSpec((tk, tn), lambda j, s: (s, j))],
        out_specs=pl.BlockSpec((m, tn), lambda j, s: (0, j)),
        out_shape=jax.ShapeDtypeStruct((m, n), F32),
        compiler_params=_cparams("parallel", "arbitrary"), name=name)(a, b)


def _norm_fwd(h, w, name):
    t, d = h.shape
    tm = _row_tile(t)

    def body(h_ref, w_ref, u_ref):
        x = h_ref[...]
        r = lax.rsqrt(jnp.mean(x * x, axis=-1, keepdims=True) + EPS)
        u_ref[...] = (x * r * w_ref[...]).astype(BF16)

    return pl.pallas_call(
        body, grid=(t // tm,),
        in_specs=[pl.BlockSpec((tm, d), lambda i: (i, 0)), pl.BlockSpec((1, d), lambda i: (0, 0))],
        out_specs=pl.BlockSpec((tm, d), lambda i: (i, 0)),
        out_shape=jax.ShapeDtypeStruct((t, d), BF16),
        compiler_params=_cparams("parallel"), name=name)(h, w)


def _norm_bwd(h, w, du, dres, name):
    t, d = h.shape
    tm = _row_tile(t)

    def body(h_ref, w_ref, du_ref, dr_ref, dh_ref, dw_ref):
        @pl.when(pl.program_id(0) == 0)
        def _():
            dw_ref[...] = jnp.zeros_like(dw_ref)
        x = h_ref[...]
        r = lax.rsqrt(jnp.mean(x * x, axis=-1, keepdims=True) + EPS)
        xh = x * r
        du_ = du_ref[...]
        g = du_ * w_ref[...]
        dh_ref[...] = dr_ref[...] + r * (g - xh * jnp.mean(g * xh, axis=-1, keepdims=True))
        dw_ref[...] += jnp.sum(du_ * xh, axis=0, keepdims=True)

    return pl.pallas_call(
        body, grid=(t // tm,),
        in_specs=[pl.BlockSpec((tm, d), lambda i: (i, 0)), pl.BlockSpec((1, d), lambda i: (0, 0)),
                  pl.BlockSpec((tm, d), lambda i: (i, 0)), pl.BlockSpec((tm, d), lambda i: (i, 0))],
        out_specs=[pl.BlockSpec((tm, d), lambda i: (i, 0)), pl.BlockSpec((1, d), lambda i: (0, 0))],
        out_shape=[jax.ShapeDtypeStruct((t, d), F32), jax.ShapeDtypeStruct((1, d), F32)],
        compiler_params=_cparams("arbitrary"), name=name)(h, w, du, dres)


def _final_loss(h2, w, tgt, lp):
    t, d = h2.shape
    bsz, seq, _ = tgt.shape
    nblk = lp // SB_T

    def body(h_ref, w_ref, t_ref, loss_ref, dh_ref, dw_ref):
        b, i = pl.program_id(0), pl.program_id(1)

        @pl.when((b == 0) & (i == 0))
        def _():
            loss_ref[...] = jnp.zeros_like(loss_ref)
            dw_ref[...] = jnp.zeros_like(dw_ref)

        @pl.when(i == 0)
        def _():
            dh_ref[...] = jnp.zeros_like(dh_ref)

        @pl.when(i > 0)
        def _():
            x = h_ref[...]
            r = lax.rsqrt(jnp.mean(x * x, axis=-1, keepdims=True) + EPS)
            xh = x * r
            wv = w_ref[...]
            diff = xh * wv - t_ref[0]
            loss_ref[...] += 0.5 * jnp.sum(jnp.mean(diff * diff, axis=-1, keepdims=True), axis=0, keepdims=True)
            dy = diff * (1.0 / d)
            g = dy * wv
            dh_ref[...] = r * (g - xh * jnp.mean(g * xh, axis=-1, keepdims=True))
            dw_ref[...] += jnp.sum(dy * xh, axis=0, keepdims=True)

    return pl.pallas_call(
        body, grid=(bsz, nblk),
        in_specs=[pl.BlockSpec((SB_T, d), lambda b, i: (b * nblk + i, 0)), pl.BlockSpec((1, d), lambda b, i: (0, 0)),
                  pl.BlockSpec((1, SB_T, d), lambda b, i: (b, jnp.maximum(i - 1, 0), 0))],
        out_specs=[pl.BlockSpec((1, 128), lambda b, i: (0, 0)), pl.BlockSpec((SB_T, d), lambda b, i: (b * nblk + i, 0)),
                   pl.BlockSpec((1, d), lambda b, i: (0, 0))],
        out_shape=[jax.ShapeDtypeStruct((1, 128), F32), jax.ShapeDtypeStruct((t, d), F32), jax.ShapeDtypeStruct((1, d), F32)],
        compiler_params=_cparams("arbitrary", "arbitrary"), name="final_loss")(h2, w, tgt)


def _shift_down(x, j):
    return x if j == 0 else pltpu.roll(x, j, 0)


def _shift_up(x, j):
    return x if j == 0 else pltpu.roll(x, x.shape[0] - j, 0)


def _conv(x, cw, cb):
    out = cb + cw[CONV_W - 1:CONV_W, :] * x
    for j in range(1, CONV_W):
        out = out + cw[CONV_W - 1 - j:CONV_W - j, :] * _shift_down(x, j)
    return out


def _conv_bwd_x(dy, cw):
    out = cw[CONV_W - 1:CONV_W, :] * dy
    for j in range(1, CONV_W):
        out = out + cw[CONV_W - 1 - j:CONV_W - j, :] * _shift_up(dy, j)
    return out


def _conv_bwd_w(dcw_ref, dy, x):
    for j in range(CONV_W):
        k = CONV_W - 1 - j
        dcw_ref[k:k + 1, :] += jnp.sum(dy * _shift_down(x, j), axis=0, keepdims=True)


def _lru_forward_block(x, cw, cb, wa, ba, wx, bx, lam, a_ref, b_ref, h_ref):
    lp = x.shape[0]
    lx = _conv(x, cw, cb)
    lxb = lx.astype(BF16)
    r = _sigmoid(_dot(lxb, wa) + ba)
    i = _sigmoid(_dot(lxb, wx) + bx)
    spl = _softplus(-lam)
    log_a = (-RG_LRU_C) * r * spl
    a = jnp.exp(log_a)
    mult = jnp.sqrt(-jnp.tanh(log_a) * (1.0 + a * a))
    valid = _iota((lp, 1), 0) >= PAD
    a_ref[...] = a
    b_ref[...] = jnp.where(valid, mult * i * lx, 0.0)
    _block_scan(a_ref, b_ref, h_ref)
    return lx, lxb, r, i, spl, a, mult, valid


def _lru_specs(lp):
    w = LRU_BLOCK
    return [
        pl.BlockSpec((lp, w), lambda g, b: (b, g)),
        pl.BlockSpec((lp, w), lambda g, b: (b, LRU_BLOCKS + g)),
        pl.BlockSpec((CONV_W, w), lambda g, b: (0, g)),
        pl.BlockSpec((1, w), lambda g, b: (0, g)),
        pl.BlockSpec((1, w, w), lambda g, b: (g, 0, 0)),
        pl.BlockSpec((1, w), lambda g, b: (0, g)),
        pl.BlockSpec((1, w, w), lambda g, b: (g, 0, 0)),
        pl.BlockSpec((1, w), lambda g, b: (0, g)),
        pl.BlockSpec((1, w), lambda g, b: (0, g)),
    ]


def _lru_fwd(proj, cw, cb, wa, ba, wx, bx, lam, bsz, lp):
    w = LRU_BLOCK

    def body(x_ref, g_ref, cw_ref, cb_ref, wa_ref, ba_ref, wx_ref, bx_ref, lam_ref, y_ref, a_s, b_s, h_s):
        _lru_forward_block(x_ref[...], cw_ref[...], cb_ref[...], wa_ref[0], ba_ref[...], wx_ref[0], bx_ref[...],
                           lam_ref[...], a_s, b_s, h_s)
        gate = g_ref[...]
        y_ref[...] = (h_s[...] * gate * _sigmoid(gate)).astype(BF16)

    return pl.pallas_call(
        body, grid=(LRU_BLOCKS, bsz), in_specs=_lru_specs(lp),
        out_specs=pl.BlockSpec((lp, w), lambda g, b: (b, g)),
        out_shape=jax.ShapeDtypeStruct((bsz * lp, 2 * LRU_BLOCKS * w), BF16),
        scratch_shapes=[pltpu.VMEM((lp, w), F32)] * 3,
        compiler_params=_cparams("parallel", "arbitrary"), name="lru_fwd")(proj, proj, cw, cb, wa, ba, wx, bx, lam)


def _lru_bwd(proj, dmixed, cw, cb, wa, ba, wx, bx, lam, bsz, lp):
    w = LRU_BLOCK

    def body(x_ref, g_ref, cw_ref, cb_ref, wa_ref, ba_ref, wx_ref, bx_ref, lam_ref, dy_ref,
             dx_ref, dg_ref, dcw_ref, dcb_ref, dwa_ref, dba_ref, dwx_ref, dbx_ref, dlam_ref, a_s, b_s, h_s, dh_s):
        @pl.when(pl.program_id(1) == 0)
        def _():
            for ref in (dcw_ref, dcb_ref, dwa_ref, dba_ref, dwx_ref, dbx_ref, dlam_ref):
                ref[...] = jnp.zeros_like(ref)

        x = x_ref[...]
        cwv = cw_ref[...]
        wav, wxv, lam_ = wa_ref[0], wx_ref[0], lam_ref[...]
        lx, lxb, r, i, spl, a, mult, valid = _lru_forward_block(
            x, cwv, cb_ref[...], wav, ba_ref[...], wxv, bx_ref[...], lam_, a_s, b_s, h_s)
        gate = g_ref[...]
        sg = _sigmoid(gate)
        dy = dy_ref[...]
        h = h_s[...]
        dg_ref[...] = (dy * h * sg * (1.0 + gate * (1.0 - sg))).astype(BF16)
        b_s[...] = dy * gate * sg
        a_s[...] = jnp.where(_iota((lp, 1), 0) < lp - 1, _shift_up(a, 1), 0.0)
        _block_scan(a_s, b_s, dh_s, reverse=True)
        dh = dh_s[...]
        hprev = jnp.where(_iota((lp, 1), 0) >= 1, _shift_down(h, 1), 0.0)
        db = jnp.where(valid, dh, 0.0)
        dmult = db * i * lx
        di = db * mult * lx
        dlx = db * mult * i
        inv_mult = jnp.where(mult > 0.0, 1.0 / mult, 0.0)
        dlog_a = dh * hprev * a - dmult * (a * a) * inv_mult
        drp = dlog_a * ((-RG_LRU_C) * spl) * r * (1.0 - r)
        dip = di * i * (1.0 - i)
        dspl = jnp.sum(dlog_a * ((-RG_LRU_C) * r), axis=0, keepdims=True)
        dlam_ref[...] += dspl * (-_sigmoid(-lam_))
        dba_ref[...] += jnp.sum(drp, axis=0, keepdims=True)
        dbx_ref[...] += jnp.sum(dip, axis=0, keepdims=True)
        drpb, dipb = drp.astype(BF16), dip.astype(BF16)
        dwa_ref[0] += _dot_tn(lxb, drpb)
        dwx_ref[0] += _dot_tn(lxb, dipb)
        dlx = dlx + _dot_nt(drpb, wav) + _dot_nt(dipb, wxv)
        dcb_ref[...] += jnp.sum(dlx, axis=0, keepdims=True)
        _conv_bwd_w(dcw_ref, dlx, x)
        dx_ref[...] = _conv_bwd_x(dlx, cwv).astype(BF16)

    t = bsz * lp
    vec = pl.BlockSpec((1, w), lambda g, b: (0, g))
    mat = pl.BlockSpec((1, w, w), lambda g, b: (g, 0, 0))
    act = pl.BlockSpec((lp, w), lambda g, b: (b, g))
    return pl.pallas_call(
        body, grid=(LRU_BLOCKS, bsz), in_specs=_lru_specs(lp) + [act],
        out_specs=[act, act, pl.BlockSpec((CONV_W, w), lambda g, b: (0, g)), vec, mat, vec, mat, vec, vec],
        out_shape=[jax.ShapeDtypeStruct((t, 1024), BF16), jax.ShapeDtypeStruct((t, 1024), BF16),
                   jax.ShapeDtypeStruct((CONV_W, 1024), F32), jax.ShapeDtypeStruct((1, 1024), F32),
                   jax.ShapeDtypeStruct((LRU_BLOCKS, w, w), F32), jax.ShapeDtypeStruct((1, 1024), F32),
                   jax.ShapeDtypeStruct((LRU_BLOCKS, w, w), F32), jax.ShapeDtypeStruct((1, 1024), F32),
                   jax.ShapeDtypeStruct((1, 1024), F32)],
        scratch_shapes=[pltpu.VMEM((lp, w), F32)] * 4,
        compiler_params=_cparams("parallel", "arbitrary"), name="lru_bwd")(
            proj, proj, cw, cb, wa, ba, wx, bx, lam, dmixed)


def _sb_masks():
    j = _iota((SB_T, 2 * SB_T), 0)
    s = _iota((SB_T, 2 * SB_T), 1)
    right = ((s < SB_T) & (j > s)) | (s >= SB_T)
    left = ((s < SB_T) & (j < s)) | (s >= SB_T)
    return jnp.where(right, -1.0, 0.0).astype(BF16), left.astype(BF16)


def _xdot2(a, m01):
    a1 = a.astype(BF16)
    a2 = (a - a1.astype(F32)).astype(BF16)
    return _dot(a1, m01) + _dot(a2, m01)


def _sb_query_block(lp):
    for tq in (512, 256, 128):
        if (lp - SB_T) % tq == 0:
            return tq, (lp - SB_T) // tq
    raise ValueError(lp)


def _sb_key_rows(kj):
    return pl.ds(kj * SB_T if isinstance(kj, int) else pl.multiple_of(kj * SB_T, SB_T), SB_T)


def _sb_valid(q0, tq, kj):
    t_pos = q0 + _iota((tq, SB_T), 0)
    s_pos = kj * SB_T + _iota((tq, SB_T), 1)
    return (s_pos < t_pos) & (s_pos >= PAD)


def _sb_by_head(ref, rows):
    t = ref[rows, :]
    head0 = _iota((SB_T, 128), 1) < SB_D
    return jnp.concatenate([jnp.where(head0, t, 0.0), jnp.where(head0, 0.0, t)], axis=0).astype(BF16)


def _sb_tile(qb, k_ref, q0, kj, carries, m_right_neg, masked):
    rows_k = _sb_key_rows(kj)
    z2 = _dot_nt(qb, _sb_by_head(k_ref, rows_k))
    valid = _sb_valid(q0, qb.shape[0], kj) if masked else None
    out = []
    for h in range(2):
        z = z2[:, h * SB_T:(h + 1) * SB_T]
        sp = _softplus(z)
        rs = _dot((jnp.where(valid, sp, 0.0) if masked else sp).astype(BF16), m_right_neg)
        lb = z - sp
        wgt = jnp.exp(lb + rs[:, :SB_T] + carries[h])
        if masked:
            wgt = jnp.where(valid, wgt, 0.0)
        out.append((lb, wgt, carries[h] + rs[:, SB_T:]))
    return rows_k, valid, out


def _tail(x, row0):
    return x if row0 == 0 else x[row0:]


def _merge(old, new_tail, row0):
    return new_tail if row0 == 0 else jnp.concatenate([old[:row0], new_tail], axis=0)


def _sb_sweep(step, c, blk, tq, first, leftwards):
    if first:
        return step(0, True, 0)(c)
    r = tq // SB_T
    u = 4 if r % 4 == 0 else 1
    lo = 1 + r * blk

    def diag(c):
        for d in (range(r - 1, -1, -1) if leftwards else range(r)):
            c = step(lo + d, True, SB_T * d)(c)
        return c

    def inner(c):
        def trip(i, c):
            for j in range(u):
                d = u * i + j
                c = step(lo - 1 - d if leftwards else 1 + d, False, 0)(c)
            return c
        return lax.fori_loop(0, (r // u) * blk, trip, c)

    if leftwards:
        return step(0, True, 0)(inner(diag(c)))
    return diag(inner(step(0, True, 0)(c)))


def _sb_specs(lp):
    nh = SB_HEADS // 2
    return [pl.BlockSpec((lp, 128), lambda b, p: (b, p)), pl.BlockSpec((lp, 128), lambda b, p: (b, nh + p)),
            pl.BlockSpec((lp, 128), lambda b, p: (b, 2 * nh + p)), pl.BlockSpec((lp, 128), lambda b, p: (b, 3 * nh + p))]


def _sb_fwd(qkvg, bsz, lp):
    tq, nb = _sb_query_block(lp)
    scale = SB_D ** -0.5

    def body(q_ref, k_ref, v_ref, g_ref, o_ref, og_ref):
        m_right_neg, _ = _sb_masks()

        def q_block(blk, rows, first):
            q0 = 0 if first else (1 + (rows // SB_T) * blk) * SB_T
            rows_q = pl.ds(q0 if first else pl.multiple_of(q0, SB_T), rows)
            qb = (q_ref[rows_q, :] * scale).astype(BF16)

            def step(kj, masked, row0):
                def run(c):
                    acc, car0, car1 = c
                    rows_k, _, ((_, w0, new0), (_, w1, new1)) = _sb_tile(
                        _tail(qb, row0), k_ref, q0 + row0, kj, (_tail(car0, row0), _tail(car1, row0)), m_right_neg, masked)
                    w2 = jnp.concatenate([w0.astype(BF16), w1.astype(BF16)], axis=1)
                    upd = _tail(acc, row0) + _dot(w2, _sb_by_head(v_ref, rows_k))
                    return _merge(acc, upd, row0), _merge(car0, new0, row0), _merge(car1, new1, row0)
                return run

            zero = jnp.zeros((rows, SB_T), F32)
            o_ref[rows_q, :] = _sb_sweep(step, (zero, zero, zero), blk, rows, first, True)[0]

        q_block(0, SB_T, True)

        def big_block(i, _):
            q_block(i, tq, False)
            return 0

        lax.fori_loop(0, nb, big_block, 0)
        gate = g_ref[...]
        og_ref[...] = (o_ref[...] * gate * _sigmoid(gate)).astype(BF16)

    t = bsz * lp
    blk = pl.BlockSpec((lp, 128), lambda b, p: (b, p))
    return pl.pallas_call(
        body, grid=(bsz, SB_HEADS // 2), in_specs=_sb_specs(lp), out_specs=[blk, blk],
        out_shape=[jax.ShapeDtypeStruct((t, 1024), F32), jax.ShapeDtypeStruct((t, 1024), BF16)],
        compiler_params=_cparams("parallel", "parallel"), name="sb_fwd")(qkvg, qkvg, qkvg, qkvg)


def _sb_bwd(qkvg, o, dog, bsz, lp):
    tq, nb = _sb_query_block(lp)
    nk = lp // SB_T
    scale = SB_D ** -0.5

    def body(q_ref, k_ref, v_ref, g_ref, o_ref, dog_ref, dq_ref, dk_ref, dv_ref, dg_ref, do_s, dk_s, dv_s, e_s, sig_s):
        m_right_neg, m_left = _sb_masks()
        gate = g_ref[...]
        sg = _sigmoid(gate)
        dog = dog_ref[...]
        do_s[...] = dog * gate * sg
        dg_ref[...] = (dog * o_ref[...] * sg * (1.0 + gate * (1.0 - sg))).astype(BF16)
        dk_s[...] = jnp.zeros_like(dk_s)
        dv_s[...] = jnp.zeros_like(dv_s)
        head0_rows = _iota((128, SB_T), 0) < SB_D

        def add_transposed(acc_ref, kj, lhs_t, both):
            res = _dot(lhs_t, both)
            acc_ref[kj] += jnp.where(head0_rows, res[:, :SB_T], res[:, SB_T:])

        def q_block(blk, rows, first):
            q0 = 0 if first else (1 + (rows // SB_T) * blk) * SB_T
            rows_q = pl.ds(q0 if first else pl.multiple_of(q0, SB_T), rows)
            qf = q_ref[rows_q, :] * scale
            qb = qf.astype(BF16)
            q_t = qf.T.astype(BF16)
            do_f = do_s[rows_q, :]
            do_b = do_f.astype(BF16)
            do_t = do_f.T.astype(BF16)

            def left(kj, masked, row0):
                def run(c):
                    rows_k, _, heads = _sb_tile(_tail(qb, row0), k_ref, q0 + row0, kj, (_tail(c[0], row0), _tail(c[1], row0)),
                                                m_right_neg, masked)
                    dw2 = _dot_nt(_tail(do_b, row0), _sb_by_head(v_ref, rows_k))
                    for h, (lb, wgt, _) in enumerate(heads):
                        e_s[2 * kj + h, row0:rows, :] = wgt * dw2[:, h * SB_T:(h + 1) * SB_T]
                        sig_s[2 * kj + h, row0:rows, :] = jnp.exp(lb)
                    add_transposed(dv_s, kj, do_t[:, row0:], jnp.concatenate([hd[1].astype(BF16) for hd in heads], axis=1))
                    return _merge(c[0], heads[0][2], row0), _merge(c[1], heads[1][2], row0)
                return run

            zero = jnp.zeros((rows, SB_T), F32)
            _sb_sweep(left, (zero, zero), blk, rows, first, True)

            def right(kj, masked, row0):
                def run(c):
                    dq, ecar = c[0], list(c[1:])
                    rows_k = _sb_key_rows(kj)
                    valid = _sb_valid(q0 + row0, rows - row0, kj) if masked else None
                    dzs = []
                    for h in range(2):
                        e = e_s[2 * kj + h, row0:rows, :]
                        se = _dot(e.astype(BF16), m_left)
                        ec = _tail(ecar[h], row0)
                        dz = e - sig_s[2 * kj + h, row0:rows, :] * (e + se[:, :SB_T] + ec)
                        if masked:
                            dz = jnp.where(valid, dz, 0.0)
                        dzs.append(dz.astype(BF16))
                        ecar[h] = _merge(ecar[h], ec + se[:, SB_T:], row0)
                    dz2 = jnp.concatenate(dzs, axis=1)
                    add_transposed(dk_s, kj, q_t[:, row0:], dz2)
                    upd = _tail(dq, row0) + _dot(dz2, _sb_by_head(k_ref, rows_k))
                    return _merge(dq, upd, row0), ecar[0], ecar[1]
                return run

            dq = _sb_sweep(right, (zero, zero, zero), blk, rows, first, False)[0]
            dq_ref[rows_q, :] = (dq * scale).astype(BF16)

        q_block(0, SB_T, True)

        def big_block(i, _):
            q_block(i, tq, False)
            return 0

        lax.fori_loop(0, nb, big_block, 0)
        for kj in range(nk):
            dk_ref[kj * SB_T:(kj + 1) * SB_T, :] = dk_s[kj].T.astype(BF16)
            dv_ref[kj * SB_T:(kj + 1) * SB_T, :] = dv_s[kj].T.astype(BF16)

    t = bsz * lp
    blk = pl.BlockSpec((lp, 128), lambda b, p: (b, p))
    shp = jax.ShapeDtypeStruct((t, 1024), BF16)
    return pl.pallas_call(
        body, grid=(bsz, SB_HEADS // 2), in_specs=_sb_specs(lp) + [blk, blk], out_specs=[blk] * 4,
        out_shape=[shp] * 4,
        scratch_shapes=[pltpu.VMEM((lp, 128), F32)] + [pltpu.VMEM((nk, 128, SB_T), F32)] * 2
        + [pltpu.VMEM((2 * nk, tq, SB_T), F32)] * 2,
        compiler_params=_cparams("parallel", "parallel"), name="sb_bwd")(qkvg, qkvg, qkvg, qkvg, o, dog)


XBC_COL0 = 3072 // 256
DT_COL0 = 4608 // 128
DT_L = 128


def _ssd_pre_fwd(proj, cw, cb, bsz, lp):
    def body(x_ref, cw_ref, cb_ref, o_ref):
        pre = _conv(x_ref[...], cw_ref[...], cb_ref[...])
        o_ref[...] = pre * _sigmoid(pre)

    return pl.pallas_call(
        body, grid=(bsz, SSD_CONV_DIM // 256),
        in_specs=[pl.BlockSpec((lp, 256), lambda b, j: (b, XBC_COL0 + j)), pl.BlockSpec((CONV_W, 256), lambda b, j: (0, j)),
                  pl.BlockSpec((1, 256), lambda b, j: (0, j))],
        out_specs=pl.BlockSpec((lp, 256), lambda b, j: (b, j)),
        out_shape=jax.ShapeDtypeStruct((bsz * lp, SSD_CONV_DIM), F32),
        compiler_params=_cparams("parallel", "parallel"), name="ssd_pre_fwd")(proj, cw, cb)


def _ssd_pre_bwd(proj, dxs, dbm, dcm, cw, cb, bsz, lp):
    nx = 1024 // 256

    def body(x_ref, dxs_ref, db_ref, dc_ref, cw_ref, cb_ref, dx_ref, dcw_ref, dcb_ref):
        @pl.when(pl.program_id(1) == 0)
        def _():
            dcw_ref[...] = jnp.zeros_like(dcw_ref)
            dcb_ref[...] = jnp.zeros_like(dcb_ref)
        j = pl.program_id(0)
        x = x_ref[...]
        cwv = cw_ref[...]
        pre = _conv(x, cwv, cb_ref[...])
        s = _sigmoid(pre)
        d = jnp.where(j < nx, dxs_ref[...], jnp.where(j == nx, db_ref[...], dc_ref[...]))
        dpre = d.astype(F32) * s * (1.0 + pre * (1.0 - s))
        dcb_ref[...] += jnp.sum(dpre, axis=0, keepdims=True)
        _conv_bwd_w(dcw_ref, dpre, x)
        dx_ref[...] = _conv_bwd_x(dpre, cwv).astype(BF16)

    return pl.pallas_call(
        body, grid=(SSD_CONV_DIM // 256, bsz),
        in_specs=[pl.BlockSpec((lp, 256), lambda j, b: (b, XBC_COL0 + j)),
                  pl.BlockSpec((lp, 256), lambda j, b: (b, jnp.minimum(j, nx - 1))),
                  pl.BlockSpec((lp, 256), lambda j, b: (b, 0)), pl.BlockSpec((lp, 256), lambda j, b: (b, 0)),
                  pl.BlockSpec((CONV_W, 256), lambda j, b: (0, j)), pl.BlockSpec((1, 256), lambda j, b: (0, j))],
        out_specs=[pl.BlockSpec((lp, 256), lambda j, b: (b, j)), pl.BlockSpec((CONV_W, 256), lambda j, b: (0, j)),
                   pl.BlockSpec((1, 256), lambda j, b: (0, j))],
        out_shape=[jax.ShapeDtypeStruct((bsz * lp, SSD_CONV_DIM), BF16), jax.ShapeDtypeStruct((CONV_W, SSD_CONV_DIM), F32),
                   jax.ShapeDtypeStruct((1, SSD_CONV_DIM), F32)],
        compiler_params=_cparams("parallel", "arbitrary"), name="ssd_pre_bwd")(proj, dxs, dbm, dcm, cw, cb)


def _split2(a):
    a1 = a.astype(BF16)
    return a1, (a - a1.astype(F32)).astype(BF16)


def _xdot2_nt(a, m01):
    a1, a2 = _split2(a)
    return _dot_nt(a1, m01) + _dot_nt(a2, m01)


def _xdot2_l(m01, a):
    a1, a2 = _split2(a)
    return _dot(m01, a1) + _dot(m01, a2)


class _SsdConsts:
    def __init__(self, g):
        q, gw = SSD_Q, SSD_GW
        head_of_lane = lax.shift_right_logical(_iota((DT_L, gw), 1), 6)
        self.sel = (_iota((DT_L, gw), 0) == 8 * g + head_of_lane).astype(BF16)
        r = _iota((q, gw), 0)
        c = jnp.bitwise_and(_iota((q, gw), 1), q - 1)
        self.diag = r == c
        self.diag_b = self.diag.astype(BF16)
        self.lower = c <= r
        self.upper = c >= r
        self.bd = lax.shift_right_logical(_iota((128, 128), 0), 6) == lax.shift_right_logical(_iota((128, 128), 1), 6)
        jj, ll = _iota((q, q), 1), _iota((q, q), 0)
        self.tri = (jj <= ll).astype(BF16)
        self.tri_t = (jj >= ll).astype(BF16)
        self.last = _iota((q, 1), 0) == q - 1


def _ssd_prepass(k, dt_ref, bias_ref, alog_ref, dtm_s, abc_s, lp):
    valid = _iota((lp, 1), 0) >= PAD
    sp_in = dt_ref[...] + bias_ref[...]
    dtm = jnp.where(valid, _softplus(sp_in), 0.0)
    aneg = -jnp.exp(alog_ref[...])
    dtm_s[...] = dtm
    abc_s[...] = _xdot_r(dtm * aneg, k.sel)
    return valid, sp_in, aneg


def _ssd_loop(nc, chunk, init):
    u = 2 if nc % 2 == 0 else 1

    def trip(i, carry):
        for j in range(u):
            carry = chunk(i * u + j, carry)
        return carry

    return lax.fori_loop(0, nc // u, trip, init)


def _ssd_chunk(k, rows, xs_ref, b_ref, c_ref, dtm_s, abc_s):
    bc = _xdot_l(k.tri, abc_s[rows, :])
    tt = jnp.sum(jnp.where(k.diag, bc, 0.0), axis=0, keepdims=True)
    dtbc = _xdot2(dtm_s[rows, :], k.sel)
    xs = xs_ref[rows, :]
    x = xs * dtbc
    bb = b_ref[rows, :].astype(BF16)
    cc = c_ref[rows, :].astype(BF16)
    tot = bc[SSD_Q - 1:SSD_Q, :]
    return bc, tt, dtbc, xs, x, bb, cc, tot, _pair_blocks(k, x)


def _pair_blocks(k, x):
    out = []
    for p in range(SSD_GW // 128):
        xp = x[:, 128 * p:128 * (p + 1)]
        out.append(jnp.where(k.bd, jnp.concatenate([xp, xp], axis=0), 0.0).astype(BF16))
    return out


def _pair_dot(a, blocks, transposed=False):
    dot = _dot_nt if transposed else _dot
    return jnp.concatenate([dot(a[:, 128 * p:128 * (p + 1)], blk) for p, blk in enumerate(blocks)], axis=1)


def _ssd_specs(lp, order):
    ix = (lambda f: (lambda b, g: f(b, g))) if order == "bg" else (lambda f: (lambda g, b: f(b, g)))
    return [pl.BlockSpec((lp, SSD_GW), ix(lambda b, g: (b, g))),
            pl.BlockSpec((lp, SSD_N), ix(lambda b, g: (b, 1024 // SSD_N + g))),
            pl.BlockSpec((lp, SSD_N), ix(lambda b, g: (b, 1280 // SSD_N + g))),
            pl.BlockSpec((lp, DT_L), ix(lambda b, g: (b, DT_COL0))),
            pl.BlockSpec((1, DT_L), ix(lambda b, g: (0, 0))),
            pl.BlockSpec((1, DT_L), ix(lambda b, g: (0, 0))),
            pl.BlockSpec((1, DT_L), ix(lambda b, g: (0, 0)))]


def _ssd_fwd(xbc, proj, dt_bias, a_log, dskip, bsz, lp, shards=()):
    nc = lp // SSD_Q
    ns = len(shards)
    steps = bsz * SSD_GROUPS

    def body(xs_ref, b_ref, c_ref, dt_ref, bias_ref, alog_ref, dsk_ref, *rest):
        y_ref = rest[ns]
        dtm_s, abc_s = rest[2 * ns + 1:2 * ns + 3]
        if ns:
            start, forward, finish = _gather_phases(rest[:ns], rest[ns + 1:2 * ns + 1], *rest[2 * ns + 3:])
            step = pl.program_id(0) * SSD_GROUPS + pl.program_id(1)
            pl.when(step == 0)(start)
            pl.when(step == steps // 2)(forward)
        k = _SsdConsts(pl.program_id(1))
        _ssd_prepass(k, dt_ref, bias_ref, alog_ref, dtm_s, abc_s, lp)
        dvec = _xdot_r(jnp.broadcast_to(dsk_ref[...], (8, DT_L)), k.sel)[0:1, :]

        def chunk(c, state):
            rows = pl.ds(pl.multiple_of(c * SSD_Q, SSD_Q), SSD_Q)
            bc, tt, _, xs, x, bb, cc, tot, xbd = _ssd_chunk(k, rows, xs_ref, b_ref, c_ref, dtm_s, abc_s)
            lm = jnp.where(k.lower, jnp.exp(jnp.minimum(bc - tt, 0.0)), 0.0)
            g_all = _dot(_dot_nt(cc, bb).astype(BF16), k.diag_b) * lm
            y = _pair_dot(g_all.astype(BF16), xbd) + jnp.exp(bc) * _dot(cc, state.astype(BF16)) + dvec * xs
            y_ref[rows, :] = y
            return jnp.exp(tot) * state + _dot_tn(bb, (jnp.exp(tot - bc) * x).astype(BF16))

        _ssd_loop(nc, chunk, jnp.zeros((SSD_N, SSD_GW), F32))
        if ns:
            pl.when(step == steps - 1)(finish)

    sems = [pltpu.SemaphoreType.DMA((ns, 7)), pltpu.SemaphoreType.DMA((ns, 7))] if ns else []
    outs = pl.pallas_call(
        body, grid=(bsz, SSD_GROUPS), in_specs=_ssd_specs(lp, "bg") + [ANY_SPEC] * ns,
        out_specs=[pl.BlockSpec((lp, SSD_GW), lambda b, g: (b, g))] + [ANY_SPEC] * ns,
        out_shape=[jax.ShapeDtypeStruct((bsz * lp, 1024), F32)] + [jax.ShapeDtypeStruct((N_DEV,) + s.shape, s.dtype) for s in shards],
        scratch_shapes=[pltpu.VMEM((lp, DT_L), F32), pltpu.VMEM((lp, SSD_GW), F32)] + sems,
        compiler_params=pltpu.CompilerParams(dimension_semantics=("arbitrary", "arbitrary"), vmem_limit_bytes=VMEM_LIMIT,
                                             has_side_effects=bool(ns)), name="ssd_fwd")(
            xbc, xbc, xbc, proj, dt_bias, a_log, dskip, *shards)
    return outs[0], (_own_blocks(outs[1:], shards) if ns else [])


def _ssd_bwd(xbc, proj, dt_bias, a_log, dskip, dy, bsz, lp):
    nc = lp // SSD_Q

    def body(xs_ref, b_ref, c_ref, dt_ref, bias_ref, alog_ref, dsk_ref, dy_ref,
             dxs_ref, db_ref, dc_ref, ddt_ref, dbias_ref, dalog_ref, ddsk_ref, dtm_s, abc_s, st_s):
        @pl.when(pl.program_id(1) == 0)
        def _():
            for ref in (dbias_ref, dalog_ref, ddsk_ref):
                ref[...] = jnp.zeros_like(ref)

        k = _SsdConsts(pl.program_id(0))
        valid, sp_in, aneg = _ssd_prepass(k, dt_ref, bias_ref, alog_ref, dtm_s, abc_s, lp)
        dvec = _xdot_r(jnp.broadcast_to(dsk_ref[...], (8, DT_L)), k.sel)[0:1, :]

        def fwd_chunk(c, state):
            rows = pl.ds(pl.multiple_of(c * SSD_Q, SSD_Q), SSD_Q)
            st_s[c] = state.astype(BF16)
            bc, _, _, _, x, bb, _, tot, _ = _ssd_chunk(k, rows, xs_ref, b_ref, c_ref, dtm_s, abc_s)
            return jnp.exp(tot) * state + _dot_tn(bb, (jnp.exp(tot - bc) * x).astype(BF16))

        _ssd_loop(nc, fwd_chunk, jnp.zeros((SSD_N, SSD_GW), F32))

        def bwd_chunk(i, carry):
            dstate, daneg, ddsk = carry
            c = nc - 1 - i
            rows = pl.ds(pl.multiple_of(c * SSD_Q, SSD_Q), SSD_Q)
            bc, tt, dtbc, xs, x, bb, cc, tot, xbd = _ssd_chunk(k, rows, xs_ref, b_ref, c_ref, dtm_s, abc_s)
            sprev = st_s[c]
            dyc = dy_ref[rows, :]
            dyb = dyc.astype(BF16)
            seg = bc - tt
            lm = jnp.where(k.lower, jnp.exp(jnp.minimum(seg, 0.0)), 0.0)
            lm_t = jnp.where(k.upper, jnp.exp(jnp.minimum(-seg, 0.0)), 0.0)
            cb_all = _dot(_dot_nt(cc, bb).astype(BF16), k.diag_b)
            cbt_all = _dot(_dot_nt(bb, cc).astype(BF16), k.diag_b)
            g_all = cb_all * lm
            dg = _pair_dot(dyb, xbd, transposed=True)
            dx = _pair_dot((cbt_all * lm_t).astype(BF16), _pair_blocks(k, dyc))
            hh = dg * g_all
            ea = jnp.exp(bc)
            yo = ea * _dot(cc, sprev)
            col_h = jnp.sum(hh, axis=0, keepdims=True)
            dcb = _dot_nt((dg * lm).astype(BF16), k.diag_b).astype(BF16)
            dcs = (ea * dyc).astype(BF16)
            dstb = dstate.astype(BF16)
            dec = jnp.exp(tot - bc)
            w = dec * x
            dw = _dot(bb, dstb)
            dc_ref[rows, :] = (_dot(dcb, bb) + _dot_nt(dcs, sprev)).astype(BF16)
            db_ref[rows, :] = (_dot_tn(dcb, cc) + _dot_nt(w.astype(BF16), dstb)).astype(BF16)
            dx = dx + dec * dw
            etot = jnp.exp(tot)
            r8 = _iota((8, SSD_GW), 0)
            st_row = jnp.sum(dstate * sprev.astype(F32), axis=0, keepdims=True) * etot
            sk_row = jnp.sum(dyc * xs, axis=0, keepdims=True)
            small = jnp.where(r8 == 0, st_row, jnp.where(r8 == 1, sk_row, 0.0))
            q = SSD_Q
            sums = _xdot2_nt(jnp.concatenate([hh + dyc * yo - jnp.where(k.diag, col_h, 0.0) - dw * w, dw * w, dx * xs, small],
                                             axis=0), k.sel)
            kk = sums[q:2 * q]
            dtot = sums[3 * q:3 * q + 1] + jnp.sum(kk, axis=0, keepdims=True)
            dacum = sums[0:q] + jnp.where(k.last, dtot, 0.0)
            da = _xdot2_l(k.tri_t, dacum)
            dtm_c = dtm_s[rows, :]
            ddtm = da * aneg + sums[2 * q:3 * q]
            vrow = (c * SSD_Q + _iota((SSD_Q, 1), 0)) >= PAD
            ddt_ref[rows, :] = jnp.where(vrow, ddtm * _sigmoid(dt_ref[rows, :] + bias_ref[...]), 0.0)
            dxs_ref[rows, :] = (dx * dtbc + dvec * dyc).astype(BF16)
            daneg = daneg + jnp.sum(da * dtm_c, axis=0, keepdims=True)
            ddsk = ddsk + sums[3 * q + 1:3 * q + 2]
            dstate = etot * dstate + _dot_tn(cc, dcs)
            return dstate, daneg, ddsk

        zrow = jnp.zeros((1, DT_L), F32)
        _, daneg, ddsk = _ssd_loop(nc, bwd_chunk, (jnp.zeros((SSD_N, SSD_GW), F32), zrow, zrow))
        dbias_ref[...] += jnp.broadcast_to(jnp.sum(ddt_ref[...], axis=0, keepdims=True), (8, DT_L))
        dalog_ref[...] += jnp.broadcast_to(daneg * aneg, (8, DT_L))
        ddsk_ref[...] += jnp.broadcast_to(ddsk, (8, DT_L))

    t = bsz * lp
    par = pl.BlockSpec((8, DT_L), lambda g, b: (g, 0))
    par_shape = jax.ShapeDtypeStruct((8 * SSD_GROUPS, DT_L), F32)
    return pl.pallas_call(
        body, grid=(SSD_GROUPS, bsz),
        in_specs=_ssd_specs(lp, "gb") + [pl.BlockSpec((lp, SSD_GW), lambda g, b: (b, g))],
        out_specs=[pl.BlockSpec((lp, SSD_GW), lambda g, b: (b, g)), pl.BlockSpec((lp, SSD_N), lambda g, b: (b, g)),
                   pl.BlockSpec((lp, SSD_N), lambda g, b: (b, g)), pl.BlockSpec((lp, DT_L), lambda g, b: (b, g)), par, par, par],
        out_shape=[jax.ShapeDtypeStruct((t, 1024), BF16), jax.ShapeDtypeStruct((t, 256), BF16),
                   jax.ShapeDtypeStruct((t, 256), BF16), jax.ShapeDtypeStruct((t, SSD_GROUPS * DT_L), F32),
                   par_shape, par_shape, par_shape],
        scratch_shapes=[pltpu.VMEM((lp, DT_L), F32), pltpu.VMEM((lp, SSD_GW), F32), pltpu.VMEM((nc, SSD_N, SSD_GW), BF16)],
        compiler_params=_cparams("parallel", "arbitrary"), name="ssd_bwd")(
            xbc, xbc, xbc, proj, dt_bias, a_log, dskip, dy)


Z_COL0 = 2048 // SSD_GW


def _gnorm_fwd(y, proj, w, mixed, name="gnorm_fwd"):
    t = y.shape[0]
    tm = _row_tile(t)

    def body(y_ref, z_ref, w_ref, mixed_ref, o_ref):
        z = z_ref[...]
        gt = y_ref[...] * z * _sigmoid(z)
        r = lax.rsqrt(jnp.mean(gt * gt, axis=-1, keepdims=True) + EPS)
        o_ref[...] = (gt * r * w_ref[...]).astype(BF16)

    return pl.pallas_call(
        body, grid=(t // tm, SSD_GROUPS),
        in_specs=[pl.BlockSpec((tm, SSD_GW), lambda i, g: (i, g)), pl.BlockSpec((tm, SSD_GW), lambda i, g: (i, Z_COL0 + g)),
                  pl.BlockSpec((1, SSD_GW), lambda i, g: (0, g)), ANY_SPEC],
        out_specs=pl.BlockSpec((tm, SSD_GW), lambda i, g: (i, 1024 // SSD_GW + g)),
        out_shape=jax.ShapeDtypeStruct((t, 2048), BF16), input_output_aliases={3: 0},
        compiler_params=_cparams("parallel", "parallel"), name=name)(y, proj, w, mixed)


def _gnorm_bwd(y, proj, w, dmixed):
    t = y.shape[0]
    tm = _row_tile(t)

    def body(y_ref, z_ref, w_ref, d_ref, dy_ref, dz_ref, dw_ref):
        @pl.when(pl.program_id(1) == 0)
        def _():
            dw_ref[...] = jnp.zeros_like(dw_ref)
        z, yv, d = z_ref[...], y_ref[...], d_ref[...]
        s = _sigmoid(z)
        sz = z * s
        gt = yv * sz
        r = lax.rsqrt(jnp.mean(gt * gt, axis=-1, keepdims=True) + EPS)
        gh = gt * r
        dgn = d * w_ref[...]
        dgt = r * (dgn - gh * jnp.mean(dgn * gh, axis=-1, keepdims=True))
        dw_ref[...] += jnp.sum(d * gh, axis=0, keepdims=True)
        dy_ref[...] = dgt * sz
        dz_ref[...] = (dgt * yv * s * (1.0 + z * (1.0 - s))).astype(BF16)

    blk = pl.BlockSpec((tm, SSD_GW), lambda g, i: (i, g))
    return pl.pallas_call(
        body, grid=(SSD_GROUPS, t // tm),
        in_specs=[blk, pl.BlockSpec((tm, SSD_GW), lambda g, i: (i, Z_COL0 + g)), pl.BlockSpec((1, SSD_GW), lambda g, i: (0, g)),
                  pl.BlockSpec((tm, SSD_GW), lambda g, i: (i, 1024 // SSD_GW + g))],
        out_specs=[blk, blk, pl.BlockSpec((1, SSD_GW), lambda g, i: (0, g))],
        out_shape=[jax.ShapeDtypeStruct((t, 1024), F32), jax.ShapeDtypeStruct((t, 1024), BF16),
                   jax.ShapeDtypeStruct((1, 1024), F32)],
        compiler_params=_cparams("parallel", "arbitrary"), name="gnorm_bwd")(y, proj, w, dmixed)


def _meta_grad(dh0, bsz, lp):
    def body(d_ref, o_ref):
        @pl.when(pl.program_id(0) == 0)
        def _():
            o_ref[...] = jnp.zeros_like(o_ref)
        o_ref[...] += d_ref[...]

    return pl.pallas_call(
        body, grid=(bsz,),
        in_specs=[pl.BlockSpec((N_META, D_MODEL), lambda b: (b * (lp // N_META) + PAD // N_META, 0))],
        out_specs=pl.BlockSpec((N_META, D_MODEL), lambda b: (0, 0)),
        out_shape=jax.ShapeDtypeStruct((N_META, D_MODEL), F32),
        compiler_params=_cparams("arbitrary"), name="meta_grad")(dh0)


OTHER_CHIPS = ((1, 0), (0, 1), (1, 1))
ANY_SPEC = pl.BlockSpec(memory_space=pl.ANY)


class _Mesh:
    def __init__(self):
        self.x, self.y, self.c = lax.axis_index("x"), lax.axis_index("y"), lax.axis_index("c")

    def dev(self, fx, fy, fc):
        return (1 - self.x if fx else self.x, 1 - self.y if fy else self.y, 1 - self.c if fc else self.c)

    def slot(self, fx, fy, fc):
        px, py, pc = self.dev(fx, fy, fc)
        return 4 * px + 2 * py + pc

    def chip(self, fx, fy):
        px, py, _ = self.dev(fx, fy, 0)
        return 2 * px + py


def _remote(src, dst, send_sems, recv_sems, idx, dev):
    return pltpu.make_async_remote_copy(src_ref=src, dst_ref=dst, send_sem=send_sems.at[idx], recv_sem=recv_sems.at[idx],
                                        device_id=dev, device_id_type=pl.DeviceIdType.MESH)


def _gather_phases(ins, outs, send_sems, recv_sems):
    n = len(ins)
    me = _Mesh()

    def first():
        cps = []
        for a in range(n):
            mine = outs[a].at[me.slot(0, 0, 0)]
            cps.append(_remote(ins[a], mine, send_sems, recv_sems, (a, 0), me.dev(0, 0, 1)))
            for j, (fx, fy) in enumerate(OTHER_CHIPS):
                cps.append(_remote(ins[a], mine, send_sems, recv_sems, (a, 1 + j), me.dev(fx, fy, 0)))
        return cps

    def passed():
        return [_remote(outs[a].at[me.slot(fx, fy, 0)], outs[a].at[me.slot(fx, fy, 0)], send_sems, recv_sems, (a, 4 + j),
                        me.dev(0, 0, 1)) for a in range(n) for j, (fx, fy) in enumerate(OTHER_CHIPS)]

    def start():
        for cp in first():
            cp.start()

    def forward():
        fwd = passed()
        for a in range(n):
            for j, (fx, fy) in enumerate(OTHER_CHIPS):
                _remote(ins[a], outs[a].at[me.slot(fx, fy, 0)], send_sems, recv_sems, (a, 1 + j), me.dev(fx, fy, 0)).wait_recv()
                fwd[3 * a + j].start()

    def finish():
        for a in range(n):
            _remote(ins[a], outs[a].at[me.slot(0, 0, 1)], send_sems, recv_sems, (a, 0), me.dev(0, 0, 1)).wait_recv()
            for j, (fx, fy) in enumerate(OTHER_CHIPS):
                _remote(ins[a], outs[a].at[me.slot(fx, fy, 1)], send_sems, recv_sems, (a, 4 + j), me.dev(0, 0, 1)).wait_recv()
        for cp in first() + passed():
            cp.wait_send()

    return start, forward, finish


def _own_blocks(stacks, arrs):
    me = 4 * lax.axis_index("x") + 2 * lax.axis_index("y") + lax.axis_index("c")
    return [lax.dynamic_update_slice_in_dim(s, a[None], me, axis=0) for s, a in zip(stacks, arrs)]


def _gather_all(arrs, name):
    n = len(arrs)

    def body(*refs):
        start, forward, finish = _gather_phases(refs[:n], refs[n:2 * n], *refs[2 * n:])
        start()
        forward()
        finish()

    stacks = pl.pallas_call(
        body, in_specs=[ANY_SPEC] * n, out_specs=[ANY_SPEC] * n,
        out_shape=[jax.ShapeDtypeStruct((N_DEV,) + a.shape, a.dtype) for a in arrs],
        scratch_shapes=[pltpu.SemaphoreType.DMA((n, 7)), pltpu.SemaphoreType.DMA((n, 7))],
        compiler_params=pltpu.CompilerParams(has_side_effects=True), name=name)(*arrs)
    return _own_blocks(stacks, arrs)


def _mm_nn_gather(a, b, out_dtype, name, shards):
    m, k = a.shape
    _, n = b.shape
    tm = _row_tile(m)
    steps = m // tm
    ns = len(shards)

    def body(a_ref, b_ref, *rest):
        o_ref = rest[ns]
        start, forward, finish = _gather_phases(rest[:ns], rest[ns + 1:2 * ns + 1], *rest[2 * ns + 1:])
        i = pl.program_id(0)
        pl.when(i == 0)(start)
        pl.when(i == steps // 2)(forward)
        av = a_ref[...].astype(BF16)
        for s, w in _chunks(n, 512):
            o_ref[:, s:s + w] = _dot(av, b_ref[:, s:s + w]).astype(out_dtype)
        pl.when(i == steps - 1)(finish)

    outs = pl.pallas_call(
        body, grid=(steps,),
        in_specs=[pl.BlockSpec((tm, k), lambda i: (i, 0)), pl.BlockSpec((k, n), lambda i: (0, 0))] + [ANY_SPEC] * ns,
        out_specs=[pl.BlockSpec((tm, n), lambda i: (i, 0))] + [ANY_SPEC] * ns,
        out_shape=[jax.ShapeDtypeStruct((m, n), out_dtype)] + [jax.ShapeDtypeStruct((N_DEV,) + s.shape, s.dtype) for s in shards],
        scratch_shapes=[pltpu.SemaphoreType.DMA((ns, 7)), pltpu.SemaphoreType.DMA((ns, 7))],
        compiler_params=pltpu.CompilerParams(dimension_semantics=("arbitrary",), vmem_limit_bytes=VMEM_LIMIT,
                                             has_side_effects=True), name=name)(a, b, *shards)
    return outs[0], _own_blocks(outs[1:], shards)


def _swap_in_chip(arrs, name):
    n = len(arrs)

    def body(*refs):
        ins, got = refs[:n], refs[n:2 * n]
        send_sems, recv_sems = refs[2 * n:]
        me = _Mesh()
        sends = [_remote(ins[a].at[2 * j + (1 - me.c)], got[a].at[j], send_sems, recv_sems, (a, j), me.dev(0, 0, 1))
                 for a in range(n) for j in range(4)]
        for cp in sends:
            cp.start()
        for cp in sends:
            cp.wait()

    return pl.pallas_call(
        body, in_specs=[ANY_SPEC] * n, out_specs=[ANY_SPEC] * n,
        out_shape=[jax.ShapeDtypeStruct((4,) + a.shape[1:], a.dtype) for a in arrs],
        scratch_shapes=[pltpu.SemaphoreType.DMA((n, 4)), pltpu.SemaphoreType.DMA((n, 4))],
        compiler_params=pltpu.CompilerParams(has_side_effects=True), name=name)(*arrs)


def _add_pair(a, b, name):
    _, r, c = a.shape
    tr = _adam_rows(r)

    def body(a_ref, b_ref, o_ref):
        o_ref[...] = (a_ref[...].astype(F32) + b_ref[...].astype(F32)).astype(o_ref.dtype)

    blk = pl.BlockSpec((4, tr, c), lambda i: (0, i, 0))
    return pl.pallas_call(body, grid=(r // tr,), in_specs=[blk, blk], out_specs=blk,
                          out_shape=jax.ShapeDtypeStruct(a.shape, a.dtype), compiler_params=_cparams("parallel"), name=name)(a, b)


def _swap_chips(sums, gather, name):
    ns, ng = len(sums), len(gather)
    n = ns + ng

    def body(*refs):
        ins, outs = refs[:n], refs[n:2 * n]
        send_sems, recv_sems, local_sems = refs[2 * n:]
        me = _Mesh()
        sends, recvs = _chip_swap_copies(ins[:ns], outs[:ns], send_sems, recv_sems)
        local = []
        for a in range(ns, n):
            local.append(pltpu.make_async_copy(ins[a], outs[a].at[me.slot(0, 0, 0)], local_sems.at[a]))
            for k in range(1, N_DEV):
                flips = (k >> 2 & 1, k >> 1 & 1, k & 1)
                sends.append(_remote(ins[a], outs[a].at[me.slot(0, 0, 0)], send_sems, recv_sems, (a, k - 1), me.dev(*flips)))
                recvs.append(_remote(ins[a], outs[a].at[me.slot(*flips)], send_sems, recv_sems, (a, k - 1), me.dev(*flips)))
        for cp in local + sends:
            cp.start()
        for cp in recvs:
            cp.wait_recv()
        for cp in sends:
            cp.wait_send()
        for cp in local:
            cp.wait()

    out_shape = [jax.ShapeDtypeStruct(a.shape, a.dtype) for a in sums]
    out_shape += [jax.ShapeDtypeStruct((N_DEV,) + a.shape, a.dtype) for a in gather]
    outs = pl.pallas_call(
        body, in_specs=[ANY_SPEC] * n, out_specs=[ANY_SPEC] * n, out_shape=out_shape,
        scratch_shapes=[pltpu.SemaphoreType.DMA((n, N_DEV - 1)), pltpu.SemaphoreType.DMA((n, N_DEV - 1)),
                        pltpu.SemaphoreType.DMA((n,))],
        compiler_params=pltpu.CompilerParams(has_side_effects=True), name=name)(*sums, *gather)
    return _own_chip(outs[:ns], sums) + list(outs[ns:])


def _chip_swap_copies(ins, outs, send_sems, recv_sems, arrivals=True):
    me = _Mesh()
    here = me.chip(0, 0)
    sends, recvs = [], []
    for a in range(len(ins)):
        for j, (fx, fy) in enumerate(OTHER_CHIPS):
            there = me.chip(fx, fy)
            sends.append(_remote(ins[a].at[there], outs[a].at[here], send_sems, recv_sems, (a, j), me.dev(fx, fy, 0)))
            if arrivals:
                recvs.append(_remote(ins[a].at[there], outs[a].at[there], send_sems, recv_sems, (a, j), me.dev(fx, fy, 0)))
    return sends, recvs


def _own_chip(outs, sums):
    here = 2 * lax.axis_index("x") + lax.axis_index("y")
    return [lax.dynamic_update_slice_in_dim(o, lax.dynamic_slice_in_dim(s, here, 1, axis=0), here, axis=0)
            for o, s in zip(outs, sums)]


def _mm_tn_swap(a, b, name, sums):
    t, m = a.shape
    _, n = b.shape
    tk = _row_tile(t)
    halves = 2 if (m * n * 4 > 8 * 1024 * 1024 and n % 256 == 0) else 1
    tn = n // halves
    steps = t // tk
    ns = len(sums)

    def body(a_ref, b_ref, *rest):
        o_ref = rest[ns]
        j, s = pl.program_id(0), pl.program_id(1)

        def start():
            for cp in _chip_swap_copies(rest[:ns], rest[ns + 1:2 * ns + 1], *rest[2 * ns + 1:], arrivals=False)[0]:
                cp.start()

        def finish():
            sends, recvs = _chip_swap_copies(rest[:ns], rest[ns + 1:2 * ns + 1], *rest[2 * ns + 1:])
            for cp in recvs:
                cp.wait_recv()
            for cp in sends:
                cp.wait_send()

        pl.when((j == 0) & (s == 0))(start)

        @pl.when(s == 0)
        def _():
            o_ref[...] = jnp.zeros_like(o_ref)
        at = a_ref[...].astype(BF16).T
        for c0, w in _chunks(tn, 512):
            o_ref[:, c0:c0 + w] += _dot(at, b_ref[:, c0:c0 + w].astype(BF16))
        pl.when((j == halves - 1) & (s == steps - 1))(finish)

    outs = pl.pallas_call(
        body, grid=(halves, steps),
        in_specs=[pl.BlockSpec((tk, m), lambda j, s: (s, 0)), pl.BlockSpec((tk, tn), lambda j, s: (s, j))] + [ANY_SPEC] * ns,
        out_specs=[pl.BlockSpec((m, tn), lambda j, s: (0, j))] + [ANY_SPEC] * ns,
        out_shape=[jax.ShapeDtypeStruct((m, n), F32)] + [jax.ShapeDtypeStruct(x.shape, x.dtype) for x in sums],
        scratch_shapes=[pltpu.SemaphoreType.DMA((ns, 3)), pltpu.SemaphoreType.DMA((ns, 3))],
        compiler_params=pltpu.CompilerParams(dimension_semantics=("arbitrary", "arbitrary"), vmem_limit_bytes=VMEM_LIMIT,
                                             has_side_effects=True), name=name)(a, b, *sums)
    return outs[0], _own_chip(outs[1:], sums)


def _adam_rows(r):
    for cand in (128, 64, 32, 16, 8):
        if r % cand == 0 and r > cand:
            return cand
    return r


def _adamw(parts, w, m, v, name):
    r, c = w.shape
    tr = _adam_rows(r)
    n_parts = parts.shape[0]

    def body(p_ref, w_ref, m_ref, v_ref, g_ref, d_ref, nm_ref, nv_ref):
        g = p_ref[0].astype(F32)
        for k in range(1, n_parts):
            g = g + p_ref[k].astype(F32)
        m_new = ADAM_B1 * m_ref[...] + (1.0 - ADAM_B1) * g
        v_new = ADAM_B2 * v_ref[...] + (1.0 - ADAM_B2) * (g * g)
        m_hat = m_new / (1.0 - ADAM_B1 ** ADAM_STEP)
        v_hat = v_new / (1.0 - ADAM_B2 ** ADAM_STEP)
        g_ref[...] = g
        d_ref[...] = -ADAM_LR * (m_hat / (jnp.sqrt(v_hat) + ADAM_EPS) + ADAM_WD * w_ref[...])
        nm_ref[...] = m_new
        nv_ref[...] = v_new

    blk = pl.BlockSpec((tr, c), lambda i: (i, 0))
    shp = jax.ShapeDtypeStruct((r, c), F32)
    return pl.pallas_call(
        body, grid=(r // tr,), in_specs=[pl.BlockSpec((n_parts, tr, c), lambda i: (0, i, 0)), blk, blk, blk],
        out_specs=[blk] * 4, out_shape=[shp] * 4, compiler_params=_cparams("parallel"), name=name)(parts, w, m, v)


def _rows128(a):
    return a.reshape(-1, 128)


def _pad_rows(a, rows):
    return jnp.pad(a, ((0, rows - a.shape[0]), (0, 0)))


def _lane16(a):
    return jnp.pad(a.reshape(1, -1), ((0, 0), (0, 128 - a.size)))


SHARD_PACK_ROWS = 544
REPL_PACK_ROWS = 72
LOSS_ROW = 71


def _pack_shard(meta, lru_conv_w, odd_norm, ssd_conv_w, lru_w_a, lru_w_x):
    parts = [meta.reshape(16, 128), lru_conv_w.reshape(4, 128), odd_norm.reshape(1, 128), _rows128(ssd_conv_w.reshape(4, 192)),
             _rows128(lru_w_a.reshape(4, 32, 256)), _rows128(lru_w_x.reshape(4, 32, 256))]
    return _pad_rows(jnp.concatenate(parts, axis=0), SHARD_PACK_ROWS)


def _unpack_shard(p):
    return (p[0:16], p[16:20].reshape(1, 4, 128), p[20:21], p[21:27].reshape(1, 4, 192),
            p[27:283].reshape(1, 4, 32, 256), p[283:539].reshape(1, 4, 32, 256))


def _pack_repl(even_norm, lru_conv_b, lru_b_a, lru_b_x, lru_lambda, ssd_norm, final_norm, ssd_conv_b, dt_bias, a_log, ssd_d):
    parts = [_rows128(v) for v in (even_norm, lru_conv_b, lru_b_a, lru_b_x, lru_lambda, ssd_norm, final_norm, ssd_conv_b)]
    parts += [_lane16(dt_bias), _lane16(a_log), _lane16(ssd_d)]
    return _pad_rows(jnp.concatenate(parts, axis=0), REPL_PACK_ROWS)


def _unpack_repl(p):
    vec = lambda i: p[8 * i:8 * i + 8].reshape(1, 1024)
    return (vec(0), vec(1), vec(2), vec(3), vec(4), vec(5), p[48:56].reshape(1024), p[56:68].reshape(1, 1536),
            p[68:69, :16], p[69:70, :16], p[70:71, :16])


def _local_step(x, tgt, meta, even_norm, w_in_e, lru_conv_w, lru_conv_b, lru_w_a, lru_b_a, lru_w_x, lru_b_x, lru_lambda,
                ssd_conv_w, ssd_conv_b, dt_bias, a_log, ssd_d, ssd_norm, odd_norm, final_norm, even_in, even_in_dw, late=None):
    bsz, seq, d = x.shape
    lp = PAD + N_META + seq
    t = bsz * lp
    h0 = jnp.concatenate([jnp.zeros((bsz, PAD, d), F32), jnp.broadcast_to(meta[None], (bsz, N_META, d)), x], axis=1).reshape(t, d)

    u0 = _norm_fwd(h0, even_norm, "norm0_fwd")
    proj, w_out_e, w_in_o, w_out_o = even_in(u0)
    lru = (lru_conv_w, lru_conv_b, lru_w_a, lru_b_a, lru_w_x, lru_b_x, lru_lambda)
    ya = _lru_fwd(proj, *lru, bsz, lp)
    xbc = _ssd_pre_fwd(proj, ssd_conv_w, ssd_conv_b, bsz, lp)
    y, stacks = _ssd_fwd(xbc, proj, dt_bias, a_log, ssd_d, bsz, lp, late[0] if late else ())
    if late:
        w_in_o, w_out_o = late[1](stacks)
    mixed = _gnorm_fwd(y, proj, ssd_norm, ya)
    h1 = _mm_nn(mixed, w_out_e, F32, "even_out", res=h0)
    u2 = _norm_fwd(h1, odd_norm, "norm1_fwd")
    qkvg = _mm_nn(u2, w_in_o, F32, "odd_in")
    o, og = _sb_fwd(qkvg, bsz, lp)
    h2 = _mm_nn(og, w_out_o, F32, "odd_out", res=h1)
    loss, dh2, d_final_norm = _final_loss(h2, final_norm, tgt, lp)

    dog = _mm_nn(dh2, w_out_o.T, F32, "odd_out_dx")
    d_w_out_o = _mm_tn(og, dh2, "odd_out_dw")
    dqkvg = jnp.concatenate(_sb_bwd(qkvg, o, dog, bsz, lp), axis=1)
    du2 = _mm_nn(dqkvg, w_in_o.T, F32, "odd_in_dx")
    d_w_in_o = _mm_tn(u2, dqkvg, "odd_in_dw")
    dh1, d_odd_norm = _norm_bwd(h1, odd_norm, du2, dh2, "norm1_bwd")

    dmixed = _mm_nn(dh1, w_out_e.T, F32, "even_out_dx")
    d_w_out_e = _mm_tn(mixed, dh1, "even_out_dw")
    dlx, dgate, d_lru_conv_w, d_lru_conv_b, d_w_a, d_b_a, d_w_x, d_b_x, d_lambda = _lru_bwd(proj, dmixed, *lru, bsz, lp)
    dy, dz, d_ssd_norm = _gnorm_bwd(y, proj, ssd_norm, dmixed)
    dxs, dbm, dcm, ddt, d_dt_bias, d_a_log, d_ssd_d = _ssd_bwd(xbc, proj, dt_bias, a_log, ssd_d, dy, bsz, lp)
    dxbc, d_ssd_conv_w, d_ssd_conv_b = _ssd_pre_bwd(proj, dxs, dbm, dcm, ssd_conv_w, ssd_conv_b, bsz, lp)
    ddt = (ddt[:, :DT_L] + ddt[:, DT_L:]).astype(BF16)
    dproj = jnp.concatenate([dlx, dgate, dz, dxbc, ddt, jnp.zeros((t, EVEN_NP - 4608 - DT_L), BF16)], axis=1)
    du0 = _mm_nn(dproj, w_in_e.T, F32, "even_in_dx")
    d_w_in_e, early = even_in_dw(u0, dproj, (d_w_out_e, d_w_in_o, d_w_out_o))
    dh0, d_even_norm = _norm_bwd(h0, even_norm, du0, dh1, "norm0_bwd")
    grad_x = dh0.reshape(bsz, lp, d)[:, PAD + N_META:]
    d_meta = _meta_grad(dh0, bsz, lp)
    heads = lambda p: (p[0:1] + p[8:9])[:, :SSD_HEADS]
    grads = dict(meta=d_meta, even_norm=d_even_norm, even_w_in=d_w_in_e[:, :EVEN_IN], lru_conv_w=d_lru_conv_w,
                 lru_conv_b=d_lru_conv_b, lru_w_a=d_w_a, lru_b_a=d_b_a, lru_w_x=d_w_x, lru_b_x=d_b_x, lru_lambda=d_lambda,
                 ssd_conv_w=d_ssd_conv_w, ssd_conv_b=d_ssd_conv_b, ssd_dt_bias=heads(d_dt_bias), ssd_a_log=heads(d_a_log),
                 ssd_d=heads(d_ssd_d), ssd_norm=d_ssd_norm, even_w_out=d_w_out_e, odd_norm=d_odd_norm, odd_w_in=d_w_in_o,
                 odd_w_out=d_w_out_o, final_norm=d_final_norm)
    return loss[0, 0], grad_x, grads, early


WEIGHTS = ['meta', 'even_norm', 'even_w_in', 'lru_conv_w', 'lru_conv_b', 'lru_w_a', 'lru_b_a', 'lru_w_x', 'lru_b_x', 'lru_lambda',
           'ssd_conv_w', 'ssd_conv_b', 'ssd_dt_bias', 'ssd_a_log', 'ssd_d', 'ssd_norm', 'even_w_out', 'odd_norm', 'odd_w_in',
           'odd_w_out', 'final_norm']


def _blocks_of(a, axis):
    shp = a.shape
    a = a.reshape(shp[:axis] + (N_DEV, shp[axis] // N_DEV) + shp[axis + 1:])
    return jnp.moveaxis(a, axis, 0)


def _unblock(a, axis):
    a = jnp.moveaxis(a, 0, axis)
    shp = a.shape
    return a.reshape(shp[:axis] + (shp[axis] * shp[axis + 1],) + shp[axis + 2:])


def kernel(x, meta, even_norm, even_w_in, lru_conv_w, lru_conv_b, lru_w_a, lru_b_a, lru_w_x, lru_b_x, lru_lambda, ssd_conv_w, ssd_conv_b, ssd_dt_bias, ssd_a_log, ssd_d, ssd_norm, even_w_out, odd_norm, odd_w_in, odd_w_out, final_norm, loss_target, m_meta, m_even_norm, m_even_w_in, m_lru_conv_w, m_lru_conv_b, m_lru_w_a, m_lru_b_a, m_lru_w_x, m_lru_b_x, m_lru_lambda, m_ssd_conv_w, m_ssd_conv_b, m_ssd_dt_bias, m_ssd_a_log, m_ssd_d, m_ssd_norm, m_even_w_out, m_odd_norm, m_odd_w_in, m_odd_w_out, m_final_norm, v_meta, v_even_norm, v_even_w_in, v_lru_conv_w, v_lru_conv_b, v_lru_w_a, v_lru_b_a, v_lru_w_x, v_lru_b_x, v_lru_lambda, v_ssd_conv_w, v_ssd_conv_b, v_ssd_dt_bias, v_ssd_a_log, v_ssd_d, v_ssd_norm, v_even_w_out, v_odd_norm, v_odd_w_in, v_odd_w_out, v_final_norm):
    w = dict(meta=meta, even_norm=even_norm, even_w_in=even_w_in, lru_conv_w=lru_conv_w, lru_conv_b=lru_conv_b, lru_w_a=lru_w_a,
             lru_b_a=lru_b_a, lru_w_x=lru_w_x, lru_b_x=lru_b_x, lru_lambda=lru_lambda, ssd_conv_w=ssd_conv_w,
             ssd_conv_b=ssd_conv_b, ssd_dt_bias=ssd_dt_bias, ssd_a_log=ssd_a_log, ssd_d=ssd_d, ssd_norm=ssd_norm,
             even_w_out=even_w_out, odd_norm=odd_norm, odd_w_in=odd_w_in, odd_w_out=odd_w_out, final_norm=final_norm)
    m = dict(meta=m_meta, even_norm=m_even_norm, even_w_in=m_even_w_in, lru_conv_w=m_lru_conv_w, lru_conv_b=m_lru_conv_b,
             lru_w_a=m_lru_w_a, lru_b_a=m_lru_b_a, lru_w_x=m_lru_w_x, lru_b_x=m_lru_b_x, lru_lambda=m_lru_lambda,
             ssd_conv_w=m_ssd_conv_w, ssd_conv_b=m_ssd_conv_b, ssd_dt_bias=m_ssd_dt_bias, ssd_a_log=m_ssd_a_log, ssd_d=m_ssd_d,
             ssd_norm=m_ssd_norm, even_w_out=m_even_w_out, odd_norm=m_odd_norm, odd_w_in=m_odd_w_in, odd_w_out=m_odd_w_out,
             final_norm=m_final_norm)
    v = dict(meta=v_meta, even_norm=v_even_norm, even_w_in=v_even_w_in, lru_conv_w=v_lru_conv_w, lru_conv_b=v_lru_conv_b,
             lru_w_a=v_lru_w_a, lru_b_a=v_lru_b_a, lru_w_x=v_lru_w_x, lru_b_x=v_lru_b_x, lru_lambda=v_lru_lambda,
             ssd_conv_w=v_ssd_conv_w, ssd_conv_b=v_ssd_conv_b, ssd_dt_bias=v_ssd_dt_bias, ssd_a_log=v_ssd_a_log, ssd_d=v_ssd_d,
             ssd_norm=v_ssd_norm, even_w_out=v_even_w_out, odd_norm=v_odd_norm, odd_w_in=v_odd_w_in, odd_w_out=v_odd_w_out,
             final_norm=v_final_norm)
    shard_names = ('meta', 'lru_conv_w', 'odd_norm', 'ssd_conv_w', 'lru_w_a', 'lru_w_x')
    repl_names = ('even_norm', 'lru_conv_b', 'lru_b_a', 'lru_b_x', 'lru_lambda', 'ssd_norm', 'final_norm', 'ssd_conv_b',
                  'ssd_dt_bias', 'ssd_a_log', 'ssd_d')
    big_names = ('even_w_in', 'even_w_out', 'odd_w_in', 'odd_w_out')

    gates = jnp.concatenate([lru_w_a.reshape(128, 256), lru_w_x.reshape(128, 256)], axis=0).astype(BF16)
    small = _pack_shard(*[w[k] for k in shard_names])
    g_in_e, g_gates, g_small = _gather_all([even_w_in[0].astype(BF16), gates, small], "gather_weights")
    w_in_e = jnp.pad(_unblock(g_in_e, 1), ((0, 0), (0, EVEN_NP - EVEN_IN)))

    def even_in(u0):
        proj, (g_out_e,) = _mm_nn_gather(u0, w_in_e, F32, "even_in", [even_w_out[0].astype(BF16)])
        return proj, g_out_e.reshape(2048, 1024), None, None

    late = ([odd_w_in[0].astype(BF16), odd_w_out[0].astype(BF16)],
            lambda stacks: (_unblock(stacks[0], 1), stacks[1].reshape(1024, 1024)))

    gates_full = jnp.moveaxis(g_gates.reshape(N_DEV, 2, 4, 32, 256), 0, 2).reshape(2, 4, 256, 256)
    f_meta = _unblock(g_small[:, 0:16], 1)
    f_lru_conv_w = _unblock(g_small[:, 16:20], 1)
    f_odd_norm = _unblock(g_small[:, 20:21], 1)
    f_ssd_conv_w = _unblock(g_small[:, 21:27].reshape(N_DEV, 4, 192), 1)

    my_c = lax.axis_index("c")

    def chip_sums(by_owner, tag):
        got = _swap_in_chip(by_owner, "reduce_in_chip_" + tag)
        kept = [lax.dynamic_index_in_dim(a.reshape((4, 2) + a.shape[1:]), my_c, axis=1, keepdims=False) for a in by_owner]
        return [_add_pair(a, b, "chip_sum_%s%d" % (tag, i)) for i, (a, b) in enumerate(zip(kept, got))]

    def even_in_dw(u0, dproj, done):
        d_out_e, d_in_o, d_out_o = done
        sums = chip_sums([d_out_e.reshape(N_DEV, 256, 1024).astype(BF16), _blocks_of(d_in_o, 1).astype(BF16),
                          d_out_o.reshape(N_DEV, 128, 1024).astype(BF16)], "a")
        return _mm_tn_swap(u0, dproj, "even_in_dw", sums)

    loss, grad_x, g, (p_out_e, p_in_o, p_out_o) = _local_step(
        x, loss_target, f_meta, even_norm, w_in_e, f_lru_conv_w, lru_conv_b, gates_full[0], lru_b_a, gates_full[1], lru_b_x,
        lru_lambda, f_ssd_conv_w, ssd_conv_b, _lane16(ssd_dt_bias), _lane16(ssd_a_log), _lane16(ssd_d), ssd_norm,
        f_odd_norm, final_norm.reshape(1, -1), even_in, even_in_dw, late)

    s_small = jnp.stack([_pack_shard(g['meta'][:, 128 * p:128 * (p + 1)], g['lru_conv_w'][:, 128 * p:128 * (p + 1)],
                                     g['odd_norm'][:, 128 * p:128 * (p + 1)], g['ssd_conv_w'][:, 192 * p:192 * (p + 1)],
                                     g['lru_w_a'][:, 32 * p:32 * (p + 1)], g['lru_w_x'][:, 32 * p:32 * (p + 1)])
                         for p in range(N_DEV)])
    r_pack = _pack_repl(*[g[k] for k in repl_names]).at[LOSS_ROW, 0].set(loss)
    p_in_e, p_small, p_repl = _swap_chips(chip_sums([_blocks_of(g['even_w_in'], 1).astype(BF16), s_small], "b"), [r_pack],
                                          "reduce_across_chips")

    res = {}
    for k, parts in zip(big_names, (p_in_e, p_out_e, p_in_o, p_out_o)):
        outs = _adamw(parts, w[k][0], m[k][0], v[k][0], "adamw_" + k)
        res[k] = [o[None] for o in outs]
    outs = _adamw(p_small, small, _pack_shard(*[m[k] for k in shard_names]), _pack_shard(*[v[k] for k in shard_names]), "adamw_sharded")
    unpacked = [_unpack_shard(o) for o in outs]
    for i, k in enumerate(shard_names):
        res[k] = [u[i] for u in unpacked]
    outs = _adamw(p_repl, _pack_repl(*[w[k] for k in repl_names]), _pack_repl(*[m[k] for k in repl_names]),
                  _pack_repl(*[v[k] for k in repl_names]), "adamw_replicated")
    unpacked = [_unpack_repl(o) for o in outs]
    for i, k in enumerate(repl_names):
        res[k] = [u[i] for u in unpacked]
    loss = outs[0][LOSS_ROW, 0]
    return (loss, grad_x, *[res[k][0] for k in WEIGHTS], *[res[k][1] for k in WEIGHTS], *[res[k][2] for k in WEIGHTS],
            *[res[k][3] for k in WEIGHTS])
```

```python
import jax
import jax.numpy as jnp
from jax import lax
from jax.experimental import pallas as pl
from jax.experimental.pallas import tpu as pltpu

F32 = jnp.float32
BF16 = jnp.bfloat16

D_MODEL = 1024
N_META = 16
PAD = 112
EPS = 1e-6
CONV_W = 4
LRU_BLOCKS = 4
LRU_BLOCK = 256
RG_LRU_C = 8.0
SSD_HEADS = 16
SSD_P = 64
SSD_N = 128
SSD_Q = 64
SSD_GROUPS = 2
SSD_GW = 512
SSD_CONV_DIM = 1536
SB_HEADS = 16
SB_D = 64
SB_T = 128
EVEN_IN = 4624
EVEN_NP = 4864
N_DEV = 8

ADAM_LR = 0.001
ADAM_B1 = 0.9
ADAM_B2 = 0.999
ADAM_EPS = 1e-08
ADAM_WD = 0.01
ADAM_STEP = 10

VMEM_LIMIT = 56 * 1024 * 1024


def _cparams(*sem):
    return pltpu.CompilerParams(dimension_semantics=sem, vmem_limit_bytes=VMEM_LIMIT)


def _row_tile(t):
    for c in (512, 256, 128):
        if t % c == 0:
            return c
    raise ValueError(t)


def _dot(a, b):
    return jnp.dot(a, b, preferred_element_type=F32)


def _dot_nt(a, b):
    return lax.dot_general(a, b, (((1,), (1,)), ((), ())), preferred_element_type=F32)


def _dot_tn(a, b):
    return lax.dot_general(a, b, (((0,), (0,)), ((), ())), preferred_element_type=F32)


def _split3(a):
    a1 = a.astype(BF16)
    r1 = a - a1.astype(F32)
    a2 = r1.astype(BF16)
    a3 = (r1 - a2.astype(F32)).astype(BF16)
    return a1, a2, a3


def _xdot_r(a, m01):
    a1, a2, a3 = _split3(a)
    return _dot(a1, m01) + _dot(a2, m01) + _dot(a3, m01)


def _xdot_l(m01, a):
    a1, a2, a3 = _split3(a)
    return _dot(m01, a1) + _dot(m01, a2) + _dot(m01, a3)


def _sigmoid(x):
    return 0.5 * jnp.tanh(0.5 * x) + 0.5


def _softplus(x):
    return jnp.maximum(x, 0.0) + jnp.log(1.0 + jnp.exp(-jnp.abs(x)))


def _iota(shape, dim):
    return lax.broadcasted_iota(jnp.int32, shape, dim)


def _block_scan(a_ref, b_ref, o_ref, reverse=False):
    n, w = a_ref.shape
    nb = n // 8
    unroll = 4 if nb % 4 == 0 else 1
    row = _iota((8, w), 0)

    def block(blk, carry):
        rows = pl.ds(pl.multiple_of(blk * 8, 8), 8)
        a, b = a_ref[rows, :], b_ref[rows, :]
        for k in (1, 2, 4):
            keep = (row < 8 - k) if reverse else (row >= k)
            shift = 8 - k if reverse else k
            a_sh = jnp.where(keep, pltpu.roll(a, shift, 0), 1.0)
            b_sh = jnp.where(keep, pltpu.roll(b, shift, 0), 0.0)
            b = a * b_sh + b
            a = a * a_sh
        o = a * carry + b
        o_ref[rows, :] = o
        return o[0:1, :] if reverse else o[7:8, :]

    def trip(i, carry):
        for u in range(unroll):
            j = i * unroll + u
            carry = block(nb - 1 - j if reverse else j, carry)
        return carry

    lax.fori_loop(0, nb // unroll, trip, jnp.zeros((1, w), F32))


def _chunks(n, c):
    return [(s, min(c, n - s)) for s in range(0, n, c)]


def _mm_nn(a, b, out_dtype, name, res=None):
    m, k = a.shape
    _, n = b.shape
    tm = _row_tile(m)

    def body(*refs):
        if res is None:
            a_ref, b_ref, o_ref = refs
        else:
            a_ref, b_ref, r_ref, o_ref = refs
        av = a_ref[...].astype(BF16)
        for s, w in _chunks(n, 512):
            acc = _dot(av, b_ref[:, s:s + w])
            if res is not None:
                acc = acc + r_ref[:, s:s + w]
            o_ref[:, s:s + w] = acc.astype(out_dtype)

    in_specs = [pl.BlockSpec((tm, k), lambda i: (i, 0)), pl.BlockSpec((k, n), lambda i: (0, 0))]
    args = [a, b]
    if res is not None:
        in_specs.append(pl.BlockSpec((tm, n), lambda i: (i, 0)))
        args.append(res)
    return pl.pallas_call(
        body, grid=(m // tm,), in_specs=in_specs,
        out_specs=pl.BlockSpec((tm, n), lambda i: (i, 0)),
        out_shape=jax.ShapeDtypeStruct((m, n), out_dtype),
        compiler_params=_cparams("parallel"), name=name)(*args)


def _mm_tn(a, b, name):
    t, m = a.shape
    _, n = b.shape
    tk = _row_tile(t)
    halves = 2 if (m * n * 4 > 8 * 1024 * 1024 and n % 256 == 0) else 1
    tn = n // halves

    def body(a_ref, b_ref, o_ref):
        @pl.when(pl.program_id(1) == 0)
        def _():
            o_ref[...] = jnp.zeros_like(o_ref)
        at = a_ref[...].astype(BF16).T
        for s, w in _chunks(tn, 512):
            o_ref[:, s:s + w] += _dot(at, b_ref[:, s:s + w].astype(BF16))

    return pl.pallas_call(
        body, grid=(halves, t // tk),
        in_specs=[pl.BlockSpec((tk, m), lambda j, s: (s, 0)), pl.BlockSpec((tk, tn), lambda j, s: (s, j))],
        out_specs=pl.BlockSpec((m, tn), lambda j, s: (0, j)),
        out_shape=jax.ShapeDtypeStruct((m, n), F32),
        compiler_params=_cparams("parallel", "arbitrary"), name=name)(a, b)


def _norm_fwd(h, w, name):
    t, d = h.shape
    tm = _row_tile(t)

    def body(h_ref, w_ref, u_ref):
        x = h_ref[...]
        r = lax.rsqrt(jnp.mean(x * x, axis=-1, keepdims=True) + EPS)
        u_ref[...] = (x * r * w_ref[...]).astype(BF16)

    return pl.pallas_call(
        body, grid=(t // tm,),
        in_specs=[pl.BlockSpec((tm, d), lambda i: (i, 0)), pl.BlockSpec((1, d), lambda i: (0, 0))],
        out_specs=pl.BlockSpec((tm, d), lambda i: (i, 0)),
        out_shape=jax.ShapeDtypeStruct((t, d), BF16),
        compiler_params=_cparams("parallel"), name=name)(h, w)


def _norm_bwd(h, w, du, dres, name):
    t, d = h.shape
    tm = _row_tile(t)

    def body(h_ref, w_ref, du_ref, dr_ref, dh_ref, dw_ref):
        @pl.when(pl.program_id(0) == 0)
        def _():
            dw_ref[...] = jnp.zeros_like(dw_ref)
        x = h_ref[...]
        r = lax.rsqrt(jnp.mean(x * x, axis=-1, keepdims=True) + EPS)
        xh = x * r
        du_ = du_ref[...]
        g = du_ * w_ref[...]
        dh_ref[...] = dr_ref[...] + r * (g - xh * jnp.mean(g * xh, axis=-1, keepdims=True))
        dw_ref[...] += jnp.sum(du_ * xh, axis=0, keepdims=True)

    return pl.pallas_call(
        body, grid=(t // tm,),
        in_specs=[pl.BlockSpec((tm, d), lambda i: (i, 0)), pl.BlockSpec((1, d), lambda i: (0, 0)),
                  pl.BlockSpec((tm, d), lambda i: (i, 0)), pl.BlockSpec((tm, d), lambda i: (i, 0))],
        out_specs=[pl.BlockSpec((tm, d), lambda i: (i, 0)), pl.BlockSpec((1, d), lambda i: (0, 0))],
        out_shape=[jax.ShapeDtypeStruct((t, d), F32), jax.ShapeDtypeStruct((1, d), F32)],
        compiler_params=_cparams("arbitrary"), name=name)(h, w, du, dres)


def _final_loss(h2, w, tgt, lp):
    t, d = h2.shape
    bsz, seq, _ = tgt.shape
    nblk = lp // SB_T

    def body(h_ref, w_ref, t_ref, loss_ref, dh_ref, dw_ref):
        b, i = pl.program_id(0), pl.program_id(1)

        @pl.when((b == 0) & (i == 0))
        def _():
            loss_ref[...] = jnp.zeros_like(loss_ref)
            dw_ref[...] = jnp.zeros_like(dw_ref)

        @pl.when(i == 0)
        def _():
            dh_ref[...] = jnp.zeros_like(dh_ref)

        @pl.when(i > 0)
        def _():
            x = h_ref[...]
            r = lax.rsqrt(jnp.mean(x * x, axis=-1, keepdims=True) + EPS)
            xh = x * r
            wv = w_ref[...]
            diff = xh * wv - t_ref[0]
            loss_ref[...] += 0.5 * jnp.sum(jnp.mean(diff * diff, axis=-1, keepdims=True), axis=0, keepdims=True)
            dy = diff * (1.0 / d)
            g = dy * wv
            dh_ref[...] = r * (g - xh * jnp.mean(g * xh, axis=-1, keepdims=True))
            dw_ref[...] += jnp.sum(dy * xh, axis=0, keepdims=True)

    return pl.pallas_call(
        body, grid=(bsz, nblk),
        in_specs=[pl.BlockSpec((SB_T, d), lambda b, i: (b * nblk + i, 0)), pl.BlockSpec((1, d), lambda b, i: (0, 0)),
                  pl.BlockSpec((1, SB_T, d), lambda b, i: (b, jnp.maximum(i - 1, 0), 0))],
        out_specs=[pl.BlockSpec((1, 128), lambda b, i: (0, 0)), pl.BlockSpec((SB_T, d), lambda b, i: (b * nblk + i, 0)),
                   pl.BlockSpec((1, d), lambda b, i: (0, 0))],
        out_shape=[jax.ShapeDtypeStruct((1, 128), F32), jax.ShapeDtypeStruct((t, d), F32), jax.ShapeDtypeStruct((1, d), F32)],
        compiler_params=_cparams("arbitrary", "arbitrary"), name="final_loss")(h2, w, tgt)


def _shift_down(x, j):
    return x if j == 0 else pltpu.roll(x, j, 0)


def _shift_up(x, j):
    return x if j == 0 else pltpu.roll(x, x.shape[0] - j, 0)


def _conv(x, cw, cb):
    out = cb + cw[CONV_W - 1:CONV_W, :] * x
    for j in range(1, CONV_W):
        out = out + cw[CONV_W - 1 - j:CONV_W - j, :] * _shift_down(x, j)
    return out


def _conv_bwd_x(dy, cw):
    out = cw[CONV_W - 1:CONV_W, :] * dy
    for j in range(1, CONV_W):
        out = out + cw[CONV_W - 1 - j:CONV_W - j, :] * _shift_up(dy, j)
    return out


def _conv_bwd_w(dcw_ref, dy, x):
    for j in range(CONV_W):
        k = CONV_W - 1 - j
        dcw_ref[k:k + 1, :] += jnp.sum(dy * _shift_down(x, j), axis=0, keepdims=True)


def _lru_forward_block(x, cw, cb, wa, ba, wx, bx, lam, a_ref, b_ref, h_ref):
    lp = x.shape[0]
    lx = _conv(x, cw, cb)
    lxb = lx.astype(BF16)
    r = _sigmoid(_dot(lxb, wa) + ba)
    i = _sigmoid(_dot(lxb, wx) + bx)
    spl = _softplus(-lam)
    log_a = (-RG_LRU_C) * r * spl
    a = jnp.exp(log_a)
    mult = jnp.sqrt(-jnp.tanh(log_a) * (1.0 + a * a))
    valid = _iota((lp, 1), 0) >= PAD
    a_ref[...] = a
    b_ref[...] = jnp.where(valid, mult * i * lx, 0.0)
    _block_scan(a_ref, b_ref, h_ref)
    return lx, lxb, r, i, spl, a, mult, valid


def _lru_specs(lp):
    w = LRU_BLOCK
    return [
        pl.BlockSpec((lp, w), lambda g, b: (b, g)),
        pl.BlockSpec((lp, w), lambda g, b: (b, LRU_BLOCKS + g)),
        pl.BlockSpec((CONV_W, w), lambda g, b: (0, g)),
        pl.BlockSpec((1, w), lambda g, b: (0, g)),
        pl.BlockSpec((1, w, w), lambda g, b: (g, 0, 0)),
        pl.BlockSpec((1, w), lambda g, b: (0, g)),
        pl.BlockSpec((1, w, w), lambda g, b: (g, 0, 0)),
        pl.BlockSpec((1, w), lambda g, b: (0, g)),
        pl.BlockSpec((1, w), lambda g, b: (0, g)),
    ]


def _lru_fwd(proj, cw, cb, wa, ba, wx, bx, lam, bsz, lp):
    w = LRU_BLOCK

    def body(x_ref, g_ref, cw_ref, cb_ref, wa_ref, ba_ref, wx_ref, bx_ref, lam_ref, y_ref, a_s, b_s, h_s):
        _lru_forward_block(x_ref[...], cw_ref[...], cb_ref[...], wa_ref[0], ba_ref[...], wx_ref[0], bx_ref[...],
                           lam_ref[...], a_s, b_s, h_s)
        gate = g_ref[...]
        y_ref[...] = (h_s[...] * gate * _sigmoid(gate)).astype(BF16)

    return pl.pallas_call(
        body, grid=(LRU_BLOCKS, bsz), in_specs=_lru_specs(lp),
        out_specs=pl.BlockSpec((lp, w), lambda g, b: (b, g)),
        out_shape=jax.ShapeDtypeStruct((bsz * lp, 2 * LRU_BLOCKS * w), BF16),
        scratch_shapes=[pltpu.VMEM((lp, w), F32)] * 3,
        compiler_params=_cparams("parallel", "arbitrary"), name="lru_fwd")(proj, proj, cw, cb, wa, ba, wx, bx, lam)


def _lru_bwd(proj, dmixed, cw, cb, wa, ba, wx, bx, lam, bsz, lp):
    w = LRU_BLOCK

    def body(x_ref, g_ref, cw_ref, cb_ref, wa_ref, ba_ref, wx_ref, bx_ref, lam_ref, dy_ref,
             dx_ref, dg_ref, dcw_ref, dcb_ref, dwa_ref, dba_ref, dwx_ref, dbx_ref, dlam_ref, a_s, b_s, h_s, dh_s):
        @pl.when(pl.program_id(1) == 0)
        def _():
            for ref in (dcw_ref, dcb_ref, dwa_ref, dba_ref, dwx_ref, dbx_ref, dlam_ref):
                ref[...] = jnp.zeros_like(ref)

        x = x_ref[...]
        cwv = cw_ref[...]
        wav, wxv, lam_ = wa_ref[0], wx_ref[0], lam_ref[...]
        lx, lxb, r, i, spl, a, mult, valid = _lru_forward_block(
            x, cwv, cb_ref[...], wav, ba_ref[...], wxv, bx_ref[...], lam_, a_s, b_s, h_s)
        gate = g_ref[...]
        sg = _sigmoid(gate)
        dy = dy_ref[...]
        h = h_s[...]
        dg_ref[...] = (dy * h * sg * (1.0 + gate * (1.0 - sg))).astype(BF16)
        b_s[...] = dy * gate * sg
        a_s[...] = jnp.where(_iota((lp, 1), 0) < lp - 1, _shift_up(a, 1), 0.0)
        _block_scan(a_s, b_s, dh_s, reverse=True)
        dh = dh_s[...]
        hprev = jnp.where(_iota((lp, 1), 0) >= 1, _shift_down(h, 1), 0.0)
        db = jnp.where(valid, dh, 0.0)
        dmult = db * i * lx
        di = db * mult * lx
        dlx = db * mult * i
        inv_mult = jnp.where(mult > 0.0, 1.0 / mult, 0.0)
        dlog_a = dh * hprev * a - dmult * (a * a) * inv_mult
        drp = dlog_a * ((-RG_LRU_C) * spl) * r * (1.0 - r)
        dip = di * i * (1.0 - i)
        dspl = jnp.sum(dlog_a * ((-RG_LRU_C) * r), axis=0, keepdims=True)
        dlam_ref[...] += dspl * (-_sigmoid(-lam_))
        dba_ref[...] += jnp.sum(drp, axis=0, keepdims=True)
        dbx_ref[...] += jnp.sum(dip, axis=0, keepdims=True)
        drpb, dipb = drp.astype(BF16), dip.astype(BF16)
        dwa_ref[0] += _dot_tn(lxb, drpb)
        dwx_ref[0] += _dot_tn(lxb, dipb)
        dlx = dlx + _dot_nt(drpb, wav) + _dot_nt(dipb, wxv)
        dcb_ref[...] += jnp.sum(dlx, axis=0, keepdims=True)
        _conv_bwd_w(dcw_ref, dlx, x)
        dx_ref[...] = _conv_bwd_x(dlx, cwv).astype(BF16)

    t = bsz * lp
    vec = pl.BlockSpec((1, w), lambda g, b: (0, g))
    mat = pl.BlockSpec((1, w, w), lambda g, b: (g, 0, 0))
    act = pl.BlockSpec((lp, w), lambda g, b: (b, g))
    return pl.pallas_call(
        body, grid=(LRU_BLOCKS, bsz), in_specs=_lru_specs(lp) + [act],
        out_specs=[act, act, pl.BlockSpec((CONV_W, w), lambda g, b: (0, g)), vec, mat, vec, mat, vec, vec],
        out_shape=[jax.ShapeDtypeStruct((t, 1024), BF16), jax.ShapeDtypeStruct((t, 1024), BF16),
                   jax.ShapeDtypeStruct((CONV_W, 1024), F32), jax.ShapeDtypeStruct((1, 1024), F32),
                   jax.ShapeDtypeStruct((LRU_BLOCKS, w, w), F32), jax.ShapeDtypeStruct((1, 1024), F32),
                   jax.ShapeDtypeStruct((LRU_BLOCKS, w, w), F32), jax.ShapeDtypeStruct((1, 1024), F32),
                   jax.ShapeDtypeStruct((1, 1024), F32)],
        scratch_shapes=[pltpu.VMEM((lp, w), F32)] * 4,
        compiler_params=_cparams("parallel", "arbitrary"), name="lru_bwd")(
            proj, proj, cw, cb, wa, ba, wx, bx, lam, dmixed)


def _sb_masks():
    j = _iota((SB_T, 2 * SB_T), 0)
    s = _iota((SB_T, 2 * SB_T), 1)
    right = ((s < SB_T) & (j > s)) | (s >= SB_T)
    left = ((s < SB_T) & (j < s)) | (s >= SB_T)
    return jnp.where(right, -1.0, 0.0).astype(BF16), left.astype(BF16)


def _xdot2(a, m01):
    a1 = a.astype(BF16)
    a2 = (a - a1.astype(F32)).astype(BF16)
    return _dot(a1, m01) + _dot(a2, m01)


def _sb_query_block(lp):
    for tq in (512, 256, 128):
        if (lp - SB_T) % tq == 0:
            return tq, (lp - SB_T) // tq
    raise ValueError(lp)


def _sb_key_rows(kj):
    return pl.ds(kj * SB_T if isinstance(kj, int) else pl.multiple_of(kj * SB_T, SB_T), SB_T)


def _sb_valid(q0, tq, kj):
    t_pos = q0 + _iota((tq, SB_T), 0)
    s_pos = kj * SB_T + _iota((tq, SB_T), 1)
    return (s_pos < t_pos) & (s_pos >= PAD)


def _sb_by_head(ref, rows):
    t = ref[rows, :]
    head0 = _iota((SB_T, 128), 1) < SB_D
    return jnp.concatenate([jnp.where(head0, t, 0.0), jnp.where(head0, 0.0, t)], axis=0).astype(BF16)


def _sb_tile(qb, k_ref, q0, kj, carries, m_right_neg, masked):
    rows_k = _sb_key_rows(kj)
    z2 = _dot_nt(qb, _sb_by_head(k_ref, rows_k))
    valid = _sb_valid(q0, qb.shape[0], kj) if masked else None
    out = []
    for h in range(2):
        z = z2[:, h * SB_T:(h + 1) * SB_T]
        sp = _softplus(z)
        rs = _dot((jnp.where(valid, sp, 0.0) if masked else sp).astype(BF16), m_right_neg)
        lb = z - sp
        wgt = jnp.exp(lb + rs[:, :SB_T] + carries[h])
        if masked:
            wgt = jnp.where(valid, wgt, 0.0)
        out.append((lb, wgt, carries[h] + rs[:, SB_T:]))
    return rows_k, valid, out


def _tail(x, row0):
    return x if row0 == 0 else x[row0:]


def _merge(old, new_tail, row0):
    return new_tail if row0 == 0 else jnp.concatenate([old[:row0], new_tail], axis=0)


def _sb_sweep(step, c, blk, tq, first, leftwards):
    if first:
        return step(0, True, 0)(c)
    r = tq // SB_T
    u = 4 if r % 4 == 0 else 1
    lo = 1 + r * blk

    def diag(c):
        for d in (range(r - 1, -1, -1) if leftwards else range(r)):
            c = step(lo + d, True, SB_T * d)(c)
        return c

    def inner(c):
        def trip(i, c):
            for j in range(u):
                d = u * i + j
                c = step(lo - 1 - d if leftwards else 1 + d, False, 0)(c)
            return c
        return lax.fori_loop(0, (r // u) * blk, trip, c)

    if leftwards:
        return step(0, True, 0)(inner(diag(c)))
    return diag(inner(step(0, True, 0)(c)))


def _sb_specs(lp):
    nh = SB_HEADS // 2
    return [pl.BlockSpec((lp, 128), lambda b, p: (b, p)), pl.BlockSpec((lp, 128), lambda b, p: (b, nh + p)),
            pl.BlockSpec((lp, 128), lambda b, p: (b, 2 * nh + p)), pl.BlockSpec((lp, 128), lambda b, p: (b, 3 * nh + p))]


def _sb_fwd(qkvg, bsz, lp):
    tq, nb = _sb_query_block(lp)
    scale = SB_D ** -0.5

    def body(q_ref, k_ref, v_ref, g_ref, o_ref, og_ref):
        m_right_neg, _ = _sb_masks()

        def q_block(blk, rows, first):
            q0 = 0 if first else (1 + (rows // SB_T) * blk) * SB_T
            rows_q = pl.ds(q0 if first else pl.multiple_of(q0, SB_T), rows)
            qb = (q_ref[rows_q, :] * scale).astype(BF16)

            def step(kj, masked, row0):
                def run(c):
                    acc, car0, car1 = c
                    rows_k, _, ((_, w0, new0), (_, w1, new1)) = _sb_tile(
                        _tail(qb, row0), k_ref, q0 + row0, kj, (_tail(car0, row0), _tail(car1, row0)), m_right_neg, masked)
                    w2 = jnp.concatenate([w0.astype(BF16), w1.astype(BF16)], axis=1)
                    upd = _tail(acc, row0) + _dot(w2, _sb_by_head(v_ref, rows_k))
                    return _merge(acc, upd, row0), _merge(car0, new0, row0), _merge(car1, new1, row0)
                return run

            zero = jnp.zeros((rows, SB_T), F32)
            o_ref[rows_q, :] = _sb_sweep(step, (zero, zero, zero), blk, rows, first, True)[0]

        q_block(0, SB_T, True)

        def big_block(i, _):
            q_block(i, tq, False)
            return 0

        lax.fori_loop(0, nb, big_block, 0)
        gate = g_ref[...]
        og_ref[...] = (o_ref[...] * gate * _sigmoid(gate)).astype(BF16)

    t = bsz * lp
    blk = pl.BlockSpec((lp, 128), lambda b, p: (b, p))
    return pl.pallas_call(
        body, grid=(bsz, SB_HEADS // 2), in_specs=_sb_specs(lp), out_specs=[blk, blk],
        out_shape=[jax.ShapeDtypeStruct((t, 1024), F32), jax.ShapeDtypeStruct((t, 1024), BF16)],
        compiler_params=_cparams("parallel", "parallel"), name="sb_fwd")(qkvg, qkvg, qkvg, qkvg)


def _sb_bwd(qkvg, o, dog, bsz, lp):
    tq, nb = _sb_query_block(lp)
    nk = lp // SB_T
    scale = SB_D ** -0.5

    def body(q_ref, k_ref, v_ref, g_ref, o_ref, dog_ref, dq_ref, dk_ref, dv_ref, dg_ref, do_s, dk_s, dv_s, e_s, sig_s):
        m_right_neg, m_left = _sb_masks()
        gate = g_ref[...]
        sg = _sigmoid(gate)
        dog = dog_ref[...]
        do_s[...] = dog * gate * sg
        dg_ref[...] = (dog * o_ref[...] * sg * (1.0 + gate * (1.0 - sg))).astype(BF16)
        dk_s[...] = jnp.zeros_like(dk_s)
        dv_s[...] = jnp.zeros_like(dv_s)
        head0_rows = _iota((128, SB_T), 0) < SB_D

        def add_transposed(acc_ref, kj, lhs_t, both):
            res = _dot(lhs_t, both)
            acc_ref[kj] += jnp.where(head0_rows, res[:, :SB_T], res[:, SB_T:])

        def q_block(blk, rows, first):
            q0 = 0 if first else (1 + (rows // SB_T) * blk) * SB_T
            rows_q = pl.ds(q0 if first else pl.multiple_of(q0, SB_T), rows)
            qf = q_ref[rows_q, :] * scale
            qb = qf.astype(BF16)
            q_t = qf.T.astype(BF16)
            do_f = do_s[rows_q, :]
            do_b = do_f.astype(BF16)
            do_t = do_f.T.astype(BF16)

            def left(kj, masked, row0):
                def run(c):
                    rows_k, _, heads = _sb_tile(_tail(qb, row0), k_ref, q0 + row0, kj, (_tail(c[0], row0), _tail(c[1], row0)),
                                                m_right_neg, masked)
                    dw2 = _dot_nt(_tail(do_b, row0), _sb_by_head(v_ref, rows_k))
                    for h, (lb, wgt, _) in enumerate(heads):
                        e_s[2 * kj + h, row0:rows, :] = wgt * dw2[:, h * SB_T:(h + 1) * SB_T]
                        sig_s[2 * kj + h, row0:rows, :] = jnp.exp(lb)
                    add_transposed(dv_s, kj, do_t[:, row0:], jnp.concatenate([hd[1].astype(BF16) for hd in heads], axis=1))
                    return _merge(c[0], heads[0][2], row0), _merge(c[1], heads[1][2], row0)
                return run

            zero = jnp.zeros((rows, SB_T), F32)
            _sb_sweep(left, (zero, zero), blk, rows, first, True)

            def right(kj, masked, row0):
                def run(c):
                    dq, ecar = c[0], list(c[1:])
                    rows_k = _sb_key_rows(kj)
                    valid = _sb_valid(q0 + row0, rows - row0, kj) if masked else None
                    dzs = []
                    for h in range(2):
                        e = e_s[2 * kj + h, row0:rows, :]
                        se = _dot(e.astype(BF16), m_left)
                        ec = _tail(ecar[h], row0)
                        dz = e - sig_s[2 * kj + h, row0:rows, :] * (e + se[:, :SB_T] + ec)
                        if masked:
                            dz = jnp.where(valid, dz, 0.0)
                        dzs.append(dz.astype(BF16))
                        ecar[h] = _merge(ecar[h], ec + se[:, SB_T:], row0)
                    dz2 = jnp.concatenate(dzs, axis=1)
                    add_transposed(dk_s, kj, q_t[:, row0:], dz2)
                    upd = _tail(dq, row0) + _dot(dz2, _sb_by_head(k_ref, rows_k))
                    return _merge(dq, upd, row0), ecar[0], ecar[1]
                return run

            dq = _sb_sweep(right, (zero, zero, zero), blk, rows, first, False)[0]
            dq_ref[rows_q, :] = (dq * scale).astype(BF16)

        q_block(0, SB_T, True)

        def big_block(i, _):
            q_block(i, tq, False)
            return 0

        lax.fori_loop(0, nb, big_block, 0)
        for kj in range(nk):
            dk_ref[kj * SB_T:(kj + 1) * SB_T, :] = dk_s[kj].T.astype(BF16)
            dv_ref[kj * SB_T:(kj + 1) * SB_T, :] = dv_s[kj].T.astype(BF16)

    t = bsz * lp
    blk = pl.BlockSpec((lp, 128), lambda b, p: (b, p))
    shp = jax.ShapeDtypeStruct((t, 1024), BF16)
    return pl.pallas_call(
        body, grid=(bsz, SB_HEADS // 2), in_specs=_sb_specs(lp) + [blk, blk], out_specs=[blk] * 4,
        out_shape=[shp] * 4,
        scratch_shapes=[pltpu.VMEM((lp, 128), F32)] + [pltpu.VMEM((nk, 128, SB_T), F32)] * 2
        + [pltpu.VMEM((2 * nk, tq, SB_T), F32)] * 2,
        compiler_params=_cparams("parallel", "parallel"), name="sb_bwd")(qkvg, qkvg, qkvg, qkvg, o, dog)


XBC_COL0 = 3072 // 256
DT_COL0 = 4608 // 128
DT_L = 128


def _ssd_pre_fwd(proj, cw, cb, bsz, lp):
    def body(x_ref, cw_ref, cb_ref, o_ref):
        pre = _conv(x_ref[...], cw_ref[...], cb_ref[...])
        o_ref[...] = pre * _sigmoid(pre)

    return pl.pallas_call(
        body, grid=(bsz, SSD_CONV_DIM // 256),
        in_specs=[pl.BlockSpec((lp, 256), lambda b, j: (b, XBC_COL0 + j)), pl.BlockSpec((CONV_W, 256), lambda b, j: (0, j)),
                  pl.BlockSpec((1, 256), lambda b, j: (0, j))],
        out_specs=pl.BlockSpec((lp, 256), lambda b, j: (b, j)),
        out_shape=jax.ShapeDtypeStruct((bsz * lp, SSD_CONV_DIM), F32),
        compiler_params=_cparams("parallel", "parallel"), name="ssd_pre_fwd")(proj, cw, cb)


def _ssd_pre_bwd(proj, dxs, dbm, dcm, cw, cb, bsz, lp):
    nx = 1024 // 256

    def body(x_ref, dxs_ref, db_ref, dc_ref, cw_ref, cb_ref, dx_ref, dcw_ref, dcb_ref):
        @pl.when(pl.program_id(1) == 0)
        def _():
            dcw_ref[...] = jnp.zeros_like(dcw_ref)
            dcb_ref[...] = jnp.zeros_like(dcb_ref)
        j = pl.program_id(0)
        x = x_ref[...]
        cwv = cw_ref[...]
        pre = _conv(x, cwv, cb_ref[...])
        s = _sigmoid(pre)
        d = jnp.where(j < nx, dxs_ref[...], jnp.where(j == nx, db_ref[...], dc_ref[...]))
        dpre = d.astype(F32) * s * (1.0 + pre * (1.0 - s))
        dcb_ref[...] += jnp.sum(dpre, axis=0, keepdims=True)
        _conv_bwd_w(dcw_ref, dpre, x)
        dx_ref[...] = _conv_bwd_x(dpre, cwv).astype(BF16)

    return pl.pallas_call(
        body, grid=(SSD_CONV_DIM // 256, bsz),
        in_specs=[pl.BlockSpec((lp, 256), lambda j, b: (b, XBC_COL0 + j)),
                  pl.BlockSpec((lp, 256), lambda j, b: (b, jnp.minimum(j, nx - 1))),
                  pl.BlockSpec((lp, 256), lambda j, b: (b, 0)), pl.BlockSpec((lp, 256), lambda j, b: (b, 0)),
                  pl.BlockSpec((CONV_W, 256), lambda j, b: (0, j)), pl.BlockSpec((1, 256), lambda j, b: (0, j))],
        out_specs=[pl.BlockSpec((lp, 256), lambda j, b: (b, j)), pl.BlockSpec((CONV_W, 256), lambda j, b: (0, j)),
                   pl.BlockSpec((1, 256), lambda j, b: (0, j))],
        out_shape=[jax.ShapeDtypeStruct((bsz * lp, SSD_CONV_DIM), BF16), jax.ShapeDtypeStruct((CONV_W, SSD_CONV_DIM), F32),
                   jax.ShapeDtypeStruct((1, SSD_CONV_DIM), F32)],
        compiler_params=_cparams("parallel", "arbitrary"), name="ssd_pre_bwd")(proj, dxs, dbm, dcm, cw, cb)


def _split2(a):
    a1 = a.astype(BF16)
    return a1, (a - a1.astype(F32)).astype(BF16)


def _xdot2_nt(a, m01):
    a1, a2 = _split2(a)
    return _dot_nt(a1, m01) + _dot_nt(a2, m01)


def _xdot2_l(m01, a):
    a1, a2 = _split2(a)
    return _dot(m01, a1) + _dot(m01, a2)


class _SsdConsts:
    def __init__(self, g):
        q, gw = SSD_Q, SSD_GW
        head_of_lane = lax.shift_right_logical(_iota((DT_L, gw), 1), 6)
        self.sel = (_iota((DT_L, gw), 0) == 8 * g + head_of_lane).astype(BF16)
        r = _iota((q, gw), 0)
        c = jnp.bitwise_and(_iota((q, gw), 1), q - 1)
        self.diag = r == c
        self.diag_b = self.diag.astype(BF16)
        self.lower = c <= r
        self.upper = c >= r
        self.bd = lax.shift_right_logical(_iota((128, 128), 0), 6) == lax.shift_right_logical(_iota((128, 128), 1), 6)
        jj, ll = _iota((q, q), 1), _iota((q, q), 0)
        self.tri = (jj <= ll).astype(BF16)
        self.tri_t = (jj >= ll).astype(BF16)
        self.last = _iota((q, 1), 0) == q - 1


def _ssd_prepass(k, dt_ref, bias_ref, alog_ref, dtm_s, abc_s, lp):
    valid = _iota((lp, 1), 0) >= PAD
    sp_in = dt_ref[...] + bias_ref[...]
    dtm = jnp.where(valid, _softplus(sp_in), 0.0)
    aneg = -jnp.exp(alog_ref[...])
    dtm_s[...] = dtm
    abc_s[...] = _xdot_r(dtm * aneg, k.sel)
    return valid, sp_in, aneg


def _ssd_loop(nc, chunk, init):
    u = 2 if nc % 2 == 0 else 1

    def trip(i, carry):
        for j in range(u):
            carry = chunk(i * u + j, carry)
        return carry

    return lax.fori_loop(0, nc // u, trip, init)


def _ssd_chunk(k, rows, xs_ref, b_ref, c_ref, dtm_s, abc_s):
    bc = _xdot_l(k.tri, abc_s[rows, :])
    tt = jnp.sum(jnp.where(k.diag, bc, 0.0), axis=0, keepdims=True)
    dtbc = _xdot2(dtm_s[rows, :], k.sel)
    xs = xs_ref[rows, :]
    x = xs * dtbc
    bb = b_ref[rows, :].astype(BF16)
    cc = c_ref[rows, :].astype(BF16)
    tot = bc[SSD_Q - 1:SSD_Q, :]
    return bc, tt, dtbc, xs, x, bb, cc, tot, _pair_blocks(k, x)


def _pair_blocks(k, x):
    out = []
    for p in range(SSD_GW // 128):
        xp = x[:, 128 * p:128 * (p + 1)]
        out.append(jnp.where(k.bd, jnp.concatenate([xp, xp], axis=0), 0.0).astype(BF16))
    return out


def _pair_dot(a, blocks, transposed=False):
    dot = _dot_nt if transposed else _dot
    return jnp.concatenate([dot(a[:, 128 * p:128 * (p + 1)], blk) for p, blk in enumerate(blocks)], axis=1)


def _ssd_specs(lp, order):
    ix = (lambda f: (lambda b, g: f(b, g))) if order == "bg" else (lambda f: (lambda g, b: f(b, g)))
    return [pl.BlockSpec((lp, SSD_GW), ix(lambda b, g: (b, g))),
            pl.BlockSpec((lp, SSD_N), ix(lambda b, g: (b, 1024 // SSD_N + g))),
            pl.BlockSpec((lp, SSD_N), ix(lambda b, g: (b, 1280 // SSD_N + g))),
            pl.BlockSpec((lp, DT_L), ix(lambda b, g: (b, DT_COL0))),
            pl.BlockSpec((1, DT_L), ix(lambda b, g: (0, 0))),
            pl.BlockSpec((1, DT_L), ix(lambda b, g: (0, 0))),
            pl.BlockSpec((1, DT_L), ix(lambda b, g: (0, 0)))]


def _ssd_fwd(xbc, proj, dt_bias, a_log, dskip, bsz, lp, shards=()):
    nc = lp // SSD_Q
    ns = len(shards)
    steps = bsz * SSD_GROUPS

    def body(xs_ref, b_ref, c_ref, dt_ref, bias_ref, alog_ref, dsk_ref, *rest):
        y_ref = rest[ns]
        dtm_s, abc_s = rest[2 * ns + 1:2 * ns + 3]
        if ns:
            start, forward, finish = _gather_phases(rest[:ns], rest[ns + 1:2 * ns + 1], *rest[2 * ns + 3:])
            step = pl.program_id(0) * SSD_GROUPS + pl.program_id(1)
            pl.when(step == 0)(start)
            pl.when(step == steps // 2)(forward)
        k = _SsdConsts(pl.program_id(1))
        _ssd_prepass(k, dt_ref, bias_ref, alog_ref, dtm_s, abc_s, lp)
        dvec = _xdot_r(jnp.broadcast_to(dsk_ref[...], (8, DT_L)), k.sel)[0:1, :]

        def chunk(c, state):
            rows = pl.ds(pl.multiple_of(c * SSD_Q, SSD_Q), SSD_Q)
            bc, tt, _, xs, x, bb, cc, tot, xbd = _ssd_chunk(k, rows, xs_ref, b_ref, c_ref, dtm_s, abc_s)
            lm = jnp.where(k.lower, jnp.exp(jnp.minimum(bc - tt, 0.0)), 0.0)
            g_all = _dot(_dot_nt(cc, bb).astype(BF16), k.diag_b) * lm
            y = _pair_dot(g_all.astype(BF16), xbd) + jnp.exp(bc) * _dot(cc, state.astype(BF16)) + dvec * xs
            y_ref[rows, :] = y
            return jnp.exp(tot) * state + _dot_tn(bb, (jnp.exp(tot - bc) * x).astype(BF16))

        _ssd_loop(nc, chunk, jnp.zeros((SSD_N, SSD_GW), F32))
        if ns:
            pl.when(step == steps - 1)(finish)

    sems = [pltpu.SemaphoreType.DMA((ns, 7)), pltpu.SemaphoreType.DMA((ns, 7))] if ns else []
    outs = pl.pallas_call(
        body, grid=(bsz, SSD_GROUPS), in_specs=_ssd_specs(lp, "bg") + [ANY_SPEC] * ns,
        out_specs=[pl.BlockSpec((lp, SSD_GW), lambda b, g: (b, g))] + [ANY_SPEC] * ns,
        out_shape=[jax.ShapeDtypeStruct((bsz * lp, 1024), F32)] + [jax.ShapeDtypeStruct((N_DEV,) + s.shape, s.dtype) for s in shards],
        scratch_shapes=[pltpu.VMEM((lp, DT_L), F32), pltpu.VMEM((lp, SSD_GW), F32)] + sems,
        compiler_params=pltpu.CompilerParams(dimension_semantics=("arbitrary", "arbitrary"), vmem_limit_bytes=VMEM_LIMIT,
                                             has_side_effects=bool(ns)), name="ssd_fwd")(
            xbc, xbc, xbc, proj, dt_bias, a_log, dskip, *shards)
    return outs[0], (_own_blocks(outs[1:], shards) if ns else [])


def _ssd_bwd(xbc, proj, dt_bias, a_log, dskip, dy, bsz, lp):
    nc = lp // SSD_Q

    def body(xs_ref, b_ref, c_ref, dt_ref, bias_ref, alog_ref, dsk_ref, dy_ref,
             dxs_ref, db_ref, dc_ref, ddt_ref, dbias_ref, dalog_ref, ddsk_ref, dtm_s, abc_s, st_s):
        @pl.when(pl.program_id(1) == 0)
        def _():
            for ref in (dbias_ref, dalog_ref, ddsk_ref):
                ref[...] = jnp.zeros_like(ref)

        k = _SsdConsts(pl.program_id(0))
        valid, sp_in, aneg = _ssd_prepass(k, dt_ref, bias_ref, alog_ref, dtm_s, abc_s, lp)
        dvec = _xdot_r(jnp.broadcast_to(dsk_ref[...], (8, DT_L)), k.sel)[0:1, :]

        def fwd_chunk(c, state):
            rows = pl.ds(pl.multiple_of(c * SSD_Q, SSD_Q), SSD_Q)
            st_s[c] = state.astype(BF16)
            bc, _, _, _, x, bb, _, tot, _ = _ssd_chunk(k, rows, xs_ref, b_ref, c_ref, dtm_s, abc_s)
            return jnp.exp(tot) * state + _dot_tn(bb, (jnp.exp(tot - bc) * x).astype(BF16))

        _ssd_loop(nc, fwd_chunk, jnp.zeros((SSD_N, SSD_GW), F32))

        def bwd_chunk(i, carry):
            dstate, daneg, ddsk = carry
            c = nc - 1 - i
            rows = pl.ds(pl.multiple_of(c * SSD_Q, SSD_Q), SSD_Q)
            bc, tt, dtbc, xs, x, bb, cc, tot, xbd = _ssd_chunk(k, rows, xs_ref, b_ref, c_ref, dtm_s, abc_s)
            sprev = st_s[c]
            dyc = dy_ref[rows, :]
            dyb = dyc.astype(BF16)
            seg = bc - tt
            lm = jnp.where(k.lower, jnp.exp(jnp.minimum(seg, 0.0)), 0.0)
            lm_t = jnp.where(k.upper, jnp.exp(jnp.minimum(-seg, 0.0)), 0.0)
            cb_all = _dot(_dot_nt(cc, bb).astype(BF16), k.diag_b)
            cbt_all = _dot(_dot_nt(bb, cc).astype(BF16), k.diag_b)
            g_all = cb_all * lm
            dg = _pair_dot(dyb, xbd, transposed=True)
            dx = _pair_dot((cbt_all * lm_t).astype(BF16), _pair_blocks(k, dyc))
            hh = dg * g_all
            ea = jnp.exp(bc)
            yo = ea * _dot(cc, sprev)
            col_h = jnp.sum(hh, axis=0, keepdims=True)
            dcb = _dot_nt((dg * lm).astype(BF16), k.diag_b).astype(BF16)
            dcs = (ea * dyc).astype(BF16)
            dstb = dstate.astype(BF16)
            dec = jnp.exp(tot - bc)
            w = dec * x
            dw = _dot(bb, dstb)
            dc_ref[rows, :] = (_dot(dcb, bb) + _dot_nt(dcs, sprev)).astype(BF16)
            db_ref[rows, :] = (_dot_tn(dcb, cc) + _dot_nt(w.astype(BF16), dstb)).astype(BF16)
            dx = dx + dec * dw
            etot = jnp.exp(tot)
            r8 = _iota((8, SSD_GW), 0)
            st_row = jnp.sum(dstate * sprev.astype(F32), axis=0, keepdims=True) * etot
            sk_row = jnp.sum(dyc * xs, axis=0, keepdims=True)
            small = jnp.where(r8 == 0, st_row, jnp.where(r8 == 1, sk_row, 0.0))
            q = SSD_Q
            sums = _xdot2_nt(jnp.concatenate([hh + dyc * yo - jnp.where(k.diag, col_h, 0.0) - dw * w, dw * w, dx * xs, small],
                                             axis=0), k.sel)
            kk = sums[q:2 * q]
            dtot = sums[3 * q:3 * q + 1] + jnp.sum(kk, axis=0, keepdims=True)
            dacum = sums[0:q] + jnp.where(k.last, dtot, 0.0)
            da = _xdot2_l(k.tri_t, dacum)
            dtm_c = dtm_s[rows, :]
            ddtm = da * aneg + sums[2 * q:3 * q]
            vrow = (c * SSD_Q + _iota((SSD_Q, 1), 0)) >= PAD
            ddt_ref[rows, :] = jnp.where(vrow, ddtm * _sigmoid(dt_ref[rows, :] + bias_ref[...]), 0.0)
            dxs_ref[rows, :] = (dx * dtbc + dvec * dyc).astype(BF16)
            daneg = daneg + jnp.sum(da * dtm_c, axis=0, keepdims=True)
            ddsk = ddsk + sums[3 * q + 1:3 * q + 2]
            dstate = etot * dstate + _dot_tn(cc, dcs)
            return dstate, daneg, ddsk

        zrow = jnp.zeros((1, DT_L), F32)
        _, daneg, ddsk = _ssd_loop(nc, bwd_chunk, (jnp.zeros((SSD_N, SSD_GW), F32), zrow, zrow))
        dbias_ref[...] += jnp.broadcast_to(jnp.sum(ddt_ref[...], axis=0, keepdims=True), (8, DT_L))
        dalog_ref[...] += jnp.broadcast_to(daneg * aneg, (8, DT_L))
        ddsk_ref[...] += jnp.broadcast_to(ddsk, (8, DT_L))

    t = bsz * lp
    par = pl.BlockSpec((8, DT_L), lambda g, b: (g, 0))
    par_shape = jax.ShapeDtypeStruct((8 * SSD_GROUPS, DT_L), F32)
    return pl.pallas_call(
        body, grid=(SSD_GROUPS, bsz),
        in_specs=_ssd_specs(lp, "gb") + [pl.BlockSpec((lp, SSD_GW), lambda g, b: (b, g))],
        out_specs=[pl.BlockSpec((lp, SSD_GW), lambda g, b: (b, g)), pl.BlockSpec((lp, SSD_N), lambda g, b: (b, g)),
                   pl.BlockSpec((lp, SSD_N), lambda g, b: (b, g)), pl.BlockSpec((lp, DT_L), lambda g, b: (b, g)), par, par, par],
        out_shape=[jax.ShapeDtypeStruct((t, 1024), BF16), jax.ShapeDtypeStruct((t, 256), BF16),
                   jax.ShapeDtypeStruct((t, 256), BF16), jax.ShapeDtypeStruct((t, SSD_GROUPS * DT_L), F32),
                   par_shape, par_shape, par_shape],
        scratch_shapes=[pltpu.VMEM((lp, DT_L), F32), pltpu.VMEM((lp, SSD_GW), F32), pltpu.VMEM((nc, SSD_N, SSD_GW), BF16)],
        compiler_params=_cparams("parallel", "arbitrary"), name="ssd_bwd")(
            xbc, xbc, xbc, proj, dt_bias, a_log, dskip, dy)


Z_COL0 = 2048 // SSD_GW


def _gnorm_fwd(y, proj, w, mixed, name="gnorm_fwd"):
    t = y.shape[0]
    tm = _row_tile(t)

    def body(y_ref, z_ref, w_ref, mixed_ref, o_ref):
        z = z_ref[...]
        gt = y_ref[...] * z * _sigmoid(z)
        r = lax.rsqrt(jnp.mean(gt * gt, axis=-1, keepdims=True) + EPS)
        o_ref[...] = (gt * r * w_ref[...]).astype(BF16)

    return pl.pallas_call(
        body, grid=(t // tm, SSD_GROUPS),
        in_specs=[pl.BlockSpec((tm, SSD_GW), lambda i, g: (i, g)), pl.BlockSpec((tm, SSD_GW), lambda i, g: (i, Z_COL0 + g)),
                  pl.BlockSpec((1, SSD_GW), lambda i, g: (0, g)), ANY_SPEC],
        out_specs=pl.BlockSpec((tm, SSD_GW), lambda i, g: (i, 1024 // SSD_GW + g)),
        out_shape=jax.ShapeDtypeStruct((t, 2048), BF16), input_output_aliases={3: 0},
        compiler_params=_cparams("parallel", "parallel"), name=name)(y, proj, w, mixed)


def _gnorm_bwd(y, proj, w, dmixed):
    t = y.shape[0]
    tm = _row_tile(t)

    def body(y_ref, z_ref, w_ref, d_ref, dy_ref, dz_ref, dw_ref):
        @pl.when(pl.program_id(1) == 0)
        def _():
            dw_ref[...] = jnp.zeros_like(dw_ref)
        z, yv, d = z_ref[...], y_ref[...], d_ref[...]
        s = _sigmoid(z)
        sz = z * s
        gt = yv * sz
        r = lax.rsqrt(jnp.mean(gt * gt, axis=-1, keepdims=True) + EPS)
        gh = gt * r
        dgn = d * w_ref[...]
        dgt = r * (dgn - gh * jnp.mean(dgn * gh, axis=-1, keepdims=True))
        dw_ref[...] += jnp.sum(d * gh, axis=0, keepdims=True)
        dy_ref[...] = dgt * sz
        dz_ref[...] = (dgt * yv * s * (1.0 + z * (1.0 - s))).astype(BF16)

    blk = pl.BlockSpec((tm, SSD_GW), lambda g, i: (i, g))
    return pl.pallas_call(
        body, grid=(SSD_GROUPS, t // tm),
        in_specs=[blk, pl.BlockSpec((tm, SSD_GW), lambda g, i: (i, Z_COL0 + g)), pl.BlockSpec((1, SSD_GW), lambda g, i: (0, g)),
                  pl.BlockSpec((tm, SSD_GW), lambda g, i: (i, 1024 // SSD_GW + g))],
        out_specs=[blk, blk, pl.BlockSpec((1, SSD_GW), lambda g, i: (0, g))],
        out_shape=[jax.ShapeDtypeStruct((t, 1024), F32), jax.ShapeDtypeStruct((t, 1024), BF16),
                   jax.ShapeDtypeStruct((1, 1024), F32)],
        compiler_params=_cparams("parallel", "arbitrary"), name="gnorm_bwd")(y, proj, w, dmixed)


def _meta_grad(dh0, bsz, lp):
    def body(d_ref, o_ref):
        @pl.when(pl.program_id(0) == 0)
        def _():
            o_ref[...] = jnp.zeros_like(o_ref)
        o_ref[...] += d_ref[...]

    return pl.pallas_call(
        body, grid=(bsz,),
        in_specs=[pl.BlockSpec((N_META, D_MODEL), lambda b: (b * (lp // N_META) + PAD // N_META, 0))],
        out_specs=pl.BlockSpec((N_META, D_MODEL), lambda b: (0, 0)),
        out_shape=jax.ShapeDtypeStruct((N_META, D_MODEL), F32),
        compiler_params=_cparams("arbitrary"), name="meta_grad")(dh0)


OTHER_CHIPS = ((1, 0), (0, 1), (1, 1))
ANY_SPEC = pl.BlockSpec(memory_space=pl.ANY)


class _Mesh:
    def __init__(self):
        self.x, self.y, self.c = lax.axis_index("x"), lax.axis_index("y"), lax.axis_index("c")

    def dev(self, fx, fy, fc):
        return (1 - self.x if fx else self.x, 1 - self.y if fy else self.y, 1 - self.c if fc else self.c)

    def slot(self, fx, fy, fc):
        px, py, pc = self.dev(fx, fy, fc)
        return 4 * px + 2 * py + pc

    def chip(self, fx, fy):
        px, py, _ = self.dev(fx, fy, 0)
        return 2 * px + py


def _remote(src, dst, send_sems, recv_sems, idx, dev):
    return pltpu.make_async_remote_copy(src_ref=src, dst_ref=dst, send_sem=send_sems.at[idx], recv_sem=recv_sems.at[idx],
                                        device_id=dev, device_id_type=pl.DeviceIdType.MESH)


def _gather_phases(ins, outs, send_sems, recv_sems):
    n = len(ins)
    me = _Mesh()

    def first():
        cps = []
        for a in range(n):
            mine = outs[a].at[me.slot(0, 0, 0)]
            cps.append(_remote(ins[a], mine, send_sems, recv_sems, (a, 0), me.dev(0, 0, 1)))
            for j, (fx, fy) in enumerate(OTHER_CHIPS):
                cps.append(_remote(ins[a], mine, send_sems, recv_sems, (a, 1 + j), me.dev(fx, fy, 0)))
        return cps

    def passed():
        return [_remote(outs[a].at[me.slot(fx, fy, 0)], outs[a].at[me.slot(fx, fy, 0)], send_sems, recv_sems, (a, 4 + j),
                        me.dev(0, 0, 1)) for a in range(n) for j, (fx, fy) in enumerate(OTHER_CHIPS)]

    def start():
        for cp in first():
            cp.start()

    def forward():
        fwd = passed()
        for a in range(n):
            for j, (fx, fy) in enumerate(OTHER_CHIPS):
                _remote(ins[a], outs[a].at[me.slot(fx, fy, 0)], send_sems, recv_sems, (a, 1 + j), me.dev(fx, fy, 0)).wait_recv()
                fwd[3 * a + j].start()

    def finish():
        for a in range(n):
            _remote(ins[a], outs[a].at[me.slot(0, 0, 1)], send_sems, recv_sems, (a, 0), me.dev(0, 0, 1)).wait_recv()
            for j, (fx, fy) in enumerate(OTHER_CHIPS):
                _remote(ins[a], outs[a].at[me.slot(fx, fy, 1)], send_sems, recv_sems, (a, 4 + j), me.dev(0, 0, 1)).wait_recv()
        for cp in first() + passed():
            cp.wait_send()

    return start, forward, finish


def _own_blocks(stacks, arrs):
    me = 4 * lax.axis_index("x") + 2 * lax.axis_index("y") + lax.axis_index("c")
    return [lax.dynamic_update_slice_in_dim(s, a[None], me, axis=0) for s, a in zip(stacks, arrs)]


def _gather_all(arrs, name):
    n = len(arrs)

    def body(*refs):
        start, forward, finish = _gather_phases(refs[:n], refs[n:2 * n], *refs[2 * n:])
        start()
        forward()
        finish()

    stacks = pl.pallas_call(
        body, in_specs=[ANY_SPEC] * n, out_specs=[ANY_SPEC] * n,
        out_shape=[jax.ShapeDtypeStruct((N_DEV,) + a.shape, a.dtype) for a in arrs],
        scratch_shapes=[pltpu.SemaphoreType.DMA((n, 7)), pltpu.SemaphoreType.DMA((n, 7))],
        compiler_params=pltpu.CompilerParams(has_side_effects=True), name=name)(*arrs)
    return _own_blocks(stacks, arrs)


def _mm_nn_gather(a, b, out_dtype, name, shards):
    m, k = a.shape
    _, n = b.shape
    tm = _row_tile(m)
    steps = m // tm
    ns = len(shards)

    def body(a_ref, b_ref, *rest):
        o_ref = rest[ns]
        start, forward, finish = _gather_phases(rest[:ns], rest[ns + 1:2 * ns + 1], *rest[2 * ns + 1:])
        i = pl.program_id(0)
        pl.when(i == 0)(start)
        pl.when(i == steps // 2)(forward)
        av = a_ref[...].astype(BF16)
        for s, w in _chunks(n, 512):
            o_ref[:, s:s + w] = _dot(av, b_ref[:, s:s + w]).astype(out_dtype)
        pl.when(i == steps - 1)(finish)

    outs = pl.pallas_call(
        body, grid=(steps,),
        in_specs=[pl.BlockSpec((tm, k), lambda i: (i, 0)), pl.BlockSpec((k, n), lambda i: (0, 0))] + [ANY_SPEC] * ns,
        out_specs=[pl.BlockSpec((tm, n), lambda i: (i, 0))] + [ANY_SPEC] * ns,
        out_shape=[jax.ShapeDtypeStruct((m, n), out_dtype)] + [jax.ShapeDtypeStruct((N_DEV,) + s.shape, s.dtype) for s in shards],
        scratch_shapes=[pltpu.SemaphoreType.DMA((ns, 7)), pltpu.SemaphoreType.DMA((ns, 7))],
        compiler_params=pltpu.CompilerParams(dimension_semantics=("arbitrary",), vmem_limit_bytes=VMEM_LIMIT,
                                             has_side_effects=True), name=name)(a, b, *shards)
    return outs[0], _own_blocks(outs[1:], shards)


def _swap_in_chip(arrs, name):
    n = len(arrs)

    def body(*refs):
        ins, got = refs[:n], refs[n:2 * n]
        send_sems, recv_sems = refs[2 * n:]
        me = _Mesh()
        sends = [_remote(ins[a].at[2 * j + (1 - me.c)], got[a].at[j], send_sems, recv_sems, (a, j), me.dev(0, 0, 1))
                 for a in range(n) for j in range(4)]
        for cp in sends:
            cp.start()
        for cp in sends:
            cp.wait()

    return pl.pallas_call(
        body, in_specs=[ANY_SPEC] * n, out_specs=[ANY_SPEC] * n,
        out_shape=[jax.ShapeDtypeStruct((4,) + a.shape[1:], a.dtype) for a in arrs],
        scratch_shapes=[pltpu.SemaphoreType.DMA((n, 4)), pltpu.SemaphoreType.DMA((n, 4))],
        compiler_params=pltpu.CompilerParams(has_side_effects=True), name=name)(*arrs)


def _mm_nn_swap(a, b, out_dtype, name, by_owner):
    m, k = a.shape
    _, n = b.shape
    tm = _row_tile(m)
    steps = m // tm
    ns = len(by_owner)

    def body(a_ref, b_ref, *rest):
        o_ref = rest[ns]
        i = pl.program_id(0)

        def copies():
            me = _Mesh()
            return [_remote(rest[x].at[2 * j + (1 - me.c)], rest[ns + 1 + x].at[j], rest[2 * ns + 1], rest[2 * ns + 2], (x, j),
                            me.dev(0, 0, 1)) for x in range(ns) for j in range(4)]

        def start():
            for cp in copies():
                cp.start()

        def finish():
            for cp in copies():
                cp.wait()

        pl.when(i == 0)(start)
        av = a_ref[...].astype(BF16)
        for s, w in _chunks(n, 512):
            o_ref[:, s:s + w] = _dot(av, b_ref[:, s:s + w]).astype(out_dtype)
        pl.when(i == steps - 1)(finish)

    outs = pl.pallas_call(
        body, grid=(steps,),
        in_specs=[pl.BlockSpec((tm, k), lambda i: (i, 0)), pl.BlockSpec((k, n), lambda i: (0, 0))] + [ANY_SPEC] * ns,
        out_specs=[pl.BlockSpec((tm, n), lambda i: (i, 0))] + [ANY_SPEC] * ns,
        out_shape=[jax.ShapeDtypeStruct((m, n), out_dtype)] + [jax.ShapeDtypeStruct((4,) + x.shape[1:], x.dtype) for x in by_owner],
        scratch_shapes=[pltpu.SemaphoreType.DMA((ns, 4)), pltpu.SemaphoreType.DMA((ns, 4))],
        compiler_params=pltpu.CompilerParams(dimension_semantics=("arbitrary",), vmem_limit_bytes=VMEM_LIMIT,
                                             has_side_effects=True), name=name)(a, b, *by_owner)
    return outs[0], list(outs[1:])


def _add_pair(a, b, name):
    _, r, c = a.shape
    tr = _adam_rows(r)

    def body(a_ref, b_ref, o_ref):
        o_ref[...] = (a_ref[...].astype(F32) + b_ref[...].astype(F32)).astype(o_ref.dtype)

    blk = pl.BlockSpec((4, tr, c), lambda i: (0, i, 0))
    return pl.pallas_call(body, grid=(r // tr,), in_specs=[blk, blk], out_specs=blk,
                          out_shape=jax.ShapeDtypeStruct(a.shape, a.dtype), compiler_params=_cparams("parallel"), name=name)(a, b)


def _swap_chips(sums, gather, name):
    ns, ng = len(sums), len(gather)
    n = ns + ng

    def body(*refs):
        ins, outs = refs[:n], refs[n:2 * n]
        send_sems, recv_sems, local_sems = refs[2 * n:]
        me = _Mesh()
        sends, recvs = _chip_swap_copies(ins[:ns], outs[:ns], send_sems, recv_sems)
        local = []
        for a in range(ns, n):
            local.append(pltpu.make_async_copy(ins[a], outs[a].at[me.slot(0, 0, 0)], local_sems.at[a]))
            for k in range(1, N_DEV):
                flips = (k >> 2 & 1, k >> 1 & 1, k & 1)
                sends.append(_remote(ins[a], outs[a].at[me.slot(0, 0, 0)], send_sems, recv_sems, (a, k - 1), me.dev(*flips)))
                recvs.append(_remote(ins[a], outs[a].at[me.slot(*flips)], send_sems, recv_sems, (a, k - 1), me.dev(*flips)))
        for cp in local + sends:
            cp.start()
        for cp in recvs:
            cp.wait_recv()
        for cp in sends:
            cp.wait_send()
        for cp in local:
            cp.wait()

    out_shape = [jax.ShapeDtypeStruct(a.shape, a.dtype) for a in sums]
    out_shape += [jax.ShapeDtypeStruct((N_DEV,) + a.shape, a.dtype) for a in gather]
    outs = pl.pallas_call(
        body, in_specs=[ANY_SPEC] * n, out_specs=[ANY_SPEC] * n, out_shape=out_shape,
        scratch_shapes=[pltpu.SemaphoreType.DMA((n, N_DEV - 1)), pltpu.SemaphoreType.DMA((n, N_DEV - 1)),
                        pltpu.SemaphoreType.DMA((n,))],
        compiler_params=pltpu.CompilerParams(has_side_effects=True), name=name)(*sums, *gather)
    return _own_chip(outs[:ns], sums) + list(outs[ns:])


def _chip_swap_copies(ins, outs, send_sems, recv_sems, arrivals=True):
    me = _Mesh()
    here = me.chip(0, 0)
    sends, recvs = [], []
    for a in range(len(ins)):
        for j, (fx, fy) in enumerate(OTHER_CHIPS):
            there = me.chip(fx, fy)
            sends.append(_remote(ins[a].at[there], outs[a].at[here], send_sems, recv_sems, (a, j), me.dev(fx, fy, 0)))
            if arrivals:
                recvs.append(_remote(ins[a].at[there], outs[a].at[there], send_sems, recv_sems, (a, j), me.dev(fx, fy, 0)))
    return sends, recvs


def _own_chip(outs, sums):
    here = 2 * lax.axis_index("x") + lax.axis_index("y")
    return [lax.dynamic_update_slice_in_dim(o, lax.dynamic_slice_in_dim(s, here, 1, axis=0), here, axis=0)
            for o, s in zip(outs, sums)]


def _mm_tn_swap(a, b, name, sums):
    t, m = a.shape
    _, n = b.shape
    tk = _row_tile(t)
    halves = 2 if (m * n * 4 > 8 * 1024 * 1024 and n % 256 == 0) else 1
    tn = n // halves
    steps = t // tk
    ns = len(sums)

    def body(a_ref, b_ref, *rest):
        o_ref = rest[ns]
        j, s = pl.program_id(0), pl.program_id(1)

        def start():
            for cp in _chip_swap_copies(rest[:ns], rest[ns + 1:2 * ns + 1], *rest[2 * ns + 1:], arrivals=False)[0]:
                cp.start()

        def finish():
            sends, recvs = _chip_swap_copies(rest[:ns], rest[ns + 1:2 * ns + 1], *rest[2 * ns + 1:])
            for cp in recvs:
                cp.wait_recv()
            for cp in sends:
                cp.wait_send()

        pl.when((j == 0) & (s == 0))(start)

        @pl.when(s == 0)
        def _():
            o_ref[...] = jnp.zeros_like(o_ref)
        at = a_ref[...].astype(BF16).T
        for c0, w in _chunks(tn, 512):
            o_ref[:, c0:c0 + w] += _dot(at, b_ref[:, c0:c0 + w].astype(BF16))
        pl.when((j == halves - 1) & (s == steps - 1))(finish)

    outs = pl.pallas_call(
        body, grid=(halves, steps),
        in_specs=[pl.BlockSpec((tk, m), lambda j, s: (s, 0)), pl.BlockSpec((tk, tn), lambda j, s: (s, j))] + [ANY_SPEC] * ns,
        out_specs=[pl.BlockSpec((m, tn), lambda j, s: (0, j))] + [ANY_SPEC] * ns,
        out_shape=[jax.ShapeDtypeStruct((m, n), F32)] + [jax.ShapeDtypeStruct(x.shape, x.dtype) for x in sums],
        scratch_shapes=[pltpu.SemaphoreType.DMA((ns, 3)), pltpu.SemaphoreType.DMA((ns, 3))],
        compiler_params=pltpu.CompilerParams(dimension_semantics=("arbitrary", "arbitrary"), vmem_limit_bytes=VMEM_LIMIT,
                                             has_side_effects=True), name=name)(a, b, *sums)
    return outs[0], _own_chip(outs[1:], sums)


def _adam_rows(r):
    for cand in (128, 64, 32, 16, 8):
        if r % cand == 0 and r > cand:
            return cand
    return r


def _adamw(parts, w, m, v, name):
    r, c = w.shape
    tr = _adam_rows(r)
    n_parts = parts.shape[0]

    def body(p_ref, w_ref, m_ref, v_ref, g_ref, d_ref, nm_ref, nv_ref):
        g = p_ref[0].astype(F32)
        for k in range(1, n_parts):
            g = g + p_ref[k].astype(F32)
        m_new = ADAM_B1 * m_ref[...] + (1.0 - ADAM_B1) * g
        v_new = ADAM_B2 * v_ref[...] + (1.0 - ADAM_B2) * (g * g)
        m_hat = m_new / (1.0 - ADAM_B1 ** ADAM_STEP)
        v_hat = v_new / (1.0 - ADAM_B2 ** ADAM_STEP)
        g_ref[...] = g
        d_ref[...] = -ADAM_LR * (m_hat / (jnp.sqrt(v_hat) + ADAM_EPS) + ADAM_WD * w_ref[...])
        nm_ref[...] = m_new
        nv_ref[...] = v_new

    blk = pl.BlockSpec((tr, c), lambda i: (i, 0))
    shp = jax.ShapeDtypeStruct((r, c), F32)
    return pl.pallas_call(
        body, grid=(r // tr,), in_specs=[pl.BlockSpec((n_parts, tr, c), lambda i: (0, i, 0)), blk, blk, blk],
        out_specs=[blk] * 4, out_shape=[shp] * 4, compiler_params=_cparams("parallel"), name=name)(parts, w, m, v)


def _rows128(a):
    return a.reshape(-1, 128)


def _pad_rows(a, rows):
    return jnp.pad(a, ((0, rows - a.shape[0]), (0, 0)))


def _lane16(a):
    return jnp.pad(a.reshape(1, -1), ((0, 0), (0, 128 - a.size)))


SHARD_PACK_ROWS = 544
REPL_PACK_ROWS = 72
LOSS_ROW = 71


def _pack_shard(meta, lru_conv_w, odd_norm, ssd_conv_w, lru_w_a, lru_w_x):
    parts = [meta.reshape(16, 128), lru_conv_w.reshape(4, 128), odd_norm.reshape(1, 128), _rows128(ssd_conv_w.reshape(4, 192)),
             _rows128(lru_w_a.reshape(4, 32, 256)), _rows128(lru_w_x.reshape(4, 32, 256))]
    return _pad_rows(jnp.concatenate(parts, axis=0), SHARD_PACK_ROWS)


def _unpack_shard(p):
    return (p[0:16], p[16:20].reshape(1, 4, 128), p[20:21], p[21:27].reshape(1, 4, 192),
            p[27:283].reshape(1, 4, 32, 256), p[283:539].reshape(1, 4, 32, 256))


def _pack_repl(even_norm, lru_conv_b, lru_b_a, lru_b_x, lru_lambda, ssd_norm, final_norm, ssd_conv_b, dt_bias, a_log, ssd_d):
    parts = [_rows128(v) for v in (even_norm, lru_conv_b, lru_b_a, lru_b_x, lru_lambda, ssd_norm, final_norm, ssd_conv_b)]
    parts += [_lane16(dt_bias), _lane16(a_log), _lane16(ssd_d)]
    return _pad_rows(jnp.concatenate(parts, axis=0), REPL_PACK_ROWS)


def _unpack_repl(p):
    vec = lambda i: p[8 * i:8 * i + 8].reshape(1, 1024)
    return (vec(0), vec(1), vec(2), vec(3), vec(4), vec(5), p[48:56].reshape(1024), p[56:68].reshape(1, 1536),
            p[68:69, :16], p[69:70, :16], p[70:71, :16])


def _local_step(x, tgt, meta, even_norm, w_in_e, lru_conv_w, lru_conv_b, lru_w_a, lru_b_a, lru_w_x, lru_b_x, lru_lambda,
                ssd_conv_w, ssd_conv_b, dt_bias, a_log, ssd_d, ssd_norm, odd_norm, final_norm, even_in, even_in_dx, even_in_dw, late=None):
    bsz, seq, d = x.shape
    lp = PAD + N_META + seq
    t = bsz * lp
    h0 = jnp.concatenate([jnp.zeros((bsz, PAD, d), F32), jnp.broadcast_to(meta[None], (bsz, N_META, d)), x], axis=1).reshape(t, d)

    u0 = _norm_fwd(h0, even_norm, "norm0_fwd")
    proj, w_out_e, w_in_o, w_out_o = even_in(u0)
    lru = (lru_conv_w, lru_conv_b, lru_w_a, lru_b_a, lru_w_x, lru_b_x, lru_lambda)
    ya = _lru_fwd(proj, *lru, bsz, lp)
    xbc = _ssd_pre_fwd(proj, ssd_conv_w, ssd_conv_b, bsz, lp)
    y, stacks = _ssd_fwd(xbc, proj, dt_bias, a_log, ssd_d, bsz, lp, late[0] if late else ())
    if late:
        w_in_o, w_out_o = late[1](stacks)
    mixed = _gnorm_fwd(y, proj, ssd_norm, ya)
    h1 = _mm_nn(mixed, w_out_e, F32, "even_out", res=h0)
    u2 = _norm_fwd(h1, odd_norm, "norm1_fwd")
    qkvg = _mm_nn(u2, w_in_o, F32, "odd_in")
    o, og = _sb_fwd(qkvg, bsz, lp)
    h2 = _mm_nn(og, w_out_o, F32, "odd_out", res=h1)
    loss, dh2, d_final_norm = _final_loss(h2, final_norm, tgt, lp)

    dog = _mm_nn(dh2, w_out_o.T, F32, "odd_out_dx")
    d_w_out_o = _mm_tn(og, dh2, "odd_out_dw")
    dqkvg = jnp.concatenate(_sb_bwd(qkvg, o, dog, bsz, lp), axis=1)
    du2 = _mm_nn(dqkvg, w_in_o.T, F32, "odd_in_dx")
    d_w_in_o = _mm_tn(u2, dqkvg, "odd_in_dw")
    dh1, d_odd_norm = _norm_bwd(h1, odd_norm, du2, dh2, "norm1_bwd")

    dmixed = _mm_nn(dh1, w_out_e.T, F32, "even_out_dx")
    d_w_out_e = _mm_tn(mixed, dh1, "even_out_dw")
    dlx, dgate, d_lru_conv_w, d_lru_conv_b, d_w_a, d_b_a, d_w_x, d_b_x, d_lambda = _lru_bwd(proj, dmixed, *lru, bsz, lp)
    dy, dz, d_ssd_norm = _gnorm_bwd(y, proj, ssd_norm, dmixed)
    dxs, dbm, dcm, ddt, d_dt_bias, d_a_log, d_ssd_d = _ssd_bwd(xbc, proj, dt_bias, a_log, ssd_d, dy, bsz, lp)
    dxbc, d_ssd_conv_w, d_ssd_conv_b = _ssd_pre_bwd(proj, dxs, dbm, dcm, ssd_conv_w, ssd_conv_b, bsz, lp)
    ddt = (ddt[:, :DT_L] + ddt[:, DT_L:]).astype(BF16)
    dproj = jnp.concatenate([dlx, dgate, dz, dxbc, ddt, jnp.zeros((t, EVEN_NP - 4608 - DT_L), BF16)], axis=1)
    du0, swapped = even_in_dx(dproj, w_in_e.T, (d_w_out_e, d_w_in_o, d_w_out_o))
    d_w_in_e, early = even_in_dw(u0, dproj, swapped)
    dh0, d_even_norm = _norm_bwd(h0, even_norm, du0, dh1, "norm0_bwd")
    grad_x = dh0.reshape(bsz, lp, d)[:, PAD + N_META:]
    d_meta = _meta_grad(dh0, bsz, lp)
    heads = lambda p: (p[0:1] + p[8:9])[:, :SSD_HEADS]
    grads = dict(meta=d_meta, even_norm=d_even_norm, even_w_in=d_w_in_e[:, :EVEN_IN], lru_conv_w=d_lru_conv_w,
                 lru_conv_b=d_lru_conv_b, lru_w_a=d_w_a, lru_b_a=d_b_a, lru_w_x=d_w_x, lru_b_x=d_b_x, lru_lambda=d_lambda,
                 ssd_conv_w=d_ssd_conv_w, ssd_conv_b=d_ssd_conv_b, ssd_dt_bias=heads(d_dt_bias), ssd_a_log=heads(d_a_log),
                 ssd_d=heads(d_ssd_d), ssd_norm=d_ssd_norm, even_w_out=d_w_out_e, odd_norm=d_odd_norm, odd_w_in=d_w_in_o,
                 odd_w_out=d_w_out_o, final_norm=d_final_norm)
    return loss[0, 0], grad_x, grads, early


WEIGHTS = ['meta', 'even_norm', 'even_w_in', 'lru_conv_w', 'lru_conv_b', 'lru_w_a', 'lru_b_a', 'lru_w_x', 'lru_b_x', 'lru_lambda',
           'ssd_conv_w', 'ssd_conv_b', 'ssd_dt_bias', 'ssd_a_log', 'ssd_d', 'ssd_norm', 'even_w_out', 'odd_norm', 'odd_w_in',
           'odd_w_out', 'final_norm']


def _blocks_of(a, axis):
    shp = a.shape
    a = a.reshape(shp[:axis] + (N_DEV, shp[axis] // N_DEV) + shp[axis + 1:])
    return jnp.moveaxis(a, axis, 0)


def _unblock(a, axis):
    a = jnp.moveaxis(a, 0, axis)
    shp = a.shape
    return a.reshape(shp[:axis] + (shp[axis] * shp[axis + 1],) + shp[axis + 2:])


def kernel(x, meta, even_norm, even_w_in, lru_conv_w, lru_conv_b, lru_w_a, lru_b_a, lru_w_x, lru_b_x, lru_lambda, ssd_conv_w, ssd_conv_b, ssd_dt_bias, ssd_a_log, ssd_d, ssd_norm, even_w_out, odd_norm, odd_w_in, odd_w_out, final_norm, loss_target, m_meta, m_even_norm, m_even_w_in, m_lru_conv_w, m_lru_conv_b, m_lru_w_a, m_lru_b_a, m_lru_w_x, m_lru_b_x, m_lru_lambda, m_ssd_conv_w, m_ssd_conv_b, m_ssd_dt_bias, m_ssd_a_log, m_ssd_d, m_ssd_norm, m_even_w_out, m_odd_norm, m_odd_w_in, m_odd_w_out, m_final_norm, v_meta, v_even_norm, v_even_w_in, v_lru_conv_w, v_lru_conv_b, v_lru_w_a, v_lru_b_a, v_lru_w_x, v_lru_b_x, v_lru_lambda, v_ssd_conv_w, v_ssd_conv_b, v_ssd_dt_bias, v_ssd_a_log, v_ssd_d, v_ssd_norm, v_even_w_out, v_odd_norm, v_odd_w_in, v_odd_w_out, v_final_norm):
    w = dict(meta=meta, even_norm=even_norm, even_w_in=even_w_in, lru_conv_w=lru_conv_w, lru_conv_b=lru_conv_b, lru_w_a=lru_w_a,
             lru_b_a=lru_b_a, lru_w_x=lru_w_x, lru_b_x=lru_b_x, lru_lambda=lru_lambda, ssd_conv_w=ssd_conv_w,
             ssd_conv_b=ssd_conv_b, ssd_dt_bias=ssd_dt_bias, ssd_a_log=ssd_a_log, ssd_d=ssd_d, ssd_norm=ssd_norm,
             even_w_out=even_w_out, odd_norm=odd_norm, odd_w_in=odd_w_in, odd_w_out=odd_w_out, final_norm=final_norm)
    m = dict(meta=m_meta, even_norm=m_even_norm, even_w_in=m_even_w_in, lru_conv_w=m_lru_conv_w, lru_conv_b=m_lru_conv_b,
             lru_w_a=m_lru_w_a, lru_b_a=m_lru_b_a, lru_w_x=m_lru_w_x, lru_b_x=m_lru_b_x, lru_lambda=m_lru_lambda,
             ssd_conv_w=m_ssd_conv_w, ssd_conv_b=m_ssd_conv_b, ssd_dt_bias=m_ssd_dt_bias, ssd_a_log=m_ssd_a_log, ssd_d=m_ssd_d,
             ssd_norm=m_ssd_norm, even_w_out=m_even_w_out, odd_norm=m_odd_norm, odd_w_in=m_odd_w_in, odd_w_out=m_odd_w_out,
             final_norm=m_final_norm)
    v = dict(meta=v_meta, even_norm=v_even_norm, even_w_in=v_even_w_in, lru_conv_w=v_lru_conv_w, lru_conv_b=v_lru_conv_b,
             lru_w_a=v_lru_w_a, lru_b_a=v_lru_b_a, lru_w_x=v_lru_w_x, lru_b_x=v_lru_b_x, lru_lambda=v_lru_lambda,
             ssd_conv_w=v_ssd_conv_w, ssd_conv_b=v_ssd_conv_b, ssd_dt_bias=v_ssd_dt_bias, ssd_a_log=v_ssd_a_log, ssd_d=v_ssd_d,
             ssd_norm=v_ssd_norm, even_w_out=v_even_w_out, odd_norm=v_odd_norm, odd_w_in=v_odd_w_in, odd_w_out=v_odd_w_out,
             final_norm=v_final_norm)
    shard_names = ('meta', 'lru_conv_w', 'odd_norm', 'ssd_conv_w', 'lru_w_a', 'lru_w_x')
    repl_names = ('even_norm', 'lru_conv_b', 'lru_b_a', 'lru_b_x', 'lru_lambda', 'ssd_norm', 'final_norm', 'ssd_conv_b',
                  'ssd_dt_bias', 'ssd_a_log', 'ssd_d')
    big_names = ('even_w_in', 'even_w_out', 'odd_w_in', 'odd_w_out')

    gates = jnp.concatenate([lru_w_a.reshape(128, 256), lru_w_x.reshape(128, 256)], axis=0).astype(BF16)
    small = _pack_shard(*[w[k] for k in shard_names])
    g_in_e, g_gates, g_small = _gather_all([even_w_in[0].astype(BF16), gates, small], "gather_weights")
    w_in_e = jnp.pad(_unblock(g_in_e, 1), ((0, 0), (0, EVEN_NP - EVEN_IN)))

    def even_in(u0):
        proj, (g_out_e,) = _mm_nn_gather(u0, w_in_e, F32, "even_in", [even_w_out[0].astype(BF16)])
        return proj, g_out_e.reshape(2048, 1024), None, None

    late = ([odd_w_in[0].astype(BF16), odd_w_out[0].astype(BF16)],
            lambda stacks: (_unblock(stacks[0], 1), stacks[1].reshape(1024, 1024)))

    gates_full = jnp.moveaxis(g_gates.reshape(N_DEV, 2, 4, 32, 256), 0, 2).reshape(2, 4, 256, 256)
    f_meta = _unblock(g_small[:, 0:16], 1)
    f_lru_conv_w = _unblock(g_small[:, 16:20], 1)
    f_odd_norm = _unblock(g_small[:, 20:21], 1)
    f_ssd_conv_w = _unblock(g_small[:, 21:27].reshape(N_DEV, 4, 192), 1)

    my_c = lax.axis_index("c")

    def chip_sums(by_owner, tag):
        got = _swap_in_chip(by_owner, "reduce_in_chip_" + tag)
        kept = [lax.dynamic_index_in_dim(a.reshape((4, 2) + a.shape[1:]), my_c, axis=1, keepdims=False) for a in by_owner]
        return [_add_pair(a, b, "chip_sum_%s%d" % (tag, i)) for i, (a, b) in enumerate(zip(kept, got))]

    def even_in_dx(dproj, w_t, done):
        d_out_e, d_in_o, d_out_o = done
        by_owner = [d_out_e.reshape(N_DEV, 256, 1024).astype(BF16), _blocks_of(d_in_o, 1).astype(BF16),
                    d_out_o.reshape(N_DEV, 128, 1024).astype(BF16)]
        du0, got = _mm_nn_swap(dproj, w_t, F32, "even_in_dx", by_owner)
        return du0, (by_owner, got)

    def even_in_dw(u0, dproj, swapped):
        by_owner, got = swapped
        kept = [lax.dynamic_index_in_dim(a.reshape((4, 2) + a.shape[1:]), my_c, axis=1, keepdims=False) for a in by_owner]
        sums = [_add_pair(a, b, "chip_sum_a%d" % i) for i, (a, b) in enumerate(zip(kept, got))]
        return _mm_tn_swap(u0, dproj, "even_in_dw", sums)

    loss, grad_x, g, (p_out_e, p_in_o, p_out_o) = _local_step(
        x, loss_target, f_meta, even_norm, w_in_e, f_lru_conv_w, lru_conv_b, gates_full[0], lru_b_a, gates_full[1], lru_b_x,
        lru_lambda, f_ssd_conv_w, ssd_conv_b, _lane16(ssd_dt_bias), _lane16(ssd_a_log), _lane16(ssd_d), ssd_norm,
        f_odd_norm, final_norm.reshape(1, -1), even_in, even_in_dx, even_in_dw, late)

    s_small = jnp.stack([_pack_shard(g['meta'][:, 128 * p:128 * (p + 1)], g['lru_conv_w'][:, 128 * p:128 * (p + 1)],
                                     g['odd_norm'][:, 128 * p:128 * (p + 1)], g['ssd_conv_w'][:, 192 * p:192 * (p + 1)],
                                     g['lru_w_a'][:, 32 * p:32 * (p + 1)], g['lru_w_x'][:, 32 * p:32 * (p + 1)])
                         for p in range(N_DEV)])
    r_pack = _pack_repl(*[g[k] for k in repl_names]).at[LOSS_ROW, 0].set(loss)
    p_in_e, p_small, p_repl = _swap_chips(chip_sums([_blocks_of(g['even_w_in'], 1).astype(BF16), s_small], "b"), [r_pack],
                                          "reduce_across_chips")

    res = {}
    for k, parts in zip(big_names, (p_in_e, p_out_e, p_in_o, p_out_o)):
        outs = _adamw(parts, w[k][0], m[k][0], v[k][0], "adamw_" + k)
        res[k] = [o[None] for o in outs]
    outs = _adamw(p_small, small, _pack_shard(*[m[k] for k in shard_names]), _pack_shard(*[v[k] for k in shard_names]), "adamw_sharded")
    unpacked = [_unpack_shard(o) for o in outs]
    for i, k in enumerate(shard_names):
        res[k] = [u[i] for u in unpacked]
    outs = _adamw(p_repl, _pack_repl(*[w[k] for k in repl_names]), _pack_repl(*[m[k] for k in repl_names]),
                  _pack_repl(*[v[k] for k in repl_names]), "adamw_replicated")
    unpacked = [_unpack_repl(o) for o in outs]
    for i, k in enumerate(repl_names):
        res[k] = [u[i] for u in unpacked]
    loss = outs[0][LOSS_ROW, 0]
    return (loss, grad_x, *[res[k][0] for k in WEIGHTS], *[res[k][1] for k in WEIGHTS], *[res[k][2] for k in WEIGHTS],
            *[res[k][3] for k in WEIGHTS])
```
